```python
import jax, jax.numpy as jnp
from jax import lax
import numpy as np

D_MODEL = 1024
BATCH = 8
SEQ = 4096
DEPTH = 4

CHUNK = 64
EPS = 1e-6
N_A = DEPTH // 2
N_B = DEPTH - N_A
A_CHUNK = 128
A_DFF = 2 * D_MODEL
A_GROUPS = 8
A_GROUP_DIM = A_DFF // A_GROUPS
B_HEADS = 16
B_HEAD_DIM = D_MODEL // B_HEADS
N_LEFT_CHUNKS = 8
BAND = (N_LEFT_CHUNKS + 1) * CHUNK
MAX_REL = 256
ATTN_SCALE = B_HEAD_DIM ** -0.5
NEG_INF = -1e30
FFN_HIDDEN = -(-8 * D_MODEL // (3 * 256)) * 256

kernel_name = "yoco_gmlp_chunked_relbias_attention_trunk"


def _rms_norm(x, g):
    xf = x.astype(jnp.float32)
    y = xf * lax.rsqrt(jnp.mean(xf * xf, axis=-1, keepdims=True) + EPS)
    return (y * g.astype(jnp.float32)).astype(x.dtype)


def _spatial_mask():
    pos = jnp.arange(A_CHUNK) // CHUNK
    return pos[:, None] >= pos[None, :]


def _gmlp_mixer(x, g_norm, w_in, g_sgu, w_s, b_s, w_out):
    b, s, _ = x.shape
    h = _rms_norm(x, g_norm)
    z = jax.nn.gelu(h @ w_in, approximate=False)
    u, v = jnp.split(z, 2, axis=-1)
    v = _rms_norm(v, g_sgu)
    v = v.reshape(b, s // A_CHUNK, A_CHUNK, A_GROUPS, A_GROUP_DIM)
    w = w_s * _spatial_mask().astype(w_s.dtype)[None]
    v = jnp.einsum('gij,bnjgc->bnigc', w, v) + b_s.T[None, None, :, :, None]
    v = v.reshape(b, s, A_DFF)
    return (u * v) @ w_out


def _swiglu_ffn(x, g_norm, w_gate_up, w_down):
    h = _rms_norm(x, g_norm)
    gate, up = jnp.split(h @ w_gate_up, 2, axis=-1)
    return (jax.nn.silu(gate) * up) @ w_down


def _shared_kv(x, g_norm, w_kv):
    b, s, _ = x.shape
    h = _rms_norm(x, g_norm)
    k, v = jnp.split(h @ w_kv, 2, axis=-1)
    k = k.reshape(b, s, B_HEADS, B_HEAD_DIM)
    v = v.reshape(b, s, B_HEADS, B_HEAD_DIM)
    pad = ((0, 0), (N_LEFT_CHUNKS * CHUNK, 0), (0, 0), (0, 0))
    return jnp.pad(k, pad), jnp.pad(v, pad)


def _rel_index():
    qi = jnp.arange(CHUNK)[:, None]
    kj = jnp.arange(BAND)[None, :] - N_LEFT_CHUNKS * CHUNK
    return jnp.clip(qi - kj, -MAX_REL, MAX_REL) + MAX_REL


def _chunked_relbias_attention(x, g_norm, w_q, rel_table, w_o, k_pad, v_pad):
    b, s, _ = x.shape
    n_chunks = s // CHUNK
    q = (_rms_norm(x, g_norm) @ w_q).reshape(b, n_chunks, CHUNK, B_HEADS, B_HEAD_DIM)
    q = jnp.moveaxis(q, 1, 0)
    bias = rel_table[:, _rel_index()].astype(jnp.float32)
    key_offset = jnp.arange(BAND) - N_LEFT_CHUNKS * CHUNK

    def attend(args):
        c, q_c = args
        k_c = lax.dynamic_slice_in_dim(k_pad, c * CHUNK, BAND, axis=1)
        v_c = lax.dynamic_slice_in_dim(v_pad, c * CHUNK, BAND, axis=1)
        sc = jnp.einsum('bqhd,bkhd->bhqk', q_c, k_c).astype(jnp.float32) * ATTN_SCALE + bias
        valid = (c * CHUNK + key_offset) >= 0
        sc = jnp.where(valid, sc, NEG_INF)
        p = jax.nn.softmax(sc, axis=-1).astype(v_c.dtype)
        return jnp.einsum('bhqk,bkhd->bqhd', p, v_c)

    o = lax.map(attend, (jnp.arange(n_chunks), q))
    o = jnp.moveaxis(o, 0, 1).reshape(b, s, D_MODEL)
    return o @ w_o


def _normal(key, shape, scale):
    return jax.random.normal(key, shape, jnp.float32) * scale


def _fwd_setup_inputs(seed: int = 0) -> dict:
    key = jax.random.key(seed)
    ks = jax.random.split(key, 18)
    return {
        "x": _normal(ks[0], (BATCH, SEQ, D_MODEL), 1.0),
        "a_norm": 1.0 + _normal(ks[1], (N_A, D_MODEL), 0.02),
        "a_w_in": _normal(ks[2], (N_A, D_MODEL, 2 * A_DFF), D_MODEL ** -0.5),
        "a_sgu_norm": 1.0 + _normal(ks[3], (N_A, A_DFF), 0.02),
        "a_w_spatial": _normal(ks[4], (N_A, A_GROUPS, A_CHUNK, A_CHUNK), A_CHUNK ** -0.5),
        "a_b_spatial": 1.0 + _normal(ks[5], (N_A, A_GROUPS, A_CHUNK), 0.1),
        "a_w_out": _normal(ks[6], (N_A, A_DFF, D_MODEL), A_DFF ** -0.5),
        "kv_norm": 1.0 + _normal(ks[7], (D_MODEL,), 0.02),
        "w_kv": _normal(ks[8], (D_MODEL, 2 * D_MODEL), D_MODEL ** -0.5),
        "b_norm": 1.0 + _normal(ks[9], (N_B, D_MODEL), 0.02),
        "b_w_q": _normal(ks[10], (N_B, D_MODEL, D_MODEL), D_MODEL ** -0.5),
        "b_rel_bias": _normal(ks[11], (N_B, B_HEADS, 2 * MAX_REL + 1), 0.1),
        "b_w_o": _normal(ks[12], (N_B, D_MODEL, D_MODEL), D_MODEL ** -0.5),
        "ffn_norm": 1.0 + _normal(ks[13], (DEPTH, D_MODEL), 0.02),
        "ffn_w_gate_up": _normal(ks[14], (DEPTH, D_MODEL, 2 * FFN_HIDDEN), D_MODEL ** -0.5),
        "ffn_w_down": _normal(ks[15], (DEPTH, FFN_HIDDEN, D_MODEL), FFN_HIDDEN ** -0.5),
        "final_norm": 1.0 + _normal(ks[16], (D_MODEL,), 0.02),
    }


def _fwd_reference(x, a_norm, a_w_in, a_sgu_norm, a_w_spatial, a_b_spatial, a_w_out,
              kv_norm, w_kv, b_norm, b_w_q, b_rel_bias, b_w_o,
              ffn_norm, ffn_w_gate_up, ffn_w_down, final_norm):
    k_pad = v_pad = None
    for layer in range(DEPTH):
        if layer < N_A:
            i = layer
            x = x + _gmlp_mixer(x, a_norm[i], a_w_in[i], a_sgu_norm[i],
                                a_w_spatial[i], a_b_spatial[i], a_w_out[i])
        else:
            if layer == N_A:
                k_pad, v_pad = _shared_kv(x, kv_norm, w_kv)
            i = layer - N_A
            x = x + _chunked_relbias_attention(x, b_norm[i], b_w_q[i], b_rel_bias[i],
                                               b_w_o[i], k_pad, v_pad)
        x = x + _swiglu_ffn(x, ffn_norm[layer], ffn_w_gate_up[layer], ffn_w_down[layer])
    return _rms_norm(x, final_norm)


import jax as _jax
import jax.numpy as _jnp

TWIN_FORMAT = 'train_step'
FWD_PARAMS = ['x', 'a_norm', 'a_w_in', 'a_sgu_norm', 'a_w_spatial', 'a_b_spatial', 'a_w_out', 'kv_norm', 'w_kv', 'b_norm', 'b_w_q', 'b_rel_bias', 'b_w_o', 'ffn_norm', 'ffn_w_gate_up', 'ffn_w_down', 'final_norm']
TWIN_WEIGHTS = ['a_norm', 'a_w_in', 'a_sgu_norm', 'a_w_spatial', 'a_b_spatial', 'a_w_out', 'kv_norm', 'w_kv', 'b_norm', 'b_w_q', 'b_rel_bias', 'b_w_o', 'ffn_norm', 'ffn_w_gate_up', 'ffn_w_down', 'final_norm']
TWIN_DIFF_INPUT = 'x'
TWIN_INPUTS = ['x', 'a_norm', 'a_w_in', 'a_sgu_norm', 'a_w_spatial', 'a_b_spatial', 'a_w_out', 'kv_norm', 'w_kv', 'b_norm', 'b_w_q', 'b_rel_bias', 'b_w_o', 'ffn_norm', 'ffn_w_gate_up', 'ffn_w_down', 'final_norm', 'loss_target', 'm_a_norm', 'm_a_w_in', 'm_a_sgu_norm', 'm_a_w_spatial', 'm_a_b_spatial', 'm_a_w_out', 'm_kv_norm', 'm_w_kv', 'm_b_norm', 'm_b_w_q', 'm_b_rel_bias', 'm_b_w_o', 'm_ffn_norm', 'm_ffn_w_gate_up', 'm_ffn_w_down', 'm_final_norm', 'v_a_norm', 'v_a_w_in', 'v_a_sgu_norm', 'v_a_w_spatial', 'v_a_b_spatial', 'v_a_w_out', 'v_kv_norm', 'v_w_kv', 'v_b_norm', 'v_b_w_q', 'v_b_rel_bias', 'v_b_w_o', 'v_ffn_norm', 'v_ffn_w_gate_up', 'v_ffn_w_down', 'v_final_norm']
TWIN_OUTPUTS = ['loss', 'grad_x', 'grad_a_norm', 'grad_a_w_in', 'grad_a_sgu_norm', 'grad_a_w_spatial', 'grad_a_b_spatial', 'grad_a_w_out', 'grad_kv_norm', 'grad_w_kv', 'grad_b_norm', 'grad_b_w_q', 'grad_b_rel_bias', 'grad_b_w_o', 'grad_ffn_norm', 'grad_ffn_w_gate_up', 'grad_ffn_w_down', 'grad_final_norm', 'delta_a_norm', 'delta_a_w_in', 'delta_a_sgu_norm', 'delta_a_w_spatial', 'delta_a_b_spatial', 'delta_a_w_out', 'delta_kv_norm', 'delta_w_kv', 'delta_b_norm', 'delta_b_w_q', 'delta_b_rel_bias', 'delta_b_w_o', 'delta_ffn_norm', 'delta_ffn_w_gate_up', 'delta_ffn_w_down', 'delta_final_norm', 'new_m_a_norm', 'new_m_a_w_in', 'new_m_a_sgu_norm', 'new_m_a_w_spatial', 'new_m_a_b_spatial', 'new_m_a_w_out', 'new_m_kv_norm', 'new_m_w_kv', 'new_m_b_norm', 'new_m_b_w_q', 'new_m_b_rel_bias', 'new_m_b_w_o', 'new_m_ffn_norm', 'new_m_ffn_w_gate_up', 'new_m_ffn_w_down', 'new_m_final_norm', 'new_v_a_norm', 'new_v_a_w_in', 'new_v_a_sgu_norm', 'new_v_a_w_spatial', 'new_v_a_b_spatial', 'new_v_a_w_out', 'new_v_kv_norm', 'new_v_w_kv', 'new_v_b_norm', 'new_v_b_w_q', 'new_v_b_rel_bias', 'new_v_b_w_o', 'new_v_ffn_norm', 'new_v_ffn_w_gate_up', 'new_v_ffn_w_down', 'new_v_final_norm']
TWIN_LEAF_KINDS = {'loss': 'loss', 'grad_x': 'grad_x', 'grad_a_norm': 'grad_w', 'grad_a_w_in': 'grad_w', 'grad_a_sgu_norm': 'grad_w', 'grad_a_w_spatial': 'grad_w', 'grad_a_b_spatial': 'grad_w', 'grad_a_w_out': 'grad_w', 'grad_kv_norm': 'grad_w', 'grad_w_kv': 'grad_w', 'grad_b_norm': 'grad_w', 'grad_b_w_q': 'grad_w', 'grad_b_rel_bias': 'grad_w', 'grad_b_w_o': 'grad_w', 'grad_ffn_norm': 'grad_w', 'grad_ffn_w_gate_up': 'grad_w', 'grad_ffn_w_down': 'grad_w', 'grad_final_norm': 'grad_w', 'delta_a_norm': 'delta_w', 'delta_a_w_in': 'delta_w', 'delta_a_sgu_norm': 'delta_w', 'delta_a_w_spatial': 'delta_w', 'delta_a_b_spatial': 'delta_w', 'delta_a_w_out': 'delta_w', 'delta_kv_norm': 'delta_w', 'delta_w_kv': 'delta_w', 'delta_b_norm': 'delta_w', 'delta_b_w_q': 'delta_w', 'delta_b_rel_bias': 'delta_w', 'delta_b_w_o': 'delta_w', 'delta_ffn_norm': 'delta_w', 'delta_ffn_w_gate_up': 'delta_w', 'delta_ffn_w_down': 'delta_w', 'delta_final_norm': 'delta_w', 'new_m_a_norm': 'new_m', 'new_m_a_w_in': 'new_m', 'new_m_a_sgu_norm': 'new_m', 'new_m_a_w_spatial': 'new_m', 'new_m_a_b_spatial': 'new_m', 'new_m_a_w_out': 'new_m', 'new_m_kv_norm': 'new_m', 'new_m_w_kv': 'new_m', 'new_m_b_norm': 'new_m', 'new_m_b_w_q': 'new_m', 'new_m_b_rel_bias': 'new_m', 'new_m_b_w_o': 'new_m', 'new_m_ffn_norm': 'new_m', 'new_m_ffn_w_gate_up': 'new_m', 'new_m_ffn_w_down': 'new_m', 'new_m_final_norm': 'new_m', 'new_v_a_norm': 'new_v', 'new_v_a_w_in': 'new_v', 'new_v_a_sgu_norm': 'new_v', 'new_v_a_w_spatial': 'new_v', 'new_v_a_b_spatial': 'new_v', 'new_v_a_w_out': 'new_v', 'new_v_kv_norm': 'new_v', 'new_v_w_kv': 'new_v', 'new_v_b_norm': 'new_v', 'new_v_b_w_q': 'new_v', 'new_v_b_rel_bias': 'new_v', 'new_v_b_w_o': 'new_v', 'new_v_ffn_norm': 'new_v', 'new_v_ffn_w_gate_up': 'new_v', 'new_v_ffn_w_down': 'new_v', 'new_v_final_norm': 'new_v'}


def _forward(args):
    return _fwd_reference(*[args[k] for k in FWD_PARAMS])


def _output_shape():
    def fwd():
        inp = _fwd_setup_inputs(0)
        return _fwd_reference(*[inp[k] for k in FWD_PARAMS])
    out = _jax.eval_shape(fwd)
    return out.shape, out.dtype

N_MICROBATCH = 1
ADAM_LR = 0.001
ADAM_B1 = 0.9
ADAM_B2 = 0.999
ADAM_EPS = 1e-08
ADAM_WD = 0.01
ADAM_STEP = 10
PER_EXAMPLE_BATCH_AXIS = {'x': 0, 'loss_target': 0}
SHARED_INPUTS = []
_WEIGHT_DTYPES = {'a_norm': _jnp.float32, 'a_w_in': _jnp.float32, 'a_sgu_norm': _jnp.float32, 'a_w_spatial': _jnp.float32, 'a_b_spatial': _jnp.float32, 'a_w_out': _jnp.float32, 'kv_norm': _jnp.float32, 'w_kv': _jnp.float32, 'b_norm': _jnp.float32, 'b_w_q': _jnp.float32, 'b_rel_bias': _jnp.float32, 'b_w_o': _jnp.float32, 'ffn_norm': _jnp.float32, 'ffn_w_gate_up': _jnp.float32, 'ffn_w_down': _jnp.float32, 'final_norm': _jnp.float32}
MOMENT_SCALE = {'a_norm': 1.702455e-01, 'a_w_in': 8.149066e-02, 'a_sgu_norm': 5.950526e-02, 'a_w_spatial': 8.537104e-02, 'a_b_spatial': 9.954682e-02, 'a_w_out': 1.414443e-01, 'kv_norm': 4.680555e-02, 'w_kv': 3.322576e-02, 'b_norm': 1.331163e-02, 'b_w_q': 1.343565e-02, 'b_rel_bias': 4.723575e-03, 'b_w_o': 3.445440e-02, 'ffn_norm': 9.906822e-02, 'ffn_w_gate_up': 4.161678e-02, 'ffn_w_down': 6.819876e-02, 'final_norm': 3.218795e+01}


def _to_microbatches(a, axis):
    t = _jnp.moveaxis(a, axis, 0)
    t = t.reshape((N_MICROBATCH, t.shape[0] // N_MICROBATCH) + t.shape[1:])
    return _jnp.moveaxis(t, 1, axis + 1)


def setup_inputs(seed: int = 0) -> dict:
    inp = _fwd_setup_inputs(seed)
    key = _jax.random.fold_in(_jax.random.key(seed), 7919)
    shape, _ = _output_shape()
    out = dict(inp)
    out["loss_target"] = _jax.random.normal(_jax.random.fold_in(key, 0), shape, _jnp.float32)
    for i, name in enumerate(TWIN_WEIGHTS):
        w = inp[name].astype(_jnp.float32)
        if MOMENT_SCALE is None:
            s = _jnp.sqrt(_jnp.mean(_jnp.square(w)) + 1e-30)
        else:
            s = MOMENT_SCALE[name]
        km, kv = _jax.random.split(_jax.random.fold_in(key, i + 1))
        out[name] = w
        out["m_" + name] = s * _jax.random.normal(km, w.shape, _jnp.float32)
        out["v_" + name] = (s * s) * _jax.random.uniform(kv, w.shape, _jnp.float32, 0.5, 1.5)
    if N_MICROBATCH > 1:
        for name, axis in PER_EXAMPLE_BATCH_AXIS.items():
            out[name] = _to_microbatches(out[name], axis)
    return {'x': out['x'], 'a_norm': out['a_norm'], 'a_w_in': out['a_w_in'], 'a_sgu_norm': out['a_sgu_norm'], 'a_w_spatial': out['a_w_spatial'], 'a_b_spatial': out['a_b_spatial'], 'a_w_out': out['a_w_out'], 'kv_norm': out['kv_norm'], 'w_kv': out['w_kv'], 'b_norm': out['b_norm'], 'b_w_q': out['b_w_q'], 'b_rel_bias': out['b_rel_bias'], 'b_w_o': out['b_w_o'], 'ffn_norm': out['ffn_norm'], 'ffn_w_gate_up': out['ffn_w_gate_up'], 'ffn_w_down': out['ffn_w_down'], 'final_norm': out['final_norm'], 'loss_target': out['loss_target'], 'm_a_norm': out['m_a_norm'], 'm_a_w_in': out['m_a_w_in'], 'm_a_sgu_norm': out['m_a_sgu_norm'], 'm_a_w_spatial': out['m_a_w_spatial'], 'm_a_b_spatial': out['m_a_b_spatial'], 'm_a_w_out': out['m_a_w_out'], 'm_kv_norm': out['m_kv_norm'], 'm_w_kv': out['m_w_kv'], 'm_b_norm': out['m_b_norm'], 'm_b_w_q': out['m_b_w_q'], 'm_b_rel_bias': out['m_b_rel_bias'], 'm_b_w_o': out['m_b_w_o'], 'm_ffn_norm': out['m_ffn_norm'], 'm_ffn_w_gate_up': out['m_ffn_w_gate_up'], 'm_ffn_w_down': out['m_ffn_w_down'], 'm_final_norm': out['m_final_norm'], 'v_a_norm': out['v_a_norm'], 'v_a_w_in': out['v_a_w_in'], 'v_a_sgu_norm': out['v_a_sgu_norm'], 'v_a_w_spatial': out['v_a_w_spatial'], 'v_a_b_spatial': out['v_a_b_spatial'], 'v_a_w_out': out['v_a_w_out'], 'v_kv_norm': out['v_kv_norm'], 'v_w_kv': out['v_w_kv'], 'v_b_norm': out['v_b_norm'], 'v_b_w_q': out['v_b_w_q'], 'v_b_rel_bias': out['v_b_rel_bias'], 'v_b_w_o': out['v_b_w_o'], 'v_ffn_norm': out['v_ffn_norm'], 'v_ffn_w_gate_up': out['v_ffn_w_gate_up'], 'v_ffn_w_down': out['v_ffn_w_down'], 'v_final_norm': out['v_final_norm']}


def _loss(weights, diff, rest, loss_target):
    with _jax.named_scope("forward"):
        args = {**rest, TWIN_DIFF_INPUT: diff, **{k: w.astype(_WEIGHT_DTYPES[k]) for k, w in weights.items()}}
        y = _forward(args)
    with _jax.named_scope("loss_head"):
        err = _jnp.square(y.astype(_jnp.float32) - loss_target)
        return 0.5 * _jnp.sum(_jnp.mean(err, axis=-1)) if err.ndim else 0.5 * err


def _adamw(w, g, m, v):
    m = ADAM_B1 * m + (1.0 - ADAM_B1) * g
    v = ADAM_B2 * v + (1.0 - ADAM_B2) * _jnp.square(g)
    m_hat = m / (1.0 - ADAM_B1 ** ADAM_STEP)
    v_hat = v / (1.0 - ADAM_B2 ** ADAM_STEP)
    delta = -ADAM_LR * (m_hat / (_jnp.sqrt(v_hat) + ADAM_EPS) + ADAM_WD * w)
    return delta, m, v


def reference(x, a_norm, a_w_in, a_sgu_norm, a_w_spatial, a_b_spatial, a_w_out, kv_norm, w_kv, b_norm, b_w_q, b_rel_bias, b_w_o, ffn_norm, ffn_w_gate_up, ffn_w_down, final_norm, loss_target, m_a_norm, m_a_w_in, m_a_sgu_norm, m_a_w_spatial, m_a_b_spatial, m_a_w_out, m_kv_norm, m_w_kv, m_b_norm, m_b_w_q, m_b_rel_bias, m_b_w_o, m_ffn_norm, m_ffn_w_gate_up, m_ffn_w_down, m_final_norm, v_a_norm, v_a_w_in, v_a_sgu_norm, v_a_w_spatial, v_a_b_spatial, v_a_w_out, v_kv_norm, v_w_kv, v_b_norm, v_b_w_q, v_b_rel_bias, v_b_w_o, v_ffn_norm, v_ffn_w_gate_up, v_ffn_w_down, v_final_norm):
    given = dict(x=x, a_norm=a_norm, a_w_in=a_w_in, a_sgu_norm=a_sgu_norm, a_w_spatial=a_w_spatial, a_b_spatial=a_b_spatial, a_w_out=a_w_out, kv_norm=kv_norm, w_kv=w_kv, b_norm=b_norm, b_w_q=b_w_q, b_rel_bias=b_rel_bias, b_w_o=b_w_o, ffn_norm=ffn_norm, ffn_w_gate_up=ffn_w_gate_up, ffn_w_down=ffn_w_down, final_norm=final_norm, loss_target=loss_target, m_a_norm=m_a_norm, m_a_w_in=m_a_w_in, m_a_sgu_norm=m_a_sgu_norm, m_a_w_spatial=m_a_w_spatial, m_a_b_spatial=m_a_b_spatial, m_a_w_out=m_a_w_out, m_kv_norm=m_kv_norm, m_w_kv=m_w_kv, m_b_norm=m_b_norm, m_b_w_q=m_b_w_q, m_b_rel_bias=m_b_rel_bias, m_b_w_o=m_b_w_o, m_ffn_norm=m_ffn_norm, m_ffn_w_gate_up=m_ffn_w_gate_up, m_ffn_w_down=m_ffn_w_down, m_final_norm=m_final_norm, v_a_norm=v_a_norm, v_a_w_in=v_a_w_in, v_a_sgu_norm=v_a_sgu_norm, v_a_w_spatial=v_a_w_spatial, v_a_b_spatial=v_a_b_spatial, v_a_w_out=v_a_w_out, v_kv_norm=v_kv_norm, v_w_kv=v_w_kv, v_b_norm=v_b_norm, v_b_w_q=v_b_w_q, v_b_rel_bias=v_b_rel_bias, v_b_w_o=v_b_w_o, v_ffn_norm=v_ffn_norm, v_ffn_w_gate_up=v_ffn_w_gate_up, v_ffn_w_down=v_ffn_w_down, v_final_norm=v_final_norm)
    weights = {n: given[n] for n in TWIN_WEIGHTS}
    shared = {n: given[n] for n in SHARED_INPUTS}
    per_example = {n: given[n] for n in ['x']}
    grad_fn = _jax.value_and_grad(_loss, argnums=(0, 1))

    def one_microbatch(ex, loss_target):
        ex = dict(ex)
        diff = ex.pop(TWIN_DIFF_INPUT)
        return grad_fn(weights, diff, {**shared, **ex}, loss_target)

    if N_MICROBATCH == 1:
        loss, (grad_w, grad_x) = one_microbatch(per_example, given["loss_target"])
    else:
        def body(carry, xs):
            loss_sum, grad_sum = carry
            l_k, (gw_k, gx_k) = one_microbatch(xs[0], xs[1])
            with _jax.named_scope("update"):
                return (loss_sum + l_k, _jax.tree.map(_jnp.add, grad_sum, gw_k)), gx_k

        init = (_jnp.zeros((), _jnp.float32), _jax.tree.map(_jnp.zeros_like, weights))
        (loss, grad_w), grad_x = _jax.lax.scan(body, init, (per_example, given["loss_target"]))
    with _jax.named_scope("update"):
        delta_w, new_m, new_v = {}, {}, {}
        for n in TWIN_WEIGHTS:
            delta_w[n], new_m[n], new_v[n] = _adamw(weights[n], grad_w[n], given["m_" + n], given["v_" + n])
    return (loss, grad_x, *[grad_w[n] for n in TWIN_WEIGHTS], *[delta_w[n] for n in TWIN_WEIGHTS],
            *[new_m[n] for n in TWIN_WEIGHTS], *[new_v[n] for n in TWIN_WEIGHTS])
```

```python
import math

import jax
import jax.numpy as jnp
from jax import lax
from jax.experimental import pallas as pl
from jax.experimental.pallas import tpu as pltpu

F32, BF16 = jnp.float32, jnp.bfloat16
MESH = pl.DeviceIdType.MESH
HIGHEST = lax.Precision.HIGHEST
NT_DIMS = (((1,), (1,)), ((), ()))
TN_DIMS = (((0,), (0,)), ((), ()))

EPS = 1e-6
CHUNK = 64
A_CHUNK = 128
A_GROUPS = 8
N_HEADS = 16
HEAD_DIM = 64
N_LEFT = 8
MAX_REL = 256
ATTN_SCALE = HEAD_DIM ** -0.5
NEG_INF = -1e30
BAND = (N_LEFT + 2) * CHUNK
KV_PAD = BAND - CHUNK
TABLE_PAD = 640
Q_BLOCK = 4 * CHUNK
HEADS_PER_BLOCK = 2

ADAM_LR, ADAM_B1, ADAM_B2, ADAM_EPS, ADAM_WD, ADAM_STEP = 0.001, 0.9, 0.999, 1e-08, 0.01, 10

VMEM_LIMIT_BYTES = 56 * 1024 * 1024
N_CHIPS = 4
N_DEV = 8
SMALL_COLS = 1024


def _cparams(n_grid):
    return pltpu.CompilerParams(dimension_semantics=("arbitrary",) * n_grid, vmem_limit_bytes=VMEM_LIMIT_BYTES)


def _sds(shape, dtype):
    return jax.ShapeDtypeStruct(tuple(shape), dtype)


def _gelu(x):
    return 0.5 * x * (1.0 + lax.erf(x * math.sqrt(0.5)))


def _gelu_grad(x):
    return 0.5 * (1.0 + lax.erf(x * math.sqrt(0.5))) + x * (jnp.exp(-0.5 * x * x) * (1.0 / math.sqrt(2.0 * math.pi)))


def _rms_hat(xv):
    r = lax.rsqrt(jnp.mean(xv * xv, axis=-1, keepdims=True) + EPS)
    return xv * r, r


def _rms_bwd(xhat, r, g, dy):
    dxhat = dy * g
    dx = r * (dxhat - xhat * jnp.mean(dxhat * xhat, axis=-1, keepdims=True))
    return dx, dy * xhat


def _swiglu(gate, up):
    return (gate * jax.nn.sigmoid(gate)) * up


def _row_tile(rows, cols, itemsize, cap_bytes, align):
    t = rows
    while t * cols * itemsize > cap_bytes and t % (2 * align) == 0:
        t //= 2
    return t


def _cast_bf16(w, name):
    L, r, C = w.shape
    tr = _row_tile(r, C, 4, 4 * 1024 * 1024, 16)

    def body(w_ref, o_ref):
        o_ref[...] = w_ref[...].astype(BF16)

    spec = pl.BlockSpec((None, tr, C), lambda l, i: (l, i, 0))
    return pl.pallas_call(body, name=name, grid=(L, r // tr), in_specs=[spec], out_specs=spec,
                          out_shape=_sds(w.shape, BF16), compiler_params=_cparams(2))(w)


def _adamw(w, g, m, v, name):
    R, C = w.shape
    tr = _row_tile(R, C, 4, 1024 * 1024, 8)

    def body(w_ref, g_ref, m_ref, v_ref, d_ref, nm_ref, nv_ref):
        gv = g_ref[...]
        mn = ADAM_B1 * m_ref[...] + (1.0 - ADAM_B1) * gv
        vn = ADAM_B2 * v_ref[...] + (1.0 - ADAM_B2) * jnp.square(gv)
        m_hat = mn / (1.0 - ADAM_B1 ** ADAM_STEP)
        v_hat = vn / (1.0 - ADAM_B2 ** ADAM_STEP)
        d_ref[...] = -ADAM_LR * (m_hat / (jnp.sqrt(v_hat) + ADAM_EPS) + ADAM_WD * w_ref[...])
        nm_ref[...] = mn
        nv_ref[...] = vn

    spec = pl.BlockSpec((tr, C), lambda i: (i, 0))
    return pl.pallas_call(body, name=name, grid=(R // tr,), in_specs=[spec] * 4, out_specs=[spec] * 3,
                          out_shape=[_sds((R, C), F32)] * 3, compiler_params=_cparams(1))(w, g, m, v)


def _norm_matmul(x, g, w_g, layer, out_dtype, name, row_sharded=False, tm=1024):
    S, D = x.shape
    tm = min(tm, S)
    if row_sharded:
        r, N = w_g.shape[2], w_g.shape[3]
        tn = 512
        w_spec = pl.BlockSpec((N_CHIPS, None, r, tn), lambda i, j: (0, layer, 0, j))
    else:
        nsh = w_g.shape[3]
        N = N_CHIPS * nsh
        tn = 512 if nsh % 512 == 0 else nsh
        bps = nsh // tn
        w_spec = pl.BlockSpec((None, None, D, tn), lambda i, j: (j // bps, layer, 0, j % bps))

    def body(x_ref, g_ref, w_ref, y_ref, h_ref):
        @pl.when(pl.program_id(1) == 0)
        def _():
            xhat, _ = _rms_hat(x_ref[...])
            h_ref[...] = (xhat * g_ref[...]).astype(BF16)

        w = w_ref[...].reshape(D, tn)
        y_ref[...] = jnp.dot(h_ref[...], w, preferred_element_type=F32).astype(y_ref.dtype)

    return pl.pallas_call(
        body, name=name, grid=(S // tm, N // tn),
        in_specs=[pl.BlockSpec((tm, D), lambda i, j: (i, 0)), pl.BlockSpec((1, D), lambda i, j: (0, 0)), w_spec],
        out_specs=[pl.BlockSpec((tm, tn), lambda i, j: (i, j)), pl.BlockSpec((tm, D), lambda i, j: (i, 0))],
        out_shape=[_sds((S, N), out_dtype), _sds((S, D), BF16)],
        compiler_params=_cparams(2))(x, g, w_g)


def _matmul_res(a, w_g, layer, res, name, swiglu=False, tm=256):
    S, N = res.shape
    r = w_g.shape[2]
    K = N_CHIPS * r

    def body(*refs):
        if swiglu:
            gate_ref, up_ref, w_ref, res_ref, o_ref = refs
            a_blk = _swiglu(gate_ref[...], up_ref[...]).astype(BF16)
        else:
            a_ref, w_ref, res_ref, o_ref = refs
            a_blk = a_ref[...]
        o_ref[...] = res_ref[...] + jnp.dot(a_blk, w_ref[...].reshape(K, N), preferred_element_type=F32)

    a_specs, a_ops = [pl.BlockSpec((tm, K), lambda i: (i, 0))], [a]
    if swiglu:
        a_specs.append(pl.BlockSpec((tm, K), lambda i: (i, 1)))
        a_ops.append(a)
    row = pl.BlockSpec((tm, N), lambda i: (i, 0))
    return pl.pallas_call(
        body, name=name, grid=(S // tm,),
        in_specs=a_specs + [pl.BlockSpec((N_CHIPS, None, r, N), lambda i: (0, layer, 0, 0)), row],
        out_specs=row, out_shape=_sds((S, N), F32), compiler_params=_cparams(1))(*a_ops, w_g, res)


def _matmul_tn(a_ops, a_specs, a_fn, b_op, b_spec, out_spec, out_shape, acc_shape, n_outer, name, gbuf=None, tt=512):
    S = b_op.shape[-2]
    na = len(a_ops)

    def body(*refs):
        a_refs, b_ref, o_ref = refs[:na], refs[na], refs[-1]

        @pl.when(pl.program_id(1) == 0)
        def _():
            o_ref[...] = jnp.zeros_like(o_ref)

        part = lax.dot_general(a_fn(*a_refs), b_ref[...].astype(BF16), TN_DIMS, preferred_element_type=F32)
        o_ref[...] += part.reshape(o_ref.shape)

    del acc_shape
    in_specs = list(a_specs) + [b_spec]
    ops = list(a_ops) + [b_op]
    aliases = {}
    if gbuf is not None:
        in_specs.append(pl.BlockSpec(memory_space=pl.ANY))
        ops.append(gbuf)
        aliases = {len(ops) - 1: 0}
    return pl.pallas_call(
        body, name=name, grid=(n_outer, S // tt), in_specs=in_specs, out_specs=out_spec,
        out_shape=_sds(out_shape, F32), input_output_aliases=aliases, compiler_params=_cparams(2))(*ops)


def _nt_accumulate(a_ref, w_ref, acc_ref, w2d, nk):
    k = pl.program_id(1)
    part = lax.dot_general(a_ref[...].astype(BF16), w_ref[...].reshape(w2d), NT_DIMS, preferred_element_type=F32)

    @pl.when(k == 0)
    def _():
        acc_ref[...] = part

    @pl.when(k > 0)
    def _():
        acc_ref[...] += part

    return k == nk - 1


def _nt_normbwd(dy, dy_spec, w_g, w_spec, w2d, nk, x, g, dres, name, tm=512):
    S, D = x.shape

    def body(dy_ref, w_ref, x_ref, g_ref, dres_ref, dx_ref, dg_ref, acc_ref):
        @pl.when((pl.program_id(0) == 0) & (pl.program_id(1) == 0))
        def _():
            dg_ref[...] = jnp.zeros_like(dg_ref)

        last = _nt_accumulate(dy_ref, w_ref, acc_ref, w2d, nk)

        @pl.when(last)
        def _():
            xhat, r = _rms_hat(x_ref[...])
            dx, dgp = _rms_bwd(xhat, r, g_ref[...], acc_ref[...])
            dx_ref[...] = dres_ref[...] + dx
            dg_ref[...] += jnp.sum(dgp, axis=0, keepdims=True)

    row = pl.BlockSpec((tm, D), lambda i, k: (i, 0))
    vec = pl.BlockSpec((1, D), lambda i, k: (0, 0))
    return pl.pallas_call(
        body, name=name, grid=(S // tm, nk),
        in_specs=[dy_spec, w_spec, row, vec, row], out_specs=[row, vec],
        out_shape=[_sds((S, D), F32), _sds((1, D), F32)],
        scratch_shapes=[pltpu.VMEM((tm, D), F32)], compiler_params=_cparams(2))(dy, w_g, x, g, dres)


def _nt_rows(dy, w_g, layer, shards_per_block, out_dtype, name, tm=512):
    S, N = dy.shape
    r = w_g.shape[2]
    tn = shards_per_block * r

    def body(dy_ref, w_ref, o_ref):
        o_ref[...] = lax.dot_general(dy_ref[...].astype(BF16), w_ref[...].reshape(tn, N), NT_DIMS,
                                     preferred_element_type=F32).astype(o_ref.dtype)

    return pl.pallas_call(
        body, name=name, grid=(S // tm, N_CHIPS // shards_per_block),
        in_specs=[pl.BlockSpec((tm, N), lambda i, j: (i, 0)),
                  pl.BlockSpec((shards_per_block, None, r, N), lambda i, j: (j, layer, 0, 0))],
        out_specs=pl.BlockSpec((tm, tn), lambda i, j: (i, j)),
        out_shape=_sds((S, N_CHIPS * r), out_dtype), compiler_params=_cparams(2))(dy, w_g)


def _nt_swiglu_bwd(dy, w_g, layer, gu, name, tm=256):
    S, N = dy.shape
    r = w_g.shape[2]
    tn = 2 * r
    F = N_CHIPS * r

    def body(dy_ref, w_ref, gate_ref, up_ref, o_ref):
        dact = lax.dot_general(dy_ref[...].astype(BF16), w_ref[...].reshape(tn, N), NT_DIMS,
                               preferred_element_type=F32)
        gate, up = gate_ref[...], up_ref[...]
        sg = jax.nn.sigmoid(gate)
        silu = gate * sg
        o_ref[0] = ((dact * up) * (sg + silu * (1.0 - sg))).astype(BF16)
        o_ref[1] = (dact * silu).astype(BF16)

    return pl.pallas_call(
        body, name=name, grid=(S // tm, 2),
        in_specs=[pl.BlockSpec((tm, N), lambda i, j: (i, 0)),
                  pl.BlockSpec((2, None, r, N), lambda i, j: (j, layer, 0, 0)),
                  pl.BlockSpec((tm, tn), lambda i, j: (i, j)),
                  pl.BlockSpec((tm, tn), lambda i, j: (i, 2 + j))],
        out_specs=pl.BlockSpec((2, tm, tn), lambda i, j: (0, i, j)),
        out_shape=_sds((2, S, F), BF16), compiler_params=_cparams(2))(dy, w_g, gu, gu)


def _chunk_causal_mask(transposed):
    i = lax.broadcasted_iota(jnp.int32, (A_CHUNK, A_CHUNK), 0) // CHUNK
    j = lax.broadcasted_iota(jnp.int32, (A_CHUNK, A_CHUNK), 1) // CHUNK
    return ((i <= j) if transposed else (i >= j)).astype(F32)


def _sgu_fwd(zpre, g_sgu, ws, bs_t, name):
    S, F2 = zpre.shape
    F = F2 // 2
    gd = F // A_GROUPS

    def body(zu_ref, zv_ref, g_ref, ws_ref, b_ref, o_ref):
        vhat, _ = _rms_hat(_gelu(zv_ref[...]))
        vn = (vhat * g_ref[...]).astype(BF16)
        u = _gelu(zu_ref[...])
        mask = _chunk_causal_mask(False)
        for gi in range(A_GROUPS):
            sl = slice(gi * gd, (gi + 1) * gd)
            wm = (ws_ref[gi] * mask).astype(BF16)
            vs = jnp.dot(wm, vn[:, sl], preferred_element_type=F32) + b_ref[:, gi:gi + 1]
            o_ref[:, sl] = (u[:, sl] * vs).astype(BF16)

    return pl.pallas_call(
        body, name=name, grid=(S // A_CHUNK,),
        in_specs=[pl.BlockSpec((A_CHUNK, F), lambda i: (i, 0)),
                  pl.BlockSpec((A_CHUNK, F), lambda i: (i, 1)),
                  pl.BlockSpec((1, F), lambda i: (0, 0)),
                  pl.BlockSpec((A_GROUPS, A_CHUNK, A_CHUNK), lambda i: (0, 0, 0)),
                  pl.BlockSpec((A_CHUNK, A_GROUPS), lambda i: (0, 0))],
        out_specs=pl.BlockSpec((A_CHUNK, F), lambda i: (i, 0)),
        out_shape=_sds((S, F), BF16), compiler_params=_cparams(1))(zpre, zpre, g_sgu, ws, bs_t)


def _sgu_bwd(zpre, duv, g_sgu, ws, ws_t, bs_t, name):
    S, F2 = zpre.shape
    F = F2 // 2
    gd = F // A_GROUPS

    def body(zu_ref, zv_ref, duv_ref, g_ref, ws_ref, wst_ref, b_ref, dz_ref, dg_ref, dws_ref, dbs_ref, dvn_ref):
        @pl.when(pl.program_id(0) == 0)
        def _():
            dg_ref[...] = jnp.zeros_like(dg_ref)
            dws_ref[...] = jnp.zeros_like(dws_ref)
            dbs_ref[...] = jnp.zeros_like(dbs_ref)

        zu, zv = zu_ref[...], zv_ref[...]
        gv = g_ref[...]
        vhat, r = _rms_hat(_gelu(zv))
        vn = (vhat * gv).astype(BF16)
        u = _gelu(zu)
        duv_v = duv_ref[...]
        dvs = duv_v * u
        dvs_b = dvs.astype(BF16)
        mask = _chunk_causal_mask(False)
        mask_t = _chunk_causal_mask(True)
        for gi in range(A_GROUPS):
            sl = slice(gi * gd, (gi + 1) * gd)
            wm = (ws_ref[gi] * mask).astype(BF16)
            vs = jnp.dot(wm, vn[:, sl], preferred_element_type=F32) + b_ref[:, gi:gi + 1]
            dz_ref[:, sl] = ((duv_v[:, sl] * vs) * _gelu_grad(zu[:, sl])).astype(BF16)
            dws_ref[gi] += lax.dot_general(dvs_b[:, sl], vn[:, sl], NT_DIMS, preferred_element_type=F32) * mask
            dbs_ref[gi] += jnp.broadcast_to(jnp.sum(dvs[:, sl], axis=1, keepdims=True), (A_CHUNK, A_CHUNK))
            wm_t = (wst_ref[gi] * mask_t).astype(BF16)
            dvn_ref[:, sl] = jnp.dot(wm_t, dvs_b[:, sl], preferred_element_type=F32)
        dv, dg_part = _rms_bwd(vhat, r, gv, dvn_ref[...])
        dg_ref[...] += jnp.sum(dg_part, axis=0, keepdims=True)
        dz_ref[:, F:] = (dv * _gelu_grad(zv)).astype(BF16)

    blk = pl.BlockSpec((A_CHUNK, F), lambda i: (i, 0))
    const3 = pl.BlockSpec((A_GROUPS, A_CHUNK, A_CHUNK), lambda i: (0, 0, 0))
    return pl.pallas_call(
        body, name=name, grid=(S // A_CHUNK,),
        in_specs=[blk, pl.BlockSpec((A_CHUNK, F), lambda i: (i, 1)), blk,
                  pl.BlockSpec((1, F), lambda i: (0, 0)), const3, const3,
                  pl.BlockSpec((A_CHUNK, A_GROUPS), lambda i: (0, 0))],
        out_specs=[pl.BlockSpec((A_CHUNK, F2), lambda i: (i, 0)), pl.BlockSpec((1, F), lambda i: (0, 0)),
                   const3, const3],
        out_shape=[_sds((S, F2), BF16), _sds((1, F), F32), _sds((A_GROUPS, A_CHUNK, A_CHUNK), F32),
                   _sds((A_GROUPS, A_CHUNK, A_CHUNK), F32)],
        scratch_shapes=[pltpu.VMEM((A_CHUNK, F), F32)],
        compiler_params=_cparams(1))(zpre, zpre, duv, g_sgu, ws, ws_t, bs_t)


def _rel_one_hot(qi):
    row = lax.broadcasted_iota(jnp.int32, (TABLE_PAD, BAND), 0)
    kj = lax.broadcasted_iota(jnp.int32, (TABLE_PAD, BAND), 1)
    idx = jnp.clip(qi - (kj - KV_PAD), -MAX_REL, MAX_REL) + MAX_REL
    return (row == idx).astype(F32)


def _rel_bias_fwd(table, name):
    H = table.shape[0]

    def body(t_ref, o_ref):
        def step(qi, carry):
            o_ref[qi] = jnp.dot(t_ref[...], _rel_one_hot(qi), precision=HIGHEST, preferred_element_type=F32)
            return carry

        lax.fori_loop(0, CHUNK, step, 0)

    return pl.pallas_call(body, name=name, out_shape=_sds((CHUNK, H, BAND), F32),
                          compiler_params=pltpu.CompilerParams(vmem_limit_bytes=VMEM_LIMIT_BYTES))(table)


def _rel_bias_bwd(dbias, name):
    H = dbias.shape[1]

    def body(d_ref, o_ref):
        o_ref[...] = jnp.zeros_like(o_ref)

        def step(qi, carry):
            o_ref[...] += lax.dot_general(d_ref[qi], _rel_one_hot(qi), NT_DIMS, precision=HIGHEST,
                                          preferred_element_type=F32)
            return carry

        lax.fori_loop(0, CHUNK, step, 0)

    return pl.pallas_call(body, name=name, out_shape=_sds((H, TABLE_PAD), F32),
                          compiler_params=pltpu.CompilerParams(vmem_limit_bytes=VMEM_LIMIT_BYTES))(dbias)


def _attn_probs(qh, kh, bias, chunk):
    s = lax.dot_general(qh, kh, NT_DIMS, preferred_element_type=F32) * ATTN_SCALE + bias
    kj = lax.broadcasted_iota(jnp.int32, (CHUNK, BAND), 1)
    valid = (kj >= CHUNK) & (chunk * CHUNK + kj - KV_PAD >= 0)
    s = jnp.where(valid, s, NEG_INF)
    e = jnp.exp(s - jnp.max(s, axis=-1, keepdims=True))
    return e / jnp.sum(e, axis=-1, keepdims=True)


def _attn_specs(S):
    lanes = HEADS_PER_BLOCK * HEAD_DIM
    rows = S + KV_PAD
    q_spec = pl.BlockSpec((Q_BLOCK, lanes), lambda h, i: (i, h))
    k_spec = pl.BlockSpec((rows, lanes), lambda h, i: (0, h))
    v_spec = pl.BlockSpec((rows, lanes), lambda h, i: (0, N_HEADS // HEADS_PER_BLOCK + h))
    b_spec = pl.BlockSpec((HEADS_PER_BLOCK, CHUNK, BAND), lambda h, i: (h, 0, 0))
    return q_spec, k_spec, v_spec, b_spec


def _attn_fwd(q, kvp, bias, name):
    S, HD = q.shape
    q_spec, k_spec, v_spec, b_spec = _attn_specs(S)

    def body(q_ref, k_ref, v_ref, b_ref, o_ref):
        for cc in range(Q_BLOCK // CHUNK):
            chunk = pl.program_id(1) * (Q_BLOCK // CHUNK) + cc
            band = pl.ds(pl.multiple_of(chunk * CHUNK, CHUNK), BAND)
            kb, vb = k_ref[band, :], v_ref[band, :]
            rows = slice(cc * CHUNK, (cc + 1) * CHUNK)
            for hh in range(HEADS_PER_BLOCK):
                hs = slice(hh * HEAD_DIM, (hh + 1) * HEAD_DIM)
                p = _attn_probs(q_ref[rows, hs], kb[:, hs], b_ref[hh], chunk)
                o_ref[rows, hs] = jnp.dot(p.astype(BF16), vb[:, hs], preferred_element_type=F32).astype(BF16)

    return pl.pallas_call(
        body, name=name, grid=(N_HEADS // HEADS_PER_BLOCK, S // Q_BLOCK),
        in_specs=[q_spec, k_spec, v_spec, b_spec], out_specs=q_spec,
        out_shape=_sds((S, HD), BF16), compiler_params=_cparams(2))(q, kvp, kvp, bias)


def _attn_bwd(q, kvp, bias, do, dkv_prev, name):
    S, HD = q.shape
    lanes = HEADS_PER_BLOCK * HEAD_DIM
    q_spec, k_spec, v_spec, b_spec = _attn_specs(S)
    dkv_spec = pl.BlockSpec((2, S + KV_PAD, lanes), lambda h, i: (0, 0, h))

    def body(q_ref, k_ref, v_ref, b_ref, do_ref, prev_ref, dq_ref, dkv_ref, db_ref):
        @pl.when(pl.program_id(1) == 0)
        def _():
            dkv_ref[...] = prev_ref[...]
            db_ref[...] = jnp.zeros_like(db_ref)

        for cc in range(Q_BLOCK // CHUNK):
            chunk = pl.program_id(1) * (Q_BLOCK // CHUNK) + cc
            band = pl.ds(pl.multiple_of(chunk * CHUNK, CHUNK), BAND)
            kb, vb = k_ref[band, :], v_ref[band, :]
            rows = slice(cc * CHUNK, (cc + 1) * CHUNK)
            for hh in range(HEADS_PER_BLOCK):
                hs = slice(hh * HEAD_DIM, (hh + 1) * HEAD_DIM)
                qh, doh = q_ref[rows, hs], do_ref[rows, hs]
                p = _attn_probs(qh, kb[:, hs], b_ref[hh], chunk)
                dp = lax.dot_general(doh, vb[:, hs], NT_DIMS, preferred_element_type=F32)
                ds = p * (dp - jnp.sum(dp * p, axis=-1, keepdims=True))
                db_ref[hh] += ds
                ds_b = (ds * ATTN_SCALE).astype(BF16)
                dq_ref[rows, hs] = jnp.dot(ds_b, kb[:, hs], preferred_element_type=F32).astype(BF16)
                dkv_ref[0, band, hs] += lax.dot_general(ds_b, qh, TN_DIMS, preferred_element_type=F32)
                dkv_ref[1, band, hs] += lax.dot_general(p.astype(BF16), doh, TN_DIMS, preferred_element_type=F32)

    return pl.pallas_call(
        body, name=name, grid=(N_HEADS // HEADS_PER_BLOCK, S // Q_BLOCK),
        in_specs=[q_spec, k_spec, v_spec, b_spec, q_spec, dkv_spec], out_specs=[q_spec, dkv_spec, b_spec],
        out_shape=[_sds((S, HD), BF16), _sds((2, S + KV_PAD, HD), F32), _sds((N_HEADS, CHUNK, BAND), F32)],
        compiler_params=_cparams(2))(q, kvp, kvp, bias, do, dkv_prev)


def _loss_head(x, g, target, name, tm=512):
    S, D = x.shape

    def body(x_ref, g_ref, t_ref, loss_ref, dx_ref, dg_ref):
        @pl.when(pl.program_id(0) == 0)
        def _():
            loss_ref[...] = jnp.zeros_like(loss_ref)
            dg_ref[...] = jnp.zeros_like(dg_ref)

        xhat, r = _rms_hat(x_ref[...])
        gv = g_ref[...]
        err = xhat * gv - t_ref[...]
        loss_ref[...] += 0.5 * jnp.sum(jnp.mean(err * err, axis=-1, keepdims=True))
        dx, dgp = _rms_bwd(xhat, r, gv, err * (1.0 / D))
        dx_ref[...] = dx
        dg_ref[...] += jnp.sum(dgp, axis=0, keepdims=True)

    row = pl.BlockSpec((tm, D), lambda i: (i, 0))
    vec = pl.BlockSpec((1, D), lambda i: (0, 0))
    return pl.pallas_call(
        body, name=name, grid=(S // tm,), in_specs=[row, vec, row],
        out_specs=[pl.BlockSpec((8, 128), lambda i: (0, 0)), row, vec],
        out_shape=[_sds((8, 128), F32), _sds((S, D), F32), _sds((1, D), F32)],
        compiler_params=_cparams(1))(x, g, target)


def _place():
    x, y, c = lax.axis_index("x"), lax.axis_index("y"), lax.axis_index("c")
    chips = [(1 - x, y), (x, 1 - y), (1 - x, 1 - y)]
    return x, y, c, chips


def _half_rows(c, r):
    return pl.ds(pl.multiple_of(c * (r // 2), 8), r // 2)


HBM_SPEC = pl.BlockSpec(memory_space=pl.ANY)


def _allgather_weights(shards, small, name):
    n = len(shards)

    def body(*refs):
        sh, sm, out, osm = refs[:n], refs[n], refs[n + 1:2 * n + 1], refs[2 * n + 1]
        send1, recv1, send2, recv2, send_s, recv_s, local = refs[2 * n + 2:]
        x, y, c, chips = _place()
        q = 2 * x + y
        me, sibling = (x, y, c), (x, y, 1 - c)
        own = [pltpu.make_async_copy(sh[w], out[w].at[q], local.at[w]) for w in range(n)]
        own.append(pltpu.make_async_copy(sm, osm.at[q], local.at[n]))
        for cp in own:
            cp.start()

        def piece(w, chip, core):
            return out[w].at[2 * chip[0] + chip[1], :, _half_rows(core, sh[w].shape[1]), :]

        sends = []
        for w in range(n):
            for j, chip in enumerate(chips):
                sends.append(pltpu.make_async_remote_copy(
                    src_ref=sh[w].at[:, _half_rows(c, sh[w].shape[1]), :], dst_ref=piece(w, (x, y), c),
                    send_sem=send1.at[w, j], recv_sem=recv1.at[w, j], device_id=(*chip, c), device_id_type=MESH))
        for j, chip in enumerate(chips):
            sends.append(pltpu.make_async_remote_copy(
                src_ref=sm, dst_ref=osm.at[q], send_sem=send_s.at[j], recv_sem=recv_s.at[j],
                device_id=(*chip, c), device_id_type=MESH))
        for cp in sends:
            cp.start()
        for w in range(n):
            for j, chip in enumerate(chips):
                landed = piece(w, chip, c)
                pltpu.make_async_remote_copy(src_ref=landed, dst_ref=landed, send_sem=send1.at[w, j],
                                             recv_sem=recv1.at[w, j], device_id=me, device_id_type=MESH).wait_recv()
                fwd = pltpu.make_async_remote_copy(src_ref=landed, dst_ref=landed, send_sem=send2.at[w, j],
                                                   recv_sem=recv2.at[w, j], device_id=sibling, device_id_type=MESH)
                fwd.start()
                sends.append(fwd)
        for j, chip in enumerate(chips):
            got = osm.at[2 * chip[0] + chip[1]]
            pltpu.make_async_remote_copy(src_ref=got, dst_ref=got, send_sem=send_s.at[j], recv_sem=recv_s.at[j],
                                         device_id=me, device_id_type=MESH).wait_recv()
        for w in range(n):
            for j, chip in enumerate(chips):
                got = piece(w, chip, 1 - c)
                pltpu.make_async_remote_copy(src_ref=got, dst_ref=got, send_sem=send2.at[w, j],
                                             recv_sem=recv2.at[w, j], device_id=me, device_id_type=MESH).wait_recv()
        for cp in sends:
            cp.wait_send()
        for cp in own:
            cp.wait()

    pair = pltpu.SemaphoreType.DMA((n, 3))
    return pl.pallas_call(
        body, name=name, in_specs=[HBM_SPEC] * (n + 1), out_specs=[HBM_SPEC] * (n + 1),
        out_shape=[_sds((N_CHIPS, *s.shape), s.dtype) for s in shards] + [_sds((N_CHIPS, *small.shape), small.dtype)],
        scratch_shapes=[pair, pair, pair, pair, pltpu.SemaphoreType.DMA((3,)), pltpu.SemaphoreType.DMA((3,)),
                        pltpu.SemaphoreType.DMA((n + 1,))],
    )(*shards, small)


def _sibling_exchange(grads, name):
    n = len(grads)

    def body(*refs):
        g, got, send, recv = refs[:n], refs[n:2 * n], refs[2 * n], refs[2 * n + 1]
        x, y, c, _ = _place()
        cps = [pltpu.make_async_remote_copy(
            src_ref=g[w].at[:, :, _half_rows(1 - c, g[w].shape[2]), :], dst_ref=got[w],
            send_sem=send.at[w], recv_sem=recv.at[w], device_id=(x, y, 1 - c), device_id_type=MESH) for w in range(n)]
        for cp in cps:
            cp.start()
        for cp in cps:
            cp.wait()

    return pl.pallas_call(
        body, name=name, in_specs=[HBM_SPEC] * n, out_specs=[HBM_SPEC] * n,
        out_shape=[_sds((a.shape[0], a.shape[1], a.shape[2] // 2, a.shape[3]), F32) for a in grads],
        scratch_shapes=[pltpu.SemaphoreType.DMA((n,)), pltpu.SemaphoreType.DMA((n,))])(*grads)


def _sibling_sum(grad, got, core, name):
    n_sh, L, r2, C = got.shape
    tr = _row_tile(r2, C, 4, 2 * 1024 * 1024, 16)
    nb = r2 // tr

    def body(core_ref, g_ref, o_ref_in, o_ref):
        del core_ref
        o_ref[...] = (g_ref[...] + o_ref_in[...]).astype(BF16)

    grid_spec = pltpu.PrefetchScalarGridSpec(
        num_scalar_prefetch=1, grid=(n_sh, L, nb),
        in_specs=[pl.BlockSpec((None, None, tr, C), lambda s, l, i, core_ref: (s, l, core_ref[0] * nb + i, 0)),
                  pl.BlockSpec((None, None, tr, C), lambda s, l, i, core_ref: (s, l, i, 0))],
        out_specs=pl.BlockSpec((None, None, tr, C), lambda s, l, i, core_ref: (s, l, i, 0)))
    return pl.pallas_call(body, name=name, grid_spec=grid_spec, out_shape=_sds(got.shape, BF16),
                          compiler_params=_cparams(3))(core, grad, got)


def _chip_exchange(parts, name):
    n = len(parts)

    def body(*refs):
        p, got, send, recv, local = refs[:n], refs[n:2 * n], refs[2 * n], refs[2 * n + 1], refs[2 * n + 2]
        x, y, c, chips = _place()
        q = 2 * x + y
        own = [pltpu.make_async_copy(p[w].at[q], got[w].at[q], local.at[w]) for w in range(n)]
        for cp in own:
            cp.start()
        cps = []
        for w in range(n):
            for j, chip in enumerate(chips):
                cps.append(pltpu.make_async_remote_copy(
                    src_ref=p[w].at[2 * chip[0] + chip[1]], dst_ref=got[w].at[q],
                    send_sem=send.at[w, j], recv_sem=recv.at[w, j], device_id=(*chip, c), device_id_type=MESH))
        for cp in cps:
            cp.start()
        for w in range(n):
            for j, chip in enumerate(chips):
                landed = got[w].at[2 * chip[0] + chip[1]]
                pltpu.make_async_remote_copy(src_ref=landed, dst_ref=landed, send_sem=send.at[w, j],
                                             recv_sem=recv.at[w, j], device_id=(x, y, c), device_id_type=MESH).wait_recv()
        for cp in cps:
            cp.wait_send()
        for cp in own:
            cp.wait()

    pair = pltpu.SemaphoreType.DMA((n, 3))
    return pl.pallas_call(
        body, name=name, in_specs=[HBM_SPEC] * n, out_specs=[HBM_SPEC] * n,
        out_shape=[_sds(a.shape, a.dtype) for a in parts],
        scratch_shapes=[pair, pair, pltpu.SemaphoreType.DMA((n,))])(*parts)


def _chip_sum(got, name):
    _, L, r2, C = got.shape
    tr = _row_tile(r2, C, 4, 2 * 1024 * 1024, 16)

    def body(g_ref, o_ref):
        acc = g_ref[0].astype(F32)
        for s in range(1, N_CHIPS):
            acc = acc + g_ref[s].astype(F32)
        o_ref[...] = acc

    return pl.pallas_call(
        body, name=name, grid=(L, r2 // tr),
        in_specs=[pl.BlockSpec((N_CHIPS, None, tr, C), lambda l, i: (0, l, i, 0))],
        out_specs=pl.BlockSpec((None, tr, C), lambda l, i: (l, i, 0)),
        out_shape=_sds((L, r2, C), F32), compiler_params=_cparams(2))(got)


def _sibling_join(halves, name):
    n = len(halves)

    def body(*refs):
        h, out, send, recv, local = refs[:n], refs[n:2 * n], refs[2 * n], refs[2 * n + 1], refs[2 * n + 2]
        x, y, c, _ = _place()
        own, cps = [], []
        for w in range(n):
            mine = out[w].at[:, _half_rows(c, out[w].shape[1]), :]
            own.append(pltpu.make_async_copy(h[w], mine, local.at[w]))
            cps.append(pltpu.make_async_remote_copy(src_ref=h[w], dst_ref=mine, send_sem=send.at[w],
                                                    recv_sem=recv.at[w], device_id=(x, y, 1 - c), device_id_type=MESH))
        for cp in own + cps:
            cp.start()
        for w in range(n):
            theirs = out[w].at[:, _half_rows(1 - c, out[w].shape[1]), :]
            pltpu.make_async_remote_copy(src_ref=theirs, dst_ref=theirs, send_sem=send.at[w], recv_sem=recv.at[w],
                                         device_id=(x, y, c), device_id_type=MESH).wait_recv()
        for cp in cps:
            cp.wait_send()
        for cp in own:
            cp.wait()

    return pl.pallas_call(
        body, name=name, in_specs=[HBM_SPEC] * n, out_specs=[HBM_SPEC] * n,
        out_shape=[_sds((a.shape[0], 2 * a.shape[1], a.shape[2]), F32) for a in halves],
        scratch_shapes=[pltpu.SemaphoreType.DMA((n,)), pltpu.SemaphoreType.DMA((n,)),
                        pltpu.SemaphoreType.DMA((n,))])(*halves)


def _gather_small(packed, name):
    def body(p_ref, out, send, recv, local):
        x, y, c, _ = _place()
        me = 4 * x + 2 * y + c
        own = pltpu.make_async_copy(p_ref, out.at[me], local)
        own.start()
        cps = []
        for k in range(1, N_DEV):
            fx, fy, fc = (k >> 2) & 1, (k >> 1) & 1, k & 1
            peer = ((x + fx) % 2, (y + fy) % 2, (c + fc) % 2)
            cps.append(pltpu.make_async_remote_copy(src_ref=p_ref, dst_ref=out.at[me], send_sem=send.at[k - 1],
                                                    recv_sem=recv.at[k - 1], device_id=peer, device_id_type=MESH))
        for cp in cps:
            cp.start()
        for k in range(1, N_DEV):
            fx, fy, fc = (k >> 2) & 1, (k >> 1) & 1, k & 1
            src = out.at[4 * ((x + fx) % 2) + 2 * ((y + fy) % 2) + (c + fc) % 2]
            pltpu.make_async_remote_copy(src_ref=src, dst_ref=src, send_sem=send.at[k - 1], recv_sem=recv.at[k - 1],
                                         device_id=(x, y, c), device_id_type=MESH).wait_recv()
        for cp in cps:
            cp.wait_send()
        own.wait()

    return pl.pallas_call(
        body, name=name, in_specs=[HBM_SPEC], out_specs=HBM_SPEC, out_shape=_sds((N_DEV, *packed.shape), F32),
        scratch_shapes=[pltpu.SemaphoreType.DMA((N_DEV - 1,)), pltpu.SemaphoreType.DMA((N_DEV - 1,)),
                        pltpu.SemaphoreType.DMA])(packed)


def _sum_devices(gathered, name):
    _, R, C = gathered.shape

    def body(g_ref, o_ref):
        acc = g_ref[0]
        for d in range(1, N_DEV):
            acc = acc + g_ref[d]
        o_ref[...] = acc

    tr = 8
    return pl.pallas_call(
        body, name=name, grid=(R // tr,), in_specs=[pl.BlockSpec((N_DEV, tr, C), lambda i: (0, i, 0))],
        out_specs=pl.BlockSpec((tr, C), lambda i: (i, 0)), out_shape=_sds((R, C), F32),
        compiler_params=_cparams(1))(gathered)


def _pack_small(arrays):
    rows = []
    for a in arrays:
        flat = a.reshape(-1)
        pad = (-flat.shape[0]) % SMALL_COLS
        rows.append(jnp.pad(flat, (0, pad)).reshape(-1, SMALL_COLS))
    packed = jnp.concatenate(rows, axis=0)
    return jnp.pad(packed, ((0, (-packed.shape[0]) % 8), (0, 0)))


def _unpack_small(packed, shapes):
    out, row = [], 0
    for shape in shapes:
        size = math.prod(shape)
        n_rows = -(-size // SMALL_COLS)
        out.append(packed[row:row + n_rows].reshape(-1)[:size].reshape(shape))
        row += n_rows
    return out


def kernel(x, a_norm, a_w_in, a_sgu_norm, a_w_spatial, a_b_spatial, a_w_out, kv_norm, w_kv, b_norm, b_w_q, b_rel_bias, b_w_o, ffn_norm, ffn_w_gate_up, ffn_w_down, final_norm, loss_target, m_a_norm, m_a_w_in, m_a_sgu_norm, m_a_w_spatial, m_a_b_spatial, m_a_w_out, m_kv_norm, m_w_kv, m_b_norm, m_b_w_q, m_b_rel_bias, m_b_w_o, m_ffn_norm, m_ffn_w_gate_up, m_ffn_w_down, m_final_norm, v_a_norm, v_a_w_in, v_a_sgu_norm, v_a_w_spatial, v_a_b_spatial, v_a_w_out, v_kv_norm, v_w_kv, v_b_norm, v_b_w_q, v_b_rel_bias, v_b_w_o, v_ffn_norm, v_ffn_w_gate_up, v_ffn_w_down, v_final_norm):
    S, D = x.shape[1], x.shape[2]
    n_a = a_w_in.shape[0]
    n_b = b_w_q.shape[0]
    depth = ffn_w_gate_up.shape[0]
    xi, yi, ci = lax.axis_index("x"), lax.axis_index("y"), lax.axis_index("c")
    chip = 2 * xi + yi

    big = {"a_w_in": a_w_in, "a_w_out": a_w_out, "w_kv": w_kv[None], "b_w_q": b_w_q, "b_w_o": b_w_o,
           "ffn_w_gate_up": ffn_w_gate_up, "ffn_w_down": ffn_w_down}
    names = list(big)
    shards = [_cast_bf16(big[k], "cast_" + k) for k in names]
    na_w, ns_w = a_norm.shape[1], a_sgu_norm.shape[1]
    *gathered, small_g = _allgather_weights(shards, jnp.concatenate([a_norm, a_sgu_norm], axis=1), "allgather_weights")
    W = dict(zip(names, gathered))
    a_norm_f = small_g[:, :, :na_w].transpose(1, 0, 2).reshape(n_a, N_CHIPS * na_w)
    a_sgu_f = small_g[:, :, na_w:].transpose(1, 0, 2).reshape(n_a, N_CHIPS * ns_w)

    xc = x.reshape(S, D)
    saved = []
    kvp = x_kv = h_kv = None
    for layer in range(depth):
        rec = {"x_in": xc}
        if layer < n_a:
            i = layer
            rec["zpre"], rec["h"] = _norm_matmul(xc, a_norm_f[i][None], W["a_w_in"], i, F32, f"a{i}_in")
            rec["uv"] = _sgu_fwd(rec["zpre"], a_sgu_f[i][None], a_w_spatial[i], a_b_spatial[i].T, f"a{i}_sgu")
            xm = _matmul_res(rec["uv"], W["a_w_out"], i, xc, f"a{i}_out")
        else:
            i = layer - n_a
            if i == 0:
                kv, h_kv = _norm_matmul(xc, kv_norm[None], W["w_kv"], 0, BF16, "kv_proj")
                kvp = jnp.pad(kv, ((KV_PAD, 0), (0, 0)))
                x_kv = xc
            rec["q"], rec["h"] = _norm_matmul(xc, b_norm[i][None], W["b_w_q"], i, BF16, f"b{i}_q", row_sharded=True)
            table = jnp.pad(b_rel_bias[i], ((0, 0), (0, TABLE_PAD - b_rel_bias.shape[2])))
            rec["bias"] = _rel_bias_fwd(table, f"b{i}_bias").transpose(1, 0, 2)
            rec["o"] = _attn_fwd(rec["q"], kvp, rec["bias"], f"b{i}_attn")
            xm = _matmul_res(rec["o"], W["b_w_o"], i, xc, f"b{i}_o")
        rec["x_mid"] = xm
        rec["gu"], rec["h_f"] = _norm_matmul(xm, ffn_norm[layer][None], W["ffn_w_gate_up"], layer, F32, f"f{layer}_in")
        xc = _matmul_res(rec["gu"], W["ffn_w_down"], layer, xm, f"f{layer}_out", swiglu=True)
        saved.append(rec)

    loss_tile, dx, d_final = _loss_head(xc, final_norm[None], loss_target.reshape(S, D), "loss_head")
    loss = lax.psum(loss_tile[0, 0], ("x", "y", "c"))

    G = {}

    def weight_grad(key, layer, **kw):
        full = (N_CHIPS,) + tuple(big[key].shape)
        G[key] = _matmul_tn(out_shape=full, gbuf=G.get(key), name=f"d_{key}_{layer}", acc_shape=None, **kw)

    row_a = lambda w: pl.BlockSpec((512, w), lambda o, t: (t, 0))
    d_ffn_norm, d_b_norm, d_a_norm, d_a_sgu = [None] * depth, [None] * n_b, [None] * n_a, [None] * n_a
    d_ws, d_bs, d_rel = [None] * n_a, [None] * n_a, [None] * n_b
    dkv = jnp.zeros((2, S + KV_PAD, D), F32)
    first = lambda ref: ref[...]
    for layer in reversed(range(depth)):
        rec = saved[layer]
        F = ffn_w_down.shape[1] * N_CHIPS
        half_f = F // 2
        r_d = ffn_w_down.shape[1]
        weight_grad(
            "ffn_w_down", layer, a_ops=[rec["gu"], rec["gu"]],
            a_specs=[pl.BlockSpec((512, half_f), lambda o, t: (t, o)), pl.BlockSpec((512, half_f), lambda o, t: (t, 2 + o))],
            a_fn=lambda g_ref, u_ref: _swiglu(g_ref[...], u_ref[...]).astype(BF16),
            b_op=dx, b_spec=row_a(D),
            out_spec=pl.BlockSpec((2, None, r_d, D), lambda o, t, layer=layer: (o, layer, 0, 0)), n_outer=2)
        dgu = _nt_swiglu_bwd(dx, W["ffn_w_down"], layer, rec["gu"], f"f{layer}_dgu")
        nsh = ffn_w_gate_up.shape[2]
        weight_grad(
            "ffn_w_gate_up", layer, a_ops=[rec["h_f"]], a_specs=[row_a(D)], a_fn=first,
            b_op=dgu, b_spec=pl.BlockSpec((None, 512, nsh), lambda o, t: (o // 2, t, o % 2)),
            out_spec=pl.BlockSpec((None, None, D, nsh), lambda o, t, layer=layer: (o, layer, 0, 0)), n_outer=N_CHIPS)
        dx, d_ffn_norm[layer] = _nt_normbwd(
            dgu, pl.BlockSpec((None, 512, nsh), lambda i, k: (k // 2, i, k % 2)),
            W["ffn_w_gate_up"], pl.BlockSpec((None, None, D, nsh), lambda i, k, layer=layer: (k, layer, 0, 0)),
            (D, nsh), N_CHIPS, rec["x_mid"], ffn_norm[layer][None], dx, f"f{layer}_dx")
        if layer >= n_a:
            i = layer - n_a
            r_o = b_w_o.shape[1]
            weight_grad(
                "b_w_o", i, a_ops=[rec["o"]], a_specs=[row_a(D)], a_fn=first,
                b_op=dx, b_spec=pl.BlockSpec((512, 512), lambda o, t: (t, o)),
                out_spec=pl.BlockSpec((N_CHIPS, None, r_o, 512), lambda o, t, i=i: (0, i, 0, o)), n_outer=2)
            do = _nt_rows(dx, W["b_w_o"], i, 2, BF16, f"b{i}_do")
            dq, dkv, dbias = _attn_bwd(rec["q"], kvp, rec["bias"], do, dkv, f"b{i}_attn_bwd")
            d_rel[i] = _rel_bias_bwd(dbias.transpose(1, 0, 2), f"b{i}_dbias")[:, :b_rel_bias.shape[2]]
            weight_grad(
                "b_w_q", i, a_ops=[rec["h"]], a_specs=[row_a(D)], a_fn=first,
                b_op=dq, b_spec=pl.BlockSpec((512, 512), lambda o, t: (t, o)),
                out_spec=pl.BlockSpec((N_CHIPS, None, r_o, 512), lambda o, t, i=i: (0, i, 0, o)), n_outer=2)
            dx, d_b_norm[i] = _nt_normbwd(
                dq, pl.BlockSpec((512, D), lambda i_, k: (i_, 0)),
                W["b_w_q"], pl.BlockSpec((N_CHIPS, None, r_o, D), lambda i_, k, i=i: (0, i, 0, 0)),
                (D, D), 1, rec["x_in"], b_norm[i][None], dx, f"b{i}_dx")
            if i == 0:
                dkv_b = dkv[:, KV_PAD:, :].astype(BF16)
                n_kv = w_kv.shape[1]
                weight_grad(
                    "w_kv", 0, a_ops=[h_kv], a_specs=[row_a(D)], a_fn=first,
                    b_op=dkv_b, b_spec=pl.BlockSpec((None, 512, n_kv), lambda o, t: (o // 2, t, o % 2)),
                    out_spec=pl.BlockSpec((None, None, D, n_kv), lambda o, t: (o, 0, 0, 0)), n_outer=N_CHIPS)
                dx, d_kv_norm = _nt_normbwd(
                    dkv_b, pl.BlockSpec((None, 512, n_kv), lambda i_, k: (k // 2, i_, k % 2)),
                    W["w_kv"], pl.BlockSpec((None, None, D, n_kv), lambda i_, k: (k, 0, 0, 0)),
                    (D, n_kv), N_CHIPS, x_kv, kv_norm[None], dx, "kv_dx")
        else:
            i = layer
            r_w = a_w_out.shape[1]
            weight_grad(
                "a_w_out", i, a_ops=[rec["uv"]], a_specs=[pl.BlockSpec((512, r_w), lambda o, t: (t, o))], a_fn=first,
                b_op=dx, b_spec=row_a(D),
                out_spec=pl.BlockSpec((None, None, r_w, D), lambda o, t, i=i: (o, i, 0, 0)), n_outer=N_CHIPS)
            duv = _nt_rows(dx, W["a_w_out"], i, 1, F32, f"a{i}_duv")
            dz, d_a_sgu[i], d_ws[i], dbs = _sgu_bwd(rec["zpre"], duv, a_sgu_f[i][None], a_w_spatial[i],
                                                  a_w_spatial[i].transpose(0, 2, 1), a_b_spatial[i].T, f"a{i}_sgu_bwd")
            d_bs[i] = dbs[:, :, 0]
            n_in = a_w_in.shape[2]
            weight_grad(
                "a_w_in", i, a_ops=[rec["h"]], a_specs=[row_a(D)], a_fn=first,
                b_op=dz, b_spec=pl.BlockSpec((512, 512), lambda o, t: (t, o)),
                out_spec=pl.BlockSpec((None, None, D, 512), lambda o, t, i=i: (o // 2, i, 0, o % 2)),
                n_outer=N_CHIPS * n_in // 512)
            dx, d_a_norm[i] = _nt_normbwd(
                dz, pl.BlockSpec((512, 512), lambda i_, k: (i_, k)),
                W["a_w_in"], pl.BlockSpec((None, None, D, 512), lambda i_, k, i=i: (k // 2, i, 0, k % 2)),
                (D, 512), N_CHIPS * n_in // 512, rec["x_in"], a_norm_f[i][None], dx, f"a{i}_dx")
    grad_x = dx.reshape(x.shape)

    core = ci.reshape(1).astype(jnp.int32)
    grads = [G[k] for k in names]
    theirs = _sibling_exchange(grads, "rs_sibling_exchange")
    parts = [_sibling_sum(g, t, core, f"rs_sibling_sum_{k}") for g, t, k in zip(grads, theirs, names)]
    from_chips = _chip_exchange(parts, "rs_chip_exchange")
    halves = [_chip_sum(p, f"rs_chip_sum_{k}") for p, k in zip(from_chips, names)]
    reduced = dict(zip(names, _sibling_join(halves, "rs_sibling_join")))
    reduced["w_kv"] = reduced["w_kv"][0]

    small = [jnp.concatenate(d_a_norm, axis=0), jnp.concatenate(d_a_sgu, axis=0), jnp.stack(d_ws), jnp.stack(d_bs),
             d_kv_norm, jnp.concatenate(d_b_norm, axis=0), jnp.stack(d_rel), jnp.concatenate(d_ffn_norm, axis=0), d_final]
    total = _sum_devices(_gather_small(_pack_small(small), "gather_small_grads"), "sum_small_grads")
    (g_a_norm, g_a_sgu, g_ws, g_bs, g_kv_norm, g_b_norm, g_rel, g_ffn_norm, g_final) = _unpack_small(
        total, [a.shape for a in small])
    reduced["a_norm"] = lax.dynamic_slice_in_dim(g_a_norm, chip * na_w, na_w, axis=1)
    reduced["a_sgu_norm"] = lax.dynamic_slice_in_dim(g_a_sgu, chip * ns_w, ns_w, axis=1)
    reduced.update(a_w_spatial=g_ws, a_b_spatial=g_bs, kv_norm=g_kv_norm.reshape(kv_norm.shape), b_norm=g_b_norm,
                   b_rel_bias=g_rel, ffn_norm=g_ffn_norm, final_norm=g_final.reshape(final_norm.shape))

    weights = dict(a_norm=a_norm, a_w_in=a_w_in, a_sgu_norm=a_sgu_norm, a_w_spatial=a_w_spatial,
                   a_b_spatial=a_b_spatial, a_w_out=a_w_out, kv_norm=kv_norm, w_kv=w_kv, b_norm=b_norm, b_w_q=b_w_q,
                   b_rel_bias=b_rel_bias, b_w_o=b_w_o, ffn_norm=ffn_norm, ffn_w_gate_up=ffn_w_gate_up,
                   ffn_w_down=ffn_w_down, final_norm=final_norm)
    m_in = dict(a_norm=m_a_norm, a_w_in=m_a_w_in, a_sgu_norm=m_a_sgu_norm, a_w_spatial=m_a_w_spatial,
                a_b_spatial=m_a_b_spatial, a_w_out=m_a_w_out, kv_norm=m_kv_norm, w_kv=m_w_kv, b_norm=m_b_norm,
                b_w_q=m_b_w_q, b_rel_bias=m_b_rel_bias, b_w_o=m_b_w_o, ffn_norm=m_ffn_norm,
                ffn_w_gate_up=m_ffn_w_gate_up, ffn_w_down=m_ffn_w_down, final_norm=m_final_norm)
    v_in = dict(a_norm=v_a_norm, a_w_in=v_a_w_in, a_sgu_norm=v_a_sgu_norm, a_w_spatial=v_a_w_spatial,
                a_b_spatial=v_a_b_spatial, a_w_out=v_a_w_out, kv_norm=v_kv_norm, w_kv=v_w_kv, b_norm=v_b_norm,
                b_w_q=v_b_w_q, b_rel_bias=v_b_rel_bias, b_w_o=v_b_w_o, ffn_norm=v_ffn_norm,
                ffn_w_gate_up=v_ffn_w_gate_up, ffn_w_down=v_ffn_w_down, final_norm=v_final_norm)
    grad_out, delta_out, m_out, v_out = [], [], [], []
    for key, w in weights.items():
        g = reduced[key].reshape(w.shape)
        view = (1, w.shape[0]) if w.ndim == 1 else (-1, w.shape[-1])
        d, nm, nv = _adamw(w.reshape(view), g.reshape(view), m_in[key].reshape(view), v_in[key].reshape(view),
                           "adamw_" + key)
        grad_out.append(g)
        delta_out.append(d.reshape(w.shape))
        m_out.append(nm.reshape(w.shape))
        v_out.append(nv.reshape(w.shape))
    return (loss, grad_x, *grad_out, *delta_out, *m_out, *v_out)
```

```python
import math

import jax
import jax.numpy as jnp
from jax import lax
from jax.experimental import pallas as pl
from jax.experimental.pallas import tpu as pltpu

F32, BF16 = jnp.float32, jnp.bfloat16
MESH = pl.DeviceIdType.MESH
HIGHEST = lax.Precision.HIGHEST
NT_DIMS = (((1,), (1,)), ((), ()))
TN_DIMS = (((0,), (0,)), ((), ()))

EPS = 1e-6
CHUNK = 64
A_CHUNK = 128
A_GROUPS = 8
N_HEADS = 16
HEAD_DIM = 64
N_LEFT = 8
MAX_REL = 256
ATTN_SCALE = HEAD_DIM ** -0.5
NEG_INF = -1e30
Q_BLOCK = 2 * CHUNK
KV_PAD = N_LEFT * CHUNK
BAND = KV_PAD + Q_BLOCK
DIAGS = BAND + Q_BLOCK
TABLE_PAD = 640
HEADS_PER_BLOCK = 2

ADAM_LR, ADAM_B1, ADAM_B2, ADAM_EPS, ADAM_WD, ADAM_STEP = 0.001, 0.9, 0.999, 1e-08, 0.01, 10

VMEM_LIMIT_BYTES = 56 * 1024 * 1024
N_CHIPS = 4
N_DEV = 8
SMALL_COLS = 1024


def _cparams(n_grid):
    return pltpu.CompilerParams(dimension_semantics=("arbitrary",) * n_grid, vmem_limit_bytes=VMEM_LIMIT_BYTES)


def _sds(shape, dtype):
    return jax.ShapeDtypeStruct(tuple(shape), dtype)


def _gelu(x):
    return 0.5 * x * (1.0 + lax.erf(x * math.sqrt(0.5)))


def _gelu_grad(x):
    return 0.5 * (1.0 + lax.erf(x * math.sqrt(0.5))) + x * (jnp.exp(-0.5 * x * x) * (1.0 / math.sqrt(2.0 * math.pi)))


def _rms_hat(xv):
    r = lax.rsqrt(jnp.mean(xv * xv, axis=-1, keepdims=True) + EPS)
    return xv * r, r


def _rms_bwd(xhat, r, g, dy):
    dxhat = dy * g
    dx = r * (dxhat - xhat * jnp.mean(dxhat * xhat, axis=-1, keepdims=True))
    return dx, dy * xhat


def _swiglu(gate, up):
    return (gate * jax.nn.sigmoid(gate)) * up


def _row_tile(rows, cols, itemsize, cap_bytes, align):
    t = rows
    while t * cols * itemsize > cap_bytes and t % (2 * align) == 0:
        t //= 2
    return t


def _cast_bf16(w, chip, name):
    L, r, C = w.shape
    tr = _row_tile(r, C, 4, 4 * 1024 * 1024, 16)

    def body(chip_ref, w_ref, o_ref):
        del chip_ref
        o_ref[...] = w_ref[...].astype(BF16)

    grid_spec = pltpu.PrefetchScalarGridSpec(
        num_scalar_prefetch=1, grid=(L, r // tr),
        in_specs=[pl.BlockSpec((None, tr, C), lambda l, i, chip_ref: (l, i, 0))],
        out_specs=pl.BlockSpec((None, None, tr, C), lambda l, i, chip_ref: (chip_ref[0], l, i, 0)))
    return pl.pallas_call(body, name=name, grid_spec=grid_spec, out_shape=_sds((N_CHIPS, L, r, C), BF16),
                          compiler_params=_cparams(2))(chip, w)


def _adamw(w, g, m, v, name):
    R, C = w.shape
    tr = _row_tile(R, C, 4, 1024 * 1024, 8)

    def body(w_ref, g_ref, m_ref, v_ref, d_ref, nm_ref, nv_ref):
        gv = g_ref[...]
        mn = ADAM_B1 * m_ref[...] + (1.0 - ADAM_B1) * gv
        vn = ADAM_B2 * v_ref[...] + (1.0 - ADAM_B2) * jnp.square(gv)
        m_hat = mn / (1.0 - ADAM_B1 ** ADAM_STEP)
        v_hat = vn / (1.0 - ADAM_B2 ** ADAM_STEP)
        d_ref[...] = -ADAM_LR * (m_hat / (jnp.sqrt(v_hat) + ADAM_EPS) + ADAM_WD * w_ref[...])
        nm_ref[...] = mn
        nv_ref[...] = vn

    spec = pl.BlockSpec((tr, C), lambda i: (i, 0))
    return pl.pallas_call(body, name=name, grid=(R // tr,), in_specs=[spec] * 4, out_specs=[spec] * 3,
                          out_shape=[_sds((R, C), F32)] * 3, compiler_params=_cparams(1))(w, g, m, v)


def _norm_matmul(x, g, w_g, layer, out_dtype, name, row_sharded=False, tm=1024):
    S, D = x.shape
    tm = min(tm, S)
    if row_sharded:
        r, N = w_g.shape[2], w_g.shape[3]
        tn = 512
        w_spec = pl.BlockSpec((N_CHIPS, None, r, tn), lambda i, j: (0, layer, 0, j))
    else:
        nsh = w_g.shape[3]
        N = N_CHIPS * nsh
        tn = 512 if nsh % 512 == 0 else nsh
        bps = nsh // tn
        w_spec = pl.BlockSpec((None, None, D, tn), lambda i, j: (j // bps, layer, 0, j % bps))

    def body(x_ref, g_ref, w_ref, y_ref, h_ref):
        @pl.when(pl.program_id(1) == 0)
        def _():
            xhat, _ = _rms_hat(x_ref[...])
            h_ref[...] = (xhat * g_ref[...]).astype(BF16)

        w = w_ref[...].reshape(D, tn)
        y_ref[...] = jnp.dot(h_ref[...], w, preferred_element_type=F32).astype(y_ref.dtype)

    return pl.pallas_call(
        body, name=name, grid=(S // tm, N // tn),
        in_specs=[pl.BlockSpec((tm, D), lambda i, j: (i, 0)), pl.BlockSpec((1, D), lambda i, j: (0, 0)), w_spec],
        out_specs=[pl.BlockSpec((tm, tn), lambda i, j: (i, j)), pl.BlockSpec((tm, D), lambda i, j: (i, 0))],
        out_shape=[_sds((S, N), out_dtype), _sds((S, D), BF16)],
        compiler_params=_cparams(2))(x, g, w_g)


def _matmul_res(a, w_g, layer, res, name, swiglu=False, tm=256):
    S, N = res.shape
    r = w_g.shape[2]
    K = N_CHIPS * r

    def body(*refs):
        if swiglu:
            gate_ref, up_ref, w_ref, res_ref, o_ref = refs
            a_blk = _swiglu(gate_ref[...], up_ref[...]).astype(BF16)
        else:
            a_ref, w_ref, res_ref, o_ref = refs
            a_blk = a_ref[...]
        o_ref[...] = res_ref[...] + jnp.dot(a_blk, w_ref[...].reshape(K, N), preferred_element_type=F32)

    a_specs, a_ops = [pl.BlockSpec((tm, K), lambda i: (i, 0))], [a]
    if swiglu:
        a_specs.append(pl.BlockSpec((tm, K), lambda i: (i, 1)))
        a_ops.append(a)
    row = pl.BlockSpec((tm, N), lambda i: (i, 0))
    return pl.pallas_call(
        body, name=name, grid=(S // tm,),
        in_specs=a_specs + [pl.BlockSpec((N_CHIPS, None, r, N), lambda i: (0, layer, 0, 0)), row],
        out_specs=row, out_shape=_sds((S, N), F32), compiler_params=_cparams(1))(*a_ops, w_g, res)


def _matmul_tn(a_ops, a_specs, a_fn, b_op, b_spec, out_spec, out_shape, acc_shape, n_outer, name, gbuf=None, tt=512):
    S = b_op.shape[-2]
    na = len(a_ops)

    def body(*refs):
        a_refs, b_ref, o_ref = refs[:na], refs[na], refs[-1]

        @pl.when(pl.program_id(1) == 0)
        def _():
            o_ref[...] = jnp.zeros_like(o_ref)

        part = lax.dot_general(a_fn(*a_refs), b_ref[...].astype(BF16), TN_DIMS, preferred_element_type=F32)
        o_ref[...] += part.reshape(o_ref.shape)

    del acc_shape
    in_specs = list(a_specs) + [b_spec]
    ops = list(a_ops) + [b_op]
    aliases = {}
    if gbuf is not None:
        in_specs.append(pl.BlockSpec(memory_space=pl.ANY))
        ops.append(gbuf)
        aliases = {len(ops) - 1: 0}
    return pl.pallas_call(
        body, name=name, grid=(n_outer, S // tt), in_specs=in_specs, out_specs=out_spec,
        out_shape=_sds(out_shape, F32), input_output_aliases=aliases, compiler_params=_cparams(2))(*ops)


def _nt_accumulate(a_ref, w_ref, acc_ref, w2d, nk):
    k = pl.program_id(1)
    part = lax.dot_general(a_ref[...].astype(BF16), w_ref[...].reshape(w2d), NT_DIMS, preferred_element_type=F32)

    @pl.when(k == 0)
    def _():
        acc_ref[...] = part

    @pl.when(k > 0)
    def _():
        acc_ref[...] += part

    return k == nk - 1


def _nt_normbwd(dy, dy_spec, w_g, w_spec, w2d, nk, x, g, dres, name, tm=512):
    S, D = x.shape

    def body(dy_ref, w_ref, x_ref, g_ref, dres_ref, dx_ref, dg_ref, acc_ref):
        @pl.when((pl.program_id(0) == 0) & (pl.program_id(1) == 0))
        def _():
            dg_ref[...] = jnp.zeros_like(dg_ref)

        last = _nt_accumulate(dy_ref, w_ref, acc_ref, w2d, nk)

        @pl.when(last)
        def _():
            xhat, r = _rms_hat(x_ref[...])
            dx, dgp = _rms_bwd(xhat, r, g_ref[...], acc_ref[...])
            dx_ref[...] = dres_ref[...] + dx
            dg_ref[...] += jnp.sum(dgp, axis=0, keepdims=True)

    row = pl.BlockSpec((tm, D), lambda i, k: (i, 0))
    vec = pl.BlockSpec((1, D), lambda i, k: (0, 0))
    return pl.pallas_call(
        body, name=name, grid=(S // tm, nk),
        in_specs=[dy_spec, w_spec, row, vec, row], out_specs=[row, vec],
        out_shape=[_sds((S, D), F32), _sds((1, D), F32)],
        scratch_shapes=[pltpu.VMEM((tm, D), F32)], compiler_params=_cparams(2))(dy, w_g, x, g, dres)


def _nt_rows(dy, w_g, layer, shards_per_block, out_dtype, name, tm=512):
    S, N = dy.shape
    r = w_g.shape[2]
    tn = shards_per_block * r

    def body(dy_ref, w_ref, o_ref):
        o_ref[...] = lax.dot_general(dy_ref[...].astype(BF16), w_ref[...].reshape(tn, N), NT_DIMS,
                                     preferred_element_type=F32).astype(o_ref.dtype)

    return pl.pallas_call(
        body, name=name, grid=(S // tm, N_CHIPS // shards_per_block),
        in_specs=[pl.BlockSpec((tm, N), lambda i, j: (i, 0)),
                  pl.BlockSpec((shards_per_block, None, r, N), lambda i, j: (j, layer, 0, 0))],
        out_specs=pl.BlockSpec((tm, tn), lambda i, j: (i, j)),
        out_shape=_sds((S, N_CHIPS * r), out_dtype), compiler_params=_cparams(2))(dy, w_g)


def _nt_swiglu_bwd(dy, w_g, layer, gu, name, tm=256):
    S, N = dy.shape
    r = w_g.shape[2]
    tn = 2 * r
    F = N_CHIPS * r

    def body(dy_ref, w_ref, gate_ref, up_ref, o_ref):
        dact = lax.dot_general(dy_ref[...].astype(BF16), w_ref[...].reshape(tn, N), NT_DIMS,
                               preferred_element_type=F32)
        gate, up = gate_ref[...], up_ref[...]
        sg = jax.nn.sigmoid(gate)
        silu = gate * sg
        o_ref[0] = ((dact * up) * (sg + silu * (1.0 - sg))).astype(BF16)
        o_ref[1] = (dact * silu).astype(BF16)

    return pl.pallas_call(
        body, name=name, grid=(S // tm, 2),
        in_specs=[pl.BlockSpec((tm, N), lambda i, j: (i, 0)),
                  pl.BlockSpec((2, None, r, N), lambda i, j: (j, layer, 0, 0)),
                  pl.BlockSpec((tm, tn), lambda i, j: (i, j)),
                  pl.BlockSpec((tm, tn), lambda i, j: (i, 2 + j))],
        out_specs=pl.BlockSpec((2, tm, tn), lambda i, j: (0, i, j)),
        out_shape=_sds((2, S, F), BF16), compiler_params=_cparams(2))(dy, w_g, gu, gu)


def _chunk_causal_mask(transposed):
    i = lax.broadcasted_iota(jnp.int32, (A_CHUNK, A_CHUNK), 0) // CHUNK
    j = lax.broadcasted_iota(jnp.int32, (A_CHUNK, A_CHUNK), 1) // CHUNK
    return ((i <= j) if transposed else (i >= j)).astype(F32)


def _sgu_fwd(zpre, g_sgu, ws, bs_t, name):
    S, F2 = zpre.shape
    F = F2 // 2
    gd = F // A_GROUPS

    def body(zu_ref, zv_ref, g_ref, ws_ref, b_ref, o_ref):
        vhat, _ = _rms_hat(_gelu(zv_ref[...]))
        vn = (vhat * g_ref[...]).astype(BF16)
        u = _gelu(zu_ref[...])
        mask = _chunk_causal_mask(False)
        for gi in range(A_GROUPS):
            sl = slice(gi * gd, (gi + 1) * gd)
            wm = (ws_ref[gi] * mask).astype(BF16)
            vs = jnp.dot(wm, vn[:, sl], preferred_element_type=F32) + b_ref[:, gi:gi + 1]
            o_ref[:, sl] = (u[:, sl] * vs).astype(BF16)

    return pl.pallas_call(
        body, name=name, grid=(S // A_CHUNK,),
        in_specs=[pl.BlockSpec((A_CHUNK, F), lambda i: (i, 0)),
                  pl.BlockSpec((A_CHUNK, F), lambda i: (i, 1)),
                  pl.BlockSpec((1, F), lambda i: (0, 0)),
                  pl.BlockSpec((A_GROUPS, A_CHUNK, A_CHUNK), lambda i: (0, 0, 0)),
                  pl.BlockSpec((A_CHUNK, A_GROUPS), lambda i: (0, 0))],
        out_specs=pl.BlockSpec((A_CHUNK, F), lambda i: (i, 0)),
        out_shape=_sds((S, F), BF16), compiler_params=_cparams(1))(zpre, zpre, g_sgu, ws, bs_t)


def _sgu_bwd(zpre, duv, g_sgu, ws, ws_t, bs_t, name):
    S, F2 = zpre.shape
    F = F2 // 2
    gd = F // A_GROUPS

    def body(zu_ref, zv_ref, duv_ref, g_ref, ws_ref, wst_ref, b_ref, dz_ref, dg_ref, dws_ref, dbs_ref, dvn_ref):
        @pl.when(pl.program_id(0) == 0)
        def _():
            dg_ref[...] = jnp.zeros_like(dg_ref)
            dws_ref[...] = jnp.zeros_like(dws_ref)
            dbs_ref[...] = jnp.zeros_like(dbs_ref)

        zu, zv = zu_ref[...], zv_ref[...]
        gv = g_ref[...]
        vhat, r = _rms_hat(_gelu(zv))
        vn = (vhat * gv).astype(BF16)
        u = _gelu(zu)
        duv_v = duv_ref[...]
        dvs = duv_v * u
        dvs_b = dvs.astype(BF16)
        mask = _chunk_causal_mask(False)
        mask_t = _chunk_causal_mask(True)
        for gi in range(A_GROUPS):
            sl = slice(gi * gd, (gi + 1) * gd)
            wm = (ws_ref[gi] * mask).astype(BF16)
            vs = jnp.dot(wm, vn[:, sl], preferred_element_type=F32) + b_ref[:, gi:gi + 1]
            dz_ref[:, sl] = ((duv_v[:, sl] * vs) * _gelu_grad(zu[:, sl])).astype(BF16)
            dws_ref[gi] += lax.dot_general(dvs_b[:, sl], vn[:, sl], NT_DIMS, preferred_element_type=F32) * mask
            dbs_ref[gi] += jnp.broadcast_to(jnp.sum(dvs[:, sl], axis=1, keepdims=True), (A_CHUNK, A_CHUNK))
            wm_t = (wst_ref[gi] * mask_t).astype(BF16)
            dvn_ref[:, sl] = jnp.dot(wm_t, dvs_b[:, sl], preferred_element_type=F32)
        dv, dg_part = _rms_bwd(vhat, r, gv, dvn_ref[...])
        dg_ref[...] += jnp.sum(dg_part, axis=0, keepdims=True)
        dz_ref[:, F:] = (dv * _gelu_grad(zv)).astype(BF16)

    blk = pl.BlockSpec((A_CHUNK, F), lambda i: (i, 0))
    const3 = pl.BlockSpec((A_GROUPS, A_CHUNK, A_CHUNK), lambda i: (0, 0, 0))
    return pl.pallas_call(
        body, name=name, grid=(S // A_CHUNK,),
        in_specs=[blk, pl.BlockSpec((A_CHUNK, F), lambda i: (i, 1)), blk,
                  pl.BlockSpec((1, F), lambda i: (0, 0)), const3, const3,
                  pl.BlockSpec((A_CHUNK, A_GROUPS), lambda i: (0, 0))],
        out_specs=[pl.BlockSpec((A_CHUNK, F2), lambda i: (i, 0)), pl.BlockSpec((1, F), lambda i: (0, 0)),
                   const3, const3],
        out_shape=[_sds((S, F2), BF16), _sds((1, F), F32), _sds((A_GROUPS, A_CHUNK, A_CHUNK), F32),
                   _sds((A_GROUPS, A_CHUNK, A_CHUNK), F32)],
        scratch_shapes=[pltpu.VMEM((A_CHUNK, F), F32)],
        compiler_params=_cparams(1))(zpre, zpre, duv, g_sgu, ws, ws_t, bs_t)


def _toeplitz_one_hot():
    row = lax.broadcasted_iota(jnp.int32, (TABLE_PAD, DIAGS), 0)
    j = lax.broadcasted_iota(jnp.int32, (TABLE_PAD, DIAGS), 1)
    idx = jnp.clip(KV_PAD + Q_BLOCK - j, -MAX_REL, MAX_REL) + MAX_REL
    return (row == idx).astype(F32)


def _rel_bias_fwd(table, name):
    H = table.shape[0]

    def body(t_ref, o_ref):
        diag = jnp.dot(t_ref[...], _toeplitz_one_hot(), precision=HIGHEST, preferred_element_type=F32)

        def step(r, carry):
            o_ref[r] = pltpu.roll(diag, DIAGS - Q_BLOCK + r, 1)[:, :BAND]
            return carry

        lax.fori_loop(0, Q_BLOCK, step, 0)

    return pl.pallas_call(body, name=name, out_shape=_sds((Q_BLOCK, H, BAND), F32),
                          compiler_params=pltpu.CompilerParams(vmem_limit_bytes=VMEM_LIMIT_BYTES))(table)


def _rel_bias_bwd(dbias, name):
    H = dbias.shape[1]

    def body(d_ref, o_ref):
        def step(r, acc):
            row = jnp.concatenate([d_ref[r], jnp.zeros((H, DIAGS - BAND), F32)], axis=1)
            return acc + pltpu.roll(row, Q_BLOCK - r, 1)

        diag = lax.fori_loop(0, Q_BLOCK, step, jnp.zeros((H, DIAGS), F32))
        o_ref[...] = lax.dot_general(diag, _toeplitz_one_hot(), NT_DIMS, precision=HIGHEST,
                                     preferred_element_type=F32)

    return pl.pallas_call(body, name=name, out_shape=_sds((H, TABLE_PAD), F32),
                          compiler_params=pltpu.CompilerParams(vmem_limit_bytes=VMEM_LIMIT_BYTES))(dbias)


def _head_rows(t):
    lane = lax.broadcasted_iota(jnp.int32, t.shape, 1)
    zero = jnp.zeros_like(t)
    return jnp.concatenate([jnp.where(lane < HEAD_DIM, t, zero), jnp.where(lane >= HEAD_DIM, t, zero)], axis=0)


def _head_lanes(t2):
    lane = lax.broadcasted_iota(jnp.int32, (Q_BLOCK, t2.shape[1]), 1)
    return jnp.where(lane < HEAD_DIM, t2[:Q_BLOCK], t2[Q_BLOCK:])


def _attn_probs(q2, kb, bias2, block):
    s = lax.dot_general(q2, kb, NT_DIMS, preferred_element_type=F32) * ATTN_SCALE + bias2
    shape = (2 * Q_BLOCK, BAND)
    q_chunk = (lax.broadcasted_iota(jnp.int32, shape, 0) % Q_BLOCK) // CHUNK
    kj = lax.broadcasted_iota(jnp.int32, shape, 1)
    k_chunk = kj // CHUNK
    valid = (k_chunk >= q_chunk) & (k_chunk <= q_chunk + N_LEFT) & (block * Q_BLOCK + kj - KV_PAD >= 0)
    s = jnp.where(valid, s, NEG_INF)
    e = jnp.exp(s - jnp.max(s, axis=-1, keepdims=True))
    return e / jnp.sum(e, axis=-1, keepdims=True)


def _attn_specs(S):
    lanes = HEADS_PER_BLOCK * HEAD_DIM
    rows = S + KV_PAD
    q_spec = pl.BlockSpec((Q_BLOCK, lanes), lambda h, i: (i, h))
    k_spec = pl.BlockSpec((rows, lanes), lambda h, i: (0, h))
    v_spec = pl.BlockSpec((rows, lanes), lambda h, i: (0, N_HEADS // HEADS_PER_BLOCK + h))
    b_spec = pl.BlockSpec((HEADS_PER_BLOCK, Q_BLOCK, BAND), lambda h, i: (h, 0, 0))
    return q_spec, k_spec, v_spec, b_spec


def _attn_fwd(q, kvp, bias, name):
    S, HD = q.shape
    q_spec, k_spec, v_spec, b_spec = _attn_specs(S)

    def body(q_ref, k_ref, v_ref, b_ref, o_ref):
        block = pl.program_id(1)
        band = pl.ds(pl.multiple_of(block * Q_BLOCK, Q_BLOCK), BAND)
        p = _attn_probs(_head_rows(q_ref[...]), k_ref[band, :], b_ref[...].reshape(2 * Q_BLOCK, BAND), block)
        o2 = jnp.dot(p.astype(BF16), v_ref[band, :], preferred_element_type=F32)
        o_ref[...] = _head_lanes(o2).astype(BF16)

    return pl.pallas_call(
        body, name=name, grid=(N_HEADS // HEADS_PER_BLOCK, S // Q_BLOCK),
        in_specs=[q_spec, k_spec, v_spec, b_spec], out_specs=q_spec,
        out_shape=_sds((S, HD), BF16), compiler_params=_cparams(2))(q, kvp, kvp, bias)


def _attn_bwd(q, kvp, bias, do, dkv_prev, name):
    S, HD = q.shape
    lanes = HEADS_PER_BLOCK * HEAD_DIM
    q_spec, k_spec, v_spec, b_spec = _attn_specs(S)
    dkv_spec = pl.BlockSpec((2, S + KV_PAD, lanes), lambda h, i: (0, 0, h))

    def body(q_ref, k_ref, v_ref, b_ref, do_ref, prev_ref, dq_ref, dkv_ref, db_ref):
        block = pl.program_id(1)

        @pl.when(block == 0)
        def _():
            dkv_ref[...] = prev_ref[...]
            db_ref[...] = jnp.zeros_like(db_ref)

        band = pl.ds(pl.multiple_of(block * Q_BLOCK, Q_BLOCK), BAND)
        kb, vb = k_ref[band, :], v_ref[band, :]
        q2, do2 = _head_rows(q_ref[...]), _head_rows(do_ref[...])
        p = _attn_probs(q2, kb, b_ref[...].reshape(2 * Q_BLOCK, BAND), block)
        dp = lax.dot_general(do2, vb, NT_DIMS, preferred_element_type=F32)
        ds = p * (dp - jnp.sum(dp * p, axis=-1, keepdims=True))
        db_ref[...] += ds.reshape(HEADS_PER_BLOCK, Q_BLOCK, BAND)
        ds_b = (ds * ATTN_SCALE).astype(BF16)
        dq_ref[...] = _head_lanes(jnp.dot(ds_b, kb, preferred_element_type=F32)).astype(BF16)
        dkv_ref[0, band, :] += lax.dot_general(ds_b, q2, TN_DIMS, preferred_element_type=F32)
        dkv_ref[1, band, :] += lax.dot_general(p.astype(BF16), do2, TN_DIMS, preferred_element_type=F32)

    return pl.pallas_call(
        body, name=name, grid=(N_HEADS // HEADS_PER_BLOCK, S // Q_BLOCK),
        in_specs=[q_spec, k_spec, v_spec, b_spec, q_spec, dkv_spec], out_specs=[q_spec, dkv_spec, b_spec],
        out_shape=[_sds((S, HD), BF16), _sds((2, S + KV_PAD, HD), F32), _sds((N_HEADS, Q_BLOCK, BAND), F32)],
        compiler_params=_cparams(2))(q, kvp, kvp, bias, do, dkv_prev)


def _loss_head(x, g, target, name, tm=512):
    S, D = x.shape

    def body(x_ref, g_ref, t_ref, loss_ref, dx_ref, dg_ref):
        @pl.when(pl.program_id(0) == 0)
        def _():
            loss_ref[...] = jnp.zeros_like(loss_ref)
            dg_ref[...] = jnp.zeros_like(dg_ref)

        xhat, r = _rms_hat(x_ref[...])
        gv = g_ref[...]
        err = xhat * gv - t_ref[...]
        loss_ref[...] += 0.5 * jnp.sum(jnp.mean(err * err, axis=-1, keepdims=True))
        dx, dgp = _rms_bwd(xhat, r, gv, err * (1.0 / D))
        dx_ref[...] = dx
        dg_ref[...] += jnp.sum(dgp, axis=0, keepdims=True)

    row = pl.BlockSpec((tm, D), lambda i: (i, 0))
    vec = pl.BlockSpec((1, D), lambda i: (0, 0))
    return pl.pallas_call(
        body, name=name, grid=(S // tm,), in_specs=[row, vec, row],
        out_specs=[pl.BlockSpec((8, 128), lambda i: (0, 0)), row, vec],
        out_shape=[_sds((8, 128), F32), _sds((S, D), F32), _sds((1, D), F32)],
        compiler_params=_cparams(1))(x, g, target)


def _place():
    x, y, c = lax.axis_index("x"), lax.axis_index("y"), lax.axis_index("c")
    chips = [(1 - x, y), (x, 1 - y), (1 - x, 1 - y)]
    return x, y, c, chips


def _half_rows(c, r):
    return pl.ds(pl.multiple_of(c * (r // 2), 8), r // 2)


HBM_SPEC = pl.BlockSpec(memory_space=pl.ANY)


def _allgather_weights(slabs, small, name):
    n = len(slabs)

    def body(*refs):
        sm, out, osm = refs[n], refs[n + 1:2 * n + 1], refs[2 * n + 1]
        send1, recv1, send2, recv2, send_s, recv_s, local = refs[2 * n + 2:]
        x, y, c, chips = _place()
        q = 2 * x + y
        me, sibling = (x, y, c), (x, y, 1 - c)
        own = [pltpu.make_async_copy(sm, osm.at[q], local)]
        for cp in own:
            cp.start()

        def piece(w, chip, core):
            return out[w].at[2 * chip[0] + chip[1], :, _half_rows(core, out[w].shape[2]), :]

        sends = []
        for w in range(n):
            for j, chip in enumerate(chips):
                sends.append(pltpu.make_async_remote_copy(
                    src_ref=piece(w, (x, y), c), dst_ref=piece(w, (x, y), c),
                    send_sem=send1.at[w, j], recv_sem=recv1.at[w, j], device_id=(*chip, c), device_id_type=MESH))
        for j, chip in enumerate(chips):
            sends.append(pltpu.make_async_remote_copy(
                src_ref=sm, dst_ref=osm.at[q], send_sem=send_s.at[j], recv_sem=recv_s.at[j],
                device_id=(*chip, c), device_id_type=MESH))
        for cp in sends:
            cp.start()
        for w in range(n):
            for j, chip in enumerate(chips):
                landed = piece(w, chip, c)
                pltpu.make_async_remote_copy(src_ref=landed, dst_ref=landed, send_sem=send1.at[w, j],
                                             recv_sem=recv1.at[w, j], device_id=me, device_id_type=MESH).wait_recv()
                fwd = pltpu.make_async_remote_copy(src_ref=landed, dst_ref=landed, send_sem=send2.at[w, j],
                                                   recv_sem=recv2.at[w, j], device_id=sibling, device_id_type=MESH)
                fwd.start()
                sends.append(fwd)
        for j, chip in enumerate(chips):
            got = osm.at[2 * chip[0] + chip[1]]
            pltpu.make_async_remote_copy(src_ref=got, dst_ref=got, send_sem=send_s.at[j], recv_sem=recv_s.at[j],
                                         device_id=me, device_id_type=MESH).wait_recv()
        for w in range(n):
            for j, chip in enumerate(chips):
                got = piece(w, chip, 1 - c)
                pltpu.make_async_remote_copy(src_ref=got, dst_ref=got, send_sem=send2.at[w, j],
                                             recv_sem=recv2.at[w, j], device_id=me, device_id_type=MESH).wait_recv()
        for cp in sends:
            cp.wait_send()
        for cp in own:
            cp.wait()

    pair = pltpu.SemaphoreType.DMA((n, 3))
    return pl.pallas_call(
        body, name=name, in_specs=[HBM_SPEC] * (n + 1), out_specs=[HBM_SPEC] * (n + 1),
        out_shape=[_sds(s.shape, s.dtype) for s in slabs] + [_sds((N_CHIPS, *small.shape), small.dtype)],
        input_output_aliases={w: w for w in range(n)},
        scratch_shapes=[pair, pair, pair, pair, pltpu.SemaphoreType.DMA((3,)), pltpu.SemaphoreType.DMA((3,)),
                        pltpu.SemaphoreType.DMA],
    )(*slabs, small)


def _sibling_exchange(grads, name):
    n = len(grads)

    def body(*refs):
        g, got, send, recv = refs[:n], refs[n:2 * n], refs[2 * n], refs[2 * n + 1]
        x, y, c, _ = _place()
        cps = [pltpu.make_async_remote_copy(
            src_ref=g[w].at[:, :, _half_rows(1 - c, g[w].shape[2]), :], dst_ref=got[w],
            send_sem=send.at[w], recv_sem=recv.at[w], device_id=(x, y, 1 - c), device_id_type=MESH) for w in range(n)]
        for cp in cps:
            cp.start()
        for cp in cps:
            cp.wait()

    return pl.pallas_call(
        body, name=name, in_specs=[HBM_SPEC] * n, out_specs=[HBM_SPEC] * n,
        out_shape=[_sds((a.shape[0], a.shape[1], a.shape[2] // 2, a.shape[3]), F32) for a in grads],
        scratch_shapes=[pltpu.SemaphoreType.DMA((n,)), pltpu.SemaphoreType.DMA((n,))])(*grads)


def _sibling_sum(grad, got, core, name):
    n_sh, L, r2, C = got.shape
    tr = _row_tile(r2, C, 4, 2 * 1024 * 1024, 16)
    nb = r2 // tr

    def body(core_ref, g_ref, o_ref_in, o_ref):
        del core_ref
        o_ref[...] = (g_ref[...] + o_ref_in[...]).astype(BF16)

    grid_spec = pltpu.PrefetchScalarGridSpec(
        num_scalar_prefetch=1, grid=(n_sh, L, nb),
        in_specs=[pl.BlockSpec((None, None, tr, C), lambda s, l, i, core_ref: (s, l, core_ref[0] * nb + i, 0)),
                  pl.BlockSpec((None, None, tr, C), lambda s, l, i, core_ref: (s, l, i, 0))],
        out_specs=pl.BlockSpec((None, None, tr, C), lambda s, l, i, core_ref: (s, l, i, 0)))
    return pl.pallas_call(body, name=name, grid_spec=grid_spec, out_shape=_sds(got.shape, BF16),
                          compiler_params=_cparams(3))(core, grad, got)


def _chip_exchange(parts, name):
    n = len(parts)

    def body(*refs):
        p, got, send, recv = refs[:n], refs[n:2 * n], refs[2 * n], refs[2 * n + 1]
        x, y, c, chips = _place()
        q = 2 * x + y
        cps = []
        for w in range(n):
            for j, chip in enumerate(chips):
                cps.append(pltpu.make_async_remote_copy(
                    src_ref=p[w].at[2 * chip[0] + chip[1]], dst_ref=got[w].at[q],
                    send_sem=send.at[w, j], recv_sem=recv.at[w, j], device_id=(*chip, c), device_id_type=MESH))
        for cp in cps:
            cp.start()
        for w in range(n):
            for j, chip in enumerate(chips):
                landed = got[w].at[2 * chip[0] + chip[1]]
                pltpu.make_async_remote_copy(src_ref=landed, dst_ref=landed, send_sem=send.at[w, j],
                                             recv_sem=recv.at[w, j], device_id=(x, y, c), device_id_type=MESH).wait_recv()
        for cp in cps:
            cp.wait_send()

    pair = pltpu.SemaphoreType.DMA((n, 3))
    return pl.pallas_call(
        body, name=name, in_specs=[HBM_SPEC] * n, out_specs=[HBM_SPEC] * n,
        out_shape=[_sds(a.shape, a.dtype) for a in parts], scratch_shapes=[pair, pair])(*parts)


def _chip_sum(part, got, place, name):
    _, L, r2, C = got.shape
    tr = _row_tile(r2, C, 4, 2 * 1024 * 1024, 16)
    nb = r2 // tr

    def body(place_ref, own_ref, a_ref, b_ref, c_ref, o_ref):
        del place_ref
        o_ref[...] = ((own_ref[...].astype(F32) + a_ref[...].astype(F32)) + b_ref[...].astype(F32)) + c_ref[...].astype(F32)

    def from_chip(k):
        return pl.BlockSpec((None, None, tr, C), lambda l, i, place_ref: ((place_ref[0] + k) % N_CHIPS, l, i, 0))

    grid_spec = pltpu.PrefetchScalarGridSpec(
        num_scalar_prefetch=1, grid=(L, nb), in_specs=[from_chip(k) for k in range(N_CHIPS)],
        out_specs=pl.BlockSpec((None, tr, C), lambda l, i, place_ref: (l, place_ref[1] * nb + i, 0)))
    return pl.pallas_call(body, name=name, grid_spec=grid_spec, out_shape=_sds((L, 2 * r2, C), F32),
                          compiler_params=_cparams(2))(place, part, got, got, got)


def _sibling_join(halves, name):
    n = len(halves)

    def body(*refs):
        out, send, recv = refs[n:2 * n], refs[2 * n], refs[2 * n + 1]
        x, y, c, _ = _place()
        cps = []
        for w in range(n):
            mine = out[w].at[:, _half_rows(c, out[w].shape[1]), :]
            cps.append(pltpu.make_async_remote_copy(src_ref=mine, dst_ref=mine, send_sem=send.at[w],
                                                    recv_sem=recv.at[w], device_id=(x, y, 1 - c), device_id_type=MESH))
        for cp in cps:
            cp.start()
        for w in range(n):
            theirs = out[w].at[:, _half_rows(1 - c, out[w].shape[1]), :]
            pltpu.make_async_remote_copy(src_ref=theirs, dst_ref=theirs, send_sem=send.at[w], recv_sem=recv.at[w],
                                         device_id=(x, y, c), device_id_type=MESH).wait_recv()
        for cp in cps:
            cp.wait_send()

    return pl.pallas_call(
        body, name=name, in_specs=[HBM_SPEC] * n, out_specs=[HBM_SPEC] * n,
        out_shape=[_sds(a.shape, F32) for a in halves], input_output_aliases={w: w for w in range(n)},
        scratch_shapes=[pltpu.SemaphoreType.DMA((n,)), pltpu.SemaphoreType.DMA((n,))])(*halves)


def _gather_small(packed, name):
    def body(p_ref, out, send, recv, local):
        x, y, c, _ = _place()
        me = 4 * x + 2 * y + c
        own = pltpu.make_async_copy(p_ref, out.at[me], local)
        own.start()
        cps = []
        for k in range(1, N_DEV):
            fx, fy, fc = (k >> 2) & 1, (k >> 1) & 1, k & 1
            peer = ((x + fx) % 2, (y + fy) % 2, (c + fc) % 2)
            cps.append(pltpu.make_async_remote_copy(src_ref=p_ref, dst_ref=out.at[me], send_sem=send.at[k - 1],
                                                    recv_sem=recv.at[k - 1], device_id=peer, device_id_type=MESH))
        for cp in cps:
            cp.start()
        for k in range(1, N_DEV):
            fx, fy, fc = (k >> 2) & 1, (k >> 1) & 1, k & 1
            src = out.at[4 * ((x + fx) % 2) + 2 * ((y + fy) % 2) + (c + fc) % 2]
            pltpu.make_async_remote_copy(src_ref=src, dst_ref=src, send_sem=send.at[k - 1], recv_sem=recv.at[k - 1],
                                         device_id=(x, y, c), device_id_type=MESH).wait_recv()
        for cp in cps:
            cp.wait_send()
        own.wait()

    return pl.pallas_call(
        body, name=name, in_specs=[pl.BlockSpec(memory_space=pltpu.VMEM)], out_specs=HBM_SPEC,
        out_shape=_sds((N_DEV, *packed.shape), F32),
        scratch_shapes=[pltpu.SemaphoreType.DMA((N_DEV - 1,)), pltpu.SemaphoreType.DMA((N_DEV - 1,)),
                        pltpu.SemaphoreType.DMA])(packed)


def _sum_devices(gathered, name):
    _, R, C = gathered.shape

    def body(g_ref, o_ref):
        acc = g_ref[0]
        for d in range(1, N_DEV):
            acc = acc + g_ref[d]
        o_ref[...] = acc

    tr = 8
    return pl.pallas_call(
        body, name=name, grid=(R // tr,), in_specs=[pl.BlockSpec((N_DEV, tr, C), lambda i: (0, i, 0))],
        out_specs=pl.BlockSpec((tr, C), lambda i: (i, 0)), out_shape=_sds((R, C), F32),
        compiler_params=_cparams(1))(gathered)


def _pack_small(arrays):
    rows = []
    for a in arrays:
        flat = a.reshape(-1)
        pad = (-flat.shape[0]) % SMALL_COLS
        rows.append(jnp.pad(flat, (0, pad)).reshape(-1, SMALL_COLS))
    packed = jnp.concatenate(rows, axis=0)
    return jnp.pad(packed, ((0, (-packed.shape[0]) % 8), (0, 0)))


def _unpack_small(packed, shapes):
    out, row = [], 0
    for shape in shapes:
        size = math.prod(shape)
        n_rows = -(-size // SMALL_COLS)
        out.append(packed[row:row + n_rows].reshape(-1)[:size].reshape(shape))
        row += n_rows
    return out


def kernel(x, a_norm, a_w_in, a_sgu_norm, a_w_spatial, a_b_spatial, a_w_out, kv_norm, w_kv, b_norm, b_w_q, b_rel_bias, b_w_o, ffn_norm, ffn_w_gate_up, ffn_w_down, final_norm, loss_target, m_a_norm, m_a_w_in, m_a_sgu_norm, m_a_w_spatial, m_a_b_spatial, m_a_w_out, m_kv_norm, m_w_kv, m_b_norm, m_b_w_q, m_b_rel_bias, m_b_w_o, m_ffn_norm, m_ffn_w_gate_up, m_ffn_w_down, m_final_norm, v_a_norm, v_a_w_in, v_a_sgu_norm, v_a_w_spatial, v_a_b_spatial, v_a_w_out, v_kv_norm, v_w_kv, v_b_norm, v_b_w_q, v_b_rel_bias, v_b_w_o, v_ffn_norm, v_ffn_w_gate_up, v_ffn_w_down, v_final_norm):
    S, D = x.shape[1], x.shape[2]
    n_a = a_w_in.shape[0]
    n_b = b_w_q.shape[0]
    depth = ffn_w_gate_up.shape[0]
    xi, yi, ci = lax.axis_index("x"), lax.axis_index("y"), lax.axis_index("c")
    chip = 2 * xi + yi

    big = {"a_w_in": a_w_in, "a_w_out": a_w_out, "w_kv": w_kv[None], "b_w_q": b_w_q, "b_w_o": b_w_o,
           "ffn_w_gate_up": ffn_w_gate_up, "ffn_w_down": ffn_w_down}
    names = list(big)
    place = jnp.stack([chip, ci]).astype(jnp.int32)
    slabs = [_cast_bf16(big[k], place[:1], "cast_" + k) for k in names]
    na_w, ns_w = a_norm.shape[1], a_sgu_norm.shape[1]
    *gathered, small_g = _allgather_weights(slabs, jnp.concatenate([a_norm, a_sgu_norm], axis=1), "allgather_weights")
    W = dict(zip(names, gathered))
    a_norm_f = small_g[:, :, :na_w].transpose(1, 0, 2).reshape(n_a, N_CHIPS * na_w)
    a_sgu_f = small_g[:, :, na_w:].transpose(1, 0, 2).reshape(n_a, N_CHIPS * ns_w)

    xc = x.reshape(S, D)
    saved = []
    kvp = x_kv = h_kv = None
    for layer in range(depth):
        rec = {"x_in": xc}
        if layer < n_a:
            i = layer
            rec["zpre"], rec["h"] = _norm_matmul(xc, a_norm_f[i][None], W["a_w_in"], i, F32, f"a{i}_in")
            rec["uv"] = _sgu_fwd(rec["zpre"], a_sgu_f[i][None], a_w_spatial[i], a_b_spatial[i].T, f"a{i}_sgu")
            xm = _matmul_res(rec["uv"], W["a_w_out"], i, xc, f"a{i}_out")
        else:
            i = layer - n_a
            if i == 0:
                kv, h_kv = _norm_matmul(xc, kv_norm[None], W["w_kv"], 0, BF16, "kv_proj")
                kvp = jnp.pad(kv, ((KV_PAD, 0), (0, 0)))
                x_kv = xc
            rec["q"], rec["h"] = _norm_matmul(xc, b_norm[i][None], W["b_w_q"], i, BF16, f"b{i}_q", row_sharded=True)
            table = jnp.pad(b_rel_bias[i], ((0, 0), (0, TABLE_PAD - b_rel_bias.shape[2])))
            rec["bias"] = _rel_bias_fwd(table, f"b{i}_bias").transpose(1, 0, 2)
            rec["o"] = _attn_fwd(rec["q"], kvp, rec["bias"], f"b{i}_attn")
            xm = _matmul_res(rec["o"], W["b_w_o"], i, xc, f"b{i}_o")
        rec["x_mid"] = xm
        rec["gu"], rec["h_f"] = _norm_matmul(xm, ffn_norm[layer][None], W["ffn_w_gate_up"], layer, F32, f"f{layer}_in")
        xc = _matmul_res(rec["gu"], W["ffn_w_down"], layer, xm, f"f{layer}_out", swiglu=True)
        saved.append(rec)

    loss_tile, dx, d_final = _loss_head(xc, final_norm[None], loss_target.reshape(S, D), "loss_head")
    loss = lax.psum(loss_tile[0, 0], ("x", "y", "c"))

    G = {}

    def weight_grad(key, layer, **kw):
        full = (N_CHIPS,) + tuple(big[key].shape)
        G[key] = _matmul_tn(out_shape=full, gbuf=G.get(key), name=f"d_{key}_{layer}", acc_shape=None, **kw)

    row_a = lambda w: pl.BlockSpec((512, w), lambda o, t: (t, 0))
    d_ffn_norm, d_b_norm, d_a_norm, d_a_sgu = [None] * depth, [None] * n_b, [None] * n_a, [None] * n_a
    d_ws, d_bs, d_rel = [None] * n_a, [None] * n_a, [None] * n_b
    dkv = jnp.zeros((2, S + KV_PAD, D), F32)
    first = lambda ref: ref[...]
    for layer in reversed(range(depth)):
        rec = saved[layer]
        F = ffn_w_down.shape[1] * N_CHIPS
        half_f = F // 2
        r_d = ffn_w_down.shape[1]
        weight_grad(
            "ffn_w_down", layer, a_ops=[rec["gu"], rec["gu"]],
            a_specs=[pl.BlockSpec((512, half_f), lambda o, t: (t, o)), pl.BlockSpec((512, half_f), lambda o, t: (t, 2 + o))],
            a_fn=lambda g_ref, u_ref: _swiglu(g_ref[...], u_ref[...]).astype(BF16),
            b_op=dx, b_spec=row_a(D),
            out_spec=pl.BlockSpec((2, None, r_d, D), lambda o, t, layer=layer: (o, layer, 0, 0)), n_outer=2)
        dgu = _nt_swiglu_bwd(dx, W["ffn_w_down"], layer, rec["gu"], f"f{layer}_dgu")
        nsh = ffn_w_gate_up.shape[2]
        weight_grad(
            "ffn_w_gate_up", layer, a_ops=[rec["h_f"]], a_specs=[row_a(D)], a_fn=first,
            b_op=dgu, b_spec=pl.BlockSpec((None, 512, nsh), lambda o, t: (o // 2, t, o % 2)),
            out_spec=pl.BlockSpec((None, None, D, nsh), lambda o, t, layer=layer: (o, layer, 0, 0)), n_outer=N_CHIPS)
        dx, d_ffn_norm[layer] = _nt_normbwd(
            dgu, pl.BlockSpec((None, 512, nsh), lambda i, k: (k // 2, i, k % 2)),
            W["ffn_w_gate_up"], pl.BlockSpec((None, None, D, nsh), lambda i, k, layer=layer: (k, layer, 0, 0)),
            (D, nsh), N_CHIPS, rec["x_mid"], ffn_norm[layer][None], dx, f"f{layer}_dx")
        if layer >= n_a:
            i = layer - n_a
            r_o = b_w_o.shape[1]
            weight_grad(
                "b_w_o", i, a_ops=[rec["o"]], a_specs=[row_a(D)], a_fn=first,
                b_op=dx, b_spec=pl.BlockSpec((512, 512), lambda o, t: (t, o)),
                out_spec=pl.BlockSpec((N_CHIPS, None, r_o, 512), lambda o, t, i=i: (0, i, 0, o)), n_outer=2)
            do = _nt_rows(dx, W["b_w_o"], i, 2, BF16, f"b{i}_do")
            dq, dkv, dbias = _attn_bwd(rec["q"], kvp, rec["bias"], do, dkv, f"b{i}_attn_bwd")
            d_rel[i] = _rel_bias_bwd(dbias.transpose(1, 0, 2), f"b{i}_dbias")[:, :b_rel_bias.shape[2]]
            weight_grad(
                "b_w_q", i, a_ops=[rec["h"]], a_specs=[row_a(D)], a_fn=first,
                b_op=dq, b_spec=pl.BlockSpec((512, 512), lambda o, t: (t, o)),
                out_spec=pl.BlockSpec((N_CHIPS, None, r_o, 512), lambda o, t, i=i: (0, i, 0, o)), n_outer=2)
            dx, d_b_norm[i] = _nt_normbwd(
                dq, pl.BlockSpec((512, D), lambda i_, k: (i_, 0)),
                W["b_w_q"], pl.BlockSpec((N_CHIPS, None, r_o, D), lambda i_, k, i=i: (0, i, 0, 0)),
                (D, D), 1, rec["x_in"], b_norm[i][None], dx, f"b{i}_dx")
            if i == 0:
                dkv_b = dkv[:, KV_PAD:, :].astype(BF16)
                n_kv = w_kv.shape[1]
                weight_grad(
                    "w_kv", 0, a_ops=[h_kv], a_specs=[row_a(D)], a_fn=first,
                    b_op=dkv_b, b_spec=pl.BlockSpec((None, 512, n_kv), lambda o, t: (o // 2, t, o % 2)),
                    out_spec=pl.BlockSpec((None, None, D, n_kv), lambda o, t: (o, 0, 0, 0)), n_outer=N_CHIPS)
                dx, d_kv_norm = _nt_normbwd(
                    dkv_b, pl.BlockSpec((None, 512, n_kv), lambda i_, k: (k // 2, i_, k % 2)),
                    W["w_kv"], pl.BlockSpec((None, None, D, n_kv), lambda i_, k: (k, 0, 0, 0)),
                    (D, n_kv), N_CHIPS, x_kv, kv_norm[None], dx, "kv_dx")
        else:
            i = layer
            r_w = a_w_out.shape[1]
            weight_grad(
                "a_w_out", i, a_ops=[rec["uv"]], a_specs=[pl.BlockSpec((512, r_w), lambda o, t: (t, o))], a_fn=first,
                b_op=dx, b_spec=row_a(D),
                out_spec=pl.BlockSpec((None, None, r_w, D), lambda o, t, i=i: (o, i, 0, 0)), n_outer=N_CHIPS)
            duv = _nt_rows(dx, W["a_w_out"], i, 1, F32, f"a{i}_duv")
            dz, d_a_sgu[i], d_ws[i], dbs = _sgu_bwd(rec["zpre"], duv, a_sgu_f[i][None], a_w_spatial[i],
                                                  a_w_spatial[i].transpose(0, 2, 1), a_b_spatial[i].T, f"a{i}_sgu_bwd")
            d_bs[i] = dbs[:, :, 0]
            n_in = a_w_in.shape[2]
            weight_grad(
                "a_w_in", i, a_ops=[rec["h"]], a_specs=[row_a(D)], a_fn=first,
                b_op=dz, b_spec=pl.BlockSpec((512, 512), lambda o, t: (t, o)),
                out_spec=pl.BlockSpec((None, None, D, 512), lambda o, t, i=i: (o // 2, i, 0, o % 2)),
                n_outer=N_CHIPS * n_in // 512)
            dx, d_a_norm[i] = _nt_normbwd(
                dz, pl.BlockSpec((512, 512), lambda i_, k: (i_, k)),
                W["a_w_in"], pl.BlockSpec((None, None, D, 512), lambda i_, k, i=i: (k // 2, i, 0, k % 2)),
                (D, 512), N_CHIPS * n_in // 512, rec["x_in"], a_norm_f[i][None], dx, f"a{i}_dx")
    grad_x = dx.reshape(x.shape)

    grads = [G[k] for k in names]
    theirs = _sibling_exchange(grads, "rs_sibling_exchange")
    parts = [_sibling_sum(g, t, place[1:], f"rs_sibling_sum_{k}") for g, t, k in zip(grads, theirs, names)]
    from_chips = _chip_exchange(parts, "rs_chip_exchange")
    halves = [_chip_sum(p, f, place, f"rs_chip_sum_{k}") for p, f, k in zip(parts, from_chips, names)]
    reduced = dict(zip(names, _sibling_join(halves, "rs_sibling_join")))
    reduced["w_kv"] = reduced["w_kv"][0]

    small = [jnp.concatenate(d_a_norm, axis=0), jnp.concatenate(d_a_sgu, axis=0), jnp.stack(d_ws), jnp.stack(d_bs),
             d_kv_norm, jnp.concatenate(d_b_norm, axis=0), jnp.stack(d_rel), jnp.concatenate(d_ffn_norm, axis=0), d_final]
    total = _sum_devices(_gather_small(_pack_small(small), "gather_small_grads"), "sum_small_grads")
    (g_a_norm, g_a_sgu, g_ws, g_bs, g_kv_norm, g_b_norm, g_rel, g_ffn_norm, g_final) = _unpack_small(
        total, [a.shape for a in small])
    reduced["a_norm"] = lax.dynamic_slice_in_dim(g_a_norm, chip * na_w, na_w, axis=1)
    reduced["a_sgu_norm"] = lax.dynamic_slice_in_dim(g_a_sgu, chip * ns_w, ns_w, axis=1)
    reduced.update(a_w_spatial=g_ws, a_b_spatial=g_bs, kv_norm=g_kv_norm.reshape(kv_norm.shape), b_norm=g_b_norm,
                   b_rel_bias=g_rel, ffn_norm=g_ffn_norm, final_norm=g_final.reshape(final_norm.shape))

    weights = dict(a_norm=a_norm, a_w_in=a_w_in, a_sgu_norm=a_sgu_norm, a_w_spatial=a_w_spatial,
                   a_b_spatial=a_b_spatial, a_w_out=a_w_out, kv_norm=kv_norm, w_kv=w_kv, b_norm=b_norm, b_w_q=b_w_q,
                   b_rel_bias=b_rel_bias, b_w_o=b_w_o, ffn_norm=ffn_norm, ffn_w_gate_up=ffn_w_gate_up,
                   ffn_w_down=ffn_w_down, final_norm=final_norm)
    m_in = dict(a_norm=m_a_norm, a_w_in=m_a_w_in, a_sgu_norm=m_a_sgu_norm, a_w_spatial=m_a_w_spatial,
                a_b_spatial=m_a_b_spatial, a_w_out=m_a_w_out, kv_norm=m_kv_norm, w_kv=m_w_kv, b_norm=m_b_norm,
                b_w_q=m_b_w_q, b_rel_bias=m_b_rel_bias, b_w_o=m_b_w_o, ffn_norm=m_ffn_norm,
                ffn_w_gate_up=m_ffn_w_gate_up, ffn_w_down=m_ffn_w_down, final_norm=m_final_norm)
    v_in = dict(a_norm=v_a_norm, a_w_in=v_a_w_in, a_sgu_norm=v_a_sgu_norm, a_w_spatial=v_a_w_spatial,
                a_b_spatial=v_a_b_spatial, a_w_out=v_a_w_out, kv_norm=v_kv_norm, w_kv=v_w_kv, b_norm=v_b_norm,
                b_w_q=v_b_w_q, b_rel_bias=v_b_rel_bias, b_w_o=v_b_w_o, ffn_norm=v_ffn_norm,
                ffn_w_gate_up=v_ffn_w_gate_up, ffn_w_down=v_ffn_w_down, final_norm=v_final_norm)
    grad_out, delta_out, m_out, v_out = [], [], [], []
    for key, w in weights.items():
        g = reduced[key].reshape(w.shape)
        view = (1, w.shape[0]) if w.ndim == 1 else (-1, w.shape[-1])
        d, nm, nv = _adamw(w.reshape(view), g.reshape(view), m_in[key].reshape(view), v_in[key].reshape(view),
                           "adamw_" + key)
        grad_out.append(g)
        delta_out.append(d.reshape(w.shape))
        m_out.append(nm.reshape(w.shape))
        v_out.append(nv.reshape(w.shape))
    return (loss, grad_x, *grad_out, *delta_out, *m_out, *v_out)
```

```python
import math

import jax
import jax.numpy as jnp
from jax import lax
from jax.experimental import pallas as pl
from jax.experimental.pallas import tpu as pltpu

F32, BF16 = jnp.float32, jnp.bfloat16
MESH = pl.DeviceIdType.MESH
HIGHEST = lax.Precision.HIGHEST
NT_DIMS = (((1,), (1,)), ((), ()))
TN_DIMS = (((0,), (0,)), ((), ()))

EPS = 1e-6
CHUNK = 64
A_CHUNK = 128
A_GROUPS = 8
N_HEADS = 16
HEAD_DIM = 64
N_LEFT = 8
MAX_REL = 256
ATTN_SCALE = HEAD_DIM ** -0.5
NEG_INF = -1e30
Q_BLOCK = 2 * CHUNK
KV_PAD = N_LEFT * CHUNK
BAND = KV_PAD + Q_BLOCK
DIAGS = BAND + Q_BLOCK
TABLE_PAD = 640
HEADS_PER_BLOCK = 2

ADAM_LR, ADAM_B1, ADAM_B2, ADAM_EPS, ADAM_WD, ADAM_STEP = 0.001, 0.9, 0.999, 1e-08, 0.01, 10

VMEM_LIMIT_BYTES = 56 * 1024 * 1024
N_CHIPS = 4
N_DEV = 8
SMALL_COLS = 1024


def _cparams(n_grid):
    return pltpu.CompilerParams(dimension_semantics=("arbitrary",) * n_grid, vmem_limit_bytes=VMEM_LIMIT_BYTES)


def _sds(shape, dtype):
    return jax.ShapeDtypeStruct(tuple(shape), dtype)


def _gelu(x):
    return 0.5 * x * (1.0 + lax.erf(x * math.sqrt(0.5)))


def _gelu_grad(x):
    return 0.5 * (1.0 + lax.erf(x * math.sqrt(0.5))) + x * (jnp.exp(-0.5 * x * x) * (1.0 / math.sqrt(2.0 * math.pi)))


def _rms_hat(xv):
    r = lax.rsqrt(jnp.mean(xv * xv, axis=-1, keepdims=True) + EPS)
    return xv * r, r


def _rms_bwd(xhat, r, g, dy):
    dxhat = dy * g
    dx = r * (dxhat - xhat * jnp.mean(dxhat * xhat, axis=-1, keepdims=True))
    return dx, dy * xhat


def _swiglu(gate, up):
    return (gate * jax.nn.sigmoid(gate)) * up


def _row_tile(rows, cols, itemsize, cap_bytes, align):
    t = rows
    while t * cols * itemsize > cap_bytes and t % (2 * align) == 0:
        t //= 2
    return t


def _cast_slab(w, layer, chip, name):
    _, r, C = w.shape
    tr = _row_tile(r, C, 4, 4 * 1024 * 1024, 16)

    def body(chip_ref, w_ref, o_ref):
        del chip_ref
        o_ref[...] = w_ref[...].astype(BF16)

    grid_spec = pltpu.PrefetchScalarGridSpec(
        num_scalar_prefetch=1, grid=(r // tr,),
        in_specs=[pl.BlockSpec((None, tr, C), lambda i, chip_ref: (layer, i, 0))],
        out_specs=pl.BlockSpec((None, tr, C), lambda i, chip_ref: (chip_ref[0], i, 0)))
    return pl.pallas_call(body, name=name, grid_spec=grid_spec, out_shape=_sds((N_CHIPS, r, C), BF16),
                          compiler_params=_cparams(1))(chip, w)


def _adamw_layer(w, g, m, v, layer, bufs, name):
    L, r, C = w.shape
    tr = _row_tile(r, C, 4, 1024 * 1024, 8)

    def body(w_ref, g_ref, m_ref, v_ref, *rest):
        go_ref, d_ref, nm_ref, nv_ref = rest[-4:]
        gv = g_ref[...]
        mn = ADAM_B1 * m_ref[...] + (1.0 - ADAM_B1) * gv
        vn = ADAM_B2 * v_ref[...] + (1.0 - ADAM_B2) * jnp.square(gv)
        m_hat = mn / (1.0 - ADAM_B1 ** ADAM_STEP)
        v_hat = vn / (1.0 - ADAM_B2 ** ADAM_STEP)
        d_ref[...] = -ADAM_LR * (m_hat / (jnp.sqrt(v_hat) + ADAM_EPS) + ADAM_WD * w_ref[...])
        nm_ref[...] = mn
        nv_ref[...] = vn
        go_ref[...] = gv

    stacked = pl.BlockSpec((None, tr, C), lambda i: (layer, i, 0))
    in_specs = [stacked, pl.BlockSpec((tr, C), lambda i: (i, 0)), stacked, stacked]
    ops = [w, g, m, v]
    aliases = {}
    if bufs is not None:
        in_specs += [HBM_SPEC] * 4
        ops += list(bufs)
        aliases = {4 + k: k for k in range(4)}
    return pl.pallas_call(body, name=name, grid=(r // tr,), in_specs=in_specs, out_specs=[stacked] * 4,
                          out_shape=[_sds((L, r, C), F32)] * 4, input_output_aliases=aliases,
                          compiler_params=_cparams(1))(*ops)


def _adamw(w, g, m, v, name):
    R, C = w.shape
    tr = _row_tile(R, C, 4, 1024 * 1024, 8)

    def body(w_ref, g_ref, m_ref, v_ref, d_ref, nm_ref, nv_ref):
        gv = g_ref[...]
        mn = ADAM_B1 * m_ref[...] + (1.0 - ADAM_B1) * gv
        vn = ADAM_B2 * v_ref[...] + (1.0 - ADAM_B2) * jnp.square(gv)
        m_hat = mn / (1.0 - ADAM_B1 ** ADAM_STEP)
        v_hat = vn / (1.0 - ADAM_B2 ** ADAM_STEP)
        d_ref[...] = -ADAM_LR * (m_hat / (jnp.sqrt(v_hat) + ADAM_EPS) + ADAM_WD * w_ref[...])
        nm_ref[...] = mn
        nv_ref[...] = vn

    spec = pl.BlockSpec((tr, C), lambda i: (i, 0))
    return pl.pallas_call(body, name=name, grid=(R // tr,), in_specs=[spec] * 4, out_specs=[spec] * 3,
                          out_shape=[_sds((R, C), F32)] * 3, compiler_params=_cparams(1))(w, g, m, v)


def _norm_matmul(x, g, w_g, out_dtype, name, row_sharded=False, tm=1024):
    S, D = x.shape
    tm = min(tm, S)
    if row_sharded:
        r, N = w_g.shape[1], w_g.shape[2]
        tn = 512
        w_spec = pl.BlockSpec((N_CHIPS, r, tn), lambda i, j: (0, 0, j))
    else:
        nsh = w_g.shape[2]
        N = N_CHIPS * nsh
        tn = 512 if nsh % 512 == 0 else nsh
        bps = nsh // tn
        w_spec = pl.BlockSpec((None, D, tn), lambda i, j: (j // bps, 0, j % bps))

    def body(x_ref, g_ref, w_ref, y_ref, h_ref):
        @pl.when(pl.program_id(1) == 0)
        def _():
            xhat, _ = _rms_hat(x_ref[...])
            h_ref[...] = (xhat * g_ref[...]).astype(BF16)

        w = w_ref[...].reshape(D, tn)
        y_ref[...] = jnp.dot(h_ref[...], w, preferred_element_type=F32).astype(y_ref.dtype)

    return pl.pallas_call(
        body, name=name, grid=(S // tm, N // tn),
        in_specs=[pl.BlockSpec((tm, D), lambda i, j: (i, 0)), pl.BlockSpec((1, D), lambda i, j: (0, 0)), w_spec],
        out_specs=[pl.BlockSpec((tm, tn), lambda i, j: (i, j)), pl.BlockSpec((tm, D), lambda i, j: (i, 0))],
        out_shape=[_sds((S, N), out_dtype), _sds((S, D), BF16)],
        compiler_params=_cparams(2))(x, g, w_g)


def _matmul_res(a, w_g, res, name, swiglu=False, tm=256):
    S, N = res.shape
    r = w_g.shape[1]
    K = N_CHIPS * r

    def body(*refs):
        if swiglu:
            gate_ref, up_ref, w_ref, res_ref, o_ref = refs
            a_blk = _swiglu(gate_ref[...], up_ref[...]).astype(BF16)
        else:
            a_ref, w_ref, res_ref, o_ref = refs
            a_blk = a_ref[...]
        o_ref[...] = res_ref[...] + jnp.dot(a_blk, w_ref[...].reshape(K, N), preferred_element_type=F32)

    a_specs, a_ops = [pl.BlockSpec((tm, K), lambda i: (i, 0))], [a]
    if swiglu:
        a_specs.append(pl.BlockSpec((tm, K), lambda i: (i, 1)))
        a_ops.append(a)
    row = pl.BlockSpec((tm, N), lambda i: (i, 0))
    return pl.pallas_call(
        body, name=name, grid=(S // tm,),
        in_specs=a_specs + [pl.BlockSpec((N_CHIPS, r, N), lambda i: (0, 0, 0)), row],
        out_specs=row, out_shape=_sds((S, N), F32), compiler_params=_cparams(1))(*a_ops, w_g, res)


def _matmul_tn(a_ops, a_specs, a_fn, b_op, b_spec, out_spec, out_shape, acc_shape, n_outer, name, tt=512):
    S = b_op.shape[-2]
    na = len(a_ops)
    nt = S // tt

    def body(*refs):
        a_refs, b_ref, o_ref, acc_ref = refs[:na], refs[na], refs[na + 1], refs[na + 2]
        t = pl.program_id(1)
        part = lax.dot_general(a_fn(*a_refs), b_ref[...].astype(BF16), TN_DIMS, preferred_element_type=F32)

        @pl.when(t == 0)
        def _():
            acc_ref[...] = part

        @pl.when(t > 0)
        def _():
            acc_ref[...] += part

        @pl.when(t == nt - 1)
        def _():
            o_ref[...] = acc_ref[...].reshape(o_ref.shape).astype(BF16)

    return pl.pallas_call(
        body, name=name, grid=(n_outer, nt), in_specs=list(a_specs) + [b_spec], out_specs=out_spec,
        out_shape=_sds(out_shape, BF16), scratch_shapes=[pltpu.VMEM(acc_shape, F32)],
        compiler_params=_cparams(2))(*a_ops, b_op)


def _nt_accumulate(a_ref, w_ref, acc_ref, w2d, nk):
    k = pl.program_id(1)
    part = lax.dot_general(a_ref[...].astype(BF16), w_ref[...].reshape(w2d), NT_DIMS, preferred_element_type=F32)

    @pl.when(k == 0)
    def _():
        acc_ref[...] = part

    @pl.when(k > 0)
    def _():
        acc_ref[...] += part

    return k == nk - 1


def _nt_normbwd(dy, dy_spec, w_g, w_spec, w2d, nk, x, g, dres, name, deps=(), tm=512):
    S, D = x.shape

    def body(dy_ref, w_ref, x_ref, g_ref, dres_ref, *rest):
        dx_ref, dg_ref, acc_ref = rest[-3:]

        @pl.when((pl.program_id(0) == 0) & (pl.program_id(1) == 0))
        def _():
            dg_ref[...] = jnp.zeros_like(dg_ref)

        last = _nt_accumulate(dy_ref, w_ref, acc_ref, w2d, nk)

        @pl.when(last)
        def _():
            xhat, r = _rms_hat(x_ref[...])
            dx, dgp = _rms_bwd(xhat, r, g_ref[...], acc_ref[...])
            dx_ref[...] = dres_ref[...] + dx
            dg_ref[...] += jnp.sum(dgp, axis=0, keepdims=True)

    row = pl.BlockSpec((tm, D), lambda i, k: (i, 0))
    vec = pl.BlockSpec((1, D), lambda i, k: (0, 0))
    return pl.pallas_call(
        body, name=name, grid=(S // tm, nk),
        in_specs=[dy_spec, w_spec, row, vec, row] + [HBM_SPEC] * len(deps), out_specs=[row, vec],
        out_shape=[_sds((S, D), F32), _sds((1, D), F32)],
        scratch_shapes=[pltpu.VMEM((tm, D), F32)], compiler_params=_cparams(2))(dy, w_g, x, g, dres, *deps)


def _nt_rows(dy, w_g, shards_per_block, out_dtype, name, deps=(), tm=512):
    S, N = dy.shape
    r = w_g.shape[1]
    tn = shards_per_block * r

    def body(dy_ref, w_ref, *rest):
        o_ref = rest[-1]
        o_ref[...] = lax.dot_general(dy_ref[...].astype(BF16), w_ref[...].reshape(tn, N), NT_DIMS,
                                     preferred_element_type=F32).astype(o_ref.dtype)

    return pl.pallas_call(
        body, name=name, grid=(S // tm, N_CHIPS // shards_per_block),
        in_specs=[pl.BlockSpec((tm, N), lambda i, j: (i, 0)),
                  pl.BlockSpec((shards_per_block, r, N), lambda i, j: (j, 0, 0))] + [HBM_SPEC] * len(deps),
        out_specs=pl.BlockSpec((tm, tn), lambda i, j: (i, j)),
        out_shape=_sds((S, N_CHIPS * r), out_dtype), compiler_params=_cparams(2))(dy, w_g, *deps)


def _nt_swiglu_bwd(dy, w_g, gu, name, deps=(), tm=256):
    S, N = dy.shape
    r = w_g.shape[1]
    tn = 2 * r
    F = N_CHIPS * r

    def body(dy_ref, w_ref, gate_ref, up_ref, *rest):
        o_ref = rest[-1]
        dact = lax.dot_general(dy_ref[...].astype(BF16), w_ref[...].reshape(tn, N), NT_DIMS,
                               preferred_element_type=F32)
        gate, up = gate_ref[...], up_ref[...]
        sg = jax.nn.sigmoid(gate)
        silu = gate * sg
        o_ref[0] = ((dact * up) * (sg + silu * (1.0 - sg))).astype(BF16)
        o_ref[1] = (dact * silu).astype(BF16)

    return pl.pallas_call(
        body, name=name, grid=(S // tm, 2),
        in_specs=[pl.BlockSpec((tm, N), lambda i, j: (i, 0)),
                  pl.BlockSpec((2, r, N), lambda i, j: (j, 0, 0)),
                  pl.BlockSpec((tm, tn), lambda i, j: (i, j)),
                  pl.BlockSpec((tm, tn), lambda i, j: (i, 2 + j))] + [HBM_SPEC] * len(deps),
        out_specs=pl.BlockSpec((2, tm, tn), lambda i, j: (0, i, j)),
        out_shape=_sds((2, S, F), BF16), compiler_params=_cparams(2))(dy, w_g, gu, gu, *deps)


def _chunk_causal_mask(transposed):
    i = lax.broadcasted_iota(jnp.int32, (A_CHUNK, A_CHUNK), 0) // CHUNK
    j = lax.broadcasted_iota(jnp.int32, (A_CHUNK, A_CHUNK), 1) // CHUNK
    return ((i <= j) if transposed else (i >= j)).astype(F32)


def _sgu_fwd(zpre, g_sgu, ws, bs_t, name):
    S, F2 = zpre.shape
    F = F2 // 2
    gd = F // A_GROUPS

    def body(zu_ref, zv_ref, g_ref, ws_ref, b_ref, o_ref):
        vhat, _ = _rms_hat(_gelu(zv_ref[...]))
        vn = (vhat * g_ref[...]).astype(BF16)
        u = _gelu(zu_ref[...])
        mask = _chunk_causal_mask(False)
        for gi in range(A_GROUPS):
            sl = slice(gi * gd, (gi + 1) * gd)
            wm = (ws_ref[gi] * mask).astype(BF16)
            vs = jnp.dot(wm, vn[:, sl], preferred_element_type=F32) + b_ref[:, gi:gi + 1]
            o_ref[:, sl] = (u[:, sl] * vs).astype(BF16)

    return pl.pallas_call(
        body, name=name, grid=(S // A_CHUNK,),
        in_specs=[pl.BlockSpec((A_CHUNK, F), lambda i: (i, 0)),
                  pl.BlockSpec((A_CHUNK, F), lambda i: (i, 1)),
                  pl.BlockSpec((1, F), lambda i: (0, 0)),
                  pl.BlockSpec((A_GROUPS, A_CHUNK, A_CHUNK), lambda i: (0, 0, 0)),
                  pl.BlockSpec((A_CHUNK, A_GROUPS), lambda i: (0, 0))],
        out_specs=pl.BlockSpec((A_CHUNK, F), lambda i: (i, 0)),
        out_shape=_sds((S, F), BF16), compiler_params=_cparams(1))(zpre, zpre, g_sgu, ws, bs_t)


def _sgu_bwd(zpre, duv, g_sgu, ws, ws_t, bs_t, name):
    S, F2 = zpre.shape
    F = F2 // 2
    gd = F // A_GROUPS

    def body(zu_ref, zv_ref, duv_ref, g_ref, ws_ref, wst_ref, b_ref, dz_ref, dg_ref, dws_ref, dbs_ref, dvn_ref):
        @pl.when(pl.program_id(0) == 0)
        def _():
            dg_ref[...] = jnp.zeros_like(dg_ref)
            dws_ref[...] = jnp.zeros_like(dws_ref)
            dbs_ref[...] = jnp.zeros_like(dbs_ref)

        zu, zv = zu_ref[...], zv_ref[...]
        gv = g_ref[...]
        vhat, r = _rms_hat(_gelu(zv))
        vn = (vhat * gv).astype(BF16)
        u = _gelu(zu)
        duv_v = duv_ref[...]
        dvs = duv_v * u
        dvs_b = dvs.astype(BF16)
        mask = _chunk_causal_mask(False)
        mask_t = _chunk_causal_mask(True)
        for gi in range(A_GROUPS):
            sl = slice(gi * gd, (gi + 1) * gd)
            wm = (ws_ref[gi] * mask).astype(BF16)
            vs = jnp.dot(wm, vn[:, sl], preferred_element_type=F32) + b_ref[:, gi:gi + 1]
            dz_ref[:, sl] = ((duv_v[:, sl] * vs) * _gelu_grad(zu[:, sl])).astype(BF16)
            dws_ref[gi] += lax.dot_general(dvs_b[:, sl], vn[:, sl], NT_DIMS, preferred_element_type=F32) * mask
            dbs_ref[gi] += jnp.broadcast_to(jnp.sum(dvs[:, sl], axis=1, keepdims=True), (A_CHUNK, A_CHUNK))
            wm_t = (wst_ref[gi] * mask_t).astype(BF16)
            dvn_ref[:, sl] = jnp.dot(wm_t, dvs_b[:, sl], preferred_element_type=F32)
        dv, dg_part = _rms_bwd(vhat, r, gv, dvn_ref[...])
        dg_ref[...] += jnp.sum(dg_part, axis=0, keepdims=True)
        dz_ref[:, F:] = (dv * _gelu_grad(zv)).astype(BF16)

    blk = pl.BlockSpec((A_CHUNK, F), lambda i: (i, 0))
    const3 = pl.BlockSpec((A_GROUPS, A_CHUNK, A_CHUNK), lambda i: (0, 0, 0))
    return pl.pallas_call(
        body, name=name, grid=(S // A_CHUNK,),
        in_specs=[blk, pl.BlockSpec((A_CHUNK, F), lambda i: (i, 1)), blk,
                  pl.BlockSpec((1, F), lambda i: (0, 0)), const3, const3,
                  pl.BlockSpec((A_CHUNK, A_GROUPS), lambda i: (0, 0))],
        out_specs=[pl.BlockSpec((A_CHUNK, F2), lambda i: (i, 0)), pl.BlockSpec((1, F), lambda i: (0, 0)),
                   const3, const3],
        out_shape=[_sds((S, F2), BF16), _sds((1, F), F32), _sds((A_GROUPS, A_CHUNK, A_CHUNK), F32),
                   _sds((A_GROUPS, A_CHUNK, A_CHUNK), F32)],
        scratch_shapes=[pltpu.VMEM((A_CHUNK, F), F32)],
        compiler_params=_cparams(1))(zpre, zpre, duv, g_sgu, ws, ws_t, bs_t)


def _toeplitz_one_hot():
    row = lax.broadcasted_iota(jnp.int32, (TABLE_PAD, DIAGS), 0)
    j = lax.broadcasted_iota(jnp.int32, (TABLE_PAD, DIAGS), 1)
    idx = jnp.clip(KV_PAD + Q_BLOCK - j, -MAX_REL, MAX_REL) + MAX_REL
    return (row == idx).astype(F32)


def _rel_bias_fwd(table, name):
    H = table.shape[0]

    def body(t_ref, o_ref):
        diag = jnp.dot(t_ref[...], _toeplitz_one_hot(), precision=HIGHEST, preferred_element_type=F32)

        def step(r, carry):
            o_ref[r] = pltpu.roll(diag, DIAGS - Q_BLOCK + r, 1)[:, :BAND]
            return carry

        lax.fori_loop(0, Q_BLOCK, step, 0)

    return pl.pallas_call(body, name=name, out_shape=_sds((Q_BLOCK, H, BAND), F32),
                          compiler_params=pltpu.CompilerParams(vmem_limit_bytes=VMEM_LIMIT_BYTES))(table)


def _rel_bias_bwd(dbias, name):
    H = dbias.shape[1]

    def body(d_ref, o_ref):
        def step(r, acc):
            row = jnp.concatenate([d_ref[r], jnp.zeros((H, DIAGS - BAND), F32)], axis=1)
            return acc + pltpu.roll(row, Q_BLOCK - r, 1)

        diag = lax.fori_loop(0, Q_BLOCK, step, jnp.zeros((H, DIAGS), F32))
        o_ref[...] = lax.dot_general(diag, _toeplitz_one_hot(), NT_DIMS, precision=HIGHEST,
                                     preferred_element_type=F32)

    return pl.pallas_call(body, name=name, out_shape=_sds((H, TABLE_PAD), F32),
                          compiler_params=pltpu.CompilerParams(vmem_limit_bytes=VMEM_LIMIT_BYTES))(dbias)


def _head_rows(t):
    lane = lax.broadcasted_iota(jnp.int32, t.shape, 1)
    zero = jnp.zeros_like(t)
    return jnp.concatenate([jnp.where(lane < HEAD_DIM, t, zero), jnp.where(lane >= HEAD_DIM, t, zero)], axis=0)


def _head_lanes(t2):
    lane = lax.broadcasted_iota(jnp.int32, (Q_BLOCK, t2.shape[1]), 1)
    return jnp.where(lane < HEAD_DIM, t2[:Q_BLOCK], t2[Q_BLOCK:])


def _attn_probs(q2, kb, bias2, block):
    s = lax.dot_general(q2, kb, NT_DIMS, preferred_element_type=F32) * ATTN_SCALE + bias2
    shape = (2 * Q_BLOCK, BAND)
    q_chunk = (lax.broadcasted_iota(jnp.int32, shape, 0) % Q_BLOCK) // CHUNK
    kj = lax.broadcasted_iota(jnp.int32, shape, 1)
    k_chunk = kj // CHUNK
    valid = (k_chunk >= q_chunk) & (k_chunk <= q_chunk + N_LEFT) & (block * Q_BLOCK + kj - KV_PAD >= 0)
    s = jnp.where(valid, s, NEG_INF)
    e = jnp.exp(s - jnp.max(s, axis=-1, keepdims=True))
    return e / jnp.sum(e, axis=-1, keepdims=True)


def _attn_specs(S):
    lanes = HEADS_PER_BLOCK * HEAD_DIM
    rows = S + KV_PAD
    q_spec = pl.BlockSpec((Q_BLOCK, lanes), lambda h, i: (i, h))
    k_spec = pl.BlockSpec((rows, lanes), lambda h, i: (0, h))
    v_spec = pl.BlockSpec((rows, lanes), lambda h, i: (0, N_HEADS // HEADS_PER_BLOCK + h))
    b_spec = pl.BlockSpec((HEADS_PER_BLOCK, Q_BLOCK, BAND), lambda h, i: (h, 0, 0))
    return q_spec, k_spec, v_spec, b_spec


def _attn_fwd(q, kvp, bias, name):
    S, HD = q.shape
    q_spec, k_spec, v_spec, b_spec = _attn_specs(S)

    def body(q_ref, k_ref, v_ref, b_ref, o_ref):
        block = pl.program_id(1)
        band = pl.ds(pl.multiple_of(block * Q_BLOCK, Q_BLOCK), BAND)
        p = _attn_probs(_head_rows(q_ref[...]), k_ref[band, :], b_ref[...].reshape(2 * Q_BLOCK, BAND), block)
        o2 = jnp.dot(p.astype(BF16), v_ref[band, :], preferred_element_type=F32)
        o_ref[...] = _head_lanes(o2).astype(BF16)

    return pl.pallas_call(
        body, name=name, grid=(N_HEADS // HEADS_PER_BLOCK, S // Q_BLOCK),
        in_specs=[q_spec, k_spec, v_spec, b_spec], out_specs=q_spec,
        out_shape=_sds((S, HD), BF16), compiler_params=_cparams(2))(q, kvp, kvp, bias)


def _attn_bwd(q, kvp, bias, do, dkv_prev, name):
    S, HD = q.shape
    lanes = HEADS_PER_BLOCK * HEAD_DIM
    q_spec, k_spec, v_spec, b_spec = _attn_specs(S)
    dkv_spec = pl.BlockSpec((2, S + KV_PAD, lanes), lambda h, i: (0, 0, h))

    def body(q_ref, k_ref, v_ref, b_ref, do_ref, prev_ref, dq_ref, dkv_ref, db_ref):
        block = pl.program_id(1)

        @pl.when(block == 0)
        def _():
            dkv_ref[...] = prev_ref[...]
            db_ref[...] = jnp.zeros_like(db_ref)

        band = pl.ds(pl.multiple_of(block * Q_BLOCK, Q_BLOCK), BAND)
        kb, vb = k_ref[band, :], v_ref[band, :]
        q2, do2 = _head_rows(q_ref[...]), _head_rows(do_ref[...])
        p = _attn_probs(q2, kb, b_ref[...].reshape(2 * Q_BLOCK, BAND), block)
        dp = lax.dot_general(do2, vb, NT_DIMS, preferred_element_type=F32)
        ds = p * (dp - jnp.sum(dp * p, axis=-1, keepdims=True))
        db_ref[...] += ds.reshape(HEADS_PER_BLOCK, Q_BLOCK, BAND)
        ds_b = (ds * ATTN_SCALE).astype(BF16)
        dq_ref[...] = _head_lanes(jnp.dot(ds_b, kb, preferred_element_type=F32)).astype(BF16)
        dkv_ref[0, band, :] += lax.dot_general(ds_b, q2, TN_DIMS, preferred_element_type=F32)
        dkv_ref[1, band, :] += lax.dot_general(p.astype(BF16), do2, TN_DIMS, preferred_element_type=F32)

    return pl.pallas_call(
        body, name=name, grid=(N_HEADS // HEADS_PER_BLOCK, S // Q_BLOCK),
        in_specs=[q_spec, k_spec, v_spec, b_spec, q_spec, dkv_spec], out_specs=[q_spec, dkv_spec, b_spec],
        out_shape=[_sds((S, HD), BF16), _sds((2, S + KV_PAD, HD), F32), _sds((N_HEADS, Q_BLOCK, BAND), F32)],
        compiler_params=_cparams(2))(q, kvp, kvp, bias, do, dkv_prev)


def _loss_head(x, g, target, name, tm=512):
    S, D = x.shape

    def body(x_ref, g_ref, t_ref, loss_ref, dx_ref, dg_ref):
        @pl.when(pl.program_id(0) == 0)
        def _():
            loss_ref[...] = jnp.zeros_like(loss_ref)
            dg_ref[...] = jnp.zeros_like(dg_ref)

        xhat, r = _rms_hat(x_ref[...])
        gv = g_ref[...]
        err = xhat * gv - t_ref[...]
        loss_ref[...] += 0.5 * jnp.sum(jnp.mean(err * err, axis=-1, keepdims=True))
        dx, dgp = _rms_bwd(xhat, r, gv, err * (1.0 / D))
        dx_ref[...] = dx
        dg_ref[...] += jnp.sum(dgp, axis=0, keepdims=True)

    row = pl.BlockSpec((tm, D), lambda i: (i, 0))
    vec = pl.BlockSpec((1, D), lambda i: (0, 0))
    return pl.pallas_call(
        body, name=name, grid=(S // tm,), in_specs=[row, vec, row],
        out_specs=[pl.BlockSpec((8, 128), lambda i: (0, 0)), row, vec],
        out_shape=[_sds((8, 128), F32), _sds((S, D), F32), _sds((1, D), F32)],
        compiler_params=_cparams(1))(x, g, target)


def _place():
    x, y, c = lax.axis_index("x"), lax.axis_index("y"), lax.axis_index("c")
    chips = [(1 - x, y), (x, 1 - y), (1 - x, 1 - y)]
    return x, y, c, chips


def _half_rows(c, r):
    return pl.ds(pl.multiple_of(c * (r // 2), 8), r // 2)


HBM_SPEC = pl.BlockSpec(memory_space=pl.ANY)


STRICT_HBM_SPEC = pl.BlockSpec(memory_space=pltpu.HBM)
SEM_SPEC = pl.BlockSpec(memory_space=pltpu.SEMAPHORE)
EFFECT = pltpu.SideEffectType.DATAFLOW_SIDE_EFFECTING


def _peers(x, y, c):
    out = []
    for k in range(1, N_DEV):
        px, py, pc = (x + ((k >> 2) & 1)) % 2, (y + ((k >> 1) & 1)) % 2, (c + (k & 1)) % 2
        out.append(((px, py, pc), 2 * px + py, pc, 4 * px + 2 * py + pc))
    return out


def _token_spec():
    return pl.BlockSpec(memory_space=pltpu.VMEM)


def _hbm(a):
    return pltpu.with_memory_space_constraint(a, pltpu.HBM)


def _allgather_start(slabs, name):
    n = len(slabs)

    def body(*refs):
        src, send, recv, token = refs[:n], refs[n], refs[n + 1], refs[-1]
        x, y, c, chips = _place()
        for a in range(n):
            own = src[a].at[2 * x + y]
            for j, chip in enumerate(chips):
                pltpu.make_async_remote_copy(src_ref=own, dst_ref=own, send_sem=send.at[3 * a + j], recv_sem=recv.at[3 * a + j],
                                             device_id=(*chip, c), device_id_type=MESH).start()
        token[...] = jnp.zeros_like(token)

    sems = pltpu.SemaphoreType.DMA((3 * n,))
    send, recv, *flying, token = pl.pallas_call(
        body, name=name, in_specs=[STRICT_HBM_SPEC] * n,
        out_shape=(sems, sems, *[pltpu.HBM(s.shape, s.dtype) for s in slabs], _sds((8, 128), F32)),
        out_specs=(SEM_SPEC, SEM_SPEC, *[STRICT_HBM_SPEC] * n, _token_spec()),
        input_output_aliases={a: a + 2 for a in range(n)},
        compiler_params=pltpu.CompilerParams(has_side_effects=EFFECT))(*[_hbm(s) for s in slabs])
    return send, recv, flying, token


def _allgather_wait(flying, send, recv, first, after, name):
    n = len(flying)

    def body(*refs):
        src, send_ref, recv_ref = refs[:n], refs[n], refs[n + 1]
        x, y, c, chips = _place()
        for a in range(n):
            for j, chip in enumerate(chips):
                cp = pltpu.make_async_remote_copy(
                    src_ref=src[a].at[2 * x + y], dst_ref=src[a].at[2 * chip[0] + chip[1]],
                    send_sem=send_ref.at[3 * (first + a) + j], recv_sem=recv_ref.at[3 * (first + a) + j],
                    device_id=(*chip, c), device_id_type=MESH)
                cp.wait_send()
                cp.wait_recv()

    return pl.pallas_call(
        body, name=name, in_specs=[STRICT_HBM_SPEC] * n + [SEM_SPEC, SEM_SPEC, HBM_SPEC],
        out_shape=tuple(pltpu.HBM(s.shape, s.dtype) for s in flying), out_specs=tuple([STRICT_HBM_SPEC] * n),
        input_output_aliases={a: a for a in range(n)},
        compiler_params=pltpu.CompilerParams(has_side_effects=EFFECT))(*flying, send, recv, after)


def _allgather_small(small, name):
    def body(sm, osm, send, recv, local):
        x, y, c, chips = _place()
        own = pltpu.make_async_copy(sm, osm.at[2 * x + y], local)
        own.start()
        cps = [pltpu.make_async_remote_copy(src_ref=sm, dst_ref=osm.at[2 * x + y], send_sem=send.at[j],
                                            recv_sem=recv.at[j], device_id=(*chip, c), device_id_type=MESH)
               for j, chip in enumerate(chips)]
        for cp in cps:
            cp.start()
        for j, chip in enumerate(chips):
            got = osm.at[2 * chip[0] + chip[1]]
            pltpu.make_async_remote_copy(src_ref=got, dst_ref=got, send_sem=send.at[j], recv_sem=recv.at[j],
                                         device_id=(x, y, c), device_id_type=MESH).wait_recv()
        for cp in cps:
            cp.wait_send()
        own.wait()

    return pl.pallas_call(
        body, name=name, in_specs=[pl.BlockSpec(memory_space=pltpu.VMEM)], out_specs=HBM_SPEC,
        out_shape=_sds((N_CHIPS, *small.shape), small.dtype),
        scratch_shapes=[pltpu.SemaphoreType.DMA((3,)), pltpu.SemaphoreType.DMA((3,)), pltpu.SemaphoreType.DMA])(small)


def _reduce_start(grads, name):
    n = len(grads)

    def body(*refs):
        src, land, send, recv, token = refs[:n], refs[n:2 * n], refs[2 * n], refs[2 * n + 1], refs[-1]
        x, y, c, _ = _place()
        me = 4 * x + 2 * y + c
        for a in range(n):
            for k, (peer, p_chip, p_core, _) in enumerate(_peers(x, y, c)):
                pltpu.make_async_remote_copy(
                    src_ref=src[a].at[p_chip, _half_rows(p_core, src[a].shape[1]), :], dst_ref=land[a].at[me],
                    send_sem=send.at[(N_DEV - 1) * a + k], recv_sem=recv.at[(N_DEV - 1) * a + k],
                    device_id=peer, device_id_type=MESH).start()
        token[...] = jnp.zeros_like(token)

    lands = [lax.empty((N_DEV, g.shape[1] // 2, g.shape[2]), BF16) for g in grads]
    sems = pltpu.SemaphoreType.DMA(((N_DEV - 1) * n,))
    shapes = [pltpu.HBM(a.shape, a.dtype) for a in grads + lands]
    send, recv, *flying, token = pl.pallas_call(
        body, name=name, in_specs=[STRICT_HBM_SPEC] * (2 * n),
        out_shape=(sems, sems, *shapes, _sds((8, 128), F32)),
        out_specs=(SEM_SPEC, SEM_SPEC, *[STRICT_HBM_SPEC] * (2 * n), _token_spec()),
        input_output_aliases={a: a + 2 for a in range(2 * n)},
        compiler_params=pltpu.CompilerParams(has_side_effects=EFFECT))(*[_hbm(a) for a in grads + lands])
    return send, recv, flying[:n], flying[n:], token


def _reduce_wait(started, after, name):
    sizes = [len(grads) for _, _, grads, _ in started]
    n_arr = 2 * sum(sizes)

    def body(*refs):
        x, y, c, _ = _place()
        at = 0
        for s, n in enumerate(sizes):
            src, land = refs[at:at + n], refs[at + n:at + 2 * n]
            send_ref, recv_ref = refs[n_arr + 2 * s], refs[n_arr + 2 * s + 1]
            at += 2 * n
            for a in range(n):
                for k, (peer, p_chip, p_core, p_dev) in enumerate(_peers(x, y, c)):
                    cp = pltpu.make_async_remote_copy(
                        src_ref=src[a].at[p_chip, _half_rows(p_core, src[a].shape[1]), :], dst_ref=land[a].at[p_dev],
                        send_sem=send_ref.at[(N_DEV - 1) * a + k], recv_sem=recv_ref.at[(N_DEV - 1) * a + k],
                        device_id=peer, device_id_type=MESH)
                    cp.wait_send()
                    cp.wait_recv()

    arrays, sems = [], []
    for send, recv, grads, lands in started:
        arrays += list(grads) + list(lands)
        sems += [send, recv]
    out = pl.pallas_call(
        body, name=name, in_specs=[STRICT_HBM_SPEC] * n_arr + [SEM_SPEC] * len(sems) + [HBM_SPEC],
        out_shape=tuple(pltpu.HBM(a.shape, a.dtype) for a in arrays), out_specs=tuple([STRICT_HBM_SPEC] * n_arr),
        input_output_aliases={a: a for a in range(n_arr)},
        compiler_params=pltpu.CompilerParams(has_side_effects=EFFECT))(*arrays, *sems, after)
    result, at = [], 0
    for n in sizes:
        result.append((out[at:at + n], out[at + n:at + 2 * n]))
        at += 2 * n
    return result


def _reduce_sum(grad, land, place, name):
    _, r2, C = land.shape
    tr = _row_tile(r2, C, 4, 1024 * 1024, 16)
    nb = r2 // tr

    def body(place_ref, own_ref, *rest):
        del place_ref
        acc = own_ref[...].astype(F32)
        for ref in rest[:N_DEV - 1]:
            acc = acc + ref[...].astype(F32)
        rest[-1][...] = acc

    def from_dev(k):
        return pl.BlockSpec((None, tr, C), lambda i, place_ref: ((place_ref[2] + k) % N_DEV, i, 0))

    grid_spec = pltpu.PrefetchScalarGridSpec(
        num_scalar_prefetch=1, grid=(nb,),
        in_specs=[pl.BlockSpec((None, tr, C), lambda i, place_ref: (place_ref[0], place_ref[1] * nb + i, 0))]
        + [from_dev(k) for k in range(1, N_DEV)],
        out_specs=pl.BlockSpec((tr, C), lambda i, place_ref: (place_ref[1] * nb + i, 0)))
    return pl.pallas_call(body, name=name, grid_spec=grid_spec, out_shape=_sds((2 * r2, C), F32),
                          compiler_params=_cparams(1))(place, grad, *[land] * (N_DEV - 1))


def _sibling_join(halves, name):
    n = len(halves)

    def body(*refs):
        out, send, recv = refs[n:2 * n], refs[2 * n], refs[2 * n + 1]
        x, y, c, _ = _place()
        cps = []
        for w in range(n):
            mine = out[w].at[_half_rows(c, out[w].shape[0]), :]
            cps.append(pltpu.make_async_remote_copy(src_ref=mine, dst_ref=mine, send_sem=send.at[w],
                                                    recv_sem=recv.at[w], device_id=(x, y, 1 - c), device_id_type=MESH))
        for cp in cps:
            cp.start()
        for w in range(n):
            theirs = out[w].at[_half_rows(1 - c, out[w].shape[0]), :]
            pltpu.make_async_remote_copy(src_ref=theirs, dst_ref=theirs, send_sem=send.at[w], recv_sem=recv.at[w],
                                         device_id=(x, y, c), device_id_type=MESH).wait_recv()
        for cp in cps:
            cp.wait_send()

    return pl.pallas_call(
        body, name=name, in_specs=[HBM_SPEC] * n, out_specs=[HBM_SPEC] * n,
        out_shape=[_sds(a.shape, F32) for a in halves], input_output_aliases={w: w for w in range(n)},
        scratch_shapes=[pltpu.SemaphoreType.DMA((n,)), pltpu.SemaphoreType.DMA((n,))])(*halves)


def _gather_small(packed, name):
    def body(p_ref, out, send, recv, local):
        x, y, c, _ = _place()
        me = 4 * x + 2 * y + c
        own = pltpu.make_async_copy(p_ref, out.at[me], local)
        own.start()
        cps = []
        for k in range(1, N_DEV):
            fx, fy, fc = (k >> 2) & 1, (k >> 1) & 1, k & 1
            peer = ((x + fx) % 2, (y + fy) % 2, (c + fc) % 2)
            cps.append(pltpu.make_async_remote_copy(src_ref=p_ref, dst_ref=out.at[me], send_sem=send.at[k - 1],
                                                    recv_sem=recv.at[k - 1], device_id=peer, device_id_type=MESH))
        for cp in cps:
            cp.start()
        for k in range(1, N_DEV):
            fx, fy, fc = (k >> 2) & 1, (k >> 1) & 1, k & 1
            src = out.at[4 * ((x + fx) % 2) + 2 * ((y + fy) % 2) + (c + fc) % 2]
            pltpu.make_async_remote_copy(src_ref=src, dst_ref=src, send_sem=send.at[k - 1], recv_sem=recv.at[k - 1],
                                         device_id=(x, y, c), device_id_type=MESH).wait_recv()
        for cp in cps:
            cp.wait_send()
        own.wait()

    return pl.pallas_call(
        body, name=name, in_specs=[pl.BlockSpec(memory_space=pltpu.VMEM)], out_specs=HBM_SPEC,
        out_shape=_sds((N_DEV, *packed.shape), F32),
        scratch_shapes=[pltpu.SemaphoreType.DMA((N_DEV - 1,)), pltpu.SemaphoreType.DMA((N_DEV - 1,)),
                        pltpu.SemaphoreType.DMA])(packed)


def _sum_devices(gathered, name):
    _, R, C = gathered.shape

    def body(g_ref, o_ref):
        acc = g_ref[0]
        for d in range(1, N_DEV):
            acc = acc + g_ref[d]
        o_ref[...] = acc

    tr = 8
    return pl.pallas_call(
        body, name=name, grid=(R // tr,), in_specs=[pl.BlockSpec((N_DEV, tr, C), lambda i: (0, i, 0))],
        out_specs=pl.BlockSpec((tr, C), lambda i: (i, 0)), out_shape=_sds((R, C), F32),
        compiler_params=_cparams(1))(gathered)


def _pack_small(arrays):
    rows = []
    for a in arrays:
        flat = a.reshape(-1)
        pad = (-flat.shape[0]) % SMALL_COLS
        rows.append(jnp.pad(flat, (0, pad)).reshape(-1, SMALL_COLS))
    packed = jnp.concatenate(rows, axis=0)
    return jnp.pad(packed, ((0, (-packed.shape[0]) % 8), (0, 0)))


def _unpack_small(packed, shapes):
    out, row = [], 0
    for shape in shapes:
        size = math.prod(shape)
        n_rows = -(-size // SMALL_COLS)
        out.append(packed[row:row + n_rows].reshape(-1)[:size].reshape(shape))
        row += n_rows
    return out


def kernel(x, a_norm, a_w_in, a_sgu_norm, a_w_spatial, a_b_spatial, a_w_out, kv_norm, w_kv, b_norm, b_w_q, b_rel_bias, b_w_o, ffn_norm, ffn_w_gate_up, ffn_w_down, final_norm, loss_target, m_a_norm, m_a_w_in, m_a_sgu_norm, m_a_w_spatial, m_a_b_spatial, m_a_w_out, m_kv_norm, m_w_kv, m_b_norm, m_b_w_q, m_b_rel_bias, m_b_w_o, m_ffn_norm, m_ffn_w_gate_up, m_ffn_w_down, m_final_norm, v_a_norm, v_a_w_in, v_a_sgu_norm, v_a_w_spatial, v_a_b_spatial, v_a_w_out, v_kv_norm, v_w_kv, v_b_norm, v_b_w_q, v_b_rel_bias, v_b_w_o, v_ffn_norm, v_ffn_w_gate_up, v_ffn_w_down, v_final_norm):
    S, D = x.shape[1], x.shape[2]
    n_a = a_w_in.shape[0]
    n_b = b_w_q.shape[0]
    depth = ffn_w_gate_up.shape[0]
    xi, yi, ci = lax.axis_index("x"), lax.axis_index("y"), lax.axis_index("c")
    chip = 2 * xi + yi

    place = jnp.stack([chip, ci, 2 * chip + ci]).astype(jnp.int32)
    stacked = {"a_w_in": a_w_in, "a_w_out": a_w_out, "w_kv": w_kv[None], "b_w_q": b_w_q, "b_w_o": b_w_o,
               "ffn_w_gate_up": ffn_w_gate_up, "ffn_w_down": ffn_w_down}
    groups = []
    for layer in range(depth):
        if layer < n_a:
            groups.append([("a_w_in", layer), ("a_w_out", layer)])
        elif layer == n_a:
            groups.append([("w_kv", 0), ("b_w_q", 0), ("b_w_o", 0)])
        else:
            groups.append([("b_w_q", layer - n_a), ("b_w_o", layer - n_a)])
        groups.append([("ffn_w_gate_up", layer), ("ffn_w_down", layer)])
    units = [u for group in groups for u in group]
    slabs = [_cast_slab(stacked[k], l, place[:1], f"cast_{k}_{l}") for k, l in units]
    ag_send, ag_recv, flying, _ = _allgather_start(slabs, "allgather_start")
    na_w, ns_w = a_norm.shape[1], a_sgu_norm.shape[1]
    small_g = _allgather_small(jnp.concatenate([a_norm, a_sgu_norm], axis=1), "allgather_small")
    a_norm_f = small_g[:, :, :na_w].transpose(1, 0, 2).reshape(n_a, N_CHIPS * na_w)
    a_sgu_f = small_g[:, :, na_w:].transpose(1, 0, 2).reshape(n_a, N_CHIPS * ns_w)
    W = {}

    def gathered(group_index, after):
        group = groups[group_index]
        first = units.index(group[0])
        done = _allgather_wait(flying[first:first + len(group)], ag_send, ag_recv, first, after,
                               f"allgather_wait_{group_index}")
        W.update(zip(group, done))

    xc = x.reshape(S, D)
    saved = []
    kvp = x_kv = h_kv = None
    for layer in range(depth):
        rec = {"x_in": xc}
        gathered(2 * layer, xc)
        if layer < n_a:
            i = layer
            rec["zpre"], rec["h"] = _norm_matmul(xc, a_norm_f[i][None], W["a_w_in", i], F32, f"a{i}_in")
            rec["uv"] = _sgu_fwd(rec["zpre"], a_sgu_f[i][None], a_w_spatial[i], a_b_spatial[i].T, f"a{i}_sgu")
            xm = _matmul_res(rec["uv"], W["a_w_out", i], xc, f"a{i}_out")
        else:
            i = layer - n_a
            if i == 0:
                kv, h_kv = _norm_matmul(xc, kv_norm[None], W["w_kv", 0], BF16, "kv_proj")
                kvp = jnp.pad(kv, ((KV_PAD, 0), (0, 0)))
                x_kv = xc
            rec["q"], rec["h"] = _norm_matmul(xc, b_norm[i][None], W["b_w_q", i], BF16, f"b{i}_q", row_sharded=True)
            table = jnp.pad(b_rel_bias[i], ((0, 0), (0, TABLE_PAD - b_rel_bias.shape[2])))
            rec["bias"] = _rel_bias_fwd(table, f"b{i}_bias").transpose(1, 0, 2)
            rec["o"] = _attn_fwd(rec["q"], kvp, rec["bias"], f"b{i}_attn")
            xm = _matmul_res(rec["o"], W["b_w_o", i], xc, f"b{i}_o")
        rec["x_mid"] = xm
        gathered(2 * layer + 1, xm)
        rec["gu"], rec["h_f"] = _norm_matmul(xm, ffn_norm[layer][None], W["ffn_w_gate_up", layer], F32, f"f{layer}_in")
        xc = _matmul_res(rec["gu"], W["ffn_w_down", layer], xm, f"f{layer}_out", swiglu=True)
        saved.append(rec)

    loss_tile, dx, d_final = _loss_head(xc, final_norm[None], loss_target.reshape(S, D), "loss_head")
    loss = lax.psum(loss_tile[0, 0], ("x", "y", "c"))

    started = []

    def weight_grad(unit, **kw):
        full = (N_CHIPS,) + tuple(stacked[unit[0]].shape[1:])
        g = _matmul_tn(out_shape=full, name=f"d_{unit[0]}_{unit[1]}", **kw)
        send, recv, flying_g, flying_land, token = _reduce_start([g], f"reduce_start_{unit[0]}_{unit[1]}")
        started.append((unit, send, recv, flying_g, flying_land))
        return token

    row_a = lambda w: pl.BlockSpec((512, w), lambda o, t: (t, 0))
    d_ffn_norm, d_b_norm, d_a_norm, d_a_sgu = [None] * depth, [None] * n_b, [None] * n_a, [None] * n_a
    d_ws, d_bs, d_rel = [None] * n_a, [None] * n_a, [None] * n_b
    dkv = jnp.zeros((2, S + KV_PAD, D), F32)
    first = lambda ref: ref[...]
    for layer in reversed(range(depth)):
        rec = saved[layer]
        r_d = ffn_w_down.shape[1]
        half_f = 2 * r_d
        token = weight_grad(
            ("ffn_w_down", layer), a_ops=[rec["gu"], rec["gu"]],
            a_specs=[pl.BlockSpec((512, half_f), lambda o, t: (t, o)), pl.BlockSpec((512, half_f), lambda o, t: (t, 2 + o))],
            a_fn=lambda g_ref, u_ref: _swiglu(g_ref[...], u_ref[...]).astype(BF16),
            b_op=dx, b_spec=row_a(D), out_spec=pl.BlockSpec((2, r_d, D), lambda o, t: (o, 0, 0)),
            acc_shape=(half_f, D), n_outer=2)
        dgu = _nt_swiglu_bwd(dx, W["ffn_w_down", layer], rec["gu"], f"f{layer}_dgu", deps=(token,))
        nsh = ffn_w_gate_up.shape[2]
        token = weight_grad(
            ("ffn_w_gate_up", layer), a_ops=[rec["h_f"]], a_specs=[row_a(D)], a_fn=first,
            b_op=dgu, b_spec=pl.BlockSpec((None, 512, nsh), lambda o, t: (o // 2, t, o % 2)),
            out_spec=pl.BlockSpec((None, D, nsh), lambda o, t: (o, 0, 0)), acc_shape=(D, nsh), n_outer=N_CHIPS)
        dx, d_ffn_norm[layer] = _nt_normbwd(
            dgu, pl.BlockSpec((None, 512, nsh), lambda i, k: (k // 2, i, k % 2)),
            W["ffn_w_gate_up", layer], pl.BlockSpec((None, D, nsh), lambda i, k: (k, 0, 0)),
            (D, nsh), N_CHIPS, rec["x_mid"], ffn_norm[layer][None], dx, f"f{layer}_dx", deps=(token,))
        if layer >= n_a:
            i = layer - n_a
            r_o = b_w_o.shape[1]
            token = weight_grad(
                ("b_w_o", i), a_ops=[rec["o"]], a_specs=[row_a(D)], a_fn=first,
                b_op=dx, b_spec=pl.BlockSpec((512, 512), lambda o, t: (t, o)),
                out_spec=pl.BlockSpec((N_CHIPS, r_o, 512), lambda o, t: (0, 0, o)), acc_shape=(D, 512), n_outer=2)
            do = _nt_rows(dx, W["b_w_o", i], 2, BF16, f"b{i}_do", deps=(token,))
            dq, dkv, dbias = _attn_bwd(rec["q"], kvp, rec["bias"], do, dkv, f"b{i}_attn_bwd")
            d_rel[i] = _rel_bias_bwd(dbias.transpose(1, 0, 2), f"b{i}_dbias")[:, :b_rel_bias.shape[2]]
            token = weight_grad(
                ("b_w_q", i), a_ops=[rec["h"]], a_specs=[row_a(D)], a_fn=first,
                b_op=dq, b_spec=pl.BlockSpec((512, 512), lambda o, t: (t, o)),
                out_spec=pl.BlockSpec((N_CHIPS, r_o, 512), lambda o, t: (0, 0, o)), acc_shape=(D, 512), n_outer=2)
            dx, d_b_norm[i] = _nt_normbwd(
                dq, pl.BlockSpec((512, D), lambda i_, k: (i_, 0)),
                W["b_w_q", i], pl.BlockSpec((N_CHIPS, r_o, D), lambda i_, k: (0, 0, 0)),
                (D, D), 1, rec["x_in"], b_norm[i][None], dx, f"b{i}_dx", deps=(token,))
            if i == 0:
                dkv_b = dkv[:, KV_PAD:, :].astype(BF16)
                n_kv = w_kv.shape[1]
                token = weight_grad(
                    ("w_kv", 0), a_ops=[h_kv], a_specs=[row_a(D)], a_fn=first,
                    b_op=dkv_b, b_spec=pl.BlockSpec((None, 512, n_kv), lambda o, t: (o // 2, t, o % 2)),
                    out_spec=pl.BlockSpec((None, D, n_kv), lambda o, t: (o, 0, 0)), acc_shape=(D, n_kv),
                    n_outer=N_CHIPS)
                dx, d_kv_norm = _nt_normbwd(
                    dkv_b, pl.BlockSpec((None, 512, n_kv), lambda i_, k: (k // 2, i_, k % 2)),
                    W["w_kv", 0], pl.BlockSpec((None, D, n_kv), lambda i_, k: (k, 0, 0)),
                    (D, n_kv), N_CHIPS, x_kv, kv_norm[None], dx, "kv_dx", deps=(token,))
        else:
            i = layer
            r_w = a_w_out.shape[1]
            token = weight_grad(
                ("a_w_out", i), a_ops=[rec["uv"]], a_specs=[pl.BlockSpec((512, r_w), lambda o, t: (t, o))], a_fn=first,
                b_op=dx, b_spec=row_a(D), out_spec=pl.BlockSpec((None, r_w, D), lambda o, t: (o, 0, 0)),
                acc_shape=(r_w, D), n_outer=N_CHIPS)
            duv = _nt_rows(dx, W["a_w_out", i], 1, F32, f"a{i}_duv", deps=(token,))
            dz, d_a_sgu[i], d_ws[i], dbs = _sgu_bwd(rec["zpre"], duv, a_sgu_f[i][None], a_w_spatial[i],
                                                  a_w_spatial[i].transpose(0, 2, 1), a_b_spatial[i].T, f"a{i}_sgu_bwd")
            d_bs[i] = dbs[:, :, 0]
            n_in = a_w_in.shape[2]
            token = weight_grad(
                ("a_w_in", i), a_ops=[rec["h"]], a_specs=[row_a(D)], a_fn=first,
                b_op=dz, b_spec=pl.BlockSpec((512, 512), lambda o, t: (t, o)),
                out_spec=pl.BlockSpec((None, D, 512), lambda o, t: (o // 2, 0, o % 2)), acc_shape=(D, 512),
                n_outer=N_CHIPS * n_in // 512)
            dx, d_a_norm[i] = _nt_normbwd(
                dz, pl.BlockSpec((512, 512), lambda i_, k: (i_, k)),
                W["a_w_in", i], pl.BlockSpec((None, D, 512), lambda i_, k: (k // 2, 0, k % 2)),
                (D, 512), N_CHIPS * n_in // 512, rec["x_in"], a_norm_f[i][None], dx, f"a{i}_dx", deps=(token,))
    grad_x = dx.reshape(x.shape)

    landed = _reduce_wait([(send, recv, g, land) for _, send, recv, g, land in started], dx, "reduce_wait")
    halves = [_reduce_sum(g[0], land[0], place, f"reduce_sum_{unit[0]}_{unit[1]}")
              for (unit, *_), (g, land) in zip(started, landed)]
    joined = dict(zip([s[0] for s in started], _sibling_join(halves, "reduce_join")))
    reduced = {}

    small = [jnp.concatenate(d_a_norm, axis=0), jnp.concatenate(d_a_sgu, axis=0), jnp.stack(d_ws), jnp.stack(d_bs),
             d_kv_norm, jnp.concatenate(d_b_norm, axis=0), jnp.stack(d_rel), jnp.concatenate(d_ffn_norm, axis=0), d_final]
    total = _sum_devices(_gather_small(_pack_small(small), "gather_small_grads"), "sum_small_grads")
    (g_a_norm, g_a_sgu, g_ws, g_bs, g_kv_norm, g_b_norm, g_rel, g_ffn_norm, g_final) = _unpack_small(
        total, [a.shape for a in small])
    reduced["a_norm"] = lax.dynamic_slice_in_dim(g_a_norm, chip * na_w, na_w, axis=1)
    reduced["a_sgu_norm"] = lax.dynamic_slice_in_dim(g_a_sgu, chip * ns_w, ns_w, axis=1)
    reduced.update(a_w_spatial=g_ws, a_b_spatial=g_bs, kv_norm=g_kv_norm.reshape(kv_norm.shape), b_norm=g_b_norm,
                   b_rel_bias=g_rel, ffn_norm=g_ffn_norm, final_norm=g_final.reshape(final_norm.shape))

    weights = dict(a_norm=a_norm, a_w_in=a_w_in, a_sgu_norm=a_sgu_norm, a_w_spatial=a_w_spatial,
                   a_b_spatial=a_b_spatial, a_w_out=a_w_out, kv_norm=kv_norm, w_kv=w_kv, b_norm=b_norm, b_w_q=b_w_q,
                   b_rel_bias=b_rel_bias, b_w_o=b_w_o, ffn_norm=ffn_norm, ffn_w_gate_up=ffn_w_gate_up,
                   ffn_w_down=ffn_w_down, final_norm=final_norm)
    m_in = dict(a_norm=m_a_norm, a_w_in=m_a_w_in, a_sgu_norm=m_a_sgu_norm, a_w_spatial=m_a_w_spatial,
                a_b_spatial=m_a_b_spatial, a_w_out=m_a_w_out, kv_norm=m_kv_norm, w_kv=m_w_kv, b_norm=m_b_norm,
                b_w_q=m_b_w_q, b_rel_bias=m_b_rel_bias, b_w_o=m_b_w_o, ffn_norm=m_ffn_norm,
                ffn_w_gate_up=m_ffn_w_gate_up, ffn_w_down=m_ffn_w_down, final_norm=m_final_norm)
    v_in = dict(a_norm=v_a_norm, a_w_in=v_a_w_in, a_sgu_norm=v_a_sgu_norm, a_w_spatial=v_a_w_spatial,
                a_b_spatial=v_a_b_spatial, a_w_out=v_a_w_out, kv_norm=v_kv_norm, w_kv=v_w_kv, b_norm=v_b_norm,
                b_w_q=v_b_w_q, b_rel_bias=v_b_rel_bias, b_w_o=v_b_w_o, ffn_norm=v_ffn_norm,
                ffn_w_gate_up=v_ffn_w_gate_up, ffn_w_down=v_ffn_w_down, final_norm=v_final_norm)
    grad_out, delta_out, m_out, v_out = [], [], [], []
    for key, w in weights.items():
        if key in stacked:
            as_layers = lambda a: a.reshape(stacked[key].shape)
            bufs = None
            for layer in range(stacked[key].shape[0]):
                bufs = _adamw_layer(as_layers(w), joined[key, layer], as_layers(m_in[key]), as_layers(v_in[key]),
                                    layer, bufs, f"adamw_{key}_{layer}")
            g, d, nm, nv = bufs
        else:
            g = reduced[key].reshape(w.shape)
            view = (1, w.shape[0]) if w.ndim == 1 else (-1, w.shape[-1])
            d, nm, nv = _adamw(w.reshape(view), g.reshape(view), m_in[key].reshape(view), v_in[key].reshape(view),
                               "adamw_" + key)
        grad_out.append(g.reshape(w.shape))
        delta_out.append(d.reshape(w.shape))
        m_out.append(nm.reshape(w.shape))
        v_out.append(nv.reshape(w.shape))
    return (loss, grad_x, *grad_out, *delta_out, *m_out, *v_out)
```

```python
import math

import jax
import jax.numpy as jnp
from jax import lax
from jax.experimental import pallas as pl
from jax.experimental.pallas import tpu as pltpu

F32, BF16 = jnp.float32, jnp.bfloat16
MESH = pl.DeviceIdType.MESH
HIGHEST = lax.Precision.HIGHEST
NT_DIMS = (((1,), (1,)), ((), ()))
TN_DIMS = (((0,), (0,)), ((), ()))

EPS = 1e-6
CHUNK = 64
A_CHUNK = 128
A_GROUPS = 8
N_HEADS = 16
HEAD_DIM = 64
N_LEFT = 8
MAX_REL = 256
ATTN_SCALE = HEAD_DIM ** -0.5
NEG_INF = -1e30
Q_BLOCK = 2 * CHUNK
KV_PAD = N_LEFT * CHUNK
BAND = KV_PAD + Q_BLOCK
DIAGS = BAND + Q_BLOCK
TABLE_PAD = 640
HEADS_PER_BLOCK = 2
BLOCKS_PER_STEP = 2
NORMBWD_ROWS = 1024

ADAM_LR, ADAM_B1, ADAM_B2, ADAM_EPS, ADAM_WD, ADAM_STEP = 0.001, 0.9, 0.999, 1e-08, 0.01, 10

VMEM_LIMIT_BYTES = 56 * 1024 * 1024
N_CHIPS = 4
N_DEV = 8
SMALL_COLS = 1024


def _cparams(n_grid):
    return pltpu.CompilerParams(dimension_semantics=("arbitrary",) * n_grid, vmem_limit_bytes=VMEM_LIMIT_BYTES)


def _sds(shape, dtype):
    return jax.ShapeDtypeStruct(tuple(shape), dtype)


def _gelu(x):
    return x * (0.5 * (1.0 + lax.erf(x * math.sqrt(0.5))))


def _gelu_and_grad(x):
    cdf = 0.5 * (1.0 + lax.erf(x * math.sqrt(0.5)))
    return x * cdf, cdf + x * (jnp.exp(-0.5 * x * x) * (1.0 / math.sqrt(2.0 * math.pi)))


def _rms_hat(xv):
    r = lax.rsqrt(jnp.mean(xv * xv, axis=-1, keepdims=True) + EPS)
    return xv * r, r


def _rms_bwd(xhat, r, g, dy):
    dxhat = dy * g
    dx = r * (dxhat - xhat * jnp.mean(dxhat * xhat, axis=-1, keepdims=True))
    return dx, dy * xhat


def _swiglu(gate, up):
    return (gate * jax.nn.sigmoid(gate)) * up


def _row_tile(rows, cols, itemsize, cap_bytes, align):
    t = rows
    while t * cols * itemsize > cap_bytes and t % (2 * align) == 0:
        t //= 2
    return t


def _cast_slab(w, layer, chip, name, deps=()):
    _, r, C = w.shape
    tr = _row_tile(r, C, 4, 4 * 1024 * 1024, 16)

    def body(chip_ref, w_ref, *rest):
        del chip_ref
        rest[-1][...] = w_ref[...].astype(BF16)

    grid_spec = pltpu.PrefetchScalarGridSpec(
        num_scalar_prefetch=1, grid=(r // tr,),
        in_specs=[pl.BlockSpec((None, tr, C), lambda i, chip_ref: (layer, i, 0))] + [HBM_SPEC] * len(deps),
        out_specs=pl.BlockSpec((None, tr, C), lambda i, chip_ref: (chip_ref[0], i, 0)))
    return pl.pallas_call(body, name=name, grid_spec=grid_spec, out_shape=_sds((N_CHIPS, r, C), BF16),
                          compiler_params=_cparams(1))(chip, w, *deps)


def _adamw_layer(w, g, m, v, layer, bufs, name):
    L, r, C = w.shape
    tr = _row_tile(r, C, 4, 1024 * 1024, 8)

    def body(w_ref, g_ref, m_ref, v_ref, *rest):
        go_ref, d_ref, nm_ref, nv_ref = rest[-4:]
        gv = g_ref[...]
        mn = ADAM_B1 * m_ref[...] + (1.0 - ADAM_B1) * gv
        vn = ADAM_B2 * v_ref[...] + (1.0 - ADAM_B2) * jnp.square(gv)
        m_hat = mn / (1.0 - ADAM_B1 ** ADAM_STEP)
        v_hat = vn / (1.0 - ADAM_B2 ** ADAM_STEP)
        d_ref[...] = -ADAM_LR * (m_hat / (jnp.sqrt(v_hat) + ADAM_EPS) + ADAM_WD * w_ref[...])
        nm_ref[...] = mn
        nv_ref[...] = vn
        go_ref[...] = gv

    stacked = pl.BlockSpec((None, tr, C), lambda i: (layer, i, 0))
    in_specs = [stacked, pl.BlockSpec((tr, C), lambda i: (i, 0)), stacked, stacked]
    ops = [w, g, m, v]
    aliases = {}
    if bufs is not None:
        in_specs += [HBM_SPEC] * 4
        ops += list(bufs)
        aliases = {4 + k: k for k in range(4)}
    return pl.pallas_call(body, name=name, grid=(r // tr,), in_specs=in_specs, out_specs=[stacked] * 4,
                          out_shape=[_sds((L, r, C), F32)] * 4, input_output_aliases=aliases,
                          compiler_params=_cparams(1))(*ops)


def _adamw(w, g, m, v, name):
    R, C = w.shape
    tr = _row_tile(R, C, 4, 1024 * 1024, 8)

    def body(w_ref, g_ref, m_ref, v_ref, d_ref, nm_ref, nv_ref):
        gv = g_ref[...]
        mn = ADAM_B1 * m_ref[...] + (1.0 - ADAM_B1) * gv
        vn = ADAM_B2 * v_ref[...] + (1.0 - ADAM_B2) * jnp.square(gv)
        m_hat = mn / (1.0 - ADAM_B1 ** ADAM_STEP)
        v_hat = vn / (1.0 - ADAM_B2 ** ADAM_STEP)
        d_ref[...] = -ADAM_LR * (m_hat / (jnp.sqrt(v_hat) + ADAM_EPS) + ADAM_WD * w_ref[...])
        nm_ref[...] = mn
        nv_ref[...] = vn

    spec = pl.BlockSpec((tr, C), lambda i: (i, 0))
    return pl.pallas_call(body, name=name, grid=(R // tr,), in_specs=[spec] * 4, out_specs=[spec] * 3,
                          out_shape=[_sds((R, C), F32)] * 3, compiler_params=_cparams(1))(w, g, m, v)


def _norm_matmul(x, g, w_g, out_dtype, name, row_sharded=False, tm=1024):
    S, D = x.shape
    tm = min(tm, S)
    if row_sharded:
        r, N = w_g.shape[1], w_g.shape[2]
        tn = 512
        w_spec = pl.BlockSpec((N_CHIPS, r, tn), lambda i, j: (0, 0, j))
    else:
        nsh = w_g.shape[2]
        N = N_CHIPS * nsh
        tn = 512 if nsh % 512 == 0 else nsh
        bps = nsh // tn
        w_spec = pl.BlockSpec((None, D, tn), lambda i, j: (j // bps, 0, j % bps))

    def body(x_ref, g_ref, w_ref, y_ref, h_ref):
        @pl.when(pl.program_id(1) == 0)
        def _():
            xhat, _ = _rms_hat(x_ref[...])
            h_ref[...] = (xhat * g_ref[...]).astype(BF16)

        w = w_ref[...].reshape(D, tn)
        y_ref[...] = jnp.dot(h_ref[...], w, preferred_element_type=F32).astype(y_ref.dtype)

    return pl.pallas_call(
        body, name=name, grid=(S // tm, N // tn),
        in_specs=[pl.BlockSpec((tm, D), lambda i, j: (i, 0)), pl.BlockSpec((1, D), lambda i, j: (0, 0)), w_spec],
        out_specs=[pl.BlockSpec((tm, tn), lambda i, j: (i, j)), pl.BlockSpec((tm, D), lambda i, j: (i, 0))],
        out_shape=[_sds((S, N), out_dtype), _sds((S, D), BF16)],
        compiler_params=_cparams(2))(x, g, w_g)


def _matmul_res(a, w_g, res, name, swiglu=False, tm=256):
    S, N = res.shape
    r = w_g.shape[1]
    K = N_CHIPS * r

    def body(*refs):
        if swiglu:
            gate_ref, up_ref, w_ref, res_ref, o_ref = refs
            a_blk = _swiglu(gate_ref[...].astype(F32), up_ref[...].astype(F32)).astype(BF16)
        else:
            a_ref, w_ref, res_ref, o_ref = refs
            a_blk = a_ref[...]
        o_ref[...] = res_ref[...] + jnp.dot(a_blk, w_ref[...].reshape(K, N), preferred_element_type=F32)

    a_specs, a_ops = [pl.BlockSpec((tm, K), lambda i: (i, 0))], [a]
    if swiglu:
        a_specs.append(pl.BlockSpec((tm, K), lambda i: (i, 1)))
        a_ops.append(a)
    row = pl.BlockSpec((tm, N), lambda i: (i, 0))
    return pl.pallas_call(
        body, name=name, grid=(S // tm,),
        in_specs=a_specs + [pl.BlockSpec((N_CHIPS, r, N), lambda i: (0, 0, 0)), row],
        out_specs=row, out_shape=_sds((S, N), F32), compiler_params=_cparams(1))(*a_ops, w_g, res)


def _matmul_tn(a_ops, a_specs, a_fn, b_op, b_spec, out_spec, out_shape, acc_shape, n_outer, name, tt=512):
    S = b_op.shape[-2]
    na = len(a_ops)
    nt = S // tt

    def body(*refs):
        a_refs, b_ref, o_ref, acc_ref = refs[:na], refs[na], refs[na + 1], refs[na + 2]
        t = pl.program_id(1)
        part = lax.dot_general(a_fn(*a_refs), b_ref[...].astype(BF16), TN_DIMS, preferred_element_type=F32)

        @pl.when(t == 0)
        def _():
            acc_ref[...] = part

        @pl.when(t > 0)
        def _():
            acc_ref[...] += part

        @pl.when(t == nt - 1)
        def _():
            o_ref[...] = acc_ref[...].reshape(o_ref.shape).astype(BF16)

    return pl.pallas_call(
        body, name=name, grid=(n_outer, nt), in_specs=list(a_specs) + [b_spec], out_specs=out_spec,
        out_shape=_sds(out_shape, BF16), scratch_shapes=[pltpu.VMEM(acc_shape, F32)],
        compiler_params=_cparams(2))(*a_ops, b_op)


def _nt_accumulate(a_ref, w_ref, acc_ref, w2d, nk):
    k = pl.program_id(1)
    part = lax.dot_general(a_ref[...].astype(BF16), w_ref[...].reshape(w2d), NT_DIMS, preferred_element_type=F32)

    @pl.when(k == 0)
    def _():
        acc_ref[...] = part

    @pl.when(k > 0)
    def _():
        acc_ref[...] += part

    return k == nk - 1


def _nt_normbwd(dy, dy_spec, w_g, w_spec, w2d, nk, x, g, dres, name, deps=(), tm=NORMBWD_ROWS):
    S, D = x.shape
    tm = min(tm, S)

    def body(dy_ref, w_ref, x_ref, g_ref, dres_ref, *rest):
        dx_ref, dg_ref, acc_ref = rest[-3:]

        @pl.when((pl.program_id(0) == 0) & (pl.program_id(1) == 0))
        def _():
            dg_ref[...] = jnp.zeros_like(dg_ref)

        last = _nt_accumulate(dy_ref, w_ref, acc_ref, w2d, nk)

        @pl.when(last)
        def _():
            xhat, r = _rms_hat(x_ref[...])
            dx, dgp = _rms_bwd(xhat, r, g_ref[...], acc_ref[...])
            dx_ref[...] = dres_ref[...] + dx
            dg_ref[...] += jnp.sum(dgp, axis=0, keepdims=True)

    row = pl.BlockSpec((tm, D), lambda i, k: (i, 0))
    vec = pl.BlockSpec((1, D), lambda i, k: (0, 0))
    return pl.pallas_call(
        body, name=name, grid=(S // tm, nk),
        in_specs=[dy_spec, w_spec, row, vec, row] + [HBM_SPEC] * len(deps), out_specs=[row, vec],
        out_shape=[_sds((S, D), F32), _sds((1, D), F32)],
        scratch_shapes=[pltpu.VMEM((tm, D), F32)], compiler_params=_cparams(2))(dy, w_g, x, g, dres, *deps)


def _nt_rows(dy, w_g, shards_per_block, out_dtype, name, deps=(), tm=1024):
    S, N = dy.shape
    tm = min(tm, S)
    r = w_g.shape[1]
    tn = shards_per_block * r

    def body(dy_ref, w_ref, *rest):
        o_ref = rest[-1]
        o_ref[...] = lax.dot_general(dy_ref[...].astype(BF16), w_ref[...].reshape(tn, N), NT_DIMS,
                                     preferred_element_type=F32).astype(o_ref.dtype)

    return pl.pallas_call(
        body, name=name, grid=(S // tm, N_CHIPS // shards_per_block),
        in_specs=[pl.BlockSpec((tm, N), lambda i, j: (i, 0)),
                  pl.BlockSpec((shards_per_block, r, N), lambda i, j: (j, 0, 0))] + [HBM_SPEC] * len(deps),
        out_specs=pl.BlockSpec((tm, tn), lambda i, j: (i, j)),
        out_shape=_sds((S, N_CHIPS * r), out_dtype), compiler_params=_cparams(2))(dy, w_g, *deps)


def _nt_swiglu_bwd(dy, w_g, gu, name, deps=(), tm=512):
    S, N = dy.shape
    r = w_g.shape[1]
    tn = 2 * r
    F = N_CHIPS * r

    def body(dy_ref, w_ref, gate_ref, up_ref, *rest):
        o_ref = rest[-1]
        dact = lax.dot_general(dy_ref[...].astype(BF16), w_ref[...].reshape(tn, N), NT_DIMS,
                               preferred_element_type=F32)
        gate, up = gate_ref[...].astype(F32), up_ref[...].astype(F32)
        sg = jax.nn.sigmoid(gate)
        silu = gate * sg
        o_ref[0] = ((dact * up) * (sg + silu * (1.0 - sg))).astype(BF16)
        o_ref[1] = (dact * silu).astype(BF16)

    return pl.pallas_call(
        body, name=name, grid=(2, S // tm),
        in_specs=[pl.BlockSpec((tm, N), lambda j, i: (i, 0)),
                  pl.BlockSpec((2, r, N), lambda j, i: (j, 0, 0)),
                  pl.BlockSpec((tm, tn), lambda j, i: (i, j)),
                  pl.BlockSpec((tm, tn), lambda j, i: (i, 2 + j))] + [HBM_SPEC] * len(deps),
        out_specs=pl.BlockSpec((2, tm, tn), lambda j, i: (0, i, j)),
        out_shape=_sds((2, S, F), BF16), compiler_params=_cparams(2))(dy, w_g, gu, gu, *deps)


def _chunk_causal_mask(transposed):
    i = lax.broadcasted_iota(jnp.int32, (A_CHUNK, A_CHUNK), 0) // CHUNK
    j = lax.broadcasted_iota(jnp.int32, (A_CHUNK, A_CHUNK), 1) // CHUNK
    return ((i <= j) if transposed else (i >= j)).astype(F32)


def _sgu_fwd(zpre, g_sgu, ws, bs_t, name):
    S, F2 = zpre.shape
    F = F2 // 2
    gd = F // A_GROUPS

    def body(zu_ref, zv_ref, g_ref, ws_ref, b_ref, o_ref):
        vhat, _ = _rms_hat(_gelu(zv_ref[...].astype(F32)))
        vn = (vhat * g_ref[...]).astype(BF16)
        u = _gelu(zu_ref[...].astype(F32))
        mask = _chunk_causal_mask(False)
        for gi in range(A_GROUPS):
            sl = slice(gi * gd, (gi + 1) * gd)
            wm = (ws_ref[gi] * mask).astype(BF16)
            vs = jnp.dot(wm, vn[:, sl], preferred_element_type=F32) + b_ref[:, gi:gi + 1]
            o_ref[:, sl] = (u[:, sl] * vs).astype(BF16)

    return pl.pallas_call(
        body, name=name, grid=(S // A_CHUNK,),
        in_specs=[pl.BlockSpec((A_CHUNK, F), lambda i: (i, 0)),
                  pl.BlockSpec((A_CHUNK, F), lambda i: (i, 1)),
                  pl.BlockSpec((1, F), lambda i: (0, 0)),
                  pl.BlockSpec((A_GROUPS, A_CHUNK, A_CHUNK), lambda i: (0, 0, 0)),
                  pl.BlockSpec((A_CHUNK, A_GROUPS), lambda i: (0, 0))],
        out_specs=pl.BlockSpec((A_CHUNK, F), lambda i: (i, 0)),
        out_shape=_sds((S, F), BF16), compiler_params=_cparams(1))(zpre, zpre, g_sgu, ws, bs_t)


def _sgu_bwd(zpre, duv, g_sgu, ws, ws_t, bs_t, name):
    S, F2 = zpre.shape
    F = F2 // 2
    gd = F // A_GROUPS

    def body(zu_ref, zv_ref, duv_ref, g_ref, ws_ref, wst_ref, b_ref, dz_ref, dg_ref, dws_ref, dbs_ref, dvn_ref):
        @pl.when(pl.program_id(0) == 0)
        def _():
            dg_ref[...] = jnp.zeros_like(dg_ref)
            dws_ref[...] = jnp.zeros_like(dws_ref)
            dbs_ref[...] = jnp.zeros_like(dbs_ref)

        gv = g_ref[...]
        u, u_grad = _gelu_and_grad(zu_ref[...].astype(F32))
        v, v_grad = _gelu_and_grad(zv_ref[...].astype(F32))
        vhat, r = _rms_hat(v)
        vn = (vhat * gv).astype(BF16)
        duv_v = duv_ref[...]
        dvs = duv_v * u
        dvs_b = dvs.astype(BF16)
        mask = _chunk_causal_mask(False)
        mask_t = _chunk_causal_mask(True)
        for gi in range(A_GROUPS):
            sl = slice(gi * gd, (gi + 1) * gd)
            wm = (ws_ref[gi] * mask).astype(BF16)
            vs = jnp.dot(wm, vn[:, sl], preferred_element_type=F32) + b_ref[:, gi:gi + 1]
            dz_ref[:, sl] = ((duv_v[:, sl] * vs) * u_grad[:, sl]).astype(BF16)
            dws_ref[gi] += lax.dot_general(dvs_b[:, sl], vn[:, sl], NT_DIMS, preferred_element_type=F32) * mask
            dbs_ref[gi] += jnp.broadcast_to(jnp.sum(dvs[:, sl], axis=1, keepdims=True), (A_CHUNK, A_CHUNK))
            wm_t = (wst_ref[gi] * mask_t).astype(BF16)
            dvn_ref[:, sl] = jnp.dot(wm_t, dvs_b[:, sl], preferred_element_type=F32)
        dv, dg_part = _rms_bwd(vhat, r, gv, dvn_ref[...])
        dg_ref[...] += jnp.sum(dg_part, axis=0, keepdims=True)
        dz_ref[:, F:] = (dv * v_grad).astype(BF16)

    blk = pl.BlockSpec((A_CHUNK, F), lambda i: (i, 0))
    const3 = pl.BlockSpec((A_GROUPS, A_CHUNK, A_CHUNK), lambda i: (0, 0, 0))
    return pl.pallas_call(
        body, name=name, grid=(S // A_CHUNK,),
        in_specs=[blk, pl.BlockSpec((A_CHUNK, F), lambda i: (i, 1)), blk,
                  pl.BlockSpec((1, F), lambda i: (0, 0)), const3, const3,
                  pl.BlockSpec((A_CHUNK, A_GROUPS), lambda i: (0, 0))],
        out_specs=[pl.BlockSpec((A_CHUNK, F2), lambda i: (i, 0)), pl.BlockSpec((1, F), lambda i: (0, 0)),
                   const3, const3],
        out_shape=[_sds((S, F2), BF16), _sds((1, F), F32), _sds((A_GROUPS, A_CHUNK, A_CHUNK), F32),
                   _sds((A_GROUPS, A_CHUNK, A_CHUNK), F32)],
        scratch_shapes=[pltpu.VMEM((A_CHUNK, F), F32)],
        compiler_params=_cparams(1))(zpre, zpre, duv, g_sgu, ws, ws_t, bs_t)


def _toeplitz_one_hot():
    row = lax.broadcasted_iota(jnp.int32, (TABLE_PAD, DIAGS), 0)
    j = lax.broadcasted_iota(jnp.int32, (TABLE_PAD, DIAGS), 1)
    idx = jnp.clip(KV_PAD + Q_BLOCK - j, -MAX_REL, MAX_REL) + MAX_REL
    return (row == idx).astype(F32)


def _rel_bias_fwd(table, name):
    H = table.shape[0]

    def body(t_ref, o_ref):
        diag = jnp.dot(t_ref[...], _toeplitz_one_hot(), precision=HIGHEST, preferred_element_type=F32)
        k_chunk = lax.broadcasted_iota(jnp.int32, (1, BAND), 1) // CHUNK

        def step(r, carry):
            q_chunk = r // CHUNK
            seen = (k_chunk >= q_chunk) & (k_chunk <= q_chunk + N_LEFT)
            o_ref[r] = pltpu.roll(diag, DIAGS - Q_BLOCK + r, 1)[:, :BAND] + jnp.where(seen, 0.0, NEG_INF)
            return carry

        lax.fori_loop(0, Q_BLOCK, step, 0)

    return pl.pallas_call(body, name=name, out_shape=_sds((Q_BLOCK, H, BAND), F32),
                          compiler_params=pltpu.CompilerParams(vmem_limit_bytes=VMEM_LIMIT_BYTES))(table)


def _rel_bias_bwd(dbias, name):
    H = dbias.shape[1]

    def body(d_ref, o_ref):
        def step(r, acc):
            row = jnp.concatenate([d_ref[r], jnp.zeros((H, DIAGS - BAND), F32)], axis=1)
            return acc + pltpu.roll(row, Q_BLOCK - r, 1)

        diag = lax.fori_loop(0, Q_BLOCK, step, jnp.zeros((H, DIAGS), F32))
        o_ref[...] = lax.dot_general(diag, _toeplitz_one_hot(), NT_DIMS, precision=HIGHEST,
                                     preferred_element_type=F32)

    return pl.pallas_call(body, name=name, out_shape=_sds((H, TABLE_PAD), F32),
                          compiler_params=pltpu.CompilerParams(vmem_limit_bytes=VMEM_LIMIT_BYTES))(dbias)


def _head_rows(t):
    lane = lax.broadcasted_iota(jnp.int32, t.shape, 1)
    zero = jnp.zeros_like(t)
    return jnp.concatenate([jnp.where(lane < HEAD_DIM, t, zero), jnp.where(lane >= HEAD_DIM, t, zero)], axis=0)


def _head_lanes(t2):
    lane = lax.broadcasted_iota(jnp.int32, (Q_BLOCK, t2.shape[1]), 1)
    return jnp.where(lane < HEAD_DIM, t2[:Q_BLOCK], t2[Q_BLOCK:])


def _attn_probs(q2, kb, bias2, block):
    kj = lax.broadcasted_iota(jnp.int32, (1, BAND), 1)
    before_start = jnp.where(block * Q_BLOCK + kj - KV_PAD >= 0, 0.0, NEG_INF)
    s = lax.dot_general(q2 * ATTN_SCALE, kb, NT_DIMS, preferred_element_type=F32) + bias2 + before_start
    e = jnp.exp(s - jnp.max(s, axis=-1, keepdims=True))
    return e / jnp.sum(e, axis=-1, keepdims=True)


def _attn_specs(S):
    lanes = HEADS_PER_BLOCK * HEAD_DIM
    rows = S + KV_PAD
    q_spec = pl.BlockSpec((BLOCKS_PER_STEP * Q_BLOCK, lanes), lambda h, i: (i, h))
    k_spec = pl.BlockSpec((rows, lanes), lambda h, i: (0, h))
    v_spec = pl.BlockSpec((rows, lanes), lambda h, i: (0, N_HEADS // HEADS_PER_BLOCK + h))
    b_spec = pl.BlockSpec((HEADS_PER_BLOCK, Q_BLOCK, BAND), lambda h, i: (h, 0, 0))
    return q_spec, k_spec, v_spec, b_spec


def _attn_fwd(q, kvp, bias, name):
    S, HD = q.shape
    q_spec, k_spec, v_spec, b_spec = _attn_specs(S)

    def body(q_ref, k_ref, v_ref, b_ref, o_ref):
        for b in range(BLOCKS_PER_STEP):
            block = pl.program_id(1) * BLOCKS_PER_STEP + b
            rows = slice(b * Q_BLOCK, (b + 1) * Q_BLOCK)
            band = pl.ds(pl.multiple_of(block * Q_BLOCK, Q_BLOCK), BAND)
            p = _attn_probs(_head_rows(q_ref[rows, :]), k_ref[band, :], b_ref[...].reshape(2 * Q_BLOCK, BAND), block)
            o2 = jnp.dot(p.astype(BF16), v_ref[band, :], preferred_element_type=F32)
            o_ref[rows, :] = _head_lanes(o2).astype(BF16)

    return pl.pallas_call(
        body, name=name, grid=(N_HEADS // HEADS_PER_BLOCK, S // (BLOCKS_PER_STEP * Q_BLOCK)),
        in_specs=[q_spec, k_spec, v_spec, b_spec], out_specs=q_spec,
        out_shape=_sds((S, HD), BF16), compiler_params=_cparams(2))(q, kvp, kvp, bias)


def _attn_bwd(q, kvp, bias, do, dkv_prev, name):
    S, HD = q.shape
    lanes = HEADS_PER_BLOCK * HEAD_DIM
    q_spec, k_spec, v_spec, b_spec = _attn_specs(S)
    dkv_spec = pl.BlockSpec((2, S + KV_PAD, lanes), lambda h, i: (0, 0, h))

    def body(q_ref, k_ref, v_ref, b_ref, do_ref, prev_ref, dq_ref, dkv_ref, db_ref):
        @pl.when(pl.program_id(1) == 0)
        def _():
            dkv_ref[...] = prev_ref[...]
            db_ref[...] = jnp.zeros_like(db_ref)

        db = jnp.zeros((2 * Q_BLOCK, BAND), F32)
        for b in range(BLOCKS_PER_STEP):
            block = pl.program_id(1) * BLOCKS_PER_STEP + b
            rows = slice(b * Q_BLOCK, (b + 1) * Q_BLOCK)
            band = pl.ds(pl.multiple_of(block * Q_BLOCK, Q_BLOCK), BAND)
            kb, vb = k_ref[band, :], v_ref[band, :]
            q2, do2 = _head_rows(q_ref[rows, :]), _head_rows(do_ref[rows, :])
            p = _attn_probs(q2, kb, b_ref[...].reshape(2 * Q_BLOCK, BAND), block)
            dp = lax.dot_general(do2, vb, NT_DIMS, preferred_element_type=F32)
            ds = p * (dp - jnp.sum(dp * p, axis=-1, keepdims=True))
            db = db + ds
            ds_b = (ds * ATTN_SCALE).astype(BF16)
            dq_ref[rows, :] = _head_lanes(jnp.dot(ds_b, kb, preferred_element_type=F32)).astype(BF16)
            dkv_ref[0, band, :] += lax.dot_general(ds_b, q2, TN_DIMS, preferred_element_type=F32)
            dkv_ref[1, band, :] += lax.dot_general(p.astype(BF16), do2, TN_DIMS, preferred_element_type=F32)
        db_ref[...] += db.reshape(HEADS_PER_BLOCK, Q_BLOCK, BAND)

    return pl.pallas_call(
        body, name=name, grid=(N_HEADS // HEADS_PER_BLOCK, S // (BLOCKS_PER_STEP * Q_BLOCK)),
        in_specs=[q_spec, k_spec, v_spec, b_spec, q_spec, dkv_spec], out_specs=[q_spec, dkv_spec, b_spec],
        out_shape=[_sds((S, HD), BF16), _sds((2, S + KV_PAD, HD), F32), _sds((N_HEADS, Q_BLOCK, BAND), F32)],
        compiler_params=_cparams(2))(q, kvp, kvp, bias, do, dkv_prev)


def _loss_head(x, g, target, name, tm=512):
    S, D = x.shape

    def body(x_ref, g_ref, t_ref, loss_ref, dx_ref, dg_ref):
        @pl.when(pl.program_id(0) == 0)
        def _():
            loss_ref[...] = jnp.zeros_like(loss_ref)
            dg_ref[...] = jnp.zeros_like(dg_ref)

        xhat, r = _rms_hat(x_ref[...])
        gv = g_ref[...]
        err = xhat * gv - t_ref[...]
        loss_ref[...] += 0.5 * jnp.sum(jnp.mean(err * err, axis=-1, keepdims=True))
        dx, dgp = _rms_bwd(xhat, r, gv, err * (1.0 / D))
        dx_ref[...] = dx
        dg_ref[...] += jnp.sum(dgp, axis=0, keepdims=True)

    row = pl.BlockSpec((tm, D), lambda i: (i, 0))
    vec = pl.BlockSpec((1, D), lambda i: (0, 0))
    return pl.pallas_call(
        body, name=name, grid=(S // tm,), in_specs=[row, vec, row],
        out_specs=[pl.BlockSpec((8, 128), lambda i: (0, 0)), row, vec],
        out_shape=[_sds((8, 128), F32), _sds((S, D), F32), _sds((1, D), F32)],
        compiler_params=_cparams(1))(x, g, target)


def _place():
    x, y, c = lax.axis_index("x"), lax.axis_index("y"), lax.axis_index("c")
    chips = [(1 - x, y), (x, 1 - y), (1 - x, 1 - y)]
    return x, y, c, chips


def _half_rows(c, r):
    return pl.ds(pl.multiple_of(c * (r // 2), 8), r // 2)


HBM_SPEC = pl.BlockSpec(memory_space=pl.ANY)


STRICT_HBM_SPEC = pl.BlockSpec(memory_space=pltpu.HBM)
SEM_SPEC = pl.BlockSpec(memory_space=pltpu.SEMAPHORE)
EFFECT = pltpu.SideEffectType.DATAFLOW_SIDE_EFFECTING


def _peers(x, y, c):
    out = []
    for k in range(1, N_DEV):
        px, py, pc = (x + ((k >> 2) & 1)) % 2, (y + ((k >> 1) & 1)) % 2, (c + (k & 1)) % 2
        out.append(((px, py, pc), 2 * px + py, pc, 4 * px + 2 * py + pc))
    return out


def _token_spec():
    return pl.BlockSpec(memory_space=pltpu.VMEM)


def _hbm(a):
    return pltpu.with_memory_space_constraint(a, pltpu.HBM)


def _allgather_start(slabs, name):
    n = len(slabs)

    def body(*refs):
        src, send, recv, token = refs[:n], refs[n], refs[n + 1], refs[-1]
        x, y, c, chips = _place()
        for a in range(n):
            own = src[a].at[2 * x + y]
            for j, chip in enumerate(chips):
                pltpu.make_async_remote_copy(src_ref=own, dst_ref=own, send_sem=send.at[3 * a + j], recv_sem=recv.at[3 * a + j],
                                             device_id=(*chip, c), device_id_type=MESH).start()
        token[...] = jnp.zeros_like(token)

    sems = pltpu.SemaphoreType.DMA((3 * n,))
    send, recv, *flying, token = pl.pallas_call(
        body, name=name, in_specs=[STRICT_HBM_SPEC] * n,
        out_shape=(sems, sems, *[pltpu.HBM(s.shape, s.dtype) for s in slabs], _sds((8, 128), F32)),
        out_specs=(SEM_SPEC, SEM_SPEC, *[STRICT_HBM_SPEC] * n, _token_spec()),
        input_output_aliases={a: a + 2 for a in range(n)},
        compiler_params=pltpu.CompilerParams(has_side_effects=EFFECT))(*[_hbm(s) for s in slabs])
    return send, recv, flying, token


def _allgather_wait(flying, send, recv, first, after, name):
    n = len(flying)

    def body(*refs):
        src, send_ref, recv_ref = refs[:n], refs[n], refs[n + 1]
        x, y, c, chips = _place()
        for a in range(n):
            for j, chip in enumerate(chips):
                cp = pltpu.make_async_remote_copy(
                    src_ref=src[a].at[2 * x + y], dst_ref=src[a].at[2 * chip[0] + chip[1]],
                    send_sem=send_ref.at[3 * (first + a) + j], recv_sem=recv_ref.at[3 * (first + a) + j],
                    device_id=(*chip, c), device_id_type=MESH)
                cp.wait_send()
                cp.wait_recv()

    return pl.pallas_call(
        body, name=name, in_specs=[STRICT_HBM_SPEC] * n + [SEM_SPEC, SEM_SPEC, HBM_SPEC],
        out_shape=tuple(pltpu.HBM(s.shape, s.dtype) for s in flying), out_specs=tuple([STRICT_HBM_SPEC] * n),
        input_output_aliases={a: a for a in range(n)},
        compiler_params=pltpu.CompilerParams(has_side_effects=EFFECT))(*flying, send, recv, after)


def _allgather_small(small, name):
    def body(sm, osm, send, recv, local):
        x, y, c, chips = _place()
        own = pltpu.make_async_copy(sm, osm.at[2 * x + y], local)
        own.start()
        cps = [pltpu.make_async_remote_copy(src_ref=sm, dst_ref=osm.at[2 * x + y], send_sem=send.at[j],
                                            recv_sem=recv.at[j], device_id=(*chip, c), device_id_type=MESH)
               for j, chip in enumerate(chips)]
        for cp in cps:
            cp.start()
        for j, chip in enumerate(chips):
            got = osm.at[2 * chip[0] + chip[1]]
            pltpu.make_async_remote_copy(src_ref=got, dst_ref=got, send_sem=send.at[j], recv_sem=recv.at[j],
                                         device_id=(x, y, c), device_id_type=MESH).wait_recv()
        for cp in cps:
            cp.wait_send()
        own.wait()

    return pl.pallas_call(
        body, name=name, in_specs=[pl.BlockSpec(memory_space=pltpu.VMEM)], out_specs=HBM_SPEC,
        out_shape=_sds((N_CHIPS, *small.shape), small.dtype),
        scratch_shapes=[pltpu.SemaphoreType.DMA((3,)), pltpu.SemaphoreType.DMA((3,)), pltpu.SemaphoreType.DMA])(small)


def _reduce_start(grads, name):
    n = len(grads)

    def body(*refs):
        src, land, send, recv, token = refs[:n], refs[n:2 * n], refs[2 * n], refs[2 * n + 1], refs[-1]
        x, y, c, _ = _place()
        me = 4 * x + 2 * y + c
        for a in range(n):
            for k, (peer, p_chip, p_core, _) in enumerate(_peers(x, y, c)):
                pltpu.make_async_remote_copy(
                    src_ref=src[a].at[p_chip, _half_rows(p_core, src[a].shape[1]), :], dst_ref=land[a].at[me],
                    send_sem=send.at[(N_DEV - 1) * a + k], recv_sem=recv.at[(N_DEV - 1) * a + k],
                    device_id=peer, device_id_type=MESH).start()
        token[...] = jnp.zeros_like(token)

    lands = [lax.empty((N_DEV, g.shape[1] // 2, g.shape[2]), BF16) for g in grads]
    sems = pltpu.SemaphoreType.DMA(((N_DEV - 1) * n,))
    shapes = [pltpu.HBM(a.shape, a.dtype) for a in grads + lands]
    send, recv, *flying, token = pl.pallas_call(
        body, name=name, in_specs=[STRICT_HBM_SPEC] * (2 * n),
        out_shape=(sems, sems, *shapes, _sds((8, 128), F32)),
        out_specs=(SEM_SPEC, SEM_SPEC, *[STRICT_HBM_SPEC] * (2 * n), _token_spec()),
        input_output_aliases={a: a + 2 for a in range(2 * n)},
        compiler_params=pltpu.CompilerParams(has_side_effects=EFFECT))(*[_hbm(a) for a in grads + lands])
    return send, recv, flying[:n], flying[n:], token


def _reduce_wait(started, after, name):
    sizes = [len(grads) for _, _, grads, _ in started]
    n_arr = 2 * sum(sizes)

    def body(*refs):
        x, y, c, _ = _place()
        at = 0
        for s, n in enumerate(sizes):
            src, land = refs[at:at + n], refs[at + n:at + 2 * n]
            send_ref, recv_ref = refs[n_arr + 2 * s], refs[n_arr + 2 * s + 1]
            at += 2 * n
            for a in range(n):
                for k, (peer, p_chip, p_core, p_dev) in enumerate(_peers(x, y, c)):
                    cp = pltpu.make_async_remote_copy(
                        src_ref=src[a].at[p_chip, _half_rows(p_core, src[a].shape[1]), :], dst_ref=land[a].at[p_dev],
                        send_sem=send_ref.at[(N_DEV - 1) * a + k], recv_sem=recv_ref.at[(N_DEV - 1) * a + k],
                        device_id=peer, device_id_type=MESH)
                    cp.wait_send()
                    cp.wait_recv()

    arrays, sems = [], []
    for send, recv, grads, lands in started:
        arrays += list(grads) + list(lands)
        sems += [send, recv]
    out = pl.pallas_call(
        body, name=name, in_specs=[STRICT_HBM_SPEC] * n_arr + [SEM_SPEC] * len(sems) + [HBM_SPEC],
        out_shape=tuple(pltpu.HBM(a.shape, a.dtype) for a in arrays), out_specs=tuple([STRICT_HBM_SPEC] * n_arr),
        input_output_aliases={a: a for a in range(n_arr)},
        compiler_params=pltpu.CompilerParams(has_side_effects=EFFECT))(*arrays, *sems, after)
    result, at = [], 0
    for n in sizes:
        result.append((out[at:at + n], out[at + n:at + 2 * n]))
        at += 2 * n
    return result


def _reduce_sum(grad, land, place, name):
    _, r2, C = land.shape
    tr = _row_tile(r2, C, 4, 1024 * 1024, 16)
    nb = r2 // tr

    def body(place_ref, own_ref, *rest):
        del place_ref
        acc = own_ref[...].astype(F32)
        for ref in rest[:N_DEV - 1]:
            acc = acc + ref[...].astype(F32)
        rest[-1][...] = acc

    def from_dev(k):
        return pl.BlockSpec((None, tr, C), lambda i, place_ref: ((place_ref[2] + k) % N_DEV, i, 0))

    grid_spec = pltpu.PrefetchScalarGridSpec(
        num_scalar_prefetch=1, grid=(nb,),
        in_specs=[pl.BlockSpec((None, tr, C), lambda i, place_ref: (place_ref[0], place_ref[1] * nb + i, 0))]
        + [from_dev(k) for k in range(1, N_DEV)],
        out_specs=pl.BlockSpec((tr, C), lambda i, place_ref: (place_ref[1] * nb + i, 0)))
    return pl.pallas_call(body, name=name, grid_spec=grid_spec, out_shape=_sds((2 * r2, C), F32),
                          compiler_params=_cparams(1))(place, grad, *[land] * (N_DEV - 1))


def _sibling_join(halves, name):
    n = len(halves)

    def body(*refs):
        out, send, recv = refs[n:2 * n], refs[2 * n], refs[2 * n + 1]
        x, y, c, _ = _place()
        cps = []
        for w in range(n):
            mine = out[w].at[_half_rows(c, out[w].shape[0]), :]
            cps.append(pltpu.make_async_remote_copy(src_ref=mine, dst_ref=mine, send_sem=send.at[w],
                                                    recv_sem=recv.at[w], device_id=(x, y, 1 - c), device_id_type=MESH))
        for cp in cps:
            cp.start()
        for w in range(n):
            theirs = out[w].at[_half_rows(1 - c, out[w].shape[0]), :]
            pltpu.make_async_remote_copy(src_ref=theirs, dst_ref=theirs, send_sem=send.at[w], recv_sem=recv.at[w],
                                         device_id=(x, y, c), device_id_type=MESH).wait_recv()
        for cp in cps:
            cp.wait_send()

    return pl.pallas_call(
        body, name=name, in_specs=[HBM_SPEC] * n, out_specs=[HBM_SPEC] * n,
        out_shape=[_sds(a.shape, F32) for a in halves], input_output_aliases={w: w for w in range(n)},
        scratch_shapes=[pltpu.SemaphoreType.DMA((n,)), pltpu.SemaphoreType.DMA((n,))])(*halves)


def _gather_small(packed, name):
    def body(p_ref, out, send, recv, local):
        x, y, c, _ = _place()
        me = 4 * x + 2 * y + c
        own = pltpu.make_async_copy(p_ref, out.at[me], local)
        own.start()
        cps = []
        for k in range(1, N_DEV):
            fx, fy, fc = (k >> 2) & 1, (k >> 1) & 1, k & 1
            peer = ((x + fx) % 2, (y + fy) % 2, (c + fc) % 2)
            cps.append(pltpu.make_async_remote_copy(src_ref=p_ref, dst_ref=out.at[me], send_sem=send.at[k - 1],
                                                    recv_sem=recv.at[k - 1], device_id=peer, device_id_type=MESH))
        for cp in cps:
            cp.start()
        for k in range(1, N_DEV):
            fx, fy, fc = (k >> 2) & 1, (k >> 1) & 1, k & 1
            src = out.at[4 * ((x + fx) % 2) + 2 * ((y + fy) % 2) + (c + fc) % 2]
            pltpu.make_async_remote_copy(src_ref=src, dst_ref=src, send_sem=send.at[k - 1], recv_sem=recv.at[k - 1],
                                         device_id=(x, y, c), device_id_type=MESH).wait_recv()
        for cp in cps:
            cp.wait_send()
        own.wait()

    return pl.pallas_call(
        body, name=name, in_specs=[pl.BlockSpec(memory_space=pltpu.VMEM)], out_specs=HBM_SPEC,
        out_shape=_sds((N_DEV, *packed.shape), F32),
        scratch_shapes=[pltpu.SemaphoreType.DMA((N_DEV - 1,)), pltpu.SemaphoreType.DMA((N_DEV - 1,)),
                        pltpu.SemaphoreType.DMA])(packed)


def _sum_devices(gathered, name):
    _, R, C = gathered.shape

    def body(g_ref, o_ref):
        acc = g_ref[0]
        for d in range(1, N_DEV):
            acc = acc + g_ref[d]
        o_ref[...] = acc

    tr = 8
    return pl.pallas_call(
        body, name=name, grid=(R // tr,), in_specs=[pl.BlockSpec((N_DEV, tr, C), lambda i: (0, i, 0))],
        out_specs=pl.BlockSpec((tr, C), lambda i: (i, 0)), out_shape=_sds((R, C), F32),
        compiler_params=_cparams(1))(gathered)


def _pack_small(arrays):
    rows = []
    for a in arrays:
        flat = a.reshape(-1)
        pad = (-flat.shape[0]) % SMALL_COLS
        rows.append(jnp.pad(flat, (0, pad)).reshape(-1, SMALL_COLS))
    packed = jnp.concatenate(rows, axis=0)
    return jnp.pad(packed, ((0, (-packed.shape[0]) % 8), (0, 0)))


def _unpack_small(packed, shapes):
    out, row = [], 0
    for shape in shapes:
        size = math.prod(shape)
        n_rows = -(-size // SMALL_COLS)
        out.append(packed[row:row + n_rows].reshape(-1)[:size].reshape(shape))
        row += n_rows
    return out


def kernel(x, a_norm, a_w_in, a_sgu_norm, a_w_spatial, a_b_spatial, a_w_out, kv_norm, w_kv, b_norm, b_w_q, b_rel_bias, b_w_o, ffn_norm, ffn_w_gate_up, ffn_w_down, final_norm, loss_target, m_a_norm, m_a_w_in, m_a_sgu_norm, m_a_w_spatial, m_a_b_spatial, m_a_w_out, m_kv_norm, m_w_kv, m_b_norm, m_b_w_q, m_b_rel_bias, m_b_w_o, m_ffn_norm, m_ffn_w_gate_up, m_ffn_w_down, m_final_norm, v_a_norm, v_a_w_in, v_a_sgu_norm, v_a_w_spatial, v_a_b_spatial, v_a_w_out, v_kv_norm, v_w_kv, v_b_norm, v_b_w_q, v_b_rel_bias, v_b_w_o, v_ffn_norm, v_ffn_w_gate_up, v_ffn_w_down, v_final_norm):
    S, D = x.shape[1], x.shape[2]
    n_a = a_w_in.shape[0]
    n_b = b_w_q.shape[0]
    depth = ffn_w_gate_up.shape[0]
    xi, yi, ci = lax.axis_index("x"), lax.axis_index("y"), lax.axis_index("c")
    chip = 2 * xi + yi

    place = jnp.stack([chip, ci, 2 * chip + ci]).astype(jnp.int32)
    stacked = {"a_w_in": a_w_in, "a_w_out": a_w_out, "w_kv": w_kv[None], "b_w_q": b_w_q, "b_w_o": b_w_o,
               "ffn_w_gate_up": ffn_w_gate_up, "ffn_w_down": ffn_w_down}
    groups = []
    for layer in range(depth):
        if layer < n_a:
            groups.append([("a_w_in", layer), ("a_w_out", layer)])
        elif layer == n_a:
            groups.append([("w_kv", 0), ("b_w_q", 0), ("b_w_o", 0)])
        else:
            groups.append([("b_w_q", layer - n_a), ("b_w_o", layer - n_a)])
        groups.append([("ffn_w_gate_up", layer), ("ffn_w_down", layer)])
    units = [u for group in groups for u in group]
    n_early = len(groups[0])
    slabs = [_cast_slab(stacked[k], l, place[:1], f"cast_{k}_{l}") for k, l in units[:n_early]]
    early = _allgather_start(slabs, "allgather_start_first")
    slabs = [_cast_slab(stacked[k], l, place[:1], f"cast_{k}_{l}", deps=(early[3],)) for k, l in units[n_early:]]
    late = _allgather_start(slabs, "allgather_start_rest")
    na_w, ns_w = a_norm.shape[1], a_sgu_norm.shape[1]
    small_g = _allgather_small(jnp.concatenate([a_norm, a_sgu_norm], axis=1), "allgather_small")
    a_norm_f = small_g[:, :, :na_w].transpose(1, 0, 2).reshape(n_a, N_CHIPS * na_w)
    a_sgu_f = small_g[:, :, na_w:].transpose(1, 0, 2).reshape(n_a, N_CHIPS * ns_w)
    W = {}

    def gathered(group_index, after):
        group = groups[group_index]
        (send, recv, flying, _), first = (early, 0) if group_index == 0 else (late, units.index(group[0]) - n_early)
        done = _allgather_wait(flying[first:first + len(group)], send, recv, first, after,
                               f"allgather_wait_{group_index}")
        W.update(zip(group, done))

    xc = x.reshape(S, D)
    saved = []
    kvp = x_kv = h_kv = None
    for layer in range(depth):
        rec = {"x_in": xc}
        gathered(2 * layer, xc)
        if layer < n_a:
            i = layer
            rec["zpre"], rec["h"] = _norm_matmul(xc, a_norm_f[i][None], W["a_w_in", i], BF16, f"a{i}_in")
            rec["uv"] = _sgu_fwd(rec["zpre"], a_sgu_f[i][None], a_w_spatial[i], a_b_spatial[i].T, f"a{i}_sgu")
            xm = _matmul_res(rec["uv"], W["a_w_out", i], xc, f"a{i}_out")
        else:
            i = layer - n_a
            if i == 0:
                kv, h_kv = _norm_matmul(xc, kv_norm[None], W["w_kv", 0], BF16, "kv_proj")
                kvp = jnp.pad(kv, ((KV_PAD, 0), (0, 0)))
                x_kv = xc
            rec["q"], rec["h"] = _norm_matmul(xc, b_norm[i][None], W["b_w_q", i], BF16, f"b{i}_q", row_sharded=True)
            table = jnp.pad(b_rel_bias[i], ((0, 0), (0, TABLE_PAD - b_rel_bias.shape[2])))
            rec["bias"] = _rel_bias_fwd(table, f"b{i}_bias").transpose(1, 0, 2)
            rec["o"] = _attn_fwd(rec["q"], kvp, rec["bias"], f"b{i}_attn")
            xm = _matmul_res(rec["o"], W["b_w_o", i], xc, f"b{i}_o")
        rec["x_mid"] = xm
        gathered(2 * layer + 1, xm)
        rec["gu"], rec["h_f"] = _norm_matmul(xm, ffn_norm[layer][None], W["ffn_w_gate_up", layer], BF16, f"f{layer}_in")
        xc = _matmul_res(rec["gu"], W["ffn_w_down", layer], xm, f"f{layer}_out", swiglu=True)
        saved.append(rec)

    loss_tile, dx, d_final = _loss_head(xc, final_norm[None], loss_target.reshape(S, D), "loss_head")
    loss = lax.psum(loss_tile[0, 0], ("x", "y", "c"))

    started = []

    def weight_grad(unit, **kw):
        full = (N_CHIPS,) + tuple(stacked[unit[0]].shape[1:])
        g = _matmul_tn(out_shape=full, name=f"d_{unit[0]}_{unit[1]}", **kw)
        send, recv, flying_g, flying_land, token = _reduce_start([g], f"reduce_start_{unit[0]}_{unit[1]}")
        started.append((unit, send, recv, flying_g, flying_land))
        return token

    tt = 512
    tb = min(NORMBWD_ROWS, S)
    row_a = lambda w: pl.BlockSpec((tt, w), lambda o, t: (t, 0))
    d_ffn_norm, d_b_norm, d_a_norm, d_a_sgu = [None] * depth, [None] * n_b, [None] * n_a, [None] * n_a
    d_ws, d_bs, d_rel = [None] * n_a, [None] * n_a, [None] * n_b
    dkv = jnp.zeros((2, S + KV_PAD, D), F32)
    first = lambda ref: ref[...]
    for layer in reversed(range(depth)):
        rec = saved[layer]
        r_d = ffn_w_down.shape[1]
        half_f = 2 * r_d
        token = weight_grad(
            ("ffn_w_down", layer), a_ops=[rec["gu"], rec["gu"]],
            a_specs=[pl.BlockSpec((tt, half_f), lambda o, t: (t, o)), pl.BlockSpec((tt, half_f), lambda o, t: (t, 2 + o))],
            a_fn=lambda g_ref, u_ref: _swiglu(g_ref[...].astype(F32), u_ref[...].astype(F32)).astype(BF16),
            b_op=dx, b_spec=row_a(D), out_spec=pl.BlockSpec((2, r_d, D), lambda o, t: (o, 0, 0)),
            acc_shape=(half_f, D), n_outer=2)
        dgu = _nt_swiglu_bwd(dx, W["ffn_w_down", layer], rec["gu"], f"f{layer}_dgu", deps=(token,))
        nsh = ffn_w_gate_up.shape[2]
        token = weight_grad(
            ("ffn_w_gate_up", layer), a_ops=[rec["h_f"]], a_specs=[row_a(D)], a_fn=first,
            b_op=dgu, b_spec=pl.BlockSpec((None, tt, nsh), lambda o, t: (o // 2, t, o % 2)),
            out_spec=pl.BlockSpec((None, D, nsh), lambda o, t: (o, 0, 0)), acc_shape=(D, nsh), n_outer=N_CHIPS)
        dx, d_ffn_norm[layer] = _nt_normbwd(
            dgu, pl.BlockSpec((None, tb, nsh), lambda i, k: (k // 2, i, k % 2)),
            W["ffn_w_gate_up", layer], pl.BlockSpec((None, D, nsh), lambda i, k: (k, 0, 0)),
            (D, nsh), N_CHIPS, rec["x_mid"], ffn_norm[layer][None], dx, f"f{layer}_dx", deps=(token,))
        if layer >= n_a:
            i = layer - n_a
            r_o = b_w_o.shape[1]
            token = weight_grad(
                ("b_w_o", i), a_ops=[rec["o"]], a_specs=[row_a(D)], a_fn=first, b_op=dx, b_spec=row_a(D),
                out_spec=pl.BlockSpec((N_CHIPS, r_o, D), lambda o, t: (0, 0, 0)), acc_shape=(D, D), n_outer=1)
            do = _nt_rows(dx, W["b_w_o", i], N_CHIPS, BF16, f"b{i}_do", deps=(token,))
            dq, dkv, dbias = _attn_bwd(rec["q"], kvp, rec["bias"], do, dkv, f"b{i}_attn_bwd")
            d_rel[i] = _rel_bias_bwd(dbias.transpose(1, 0, 2), f"b{i}_dbias")[:, :b_rel_bias.shape[2]]
            token = weight_grad(
                ("b_w_q", i), a_ops=[rec["h"]], a_specs=[row_a(D)], a_fn=first, b_op=dq, b_spec=row_a(D),
                out_spec=pl.BlockSpec((N_CHIPS, r_o, D), lambda o, t: (0, 0, 0)), acc_shape=(D, D), n_outer=1)
            dx, d_b_norm[i] = _nt_normbwd(
                dq, pl.BlockSpec((tb, D), lambda i_, k: (i_, 0)),
                W["b_w_q", i], pl.BlockSpec((N_CHIPS, r_o, D), lambda i_, k: (0, 0, 0)),
                (D, D), 1, rec["x_in"], b_norm[i][None], dx, f"b{i}_dx", deps=(token,))
            if i == 0:
                dkv_b = dkv[:, KV_PAD:, :].astype(BF16)
                n_kv = w_kv.shape[1]
                token = weight_grad(
                    ("w_kv", 0), a_ops=[h_kv], a_specs=[row_a(D)], a_fn=first,
                    b_op=dkv_b, b_spec=pl.BlockSpec((None, tt, n_kv), lambda o, t: (o // 2, t, o % 2)),
                    out_spec=pl.BlockSpec((None, D, n_kv), lambda o, t: (o, 0, 0)), acc_shape=(D, n_kv),
                    n_outer=N_CHIPS)
                dx, d_kv_norm = _nt_normbwd(
                    dkv_b, pl.BlockSpec((None, tb, n_kv), lambda i_, k: (k // 2, i_, k % 2)),
                    W["w_kv", 0], pl.BlockSpec((None, D, n_kv), lambda i_, k: (k, 0, 0)),
                    (D, n_kv), N_CHIPS, x_kv, kv_norm[None], dx, "kv_dx", deps=(token,))
        else:
            i = layer
            r_w = a_w_out.shape[1]
            token = weight_grad(
                ("a_w_out", i), a_ops=[rec["uv"]], a_specs=[row_a(N_CHIPS * r_w)], a_fn=first,
                b_op=dx, b_spec=row_a(D), out_spec=pl.BlockSpec((N_CHIPS, r_w, D), lambda o, t: (0, 0, 0)),
                acc_shape=(N_CHIPS * r_w, D), n_outer=1)
            duv = _nt_rows(dx, W["a_w_out", i], 2, F32, f"a{i}_duv", deps=(token,))
            dz, d_a_sgu[i], d_ws[i], dbs = _sgu_bwd(rec["zpre"], duv, a_sgu_f[i][None], a_w_spatial[i],
                                                  a_w_spatial[i].transpose(0, 2, 1), a_b_spatial[i].T, f"a{i}_sgu_bwd")
            d_bs[i] = dbs[:, :, 0]
            n_in = a_w_in.shape[2]
            token = weight_grad(
                ("a_w_in", i), a_ops=[rec["h"]], a_specs=[row_a(D)], a_fn=first,
                b_op=dz, b_spec=pl.BlockSpec((tt, n_in), lambda o, t: (t, o)),
                out_spec=pl.BlockSpec((None, D, n_in), lambda o, t: (o, 0, 0)), acc_shape=(D, n_in), n_outer=N_CHIPS)
            dx, d_a_norm[i] = _nt_normbwd(
                dz, pl.BlockSpec((tb, n_in), lambda i_, k: (i_, k)),
                W["a_w_in", i], pl.BlockSpec((None, D, n_in), lambda i_, k: (k, 0, 0)),
                (D, n_in), N_CHIPS, rec["x_in"], a_norm_f[i][None], dx, f"a{i}_dx", deps=(token,))
    grad_x = dx.reshape(x.shape)

    landed = _reduce_wait([(send, recv, g, land) for _, send, recv, g, land in started], dx, "reduce_wait")
    halves = [_reduce_sum(g[0], land[0], place, f"reduce_sum_{unit[0]}_{unit[1]}")
              for (unit, *_), (g, land) in zip(started, landed)]
    joined = dict(zip([s[0] for s in started], _sibling_join(halves, "reduce_join")))
    reduced = {}

    small = [jnp.concatenate(d_a_norm, axis=0), jnp.concatenate(d_a_sgu, axis=0), jnp.stack(d_ws), jnp.stack(d_bs),
             d_kv_norm, jnp.concatenate(d_b_norm, axis=0), jnp.stack(d_rel), jnp.concatenate(d_ffn_norm, axis=0), d_final]
    total = _sum_devices(_gather_small(_pack_small(small), "gather_small_grads"), "sum_small_grads")
    (g_a_norm, g_a_sgu, g_ws, g_bs, g_kv_norm, g_b_norm, g_rel, g_ffn_norm, g_final) = _unpack_small(
        total, [a.shape for a in small])
    reduced["a_norm"] = lax.dynamic_slice_in_dim(g_a_norm, chip * na_w, na_w, axis=1)
    reduced["a_sgu_norm"] = lax.dynamic_slice_in_dim(g_a_sgu, chip * ns_w, ns_w, axis=1)
    reduced.update(a_w_spatial=g_ws, a_b_spatial=g_bs, kv_norm=g_kv_norm.reshape(kv_norm.shape), b_norm=g_b_norm,
                   b_rel_bias=g_rel, ffn_norm=g_ffn_norm, final_norm=g_final.reshape(final_norm.shape))

    weights = dict(a_norm=a_norm, a_w_in=a_w_in, a_sgu_norm=a_sgu_norm, a_w_spatial=a_w_spatial,
                   a_b_spatial=a_b_spatial, a_w_out=a_w_out, kv_norm=kv_norm, w_kv=w_kv, b_norm=b_norm, b_w_q=b_w_q,
                   b_rel_bias=b_rel_bias, b_w_o=b_w_o, ffn_norm=ffn_norm, ffn_w_gate_up=ffn_w_gate_up,
                   ffn_w_down=ffn_w_down, final_norm=final_norm)
    m_in = dict(a_norm=m_a_norm, a_w_in=m_a_w_in, a_sgu_norm=m_a_sgu_norm, a_w_spatial=m_a_w_spatial,
                a_b_spatial=m_a_b_spatial, a_w_out=m_a_w_out, kv_norm=m_kv_norm, w_kv=m_w_kv, b_norm=m_b_norm,
                b_w_q=m_b_w_q, b_rel_bias=m_b_rel_bias, b_w_o=m_b_w_o, ffn_norm=m_ffn_norm,
                ffn_w_gate_up=m_ffn_w_gate_up, ffn_w_down=m_ffn_w_down, final_norm=m_final_norm)
    v_in = dict(a_norm=v_a_norm, a_w_in=v_a_w_in, a_sgu_norm=v_a_sgu_norm, a_w_spatial=v_a_w_spatial,
                a_b_spatial=v_a_b_spatial, a_w_out=v_a_w_out, kv_norm=v_kv_norm, w_kv=v_w_kv, b_norm=v_b_norm,
                b_w_q=v_b_w_q, b_rel_bias=v_b_rel_bias, b_w_o=v_b_w_o, ffn_norm=v_ffn_norm,
                ffn_w_gate_up=v_ffn_w_gate_up, ffn_w_down=v_ffn_w_down, final_norm=v_final_norm)
    grad_out, delta_out, m_out, v_out = [], [], [], []
    for key, w in weights.items():
        if key in stacked:
            as_layers = lambda a: a.reshape(stacked[key].shape)
            bufs = None
            for layer in range(stacked[key].shape[0]):
                bufs = _adamw_layer(as_layers(w), joined[key, layer], as_layers(m_in[key]), as_layers(v_in[key]),
                                    layer, bufs, f"adamw_{key}_{layer}")
            g, d, nm, nv = bufs
        else:
            g = reduced[key].reshape(w.shape)
            view = (1, w.shape[0]) if w.ndim == 1 else (-1, w.shape[-1])
            d, nm, nv = _adamw(w.reshape(view), g.reshape(view), m_in[key].reshape(view), v_in[key].reshape(view),
                               "adamw_" + key)
        grad_out.append(g.reshape(w.shape))
        delta_out.append(d.reshape(w.shape))
        m_out.append(nm.reshape(w.shape))
        v_out.append(nv.reshape(w.shape))
    return (loss, grad_x, *grad_out, *delta_out, *m_out, *v_out)
```

```python
import math

import jax
import jax.numpy as jnp
from jax import lax
from jax.experimental import pallas as pl
from jax.experimental.pallas import tpu as pltpu

F32, BF16 = jnp.float32, jnp.bfloat16
MESH = pl.DeviceIdType.MESH
HIGHEST = lax.Precision.HIGHEST
NT_DIMS = (((1,), (1,)), ((), ()))
TN_DIMS = (((0,), (0,)), ((), ()))

EPS = 1e-6
CHUNK = 64
A_CHUNK = 128
A_GROUPS = 8
N_HEADS = 16
HEAD_DIM = 64
N_LEFT = 8
MAX_REL = 256
ATTN_SCALE = HEAD_DIM ** -0.5
NEG_INF = -1e30
Q_BLOCK = 2 * CHUNK
KV_PAD = N_LEFT * CHUNK
BAND = KV_PAD + Q_BLOCK
DIAGS = BAND + Q_BLOCK
TABLE_PAD = 640
HEADS_PER_BLOCK = 2
BLOCKS_PER_STEP = 2
NORMBWD_ROWS = 1024

ADAM_LR, ADAM_B1, ADAM_B2, ADAM_EPS, ADAM_WD, ADAM_STEP = 0.001, 0.9, 0.999, 1e-08, 0.01, 10

VMEM_LIMIT_BYTES = 56 * 1024 * 1024
N_CHIPS = 4
N_DEV = 8
SMALL_COLS = 1024


def _cparams(n_grid):
    return pltpu.CompilerParams(dimension_semantics=("arbitrary",) * n_grid, vmem_limit_bytes=VMEM_LIMIT_BYTES)


def _sds(shape, dtype):
    return jax.ShapeDtypeStruct(tuple(shape), dtype)


def _gelu(x):
    return x * (0.5 * (1.0 + lax.erf(x * math.sqrt(0.5))))


def _gelu_and_grad(x):
    cdf = 0.5 * (1.0 + lax.erf(x * math.sqrt(0.5)))
    return x * cdf, cdf + x * (jnp.exp(-0.5 * x * x) * (1.0 / math.sqrt(2.0 * math.pi)))


def _rms_hat(xv):
    r = lax.rsqrt(jnp.mean(xv * xv, axis=-1, keepdims=True) + EPS)
    return xv * r, r


def _rms_bwd(xhat, r, g, dy):
    dxhat = dy * g
    dx = r * (dxhat - xhat * jnp.mean(dxhat * xhat, axis=-1, keepdims=True))
    return dx, dy * xhat


def _swiglu(gate, up):
    return (gate * jax.nn.sigmoid(gate)) * up


def _row_tile(rows, cols, itemsize, cap_bytes, align):
    t = rows
    while t * cols * itemsize > cap_bytes and t % (2 * align) == 0:
        t //= 2
    return t


def _cast_slab(w, layer, chip, name, deps=()):
    _, r, C = w.shape
    tr = _row_tile(r, C, 4, 4 * 1024 * 1024, 16)

    def body(chip_ref, w_ref, *rest):
        del chip_ref
        rest[-1][...] = w_ref[...].astype(BF16)

    grid_spec = pltpu.PrefetchScalarGridSpec(
        num_scalar_prefetch=1, grid=(r // tr,),
        in_specs=[pl.BlockSpec((None, tr, C), lambda i, chip_ref: (layer, i, 0))] + [HBM_SPEC] * len(deps),
        out_specs=pl.BlockSpec((None, tr, C), lambda i, chip_ref: (chip_ref[0], i, 0)))
    return pl.pallas_call(body, name=name, grid_spec=grid_spec, out_shape=_sds((N_CHIPS, r, C), BF16),
                          compiler_params=_cparams(1))(chip, w, *deps)


def _adamw_layer(w, g, m, v, layer, bufs, name):
    L, r, C = w.shape
    tr = _row_tile(r, C, 4, 1024 * 1024, 8)

    def body(w_ref, g_ref, m_ref, v_ref, *rest):
        go_ref, d_ref, nm_ref, nv_ref = rest[-4:]
        gv = g_ref[...]
        mn = ADAM_B1 * m_ref[...] + (1.0 - ADAM_B1) * gv
        vn = ADAM_B2 * v_ref[...] + (1.0 - ADAM_B2) * jnp.square(gv)
        m_hat = mn / (1.0 - ADAM_B1 ** ADAM_STEP)
        v_hat = vn / (1.0 - ADAM_B2 ** ADAM_STEP)
        d_ref[...] = -ADAM_LR * (m_hat / (jnp.sqrt(v_hat) + ADAM_EPS) + ADAM_WD * w_ref[...])
        nm_ref[...] = mn
        nv_ref[...] = vn
        go_ref[...] = gv

    stacked = pl.BlockSpec((None, tr, C), lambda i: (layer, i, 0))
    in_specs = [stacked, pl.BlockSpec((tr, C), lambda i: (i, 0)), stacked, stacked]
    ops = [w, g, m, v]
    aliases = {}
    if bufs is not None:
        in_specs += [HBM_SPEC] * 4
        ops += list(bufs)
        aliases = {4 + k: k for k in range(4)}
    return pl.pallas_call(body, name=name, grid=(r // tr,), in_specs=in_specs, out_specs=[stacked] * 4,
                          out_shape=[_sds((L, r, C), F32)] * 4, input_output_aliases=aliases,
                          compiler_params=_cparams(1))(*ops)


def _adamw(w, g, m, v, name):
    R, C = w.shape
    tr = _row_tile(R, C, 4, 1024 * 1024, 8)

    def body(w_ref, g_ref, m_ref, v_ref, d_ref, nm_ref, nv_ref):
        gv = g_ref[...]
        mn = ADAM_B1 * m_ref[...] + (1.0 - ADAM_B1) * gv
        vn = ADAM_B2 * v_ref[...] + (1.0 - ADAM_B2) * jnp.square(gv)
        m_hat = mn / (1.0 - ADAM_B1 ** ADAM_STEP)
        v_hat = vn / (1.0 - ADAM_B2 ** ADAM_STEP)
        d_ref[...] = -ADAM_LR * (m_hat / (jnp.sqrt(v_hat) + ADAM_EPS) + ADAM_WD * w_ref[...])
        nm_ref[...] = mn
        nv_ref[...] = vn

    spec = pl.BlockSpec((tr, C), lambda i: (i, 0))
    return pl.pallas_call(body, name=name, grid=(R // tr,), in_specs=[spec] * 4, out_specs=[spec] * 3,
                          out_shape=[_sds((R, C), F32)] * 3, compiler_params=_cparams(1))(w, g, m, v)


def _norm_matmul(x, g, w_g, out_dtype, name, row_sharded=False, tm=1024):
    S, D = x.shape
    tm = min(tm, S)
    if row_sharded:
        r, N = w_g.shape[1], w_g.shape[2]
        tn = 512
        w_spec = pl.BlockSpec((N_CHIPS, r, tn), lambda i, j: (0, 0, j))
    else:
        nsh = w_g.shape[2]
        N = N_CHIPS * nsh
        tn = 512 if nsh % 512 == 0 else nsh
        bps = nsh // tn
        w_spec = pl.BlockSpec((None, D, tn), lambda i, j: (j // bps, 0, j % bps))

    def body(x_ref, g_ref, w_ref, y_ref, h_ref):
        @pl.when(pl.program_id(1) == 0)
        def _():
            xhat, _ = _rms_hat(x_ref[...])
            h_ref[...] = (xhat * g_ref[...]).astype(BF16)

        w = w_ref[...].reshape(D, tn)
        y_ref[...] = jnp.dot(h_ref[...], w, preferred_element_type=F32).astype(y_ref.dtype)

    return pl.pallas_call(
        body, name=name, grid=(S // tm, N // tn),
        in_specs=[pl.BlockSpec((tm, D), lambda i, j: (i, 0)), pl.BlockSpec((1, D), lambda i, j: (0, 0)), w_spec],
        out_specs=[pl.BlockSpec((tm, tn), lambda i, j: (i, j)), pl.BlockSpec((tm, D), lambda i, j: (i, 0))],
        out_shape=[_sds((S, N), out_dtype), _sds((S, D), BF16)],
        compiler_params=_cparams(2))(x, g, w_g)


def _matmul_res(a, w_g, res, name, swiglu=False, tm=256):
    S, N = res.shape
    r = w_g.shape[1]
    K = N_CHIPS * r

    def body(*refs):
        if swiglu:
            gate_ref, up_ref, w_ref, res_ref, o_ref = refs
            a_blk = _swiglu(gate_ref[...].astype(F32), up_ref[...].astype(F32)).astype(BF16)
        else:
            a_ref, w_ref, res_ref, o_ref = refs
            a_blk = a_ref[...]
        o_ref[...] = res_ref[...] + jnp.dot(a_blk, w_ref[...].reshape(K, N), preferred_element_type=F32)

    a_specs, a_ops = [pl.BlockSpec((tm, K), lambda i: (i, 0))], [a]
    if swiglu:
        a_specs.append(pl.BlockSpec((tm, K), lambda i: (i, 1)))
        a_ops.append(a)
    row = pl.BlockSpec((tm, N), lambda i: (i, 0))
    return pl.pallas_call(
        body, name=name, grid=(S // tm,),
        in_specs=a_specs + [pl.BlockSpec((N_CHIPS, r, N), lambda i: (0, 0, 0)), row],
        out_specs=row, out_shape=_sds((S, N), F32), compiler_params=_cparams(1))(*a_ops, w_g, res)


def _matmul_tn(a_ops, a_specs, a_fn, b_op, b_spec, out_spec, out_shape, acc_shape, n_outer, name, deps=(), tt=512):
    S = b_op.shape[-2]
    na = len(a_ops)
    nt = S // tt

    def body(*refs):
        a_refs, b_ref, o_ref, acc_ref = refs[:na], refs[na], refs[-2], refs[-1]
        t = pl.program_id(1)
        part = lax.dot_general(a_fn(*a_refs), b_ref[...].astype(BF16), TN_DIMS, preferred_element_type=F32)

        @pl.when(t == 0)
        def _():
            acc_ref[...] = part

        @pl.when(t > 0)
        def _():
            acc_ref[...] += part

        @pl.when(t == nt - 1)
        def _():
            o_ref[...] = acc_ref[...].reshape(o_ref.shape).astype(BF16)

    return pl.pallas_call(
        body, name=name, grid=(n_outer, nt), in_specs=list(a_specs) + [b_spec] + [HBM_SPEC] * len(deps),
        out_specs=out_spec, out_shape=_sds(out_shape, BF16), scratch_shapes=[pltpu.VMEM(acc_shape, F32)],
        compiler_params=_cparams(2))(*a_ops, b_op, *deps)


def _nt_accumulate(a_ref, w_ref, acc_ref, w2d, nk):
    k = pl.program_id(1)
    part = lax.dot_general(a_ref[...].astype(BF16), w_ref[...].reshape(w2d), NT_DIMS, preferred_element_type=F32)

    @pl.when(k == 0)
    def _():
        acc_ref[...] = part

    @pl.when(k > 0)
    def _():
        acc_ref[...] += part

    return k == nk - 1


def _nt_normbwd(dy, dy_spec, w_g, w_spec, w2d, nk, x, g, dres, name, deps=(), tm=NORMBWD_ROWS):
    S, D = x.shape
    tm = min(tm, S)

    def body(dy_ref, w_ref, x_ref, g_ref, dres_ref, *rest):
        dx_ref, dg_ref, acc_ref = rest[-3:]

        @pl.when((pl.program_id(0) == 0) & (pl.program_id(1) == 0))
        def _():
            dg_ref[...] = jnp.zeros_like(dg_ref)

        last = _nt_accumulate(dy_ref, w_ref, acc_ref, w2d, nk)

        @pl.when(last)
        def _():
            xhat, r = _rms_hat(x_ref[...])
            dx, dgp = _rms_bwd(xhat, r, g_ref[...], acc_ref[...])
            dx_ref[...] = dres_ref[...] + dx
            dg_ref[...] += jnp.sum(dgp, axis=0, keepdims=True)

    row = pl.BlockSpec((tm, D), lambda i, k: (i, 0))
    vec = pl.BlockSpec((1, D), lambda i, k: (0, 0))
    return pl.pallas_call(
        body, name=name, grid=(S // tm, nk),
        in_specs=[dy_spec, w_spec, row, vec, row] + [HBM_SPEC] * len(deps), out_specs=[row, vec],
        out_shape=[_sds((S, D), F32), _sds((1, D), F32)],
        scratch_shapes=[pltpu.VMEM((tm, D), F32)], compiler_params=_cparams(2))(dy, w_g, x, g, dres, *deps)


def _nt_rows(dy, w_g, shards_per_block, out_dtype, name, deps=(), tm=1024):
    S, N = dy.shape
    tm = min(tm, S)
    r = w_g.shape[1]
    tn = shards_per_block * r

    def body(dy_ref, w_ref, *rest):
        o_ref = rest[-1]
        o_ref[...] = lax.dot_general(dy_ref[...].astype(BF16), w_ref[...].reshape(tn, N), NT_DIMS,
                                     preferred_element_type=F32).astype(o_ref.dtype)

    return pl.pallas_call(
        body, name=name, grid=(S // tm, N_CHIPS // shards_per_block),
        in_specs=[pl.BlockSpec((tm, N), lambda i, j: (i, 0)),
                  pl.BlockSpec((shards_per_block, r, N), lambda i, j: (j, 0, 0))] + [HBM_SPEC] * len(deps),
        out_specs=pl.BlockSpec((tm, tn), lambda i, j: (i, j)),
        out_shape=_sds((S, N_CHIPS * r), out_dtype), compiler_params=_cparams(2))(dy, w_g, *deps)


def _nt_swiglu_bwd(dy, w_g, gu, name, deps=(), tm=512):
    S, N = dy.shape
    r = w_g.shape[1]
    tn = 2 * r
    F = N_CHIPS * r

    def body(dy_ref, w_ref, gate_ref, up_ref, *rest):
        o_ref = rest[-1]
        dact = lax.dot_general(dy_ref[...].astype(BF16), w_ref[...].reshape(tn, N), NT_DIMS,
                               preferred_element_type=F32)
        gate, up = gate_ref[...].astype(F32), up_ref[...].astype(F32)
        sg = jax.nn.sigmoid(gate)
        silu = gate * sg
        o_ref[0] = ((dact * up) * (sg + silu * (1.0 - sg))).astype(BF16)
        o_ref[1] = (dact * silu).astype(BF16)

    return pl.pallas_call(
        body, name=name, grid=(2, S // tm),
        in_specs=[pl.BlockSpec((tm, N), lambda j, i: (i, 0)),
                  pl.BlockSpec((2, r, N), lambda j, i: (j, 0, 0)),
                  pl.BlockSpec((tm, tn), lambda j, i: (i, j)),
                  pl.BlockSpec((tm, tn), lambda j, i: (i, 2 + j))] + [HBM_SPEC] * len(deps),
        out_specs=pl.BlockSpec((2, tm, tn), lambda j, i: (0, i, j)),
        out_shape=_sds((2, S, F), BF16), compiler_params=_cparams(2))(dy, w_g, gu, gu, *deps)


def _chunk_causal_mask(transposed):
    i = lax.broadcasted_iota(jnp.int32, (A_CHUNK, A_CHUNK), 0) // CHUNK
    j = lax.broadcasted_iota(jnp.int32, (A_CHUNK, A_CHUNK), 1) // CHUNK
    return ((i <= j) if transposed else (i >= j)).astype(F32)


def _sgu_fwd(zpre, g_sgu, ws, bs_t, name):
    S, F2 = zpre.shape
    F = F2 // 2
    gd = F // A_GROUPS

    def body(zu_ref, zv_ref, g_ref, ws_ref, b_ref, o_ref):
        vhat, _ = _rms_hat(_gelu(zv_ref[...].astype(F32)))
        vn = (vhat * g_ref[...]).astype(BF16)
        u = _gelu(zu_ref[...].astype(F32))
        mask = _chunk_causal_mask(False)
        for gi in range(A_GROUPS):
            sl = slice(gi * gd, (gi + 1) * gd)
            wm = (ws_ref[gi] * mask).astype(BF16)
            vs = jnp.dot(wm, vn[:, sl], preferred_element_type=F32) + b_ref[:, gi:gi + 1]
            o_ref[:, sl] = (u[:, sl] * vs).astype(BF16)

    return pl.pallas_call(
        body, name=name, grid=(S // A_CHUNK,),
        in_specs=[pl.BlockSpec((A_CHUNK, F), lambda i: (i, 0)),
                  pl.BlockSpec((A_CHUNK, F), lambda i: (i, 1)),
                  pl.BlockSpec((1, F), lambda i: (0, 0)),
                  pl.BlockSpec((A_GROUPS, A_CHUNK, A_CHUNK), lambda i: (0, 0, 0)),
                  pl.BlockSpec((A_CHUNK, A_GROUPS), lambda i: (0, 0))],
        out_specs=pl.BlockSpec((A_CHUNK, F), lambda i: (i, 0)),
        out_shape=_sds((S, F), BF16), compiler_params=_cparams(1))(zpre, zpre, g_sgu, ws, bs_t)


def _sgu_bwd(zpre, duv, g_sgu, ws, ws_t, bs_t, name):
    S, F2 = zpre.shape
    F = F2 // 2
    gd = F // A_GROUPS

    def body(zu_ref, zv_ref, duv_ref, g_ref, ws_ref, wst_ref, b_ref, dz_ref, dg_ref, dws_ref, dbs_ref, dvn_ref):
        @pl.when(pl.program_id(0) == 0)
        def _():
            dg_ref[...] = jnp.zeros_like(dg_ref)
            dws_ref[...] = jnp.zeros_like(dws_ref)
            dbs_ref[...] = jnp.zeros_like(dbs_ref)

        gv = g_ref[...]
        u, u_grad = _gelu_and_grad(zu_ref[...].astype(F32))
        v, v_grad = _gelu_and_grad(zv_ref[...].astype(F32))
        vhat, r = _rms_hat(v)
        vn = (vhat * gv).astype(BF16)
        duv_v = duv_ref[...]
        dvs = duv_v * u
        dvs_b = dvs.astype(BF16)
        mask = _chunk_causal_mask(False)
        mask_t = _chunk_causal_mask(True)
        for gi in range(A_GROUPS):
            sl = slice(gi * gd, (gi + 1) * gd)
            wm = (ws_ref[gi] * mask).astype(BF16)
            vs = jnp.dot(wm, vn[:, sl], preferred_element_type=F32) + b_ref[:, gi:gi + 1]
            dz_ref[:, sl] = ((duv_v[:, sl] * vs) * u_grad[:, sl]).astype(BF16)
            dws_ref[gi] += lax.dot_general(dvs_b[:, sl], vn[:, sl], NT_DIMS, preferred_element_type=F32) * mask
            dbs_ref[gi] += jnp.broadcast_to(jnp.sum(dvs[:, sl], axis=1, keepdims=True), (A_CHUNK, A_CHUNK))
            wm_t = (wst_ref[gi] * mask_t).astype(BF16)
            dvn_ref[:, sl] = jnp.dot(wm_t, dvs_b[:, sl], preferred_element_type=F32)
        dv, dg_part = _rms_bwd(vhat, r, gv, dvn_ref[...])
        dg_ref[...] += jnp.sum(dg_part, axis=0, keepdims=True)
        dz_ref[:, F:] = (dv * v_grad).astype(BF16)

    blk = pl.BlockSpec((A_CHUNK, F), lambda i: (i, 0))
    const3 = pl.BlockSpec((A_GROUPS, A_CHUNK, A_CHUNK), lambda i: (0, 0, 0))
    return pl.pallas_call(
        body, name=name, grid=(S // A_CHUNK,),
        in_specs=[blk, pl.BlockSpec((A_CHUNK, F), lambda i: (i, 1)), blk,
                  pl.BlockSpec((1, F), lambda i: (0, 0)), const3, const3,
                  pl.BlockSpec((A_CHUNK, A_GROUPS), lambda i: (0, 0))],
        out_specs=[pl.BlockSpec((A_CHUNK, F2), lambda i: (i, 0)), pl.BlockSpec((1, F), lambda i: (0, 0)),
                   const3, const3],
        out_shape=[_sds((S, F2), BF16), _sds((1, F), F32), _sds((A_GROUPS, A_CHUNK, A_CHUNK), F32),
                   _sds((A_GROUPS, A_CHUNK, A_CHUNK), F32)],
        scratch_shapes=[pltpu.VMEM((A_CHUNK, F), F32)],
        compiler_params=_cparams(1))(zpre, zpre, duv, g_sgu, ws, ws_t, bs_t)


def _toeplitz_one_hot():
    row = lax.broadcasted_iota(jnp.int32, (TABLE_PAD, DIAGS), 0)
    j = lax.broadcasted_iota(jnp.int32, (TABLE_PAD, DIAGS), 1)
    idx = jnp.clip(KV_PAD + Q_BLOCK - j, -MAX_REL, MAX_REL) + MAX_REL
    return (row == idx).astype(F32)


def _rel_bias_fwd(table, name):
    H = table.shape[0]

    def body(t_ref, o_ref):
        diag = jnp.dot(t_ref[...], _toeplitz_one_hot(), precision=HIGHEST, preferred_element_type=F32)
        k_chunk = lax.broadcasted_iota(jnp.int32, (1, BAND), 1) // CHUNK

        def step(r, carry):
            q_chunk = r // CHUNK
            seen = (k_chunk >= q_chunk) & (k_chunk <= q_chunk + N_LEFT)
            o_ref[r] = pltpu.roll(diag, DIAGS - Q_BLOCK + r, 1)[:, :BAND] + jnp.where(seen, 0.0, NEG_INF)
            return carry

        lax.fori_loop(0, Q_BLOCK, step, 0)

    return pl.pallas_call(body, name=name, out_shape=_sds((Q_BLOCK, H, BAND), F32),
                          compiler_params=pltpu.CompilerParams(vmem_limit_bytes=VMEM_LIMIT_BYTES))(table)


def _rel_bias_bwd(dbias, name):
    H = dbias.shape[1]

    def body(d_ref, o_ref):
        def step(r, acc):
            row = jnp.concatenate([d_ref[r], jnp.zeros((H, DIAGS - BAND), F32)], axis=1)
            return acc + pltpu.roll(row, Q_BLOCK - r, 1)

        diag = lax.fori_loop(0, Q_BLOCK, step, jnp.zeros((H, DIAGS), F32))
        o_ref[...] = lax.dot_general(diag, _toeplitz_one_hot(), NT_DIMS, precision=HIGHEST,
                                     preferred_element_type=F32)

    return pl.pallas_call(body, name=name, out_shape=_sds((H, TABLE_PAD), F32),
                          compiler_params=pltpu.CompilerParams(vmem_limit_bytes=VMEM_LIMIT_BYTES))(dbias)


def _head_rows(t):
    lane = lax.broadcasted_iota(jnp.int32, t.shape, 1)
    zero = jnp.zeros_like(t)
    return jnp.concatenate([jnp.where(lane < HEAD_DIM, t, zero), jnp.where(lane >= HEAD_DIM, t, zero)], axis=0)


def _head_lanes(t2):
    lane = lax.broadcasted_iota(jnp.int32, (Q_BLOCK, t2.shape[1]), 1)
    return jnp.where(lane < HEAD_DIM, t2[:Q_BLOCK], t2[Q_BLOCK:])


def _attn_probs(q2, kb, bias2, block):
    kj = lax.broadcasted_iota(jnp.int32, (1, BAND), 1)
    before_start = jnp.where(block * Q_BLOCK + kj - KV_PAD >= 0, 0.0, NEG_INF)
    s = lax.dot_general(q2 * ATTN_SCALE, kb, NT_DIMS, preferred_element_type=F32) + bias2 + before_start
    e = jnp.exp(s - jnp.max(s, axis=-1, keepdims=True))
    return e / jnp.sum(e, axis=-1, keepdims=True)


def _attn_specs(S):
    lanes = HEADS_PER_BLOCK * HEAD_DIM
    rows = S + KV_PAD
    q_spec = pl.BlockSpec((BLOCKS_PER_STEP * Q_BLOCK, lanes), lambda h, i: (i, h))
    k_spec = pl.BlockSpec((rows, lanes), lambda h, i: (0, h))
    v_spec = pl.BlockSpec((rows, lanes), lambda h, i: (0, N_HEADS // HEADS_PER_BLOCK + h))
    b_spec = pl.BlockSpec((HEADS_PER_BLOCK, Q_BLOCK, BAND), lambda h, i: (h, 0, 0))
    return q_spec, k_spec, v_spec, b_spec


def _attn_fwd(q, kvp, bias, name):
    S, HD = q.shape
    q_spec, k_spec, v_spec, b_spec = _attn_specs(S)

    def body(q_ref, k_ref, v_ref, b_ref, o_ref):
        for b in range(BLOCKS_PER_STEP):
            block = pl.program_id(1) * BLOCKS_PER_STEP + b
            rows = slice(b * Q_BLOCK, (b + 1) * Q_BLOCK)
            band = pl.ds(pl.multiple_of(block * Q_BLOCK, Q_BLOCK), BAND)
            p = _attn_probs(_head_rows(q_ref[rows, :]), k_ref[band, :], b_ref[...].reshape(2 * Q_BLOCK, BAND), block)
            o2 = jnp.dot(p.astype(BF16), v_ref[band, :], preferred_element_type=F32)
            o_ref[rows, :] = _head_lanes(o2).astype(BF16)

    return pl.pallas_call(
        body, name=name, grid=(N_HEADS // HEADS_PER_BLOCK, S // (BLOCKS_PER_STEP * Q_BLOCK)),
        in_specs=[q_spec, k_spec, v_spec, b_spec], out_specs=q_spec,
        out_shape=_sds((S, HD), BF16), compiler_params=_cparams(2))(q, kvp, kvp, bias)


def _attn_bwd(q, kvp, bias, do, dkv_prev, name):
    S, HD = q.shape
    lanes = HEADS_PER_BLOCK * HEAD_DIM
    q_spec, k_spec, v_spec, b_spec = _attn_specs(S)
    dkv_spec = pl.BlockSpec((2, S + KV_PAD, lanes), lambda h, i: (0, 0, h))

    def body(q_ref, k_ref, v_ref, b_ref, do_ref, prev_ref, dq_ref, dkv_ref, db_ref):
        @pl.when(pl.program_id(1) == 0)
        def _():
            dkv_ref[...] = prev_ref[...]
            db_ref[...] = jnp.zeros_like(db_ref)

        db = jnp.zeros((2 * Q_BLOCK, BAND), F32)
        for b in range(BLOCKS_PER_STEP):
            block = pl.program_id(1) * BLOCKS_PER_STEP + b
            rows = slice(b * Q_BLOCK, (b + 1) * Q_BLOCK)
            band = pl.ds(pl.multiple_of(block * Q_BLOCK, Q_BLOCK), BAND)
            kb, vb = k_ref[band, :], v_ref[band, :]
            q2, do2 = _head_rows(q_ref[rows, :]), _head_rows(do_ref[rows, :])
            p = _attn_probs(q2, kb, b_ref[...].reshape(2 * Q_BLOCK, BAND), block)
            dp = lax.dot_general(do2, vb, NT_DIMS, preferred_element_type=F32)
            ds = p * (dp - jnp.sum(dp * p, axis=-1, keepdims=True))
            db = db + ds
            ds_b = (ds * ATTN_SCALE).astype(BF16)
            dq_ref[rows, :] = _head_lanes(jnp.dot(ds_b, kb, preferred_element_type=F32)).astype(BF16)
            dkv_ref[0, band, :] += lax.dot_general(ds_b, q2, TN_DIMS, preferred_element_type=F32)
            dkv_ref[1, band, :] += lax.dot_general(p.astype(BF16), do2, TN_DIMS, preferred_element_type=F32)
        db_ref[...] += db.reshape(HEADS_PER_BLOCK, Q_BLOCK, BAND)

    return pl.pallas_call(
        body, name=name, grid=(N_HEADS // HEADS_PER_BLOCK, S // (BLOCKS_PER_STEP * Q_BLOCK)),
        in_specs=[q_spec, k_spec, v_spec, b_spec, q_spec, dkv_spec], out_specs=[q_spec, dkv_spec, b_spec],
        out_shape=[_sds((S, HD), BF16), _sds((2, S + KV_PAD, HD), F32), _sds((N_HEADS, Q_BLOCK, BAND), F32)],
        compiler_params=_cparams(2))(q, kvp, kvp, bias, do, dkv_prev)


def _loss_head(x, g, target, name, tm=512):
    S, D = x.shape

    def body(x_ref, g_ref, t_ref, loss_ref, dx_ref, dg_ref):
        @pl.when(pl.program_id(0) == 0)
        def _():
            loss_ref[...] = jnp.zeros_like(loss_ref)
            dg_ref[...] = jnp.zeros_like(dg_ref)

        xhat, r = _rms_hat(x_ref[...])
        gv = g_ref[...]
        err = xhat * gv - t_ref[...]
        loss_ref[...] += 0.5 * jnp.sum(jnp.mean(err * err, axis=-1, keepdims=True))
        dx, dgp = _rms_bwd(xhat, r, gv, err * (1.0 / D))
        dx_ref[...] = dx
        dg_ref[...] += jnp.sum(dgp, axis=0, keepdims=True)

    row = pl.BlockSpec((tm, D), lambda i: (i, 0))
    vec = pl.BlockSpec((1, D), lambda i: (0, 0))
    return pl.pallas_call(
        body, name=name, grid=(S // tm,), in_specs=[row, vec, row],
        out_specs=[pl.BlockSpec((8, 128), lambda i: (0, 0)), row, vec],
        out_shape=[_sds((8, 128), F32), _sds((S, D), F32), _sds((1, D), F32)],
        compiler_params=_cparams(1))(x, g, target)


def _place():
    x, y, c = lax.axis_index("x"), lax.axis_index("y"), lax.axis_index("c")
    chips = [(1 - x, y), (x, 1 - y), (1 - x, 1 - y)]
    return x, y, c, chips


def _half_rows(c, r):
    return pl.ds(pl.multiple_of(c * (r // 2), 8), r // 2)


HBM_SPEC = pl.BlockSpec(memory_space=pl.ANY)


STRICT_HBM_SPEC = pl.BlockSpec(memory_space=pltpu.HBM)
SEM_SPEC = pl.BlockSpec(memory_space=pltpu.SEMAPHORE)
EFFECT = pltpu.SideEffectType.DATAFLOW_SIDE_EFFECTING


def _peers(x, y, c):
    out = []
    for k in range(1, N_DEV):
        px, py, pc = (x + ((k >> 2) & 1)) % 2, (y + ((k >> 1) & 1)) % 2, (c + (k & 1)) % 2
        out.append(((px, py, pc), 2 * px + py, pc, 4 * px + 2 * py + pc))
    return out


def _token_spec():
    return pl.BlockSpec(memory_space=pltpu.VMEM)


def _hbm(a):
    return pltpu.with_memory_space_constraint(a, pltpu.HBM)


def _slab_half(ref, chip, core):
    return ref.at[2 * chip[0] + chip[1], _half_rows(core, ref.shape[1]), :]


def _allgather_start(slabs, name):
    n = len(slabs)

    def body(*refs):
        src, send, recv, token = refs[:n], refs[n], refs[n + 1], refs[-1]
        x, y, c, chips = _place()
        for a in range(n):
            own = _slab_half(src[a], (x, y), c)
            for j, chip in enumerate(chips):
                pltpu.make_async_remote_copy(src_ref=own, dst_ref=own, send_sem=send.at[3 * a + j], recv_sem=recv.at[3 * a + j],
                                             device_id=(*chip, c), device_id_type=MESH).start()
        token[...] = jnp.zeros_like(token)

    sems = pltpu.SemaphoreType.DMA((3 * n,))
    send, recv, *flying, token = pl.pallas_call(
        body, name=name, in_specs=[STRICT_HBM_SPEC] * n,
        out_shape=(sems, sems, *[pltpu.HBM(s.shape, s.dtype) for s in slabs], _sds((8, 128), F32)),
        out_specs=(SEM_SPEC, SEM_SPEC, *[STRICT_HBM_SPEC] * n, _token_spec()),
        input_output_aliases={a: a + 2 for a in range(n)},
        compiler_params=pltpu.CompilerParams(has_side_effects=EFFECT))(*[_hbm(s) for s in slabs])
    return send, recv, flying, token


def _allgather_relay(flying, send, recv, first, after, name):
    n = len(flying)

    def body(*refs):
        src, send_ref, recv_ref = refs[:n], refs[n], refs[n + 1]
        send2, recv2 = refs[n + 3], refs[n + 4]
        x, y, c, chips = _place()
        for a in range(n):
            for j, chip in enumerate(chips):
                cp = pltpu.make_async_remote_copy(
                    src_ref=_slab_half(src[a], (x, y), c), dst_ref=_slab_half(src[a], chip, c),
                    send_sem=send_ref.at[3 * (first + a) + j], recv_sem=recv_ref.at[3 * (first + a) + j],
                    device_id=(*chip, c), device_id_type=MESH)
                cp.wait_send()
                cp.wait_recv()
        for a in range(n):
            for j, chip in enumerate(chips):
                landed = _slab_half(src[a], chip, c)
                pltpu.make_async_remote_copy(src_ref=landed, dst_ref=landed, send_sem=send2.at[3 * a + j],
                                             recv_sem=recv2.at[3 * a + j], device_id=(x, y, 1 - c),
                                             device_id_type=MESH).start()

    sems = pltpu.SemaphoreType.DMA((3 * n,))
    send2, recv2, *relayed = pl.pallas_call(
        body, name=name, in_specs=[STRICT_HBM_SPEC] * n + [SEM_SPEC, SEM_SPEC, HBM_SPEC],
        out_shape=(sems, sems, *[pltpu.HBM(s.shape, s.dtype) for s in flying]),
        out_specs=(SEM_SPEC, SEM_SPEC, *[STRICT_HBM_SPEC] * n),
        input_output_aliases={a: a + 2 for a in range(n)},
        compiler_params=pltpu.CompilerParams(has_side_effects=EFFECT))(*flying, send, recv, after)
    return send2, recv2, relayed


def _allgather_wait(relayed, send2, recv2, after, name):
    n = len(relayed)

    def body(*refs):
        src, send_ref, recv_ref = refs[:n], refs[n], refs[n + 1]
        x, y, c, chips = _place()
        for a in range(n):
            for j, chip in enumerate(chips):
                cp = pltpu.make_async_remote_copy(
                    src_ref=_slab_half(src[a], chip, c), dst_ref=_slab_half(src[a], chip, 1 - c),
                    send_sem=send_ref.at[3 * a + j], recv_sem=recv_ref.at[3 * a + j],
                    device_id=(x, y, 1 - c), device_id_type=MESH)
                cp.wait_send()
                cp.wait_recv()

    return pl.pallas_call(
        body, name=name, in_specs=[STRICT_HBM_SPEC] * n + [SEM_SPEC, SEM_SPEC, HBM_SPEC],
        out_shape=tuple(pltpu.HBM(s.shape, s.dtype) for s in relayed), out_specs=tuple([STRICT_HBM_SPEC] * n),
        input_output_aliases={a: a for a in range(n)},
        compiler_params=pltpu.CompilerParams(has_side_effects=EFFECT))(*relayed, send2, recv2, after)


def _allgather_small(small, name):
    def body(sm, osm, send, recv, local):
        x, y, c, chips = _place()
        own = pltpu.make_async_copy(sm, osm.at[2 * x + y], local)
        own.start()
        cps = [pltpu.make_async_remote_copy(src_ref=sm, dst_ref=osm.at[2 * x + y], send_sem=send.at[j],
                                            recv_sem=recv.at[j], device_id=(*chip, c), device_id_type=MESH)
               for j, chip in enumerate(chips)]
        for cp in cps:
            cp.start()
        for j, chip in enumerate(chips):
            got = osm.at[2 * chip[0] + chip[1]]
            pltpu.make_async_remote_copy(src_ref=got, dst_ref=got, send_sem=send.at[j], recv_sem=recv.at[j],
                                         device_id=(x, y, c), device_id_type=MESH).wait_recv()
        for cp in cps:
            cp.wait_send()
        own.wait()

    return pl.pallas_call(
        body, name=name, in_specs=[pl.BlockSpec(memory_space=pltpu.VMEM)], out_specs=HBM_SPEC,
        out_shape=_sds((N_CHIPS, *small.shape), small.dtype),
        scratch_shapes=[pltpu.SemaphoreType.DMA((3,)), pltpu.SemaphoreType.DMA((3,)), pltpu.SemaphoreType.DMA])(small)


def _reduce_start(grads, name):
    n = len(grads)

    def body(*refs):
        src, land, send, recv, token = refs[:n], refs[n:2 * n], refs[2 * n], refs[2 * n + 1], refs[-1]
        x, y, c, _ = _place()
        me = 4 * x + 2 * y + c
        for a in range(n):
            for k, (peer, p_chip, p_core, _) in enumerate(_peers(x, y, c)):
                pltpu.make_async_remote_copy(
                    src_ref=src[a].at[p_chip, _half_rows(p_core, src[a].shape[1]), :], dst_ref=land[a].at[me],
                    send_sem=send.at[(N_DEV - 1) * a + k], recv_sem=recv.at[(N_DEV - 1) * a + k],
                    device_id=peer, device_id_type=MESH).start()
        token[...] = jnp.zeros_like(token)

    lands = [lax.empty((N_DEV, g.shape[1] // 2, g.shape[2]), BF16) for g in grads]
    sems = pltpu.SemaphoreType.DMA(((N_DEV - 1) * n,))
    shapes = [pltpu.HBM(a.shape, a.dtype) for a in grads + lands]
    send, recv, *flying, token = pl.pallas_call(
        body, name=name, in_specs=[STRICT_HBM_SPEC] * (2 * n),
        out_shape=(sems, sems, *shapes, _sds((8, 128), F32)),
        out_specs=(SEM_SPEC, SEM_SPEC, *[STRICT_HBM_SPEC] * (2 * n), _token_spec()),
        input_output_aliases={a: a + 2 for a in range(2 * n)},
        compiler_params=pltpu.CompilerParams(has_side_effects=EFFECT))(*[_hbm(a) for a in grads + lands])
    return send, recv, flying[:n], flying[n:], token


def _reduce_wait(started, after, name):
    sizes = [len(grads) for _, _, grads, _ in started]
    n_arr = 2 * sum(sizes)

    def body(*refs):
        x, y, c, _ = _place()
        at = 0
        for s, n in enumerate(sizes):
            src, land = refs[at:at + n], refs[at + n:at + 2 * n]
            send_ref, recv_ref = refs[n_arr + 2 * s], refs[n_arr + 2 * s + 1]
            at += 2 * n
            for a in range(n):
                for k, (peer, p_chip, p_core, p_dev) in enumerate(_peers(x, y, c)):
                    cp = pltpu.make_async_remote_copy(
                        src_ref=src[a].at[p_chip, _half_rows(p_core, src[a].shape[1]), :], dst_ref=land[a].at[p_dev],
                        send_sem=send_ref.at[(N_DEV - 1) * a + k], recv_sem=recv_ref.at[(N_DEV - 1) * a + k],
                        device_id=peer, device_id_type=MESH)
                    cp.wait_send()
                    cp.wait_recv()

    arrays, sems = [], []
    for send, recv, grads, lands in started:
        arrays += list(grads) + list(lands)
        sems += [send, recv]
    out = pl.pallas_call(
        body, name=name, in_specs=[STRICT_HBM_SPEC] * n_arr + [SEM_SPEC] * len(sems) + [HBM_SPEC],
        out_shape=tuple(pltpu.HBM(a.shape, a.dtype) for a in arrays), out_specs=tuple([STRICT_HBM_SPEC] * n_arr),
        input_output_aliases={a: a for a in range(n_arr)},
        compiler_params=pltpu.CompilerParams(has_side_effects=EFFECT))(*arrays, *sems, after)
    result, at = [], 0
    for n in sizes:
        result.append((out[at:at + n], out[at + n:at + 2 * n]))
        at += 2 * n
    return result


def _reduce_sum(grad, land, place, name):
    _, r2, C = land.shape
    tr = _row_tile(r2, C, 4, 1024 * 1024, 16)
    nb = r2 // tr

    def body(place_ref, own_ref, *rest):
        del place_ref
        acc = own_ref[...].astype(F32)
        for ref in rest[:N_DEV - 1]:
            acc = acc + ref[...].astype(F32)
        rest[-1][...] = acc

    def from_dev(k):
        return pl.BlockSpec((None, tr, C), lambda i, place_ref: ((place_ref[2] + k) % N_DEV, i, 0))

    grid_spec = pltpu.PrefetchScalarGridSpec(
        num_scalar_prefetch=1, grid=(nb,),
        in_specs=[pl.BlockSpec((None, tr, C), lambda i, place_ref: (place_ref[0], place_ref[1] * nb + i, 0))]
        + [from_dev(k) for k in range(1, N_DEV)],
        out_specs=pl.BlockSpec((tr, C), lambda i, place_ref: (place_ref[1] * nb + i, 0)))
    return pl.pallas_call(body, name=name, grid_spec=grid_spec, out_shape=_sds((2 * r2, C), F32),
                          compiler_params=_cparams(1))(place, grad, *[land] * (N_DEV - 1))


def _sibling_join(halves, name):
    n = len(halves)

    def body(*refs):
        out, send, recv = refs[n:2 * n], refs[2 * n], refs[2 * n + 1]
        x, y, c, _ = _place()
        cps = []
        for w in range(n):
            mine = out[w].at[_half_rows(c, out[w].shape[0]), :]
            cps.append(pltpu.make_async_remote_copy(src_ref=mine, dst_ref=mine, send_sem=send.at[w],
                                                    recv_sem=recv.at[w], device_id=(x, y, 1 - c), device_id_type=MESH))
        for cp in cps:
            cp.start()
        for w in range(n):
            theirs = out[w].at[_half_rows(1 - c, out[w].shape[0]), :]
            pltpu.make_async_remote_copy(src_ref=theirs, dst_ref=theirs, send_sem=send.at[w], recv_sem=recv.at[w],
                                         device_id=(x, y, c), device_id_type=MESH).wait_recv()
        for cp in cps:
            cp.wait_send()

    return pl.pallas_call(
        body, name=name, in_specs=[HBM_SPEC] * n, out_specs=[HBM_SPEC] * n,
        out_shape=[_sds(a.shape, F32) for a in halves], input_output_aliases={w: w for w in range(n)},
        scratch_shapes=[pltpu.SemaphoreType.DMA((n,)), pltpu.SemaphoreType.DMA((n,))])(*halves)


def _gather_small(packed, name):
    def body(p_ref, out, send, recv, local):
        x, y, c, _ = _place()
        me = 4 * x + 2 * y + c
        own = pltpu.make_async_copy(p_ref, out.at[me], local)
        own.start()
        cps = []
        for k in range(1, N_DEV):
            fx, fy, fc = (k >> 2) & 1, (k >> 1) & 1, k & 1
            peer = ((x + fx) % 2, (y + fy) % 2, (c + fc) % 2)
            cps.append(pltpu.make_async_remote_copy(src_ref=p_ref, dst_ref=out.at[me], send_sem=send.at[k - 1],
                                                    recv_sem=recv.at[k - 1], device_id=peer, device_id_type=MESH))
        for cp in cps:
            cp.start()
        for k in range(1, N_DEV):
            fx, fy, fc = (k >> 2) & 1, (k >> 1) & 1, k & 1
            src = out.at[4 * ((x + fx) % 2) + 2 * ((y + fy) % 2) + (c + fc) % 2]
            pltpu.make_async_remote_copy(src_ref=src, dst_ref=src, send_sem=send.at[k - 1], recv_sem=recv.at[k - 1],
                                         device_id=(x, y, c), device_id_type=MESH).wait_recv()
        for cp in cps:
            cp.wait_send()
        own.wait()

    return pl.pallas_call(
        body, name=name, in_specs=[pl.BlockSpec(memory_space=pltpu.VMEM)], out_specs=HBM_SPEC,
        out_shape=_sds((N_DEV, *packed.shape), F32),
        scratch_shapes=[pltpu.SemaphoreType.DMA((N_DEV - 1,)), pltpu.SemaphoreType.DMA((N_DEV - 1,)),
                        pltpu.SemaphoreType.DMA])(packed)


def _sum_devices(gathered, name):
    _, R, C = gathered.shape

    def body(g_ref, o_ref):
        acc = g_ref[0]
        for d in range(1, N_DEV):
            acc = acc + g_ref[d]
        o_ref[...] = acc

    tr = 8
    return pl.pallas_call(
        body, name=name, grid=(R // tr,), in_specs=[pl.BlockSpec((N_DEV, tr, C), lambda i: (0, i, 0))],
        out_specs=pl.BlockSpec((tr, C), lambda i: (i, 0)), out_shape=_sds((R, C), F32),
        compiler_params=_cparams(1))(gathered)


def _gather_start(packed, name):
    def body(src, land, send, recv, *rest):
        x, y, c, _ = _place()
        for k, (peer, _, _, _) in enumerate(_peers(x, y, c)):
            pltpu.make_async_remote_copy(src_ref=src, dst_ref=land.at[4 * x + 2 * y + c], send_sem=send.at[k],
                                         recv_sem=recv.at[k], device_id=peer, device_id_type=MESH).start()
        rest[-1][...] = jnp.zeros_like(rest[-1])

    land = lax.empty((N_DEV, *packed.shape), F32)
    sems = pltpu.SemaphoreType.DMA((N_DEV - 1,))
    return pl.pallas_call(
        body, name=name, in_specs=[STRICT_HBM_SPEC] * 2,
        out_shape=(sems, sems, pltpu.HBM(packed.shape, F32), pltpu.HBM(land.shape, F32), _sds((8, 128), F32)),
        out_specs=(SEM_SPEC, SEM_SPEC, STRICT_HBM_SPEC, STRICT_HBM_SPEC, _token_spec()),
        input_output_aliases={0: 2, 1: 3},
        compiler_params=pltpu.CompilerParams(has_side_effects=EFFECT))(_hbm(packed), _hbm(land))


def _gather_wait(started, after, name):
    n = len(started)

    def body(*refs):
        x, y, c, _ = _place()
        for s in range(n):
            src, land, send, recv = refs[2 * s], refs[2 * s + 1], refs[2 * n + 2 * s], refs[2 * n + 2 * s + 1]
            for k, (peer, _, _, p_dev) in enumerate(_peers(x, y, c)):
                cp = pltpu.make_async_remote_copy(src_ref=src, dst_ref=land.at[p_dev], send_sem=send.at[k],
                                                  recv_sem=recv.at[k], device_id=peer,
                                                  device_id_type=MESH)
                cp.wait_send()
                cp.wait_recv()

    arrays = [a for _, _, packed, land in started for a in (packed, land)]
    sems = [s for send, recv, _, _ in started for s in (send, recv)]
    out = pl.pallas_call(
        body, name=name, in_specs=[STRICT_HBM_SPEC] * (2 * n) + [SEM_SPEC] * (2 * n) + [HBM_SPEC],
        out_shape=tuple(pltpu.HBM(a.shape, a.dtype) for a in arrays), out_specs=tuple([STRICT_HBM_SPEC] * (2 * n)),
        input_output_aliases={a: a for a in range(2 * n)},
        compiler_params=pltpu.CompilerParams(has_side_effects=EFFECT))(*arrays, *sems, after)
    return [(out[2 * s], out[2 * s + 1]) for s in range(n)]


def _sum_gathered(packed, land, device, name):
    R, C = packed.shape
    tr = 8

    def body(dev_ref, own_ref, *rest):
        me = dev_ref[0]
        acc = None
        for d in range(N_DEV):
            term = jnp.where(me == d, own_ref[...], rest[d][...])
            acc = term if acc is None else acc + term
        rest[-1][...] = acc

    def slab(d):
        return pl.BlockSpec((None, tr, C), lambda i, dev_ref: (jnp.where(dev_ref[0] == d, (d + 1) % N_DEV, d), i, 0))

    grid_spec = pltpu.PrefetchScalarGridSpec(
        num_scalar_prefetch=1, grid=(R // tr,),
        in_specs=[pl.BlockSpec((tr, C), lambda i, dev_ref: (i, 0))] + [slab(d) for d in range(N_DEV)],
        out_specs=pl.BlockSpec((tr, C), lambda i, dev_ref: (i, 0)))
    return pl.pallas_call(body, name=name, grid_spec=grid_spec, out_shape=_sds((R, C), F32),
                          compiler_params=_cparams(1))(device, packed, *[land] * N_DEV)


def _pack_small(arrays):
    rows = []
    for a in arrays:
        flat = a.reshape(-1)
        pad = (-flat.shape[0]) % SMALL_COLS
        rows.append(jnp.pad(flat, (0, pad)).reshape(-1, SMALL_COLS))
    packed = jnp.concatenate(rows, axis=0)
    return jnp.pad(packed, ((0, (-packed.shape[0]) % 8), (0, 0)))


def _unpack_small(packed, shapes):
    out, row = [], 0
    for shape in shapes:
        size = math.prod(shape)
        n_rows = -(-size // SMALL_COLS)
        out.append(packed[row:row + n_rows].reshape(-1)[:size].reshape(shape))
        row += n_rows
    return out


def kernel(x, a_norm, a_w_in, a_sgu_norm, a_w_spatial, a_b_spatial, a_w_out, kv_norm, w_kv, b_norm, b_w_q, b_rel_bias, b_w_o, ffn_norm, ffn_w_gate_up, ffn_w_down, final_norm, loss_target, m_a_norm, m_a_w_in, m_a_sgu_norm, m_a_w_spatial, m_a_b_spatial, m_a_w_out, m_kv_norm, m_w_kv, m_b_norm, m_b_w_q, m_b_rel_bias, m_b_w_o, m_ffn_norm, m_ffn_w_gate_up, m_ffn_w_down, m_final_norm, v_a_norm, v_a_w_in, v_a_sgu_norm, v_a_w_spatial, v_a_b_spatial, v_a_w_out, v_kv_norm, v_w_kv, v_b_norm, v_b_w_q, v_b_rel_bias, v_b_w_o, v_ffn_norm, v_ffn_w_gate_up, v_ffn_w_down, v_final_norm):
    S, D = x.shape[1], x.shape[2]
    n_a = a_w_in.shape[0]
    n_b = b_w_q.shape[0]
    depth = ffn_w_gate_up.shape[0]
    xi, yi, ci = lax.axis_index("x"), lax.axis_index("y"), lax.axis_index("c")
    chip = 2 * xi + yi

    place = jnp.stack([chip, ci, 2 * chip + ci]).astype(jnp.int32)
    stacked = {"a_w_in": a_w_in, "a_w_out": a_w_out, "w_kv": w_kv[None], "b_w_q": b_w_q, "b_w_o": b_w_o,
               "ffn_w_gate_up": ffn_w_gate_up, "ffn_w_down": ffn_w_down}
    groups = []
    for layer in range(depth):
        if layer < n_a:
            groups.append([("a_w_in", layer), ("a_w_out", layer)])
        elif layer == n_a:
            groups.append([("w_kv", 0), ("b_w_q", 0), ("b_w_o", 0)])
        else:
            groups.append([("b_w_q", layer - n_a), ("b_w_o", layer - n_a)])
        groups.append([("ffn_w_gate_up", layer), ("ffn_w_down", layer)])
    units = [u for group in groups for u in group]
    n_early = len(groups[0])
    slabs = [_cast_slab(stacked[k], l, place[:1], f"cast_{k}_{l}") for k, l in units[:n_early]]
    early = _allgather_start(slabs, "allgather_start_first")
    slabs = [_cast_slab(stacked[k], l, place[:1], f"cast_{k}_{l}", deps=(early[3],)) for k, l in units[n_early:]]
    late = _allgather_start(slabs, "allgather_start_rest")
    na_w, ns_w = a_norm.shape[1], a_sgu_norm.shape[1]
    small_g = _allgather_small(jnp.concatenate([a_norm, a_sgu_norm], axis=1), "allgather_small")
    a_norm_f = small_g[:, :, :na_w].transpose(1, 0, 2).reshape(n_a, N_CHIPS * na_w)
    a_sgu_f = small_g[:, :, na_w:].transpose(1, 0, 2).reshape(n_a, N_CHIPS * ns_w)
    W, relayed = {}, {}

    def relay(group_index, after):
        if group_index == len(groups):
            return
        group = groups[group_index]
        (send, recv, flying, _), first = (early, 0) if group_index == 0 else (late, units.index(group[0]) - n_early)
        relayed[group_index] = _allgather_relay(flying[first:first + len(group)], send, recv, first, after,
                                                f"allgather_relay_{group_index}")

    def gathered(group_index, after):
        send2, recv2, arrays = relayed.pop(group_index)
        W.update(zip(groups[group_index], _allgather_wait(arrays, send2, recv2, after, f"allgather_wait_{group_index}")))

    xc = x.reshape(S, D)
    saved = []
    kvp = x_kv = h_kv = None
    relay(0, xc)
    for layer in range(depth):
        rec = {"x_in": xc}
        gathered(2 * layer, xc)
        if layer < n_a:
            i = layer
            rec["zpre"], rec["h"] = _norm_matmul(xc, a_norm_f[i][None], W["a_w_in", i], BF16, f"a{i}_in")
            relay(2 * layer + 1, rec["h"])
            rec["uv"] = _sgu_fwd(rec["zpre"], a_sgu_f[i][None], a_w_spatial[i], a_b_spatial[i].T, f"a{i}_sgu")
            xm = _matmul_res(rec["uv"], W["a_w_out", i], xc, f"a{i}_out")
        else:
            i = layer - n_a
            if i == 0:
                kv, h_kv = _norm_matmul(xc, kv_norm[None], W["w_kv", 0], BF16, "kv_proj")
                kvp = jnp.pad(kv, ((KV_PAD, 0), (0, 0)))
                x_kv = xc
            rec["q"], rec["h"] = _norm_matmul(xc, b_norm[i][None], W["b_w_q", i], BF16, f"b{i}_q", row_sharded=True)
            relay(2 * layer + 1, rec["h"])
            table = jnp.pad(b_rel_bias[i], ((0, 0), (0, TABLE_PAD - b_rel_bias.shape[2])))
            rec["bias"] = _rel_bias_fwd(table, f"b{i}_bias").transpose(1, 0, 2)
            rec["o"] = _attn_fwd(rec["q"], kvp, rec["bias"], f"b{i}_attn")
            xm = _matmul_res(rec["o"], W["b_w_o", i], xc, f"b{i}_o")
        rec["x_mid"] = xm
        gathered(2 * layer + 1, xm)
        rec["gu"], rec["h_f"] = _norm_matmul(xm, ffn_norm[layer][None], W["ffn_w_gate_up", layer], BF16, f"f{layer}_in")
        relay(2 * layer + 2, rec["h_f"])
        xc = _matmul_res(rec["gu"], W["ffn_w_down", layer], xm, f"f{layer}_out", swiglu=True)
        saved.append(rec)

    loss_tile, dx, d_final = _loss_head(xc, final_norm[None], loss_target.reshape(S, D), "loss_head")
    loss = lax.psum(loss_tile[0, 0], ("x", "y", "c"))

    started = []
    small_started = []
    pending = []

    def weight_grad(unit, **kw):
        full = (N_CHIPS,) + tuple(stacked[unit[0]].shape[1:])
        g = _matmul_tn(out_shape=full, name=f"d_{unit[0]}_{unit[1]}", deps=tuple(pending), **kw)
        pending.clear()
        send, recv, flying_g, flying_land, token = _reduce_start([g], f"reduce_start_{unit[0]}_{unit[1]}")
        started.append((unit, send, recv, flying_g, flying_land))
        return token

    tt = 512
    tb = min(NORMBWD_ROWS, S)
    row_a = lambda w: pl.BlockSpec((tt, w), lambda o, t: (t, 0))
    d_ffn_norm, d_b_norm, d_a_norm, d_a_sgu = [None] * depth, [None] * n_b, [None] * n_a, [None] * n_a
    d_ws, d_bs, d_rel = [None] * n_a, [None] * n_a, [None] * n_b
    dkv = jnp.zeros((2, S + KV_PAD, D), F32)
    first = lambda ref: ref[...]
    for layer in reversed(range(depth)):
        rec = saved[layer]
        r_d = ffn_w_down.shape[1]
        half_f = 2 * r_d
        token = weight_grad(
            ("ffn_w_down", layer), a_ops=[rec["gu"], rec["gu"]],
            a_specs=[pl.BlockSpec((tt, half_f), lambda o, t: (t, o)), pl.BlockSpec((tt, half_f), lambda o, t: (t, 2 + o))],
            a_fn=lambda g_ref, u_ref: _swiglu(g_ref[...].astype(F32), u_ref[...].astype(F32)).astype(BF16),
            b_op=dx, b_spec=row_a(D), out_spec=pl.BlockSpec((2, r_d, D), lambda o, t: (o, 0, 0)),
            acc_shape=(half_f, D), n_outer=2)
        dgu = _nt_swiglu_bwd(dx, W["ffn_w_down", layer], rec["gu"], f"f{layer}_dgu", deps=(token,))
        nsh = ffn_w_gate_up.shape[2]
        token = weight_grad(
            ("ffn_w_gate_up", layer), a_ops=[rec["h_f"]], a_specs=[row_a(D)], a_fn=first,
            b_op=dgu, b_spec=pl.BlockSpec((None, tt, nsh), lambda o, t: (o // 2, t, o % 2)),
            out_spec=pl.BlockSpec((None, D, nsh), lambda o, t: (o, 0, 0)), acc_shape=(D, nsh), n_outer=N_CHIPS)
        dx, d_ffn_norm[layer] = _nt_normbwd(
            dgu, pl.BlockSpec((None, tb, nsh), lambda i, k: (k // 2, i, k % 2)),
            W["ffn_w_gate_up", layer], pl.BlockSpec((None, D, nsh), lambda i, k: (k, 0, 0)),
            (D, nsh), N_CHIPS, rec["x_mid"], ffn_norm[layer][None], dx, f"f{layer}_dx", deps=(token,))
        if layer >= n_a:
            i = layer - n_a
            r_o = b_w_o.shape[1]
            token = weight_grad(
                ("b_w_o", i), a_ops=[rec["o"]], a_specs=[row_a(D)], a_fn=first, b_op=dx, b_spec=row_a(D),
                out_spec=pl.BlockSpec((N_CHIPS, r_o, D), lambda o, t: (0, 0, 0)), acc_shape=(D, D), n_outer=1)
            do = _nt_rows(dx, W["b_w_o", i], N_CHIPS, BF16, f"b{i}_do", deps=(token,))
            dq, dkv, dbias = _attn_bwd(rec["q"], kvp, rec["bias"], do, dkv, f"b{i}_attn_bwd")
            d_rel[i] = _rel_bias_bwd(dbias.transpose(1, 0, 2), f"b{i}_dbias")[:, :b_rel_bias.shape[2]]
            token = weight_grad(
                ("b_w_q", i), a_ops=[rec["h"]], a_specs=[row_a(D)], a_fn=first, b_op=dq, b_spec=row_a(D),
                out_spec=pl.BlockSpec((N_CHIPS, r_o, D), lambda o, t: (0, 0, 0)), acc_shape=(D, D), n_outer=1)
            dx, d_b_norm[i] = _nt_normbwd(
                dq, pl.BlockSpec((tb, D), lambda i_, k: (i_, 0)),
                W["b_w_q", i], pl.BlockSpec((N_CHIPS, r_o, D), lambda i_, k: (0, 0, 0)),
                (D, D), 1, rec["x_in"], b_norm[i][None], dx, f"b{i}_dx", deps=(token,))
            if i == 0:
                dkv_b = dkv[:, KV_PAD:, :].astype(BF16)
                n_kv = w_kv.shape[1]
                token = weight_grad(
                    ("w_kv", 0), a_ops=[h_kv], a_specs=[row_a(D)], a_fn=first,
                    b_op=dkv_b, b_spec=pl.BlockSpec((None, tt, n_kv), lambda o, t: (o // 2, t, o % 2)),
                    out_spec=pl.BlockSpec((None, D, n_kv), lambda o, t: (o, 0, 0)), acc_shape=(D, n_kv),
                    n_outer=N_CHIPS)
                dx, d_kv_norm = _nt_normbwd(
                    dkv_b, pl.BlockSpec((None, tb, n_kv), lambda i_, k: (k // 2, i_, k % 2)),
                    W["w_kv", 0], pl.BlockSpec((None, D, n_kv), lambda i_, k: (k, 0, 0)),
                    (D, n_kv), N_CHIPS, x_kv, kv_norm[None], dx, "kv_dx", deps=(token,))
        else:
            i = layer
            r_w = a_w_out.shape[1]
            token = weight_grad(
                ("a_w_out", i), a_ops=[rec["uv"]], a_specs=[row_a(N_CHIPS * r_w)], a_fn=first,
                b_op=dx, b_spec=row_a(D), out_spec=pl.BlockSpec((N_CHIPS, r_w, D), lambda o, t: (0, 0, 0)),
                acc_shape=(N_CHIPS * r_w, D), n_outer=1)
            duv = _nt_rows(dx, W["a_w_out", i], 2, F32, f"a{i}_duv", deps=(token,))
            dz, d_a_sgu[i], d_ws[i], dbs = _sgu_bwd(rec["zpre"], duv, a_sgu_f[i][None], a_w_spatial[i],
                                                  a_w_spatial[i].transpose(0, 2, 1), a_b_spatial[i].T, f"a{i}_sgu_bwd")
            d_bs[i] = dbs[:, :, 0]
            if i == 0:
                batch = [d_a_sgu[0], d_ws[0][None], d_bs[0][None], d_ffn_norm[0]]
                small_started.append((batch, _gather_start(_pack_small(batch), "gather_start_late")))
                pending.append(small_started[-1][1][4])
            n_in = a_w_in.shape[2]
            token = weight_grad(
                ("a_w_in", i), a_ops=[rec["h"]], a_specs=[row_a(D)], a_fn=first,
                b_op=dz, b_spec=pl.BlockSpec((tt, n_in), lambda o, t: (t, o)),
                out_spec=pl.BlockSpec((None, D, n_in), lambda o, t: (o, 0, 0)), acc_shape=(D, n_in), n_outer=N_CHIPS)
            dx, d_a_norm[i] = _nt_normbwd(
                dz, pl.BlockSpec((tb, n_in), lambda i_, k: (i_, k)),
                W["a_w_in", i], pl.BlockSpec((None, D, n_in), lambda i_, k: (k, 0, 0)),
                (D, n_in), N_CHIPS, rec["x_in"], a_norm_f[i][None], dx, f"a{i}_dx", deps=(token,))
        if layer == 1:
            batch = [jnp.concatenate(d_a_norm[1:], axis=0), jnp.concatenate(d_a_sgu[1:], axis=0), jnp.stack(d_ws[1:]),
                     jnp.stack(d_bs[1:]), d_kv_norm, jnp.concatenate(d_b_norm, axis=0), jnp.stack(d_rel),
                     jnp.concatenate(d_ffn_norm[1:], axis=0), d_final]
            small_started.append((batch, _gather_start(_pack_small(batch), "gather_start_early")))
            pending.append(small_started[-1][1][4])
    grad_x = dx.reshape(x.shape)

    landed = _reduce_wait([(send, recv, g, land) for _, send, recv, g, land in started], dx, "reduce_wait")
    halves = [_reduce_sum(g[0], land[0], place, f"reduce_sum_{unit[0]}_{unit[1]}")
              for (unit, *_), (g, land) in zip(started, landed)]
    joined = dict(zip([s[0] for s in started], _sibling_join(halves, "reduce_join")))
    reduced = {}

    (packed_e, land_e), (packed_l, land_l) = _gather_wait([s[:4] for _, s in small_started], dx, "gather_wait")
    total_e = _sum_gathered(packed_e, land_e, place[2:], "sum_small_grads_early")
    total_l = _sum_gathered(packed_l, land_l, place[2:], "sum_small_grads_late")
    total_t = _sum_devices(_gather_small(_pack_small([d_a_norm[0]]), "gather_small_grads_last"), "sum_small_grads_last")
    (e_a_norm, e_a_sgu, e_ws, e_bs, g_kv_norm, g_b_norm, g_rel, e_ffn_norm, g_final) = _unpack_small(
        total_e, [a.shape for a in small_started[0][0]])
    l_a_sgu, l_ws, l_bs, l_ffn_norm = _unpack_small(total_l, [a.shape for a in small_started[1][0]])
    (t_a_norm,) = _unpack_small(total_t, [d_a_norm[0].shape])
    g_a_norm = jnp.concatenate([t_a_norm, e_a_norm], axis=0)
    g_a_sgu = jnp.concatenate([l_a_sgu, e_a_sgu], axis=0)
    g_ws = jnp.concatenate([l_ws, e_ws], axis=0)
    g_bs = jnp.concatenate([l_bs, e_bs], axis=0)
    g_ffn_norm = jnp.concatenate([l_ffn_norm, e_ffn_norm], axis=0)
    reduced["a_norm"] = lax.dynamic_slice_in_dim(g_a_norm, chip * na_w, na_w, axis=1)
    reduced["a_sgu_norm"] = lax.dynamic_slice_in_dim(g_a_sgu, chip * ns_w, ns_w, axis=1)
    reduced.update(a_w_spatial=g_ws, a_b_spatial=g_bs, kv_norm=g_kv_norm.reshape(kv_norm.shape), b_norm=g_b_norm,
                   b_rel_bias=g_rel, ffn_norm=g_ffn_norm, final_norm=g_final.reshape(final_norm.shape))

    weights = dict(a_norm=a_norm, a_w_in=a_w_in, a_sgu_norm=a_sgu_norm, a_w_spatial=a_w_spatial,
                   a_b_spatial=a_b_spatial, a_w_out=a_w_out, kv_norm=kv_norm, w_kv=w_kv, b_norm=b_norm, b_w_q=b_w_q,
                   b_rel_bias=b_rel_bias, b_w_o=b_w_o, ffn_norm=ffn_norm, ffn_w_gate_up=ffn_w_gate_up,
                   ffn_w_down=ffn_w_down, final_norm=final_norm)
    m_in = dict(a_norm=m_a_norm, a_w_in=m_a_w_in, a_sgu_norm=m_a_sgu_norm, a_w_spatial=m_a_w_spatial,
                a_b_spatial=m_a_b_spatial, a_w_out=m_a_w_out, kv_norm=m_kv_norm, w_kv=m_w_kv, b_norm=m_b_norm,
                b_w_q=m_b_w_q, b_rel_bias=m_b_rel_bias, b_w_o=m_b_w_o, ffn_norm=m_ffn_norm,
                ffn_w_gate_up=m_ffn_w_gate_up, ffn_w_down=m_ffn_w_down, final_norm=m_final_norm)
    v_in = dict(a_norm=v_a_norm, a_w_in=v_a_w_in, a_sgu_norm=v_a_sgu_norm, a_w_spatial=v_a_w_spatial,
                a_b_spatial=v_a_b_spatial, a_w_out=v_a_w_out, kv_norm=v_kv_norm, w_kv=v_w_kv, b_norm=v_b_norm,
                b_w_q=v_b_w_q, b_rel_bias=v_b_rel_bias, b_w_o=v_b_w_o, ffn_norm=v_ffn_norm,
                ffn_w_gate_up=v_ffn_w_gate_up, ffn_w_down=v_ffn_w_down, final_norm=v_final_norm)
    grad_out, delta_out, m_out, v_out = [], [], [], []
    for key, w in weights.items():
        if key in stacked:
            as_layers = lambda a: a.reshape(stacked[key].shape)
            bufs = None
            for layer in range(stacked[key].shape[0]):
                bufs = _adamw_layer(as_layers(w), joined[key, layer], as_layers(m_in[key]), as_layers(v_in[key]),
                                    layer, bufs, f"adamw_{key}_{layer}")
            g, d, nm, nv = bufs
        else:
            g = reduced[key].reshape(w.shape)
            view = (1, w.shape[0]) if w.ndim == 1 else (-1, w.shape[-1])
            d, nm, nv = _adamw(w.reshape(view), g.reshape(view), m_in[key].reshape(view), v_in[key].reshape(view),
                               "adamw_" + key)
        grad_out.append(g.reshape(w.shape))
        delta_out.append(d.reshape(w.shape))
        m_out.append(nm.reshape(w.shape))
        v_out.append(nv.reshape(w.shape))
    return (loss, grad_x, *grad_out, *delta_out, *m_out, *v_out)
```

```python
import math

import jax
import jax.numpy as jnp
from jax import lax
from jax.experimental import pallas as pl
from jax.experimental.pallas import tpu as pltpu

F32, BF16 = jnp.float32, jnp.bfloat16
MESH = pl.DeviceIdType.MESH
HIGHEST = lax.Precision.HIGHEST
NT_DIMS = (((1,), (1,)), ((), ()))
TN_DIMS = (((0,), (0,)), ((), ()))

EPS = 1e-6
CHUNK = 64
A_CHUNK = 128
A_GROUPS = 8
N_HEADS = 16
HEAD_DIM = 64
N_LEFT = 8
MAX_REL = 256
ATTN_SCALE = HEAD_DIM ** -0.5
NEG_INF = -1e30
Q_BLOCK = 2 * CHUNK
KV_PAD = N_LEFT * CHUNK
BAND = KV_PAD + Q_BLOCK
DIAGS = BAND + Q_BLOCK
TABLE_PAD = 640
HEADS_PER_BLOCK = 2
BLOCKS_PER_STEP = 2
NORMBWD_ROWS = 1024

ADAM_LR, ADAM_B1, ADAM_B2, ADAM_EPS, ADAM_WD, ADAM_STEP = 0.001, 0.9, 0.999, 1e-08, 0.01, 10

VMEM_LIMIT_BYTES = 56 * 1024 * 1024
N_CHIPS = 4
N_DEV = 8
SMALL_COLS = 1024


def _cparams(n_grid):
    return pltpu.CompilerParams(dimension_semantics=("arbitrary",) * n_grid, vmem_limit_bytes=VMEM_LIMIT_BYTES)


def _sds(shape, dtype):
    return jax.ShapeDtypeStruct(tuple(shape), dtype)


def _gelu(x):
    return x * (0.5 * (1.0 + lax.erf(x * math.sqrt(0.5))))


def _gelu_and_grad(x):
    cdf = 0.5 * (1.0 + lax.erf(x * math.sqrt(0.5)))
    return x * cdf, cdf + x * (jnp.exp(-0.5 * x * x) * (1.0 / math.sqrt(2.0 * math.pi)))


def _rms_hat(xv):
    r = lax.rsqrt(jnp.mean(xv * xv, axis=-1, keepdims=True) + EPS)
    return xv * r, r


def _rms_bwd(xhat, r, g, dy):
    dxhat = dy * g
    dx = r * (dxhat - xhat * jnp.mean(dxhat * xhat, axis=-1, keepdims=True))
    return dx, dy * xhat


def _swiglu(gate, up):
    return (gate * jax.nn.sigmoid(gate)) * up


def _row_tile(rows, cols, itemsize, cap_bytes, align):
    t = rows
    while t * cols * itemsize > cap_bytes and t % (2 * align) == 0:
        t //= 2
    return t


def _cast_slab(w, layer, chip, name, deps=()):
    _, r, C = w.shape
    tr = _row_tile(r, C, 4, 4 * 1024 * 1024, 16)

    def body(chip_ref, w_ref, *rest):
        del chip_ref
        rest[-1][...] = w_ref[...].astype(BF16)

    grid_spec = pltpu.PrefetchScalarGridSpec(
        num_scalar_prefetch=1, grid=(r // tr,),
        in_specs=[pl.BlockSpec((None, tr, C), lambda i, chip_ref: (layer, i, 0))] + [HBM_SPEC] * len(deps),
        out_specs=pl.BlockSpec((None, tr, C), lambda i, chip_ref: (chip_ref[0], i, 0)))
    return pl.pallas_call(body, name=name, grid_spec=grid_spec, out_shape=_sds((N_CHIPS, r, C), BF16),
                          compiler_params=_cparams(1))(chip, w, *deps)


def _adamw_layer(w, g, m, v, layer, bufs, name):
    L, r, C = w.shape
    tr = _row_tile(r, C, 4, 1024 * 1024, 8)

    def body(w_ref, g_ref, m_ref, v_ref, *rest):
        go_ref, d_ref, nm_ref, nv_ref = rest[-4:]
        gv = g_ref[...]
        mn = ADAM_B1 * m_ref[...] + (1.0 - ADAM_B1) * gv
        vn = ADAM_B2 * v_ref[...] + (1.0 - ADAM_B2) * jnp.square(gv)
        m_hat = mn / (1.0 - ADAM_B1 ** ADAM_STEP)
        v_hat = vn / (1.0 - ADAM_B2 ** ADAM_STEP)
        d_ref[...] = -ADAM_LR * (m_hat / (jnp.sqrt(v_hat) + ADAM_EPS) + ADAM_WD * w_ref[...])
        nm_ref[...] = mn
        nv_ref[...] = vn
        go_ref[...] = gv

    stacked = pl.BlockSpec((None, tr, C), lambda i: (layer, i, 0))
    in_specs = [stacked, pl.BlockSpec((tr, C), lambda i: (i, 0)), stacked, stacked]
    ops = [w, g, m, v]
    aliases = {}
    if bufs is not None:
        in_specs += [HBM_SPEC] * 4
        ops += list(bufs)
        aliases = {4 + k: k for k in range(4)}
    return pl.pallas_call(body, name=name, grid=(r // tr,), in_specs=in_specs, out_specs=[stacked] * 4,
                          out_shape=[_sds((L, r, C), F32)] * 4, input_output_aliases=aliases,
                          compiler_params=_cparams(1))(*ops)


def _adamw(w, g, m, v, name):
    R, C = w.shape
    tr = _row_tile(R, C, 4, 1024 * 1024, 8)

    def body(w_ref, g_ref, m_ref, v_ref, d_ref, nm_ref, nv_ref):
        gv = g_ref[...]
        mn = ADAM_B1 * m_ref[...] + (1.0 - ADAM_B1) * gv
        vn = ADAM_B2 * v_ref[...] + (1.0 - ADAM_B2) * jnp.square(gv)
        m_hat = mn / (1.0 - ADAM_B1 ** ADAM_STEP)
        v_hat = vn / (1.0 - ADAM_B2 ** ADAM_STEP)
        d_ref[...] = -ADAM_LR * (m_hat / (jnp.sqrt(v_hat) + ADAM_EPS) + ADAM_WD * w_ref[...])
        nm_ref[...] = mn
        nv_ref[...] = vn

    spec = pl.BlockSpec((tr, C), lambda i: (i, 0))
    return pl.pallas_call(body, name=name, grid=(R // tr,), in_specs=[spec] * 4, out_specs=[spec] * 3,
                          out_shape=[_sds((R, C), F32)] * 3, compiler_params=_cparams(1))(w, g, m, v)


def _norm_matmul(x, g, w_g, out_dtype, name, row_sharded=False, deps=(), tm=1024):
    S, D = x.shape
    tm = min(tm, S)
    if row_sharded:
        r, N = w_g.shape[1], w_g.shape[2]
        tn = 512
        w_spec = pl.BlockSpec((N_CHIPS, r, tn), lambda i, j: (0, 0, j))
    else:
        nsh = w_g.shape[2]
        N = N_CHIPS * nsh
        tn = 512 if nsh % 512 == 0 else nsh
        bps = nsh // tn
        w_spec = pl.BlockSpec((None, D, tn), lambda i, j: (j // bps, 0, j % bps))

    def body(x_ref, g_ref, w_ref, *rest):
        y_ref, h_ref = rest[-2:]

        @pl.when(pl.program_id(1) == 0)
        def _():
            xhat, _ = _rms_hat(x_ref[...])
            h_ref[...] = (xhat * g_ref[...]).astype(BF16)

        w = w_ref[...].reshape(D, tn)
        y_ref[...] = jnp.dot(h_ref[...], w, preferred_element_type=F32).astype(y_ref.dtype)

    return pl.pallas_call(
        body, name=name, grid=(S // tm, N // tn),
        in_specs=[pl.BlockSpec((tm, D), lambda i, j: (i, 0)), pl.BlockSpec((1, D), lambda i, j: (0, 0)), w_spec]
        + [HBM_SPEC] * len(deps),
        out_specs=[pl.BlockSpec((tm, tn), lambda i, j: (i, j)), pl.BlockSpec((tm, D), lambda i, j: (i, 0))],
        out_shape=[_sds((S, N), out_dtype), _sds((S, D), BF16)],
        compiler_params=_cparams(2))(x, g, w_g, *deps)


def _matmul_res(a, w_g, res, name, swiglu=False, deps=(), tm=256):
    S, N = res.shape
    r = w_g.shape[1]
    K = N_CHIPS * r

    def body(*refs):
        o_ref = refs[-1]
        if swiglu:
            gate_ref, up_ref, w_ref, res_ref = refs[:4]
            a_blk = _swiglu(gate_ref[...].astype(F32), up_ref[...].astype(F32)).astype(BF16)
        else:
            a_ref, w_ref, res_ref = refs[:3]
            a_blk = a_ref[...]
        o_ref[...] = res_ref[...] + jnp.dot(a_blk, w_ref[...].reshape(K, N), preferred_element_type=F32)

    a_specs, a_ops = [pl.BlockSpec((tm, K), lambda i: (i, 0))], [a]
    if swiglu:
        a_specs.append(pl.BlockSpec((tm, K), lambda i: (i, 1)))
        a_ops.append(a)
    row = pl.BlockSpec((tm, N), lambda i: (i, 0))
    return pl.pallas_call(
        body, name=name, grid=(S // tm,),
        in_specs=a_specs + [pl.BlockSpec((N_CHIPS, r, N), lambda i: (0, 0, 0)), row] + [HBM_SPEC] * len(deps),
        out_specs=row, out_shape=_sds((S, N), F32), compiler_params=_cparams(1))(*a_ops, w_g, res, *deps)


def _matmul_tn(a_ops, a_specs, a_fn, b_op, b_spec, out_spec, out_shape, acc_shape, n_outer, name, deps=(), tt=512):
    S = b_op.shape[-2]
    na = len(a_ops)
    nt = S // tt

    def body(*refs):
        a_refs, b_ref, o_ref, acc_ref = refs[:na], refs[na], refs[-2], refs[-1]
        t = pl.program_id(1)
        part = lax.dot_general(a_fn(*a_refs), b_ref[...].astype(BF16), TN_DIMS, preferred_element_type=F32)

        @pl.when(t == 0)
        def _():
            acc_ref[...] = part

        @pl.when(t > 0)
        def _():
            acc_ref[...] += part

        @pl.when(t == nt - 1)
        def _():
            o_ref[...] = acc_ref[...].reshape(o_ref.shape).astype(BF16)

    return pl.pallas_call(
        body, name=name, grid=(n_outer, nt), in_specs=list(a_specs) + [b_spec] + [HBM_SPEC] * len(deps),
        out_specs=out_spec, out_shape=_sds(out_shape, BF16), scratch_shapes=[pltpu.VMEM(acc_shape, F32)],
        compiler_params=_cparams(2))(*a_ops, b_op, *deps)


def _nt_accumulate(a_ref, w_ref, acc_ref, w2d, nk):
    k = pl.program_id(1)
    part = lax.dot_general(a_ref[...].astype(BF16), w_ref[...].reshape(w2d), NT_DIMS, preferred_element_type=F32)

    @pl.when(k == 0)
    def _():
        acc_ref[...] = part

    @pl.when(k > 0)
    def _():
        acc_ref[...] += part

    return k == nk - 1


def _nt_normbwd(dy, dy_spec, w_g, w_spec, w2d, nk, x, g, dres, name, deps=(), tm=NORMBWD_ROWS):
    S, D = x.shape
    tm = min(tm, S)

    def body(dy_ref, w_ref, x_ref, g_ref, dres_ref, *rest):
        dx_ref, dg_ref, acc_ref = rest[-3:]

        @pl.when((pl.program_id(0) == 0) & (pl.program_id(1) == 0))
        def _():
            dg_ref[...] = jnp.zeros_like(dg_ref)

        last = _nt_accumulate(dy_ref, w_ref, acc_ref, w2d, nk)

        @pl.when(last)
        def _():
            xhat, r = _rms_hat(x_ref[...])
            dx, dgp = _rms_bwd(xhat, r, g_ref[...], acc_ref[...])
            dx_ref[...] = dres_ref[...] + dx
            dg_ref[...] += jnp.sum(dgp, axis=0, keepdims=True)

    row = pl.BlockSpec((tm, D), lambda i, k: (i, 0))
    vec = pl.BlockSpec((1, D), lambda i, k: (0, 0))
    return pl.pallas_call(
        body, name=name, grid=(S // tm, nk),
        in_specs=[dy_spec, w_spec, row, vec, row] + [HBM_SPEC] * len(deps), out_specs=[row, vec],
        out_shape=[_sds((S, D), F32), _sds((1, D), F32)],
        scratch_shapes=[pltpu.VMEM((tm, D), F32)], compiler_params=_cparams(2))(dy, w_g, x, g, dres, *deps)


def _nt_rows(dy, w_g, shards_per_block, out_dtype, name, deps=(), tm=1024):
    S, N = dy.shape
    tm = min(tm, S)
    r = w_g.shape[1]
    tn = shards_per_block * r

    def body(dy_ref, w_ref, *rest):
        o_ref = rest[-1]
        o_ref[...] = lax.dot_general(dy_ref[...].astype(BF16), w_ref[...].reshape(tn, N), NT_DIMS,
                                     preferred_element_type=F32).astype(o_ref.dtype)

    return pl.pallas_call(
        body, name=name, grid=(S // tm, N_CHIPS // shards_per_block),
        in_specs=[pl.BlockSpec((tm, N), lambda i, j: (i, 0)),
                  pl.BlockSpec((shards_per_block, r, N), lambda i, j: (j, 0, 0))] + [HBM_SPEC] * len(deps),
        out_specs=pl.BlockSpec((tm, tn), lambda i, j: (i, j)),
        out_shape=_sds((S, N_CHIPS * r), out_dtype), compiler_params=_cparams(2))(dy, w_g, *deps)


def _nt_swiglu_bwd(dy, w_g, gu, name, deps=(), tm=512):
    S, N = dy.shape
    r = w_g.shape[1]
    tn = 2 * r
    F = N_CHIPS * r

    def body(dy_ref, w_ref, gate_ref, up_ref, *rest):
        o_ref = rest[-1]
        dact = lax.dot_general(dy_ref[...].astype(BF16), w_ref[...].reshape(tn, N), NT_DIMS,
                               preferred_element_type=F32)
        gate, up = gate_ref[...].astype(F32), up_ref[...].astype(F32)
        sg = jax.nn.sigmoid(gate)
        silu = gate * sg
        o_ref[0] = ((dact * up) * (sg + silu * (1.0 - sg))).astype(BF16)
        o_ref[1] = (dact * silu).astype(BF16)

    return pl.pallas_call(
        body, name=name, grid=(2, S // tm),
        in_specs=[pl.BlockSpec((tm, N), lambda j, i: (i, 0)),
                  pl.BlockSpec((2, r, N), lambda j, i: (j, 0, 0)),
                  pl.BlockSpec((tm, tn), lambda j, i: (i, j)),
                  pl.BlockSpec((tm, tn), lambda j, i: (i, 2 + j))] + [HBM_SPEC] * len(deps),
        out_specs=pl.BlockSpec((2, tm, tn), lambda j, i: (0, i, j)),
        out_shape=_sds((2, S, F), BF16), compiler_params=_cparams(2))(dy, w_g, gu, gu, *deps)


def _chunk_causal_mask(transposed):
    i = lax.broadcasted_iota(jnp.int32, (A_CHUNK, A_CHUNK), 0) // CHUNK
    j = lax.broadcasted_iota(jnp.int32, (A_CHUNK, A_CHUNK), 1) // CHUNK
    return ((i <= j) if transposed else (i >= j)).astype(F32)


def _sgu_fwd(zpre, g_sgu, ws, bs_t, name, deps=()):
    S, F2 = zpre.shape
    F = F2 // 2
    gd = F // A_GROUPS

    def body(zu_ref, zv_ref, g_ref, ws_ref, b_ref, *rest):
        o_ref = rest[-1]
        vhat, _ = _rms_hat(_gelu(zv_ref[...].astype(F32)))
        vn = (vhat * g_ref[...]).astype(BF16)
        u = _gelu(zu_ref[...].astype(F32))
        mask = _chunk_causal_mask(False)
        for gi in range(A_GROUPS):
            sl = slice(gi * gd, (gi + 1) * gd)
            wm = (ws_ref[gi] * mask).astype(BF16)
            vs = jnp.dot(wm, vn[:, sl], preferred_element_type=F32) + b_ref[:, gi:gi + 1]
            o_ref[:, sl] = (u[:, sl] * vs).astype(BF16)

    return pl.pallas_call(
        body, name=name, grid=(S // A_CHUNK,),
        in_specs=[pl.BlockSpec((A_CHUNK, F), lambda i: (i, 0)),
                  pl.BlockSpec((A_CHUNK, F), lambda i: (i, 1)),
                  pl.BlockSpec((1, F), lambda i: (0, 0)),
                  pl.BlockSpec((A_GROUPS, A_CHUNK, A_CHUNK), lambda i: (0, 0, 0)),
                  pl.BlockSpec((A_CHUNK, A_GROUPS), lambda i: (0, 0))] + [HBM_SPEC] * len(deps),
        out_specs=pl.BlockSpec((A_CHUNK, F), lambda i: (i, 0)),
        out_shape=_sds((S, F), BF16), compiler_params=_cparams(1))(zpre, zpre, g_sgu, ws, bs_t, *deps)


def _sgu_bwd(zpre, duv, g_sgu, ws, ws_t, bs_t, name):
    S, F2 = zpre.shape
    F = F2 // 2
    gd = F // A_GROUPS

    def body(zu_ref, zv_ref, duv_ref, g_ref, ws_ref, wst_ref, b_ref, dz_ref, dg_ref, dws_ref, dbs_ref, dvn_ref):
        @pl.when(pl.program_id(0) == 0)
        def _():
            dg_ref[...] = jnp.zeros_like(dg_ref)
            dws_ref[...] = jnp.zeros_like(dws_ref)
            dbs_ref[...] = jnp.zeros_like(dbs_ref)

        gv = g_ref[...]
        u, u_grad = _gelu_and_grad(zu_ref[...].astype(F32))
        v, v_grad = _gelu_and_grad(zv_ref[...].astype(F32))
        vhat, r = _rms_hat(v)
        vn = (vhat * gv).astype(BF16)
        duv_v = duv_ref[...]
        dvs = duv_v * u
        dvs_b = dvs.astype(BF16)
        mask = _chunk_causal_mask(False)
        mask_t = _chunk_causal_mask(True)
        for gi in range(A_GROUPS):
            sl = slice(gi * gd, (gi + 1) * gd)
            wm = (ws_ref[gi] * mask).astype(BF16)
            vs = jnp.dot(wm, vn[:, sl], preferred_element_type=F32) + b_ref[:, gi:gi + 1]
            dz_ref[:, sl] = ((duv_v[:, sl] * vs) * u_grad[:, sl]).astype(BF16)
            dws_ref[gi] += lax.dot_general(dvs_b[:, sl], vn[:, sl], NT_DIMS, preferred_element_type=F32) * mask
            dbs_ref[gi] += jnp.broadcast_to(jnp.sum(dvs[:, sl], axis=1, keepdims=True), (A_CHUNK, A_CHUNK))
            wm_t = (wst_ref[gi] * mask_t).astype(BF16)
            dvn_ref[:, sl] = jnp.dot(wm_t, dvs_b[:, sl], preferred_element_type=F32)
        dv, dg_part = _rms_bwd(vhat, r, gv, dvn_ref[...])
        dg_ref[...] += jnp.sum(dg_part, axis=0, keepdims=True)
        dz_ref[:, F:] = (dv * v_grad).astype(BF16)

    blk = pl.BlockSpec((A_CHUNK, F), lambda i: (i, 0))
    const3 = pl.BlockSpec((A_GROUPS, A_CHUNK, A_CHUNK), lambda i: (0, 0, 0))
    return pl.pallas_call(
        body, name=name, grid=(S // A_CHUNK,),
        in_specs=[blk, pl.BlockSpec((A_CHUNK, F), lambda i: (i, 1)), blk,
                  pl.BlockSpec((1, F), lambda i: (0, 0)), const3, const3,
                  pl.BlockSpec((A_CHUNK, A_GROUPS), lambda i: (0, 0))],
        out_specs=[pl.BlockSpec((A_CHUNK, F2), lambda i: (i, 0)), pl.BlockSpec((1, F), lambda i: (0, 0)),
                   const3, const3],
        out_shape=[_sds((S, F2), BF16), _sds((1, F), F32), _sds((A_GROUPS, A_CHUNK, A_CHUNK), F32),
                   _sds((A_GROUPS, A_CHUNK, A_CHUNK), F32)],
        scratch_shapes=[pltpu.VMEM((A_CHUNK, F), F32)],
        compiler_params=_cparams(1))(zpre, zpre, duv, g_sgu, ws, ws_t, bs_t)


def _toeplitz_one_hot():
    row = lax.broadcasted_iota(jnp.int32, (TABLE_PAD, DIAGS), 0)
    j = lax.broadcasted_iota(jnp.int32, (TABLE_PAD, DIAGS), 1)
    idx = jnp.clip(KV_PAD + Q_BLOCK - j, -MAX_REL, MAX_REL) + MAX_REL
    return (row == idx).astype(F32)


def _rel_bias_fwd(table, name):
    H = table.shape[0]

    def body(t_ref, o_ref):
        diag = jnp.dot(t_ref[...], _toeplitz_one_hot(), precision=HIGHEST, preferred_element_type=F32)
        k_chunk = lax.broadcasted_iota(jnp.int32, (1, BAND), 1) // CHUNK

        def step(r, carry):
            q_chunk = r // CHUNK
            seen = (k_chunk >= q_chunk) & (k_chunk <= q_chunk + N_LEFT)
            o_ref[r] = pltpu.roll(diag, DIAGS - Q_BLOCK + r, 1)[:, :BAND] + jnp.where(seen, 0.0, NEG_INF)
            return carry

        lax.fori_loop(0, Q_BLOCK, step, 0)

    return pl.pallas_call(body, name=name, out_shape=_sds((Q_BLOCK, H, BAND), F32),
                          compiler_params=pltpu.CompilerParams(vmem_limit_bytes=VMEM_LIMIT_BYTES))(table)


def _rel_bias_bwd(dbias, name):
    H = dbias.shape[1]

    def body(d_ref, o_ref):
        def step(r, acc):
            row = jnp.concatenate([d_ref[r], jnp.zeros((H, DIAGS - BAND), F32)], axis=1)
            return acc + pltpu.roll(row, Q_BLOCK - r, 1)

        diag = lax.fori_loop(0, Q_BLOCK, step, jnp.zeros((H, DIAGS), F32))
        o_ref[...] = lax.dot_general(diag, _toeplitz_one_hot(), NT_DIMS, precision=HIGHEST,
                                     preferred_element_type=F32)

    return pl.pallas_call(body, name=name, out_shape=_sds((H, TABLE_PAD), F32),
                          compiler_params=pltpu.CompilerParams(vmem_limit_bytes=VMEM_LIMIT_BYTES))(dbias)


def _head_rows(t):
    lane = lax.broadcasted_iota(jnp.int32, t.shape, 1)
    zero = jnp.zeros_like(t)
    return jnp.concatenate([jnp.where(lane < HEAD_DIM, t, zero), jnp.where(lane >= HEAD_DIM, t, zero)], axis=0)


def _head_lanes(t2):
    lane = lax.broadcasted_iota(jnp.int32, (Q_BLOCK, t2.shape[1]), 1)
    return jnp.where(lane < HEAD_DIM, t2[:Q_BLOCK], t2[Q_BLOCK:])


def _attn_probs(q2, kb, bias2, block):
    kj = lax.broadcasted_iota(jnp.int32, (1, BAND), 1)
    before_start = jnp.where(block * Q_BLOCK + kj - KV_PAD >= 0, 0.0, NEG_INF)
    s = lax.dot_general(q2 * ATTN_SCALE, kb, NT_DIMS, preferred_element_type=F32) + bias2 + before_start
    e = jnp.exp(s - jnp.max(s, axis=-1, keepdims=True))
    return e / jnp.sum(e, axis=-1, keepdims=True)


def _attn_specs(S):
    lanes = HEADS_PER_BLOCK * HEAD_DIM
    rows = S + KV_PAD
    q_spec = pl.BlockSpec((BLOCKS_PER_STEP * Q_BLOCK, lanes), lambda h, i: (i, h))
    k_spec = pl.BlockSpec((rows, lanes), lambda h, i: (0, h))
    v_spec = pl.BlockSpec((rows, lanes), lambda h, i: (0, N_HEADS // HEADS_PER_BLOCK + h))
    b_spec = pl.BlockSpec((HEADS_PER_BLOCK, Q_BLOCK, BAND), lambda h, i: (h, 0, 0))
    return q_spec, k_spec, v_spec, b_spec


def _attn_fwd(q, kvp, bias, name, deps=()):
    S, HD = q.shape
    q_spec, k_spec, v_spec, b_spec = _attn_specs(S)

    def body(q_ref, k_ref, v_ref, b_ref, *rest):
        o_ref = rest[-1]
        for b in range(BLOCKS_PER_STEP):
            block = pl.program_id(1) * BLOCKS_PER_STEP + b
            rows = slice(b * Q_BLOCK, (b + 1) * Q_BLOCK)
            band = pl.ds(pl.multiple_of(block * Q_BLOCK, Q_BLOCK), BAND)
            p = _attn_probs(_head_rows(q_ref[rows, :]), k_ref[band, :], b_ref[...].reshape(2 * Q_BLOCK, BAND), block)
            o2 = jnp.dot(p.astype(BF16), v_ref[band, :], preferred_element_type=F32)
            o_ref[rows, :] = _head_lanes(o2).astype(BF16)

    return pl.pallas_call(
        body, name=name, grid=(N_HEADS // HEADS_PER_BLOCK, S // (BLOCKS_PER_STEP * Q_BLOCK)),
        in_specs=[q_spec, k_spec, v_spec, b_spec] + [HBM_SPEC] * len(deps), out_specs=q_spec,
        out_shape=_sds((S, HD), BF16), compiler_params=_cparams(2))(q, kvp, kvp, bias, *deps)


def _attn_bwd(q, kvp, bias, do, dkv_prev, name):
    S, HD = q.shape
    lanes = HEADS_PER_BLOCK * HEAD_DIM
    q_spec, k_spec, v_spec, b_spec = _attn_specs(S)
    dkv_spec = pl.BlockSpec((2, S + KV_PAD, lanes), lambda h, i: (0, 0, h))

    def body(q_ref, k_ref, v_ref, b_ref, do_ref, prev_ref, dq_ref, dkv_ref, db_ref):
        @pl.when(pl.program_id(1) == 0)
        def _():
            dkv_ref[...] = prev_ref[...]
            db_ref[...] = jnp.zeros_like(db_ref)

        db = jnp.zeros((2 * Q_BLOCK, BAND), F32)
        for b in range(BLOCKS_PER_STEP):
            block = pl.program_id(1) * BLOCKS_PER_STEP + b
            rows = slice(b * Q_BLOCK, (b + 1) * Q_BLOCK)
            band = pl.ds(pl.multiple_of(block * Q_BLOCK, Q_BLOCK), BAND)
            kb, vb = k_ref[band, :], v_ref[band, :]
            q2, do2 = _head_rows(q_ref[rows, :]), _head_rows(do_ref[rows, :])
            p = _attn_probs(q2, kb, b_ref[...].reshape(2 * Q_BLOCK, BAND), block)
            dp = lax.dot_general(do2, vb, NT_DIMS, preferred_element_type=F32)
            ds = p * (dp - jnp.sum(dp * p, axis=-1, keepdims=True))
            db = db + ds
            ds_b = (ds * ATTN_SCALE).astype(BF16)
            dq_ref[rows, :] = _head_lanes(jnp.dot(ds_b, kb, preferred_element_type=F32)).astype(BF16)
            dkv_ref[0, band, :] += lax.dot_general(ds_b, q2, TN_DIMS, preferred_element_type=F32)
            dkv_ref[1, band, :] += lax.dot_general(p.astype(BF16), do2, TN_DIMS, preferred_element_type=F32)
        db_ref[...] += db.reshape(HEADS_PER_BLOCK, Q_BLOCK, BAND)

    return pl.pallas_call(
        body, name=name, grid=(N_HEADS // HEADS_PER_BLOCK, S // (BLOCKS_PER_STEP * Q_BLOCK)),
        in_specs=[q_spec, k_spec, v_spec, b_spec, q_spec, dkv_spec], out_specs=[q_spec, dkv_spec, b_spec],
        out_shape=[_sds((S, HD), BF16), _sds((2, S + KV_PAD, HD), F32), _sds((N_HEADS, Q_BLOCK, BAND), F32)],
        compiler_params=_cparams(2))(q, kvp, kvp, bias, do, dkv_prev)


def _loss_head(x, g, target, name, tm=512):
    S, D = x.shape

    def body(x_ref, g_ref, t_ref, loss_ref, dx_ref, dg_ref):
        @pl.when(pl.program_id(0) == 0)
        def _():
            loss_ref[...] = jnp.zeros_like(loss_ref)
            dg_ref[...] = jnp.zeros_like(dg_ref)

        xhat, r = _rms_hat(x_ref[...])
        gv = g_ref[...]
        err = xhat * gv - t_ref[...]
        loss_ref[...] += 0.5 * jnp.sum(jnp.mean(err * err, axis=-1, keepdims=True))
        dx, dgp = _rms_bwd(xhat, r, gv, err * (1.0 / D))
        dx_ref[...] = dx
        dg_ref[...] += jnp.sum(dgp, axis=0, keepdims=True)

    row = pl.BlockSpec((tm, D), lambda i: (i, 0))
    vec = pl.BlockSpec((1, D), lambda i: (0, 0))
    return pl.pallas_call(
        body, name=name, grid=(S // tm,), in_specs=[row, vec, row],
        out_specs=[pl.BlockSpec((8, 128), lambda i: (0, 0)), row, vec],
        out_shape=[_sds((8, 128), F32), _sds((S, D), F32), _sds((1, D), F32)],
        compiler_params=_cparams(1))(x, g, target)


def _place():
    x, y, c = lax.axis_index("x"), lax.axis_index("y"), lax.axis_index("c")
    chips = [(1 - x, y), (x, 1 - y), (1 - x, 1 - y)]
    return x, y, c, chips


def _half_rows(c, r):
    return pl.ds(pl.multiple_of(c * (r // 2), 8), r // 2)


HBM_SPEC = pl.BlockSpec(memory_space=pl.ANY)


STRICT_HBM_SPEC = pl.BlockSpec(memory_space=pltpu.HBM)
SEM_SPEC = pl.BlockSpec(memory_space=pltpu.SEMAPHORE)
EFFECT = pltpu.SideEffectType.DATAFLOW_SIDE_EFFECTING


def _peers(x, y, c):
    out = []
    for k in range(1, N_DEV):
        px, py, pc = (x + ((k >> 2) & 1)) % 2, (y + ((k >> 1) & 1)) % 2, (c + (k & 1)) % 2
        out.append(((px, py, pc), 2 * px + py, pc, 4 * px + 2 * py + pc))
    return out


def _token_spec():
    return pl.BlockSpec(memory_space=pltpu.VMEM)


def _hbm(a):
    return pltpu.with_memory_space_constraint(a, pltpu.HBM)


def _slab_half(ref, chip, core):
    return ref.at[2 * chip[0] + chip[1], _half_rows(core, ref.shape[1]), :]


def _allgather_start(slabs, name):
    n = len(slabs)

    def body(*refs):
        src, send, recv, token = refs[:n], refs[n], refs[n + 1], refs[-1]
        x, y, c, chips = _place()
        for a in range(n):
            own = _slab_half(src[a], (x, y), c)
            for j, chip in enumerate(chips):
                pltpu.make_async_remote_copy(src_ref=own, dst_ref=own, send_sem=send.at[3 * a + j], recv_sem=recv.at[3 * a + j],
                                             device_id=(*chip, c), device_id_type=MESH).start()
        token[...] = jnp.zeros_like(token)

    sems = pltpu.SemaphoreType.DMA((3 * n,))
    send, recv, *flying, token = pl.pallas_call(
        body, name=name, in_specs=[STRICT_HBM_SPEC] * n,
        out_shape=(sems, sems, *[pltpu.HBM(s.shape, s.dtype) for s in slabs], _sds((8, 128), F32)),
        out_specs=(SEM_SPEC, SEM_SPEC, *[STRICT_HBM_SPEC] * n, _token_spec()),
        input_output_aliases={a: a + 2 for a in range(n)},
        compiler_params=pltpu.CompilerParams(has_side_effects=EFFECT))(*[_hbm(s) for s in slabs])
    return send, recv, flying, token


def _allgather_relay(flying, send, recv, first, after, name):
    n = len(flying)

    def body(*refs):
        src, send_ref, recv_ref = refs[:n], refs[n], refs[n + 1]
        send2, recv2, token = refs[n + 3], refs[n + 4], refs[-1]
        token[...] = jnp.zeros_like(token)
        x, y, c, chips = _place()
        for a in range(n):
            for j, chip in enumerate(chips):
                cp = pltpu.make_async_remote_copy(
                    src_ref=_slab_half(src[a], (x, y), c), dst_ref=_slab_half(src[a], chip, c),
                    send_sem=send_ref.at[3 * (first + a) + j], recv_sem=recv_ref.at[3 * (first + a) + j],
                    device_id=(*chip, c), device_id_type=MESH)
                cp.wait_send()
                cp.wait_recv()
        for a in range(n):
            for j, chip in enumerate(chips):
                landed = _slab_half(src[a], chip, c)
                pltpu.make_async_remote_copy(src_ref=landed, dst_ref=landed, send_sem=send2.at[3 * a + j],
                                             recv_sem=recv2.at[3 * a + j], device_id=(x, y, 1 - c),
                                             device_id_type=MESH).start()

    sems = pltpu.SemaphoreType.DMA((3 * n,))
    send2, recv2, *relayed, token = pl.pallas_call(
        body, name=name, in_specs=[STRICT_HBM_SPEC] * n + [SEM_SPEC, SEM_SPEC, HBM_SPEC],
        out_shape=(sems, sems, *[pltpu.HBM(s.shape, s.dtype) for s in flying], _sds((8, 128), F32)),
        out_specs=(SEM_SPEC, SEM_SPEC, *[STRICT_HBM_SPEC] * n, _token_spec()),
        input_output_aliases={a: a + 2 for a in range(n)},
        compiler_params=pltpu.CompilerParams(has_side_effects=EFFECT))(*flying, send, recv, after)
    return send2, recv2, relayed, token


def _allgather_wait(relayed, send2, recv2, after, name):
    n = len(relayed)

    def body(*refs):
        src, send_ref, recv_ref = refs[:n], refs[n], refs[n + 1]
        x, y, c, chips = _place()
        for a in range(n):
            for j, chip in enumerate(chips):
                cp = pltpu.make_async_remote_copy(
                    src_ref=_slab_half(src[a], chip, c), dst_ref=_slab_half(src[a], chip, 1 - c),
                    send_sem=send_ref.at[3 * a + j], recv_sem=recv_ref.at[3 * a + j],
                    device_id=(x, y, 1 - c), device_id_type=MESH)
                cp.wait_send()
                cp.wait_recv()

    return pl.pallas_call(
        body, name=name, in_specs=[STRICT_HBM_SPEC] * n + [SEM_SPEC, SEM_SPEC, HBM_SPEC],
        out_shape=tuple(pltpu.HBM(s.shape, s.dtype) for s in relayed), out_specs=tuple([STRICT_HBM_SPEC] * n),
        input_output_aliases={a: a for a in range(n)},
        compiler_params=pltpu.CompilerParams(has_side_effects=EFFECT))(*relayed, send2, recv2, after)


def _allgather_small(small, name):
    def body(sm, osm, send, recv, local):
        x, y, c, chips = _place()
        own = pltpu.make_async_copy(sm, osm.at[2 * x + y], local)
        own.start()
        cps = [pltpu.make_async_remote_copy(src_ref=sm, dst_ref=osm.at[2 * x + y], send_sem=send.at[j],
                                            recv_sem=recv.at[j], device_id=(*chip, c), device_id_type=MESH)
               for j, chip in enumerate(chips)]
        for cp in cps:
            cp.start()
        for j, chip in enumerate(chips):
            got = osm.at[2 * chip[0] + chip[1]]
            pltpu.make_async_remote_copy(src_ref=got, dst_ref=got, send_sem=send.at[j], recv_sem=recv.at[j],
                                         device_id=(x, y, c), device_id_type=MESH).wait_recv()
        for cp in cps:
            cp.wait_send()
        own.wait()

    return pl.pallas_call(
        body, name=name, in_specs=[pl.BlockSpec(memory_space=pltpu.VMEM)], out_specs=HBM_SPEC,
        out_shape=_sds((N_CHIPS, *small.shape), small.dtype),
        scratch_shapes=[pltpu.SemaphoreType.DMA((3,)), pltpu.SemaphoreType.DMA((3,)), pltpu.SemaphoreType.DMA])(small)


def _reduce_start(grads, name):
    n = len(grads)

    def body(*refs):
        src, land, send, recv, token = refs[:n], refs[n:2 * n], refs[2 * n], refs[2 * n + 1], refs[-1]
        x, y, c, _ = _place()
        me = 4 * x + 2 * y + c
        for a in range(n):
            for k, (peer, p_chip, p_core, _) in enumerate(_peers(x, y, c)):
                pltpu.make_async_remote_copy(
                    src_ref=src[a].at[p_chip, _half_rows(p_core, src[a].shape[1]), :], dst_ref=land[a].at[me],
                    send_sem=send.at[(N_DEV - 1) * a + k], recv_sem=recv.at[(N_DEV - 1) * a + k],
                    device_id=peer, device_id_type=MESH).start()
        token[...] = jnp.zeros_like(token)

    lands = [lax.empty((N_DEV, g.shape[1] // 2, g.shape[2]), BF16) for g in grads]
    sems = pltpu.SemaphoreType.DMA(((N_DEV - 1) * n,))
    shapes = [pltpu.HBM(a.shape, a.dtype) for a in grads + lands]
    send, recv, *flying, token = pl.pallas_call(
        body, name=name, in_specs=[STRICT_HBM_SPEC] * (2 * n),
        out_shape=(sems, sems, *shapes, _sds((8, 128), F32)),
        out_specs=(SEM_SPEC, SEM_SPEC, *[STRICT_HBM_SPEC] * (2 * n), _token_spec()),
        input_output_aliases={a: a + 2 for a in range(2 * n)},
        compiler_params=pltpu.CompilerParams(has_side_effects=EFFECT))(*[_hbm(a) for a in grads + lands])
    return send, recv, flying[:n], flying[n:], token


def _reduce_wait(started, after, name):
    sizes = [len(grads) for _, _, grads, _ in started]
    n_arr = 2 * sum(sizes)

    def body(*refs):
        x, y, c, _ = _place()
        at = 0
        for s, n in enumerate(sizes):
            src, land = refs[at:at + n], refs[at + n:at + 2 * n]
            send_ref, recv_ref = refs[n_arr + 2 * s], refs[n_arr + 2 * s + 1]
            at += 2 * n
            for a in range(n):
                for k, (peer, p_chip, p_core, p_dev) in enumerate(_peers(x, y, c)):
                    cp = pltpu.make_async_remote_copy(
                        src_ref=src[a].at[p_chip, _half_rows(p_core, src[a].shape[1]), :], dst_ref=land[a].at[p_dev],
                        send_sem=send_ref.at[(N_DEV - 1) * a + k], recv_sem=recv_ref.at[(N_DEV - 1) * a + k],
                        device_id=peer, device_id_type=MESH)
                    cp.wait_send()
                    cp.wait_recv()

    arrays, sems = [], []
    for send, recv, grads, lands in started:
        arrays += list(grads) + list(lands)
        sems += [send, recv]
    out = pl.pallas_call(
        body, name=name, in_specs=[STRICT_HBM_SPEC] * n_arr + [SEM_SPEC] * len(sems) + [HBM_SPEC],
        out_shape=tuple(pltpu.HBM(a.shape, a.dtype) for a in arrays), out_specs=tuple([STRICT_HBM_SPEC] * n_arr),
        input_output_aliases={a: a for a in range(n_arr)},
        compiler_params=pltpu.CompilerParams(has_side_effects=EFFECT))(*arrays, *sems, after)
    result, at = [], 0
    for n in sizes:
        result.append((out[at:at + n], out[at + n:at + 2 * n]))
        at += 2 * n
    return result


def _reduce_sum(grad, land, place, name):
    _, r2, C = land.shape
    tr = _row_tile(r2, C, 4, 1024 * 1024, 16)
    nb = r2 // tr

    def body(place_ref, own_ref, *rest):
        del place_ref
        acc = own_ref[...].astype(F32)
        for ref in rest[:N_DEV - 1]:
            acc = acc + ref[...].astype(F32)
        rest[-1][...] = acc

    def from_dev(k):
        return pl.BlockSpec((None, tr, C), lambda i, place_ref: ((place_ref[2] + k) % N_DEV, i, 0))

    grid_spec = pltpu.PrefetchScalarGridSpec(
        num_scalar_prefetch=1, grid=(nb,),
        in_specs=[pl.BlockSpec((None, tr, C), lambda i, place_ref: (place_ref[0], place_ref[1] * nb + i, 0))]
        + [from_dev(k) for k in range(1, N_DEV)],
        out_specs=pl.BlockSpec((tr, C), lambda i, place_ref: (place_ref[1] * nb + i, 0)))
    return pl.pallas_call(body, name=name, grid_spec=grid_spec, out_shape=_sds((2 * r2, C), F32),
                          compiler_params=_cparams(1))(place, grad, *[land] * (N_DEV - 1))


def _sibling_join(halves, name):
    n = len(halves)

    def body(*refs):
        out, send, recv = refs[n:2 * n], refs[2 * n], refs[2 * n + 1]
        x, y, c, _ = _place()
        cps = []
        for w in range(n):
            mine = out[w].at[_half_rows(c, out[w].shape[0]), :]
            cps.append(pltpu.make_async_remote_copy(src_ref=mine, dst_ref=mine, send_sem=send.at[w],
                                                    recv_sem=recv.at[w], device_id=(x, y, 1 - c), device_id_type=MESH))
        for cp in cps:
            cp.start()
        for w in range(n):
            theirs = out[w].at[_half_rows(1 - c, out[w].shape[0]), :]
            pltpu.make_async_remote_copy(src_ref=theirs, dst_ref=theirs, send_sem=send.at[w], recv_sem=recv.at[w],
                                         device_id=(x, y, c), device_id_type=MESH).wait_recv()
        for cp in cps:
            cp.wait_send()

    return pl.pallas_call(
        body, name=name, in_specs=[HBM_SPEC] * n, out_specs=[HBM_SPEC] * n,
        out_shape=[_sds(a.shape, F32) for a in halves], input_output_aliases={w: w for w in range(n)},
        scratch_shapes=[pltpu.SemaphoreType.DMA((n,)), pltpu.SemaphoreType.DMA((n,))])(*halves)


def _gather_small(packed, name):
    def body(p_ref, out, send, recv, local):
        x, y, c, _ = _place()
        me = 4 * x + 2 * y + c
        own = pltpu.make_async_copy(p_ref, out.at[me], local)
        own.start()
        cps = []
        for k in range(1, N_DEV):
            fx, fy, fc = (k >> 2) & 1, (k >> 1) & 1, k & 1
            peer = ((x + fx) % 2, (y + fy) % 2, (c + fc) % 2)
            cps.append(pltpu.make_async_remote_copy(src_ref=p_ref, dst_ref=out.at[me], send_sem=send.at[k - 1],
                                                    recv_sem=recv.at[k - 1], device_id=peer, device_id_type=MESH))
        for cp in cps:
            cp.start()
        for k in range(1, N_DEV):
            fx, fy, fc = (k >> 2) & 1, (k >> 1) & 1, k & 1
            src = out.at[4 * ((x + fx) % 2) + 2 * ((y + fy) % 2) + (c + fc) % 2]
            pltpu.make_async_remote_copy(src_ref=src, dst_ref=src, send_sem=send.at[k - 1], recv_sem=recv.at[k - 1],
                                         device_id=(x, y, c), device_id_type=MESH).wait_recv()
        for cp in cps:
            cp.wait_send()
        own.wait()

    return pl.pallas_call(
        body, name=name, in_specs=[pl.BlockSpec(memory_space=pltpu.VMEM)], out_specs=HBM_SPEC,
        out_shape=_sds((N_DEV, *packed.shape), F32),
        scratch_shapes=[pltpu.SemaphoreType.DMA((N_DEV - 1,)), pltpu.SemaphoreType.DMA((N_DEV - 1,)),
                        pltpu.SemaphoreType.DMA])(packed)


def _sum_devices(gathered, name):
    _, R, C = gathered.shape

    def body(g_ref, o_ref):
        acc = g_ref[0]
        for d in range(1, N_DEV):
            acc = acc + g_ref[d]
        o_ref[...] = acc

    tr = 8
    return pl.pallas_call(
        body, name=name, grid=(R // tr,), in_specs=[pl.BlockSpec((N_DEV, tr, C), lambda i: (0, i, 0))],
        out_specs=pl.BlockSpec((tr, C), lambda i: (i, 0)), out_shape=_sds((R, C), F32),
        compiler_params=_cparams(1))(gathered)


def _gather_start(packed, name):
    def body(src, land, send, recv, *rest):
        x, y, c, _ = _place()
        for k, (peer, _, _, _) in enumerate(_peers(x, y, c)):
            pltpu.make_async_remote_copy(src_ref=src, dst_ref=land.at[4 * x + 2 * y + c], send_sem=send.at[k],
                                         recv_sem=recv.at[k], device_id=peer, device_id_type=MESH).start()
        rest[-1][...] = jnp.zeros_like(rest[-1])

    land = lax.empty((N_DEV, *packed.shape), F32)
    sems = pltpu.SemaphoreType.DMA((N_DEV - 1,))
    return pl.pallas_call(
        body, name=name, in_specs=[STRICT_HBM_SPEC] * 2,
        out_shape=(sems, sems, pltpu.HBM(packed.shape, F32), pltpu.HBM(land.shape, F32), _sds((8, 128), F32)),
        out_specs=(SEM_SPEC, SEM_SPEC, STRICT_HBM_SPEC, STRICT_HBM_SPEC, _token_spec()),
        input_output_aliases={0: 2, 1: 3},
        compiler_params=pltpu.CompilerParams(has_side_effects=EFFECT))(_hbm(packed), _hbm(land))


def _gather_wait(started, after, name):
    n = len(started)

    def body(*refs):
        x, y, c, _ = _place()
        for s in range(n):
            src, land, send, recv = refs[2 * s], refs[2 * s + 1], refs[2 * n + 2 * s], refs[2 * n + 2 * s + 1]
            for k, (peer, _, _, p_dev) in enumerate(_peers(x, y, c)):
                cp = pltpu.make_async_remote_copy(src_ref=src, dst_ref=land.at[p_dev], send_sem=send.at[k],
                                                  recv_sem=recv.at[k], device_id=peer,
                                                  device_id_type=MESH)
                cp.wait_send()
                cp.wait_recv()

    arrays = [a for _, _, packed, land in started for a in (packed, land)]
    sems = [s for send, recv, _, _ in started for s in (send, recv)]
    out = pl.pallas_call(
        body, name=name, in_specs=[STRICT_HBM_SPEC] * (2 * n) + [SEM_SPEC] * (2 * n) + [HBM_SPEC],
        out_shape=tuple(pltpu.HBM(a.shape, a.dtype) for a in arrays), out_specs=tuple([STRICT_HBM_SPEC] * (2 * n)),
        input_output_aliases={a: a for a in range(2 * n)},
        compiler_params=pltpu.CompilerParams(has_side_effects=EFFECT))(*arrays, *sems, after)
    return [(out[2 * s], out[2 * s + 1]) for s in range(n)]


def _sum_gathered(packed, land, device, name):
    R, C = packed.shape
    tr = 8

    def body(dev_ref, own_ref, *rest):
        me = dev_ref[0]
        acc = None
        for d in range(N_DEV):
            term = jnp.where(me == d, own_ref[...], rest[d][...])
            acc = term if acc is None else acc + term
        rest[-1][...] = acc

    def slab(d):
        return pl.BlockSpec((None, tr, C), lambda i, dev_ref: (jnp.where(dev_ref[0] == d, (d + 1) % N_DEV, d), i, 0))

    grid_spec = pltpu.PrefetchScalarGridSpec(
        num_scalar_prefetch=1, grid=(R // tr,),
        in_specs=[pl.BlockSpec((tr, C), lambda i, dev_ref: (i, 0))] + [slab(d) for d in range(N_DEV)],
        out_specs=pl.BlockSpec((tr, C), lambda i, dev_ref: (i, 0)))
    return pl.pallas_call(body, name=name, grid_spec=grid_spec, out_shape=_sds((R, C), F32),
                          compiler_params=_cparams(1))(device, packed, *[land] * N_DEV)


def _pack_small(arrays):
    rows = []
    for a in arrays:
        flat = a.reshape(-1)
        pad = (-flat.shape[0]) % SMALL_COLS
        rows.append(jnp.pad(flat, (0, pad)).reshape(-1, SMALL_COLS))
    packed = jnp.concatenate(rows, axis=0)
    return jnp.pad(packed, ((0, (-packed.shape[0]) % 8), (0, 0)))


def _unpack_small(packed, shapes):
    out, row = [], 0
    for shape in shapes:
        size = math.prod(shape)
        n_rows = -(-size // SMALL_COLS)
        out.append(packed[row:row + n_rows].reshape(-1)[:size].reshape(shape))
        row += n_rows
    return out


def kernel(x, a_norm, a_w_in, a_sgu_norm, a_w_spatial, a_b_spatial, a_w_out, kv_norm, w_kv, b_norm, b_w_q, b_rel_bias, b_w_o, ffn_norm, ffn_w_gate_up, ffn_w_down, final_norm, loss_target, m_a_norm, m_a_w_in, m_a_sgu_norm, m_a_w_spatial, m_a_b_spatial, m_a_w_out, m_kv_norm, m_w_kv, m_b_norm, m_b_w_q, m_b_rel_bias, m_b_w_o, m_ffn_norm, m_ffn_w_gate_up, m_ffn_w_down, m_final_norm, v_a_norm, v_a_w_in, v_a_sgu_norm, v_a_w_spatial, v_a_b_spatial, v_a_w_out, v_kv_norm, v_w_kv, v_b_norm, v_b_w_q, v_b_rel_bias, v_b_w_o, v_ffn_norm, v_ffn_w_gate_up, v_ffn_w_down, v_final_norm):
    S, D = x.shape[1], x.shape[2]
    n_a = a_w_in.shape[0]
    n_b = b_w_q.shape[0]
    depth = ffn_w_gate_up.shape[0]
    xi, yi, ci = lax.axis_index("x"), lax.axis_index("y"), lax.axis_index("c")
    chip = 2 * xi + yi

    place = jnp.stack([chip, ci, 2 * chip + ci]).astype(jnp.int32)
    stacked = {"a_w_in": a_w_in, "a_w_out": a_w_out, "w_kv": w_kv[None], "b_w_q": b_w_q, "b_w_o": b_w_o,
               "ffn_w_gate_up": ffn_w_gate_up, "ffn_w_down": ffn_w_down}
    groups = []
    for layer in range(depth):
        if layer < n_a:
            groups.append([("a_w_in", layer), ("a_w_out", layer)])
        elif layer == n_a:
            groups.append([("w_kv", 0), ("b_w_q", 0), ("b_w_o", 0)])
        else:
            groups.append([("b_w_q", layer - n_a), ("b_w_o", layer - n_a)])
        groups.append([("ffn_w_gate_up", layer), ("ffn_w_down", layer)])
    units = [u for group in groups for u in group]
    n_early = len(groups[0])
    slabs = [_cast_slab(stacked[k], l, place[:1], f"cast_{k}_{l}") for k, l in units[:n_early]]
    early = _allgather_start(slabs, "allgather_start_first")
    slabs = [_cast_slab(stacked[k], l, place[:1], f"cast_{k}_{l}", deps=(early[3],)) for k, l in units[n_early:]]
    late = _allgather_start(slabs, "allgather_start_rest")
    na_w, ns_w = a_norm.shape[1], a_sgu_norm.shape[1]
    small_g = _allgather_small(jnp.concatenate([a_norm, a_sgu_norm], axis=1), "allgather_small")
    a_norm_f = small_g[:, :, :na_w].transpose(1, 0, 2).reshape(n_a, N_CHIPS * na_w)
    a_sgu_f = small_g[:, :, na_w:].transpose(1, 0, 2).reshape(n_a, N_CHIPS * ns_w)
    W, relayed = {}, {}

    def relay(group_index, after):
        if group_index == len(groups):
            return ()
        group = groups[group_index]
        (send, recv, flying, _), first = (early, 0) if group_index == 0 else (late, units.index(group[0]) - n_early)
        relayed[group_index] = _allgather_relay(flying[first:first + len(group)], send, recv, first, after,
                                                f"allgather_relay_{group_index}")
        return (relayed[group_index][3],)

    def gathered(group_index, after):
        send2, recv2, arrays, _ = relayed.pop(group_index)
        W.update(zip(groups[group_index], _allgather_wait(arrays, send2, recv2, after, f"allgather_wait_{group_index}")))

    xc = x.reshape(S, D)
    saved = []
    kvp = x_kv = h_kv = None
    relay(0, late[3])
    order = ()
    for layer in range(depth):
        rec = {"x_in": xc}
        gathered(2 * layer, xc)
        if layer < n_a:
            i = layer
            rec["zpre"], rec["h"] = _norm_matmul(xc, a_norm_f[i][None], W["a_w_in", i], BF16, f"a{i}_in", deps=order)
            order = relay(2 * layer + 1, rec["h"])
            rec["uv"] = _sgu_fwd(rec["zpre"], a_sgu_f[i][None], a_w_spatial[i], a_b_spatial[i].T, f"a{i}_sgu",
                                 deps=order)
            xm = _matmul_res(rec["uv"], W["a_w_out", i], xc, f"a{i}_out")
        else:
            i = layer - n_a
            if i == 0:
                kv, h_kv = _norm_matmul(xc, kv_norm[None], W["w_kv", 0], BF16, "kv_proj")
                kvp = jnp.pad(kv, ((KV_PAD, 0), (0, 0)))
                x_kv = xc
            rec["q"], rec["h"] = _norm_matmul(xc, b_norm[i][None], W["b_w_q", i], BF16, f"b{i}_q", row_sharded=True)
            order = relay(2 * layer + 1, rec["h"])
            table = jnp.pad(b_rel_bias[i], ((0, 0), (0, TABLE_PAD - b_rel_bias.shape[2])))
            rec["bias"] = _rel_bias_fwd(table, f"b{i}_bias").transpose(1, 0, 2)
            rec["o"] = _attn_fwd(rec["q"], kvp, rec["bias"], f"b{i}_attn", deps=order)
            xm = _matmul_res(rec["o"], W["b_w_o", i], xc, f"b{i}_o")
        rec["x_mid"] = xm
        gathered(2 * layer + 1, xm)
        rec["gu"], rec["h_f"] = _norm_matmul(xm, ffn_norm[layer][None], W["ffn_w_gate_up", layer], BF16, f"f{layer}_in")
        order = relay(2 * layer + 2, rec["h_f"])
        xc = _matmul_res(rec["gu"], W["ffn_w_down", layer], xm, f"f{layer}_out", swiglu=True, deps=order)
        order = ()
        saved.append(rec)

    loss_tile, dx, d_final = _loss_head(xc, final_norm[None], loss_target.reshape(S, D), "loss_head")
    loss = lax.psum(loss_tile[0, 0], ("x", "y", "c"))

    started = []
    small_started = []
    pending = []

    def weight_grad(unit, **kw):
        full = (N_CHIPS,) + tuple(stacked[unit[0]].shape[1:])
        g = _matmul_tn(out_shape=full, name=f"d_{unit[0]}_{unit[1]}", deps=tuple(pending), **kw)
        pending.clear()
        send, recv, flying_g, flying_land, token = _reduce_start([g], f"reduce_start_{unit[0]}_{unit[1]}")
        started.append((unit, send, recv, flying_g, flying_land))
        return token

    tt = 512
    tb = min(NORMBWD_ROWS, S)
    row_a = lambda w: pl.BlockSpec((tt, w), lambda o, t: (t, 0))
    d_ffn_norm, d_b_norm, d_a_norm, d_a_sgu = [None] * depth, [None] * n_b, [None] * n_a, [None] * n_a
    d_ws, d_bs, d_rel = [None] * n_a, [None] * n_a, [None] * n_b
    dkv = jnp.zeros((2, S + KV_PAD, D), F32)
    first = lambda ref: ref[...]
    for layer in reversed(range(depth)):
        rec = saved[layer]
        r_d = ffn_w_down.shape[1]
        half_f = 2 * r_d
        token = weight_grad(
            ("ffn_w_down", layer), a_ops=[rec["gu"], rec["gu"]],
            a_specs=[pl.BlockSpec((tt, half_f), lambda o, t: (t, o)), pl.BlockSpec((tt, half_f), lambda o, t: (t, 2 + o))],
            a_fn=lambda g_ref, u_ref: _swiglu(g_ref[...].astype(F32), u_ref[...].astype(F32)).astype(BF16),
            b_op=dx, b_spec=row_a(D), out_spec=pl.BlockSpec((2, r_d, D), lambda o, t: (o, 0, 0)),
            acc_shape=(half_f, D), n_outer=2)
        dgu = _nt_swiglu_bwd(dx, W["ffn_w_down", layer], rec["gu"], f"f{layer}_dgu", deps=(token,))
        nsh = ffn_w_gate_up.shape[2]
        token = weight_grad(
            ("ffn_w_gate_up", layer), a_ops=[rec["h_f"]], a_specs=[row_a(D)], a_fn=first,
            b_op=dgu, b_spec=pl.BlockSpec((None, tt, nsh), lambda o, t: (o // 2, t, o % 2)),
            out_spec=pl.BlockSpec((None, D, nsh), lambda o, t: (o, 0, 0)), acc_shape=(D, nsh), n_outer=N_CHIPS)
        dx, d_ffn_norm[layer] = _nt_normbwd(
            dgu, pl.BlockSpec((None, tb, nsh), lambda i, k: (k // 2, i, k % 2)),
            W["ffn_w_gate_up", layer], pl.BlockSpec((None, D, nsh), lambda i, k: (k, 0, 0)),
            (D, nsh), N_CHIPS, rec["x_mid"], ffn_norm[layer][None], dx, f"f{layer}_dx", deps=(token,))
        if layer >= n_a:
            i = layer - n_a
            r_o = b_w_o.shape[1]
            token = weight_grad(
                ("b_w_o", i), a_ops=[rec["o"]], a_specs=[row_a(D)], a_fn=first, b_op=dx, b_spec=row_a(D),
                out_spec=pl.BlockSpec((N_CHIPS, r_o, D), lambda o, t: (0, 0, 0)), acc_shape=(D, D), n_outer=1)
            do = _nt_rows(dx, W["b_w_o", i], N_CHIPS, BF16, f"b{i}_do", deps=(token,))
            dq, dkv, dbias = _attn_bwd(rec["q"], kvp, rec["bias"], do, dkv, f"b{i}_attn_bwd")
            d_rel[i] = _rel_bias_bwd(dbias.transpose(1, 0, 2), f"b{i}_dbias")[:, :b_rel_bias.shape[2]]
            token = weight_grad(
                ("b_w_q", i), a_ops=[rec["h"]], a_specs=[row_a(D)], a_fn=first, b_op=dq, b_spec=row_a(D),
                out_spec=pl.BlockSpec((N_CHIPS, r_o, D), lambda o, t: (0, 0, 0)), acc_shape=(D, D), n_outer=1)
            dx, d_b_norm[i] = _nt_normbwd(
                dq, pl.BlockSpec((tb, D), lambda i_, k: (i_, 0)),
                W["b_w_q", i], pl.BlockSpec((N_CHIPS, r_o, D), lambda i_, k: (0, 0, 0)),
                (D, D), 1, rec["x_in"], b_norm[i][None], dx, f"b{i}_dx", deps=(token,))
            if i == 0:
                dkv_b = dkv[:, KV_PAD:, :].astype(BF16)
                n_kv = w_kv.shape[1]
                token = weight_grad(
                    ("w_kv", 0), a_ops=[h_kv], a_specs=[row_a(D)], a_fn=first,
                    b_op=dkv_b, b_spec=pl.BlockSpec((None, tt, n_kv), lambda o, t: (o // 2, t, o % 2)),
                    out_spec=pl.BlockSpec((None, D, n_kv), lambda o, t: (o, 0, 0)), acc_shape=(D, n_kv),
                    n_outer=N_CHIPS)
                dx, d_kv_norm = _nt_normbwd(
                    dkv_b, pl.BlockSpec((None, tb, n_kv), lambda i_, k: (k // 2, i_, k % 2)),
                    W["w_kv", 0], pl.BlockSpec((None, D, n_kv), lambda i_, k: (k, 0, 0)),
                    (D, n_kv), N_CHIPS, x_kv, kv_norm[None], dx, "kv_dx", deps=(token,))
        else:
            i = layer
            r_w = a_w_out.shape[1]
            token = weight_grad(
                ("a_w_out", i), a_ops=[rec["uv"]], a_specs=[row_a(N_CHIPS * r_w)], a_fn=first,
                b_op=dx, b_spec=row_a(D), out_spec=pl.BlockSpec((N_CHIPS, r_w, D), lambda o, t: (0, 0, 0)),
                acc_shape=(N_CHIPS * r_w, D), n_outer=1)
            duv = _nt_rows(dx, W["a_w_out", i], 2, F32, f"a{i}_duv", deps=(token,))
            dz, d_a_sgu[i], d_ws[i], dbs = _sgu_bwd(rec["zpre"], duv, a_sgu_f[i][None], a_w_spatial[i],
                                                  a_w_spatial[i].transpose(0, 2, 1), a_b_spatial[i].T, f"a{i}_sgu_bwd")
            d_bs[i] = dbs[:, :, 0]
            if i == 0:
                batch = [d_a_sgu[0], d_ws[0][None], d_bs[0][None], d_ffn_norm[0]]
                small_started.append((batch, _gather_start(_pack_small(batch), "gather_start_late")))
                pending.append(small_started[-1][1][4])
            n_in = a_w_in.shape[2]
            token = weight_grad(
                ("a_w_in", i), a_ops=[rec["h"]], a_specs=[row_a(D)], a_fn=first,
                b_op=dz, b_spec=pl.BlockSpec((tt, n_in), lambda o, t: (t, o)),
                out_spec=pl.BlockSpec((None, D, n_in), lambda o, t: (o, 0, 0)), acc_shape=(D, n_in), n_outer=N_CHIPS)
            dx, d_a_norm[i] = _nt_normbwd(
                dz, pl.BlockSpec((tb, n_in), lambda i_, k: (i_, k)),
                W["a_w_in", i], pl.BlockSpec((None, D, n_in), lambda i_, k: (k, 0, 0)),
                (D, n_in), N_CHIPS, rec["x_in"], a_norm_f[i][None], dx, f"a{i}_dx", deps=(token,))
        if layer == 1:
            batch = [jnp.concatenate(d_a_norm[1:], axis=0), jnp.concatenate(d_a_sgu[1:], axis=0), jnp.stack(d_ws[1:]),
                     jnp.stack(d_bs[1:]), d_kv_norm, jnp.concatenate(d_b_norm, axis=0), jnp.stack(d_rel),
                     jnp.concatenate(d_ffn_norm[1:], axis=0), d_final]
            small_started.append((batch, _gather_start(_pack_small(batch), "gather_start_early")))
            pending.append(small_started[-1][1][4])
    grad_x = dx.reshape(x.shape)

    landed = _reduce_wait([(send, recv, g, land) for _, send, recv, g, land in started], dx, "reduce_wait")
    halves = [_reduce_sum(g[0], land[0], place, f"reduce_sum_{unit[0]}_{unit[1]}")
              for (unit, *_), (g, land) in zip(started, landed)]
    joined = dict(zip([s[0] for s in started], _sibling_join(halves, "reduce_join")))
    reduced = {}

    (packed_e, land_e), (packed_l, land_l) = _gather_wait([s[:4] for _, s in small_started], dx, "gather_wait")
    total_e = _sum_gathered(packed_e, land_e, place[2:], "sum_small_grads_early")
    total_l = _sum_gathered(packed_l, land_l, place[2:], "sum_small_grads_late")
    total_t = _sum_devices(_gather_small(_pack_small([d_a_norm[0]]), "gather_small_grads_last"), "sum_small_grads_last")
    (e_a_norm, e_a_sgu, e_ws, e_bs, g_kv_norm, g_b_norm, g_rel, e_ffn_norm, g_final) = _unpack_small(
        total_e, [a.shape for a in small_started[0][0]])
    l_a_sgu, l_ws, l_bs, l_ffn_norm = _unpack_small(total_l, [a.shape for a in small_started[1][0]])
    (t_a_norm,) = _unpack_small(total_t, [d_a_norm[0].shape])
    g_a_norm = jnp.concatenate([t_a_norm, e_a_norm], axis=0)
    g_a_sgu = jnp.concatenate([l_a_sgu, e_a_sgu], axis=0)
    g_ws = jnp.concatenate([l_ws, e_ws], axis=0)
    g_bs = jnp.concatenate([l_bs, e_bs], axis=0)
    g_ffn_norm = jnp.concatenate([l_ffn_norm, e_ffn_norm], axis=0)
    reduced["a_norm"] = lax.dynamic_slice_in_dim(g_a_norm, chip * na_w, na_w, axis=1)
    reduced["a_sgu_norm"] = lax.dynamic_slice_in_dim(g_a_sgu, chip * ns_w, ns_w, axis=1)
    reduced.update(a_w_spatial=g_ws, a_b_spatial=g_bs, kv_norm=g_kv_norm.reshape(kv_norm.shape), b_norm=g_b_norm,
                   b_rel_bias=g_rel, ffn_norm=g_ffn_norm, final_norm=g_final.reshape(final_norm.shape))

    weights = dict(a_norm=a_norm, a_w_in=a_w_in, a_sgu_norm=a_sgu_norm, a_w_spatial=a_w_spatial,
                   a_b_spatial=a_b_spatial, a_w_out=a_w_out, kv_norm=kv_norm, w_kv=w_kv, b_norm=b_norm, b_w_q=b_w_q,
                   b_rel_bias=b_rel_bias, b_w_o=b_w_o, ffn_norm=ffn_norm, ffn_w_gate_up=ffn_w_gate_up,
                   ffn_w_down=ffn_w_down, final_norm=final_norm)
    m_in = dict(a_norm=m_a_norm, a_w_in=m_a_w_in, a_sgu_norm=m_a_sgu_norm, a_w_spatial=m_a_w_spatial,
                a_b_spatial=m_a_b_spatial, a_w_out=m_a_w_out, kv_norm=m_kv_norm, w_kv=m_w_kv, b_norm=m_b_norm,
                b_w_q=m_b_w_q, b_rel_bias=m_b_rel_bias, b_w_o=m_b_w_o, ffn_norm=m_ffn_norm,
                ffn_w_gate_up=m_ffn_w_gate_up, ffn_w_down=m_ffn_w_down, final_norm=m_final_norm)
    v_in = dict(a_norm=v_a_norm, a_w_in=v_a_w_in, a_sgu_norm=v_a_sgu_norm, a_w_spatial=v_a_w_spatial,
                a_b_spatial=v_a_b_spatial, a_w_out=v_a_w_out, kv_norm=v_kv_norm, w_kv=v_w_kv, b_norm=v_b_norm,
                b_w_q=v_b_w_q, b_rel_bias=v_b_rel_bias, b_w_o=v_b_w_o, ffn_norm=v_ffn_norm,
                ffn_w_gate_up=v_ffn_w_gate_up, ffn_w_down=v_ffn_w_down, final_norm=v_final_norm)
    grad_out, delta_out, m_out, v_out = [], [], [], []
    for key, w in weights.items():
        if key in stacked:
            as_layers = lambda a: a.reshape(stacked[key].shape)
            bufs = None
            for layer in range(stacked[key].shape[0]):
                bufs = _adamw_layer(as_layers(w), joined[key, layer], as_layers(m_in[key]), as_layers(v_in[key]),
                                    layer, bufs, f"adamw_{key}_{layer}")
            g, d, nm, nv = bufs
        else:
            g = reduced[key].reshape(w.shape)
            view = (1, w.shape[0]) if w.ndim == 1 else (-1, w.shape[-1])
            d, nm, nv = _adamw(w.reshape(view), g.reshape(view), m_in[key].reshape(view), v_in[key].reshape(view),
                               "adamw_" + key)
        grad_out.append(g.reshape(w.shape))
        delta_out.append(d.reshape(w.shape))
        m_out.append(nm.reshape(w.shape))
        v_out.append(nv.reshape(w.shape))
    return (loss, grad_x, *grad_out, *delta_out, *m_out, *v_out)
```

```python
import math

import jax
import jax.numpy as jnp
from jax import lax
from jax.experimental import pallas as pl
from jax.experimental.pallas import tpu as pltpu

F32, BF16 = jnp.float32, jnp.bfloat16
MESH = pl.DeviceIdType.MESH
HIGHEST = lax.Precision.HIGHEST
NT_DIMS = (((1,), (1,)), ((), ()))
TN_DIMS = (((0,), (0,)), ((), ()))

EPS = 1e-6
CHUNK = 64
A_CHUNK = 128
A_GROUPS = 8
N_HEADS = 16
HEAD_DIM = 64
N_LEFT = 8
MAX_REL = 256
ATTN_SCALE = HEAD_DIM ** -0.5
NEG_INF = -1e30
Q_BLOCK = 2 * CHUNK
KV_PAD = N_LEFT * CHUNK
BAND = KV_PAD + Q_BLOCK
DIAGS = BAND + Q_BLOCK
TABLE_PAD = 640
HEADS_PER_BLOCK = 2
BLOCKS_PER_STEP = 4
NORMBWD_ROWS = 1024

ADAM_LR, ADAM_B1, ADAM_B2, ADAM_EPS, ADAM_WD, ADAM_STEP = 0.001, 0.9, 0.999, 1e-08, 0.01, 10

VMEM_LIMIT_BYTES = 56 * 1024 * 1024
N_CHIPS = 4
N_DEV = 8
SMALL_COLS = 1024


def _cparams(n_grid):
    return pltpu.CompilerParams(dimension_semantics=("arbitrary",) * n_grid, vmem_limit_bytes=VMEM_LIMIT_BYTES)


def _sds(shape, dtype):
    return jax.ShapeDtypeStruct(tuple(shape), dtype)


def _gelu(x):
    return x * (0.5 * (1.0 + lax.erf(x * math.sqrt(0.5))))


def _gelu_and_grad(x):
    cdf = 0.5 * (1.0 + lax.erf(x * math.sqrt(0.5)))
    return x * cdf, cdf + x * (jnp.exp(-0.5 * x * x) * (1.0 / math.sqrt(2.0 * math.pi)))


def _rms_hat(xv):
    r = lax.rsqrt(jnp.mean(xv * xv, axis=-1, keepdims=True) + EPS)
    return xv * r, r


def _rms_bwd(xhat, r, g, dy):
    dxhat = dy * g
    dx = r * (dxhat - xhat * jnp.mean(dxhat * xhat, axis=-1, keepdims=True))
    return dx, dy * xhat


def _swiglu(gate, up):
    return (gate * jax.nn.sigmoid(gate)) * up


def _row_tile(rows, cols, itemsize, cap_bytes, align):
    t = rows
    while t * cols * itemsize > cap_bytes and t % (2 * align) == 0:
        t //= 2
    return t


def _cast_slab(w, layer, chip, name, deps=()):
    _, r, C = w.shape
    tr = _row_tile(r, C, 4, 4 * 1024 * 1024, 16)

    def body(chip_ref, w_ref, *rest):
        del chip_ref
        rest[-1][...] = w_ref[...].astype(BF16)

    grid_spec = pltpu.PrefetchScalarGridSpec(
        num_scalar_prefetch=1, grid=(r // tr,),
        in_specs=[pl.BlockSpec((None, tr, C), lambda i, chip_ref: (layer, i, 0))] + [HBM_SPEC] * len(deps),
        out_specs=pl.BlockSpec((None, tr, C), lambda i, chip_ref: (chip_ref[0], i, 0)))
    return pl.pallas_call(body, name=name, grid_spec=grid_spec, out_shape=_sds((N_CHIPS, r, C), BF16),
                          compiler_params=_cparams(1))(chip, w, *deps)


def _adamw_layer(w, g, m, v, layer, bufs, name):
    L, r, C = w.shape
    tr = _row_tile(r, C, 4, 1024 * 1024, 8)

    def body(w_ref, g_ref, m_ref, v_ref, *rest):
        go_ref, d_ref, nm_ref, nv_ref = rest[-4:]
        gv = g_ref[...]
        mn = ADAM_B1 * m_ref[...] + (1.0 - ADAM_B1) * gv
        vn = ADAM_B2 * v_ref[...] + (1.0 - ADAM_B2) * jnp.square(gv)
        m_hat = mn / (1.0 - ADAM_B1 ** ADAM_STEP)
        v_hat = vn / (1.0 - ADAM_B2 ** ADAM_STEP)
        d_ref[...] = -ADAM_LR * (m_hat / (jnp.sqrt(v_hat) + ADAM_EPS) + ADAM_WD * w_ref[...])
        nm_ref[...] = mn
        nv_ref[...] = vn
        go_ref[...] = gv

    stacked = pl.BlockSpec((None, tr, C), lambda i: (layer, i, 0))
    in_specs = [stacked, pl.BlockSpec((tr, C), lambda i: (i, 0)), stacked, stacked]
    ops = [w, g, m, v]
    aliases = {}
    if bufs is not None:
        in_specs += [HBM_SPEC] * 4
        ops += list(bufs)
        aliases = {4 + k: k for k in range(4)}
    return pl.pallas_call(body, name=name, grid=(r // tr,), in_specs=in_specs, out_specs=[stacked] * 4,
                          out_shape=[_sds((L, r, C), F32)] * 4, input_output_aliases=aliases,
                          compiler_params=_cparams(1))(*ops)


def _adamw(w, g, m, v, name):
    R, C = w.shape
    tr = _row_tile(R, C, 4, 1024 * 1024, 8)

    def body(w_ref, g_ref, m_ref, v_ref, d_ref, nm_ref, nv_ref):
        gv = g_ref[...]
        mn = ADAM_B1 * m_ref[...] + (1.0 - ADAM_B1) * gv
        vn = ADAM_B2 * v_ref[...] + (1.0 - ADAM_B2) * jnp.square(gv)
        m_hat = mn / (1.0 - ADAM_B1 ** ADAM_STEP)
        v_hat = vn / (1.0 - ADAM_B2 ** ADAM_STEP)
        d_ref[...] = -ADAM_LR * (m_hat / (jnp.sqrt(v_hat) + ADAM_EPS) + ADAM_WD * w_ref[...])
        nm_ref[...] = mn
        nv_ref[...] = vn

    spec = pl.BlockSpec((tr, C), lambda i: (i, 0))
    return pl.pallas_call(body, name=name, grid=(R // tr,), in_specs=[spec] * 4, out_specs=[spec] * 3,
                          out_shape=[_sds((R, C), F32)] * 3, compiler_params=_cparams(1))(w, g, m, v)


def _norm_matmul(x, g, w_g, out_dtype, name, row_sharded=False, deps=(), tm=1024):
    S, D = x.shape
    tm = min(tm, S)
    if row_sharded:
        r, N = w_g.shape[1], w_g.shape[2]
        tn = 512
        w_spec = pl.BlockSpec((N_CHIPS, r, tn), lambda i, j: (0, 0, j))
    else:
        nsh = w_g.shape[2]
        N = N_CHIPS * nsh
        tn = 512 if nsh % 512 == 0 else nsh
        bps = nsh // tn
        w_spec = pl.BlockSpec((None, D, tn), lambda i, j: (j // bps, 0, j % bps))

    def body(x_ref, g_ref, w_ref, *rest):
        y_ref, h_ref = rest[-2:]

        @pl.when(pl.program_id(1) == 0)
        def _():
            xhat, _ = _rms_hat(x_ref[...])
            h_ref[...] = (xhat * g_ref[...]).astype(BF16)

        w = w_ref[...].reshape(D, tn)
        y_ref[...] = jnp.dot(h_ref[...], w, preferred_element_type=F32).astype(y_ref.dtype)

    return pl.pallas_call(
        body, name=name, grid=(S // tm, N // tn),
        in_specs=[pl.BlockSpec((tm, D), lambda i, j: (i, 0)), pl.BlockSpec((1, D), lambda i, j: (0, 0)), w_spec]
        + [HBM_SPEC] * len(deps),
        out_specs=[pl.BlockSpec((tm, tn), lambda i, j: (i, j)), pl.BlockSpec((tm, D), lambda i, j: (i, 0))],
        out_shape=[_sds((S, N), out_dtype), _sds((S, D), BF16)],
        compiler_params=_cparams(2))(x, g, w_g, *deps)


def _matmul_res(a, w_g, res, name, swiglu=False, deps=(), tm=256):
    S, N = res.shape
    r = w_g.shape[1]
    K = N_CHIPS * r

    def body(*refs):
        o_ref = refs[-1]
        if swiglu:
            gate_ref, up_ref, w_ref, res_ref = refs[:4]
            a_blk = _swiglu(gate_ref[...].astype(F32), up_ref[...].astype(F32)).astype(BF16)
        else:
            a_ref, w_ref, res_ref = refs[:3]
            a_blk = a_ref[...]
        o_ref[...] = res_ref[...] + jnp.dot(a_blk, w_ref[...].reshape(K, N), preferred_element_type=F32)

    a_specs, a_ops = [pl.BlockSpec((tm, K), lambda i: (i, 0))], [a]
    if swiglu:
        a_specs.append(pl.BlockSpec((tm, K), lambda i: (i, 1)))
        a_ops.append(a)
    row = pl.BlockSpec((tm, N), lambda i: (i, 0))
    return pl.pallas_call(
        body, name=name, grid=(S // tm,),
        in_specs=a_specs + [pl.BlockSpec((N_CHIPS, r, N), lambda i: (0, 0, 0)), row] + [HBM_SPEC] * len(deps),
        out_specs=row, out_shape=_sds((S, N), F32), compiler_params=_cparams(1))(*a_ops, w_g, res, *deps)


def _matmul_tn(a_ops, a_specs, a_fn, b_op, b_spec, out_spec, out_shape, acc_shape, n_outer, name, deps=(), tt=512):
    S = b_op.shape[-2]
    na = len(a_ops)
    nt = S // tt

    def body(*refs):
        a_refs, b_ref, o_ref, acc_ref = refs[:na], refs[na], refs[-2], refs[-1]
        t = pl.program_id(1)
        part = lax.dot_general(a_fn(*a_refs), b_ref[...].astype(BF16), TN_DIMS, preferred_element_type=F32)

        @pl.when(t == 0)
        def _():
            acc_ref[...] = part

        @pl.when(t > 0)
        def _():
            acc_ref[...] += part

        @pl.when(t == nt - 1)
        def _():
            o_ref[...] = acc_ref[...].reshape(o_ref.shape).astype(BF16)

    return pl.pallas_call(
        body, name=name, grid=(n_outer, nt), in_specs=list(a_specs) + [b_spec] + [HBM_SPEC] * len(deps),
        out_specs=out_spec, out_shape=_sds(out_shape, BF16), scratch_shapes=[pltpu.VMEM(acc_shape, F32)],
        compiler_params=_cparams(2))(*a_ops, b_op, *deps)


def _nt_accumulate(a_ref, w_ref, acc_ref, w2d, nk):
    k = pl.program_id(1)
    part = lax.dot_general(a_ref[...].astype(BF16), w_ref[...].reshape(w2d), NT_DIMS, preferred_element_type=F32)

    @pl.when(k == 0)
    def _():
        acc_ref[...] = part

    @pl.when(k > 0)
    def _():
        acc_ref[...] += part

    return k == nk - 1


def _nt_normbwd(dy, dy_spec, w_g, w_spec, w2d, nk, x, g, dres, name, deps=(), tm=NORMBWD_ROWS):
    S, D = x.shape
    tm = min(tm, S)

    def body(dy_ref, w_ref, x_ref, g_ref, dres_ref, *rest):
        dx_ref, dg_ref, acc_ref = rest[-3:]

        @pl.when((pl.program_id(0) == 0) & (pl.program_id(1) == 0))
        def _():
            dg_ref[...] = jnp.zeros_like(dg_ref)

        last = _nt_accumulate(dy_ref, w_ref, acc_ref, w2d, nk)

        @pl.when(last)
        def _():
            xhat, r = _rms_hat(x_ref[...])
            dx, dgp = _rms_bwd(xhat, r, g_ref[...], acc_ref[...])
            dx_ref[...] = dres_ref[...] + dx
            dg_ref[...] += jnp.sum(dgp, axis=0, keepdims=True)

    row = pl.BlockSpec((tm, D), lambda i, k: (i, 0))
    vec = pl.BlockSpec((1, D), lambda i, k: (0, 0))
    return pl.pallas_call(
        body, name=name, grid=(S // tm, nk),
        in_specs=[dy_spec, w_spec, row, vec, row] + [HBM_SPEC] * len(deps), out_specs=[row, vec],
        out_shape=[_sds((S, D), F32), _sds((1, D), F32)],
        scratch_shapes=[pltpu.VMEM((tm, D), F32)], compiler_params=_cparams(2))(dy, w_g, x, g, dres, *deps)


def _nt_rows(dy, w_g, shards_per_block, out_dtype, name, deps=(), tm=1024):
    S, N = dy.shape
    tm = min(tm, S)
    r = w_g.shape[1]
    tn = shards_per_block * r

    def body(dy_ref, w_ref, *rest):
        o_ref = rest[-1]
        o_ref[...] = lax.dot_general(dy_ref[...].astype(BF16), w_ref[...].reshape(tn, N), NT_DIMS,
                                     preferred_element_type=F32).astype(o_ref.dtype)

    return pl.pallas_call(
        body, name=name, grid=(S // tm, N_CHIPS // shards_per_block),
        in_specs=[pl.BlockSpec((tm, N), lambda i, j: (i, 0)),
                  pl.BlockSpec((shards_per_block, r, N), lambda i, j: (j, 0, 0))] + [HBM_SPEC] * len(deps),
        out_specs=pl.BlockSpec((tm, tn), lambda i, j: (i, j)),
        out_shape=_sds((S, N_CHIPS * r), out_dtype), compiler_params=_cparams(2))(dy, w_g, *deps)


def _nt_swiglu_bwd(dy, w_g, gu, name, deps=(), tm=512):
    S, N = dy.shape
    r = w_g.shape[1]
    tn = 2 * r
    F = N_CHIPS * r

    def body(dy_ref, w_ref, gate_ref, up_ref, *rest):
        o_ref = rest[-1]
        dact = lax.dot_general(dy_ref[...].astype(BF16), w_ref[...].reshape(tn, N), NT_DIMS,
                               preferred_element_type=F32)
        gate, up = gate_ref[...].astype(F32), up_ref[...].astype(F32)
        sg = jax.nn.sigmoid(gate)
        silu = gate * sg
        o_ref[0] = ((dact * up) * (sg + silu * (1.0 - sg))).astype(BF16)
        o_ref[1] = (dact * silu).astype(BF16)

    return pl.pallas_call(
        body, name=name, grid=(2, S // tm),
        in_specs=[pl.BlockSpec((tm, N), lambda j, i: (i, 0)),
                  pl.BlockSpec((2, r, N), lambda j, i: (j, 0, 0)),
                  pl.BlockSpec((tm, tn), lambda j, i: (i, j)),
                  pl.BlockSpec((tm, tn), lambda j, i: (i, 2 + j))] + [HBM_SPEC] * len(deps),
        out_specs=pl.BlockSpec((2, tm, tn), lambda j, i: (0, i, j)),
        out_shape=_sds((2, S, F), BF16), compiler_params=_cparams(2))(dy, w_g, gu, gu, *deps)


def _chunk_causal_mask(transposed):
    i = lax.broadcasted_iota(jnp.int32, (A_CHUNK, A_CHUNK), 0) // CHUNK
    j = lax.broadcasted_iota(jnp.int32, (A_CHUNK, A_CHUNK), 1) // CHUNK
    return ((i <= j) if transposed else (i >= j)).astype(F32)


def _sgu_fwd(zpre, g_sgu, ws, bs_t, name, deps=()):
    S, F2 = zpre.shape
    F = F2 // 2
    gd = F // A_GROUPS

    def body(zu_ref, zv_ref, g_ref, ws_ref, b_ref, *rest):
        o_ref = rest[-1]
        vhat, _ = _rms_hat(_gelu(zv_ref[...].astype(F32)))
        vn = (vhat * g_ref[...]).astype(BF16)
        u = _gelu(zu_ref[...].astype(F32))
        mask = _chunk_causal_mask(False)
        for gi in range(A_GROUPS):
            sl = slice(gi * gd, (gi + 1) * gd)
            wm = (ws_ref[gi] * mask).astype(BF16)
            vs = jnp.dot(wm, vn[:, sl], preferred_element_type=F32) + b_ref[:, gi:gi + 1]
            o_ref[:, sl] = (u[:, sl] * vs).astype(BF16)

    return pl.pallas_call(
        body, name=name, grid=(S // A_CHUNK,),
        in_specs=[pl.BlockSpec((A_CHUNK, F), lambda i: (i, 0)),
                  pl.BlockSpec((A_CHUNK, F), lambda i: (i, 1)),
                  pl.BlockSpec((1, F), lambda i: (0, 0)),
                  pl.BlockSpec((A_GROUPS, A_CHUNK, A_CHUNK), lambda i: (0, 0, 0)),
                  pl.BlockSpec((A_CHUNK, A_GROUPS), lambda i: (0, 0))] + [HBM_SPEC] * len(deps),
        out_specs=pl.BlockSpec((A_CHUNK, F), lambda i: (i, 0)),
        out_shape=_sds((S, F), BF16), compiler_params=_cparams(1))(zpre, zpre, g_sgu, ws, bs_t, *deps)


def _sgu_bwd(zpre, duv, g_sgu, ws, ws_t, bs_t, name):
    S, F2 = zpre.shape
    F = F2 // 2
    gd = F // A_GROUPS

    def body(zu_ref, zv_ref, duv_ref, g_ref, ws_ref, wst_ref, b_ref, dz_ref, dg_ref, dws_ref, dbs_ref, dvn_ref):
        @pl.when(pl.program_id(0) == 0)
        def _():
            dg_ref[...] = jnp.zeros_like(dg_ref)
            dws_ref[...] = jnp.zeros_like(dws_ref)
            dbs_ref[...] = jnp.zeros_like(dbs_ref)

        gv = g_ref[...]
        u, u_grad = _gelu_and_grad(zu_ref[...].astype(F32))
        v, v_grad = _gelu_and_grad(zv_ref[...].astype(F32))
        vhat, r = _rms_hat(v)
        vn = (vhat * gv).astype(BF16)
        duv_v = duv_ref[...]
        dvs = duv_v * u
        dvs_b = dvs.astype(BF16)
        mask = _chunk_causal_mask(False)
        mask_t = _chunk_causal_mask(True)
        for gi in range(A_GROUPS):
            sl = slice(gi * gd, (gi + 1) * gd)
            wm = (ws_ref[gi] * mask).astype(BF16)
            vs = jnp.dot(wm, vn[:, sl], preferred_element_type=F32) + b_ref[:, gi:gi + 1]
            dz_ref[:, sl] = ((duv_v[:, sl] * vs) * u_grad[:, sl]).astype(BF16)
            dws_ref[gi] += lax.dot_general(dvs_b[:, sl], vn[:, sl], NT_DIMS, preferred_element_type=F32) * mask
            dbs_ref[gi] += jnp.broadcast_to(jnp.sum(dvs[:, sl], axis=1, keepdims=True), (A_CHUNK, A_CHUNK))
            wm_t = (wst_ref[gi] * mask_t).astype(BF16)
            dvn_ref[:, sl] = jnp.dot(wm_t, dvs_b[:, sl], preferred_element_type=F32)
        dv, dg_part = _rms_bwd(vhat, r, gv, dvn_ref[...])
        dg_ref[...] += jnp.sum(dg_part, axis=0, keepdims=True)
        dz_ref[:, F:] = (dv * v_grad).astype(BF16)

    blk = pl.BlockSpec((A_CHUNK, F), lambda i: (i, 0))
    const3 = pl.BlockSpec((A_GROUPS, A_CHUNK, A_CHUNK), lambda i: (0, 0, 0))
    return pl.pallas_call(
        body, name=name, grid=(S // A_CHUNK,),
        in_specs=[blk, pl.BlockSpec((A_CHUNK, F), lambda i: (i, 1)), blk,
                  pl.BlockSpec((1, F), lambda i: (0, 0)), const3, const3,
                  pl.BlockSpec((A_CHUNK, A_GROUPS), lambda i: (0, 0))],
        out_specs=[pl.BlockSpec((A_CHUNK, F2), lambda i: (i, 0)), pl.BlockSpec((1, F), lambda i: (0, 0)),
                   const3, const3],
        out_shape=[_sds((S, F2), BF16), _sds((1, F), F32), _sds((A_GROUPS, A_CHUNK, A_CHUNK), F32),
                   _sds((A_GROUPS, A_CHUNK, A_CHUNK), F32)],
        scratch_shapes=[pltpu.VMEM((A_CHUNK, F), F32)],
        compiler_params=_cparams(1))(zpre, zpre, duv, g_sgu, ws, ws_t, bs_t)


def _toeplitz_one_hot():
    row = lax.broadcasted_iota(jnp.int32, (TABLE_PAD, DIAGS), 0)
    j = lax.broadcasted_iota(jnp.int32, (TABLE_PAD, DIAGS), 1)
    idx = jnp.clip(KV_PAD + Q_BLOCK - j, -MAX_REL, MAX_REL) + MAX_REL
    return (row == idx).astype(F32)


def _rel_bias_fwd(table, name):
    H = table.shape[0]

    def body(t_ref, o_ref):
        diag = jnp.dot(t_ref[...], _toeplitz_one_hot(), precision=HIGHEST, preferred_element_type=F32)
        k_chunk = lax.broadcasted_iota(jnp.int32, (1, BAND), 1) // CHUNK

        def step(r, carry):
            q_chunk = r // CHUNK
            seen = (k_chunk >= q_chunk) & (k_chunk <= q_chunk + N_LEFT)
            o_ref[r] = pltpu.roll(diag, DIAGS - Q_BLOCK + r, 1)[:, :BAND] + jnp.where(seen, 0.0, NEG_INF)
            return carry

        lax.fori_loop(0, Q_BLOCK, step, 0)

    return pl.pallas_call(body, name=name, out_shape=_sds((Q_BLOCK, H, BAND), F32),
                          compiler_params=pltpu.CompilerParams(vmem_limit_bytes=VMEM_LIMIT_BYTES))(table)


def _rel_bias_bwd(dbias, name):
    H = dbias.shape[1]

    def body(d_ref, o_ref):
        def step(r, acc):
            row = jnp.concatenate([d_ref[r], jnp.zeros((H, DIAGS - BAND), F32)], axis=1)
            return acc + pltpu.roll(row, Q_BLOCK - r, 1)

        diag = lax.fori_loop(0, Q_BLOCK, step, jnp.zeros((H, DIAGS), F32))
        o_ref[...] = lax.dot_general(diag, _toeplitz_one_hot(), NT_DIMS, precision=HIGHEST,
                                     preferred_element_type=F32)

    return pl.pallas_call(body, name=name, out_shape=_sds((H, TABLE_PAD), F32),
                          compiler_params=pltpu.CompilerParams(vmem_limit_bytes=VMEM_LIMIT_BYTES))(dbias)


def _head_rows(t):
    lane = lax.broadcasted_iota(jnp.int32, t.shape, 1)
    zero = jnp.zeros_like(t)
    return jnp.concatenate([jnp.where(lane < HEAD_DIM, t, zero), jnp.where(lane >= HEAD_DIM, t, zero)], axis=0)


def _head_lanes(t2):
    lane = lax.broadcasted_iota(jnp.int32, (Q_BLOCK, t2.shape[1]), 1)
    return jnp.where(lane < HEAD_DIM, t2[:Q_BLOCK], t2[Q_BLOCK:])


def _attn_probs(q2, kb, bias2, block):
    kj = lax.broadcasted_iota(jnp.int32, (1, BAND), 1)
    before_start = jnp.where(block * Q_BLOCK + kj - KV_PAD >= 0, 0.0, NEG_INF)
    s = lax.dot_general(q2 * ATTN_SCALE, kb, NT_DIMS, preferred_element_type=F32) + bias2 + before_start
    e = jnp.exp(s - jnp.max(s, axis=-1, keepdims=True))
    return e / jnp.sum(e, axis=-1, keepdims=True)


def _attn_specs(S):
    lanes = HEADS_PER_BLOCK * HEAD_DIM
    rows = S + KV_PAD
    q_spec = pl.BlockSpec((BLOCKS_PER_STEP * Q_BLOCK, lanes), lambda h, i: (i, h))
    k_spec = pl.BlockSpec((rows, lanes), lambda h, i: (0, h))
    v_spec = pl.BlockSpec((rows, lanes), lambda h, i: (0, N_HEADS // HEADS_PER_BLOCK + h))
    b_spec = pl.BlockSpec((HEADS_PER_BLOCK, Q_BLOCK, BAND), lambda h, i: (h, 0, 0))
    return q_spec, k_spec, v_spec, b_spec


def _attn_fwd(q, kvp, bias, name, deps=()):
    S, HD = q.shape
    q_spec, k_spec, v_spec, b_spec = _attn_specs(S)

    def body(q_ref, k_ref, v_ref, b_ref, *rest):
        o_ref = rest[-1]
        for b in range(BLOCKS_PER_STEP):
            block = pl.program_id(1) * BLOCKS_PER_STEP + b
            rows = slice(b * Q_BLOCK, (b + 1) * Q_BLOCK)
            band = pl.ds(pl.multiple_of(block * Q_BLOCK, Q_BLOCK), BAND)
            p = _attn_probs(_head_rows(q_ref[rows, :]), k_ref[band, :], b_ref[...].reshape(2 * Q_BLOCK, BAND), block)
            o2 = jnp.dot(p.astype(BF16), v_ref[band, :], preferred_element_type=F32)
            o_ref[rows, :] = _head_lanes(o2).astype(BF16)

    return pl.pallas_call(
        body, name=name, grid=(N_HEADS // HEADS_PER_BLOCK, S // (BLOCKS_PER_STEP * Q_BLOCK)),
        in_specs=[q_spec, k_spec, v_spec, b_spec] + [HBM_SPEC] * len(deps), out_specs=q_spec,
        out_shape=_sds((S, HD), BF16), compiler_params=_cparams(2))(q, kvp, kvp, bias, *deps)


def _attn_bwd(q, kvp, bias, do, dkv_prev, name):
    S, HD = q.shape
    lanes = HEADS_PER_BLOCK * HEAD_DIM
    q_spec, k_spec, v_spec, b_spec = _attn_specs(S)
    dkv_spec = pl.BlockSpec((2, S + KV_PAD, lanes), lambda h, i: (0, 0, h))

    def body(q_ref, k_ref, v_ref, b_ref, do_ref, prev_ref, dq_ref, dkv_ref, db_ref):
        @pl.when(pl.program_id(1) == 0)
        def _():
            dkv_ref[...] = prev_ref[...]
            db_ref[...] = jnp.zeros_like(db_ref)

        db = jnp.zeros((2 * Q_BLOCK, BAND), F32)
        for b in range(BLOCKS_PER_STEP):
            block = pl.program_id(1) * BLOCKS_PER_STEP + b
            rows = slice(b * Q_BLOCK, (b + 1) * Q_BLOCK)
            band = pl.ds(pl.multiple_of(block * Q_BLOCK, Q_BLOCK), BAND)
            kb, vb = k_ref[band, :], v_ref[band, :]
            q2, do2 = _head_rows(q_ref[rows, :]), _head_rows(do_ref[rows, :])
            p = _attn_probs(q2, kb, b_ref[...].reshape(2 * Q_BLOCK, BAND), block)
            dp = lax.dot_general(do2, vb, NT_DIMS, preferred_element_type=F32)
            ds = p * (dp - jnp.sum(dp * p, axis=-1, keepdims=True))
            db = db + ds
            ds_b = (ds * ATTN_SCALE).astype(BF16)
            dq_ref[rows, :] = _head_lanes(jnp.dot(ds_b, kb, preferred_element_type=F32)).astype(BF16)
            dkv_ref[0, band, :] += lax.dot_general(ds_b, q2, TN_DIMS, preferred_element_type=F32)
            dkv_ref[1, band, :] += lax.dot_general(p.astype(BF16), do2, TN_DIMS, preferred_element_type=F32)
        db_ref[...] += db.reshape(HEADS_PER_BLOCK, Q_BLOCK, BAND)

    return pl.pallas_call(
        body, name=name, grid=(N_HEADS // HEADS_PER_BLOCK, S // (BLOCKS_PER_STEP * Q_BLOCK)),
        in_specs=[q_spec, k_spec, v_spec, b_spec, q_spec, dkv_spec], out_specs=[q_spec, dkv_spec, b_spec],
        out_shape=[_sds((S, HD), BF16), _sds((2, S + KV_PAD, HD), F32), _sds((N_HEADS, Q_BLOCK, BAND), F32)],
        compiler_params=_cparams(2))(q, kvp, kvp, bias, do, dkv_prev)


def _loss_head(x, g, target, name, tm=512):
    S, D = x.shape

    def body(x_ref, g_ref, t_ref, loss_ref, dx_ref, dg_ref):
        @pl.when(pl.program_id(0) == 0)
        def _():
            loss_ref[...] = jnp.zeros_like(loss_ref)
            dg_ref[...] = jnp.zeros_like(dg_ref)

        xhat, r = _rms_hat(x_ref[...])
        gv = g_ref[...]
        err = xhat * gv - t_ref[...]
        loss_ref[...] += 0.5 * jnp.sum(jnp.mean(err * err, axis=-1, keepdims=True))
        dx, dgp = _rms_bwd(xhat, r, gv, err * (1.0 / D))
        dx_ref[...] = dx
        dg_ref[...] += jnp.sum(dgp, axis=0, keepdims=True)

    row = pl.BlockSpec((tm, D), lambda i: (i, 0))
    vec = pl.BlockSpec((1, D), lambda i: (0, 0))
    return pl.pallas_call(
        body, name=name, grid=(S // tm,), in_specs=[row, vec, row],
        out_specs=[pl.BlockSpec((8, 128), lambda i: (0, 0)), row, vec],
        out_shape=[_sds((8, 128), F32), _sds((S, D), F32), _sds((1, D), F32)],
        compiler_params=_cparams(1))(x, g, target)


def _place():
    x, y, c = lax.axis_index("x"), lax.axis_index("y"), lax.axis_index("c")
    chips = [(1 - x, y), (x, 1 - y), (1 - x, 1 - y)]
    return x, y, c, chips


def _half_rows(c, r):
    return pl.ds(pl.multiple_of(c * (r // 2), 8), r // 2)


HBM_SPEC = pl.BlockSpec(memory_space=pl.ANY)


STRICT_HBM_SPEC = pl.BlockSpec(memory_space=pltpu.HBM)
SEM_SPEC = pl.BlockSpec(memory_space=pltpu.SEMAPHORE)
EFFECT = pltpu.SideEffectType.DATAFLOW_SIDE_EFFECTING


def _peers(x, y, c):
    out = []
    for k in range(1, N_DEV):
        px, py, pc = (x + ((k >> 2) & 1)) % 2, (y + ((k >> 1) & 1)) % 2, (c + (k & 1)) % 2
        out.append(((px, py, pc), 2 * px + py, pc, 4 * px + 2 * py + pc))
    return out


def _token_spec():
    return pl.BlockSpec(memory_space=pltpu.VMEM)


def _hbm(a):
    return pltpu.with_memory_space_constraint(a, pltpu.HBM)


def _slab_half(ref, chip, core):
    return ref.at[2 * chip[0] + chip[1], _half_rows(core, ref.shape[1]), :]


def _allgather_start(slabs, name):
    n = len(slabs)

    def body(*refs):
        src, send, recv, token = refs[:n], refs[n], refs[n + 1], refs[-1]
        x, y, c, chips = _place()
        for a in range(n):
            own = _slab_half(src[a], (x, y), c)
            for j, chip in enumerate(chips):
                pltpu.make_async_remote_copy(src_ref=own, dst_ref=own, send_sem=send.at[3 * a + j], recv_sem=recv.at[3 * a + j],
                                             device_id=(*chip, c), device_id_type=MESH).start()
        token[...] = jnp.zeros_like(token)

    sems = pltpu.SemaphoreType.DMA((3 * n,))
    send, recv, *flying, token = pl.pallas_call(
        body, name=name, in_specs=[STRICT_HBM_SPEC] * n,
        out_shape=(sems, sems, *[pltpu.HBM(s.shape, s.dtype) for s in slabs], _sds((8, 128), F32)),
        out_specs=(SEM_SPEC, SEM_SPEC, *[STRICT_HBM_SPEC] * n, _token_spec()),
        input_output_aliases={a: a + 2 for a in range(n)},
        compiler_params=pltpu.CompilerParams(has_side_effects=EFFECT))(*[_hbm(s) for s in slabs])
    return send, recv, flying, token


def _allgather_relay(flying, send, recv, first, after, name):
    n = len(flying)

    def body(*refs):
        src, send_ref, recv_ref = refs[:n], refs[n], refs[n + 1]
        send2, recv2, token = refs[n + 3], refs[n + 4], refs[-1]
        token[...] = jnp.zeros_like(token)
        x, y, c, chips = _place()
        for a in range(n):
            for j, chip in enumerate(chips):
                cp = pltpu.make_async_remote_copy(
                    src_ref=_slab_half(src[a], (x, y), c), dst_ref=_slab_half(src[a], chip, c),
                    send_sem=send_ref.at[3 * (first + a) + j], recv_sem=recv_ref.at[3 * (first + a) + j],
                    device_id=(*chip, c), device_id_type=MESH)
                cp.wait_send()
                cp.wait_recv()
        for a in range(n):
            for j, chip in enumerate(chips):
                landed = _slab_half(src[a], chip, c)
                pltpu.make_async_remote_copy(src_ref=landed, dst_ref=landed, send_sem=send2.at[3 * a + j],
                                             recv_sem=recv2.at[3 * a + j], device_id=(x, y, 1 - c),
                                             device_id_type=MESH).start()

    sems = pltpu.SemaphoreType.DMA((3 * n,))
    send2, recv2, *relayed, token = pl.pallas_call(
        body, name=name, in_specs=[STRICT_HBM_SPEC] * n + [SEM_SPEC, SEM_SPEC, HBM_SPEC],
        out_shape=(sems, sems, *[pltpu.HBM(s.shape, s.dtype) for s in flying], _sds((8, 128), F32)),
        out_specs=(SEM_SPEC, SEM_SPEC, *[STRICT_HBM_SPEC] * n, _token_spec()),
        input_output_aliases={a: a + 2 for a in range(n)},
        compiler_params=pltpu.CompilerParams(has_side_effects=EFFECT))(*flying, send, recv, after)
    return send2, recv2, relayed, token


def _allgather_wait(relayed, send2, recv2, after, name):
    n = len(relayed)

    def body(*refs):
        src, send_ref, recv_ref = refs[:n], refs[n], refs[n + 1]
        x, y, c, chips = _place()
        for a in range(n):
            for j, chip in enumerate(chips):
                cp = pltpu.make_async_remote_copy(
                    src_ref=_slab_half(src[a], chip, c), dst_ref=_slab_half(src[a], chip, 1 - c),
                    send_sem=send_ref.at[3 * a + j], recv_sem=recv_ref.at[3 * a + j],
                    device_id=(x, y, 1 - c), device_id_type=MESH)
                cp.wait_send()
                cp.wait_recv()

    return pl.pallas_call(
        body, name=name, in_specs=[STRICT_HBM_SPEC] * n + [SEM_SPEC, SEM_SPEC, HBM_SPEC],
        out_shape=tuple(pltpu.HBM(s.shape, s.dtype) for s in relayed), out_specs=tuple([STRICT_HBM_SPEC] * n),
        input_output_aliases={a: a for a in range(n)},
        compiler_params=pltpu.CompilerParams(has_side_effects=EFFECT))(*relayed, send2, recv2, after)


def _allgather_small(small, name):
    def body(sm, osm, send, recv, local):
        x, y, c, chips = _place()
        own = pltpu.make_async_copy(sm, osm.at[2 * x + y], local)
        own.start()
        cps = [pltpu.make_async_remote_copy(src_ref=sm, dst_ref=osm.at[2 * x + y], send_sem=send.at[j],
                                            recv_sem=recv.at[j], device_id=(*chip, c), device_id_type=MESH)
               for j, chip in enumerate(chips)]
        for cp in cps:
            cp.start()
        for j, chip in enumerate(chips):
            got = osm.at[2 * chip[0] + chip[1]]
            pltpu.make_async_remote_copy(src_ref=got, dst_ref=got, send_sem=send.at[j], recv_sem=recv.at[j],
                                         device_id=(x, y, c), device_id_type=MESH).wait_recv()
        for cp in cps:
            cp.wait_send()
        own.wait()

    return pl.pallas_call(
        body, name=name, in_specs=[pl.BlockSpec(memory_space=pltpu.VMEM)], out_specs=HBM_SPEC,
        out_shape=_sds((N_CHIPS, *small.shape), small.dtype),
        scratch_shapes=[pltpu.SemaphoreType.DMA((3,)), pltpu.SemaphoreType.DMA((3,)), pltpu.SemaphoreType.DMA])(small)


def _reduce_start(grads, name):
    n = len(grads)

    def body(*refs):
        src, land, send, recv, token = refs[:n], refs[n:2 * n], refs[2 * n], refs[2 * n + 1], refs[-1]
        x, y, c, _ = _place()
        me = 4 * x + 2 * y + c
        for a in range(n):
            for k, (peer, p_chip, p_core, _) in enumerate(_peers(x, y, c)):
                pltpu.make_async_remote_copy(
                    src_ref=src[a].at[p_chip, _half_rows(p_core, src[a].shape[1]), :], dst_ref=land[a].at[me],
                    send_sem=send.at[(N_DEV - 1) * a + k], recv_sem=recv.at[(N_DEV - 1) * a + k],
                    device_id=peer, device_id_type=MESH).start()
        token[...] = jnp.zeros_like(token)

    lands = [lax.empty((N_DEV, g.shape[1] // 2, g.shape[2]), BF16) for g in grads]
    sems = pltpu.SemaphoreType.DMA(((N_DEV - 1) * n,))
    shapes = [pltpu.HBM(a.shape, a.dtype) for a in grads + lands]
    send, recv, *flying, token = pl.pallas_call(
        body, name=name, in_specs=[STRICT_HBM_SPEC] * (2 * n),
        out_shape=(sems, sems, *shapes, _sds((8, 128), F32)),
        out_specs=(SEM_SPEC, SEM_SPEC, *[STRICT_HBM_SPEC] * (2 * n), _token_spec()),
        input_output_aliases={a: a + 2 for a in range(2 * n)},
        compiler_params=pltpu.CompilerParams(has_side_effects=EFFECT))(*[_hbm(a) for a in grads + lands])
    return send, recv, flying[:n], flying[n:], token


def _reduce_wait(started, after, name):
    sizes = [len(grads) for _, _, grads, _ in started]
    n_arr = 2 * sum(sizes)

    def body(*refs):
        x, y, c, _ = _place()
        at = 0
        for s, n in enumerate(sizes):
            src, land = refs[at:at + n], refs[at + n:at + 2 * n]
            send_ref, recv_ref = refs[n_arr + 2 * s], refs[n_arr + 2 * s + 1]
            at += 2 * n
            for a in range(n):
                for k, (peer, p_chip, p_core, p_dev) in enumerate(_peers(x, y, c)):
                    cp = pltpu.make_async_remote_copy(
                        src_ref=src[a].at[p_chip, _half_rows(p_core, src[a].shape[1]), :], dst_ref=land[a].at[p_dev],
                        send_sem=send_ref.at[(N_DEV - 1) * a + k], recv_sem=recv_ref.at[(N_DEV - 1) * a + k],
                        device_id=peer, device_id_type=MESH)
                    cp.wait_send()
                    cp.wait_recv()

    arrays, sems = [], []
    for send, recv, grads, lands in started:
        arrays += list(grads) + list(lands)
        sems += [send, recv]
    out = pl.pallas_call(
        body, name=name, in_specs=[STRICT_HBM_SPEC] * n_arr + [SEM_SPEC] * len(sems) + [HBM_SPEC],
        out_shape=tuple(pltpu.HBM(a.shape, a.dtype) for a in arrays), out_specs=tuple([STRICT_HBM_SPEC] * n_arr),
        input_output_aliases={a: a for a in range(n_arr)},
        compiler_params=pltpu.CompilerParams(has_side_effects=EFFECT))(*arrays, *sems, after)
    result, at = [], 0
    for n in sizes:
        result.append((out[at:at + n], out[at + n:at + 2 * n]))
        at += 2 * n
    return result


def _reduce_sum(grad, land, place, name):
    _, r2, C = land.shape
    tr = _row_tile(r2, C, 4, 1024 * 1024, 16)
    nb = r2 // tr

    def body(place_ref, own_ref, *rest):
        del place_ref
        acc = own_ref[...].astype(F32)
        for ref in rest[:N_DEV - 1]:
            acc = acc + ref[...].astype(F32)
        rest[-1][...] = acc

    def from_dev(k):
        return pl.BlockSpec((None, tr, C), lambda i, place_ref: ((place_ref[2] + k) % N_DEV, i, 0))

    grid_spec = pltpu.PrefetchScalarGridSpec(
        num_scalar_prefetch=1, grid=(nb,),
        in_specs=[pl.BlockSpec((None, tr, C), lambda i, place_ref: (place_ref[0], place_ref[1] * nb + i, 0))]
        + [from_dev(k) for k in range(1, N_DEV)],
        out_specs=pl.BlockSpec((tr, C), lambda i, place_ref: (place_ref[1] * nb + i, 0)))
    return pl.pallas_call(body, name=name, grid_spec=grid_spec, out_shape=_sds((2 * r2, C), F32),
                          compiler_params=_cparams(1))(place, grad, *[land] * (N_DEV - 1))


def _sibling_join(halves, name):
    n = len(halves)

    def body(*refs):
        out, send, recv = refs[n:2 * n], refs[2 * n], refs[2 * n + 1]
        x, y, c, _ = _place()
        cps = []
        for w in range(n):
            mine = out[w].at[_half_rows(c, out[w].shape[0]), :]
            cps.append(pltpu.make_async_remote_copy(src_ref=mine, dst_ref=mine, send_sem=send.at[w],
                                                    recv_sem=recv.at[w], device_id=(x, y, 1 - c), device_id_type=MESH))
        for cp in cps:
            cp.start()
        for w in range(n):
            theirs = out[w].at[_half_rows(1 - c, out[w].shape[0]), :]
            pltpu.make_async_remote_copy(src_ref=theirs, dst_ref=theirs, send_sem=send.at[w], recv_sem=recv.at[w],
                                         device_id=(x, y, c), device_id_type=MESH).wait_recv()
        for cp in cps:
            cp.wait_send()

    return pl.pallas_call(
        body, name=name, in_specs=[HBM_SPEC] * n, out_specs=[HBM_SPEC] * n,
        out_shape=[_sds(a.shape, F32) for a in halves], input_output_aliases={w: w for w in range(n)},
        scratch_shapes=[pltpu.SemaphoreType.DMA((n,)), pltpu.SemaphoreType.DMA((n,))])(*halves)


def _gather_small(packed, name):
    def body(p_ref, out, send, recv, local):
        x, y, c, _ = _place()
        me = 4 * x + 2 * y + c
        own = pltpu.make_async_copy(p_ref, out.at[me], local)
        own.start()
        cps = []
        for k in range(1, N_DEV):
            fx, fy, fc = (k >> 2) & 1, (k >> 1) & 1, k & 1
            peer = ((x + fx) % 2, (y + fy) % 2, (c + fc) % 2)
            cps.append(pltpu.make_async_remote_copy(src_ref=p_ref, dst_ref=out.at[me], send_sem=send.at[k - 1],
                                                    recv_sem=recv.at[k - 1], device_id=peer, device_id_type=MESH))
        for cp in cps:
            cp.start()
        for k in range(1, N_DEV):
            fx, fy, fc = (k >> 2) & 1, (k >> 1) & 1, k & 1
            src = out.at[4 * ((x + fx) % 2) + 2 * ((y + fy) % 2) + (c + fc) % 2]
            pltpu.make_async_remote_copy(src_ref=src, dst_ref=src, send_sem=send.at[k - 1], recv_sem=recv.at[k - 1],
                                         device_id=(x, y, c), device_id_type=MESH).wait_recv()
        for cp in cps:
            cp.wait_send()
        own.wait()

    return pl.pallas_call(
        body, name=name, in_specs=[pl.BlockSpec(memory_space=pltpu.VMEM)], out_specs=HBM_SPEC,
        out_shape=_sds((N_DEV, *packed.shape), F32),
        scratch_shapes=[pltpu.SemaphoreType.DMA((N_DEV - 1,)), pltpu.SemaphoreType.DMA((N_DEV - 1,)),
                        pltpu.SemaphoreType.DMA])(packed)


def _sum_devices(gathered, name):
    _, R, C = gathered.shape

    def body(g_ref, o_ref):
        acc = g_ref[0]
        for d in range(1, N_DEV):
            acc = acc + g_ref[d]
        o_ref[...] = acc

    tr = 8
    return pl.pallas_call(
        body, name=name, grid=(R // tr,), in_specs=[pl.BlockSpec((N_DEV, tr, C), lambda i: (0, i, 0))],
        out_specs=pl.BlockSpec((tr, C), lambda i: (i, 0)), out_shape=_sds((R, C), F32),
        compiler_params=_cparams(1))(gathered)


def _gather_start(packed, name):
    def body(src, land, send, recv, *rest):
        x, y, c, _ = _place()
        for k, (peer, _, _, _) in enumerate(_peers(x, y, c)):
            pltpu.make_async_remote_copy(src_ref=src, dst_ref=land.at[4 * x + 2 * y + c], send_sem=send.at[k],
                                         recv_sem=recv.at[k], device_id=peer, device_id_type=MESH).start()
        rest[-1][...] = jnp.zeros_like(rest[-1])

    land = lax.empty((N_DEV, *packed.shape), F32)
    sems = pltpu.SemaphoreType.DMA((N_DEV - 1,))
    return pl.pallas_call(
        body, name=name, in_specs=[STRICT_HBM_SPEC] * 2,
        out_shape=(sems, sems, pltpu.HBM(packed.shape, F32), pltpu.HBM(land.shape, F32), _sds((8, 128), F32)),
        out_specs=(SEM_SPEC, SEM_SPEC, STRICT_HBM_SPEC, STRICT_HBM_SPEC, _token_spec()),
        input_output_aliases={0: 2, 1: 3},
        compiler_params=pltpu.CompilerParams(has_side_effects=EFFECT))(_hbm(packed), _hbm(land))


def _gather_wait(started, after, name):
    n = len(started)

    def body(*refs):
        x, y, c, _ = _place()
        for s in range(n):
            src, land, send, recv = refs[2 * s], refs[2 * s + 1], refs[2 * n + 2 * s], refs[2 * n + 2 * s + 1]
            for k, (peer, _, _, p_dev) in enumerate(_peers(x, y, c)):
                cp = pltpu.make_async_remote_copy(src_ref=src, dst_ref=land.at[p_dev], send_sem=send.at[k],
                                                  recv_sem=recv.at[k], device_id=peer,
                                                  device_id_type=MESH)
                cp.wait_send()
                cp.wait_recv()

    arrays = [a for _, _, packed, land in started for a in (packed, land)]
    sems = [s for send, recv, _, _ in started for s in (send, recv)]
    out = pl.pallas_call(
        body, name=name, in_specs=[STRICT_HBM_SPEC] * (2 * n) + [SEM_SPEC] * (2 * n) + [HBM_SPEC],
        out_shape=tuple(pltpu.HBM(a.shape, a.dtype) for a in arrays), out_specs=tuple([STRICT_HBM_SPEC] * (2 * n)),
        input_output_aliases={a: a for a in range(2 * n)},
        compiler_params=pltpu.CompilerParams(has_side_effects=EFFECT))(*arrays, *sems, after)
    return [(out[2 * s], out[2 * s + 1]) for s in range(n)]


def _sum_gathered(packed, land, device, name):
    R, C = packed.shape
    tr = 8

    def body(dev_ref, own_ref, *rest):
        me = dev_ref[0]
        acc = None
        for d in range(N_DEV):
            term = jnp.where(me == d, own_ref[...], rest[d][...])
            acc = term if acc is None else acc + term
        rest[-1][...] = acc

    def slab(d):
        return pl.BlockSpec((None, tr, C), lambda i, dev_ref: (jnp.where(dev_ref[0] == d, (d + 1) % N_DEV, d), i, 0))

    grid_spec = pltpu.PrefetchScalarGridSpec(
        num_scalar_prefetch=1, grid=(R // tr,),
        in_specs=[pl.BlockSpec((tr, C), lambda i, dev_ref: (i, 0))] + [slab(d) for d in range(N_DEV)],
        out_specs=pl.BlockSpec((tr, C), lambda i, dev_ref: (i, 0)))
    return pl.pallas_call(body, name=name, grid_spec=grid_spec, out_shape=_sds((R, C), F32),
                          compiler_params=_cparams(1))(device, packed, *[land] * N_DEV)


def _pack_small(arrays):
    rows = []
    for a in arrays:
        flat = a.reshape(-1)
        pad = (-flat.shape[0]) % SMALL_COLS
        rows.append(jnp.pad(flat, (0, pad)).reshape(-1, SMALL_COLS))
    packed = jnp.concatenate(rows, axis=0)
    return jnp.pad(packed, ((0, (-packed.shape[0]) % 8), (0, 0)))


def _unpack_small(packed, shapes):
    out, row = [], 0
    for shape in shapes:
        size = math.prod(shape)
        n_rows = -(-size // SMALL_COLS)
        out.append(packed[row:row + n_rows].reshape(-1)[:size].reshape(shape))
        row += n_rows
    return out


def kernel(x, a_norm, a_w_in, a_sgu_norm, a_w_spatial, a_b_spatial, a_w_out, kv_norm, w_kv, b_norm, b_w_q, b_rel_bias, b_w_o, ffn_norm, ffn_w_gate_up, ffn_w_down, final_norm, loss_target, m_a_norm, m_a_w_in, m_a_sgu_norm, m_a_w_spatial, m_a_b_spatial, m_a_w_out, m_kv_norm, m_w_kv, m_b_norm, m_b_w_q, m_b_rel_bias, m_b_w_o, m_ffn_norm, m_ffn_w_gate_up, m_ffn_w_down, m_final_norm, v_a_norm, v_a_w_in, v_a_sgu_norm, v_a_w_spatial, v_a_b_spatial, v_a_w_out, v_kv_norm, v_w_kv, v_b_norm, v_b_w_q, v_b_rel_bias, v_b_w_o, v_ffn_norm, v_ffn_w_gate_up, v_ffn_w_down, v_final_norm):
    S, D = x.shape[1], x.shape[2]
    n_a = a_w_in.shape[0]
    n_b = b_w_q.shape[0]
    depth = ffn_w_gate_up.shape[0]
    xi, yi, ci = lax.axis_index("x"), lax.axis_index("y"), lax.axis_index("c")
    chip = 2 * xi + yi

    place = jnp.stack([chip, ci, 2 * chip + ci]).astype(jnp.int32)
    stacked = {"a_w_in": a_w_in, "a_w_out": a_w_out, "w_kv": w_kv[None], "b_w_q": b_w_q, "b_w_o": b_w_o,
               "ffn_w_gate_up": ffn_w_gate_up, "ffn_w_down": ffn_w_down}
    groups = []
    for layer in range(depth):
        if layer < n_a:
            groups.append([("a_w_in", layer), ("a_w_out", layer)])
        elif layer == n_a:
            groups.append([("w_kv", 0), ("b_w_q", 0), ("b_w_o", 0)])
        else:
            groups.append([("b_w_q", layer - n_a), ("b_w_o", layer - n_a)])
        groups.append([("ffn_w_gate_up", layer), ("ffn_w_down", layer)])
    units = [u for group in groups for u in group]
    n_early = len(groups[0])
    slabs = [_cast_slab(stacked[k], l, place[:1], f"cast_{k}_{l}") for k, l in units[:n_early]]
    early = _allgather_start(slabs, "allgather_start_first")
    slabs = [_cast_slab(stacked[k], l, place[:1], f"cast_{k}_{l}", deps=(early[3],)) for k, l in units[n_early:]]
    late = _allgather_start(slabs, "allgather_start_rest")
    na_w, ns_w = a_norm.shape[1], a_sgu_norm.shape[1]
    small_g = _allgather_small(jnp.concatenate([a_norm, a_sgu_norm], axis=1), "allgather_small")
    a_norm_f = small_g[:, :, :na_w].transpose(1, 0, 2).reshape(n_a, N_CHIPS * na_w)
    a_sgu_f = small_g[:, :, na_w:].transpose(1, 0, 2).reshape(n_a, N_CHIPS * ns_w)
    W, relayed = {}, {}

    def relay(group_index, after):
        if group_index == len(groups):
            return ()
        group = groups[group_index]
        (send, recv, flying, _), first = (early, 0) if group_index == 0 else (late, units.index(group[0]) - n_early)
        relayed[group_index] = _allgather_relay(flying[first:first + len(group)], send, recv, first, after,
                                                f"allgather_relay_{group_index}")
        return (relayed[group_index][3],)

    def gathered(group_index, after):
        send2, recv2, arrays, _ = relayed.pop(group_index)
        W.update(zip(groups[group_index], _allgather_wait(arrays, send2, recv2, after, f"allgather_wait_{group_index}")))

    xc = x.reshape(S, D)
    saved = []
    kvp = x_kv = h_kv = None
    relay(0, late[3])
    order = ()
    for layer in range(depth):
        rec = {"x_in": xc}
        gathered(2 * layer, xc)
        if layer < n_a:
            i = layer
            rec["zpre"], rec["h"] = _norm_matmul(xc, a_norm_f[i][None], W["a_w_in", i], BF16, f"a{i}_in", deps=order)
            order = relay(2 * layer + 1, rec["h"])
            rec["uv"] = _sgu_fwd(rec["zpre"], a_sgu_f[i][None], a_w_spatial[i], a_b_spatial[i].T, f"a{i}_sgu",
                                 deps=order)
            xm = _matmul_res(rec["uv"], W["a_w_out", i], xc, f"a{i}_out")
        else:
            i = layer - n_a
            if i == 0:
                kv, h_kv = _norm_matmul(xc, kv_norm[None], W["w_kv", 0], BF16, "kv_proj")
                kvp = jnp.pad(kv, ((KV_PAD, 0), (0, 0)))
                x_kv = xc
            rec["q"], rec["h"] = _norm_matmul(xc, b_norm[i][None], W["b_w_q", i], BF16, f"b{i}_q", row_sharded=True)
            order = relay(2 * layer + 1, rec["h"])
            table = jnp.pad(b_rel_bias[i], ((0, 0), (0, TABLE_PAD - b_rel_bias.shape[2])))
            rec["bias"] = _rel_bias_fwd(table, f"b{i}_bias").transpose(1, 0, 2)
            rec["o"] = _attn_fwd(rec["q"], kvp, rec["bias"], f"b{i}_attn", deps=order)
            xm = _matmul_res(rec["o"], W["b_w_o", i], xc, f"b{i}_o")
        rec["x_mid"] = xm
        gathered(2 * layer + 1, xm)
        rec["gu"], rec["h_f"] = _norm_matmul(xm, ffn_norm[layer][None], W["ffn_w_gate_up", layer], BF16, f"f{layer}_in")
        order = relay(2 * layer + 2, rec["h_f"])
        xc = _matmul_res(rec["gu"], W["ffn_w_down", layer], xm, f"f{layer}_out", swiglu=True, deps=order)
        order = ()
        saved.append(rec)

    loss_tile, dx, d_final = _loss_head(xc, final_norm[None], loss_target.reshape(S, D), "loss_head")
    loss = lax.psum(loss_tile[0, 0], ("x", "y", "c"))

    started = []
    small_started = []
    pending = []

    def weight_grad(unit, **kw):
        full = (N_CHIPS,) + tuple(stacked[unit[0]].shape[1:])
        g = _matmul_tn(out_shape=full, name=f"d_{unit[0]}_{unit[1]}", deps=tuple(pending), **kw)
        pending.clear()
        send, recv, flying_g, flying_land, token = _reduce_start([g], f"reduce_start_{unit[0]}_{unit[1]}")
        started.append((unit, send, recv, flying_g, flying_land))
        return token

    tt = 512
    tw = min(2048, S)
    tb = min(NORMBWD_ROWS, S)
    row_a = lambda w, rows=tw: pl.BlockSpec((rows, w), lambda o, t: (t, 0))
    d_ffn_norm, d_b_norm, d_a_norm, d_a_sgu = [None] * depth, [None] * n_b, [None] * n_a, [None] * n_a
    d_ws, d_bs, d_rel = [None] * n_a, [None] * n_a, [None] * n_b
    dkv = jnp.zeros((2, S + KV_PAD, D), F32)
    first = lambda ref: ref[...]
    for layer in reversed(range(depth)):
        rec = saved[layer]
        r_d = ffn_w_down.shape[1]
        half_f = 2 * r_d
        token = weight_grad(
            ("ffn_w_down", layer), a_ops=[rec["gu"], rec["gu"]],
            a_specs=[pl.BlockSpec((tt, half_f), lambda o, t: (t, o)), pl.BlockSpec((tt, half_f), lambda o, t: (t, 2 + o))],
            a_fn=lambda g_ref, u_ref: _swiglu(g_ref[...].astype(F32), u_ref[...].astype(F32)).astype(BF16),
            b_op=dx, b_spec=row_a(D, tt), out_spec=pl.BlockSpec((2, r_d, D), lambda o, t: (o, 0, 0)),
            acc_shape=(half_f, D), n_outer=2, tt=tt)
        dgu = _nt_swiglu_bwd(dx, W["ffn_w_down", layer], rec["gu"], f"f{layer}_dgu", deps=(token,))
        nsh = ffn_w_gate_up.shape[2]
        token = weight_grad(
            ("ffn_w_gate_up", layer), a_ops=[rec["h_f"]], a_specs=[row_a(D)], a_fn=first,
            b_op=dgu, b_spec=pl.BlockSpec((None, tw, nsh), lambda o, t: (o // 2, t, o % 2)),
            out_spec=pl.BlockSpec((None, D, nsh), lambda o, t: (o, 0, 0)), acc_shape=(D, nsh), n_outer=N_CHIPS, tt=tw)
        dx, d_ffn_norm[layer] = _nt_normbwd(
            dgu, pl.BlockSpec((None, tb, nsh), lambda i, k: (k // 2, i, k % 2)),
            W["ffn_w_gate_up", layer], pl.BlockSpec((None, D, nsh), lambda i, k: (k, 0, 0)),
            (D, nsh), N_CHIPS, rec["x_mid"], ffn_norm[layer][None], dx, f"f{layer}_dx", deps=(token,))
        if layer >= n_a:
            i = layer - n_a
            r_o = b_w_o.shape[1]
            token = weight_grad(
                ("b_w_o", i), a_ops=[rec["o"]], a_specs=[row_a(D)], a_fn=first, b_op=dx, b_spec=row_a(D),
                out_spec=pl.BlockSpec((N_CHIPS, r_o, D), lambda o, t: (0, 0, 0)), acc_shape=(D, D), n_outer=1, tt=tw)
            do = _nt_rows(dx, W["b_w_o", i], N_CHIPS, BF16, f"b{i}_do", deps=(token,))
            dq, dkv, dbias = _attn_bwd(rec["q"], kvp, rec["bias"], do, dkv, f"b{i}_attn_bwd")
            d_rel[i] = _rel_bias_bwd(dbias.transpose(1, 0, 2), f"b{i}_dbias")[:, :b_rel_bias.shape[2]]
            token = weight_grad(
                ("b_w_q", i), a_ops=[rec["h"]], a_specs=[row_a(D)], a_fn=first, b_op=dq, b_spec=row_a(D),
                out_spec=pl.BlockSpec((N_CHIPS, r_o, D), lambda o, t: (0, 0, 0)), acc_shape=(D, D), n_outer=1, tt=tw)
            dx, d_b_norm[i] = _nt_normbwd(
                dq, pl.BlockSpec((tb, D), lambda i_, k: (i_, 0)),
                W["b_w_q", i], pl.BlockSpec((N_CHIPS, r_o, D), lambda i_, k: (0, 0, 0)),
                (D, D), 1, rec["x_in"], b_norm[i][None], dx, f"b{i}_dx", deps=(token,))
            if i == 0:
                dkv_b = dkv[:, KV_PAD:, :].astype(BF16)
                n_kv = w_kv.shape[1]
                token = weight_grad(
                    ("w_kv", 0), a_ops=[h_kv], a_specs=[row_a(D)], a_fn=first,
                    b_op=dkv_b, b_spec=pl.BlockSpec((None, tw, n_kv), lambda o, t: (o // 2, t, o % 2)),
                    out_spec=pl.BlockSpec((None, D, n_kv), lambda o, t: (o, 0, 0)), acc_shape=(D, n_kv),
                    n_outer=N_CHIPS, tt=tw)
                dx, d_kv_norm = _nt_normbwd(
                    dkv_b, pl.BlockSpec((None, tb, n_kv), lambda i_, k: (k // 2, i_, k % 2)),
                    W["w_kv", 0], pl.BlockSpec((None, D, n_kv), lambda i_, k: (k, 0, 0)),
                    (D, n_kv), N_CHIPS, x_kv, kv_norm[None], dx, "kv_dx", deps=(token,))
        else:
            i = layer
            r_w = a_w_out.shape[1]
            token = weight_grad(
                ("a_w_out", i), a_ops=[rec["uv"]], a_specs=[row_a(N_CHIPS * r_w, tw // 2)], a_fn=first,
                b_op=dx, b_spec=row_a(D, tw // 2), out_spec=pl.BlockSpec((N_CHIPS, r_w, D), lambda o, t: (0, 0, 0)),
                acc_shape=(N_CHIPS * r_w, D), n_outer=1, tt=tw // 2)
            duv = _nt_rows(dx, W["a_w_out", i], 2, F32, f"a{i}_duv", deps=(token,))
            dz, d_a_sgu[i], d_ws[i], dbs = _sgu_bwd(rec["zpre"], duv, a_sgu_f[i][None], a_w_spatial[i],
                                                  a_w_spatial[i].transpose(0, 2, 1), a_b_spatial[i].T, f"a{i}_sgu_bwd")
            d_bs[i] = dbs[:, :, 0]
            if i == 0:
                batch = [d_a_sgu[0], d_ws[0][None], d_bs[0][None], d_ffn_norm[0]]
                small_started.append((batch, _gather_start(_pack_small(batch), "gather_start_late")))
                pending.append(small_started[-1][1][4])
            n_in = a_w_in.shape[2]
            token = weight_grad(
                ("a_w_in", i), a_ops=[rec["h"]], a_specs=[row_a(D)], a_fn=first,
                b_op=dz, b_spec=pl.BlockSpec((tw, n_in), lambda o, t: (t, o)),
                out_spec=pl.BlockSpec((None, D, n_in), lambda o, t: (o, 0, 0)), acc_shape=(D, n_in), n_outer=N_CHIPS,
                tt=tw)
            dx, d_a_norm[i] = _nt_normbwd(
                dz, pl.BlockSpec((tb, n_in), lambda i_, k: (i_, k)),
                W["a_w_in", i], pl.BlockSpec((None, D, n_in), lambda i_, k: (k, 0, 0)),
                (D, n_in), N_CHIPS, rec["x_in"], a_norm_f[i][None], dx, f"a{i}_dx", deps=(token,))
        if layer == 1:
            batch = [jnp.concatenate(d_a_norm[1:], axis=0), jnp.concatenate(d_a_sgu[1:], axis=0), jnp.stack(d_ws[1:]),
                     jnp.stack(d_bs[1:]), d_kv_norm, jnp.concatenate(d_b_norm, axis=0), jnp.stack(d_rel),
                     jnp.concatenate(d_ffn_norm[1:], axis=0), d_final]
            small_started.append((batch, _gather_start(_pack_small(batch), "gather_start_early")))
            pending.append(small_started[-1][1][4])
    grad_x = dx.reshape(x.shape)

    landed = _reduce_wait([(send, recv, g, land) for _, send, recv, g, land in started], dx, "reduce_wait")
    halves = [_reduce_sum(g[0], land[0], place, f"reduce_sum_{unit[0]}_{unit[1]}")
              for (unit, *_), (g, land) in zip(started, landed)]
    joined = dict(zip([s[0] for s in started], _sibling_join(halves, "reduce_join")))
    reduced = {}

    (packed_e, land_e), (packed_l, land_l) = _gather_wait([s[:4] for _, s in small_started], dx, "gather_wait")
    total_e = _sum_gathered(packed_e, land_e, place[2:], "sum_small_grads_early")
    total_l = _sum_gathered(packed_l, land_l, place[2:], "sum_small_grads_late")
    total_t = _sum_devices(_gather_small(_pack_small([d_a_norm[0]]), "gather_small_grads_last"), "sum_small_grads_last")
    (e_a_norm, e_a_sgu, e_ws, e_bs, g_kv_norm, g_b_norm, g_rel, e_ffn_norm, g_final) = _unpack_small(
        total_e, [a.shape for a in small_started[0][0]])
    l_a_sgu, l_ws, l_bs, l_ffn_norm = _unpack_small(total_l, [a.shape for a in small_started[1][0]])
    (t_a_norm,) = _unpack_small(total_t, [d_a_norm[0].shape])
    g_a_norm = jnp.concatenate([t_a_norm, e_a_norm], axis=0)
    g_a_sgu = jnp.concatenate([l_a_sgu, e_a_sgu], axis=0)
    g_ws = jnp.concatenate([l_ws, e_ws], axis=0)
    g_bs = jnp.concatenate([l_bs, e_bs], axis=0)
    g_ffn_norm = jnp.concatenate([l_ffn_norm, e_ffn_norm], axis=0)
    reduced["a_norm"] = lax.dynamic_slice_in_dim(g_a_norm, chip * na_w, na_w, axis=1)
    reduced["a_sgu_norm"] = lax.dynamic_slice_in_dim(g_a_sgu, chip * ns_w, ns_w, axis=1)
    reduced.update(a_w_spatial=g_ws, a_b_spatial=g_bs, kv_norm=g_kv_norm.reshape(kv_norm.shape), b_norm=g_b_norm,
                   b_rel_bias=g_rel, ffn_norm=g_ffn_norm, final_norm=g_final.reshape(final_norm.shape))

    weights = dict(a_norm=a_norm, a_w_in=a_w_in, a_sgu_norm=a_sgu_norm, a_w_spatial=a_w_spatial,
                   a_b_spatial=a_b_spatial, a_w_out=a_w_out, kv_norm=kv_norm, w_kv=w_kv, b_norm=b_norm, b_w_q=b_w_q,
                   b_rel_bias=b_rel_bias, b_w_o=b_w_o, ffn_norm=ffn_norm, ffn_w_gate_up=ffn_w_gate_up,
                   ffn_w_down=ffn_w_down, final_norm=final_norm)
    m_in = dict(a_norm=m_a_norm, a_w_in=m_a_w_in, a_sgu_norm=m_a_sgu_norm, a_w_spatial=m_a_w_spatial,
                a_b_spatial=m_a_b_spatial, a_w_out=m_a_w_out, kv_norm=m_kv_norm, w_kv=m_w_kv, b_norm=m_b_norm,
                b_w_q=m_b_w_q, b_rel_bias=m_b_rel_bias, b_w_o=m_b_w_o, ffn_norm=m_ffn_norm,
                ffn_w_gate_up=m_ffn_w_gate_up, ffn_w_down=m_ffn_w_down, final_norm=m_final_norm)
    v_in = dict(a_norm=v_a_norm, a_w_in=v_a_w_in, a_sgu_norm=v_a_sgu_norm, a_w_spatial=v_a_w_spatial,
                a_b_spatial=v_a_b_spatial, a_w_out=v_a_w_out, kv_norm=v_kv_norm, w_kv=v_w_kv, b_norm=v_b_norm,
                b_w_q=v_b_w_q, b_rel_bias=v_b_rel_bias, b_w_o=v_b_w_o, ffn_norm=v_ffn_norm,
                ffn_w_gate_up=v_ffn_w_gate_up, ffn_w_down=v_ffn_w_down, final_norm=v_final_norm)
    grad_out, delta_out, m_out, v_out = [], [], [], []
    for key, w in weights.items():
        if key in stacked:
            as_layers = lambda a: a.reshape(stacked[key].shape)
            bufs = None
            for layer in range(stacked[key].shape[0]):
                bufs = _adamw_layer(as_layers(w), joined[key, layer], as_layers(m_in[key]), as_layers(v_in[key]),
                                    layer, bufs, f"adamw_{key}_{layer}")
            g, d, nm, nv = bufs
        else:
            g = reduced[key].reshape(w.shape)
            view = (1, w.shape[0]) if w.ndim == 1 else (-1, w.shape[-1])
            d, nm, nv = _adamw(w.reshape(view), g.reshape(view), m_in[key].reshape(view), v_in[key].reshape(view),
                               "adamw_" + key)
        grad_out.append(g.reshape(w.shape))
        delta_out.append(d.reshape(w.shape))
        m_out.append(nm.reshape(w.shape))
        v_out.append(nv.reshape(w.shape))
    return (loss, grad_x, *grad_out, *delta_out, *m_out, *v_out)
```

```python
import math

import jax
import jax.numpy as jnp
from jax import lax
from jax.experimental import pallas as pl
from jax.experimental.pallas import tpu as pltpu

F32, BF16 = jnp.float32, jnp.bfloat16
MESH = pl.DeviceIdType.MESH
HIGHEST = lax.Precision.HIGHEST
NT_DIMS = (((1,), (1,)), ((), ()))
TN_DIMS = (((0,), (0,)), ((), ()))

EPS = 1e-6
CHUNK = 64
A_CHUNK = 128
A_GROUPS = 8
N_HEADS = 16
HEAD_DIM = 64
N_LEFT = 8
MAX_REL = 256
ATTN_SCALE = HEAD_DIM ** -0.5
NEG_INF = -1e30
Q_BLOCK = 2 * CHUNK
KV_PAD = N_LEFT * CHUNK
BAND = KV_PAD + Q_BLOCK
DIAGS = BAND + Q_BLOCK
TABLE_PAD = 640
HEADS_PER_BLOCK = 2
BLOCKS_PER_STEP = 4
NORMBWD_ROWS = 1024

ADAM_LR, ADAM_B1, ADAM_B2, ADAM_EPS, ADAM_WD, ADAM_STEP = 0.001, 0.9, 0.999, 1e-08, 0.01, 10

VMEM_LIMIT_BYTES = 56 * 1024 * 1024
N_CHIPS = 4
N_DEV = 8
SMALL_COLS = 1024


def _cparams(n_grid):
    return pltpu.CompilerParams(dimension_semantics=("arbitrary",) * n_grid, vmem_limit_bytes=VMEM_LIMIT_BYTES)


def _sds(shape, dtype):
    return jax.ShapeDtypeStruct(tuple(shape), dtype)


def _gelu(x):
    return x * (0.5 * (1.0 + lax.erf(x * math.sqrt(0.5))))


def _gelu_and_grad(x):
    cdf = 0.5 * (1.0 + lax.erf(x * math.sqrt(0.5)))
    return x * cdf, cdf + x * (jnp.exp(-0.5 * x * x) * (1.0 / math.sqrt(2.0 * math.pi)))


def _rms_hat(xv):
    r = lax.rsqrt(jnp.mean(xv * xv, axis=-1, keepdims=True) + EPS)
    return xv * r, r


def _rms_bwd(xhat, r, g, dy):
    dxhat = dy * g
    dx = r * (dxhat - xhat * jnp.mean(dxhat * xhat, axis=-1, keepdims=True))
    return dx, dy * xhat


def _swiglu(gate, up):
    return (gate * jax.nn.sigmoid(gate)) * up


def _row_tile(rows, cols, itemsize, cap_bytes, align):
    t = rows
    while t * cols * itemsize > cap_bytes and t % (2 * align) == 0:
        t //= 2
    return t


def _cast_slab(w, layer, chip, name, deps=()):
    _, r, C = w.shape
    tr = _row_tile(r, C, 4, 4 * 1024 * 1024, 16)

    def body(chip_ref, w_ref, *rest):
        del chip_ref
        rest[-1][...] = w_ref[...].astype(BF16)

    grid_spec = pltpu.PrefetchScalarGridSpec(
        num_scalar_prefetch=1, grid=(r // tr,),
        in_specs=[pl.BlockSpec((None, tr, C), lambda i, chip_ref: (layer, i, 0))] + [HBM_SPEC] * len(deps),
        out_specs=pl.BlockSpec((None, tr, C), lambda i, chip_ref: (chip_ref[0], i, 0)))
    return pl.pallas_call(body, name=name, grid_spec=grid_spec, out_shape=_sds((N_CHIPS, r, C), BF16),
                          compiler_params=_cparams(1))(chip, w, *deps)


def _adamw_layer(w, g, m, v, layer, bufs, name):
    L, r, C = w.shape
    tr = _row_tile(r, C, 4, 2 * 1024 * 1024, 8)

    def body(w_ref, g_ref, m_ref, v_ref, *rest):
        go_ref, d_ref, nm_ref, nv_ref = rest[-4:]
        gv = g_ref[...]
        mn = ADAM_B1 * m_ref[...] + (1.0 - ADAM_B1) * gv
        vn = ADAM_B2 * v_ref[...] + (1.0 - ADAM_B2) * jnp.square(gv)
        m_hat = mn / (1.0 - ADAM_B1 ** ADAM_STEP)
        v_hat = vn / (1.0 - ADAM_B2 ** ADAM_STEP)
        d_ref[...] = -ADAM_LR * (m_hat / (jnp.sqrt(v_hat) + ADAM_EPS) + ADAM_WD * w_ref[...])
        nm_ref[...] = mn
        nv_ref[...] = vn
        go_ref[...] = gv

    stacked = pl.BlockSpec((None, tr, C), lambda i: (layer, i, 0))
    in_specs = [stacked, pl.BlockSpec((tr, C), lambda i: (i, 0)), stacked, stacked]
    ops = [w, g, m, v]
    aliases = {}
    if bufs is not None:
        in_specs += [HBM_SPEC] * 4
        ops += list(bufs)
        aliases = {4 + k: k for k in range(4)}
    return pl.pallas_call(body, name=name, grid=(r // tr,), in_specs=in_specs, out_specs=[stacked] * 4,
                          out_shape=[_sds((L, r, C), F32)] * 4, input_output_aliases=aliases,
                          compiler_params=_cparams(1))(*ops)


def _adamw(w, g, m, v, name):
    R, C = w.shape
    tr = _row_tile(R, C, 4, 1024 * 1024, 8)

    def body(w_ref, g_ref, m_ref, v_ref, d_ref, nm_ref, nv_ref):
        gv = g_ref[...]
        mn = ADAM_B1 * m_ref[...] + (1.0 - ADAM_B1) * gv
        vn = ADAM_B2 * v_ref[...] + (1.0 - ADAM_B2) * jnp.square(gv)
        m_hat = mn / (1.0 - ADAM_B1 ** ADAM_STEP)
        v_hat = vn / (1.0 - ADAM_B2 ** ADAM_STEP)
        d_ref[...] = -ADAM_LR * (m_hat / (jnp.sqrt(v_hat) + ADAM_EPS) + ADAM_WD * w_ref[...])
        nm_ref[...] = mn
        nv_ref[...] = vn

    spec = pl.BlockSpec((tr, C), lambda i: (i, 0))
    return pl.pallas_call(body, name=name, grid=(R // tr,), in_specs=[spec] * 4, out_specs=[spec] * 3,
                          out_shape=[_sds((R, C), F32)] * 3, compiler_params=_cparams(1))(w, g, m, v)


def _norm_matmul(x, g, w_g, out_dtype, name, row_sharded=False, deps=(), tm=1024):
    S, D = x.shape
    tm = min(tm, S)
    if row_sharded:
        r, N = w_g.shape[1], w_g.shape[2]
        tn = 512
        w_spec = pl.BlockSpec((N_CHIPS, r, tn), lambda i, j: (0, 0, j))
    else:
        nsh = w_g.shape[2]
        N = N_CHIPS * nsh
        tn = 512 if nsh % 512 == 0 else nsh
        bps = nsh // tn
        w_spec = pl.BlockSpec((None, D, tn), lambda i, j: (j // bps, 0, j % bps))

    def body(x_ref, g_ref, w_ref, *rest):
        y_ref, h_ref = rest[-2:]

        @pl.when(pl.program_id(1) == 0)
        def _():
            xhat, _ = _rms_hat(x_ref[...])
            h_ref[...] = (xhat * g_ref[...]).astype(BF16)

        w = w_ref[...].reshape(D, tn)
        y_ref[...] = jnp.dot(h_ref[...], w, preferred_element_type=F32).astype(y_ref.dtype)

    return pl.pallas_call(
        body, name=name, grid=(S // tm, N // tn),
        in_specs=[pl.BlockSpec((tm, D), lambda i, j: (i, 0)), pl.BlockSpec((1, D), lambda i, j: (0, 0)), w_spec]
        + [HBM_SPEC] * len(deps),
        out_specs=[pl.BlockSpec((tm, tn), lambda i, j: (i, j)), pl.BlockSpec((tm, D), lambda i, j: (i, 0))],
        out_shape=[_sds((S, N), out_dtype), _sds((S, D), BF16)],
        compiler_params=_cparams(2))(x, g, w_g, *deps)


def _matmul_res(a, w_g, res, name, swiglu=False, deps=(), tm=256):
    S, N = res.shape
    r = w_g.shape[1]
    K = N_CHIPS * r

    def body(*refs):
        o_ref = refs[-1]
        if swiglu:
            gate_ref, up_ref, w_ref, res_ref = refs[:4]
            a_blk = _swiglu(gate_ref[...].astype(F32), up_ref[...].astype(F32)).astype(BF16)
        else:
            a_ref, w_ref, res_ref = refs[:3]
            a_blk = a_ref[...]
        o_ref[...] = res_ref[...] + jnp.dot(a_blk, w_ref[...].reshape(K, N), preferred_element_type=F32)

    a_specs, a_ops = [pl.BlockSpec((tm, K), lambda i: (i, 0))], [a]
    if swiglu:
        a_specs.append(pl.BlockSpec((tm, K), lambda i: (i, 1)))
        a_ops.append(a)
    row = pl.BlockSpec((tm, N), lambda i: (i, 0))
    return pl.pallas_call(
        body, name=name, grid=(S // tm,),
        in_specs=a_specs + [pl.BlockSpec((N_CHIPS, r, N), lambda i: (0, 0, 0)), row] + [HBM_SPEC] * len(deps),
        out_specs=row, out_shape=_sds((S, N), F32), compiler_params=_cparams(1))(*a_ops, w_g, res, *deps)


def _matmul_tn(a_ops, a_specs, a_fn, b_op, b_spec, out_spec, out_shape, acc_shape, n_outer, name, deps=(), tt=512):
    S = b_op.shape[-2]
    na = len(a_ops)
    nt = S // tt

    def body(*refs):
        a_refs, b_ref, o_ref, acc_ref = refs[:na], refs[na], refs[-2], refs[-1]
        t = pl.program_id(1)
        part = lax.dot_general(a_fn(*a_refs), b_ref[...].astype(BF16), TN_DIMS, preferred_element_type=F32)

        @pl.when(t == 0)
        def _():
            acc_ref[...] = part

        @pl.when(t > 0)
        def _():
            acc_ref[...] += part

        @pl.when(t == nt - 1)
        def _():
            o_ref[...] = acc_ref[...].reshape(o_ref.shape).astype(BF16)

    return pl.pallas_call(
        body, name=name, grid=(n_outer, nt), in_specs=list(a_specs) + [b_spec] + [HBM_SPEC] * len(deps),
        out_specs=out_spec, out_shape=_sds(out_shape, BF16), scratch_shapes=[pltpu.VMEM(acc_shape, F32)],
        compiler_params=_cparams(2))(*a_ops, b_op, *deps)


def _nt_accumulate(a_ref, w_ref, acc_ref, w2d, nk):
    k = pl.program_id(1)
    part = lax.dot_general(a_ref[...].astype(BF16), w_ref[...].reshape(w2d), NT_DIMS, preferred_element_type=F32)

    @pl.when(k == 0)
    def _():
        acc_ref[...] = part

    @pl.when(k > 0)
    def _():
        acc_ref[...] += part

    return k == nk - 1


def _nt_normbwd(dy, dy_spec, w_g, w_spec, w2d, nk, x, g, dres, name, deps=(), tm=NORMBWD_ROWS):
    S, D = x.shape
    tm = min(tm, S)

    def body(dy_ref, w_ref, x_ref, g_ref, dres_ref, *rest):
        dx_ref, dg_ref, acc_ref = rest[-3:]

        @pl.when((pl.program_id(0) == 0) & (pl.program_id(1) == 0))
        def _():
            dg_ref[...] = jnp.zeros_like(dg_ref)

        last = _nt_accumulate(dy_ref, w_ref, acc_ref, w2d, nk)

        @pl.when(last)
        def _():
            xhat, r = _rms_hat(x_ref[...])
            dx, dgp = _rms_bwd(xhat, r, g_ref[...], acc_ref[...])
            dx_ref[...] = dres_ref[...] + dx
            dg_ref[...] += jnp.sum(dgp, axis=0, keepdims=True)

    row = pl.BlockSpec((tm, D), lambda i, k: (i, 0))
    vec = pl.BlockSpec((1, D), lambda i, k: (0, 0))
    return pl.pallas_call(
        body, name=name, grid=(S // tm, nk),
        in_specs=[dy_spec, w_spec, row, vec, row] + [HBM_SPEC] * len(deps), out_specs=[row, vec],
        out_shape=[_sds((S, D), F32), _sds((1, D), F32)],
        scratch_shapes=[pltpu.VMEM((tm, D), F32)], compiler_params=_cparams(2))(dy, w_g, x, g, dres, *deps)


def _nt_rows(dy, w_g, shards_per_block, out_dtype, name, deps=(), tm=1024):
    S, N = dy.shape
    tm = min(tm, S)
    r = w_g.shape[1]
    tn = shards_per_block * r

    def body(dy_ref, w_ref, *rest):
        o_ref = rest[-1]
        o_ref[...] = lax.dot_general(dy_ref[...].astype(BF16), w_ref[...].reshape(tn, N), NT_DIMS,
                                     preferred_element_type=F32).astype(o_ref.dtype)

    return pl.pallas_call(
        body, name=name, grid=(S // tm, N_CHIPS // shards_per_block),
        in_specs=[pl.BlockSpec((tm, N), lambda i, j: (i, 0)),
                  pl.BlockSpec((shards_per_block, r, N), lambda i, j: (j, 0, 0))] + [HBM_SPEC] * len(deps),
        out_specs=pl.BlockSpec((tm, tn), lambda i, j: (i, j)),
        out_shape=_sds((S, N_CHIPS * r), out_dtype), compiler_params=_cparams(2))(dy, w_g, *deps)


def _nt_swiglu_bwd(dy, w_g, gu, name, deps=(), tm=512):
    S, N = dy.shape
    r = w_g.shape[1]
    tn = 2 * r
    F = N_CHIPS * r

    def body(dy_ref, w_ref, gate_ref, up_ref, *rest):
        o_ref = rest[-1]
        dact = lax.dot_general(dy_ref[...].astype(BF16), w_ref[...].reshape(tn, N), NT_DIMS,
                               preferred_element_type=F32)
        gate, up = gate_ref[...].astype(F32), up_ref[...].astype(F32)
        sg = jax.nn.sigmoid(gate)
        silu = gate * sg
        o_ref[0] = ((dact * up) * (sg + silu * (1.0 - sg))).astype(BF16)
        o_ref[1] = (dact * silu).astype(BF16)

    return pl.pallas_call(
        body, name=name, grid=(2, S // tm),
        in_specs=[pl.BlockSpec((tm, N), lambda j, i: (i, 0)),
                  pl.BlockSpec((2, r, N), lambda j, i: (j, 0, 0)),
                  pl.BlockSpec((tm, tn), lambda j, i: (i, j)),
                  pl.BlockSpec((tm, tn), lambda j, i: (i, 2 + j))] + [HBM_SPEC] * len(deps),
        out_specs=pl.BlockSpec((2, tm, tn), lambda j, i: (0, i, j)),
        out_shape=_sds((2, S, F), BF16), compiler_params=_cparams(2))(dy, w_g, gu, gu, *deps)


def _chunk_causal_mask(transposed):
    i = lax.broadcasted_iota(jnp.int32, (A_CHUNK, A_CHUNK), 0) // CHUNK
    j = lax.broadcasted_iota(jnp.int32, (A_CHUNK, A_CHUNK), 1) // CHUNK
    return ((i <= j) if transposed else (i >= j)).astype(F32)


def _sgu_fwd(zpre, g_sgu, ws, bs_t, name, deps=()):
    S, F2 = zpre.shape
    F = F2 // 2
    gd = F // A_GROUPS

    def body(zu_ref, zv_ref, g_ref, ws_ref, b_ref, *rest):
        o_ref = rest[-1]
        vhat, _ = _rms_hat(_gelu(zv_ref[...].astype(F32)))
        vn = (vhat * g_ref[...]).astype(BF16)
        u = _gelu(zu_ref[...].astype(F32))
        mask = _chunk_causal_mask(False)
        for gi in range(A_GROUPS):
            sl = slice(gi * gd, (gi + 1) * gd)
            wm = (ws_ref[gi] * mask).astype(BF16)
            vs = jnp.dot(wm, vn[:, sl], preferred_element_type=F32) + b_ref[:, gi:gi + 1]
            o_ref[:, sl] = (u[:, sl] * vs).astype(BF16)

    return pl.pallas_call(
        body, name=name, grid=(S // A_CHUNK,),
        in_specs=[pl.BlockSpec((A_CHUNK, F), lambda i: (i, 0)),
                  pl.BlockSpec((A_CHUNK, F), lambda i: (i, 1)),
                  pl.BlockSpec((1, F), lambda i: (0, 0)),
                  pl.BlockSpec((A_GROUPS, A_CHUNK, A_CHUNK), lambda i: (0, 0, 0)),
                  pl.BlockSpec((A_CHUNK, A_GROUPS), lambda i: (0, 0))] + [HBM_SPEC] * len(deps),
        out_specs=pl.BlockSpec((A_CHUNK, F), lambda i: (i, 0)),
        out_shape=_sds((S, F), BF16), compiler_params=_cparams(1))(zpre, zpre, g_sgu, ws, bs_t, *deps)


def _sgu_bwd(zpre, duv, g_sgu, ws, ws_t, bs_t, name):
    S, F2 = zpre.shape
    F = F2 // 2
    gd = F // A_GROUPS

    def body(zu_ref, zv_ref, duv_ref, g_ref, ws_ref, wst_ref, b_ref, dz_ref, dg_ref, dws_ref, dbs_ref, dvn_ref):
        @pl.when(pl.program_id(0) == 0)
        def _():
            dg_ref[...] = jnp.zeros_like(dg_ref)
            dws_ref[...] = jnp.zeros_like(dws_ref)
            dbs_ref[...] = jnp.zeros_like(dbs_ref)

        gv = g_ref[...]
        u, u_grad = _gelu_and_grad(zu_ref[...].astype(F32))
        v, v_grad = _gelu_and_grad(zv_ref[...].astype(F32))
        vhat, r = _rms_hat(v)
        vn = (vhat * gv).astype(BF16)
        duv_v = duv_ref[...].astype(F32)
        dvs = duv_v * u
        dvs_b = dvs.astype(BF16)
        mask = _chunk_causal_mask(False)
        mask_t = _chunk_causal_mask(True)
        for gi in range(A_GROUPS):
            sl = slice(gi * gd, (gi + 1) * gd)
            wm = (ws_ref[gi] * mask).astype(BF16)
            vs = jnp.dot(wm, vn[:, sl], preferred_element_type=F32) + b_ref[:, gi:gi + 1]
            dz_ref[:, sl] = ((duv_v[:, sl] * vs) * u_grad[:, sl]).astype(BF16)
            dws_ref[gi] += lax.dot_general(dvs_b[:, sl], vn[:, sl], NT_DIMS, preferred_element_type=F32) * mask
            dbs_ref[gi] += jnp.broadcast_to(jnp.sum(dvs[:, sl], axis=1, keepdims=True), (A_CHUNK, A_CHUNK))
            wm_t = (wst_ref[gi] * mask_t).astype(BF16)
            dvn_ref[:, sl] = jnp.dot(wm_t, dvs_b[:, sl], preferred_element_type=F32)
        dv, dg_part = _rms_bwd(vhat, r, gv, dvn_ref[...])
        dg_ref[...] += jnp.sum(dg_part, axis=0, keepdims=True)
        dz_ref[:, F:] = (dv * v_grad).astype(BF16)

    blk = pl.BlockSpec((A_CHUNK, F), lambda i: (i, 0))
    const3 = pl.BlockSpec((A_GROUPS, A_CHUNK, A_CHUNK), lambda i: (0, 0, 0))
    return pl.pallas_call(
        body, name=name, grid=(S // A_CHUNK,),
        in_specs=[blk, pl.BlockSpec((A_CHUNK, F), lambda i: (i, 1)), blk,
                  pl.BlockSpec((1, F), lambda i: (0, 0)), const3, const3,
                  pl.BlockSpec((A_CHUNK, A_GROUPS), lambda i: (0, 0))],
        out_specs=[pl.BlockSpec((A_CHUNK, F2), lambda i: (i, 0)), pl.BlockSpec((1, F), lambda i: (0, 0)),
                   const3, const3],
        out_shape=[_sds((S, F2), BF16), _sds((1, F), F32), _sds((A_GROUPS, A_CHUNK, A_CHUNK), F32),
                   _sds((A_GROUPS, A_CHUNK, A_CHUNK), F32)],
        scratch_shapes=[pltpu.VMEM((A_CHUNK, F), F32)],
        compiler_params=_cparams(1))(zpre, zpre, duv, g_sgu, ws, ws_t, bs_t)


def _toeplitz_one_hot():
    row = lax.broadcasted_iota(jnp.int32, (TABLE_PAD, DIAGS), 0)
    j = lax.broadcasted_iota(jnp.int32, (TABLE_PAD, DIAGS), 1)
    idx = jnp.clip(KV_PAD + Q_BLOCK - j, -MAX_REL, MAX_REL) + MAX_REL
    return (row == idx).astype(F32)


def _rel_bias_fwd(table, name):
    H = table.shape[0]

    def body(t_ref, o_ref):
        diag = jnp.dot(t_ref[...], _toeplitz_one_hot(), precision=HIGHEST, preferred_element_type=F32)
        k_chunk = lax.broadcasted_iota(jnp.int32, (1, BAND), 1) // CHUNK

        def step(r, carry):
            q_chunk = r // CHUNK
            seen = (k_chunk >= q_chunk) & (k_chunk <= q_chunk + N_LEFT)
            o_ref[r] = pltpu.roll(diag, DIAGS - Q_BLOCK + r, 1)[:, :BAND] + jnp.where(seen, 0.0, NEG_INF)
            return carry

        lax.fori_loop(0, Q_BLOCK, step, 0)

    return pl.pallas_call(body, name=name, out_shape=_sds((Q_BLOCK, H, BAND), F32),
                          compiler_params=pltpu.CompilerParams(vmem_limit_bytes=VMEM_LIMIT_BYTES))(table)


def _rel_bias_bwd(dbias, name):
    H = dbias.shape[1]

    def body(d_ref, o_ref):
        def step(r, acc):
            row = jnp.concatenate([d_ref[r], jnp.zeros((H, DIAGS - BAND), F32)], axis=1)
            return acc + pltpu.roll(row, Q_BLOCK - r, 1)

        diag = lax.fori_loop(0, Q_BLOCK, step, jnp.zeros((H, DIAGS), F32))
        o_ref[...] = lax.dot_general(diag, _toeplitz_one_hot(), NT_DIMS, precision=HIGHEST,
                                     preferred_element_type=F32)

    return pl.pallas_call(body, name=name, out_shape=_sds((H, TABLE_PAD), F32),
                          compiler_params=pltpu.CompilerParams(vmem_limit_bytes=VMEM_LIMIT_BYTES))(dbias)


def _head_rows(t):
    lane = lax.broadcasted_iota(jnp.int32, t.shape, 1)
    zero = jnp.zeros_like(t)
    return jnp.concatenate([jnp.where(lane < HEAD_DIM, t, zero), jnp.where(lane >= HEAD_DIM, t, zero)], axis=0)


def _head_lanes(t2):
    lane = lax.broadcasted_iota(jnp.int32, (Q_BLOCK, t2.shape[1]), 1)
    return jnp.where(lane < HEAD_DIM, t2[:Q_BLOCK], t2[Q_BLOCK:])


def _attn_probs(q2, kb, bias2, block):
    kj = lax.broadcasted_iota(jnp.int32, (1, BAND), 1)
    before_start = jnp.where(block * Q_BLOCK + kj - KV_PAD >= 0, 0.0, NEG_INF)
    s = lax.dot_general(q2 * ATTN_SCALE, kb, NT_DIMS, preferred_element_type=F32) + bias2 + before_start
    e = jnp.exp(s - jnp.max(s, axis=-1, keepdims=True))
    return e / jnp.sum(e, axis=-1, keepdims=True)


def _attn_specs(S):
    lanes = HEADS_PER_BLOCK * HEAD_DIM
    rows = S + KV_PAD
    q_spec = pl.BlockSpec((BLOCKS_PER_STEP * Q_BLOCK, lanes), lambda h, i: (i, h))
    k_spec = pl.BlockSpec((rows, lanes), lambda h, i: (0, h))
    v_spec = pl.BlockSpec((rows, lanes), lambda h, i: (0, N_HEADS // HEADS_PER_BLOCK + h))
    b_spec = pl.BlockSpec((HEADS_PER_BLOCK, Q_BLOCK, BAND), lambda h, i: (h, 0, 0))
    return q_spec, k_spec, v_spec, b_spec


def _attn_fwd(q, kvp, bias, name, deps=()):
    S, HD = q.shape
    q_spec, k_spec, v_spec, b_spec = _attn_specs(S)

    def body(q_ref, k_ref, v_ref, b_ref, *rest):
        o_ref = rest[-1]
        for b in range(BLOCKS_PER_STEP):
            block = pl.program_id(1) * BLOCKS_PER_STEP + b
            rows = slice(b * Q_BLOCK, (b + 1) * Q_BLOCK)
            band = pl.ds(pl.multiple_of(block * Q_BLOCK, Q_BLOCK), BAND)
            p = _attn_probs(_head_rows(q_ref[rows, :]), k_ref[band, :], b_ref[...].reshape(2 * Q_BLOCK, BAND), block)
            o2 = jnp.dot(p.astype(BF16), v_ref[band, :], preferred_element_type=F32)
            o_ref[rows, :] = _head_lanes(o2).astype(BF16)

    return pl.pallas_call(
        body, name=name, grid=(N_HEADS // HEADS_PER_BLOCK, S // (BLOCKS_PER_STEP * Q_BLOCK)),
        in_specs=[q_spec, k_spec, v_spec, b_spec] + [HBM_SPEC] * len(deps), out_specs=q_spec,
        out_shape=_sds((S, HD), BF16), compiler_params=_cparams(2))(q, kvp, kvp, bias, *deps)


def _attn_bwd(q, kvp, bias, do, dkv_prev, name):
    S, HD = q.shape
    lanes = HEADS_PER_BLOCK * HEAD_DIM
    q_spec, k_spec, v_spec, b_spec = _attn_specs(S)
    dkv_spec = pl.BlockSpec((2, S + KV_PAD, lanes), lambda h, i: (0, 0, h))
    prev = [] if dkv_prev is None else [dkv_prev]

    def body(q_ref, k_ref, v_ref, b_ref, do_ref, *rest):
        dq_ref, dkv_ref, db_ref = rest[-3:]

        @pl.when(pl.program_id(1) == 0)
        def _():
            dkv_ref[...] = rest[0][...] if prev else jnp.zeros_like(dkv_ref)
            db_ref[...] = jnp.zeros_like(db_ref)

        db = jnp.zeros((2 * Q_BLOCK, BAND), F32)
        for b in range(BLOCKS_PER_STEP):
            block = pl.program_id(1) * BLOCKS_PER_STEP + b
            rows = slice(b * Q_BLOCK, (b + 1) * Q_BLOCK)
            band = pl.ds(pl.multiple_of(block * Q_BLOCK, Q_BLOCK), BAND)
            kb, vb = k_ref[band, :], v_ref[band, :]
            q2, do2 = _head_rows(q_ref[rows, :]), _head_rows(do_ref[rows, :])
            p = _attn_probs(q2, kb, b_ref[...].reshape(2 * Q_BLOCK, BAND), block)
            dp = lax.dot_general(do2, vb, NT_DIMS, preferred_element_type=F32)
            ds = p * (dp - jnp.sum(dp * p, axis=-1, keepdims=True))
            db = db + ds
            ds_b = (ds * ATTN_SCALE).astype(BF16)
            dq_ref[rows, :] = _head_lanes(jnp.dot(ds_b, kb, preferred_element_type=F32)).astype(BF16)
            dkv_ref[0, band, :] += lax.dot_general(ds_b, q2, TN_DIMS, preferred_element_type=F32)
            dkv_ref[1, band, :] += lax.dot_general(p.astype(BF16), do2, TN_DIMS, preferred_element_type=F32)
        db_ref[...] += db.reshape(HEADS_PER_BLOCK, Q_BLOCK, BAND)

    return pl.pallas_call(
        body, name=name, grid=(N_HEADS // HEADS_PER_BLOCK, S // (BLOCKS_PER_STEP * Q_BLOCK)),
        in_specs=[q_spec, k_spec, v_spec, b_spec, q_spec] + [dkv_spec] * len(prev),
        out_specs=[q_spec, dkv_spec, b_spec],
        out_shape=[_sds((S, HD), BF16), _sds((2, S + KV_PAD, HD), F32), _sds((N_HEADS, Q_BLOCK, BAND), F32)],
        compiler_params=_cparams(2))(q, kvp, kvp, bias, do, *prev)


def _loss_head(x, g, target, name, tm=512):
    S, D = x.shape

    def body(x_ref, g_ref, t_ref, loss_ref, dx_ref, dg_ref):
        @pl.when(pl.program_id(0) == 0)
        def _():
            loss_ref[...] = jnp.zeros_like(loss_ref)
            dg_ref[...] = jnp.zeros_like(dg_ref)

        xhat, r = _rms_hat(x_ref[...])
        gv = g_ref[...]
        err = xhat * gv - t_ref[...]
        loss_ref[...] += 0.5 * jnp.sum(jnp.mean(err * err, axis=-1, keepdims=True))
        dx, dgp = _rms_bwd(xhat, r, gv, err * (1.0 / D))
        dx_ref[...] = dx
        dg_ref[...] += jnp.sum(dgp, axis=0, keepdims=True)

    row = pl.BlockSpec((tm, D), lambda i: (i, 0))
    vec = pl.BlockSpec((1, D), lambda i: (0, 0))
    return pl.pallas_call(
        body, name=name, grid=(S // tm,), in_specs=[row, vec, row],
        out_specs=[pl.BlockSpec((8, 128), lambda i: (0, 0)), row, vec],
        out_shape=[_sds((8, 128), F32), _sds((S, D), F32), _sds((1, D), F32)],
        compiler_params=_cparams(1))(x, g, target)


def _place():
    x, y, c = lax.axis_index("x"), lax.axis_index("y"), lax.axis_index("c")
    chips = [(1 - x, y), (x, 1 - y), (1 - x, 1 - y)]
    return x, y, c, chips


def _half_rows(c, r):
    return pl.ds(pl.multiple_of(c * (r // 2), 8), r // 2)


HBM_SPEC = pl.BlockSpec(memory_space=pl.ANY)


STRICT_HBM_SPEC = pl.BlockSpec(memory_space=pltpu.HBM)
SEM_SPEC = pl.BlockSpec(memory_space=pltpu.SEMAPHORE)
EFFECT = pltpu.SideEffectType.DATAFLOW_SIDE_EFFECTING


def _peers(x, y, c):
    out = []
    for k in range(1, N_DEV):
        px, py, pc = (x + ((k >> 2) & 1)) % 2, (y + ((k >> 1) & 1)) % 2, (c + (k & 1)) % 2
        out.append(((px, py, pc), 2 * px + py, pc, 4 * px + 2 * py + pc))
    return out


def _token_spec():
    return pl.BlockSpec(memory_space=pltpu.VMEM)


def _hbm(a):
    return pltpu.with_memory_space_constraint(a, pltpu.HBM)


def _slab_half(ref, chip, core):
    return ref.at[2 * chip[0] + chip[1], _half_rows(core, ref.shape[1]), :]


def _allgather_start(slabs, name):
    n = len(slabs)

    def body(*refs):
        src, send, recv, token = refs[:n], refs[n], refs[n + 1], refs[-1]
        x, y, c, chips = _place()
        for a in range(n):
            own = _slab_half(src[a], (x, y), c)
            for j, chip in enumerate(chips):
                pltpu.make_async_remote_copy(src_ref=own, dst_ref=own, send_sem=send.at[3 * a + j], recv_sem=recv.at[3 * a + j],
                                             device_id=(*chip, c), device_id_type=MESH).start()
        token[...] = jnp.zeros_like(token)

    sems = pltpu.SemaphoreType.DMA((3 * n,))
    send, recv, *flying, token = pl.pallas_call(
        body, name=name, in_specs=[STRICT_HBM_SPEC] * n,
        out_shape=(sems, sems, *[pltpu.HBM(s.shape, s.dtype) for s in slabs], _sds((8, 128), F32)),
        out_specs=(SEM_SPEC, SEM_SPEC, *[STRICT_HBM_SPEC] * n, _token_spec()),
        input_output_aliases={a: a + 2 for a in range(n)},
        compiler_params=pltpu.CompilerParams(has_side_effects=EFFECT))(*[_hbm(s) for s in slabs])
    return send, recv, flying, token


def _allgather_relay(flying, send, recv, first, after, name):
    n = len(flying)

    def body(*refs):
        src, send_ref, recv_ref = refs[:n], refs[n], refs[n + 1]
        send2, recv2, token = refs[n + 3], refs[n + 4], refs[-1]
        token[...] = jnp.zeros_like(token)
        x, y, c, chips = _place()
        for a in range(n):
            for j, chip in enumerate(chips):
                cp = pltpu.make_async_remote_copy(
                    src_ref=_slab_half(src[a], (x, y), c), dst_ref=_slab_half(src[a], chip, c),
                    send_sem=send_ref.at[3 * (first + a) + j], recv_sem=recv_ref.at[3 * (first + a) + j],
                    device_id=(*chip, c), device_id_type=MESH)
                cp.wait_send()
                cp.wait_recv()
        for a in range(n):
            for j, chip in enumerate(chips):
                landed = _slab_half(src[a], chip, c)
                pltpu.make_async_remote_copy(src_ref=landed, dst_ref=landed, send_sem=send2.at[3 * a + j],
                                             recv_sem=recv2.at[3 * a + j], device_id=(x, y, 1 - c),
                                             device_id_type=MESH).start()

    sems = pltpu.SemaphoreType.DMA((3 * n,))
    send2, recv2, *relayed, token = pl.pallas_call(
        body, name=name, in_specs=[STRICT_HBM_SPEC] * n + [SEM_SPEC, SEM_SPEC, HBM_SPEC],
        out_shape=(sems, sems, *[pltpu.HBM(s.shape, s.dtype) for s in flying], _sds((8, 128), F32)),
        out_specs=(SEM_SPEC, SEM_SPEC, *[STRICT_HBM_SPEC] * n, _token_spec()),
        input_output_aliases={a: a + 2 for a in range(n)},
        compiler_params=pltpu.CompilerParams(has_side_effects=EFFECT))(*flying, send, recv, after)
    return send2, recv2, relayed, token


def _allgather_wait(relayed, send2, recv2, after, name):
    n = len(relayed)

    def body(*refs):
        src, send_ref, recv_ref = refs[:n], refs[n], refs[n + 1]
        x, y, c, chips = _place()
        for a in range(n):
            for j, chip in enumerate(chips):
                cp = pltpu.make_async_remote_copy(
                    src_ref=_slab_half(src[a], chip, c), dst_ref=_slab_half(src[a], chip, 1 - c),
                    send_sem=send_ref.at[3 * a + j], recv_sem=recv_ref.at[3 * a + j],
                    device_id=(x, y, 1 - c), device_id_type=MESH)
                cp.wait_send()
                cp.wait_recv()

    return pl.pallas_call(
        body, name=name, in_specs=[STRICT_HBM_SPEC] * n + [SEM_SPEC, SEM_SPEC, HBM_SPEC],
        out_shape=tuple(pltpu.HBM(s.shape, s.dtype) for s in relayed), out_specs=tuple([STRICT_HBM_SPEC] * n),
        input_output_aliases={a: a for a in range(n)},
        compiler_params=pltpu.CompilerParams(has_side_effects=EFFECT))(*relayed, send2, recv2, after)


def _allgather_small(small, name):
    def body(sm, osm, send, recv, local):
        x, y, c, chips = _place()
        own = pltpu.make_async_copy(sm, osm.at[2 * x + y], local)
        own.start()
        cps = [pltpu.make_async_remote_copy(src_ref=sm, dst_ref=osm.at[2 * x + y], send_sem=send.at[j],
                                            recv_sem=recv.at[j], device_id=(*chip, c), device_id_type=MESH)
               for j, chip in enumerate(chips)]
        for cp in cps:
            cp.start()
        for j, chip in enumerate(chips):
            got = osm.at[2 * chip[0] + chip[1]]
            pltpu.make_async_remote_copy(src_ref=got, dst_ref=got, send_sem=send.at[j], recv_sem=recv.at[j],
                                         device_id=(x, y, c), device_id_type=MESH).wait_recv()
        for cp in cps:
            cp.wait_send()
        own.wait()

    return pl.pallas_call(
        body, name=name, in_specs=[pl.BlockSpec(memory_space=pltpu.VMEM)], out_specs=HBM_SPEC,
        out_shape=_sds((N_CHIPS, *small.shape), small.dtype),
        scratch_shapes=[pltpu.SemaphoreType.DMA((3,)), pltpu.SemaphoreType.DMA((3,)), pltpu.SemaphoreType.DMA])(small)


def _reduce_start(grads, name):
    n = len(grads)

    def body(*refs):
        src, land, send, recv, token = refs[:n], refs[n:2 * n], refs[2 * n], refs[2 * n + 1], refs[-1]
        x, y, c, _ = _place()
        me = 4 * x + 2 * y + c
        for a in range(n):
            for k, (peer, p_chip, p_core, _) in enumerate(_peers(x, y, c)):
                pltpu.make_async_remote_copy(
                    src_ref=src[a].at[p_chip, _half_rows(p_core, src[a].shape[1]), :], dst_ref=land[a].at[me],
                    send_sem=send.at[(N_DEV - 1) * a + k], recv_sem=recv.at[(N_DEV - 1) * a + k],
                    device_id=peer, device_id_type=MESH).start()
        token[...] = jnp.zeros_like(token)

    lands = [lax.empty((N_DEV, g.shape[1] // 2, g.shape[2]), BF16) for g in grads]
    sems = pltpu.SemaphoreType.DMA(((N_DEV - 1) * n,))
    shapes = [pltpu.HBM(a.shape, a.dtype) for a in grads + lands]
    send, recv, *flying, token = pl.pallas_call(
        body, name=name, in_specs=[STRICT_HBM_SPEC] * (2 * n),
        out_shape=(sems, sems, *shapes, _sds((8, 128), F32)),
        out_specs=(SEM_SPEC, SEM_SPEC, *[STRICT_HBM_SPEC] * (2 * n), _token_spec()),
        input_output_aliases={a: a + 2 for a in range(2 * n)},
        compiler_params=pltpu.CompilerParams(has_side_effects=EFFECT))(*[_hbm(a) for a in grads + lands])
    return send, recv, flying[:n], flying[n:], token


def _reduce_wait(started, after, name):
    sizes = [len(grads) for _, _, grads, _ in started]
    n_arr = 2 * sum(sizes)

    def body(*refs):
        x, y, c, _ = _place()
        at = 0
        for s, n in enumerate(sizes):
            src, land = refs[at:at + n], refs[at + n:at + 2 * n]
            send_ref, recv_ref = refs[n_arr + 2 * s], refs[n_arr + 2 * s + 1]
            at += 2 * n
            for a in range(n):
                for k, (peer, p_chip, p_core, p_dev) in enumerate(_peers(x, y, c)):
                    cp = pltpu.make_async_remote_copy(
                        src_ref=src[a].at[p_chip, _half_rows(p_core, src[a].shape[1]), :], dst_ref=land[a].at[p_dev],
                        send_sem=send_ref.at[(N_DEV - 1) * a + k], recv_sem=recv_ref.at[(N_DEV - 1) * a + k],
                        device_id=peer, device_id_type=MESH)
                    cp.wait_send()
                    cp.wait_recv()

    arrays, sems = [], []
    for send, recv, grads, lands in started:
        arrays += list(grads) + list(lands)
        sems += [send, recv]
    out = pl.pallas_call(
        body, name=name, in_specs=[STRICT_HBM_SPEC] * n_arr + [SEM_SPEC] * len(sems) + [HBM_SPEC],
        out_shape=tuple(pltpu.HBM(a.shape, a.dtype) for a in arrays), out_specs=tuple([STRICT_HBM_SPEC] * n_arr),
        input_output_aliases={a: a for a in range(n_arr)},
        compiler_params=pltpu.CompilerParams(has_side_effects=EFFECT))(*arrays, *sems, after)
    result, at = [], 0
    for n in sizes:
        result.append((out[at:at + n], out[at + n:at + 2 * n]))
        at += 2 * n
    return result


def _reduce_sum(grad, land, place, name):
    _, r2, C = land.shape
    tr = _row_tile(r2, C, 4, 1024 * 1024, 16)
    nb = r2 // tr

    def body(place_ref, own_ref, *rest):
        del place_ref
        acc = own_ref[...].astype(F32)
        for ref in rest[:N_DEV - 1]:
            acc = acc + ref[...].astype(F32)
        rest[-1][...] = acc

    def from_dev(k):
        return pl.BlockSpec((None, tr, C), lambda i, place_ref: ((place_ref[2] + k) % N_DEV, i, 0))

    grid_spec = pltpu.PrefetchScalarGridSpec(
        num_scalar_prefetch=1, grid=(nb,),
        in_specs=[pl.BlockSpec((None, tr, C), lambda i, place_ref: (place_ref[0], place_ref[1] * nb + i, 0))]
        + [from_dev(k) for k in range(1, N_DEV)],
        out_specs=pl.BlockSpec((tr, C), lambda i, place_ref: (place_ref[1] * nb + i, 0)))
    return pl.pallas_call(body, name=name, grid_spec=grid_spec, out_shape=_sds((2 * r2, C), F32),
                          compiler_params=_cparams(1))(place, grad, *[land] * (N_DEV - 1))


def _sibling_join(halves, name):
    n = len(halves)

    def body(*refs):
        out, send, recv = refs[n:2 * n], refs[2 * n], refs[2 * n + 1]
        x, y, c, _ = _place()
        cps = []
        for w in range(n):
            mine = out[w].at[_half_rows(c, out[w].shape[0]), :]
            cps.append(pltpu.make_async_remote_copy(src_ref=mine, dst_ref=mine, send_sem=send.at[w],
                                                    recv_sem=recv.at[w], device_id=(x, y, 1 - c), device_id_type=MESH))
        for cp in cps:
            cp.start()
        for w in range(n):
            theirs = out[w].at[_half_rows(1 - c, out[w].shape[0]), :]
            pltpu.make_async_remote_copy(src_ref=theirs, dst_ref=theirs, send_sem=send.at[w], recv_sem=recv.at[w],
                                         device_id=(x, y, c), device_id_type=MESH).wait_recv()
        for cp in cps:
            cp.wait_send()

    return pl.pallas_call(
        body, name=name, in_specs=[HBM_SPEC] * n, out_specs=[HBM_SPEC] * n,
        out_shape=[_sds(a.shape, F32) for a in halves], input_output_aliases={w: w for w in range(n)},
        scratch_shapes=[pltpu.SemaphoreType.DMA((n,)), pltpu.SemaphoreType.DMA((n,))])(*halves)


def _gather_small(packed, name):
    def body(p_ref, out, send, recv, local):
        x, y, c, _ = _place()
        me = 4 * x + 2 * y + c
        own = pltpu.make_async_copy(p_ref, out.at[me], local)
        own.start()
        cps = []
        for k in range(1, N_DEV):
            fx, fy, fc = (k >> 2) & 1, (k >> 1) & 1, k & 1
            peer = ((x + fx) % 2, (y + fy) % 2, (c + fc) % 2)
            cps.append(pltpu.make_async_remote_copy(src_ref=p_ref, dst_ref=out.at[me], send_sem=send.at[k - 1],
                                                    recv_sem=recv.at[k - 1], device_id=peer, device_id_type=MESH))
        for cp in cps:
            cp.start()
        for k in range(1, N_DEV):
            fx, fy, fc = (k >> 2) & 1, (k >> 1) & 1, k & 1
            src = out.at[4 * ((x + fx) % 2) + 2 * ((y + fy) % 2) + (c + fc) % 2]
            pltpu.make_async_remote_copy(src_ref=src, dst_ref=src, send_sem=send.at[k - 1], recv_sem=recv.at[k - 1],
                                         device_id=(x, y, c), device_id_type=MESH).wait_recv()
        for cp in cps:
            cp.wait_send()
        own.wait()

    return pl.pallas_call(
        body, name=name, in_specs=[pl.BlockSpec(memory_space=pltpu.VMEM)], out_specs=HBM_SPEC,
        out_shape=_sds((N_DEV, *packed.shape), F32),
        scratch_shapes=[pltpu.SemaphoreType.DMA((N_DEV - 1,)), pltpu.SemaphoreType.DMA((N_DEV - 1,)),
                        pltpu.SemaphoreType.DMA])(packed)


def _sum_devices(gathered, name):
    _, R, C = gathered.shape

    def body(g_ref, o_ref):
        acc = g_ref[0]
        for d in range(1, N_DEV):
            acc = acc + g_ref[d]
        o_ref[...] = acc

    tr = 8
    return pl.pallas_call(
        body, name=name, grid=(R // tr,), in_specs=[pl.BlockSpec((N_DEV, tr, C), lambda i: (0, i, 0))],
        out_specs=pl.BlockSpec((tr, C), lambda i: (i, 0)), out_shape=_sds((R, C), F32),
        compiler_params=_cparams(1))(gathered)


def _gather_start(packed, name):
    def body(src, land, send, recv, *rest):
        x, y, c, _ = _place()
        for k, (peer, _, _, _) in enumerate(_peers(x, y, c)):
            pltpu.make_async_remote_copy(src_ref=src, dst_ref=land.at[4 * x + 2 * y + c], send_sem=send.at[k],
                                         recv_sem=recv.at[k], device_id=peer, device_id_type=MESH).start()
        rest[-1][...] = jnp.zeros_like(rest[-1])

    land = lax.empty((N_DEV, *packed.shape), F32)
    sems = pltpu.SemaphoreType.DMA((N_DEV - 1,))
    return pl.pallas_call(
        body, name=name, in_specs=[STRICT_HBM_SPEC] * 2,
        out_shape=(sems, sems, pltpu.HBM(packed.shape, F32), pltpu.HBM(land.shape, F32), _sds((8, 128), F32)),
        out_specs=(SEM_SPEC, SEM_SPEC, STRICT_HBM_SPEC, STRICT_HBM_SPEC, _token_spec()),
        input_output_aliases={0: 2, 1: 3},
        compiler_params=pltpu.CompilerParams(has_side_effects=EFFECT))(_hbm(packed), _hbm(land))


def _gather_wait(started, after, name):
    n = len(started)

    def body(*refs):
        x, y, c, _ = _place()
        for s in range(n):
            src, land, send, recv = refs[2 * s], refs[2 * s + 1], refs[2 * n + 2 * s], refs[2 * n + 2 * s + 1]
            for k, (peer, _, _, p_dev) in enumerate(_peers(x, y, c)):
                cp = pltpu.make_async_remote_copy(src_ref=src, dst_ref=land.at[p_dev], send_sem=send.at[k],
                                                  recv_sem=recv.at[k], device_id=peer,
                                                  device_id_type=MESH)
                cp.wait_send()
                cp.wait_recv()

    arrays = [a for _, _, packed, land in started for a in (packed, land)]
    sems = [s for send, recv, _, _ in started for s in (send, recv)]
    out = pl.pallas_call(
        body, name=name, in_specs=[STRICT_HBM_SPEC] * (2 * n) + [SEM_SPEC] * (2 * n) + [HBM_SPEC],
        out_shape=tuple(pltpu.HBM(a.shape, a.dtype) for a in arrays), out_specs=tuple([STRICT_HBM_SPEC] * (2 * n)),
        input_output_aliases={a: a for a in range(2 * n)},
        compiler_params=pltpu.CompilerParams(has_side_effects=EFFECT))(*arrays, *sems, after)
    return [(out[2 * s], out[2 * s + 1]) for s in range(n)]


def _sum_gathered(packed, land, device, name):
    R, C = packed.shape
    tr = 8

    def body(dev_ref, own_ref, *rest):
        me = dev_ref[0]
        acc = None
        for d in range(N_DEV):
            term = jnp.where(me == d, own_ref[...], rest[d][...])
            acc = term if acc is None else acc + term
        rest[-1][...] = acc

    def slab(d):
        return pl.BlockSpec((None, tr, C), lambda i, dev_ref: (jnp.where(dev_ref[0] == d, (d + 1) % N_DEV, d), i, 0))

    grid_spec = pltpu.PrefetchScalarGridSpec(
        num_scalar_prefetch=1, grid=(R // tr,),
        in_specs=[pl.BlockSpec((tr, C), lambda i, dev_ref: (i, 0))] + [slab(d) for d in range(N_DEV)],
        out_specs=pl.BlockSpec((tr, C), lambda i, dev_ref: (i, 0)))
    return pl.pallas_call(body, name=name, grid_spec=grid_spec, out_shape=_sds((R, C), F32),
                          compiler_params=_cparams(1))(device, packed, *[land] * N_DEV)


def _pack_small(arrays):
    rows = []
    for a in arrays:
        flat = a.reshape(-1)
        pad = (-flat.shape[0]) % SMALL_COLS
        rows.append(jnp.pad(flat, (0, pad)).reshape(-1, SMALL_COLS))
    packed = jnp.concatenate(rows, axis=0)
    return jnp.pad(packed, ((0, (-packed.shape[0]) % 8), (0, 0)))


def _unpack_small(packed, shapes):
    out, row = [], 0
    for shape in shapes:
        size = math.prod(shape)
        n_rows = -(-size // SMALL_COLS)
        out.append(packed[row:row + n_rows].reshape(-1)[:size].reshape(shape))
        row += n_rows
    return out


def kernel(x, a_norm, a_w_in, a_sgu_norm, a_w_spatial, a_b_spatial, a_w_out, kv_norm, w_kv, b_norm, b_w_q, b_rel_bias, b_w_o, ffn_norm, ffn_w_gate_up, ffn_w_down, final_norm, loss_target, m_a_norm, m_a_w_in, m_a_sgu_norm, m_a_w_spatial, m_a_b_spatial, m_a_w_out, m_kv_norm, m_w_kv, m_b_norm, m_b_w_q, m_b_rel_bias, m_b_w_o, m_ffn_norm, m_ffn_w_gate_up, m_ffn_w_down, m_final_norm, v_a_norm, v_a_w_in, v_a_sgu_norm, v_a_w_spatial, v_a_b_spatial, v_a_w_out, v_kv_norm, v_w_kv, v_b_norm, v_b_w_q, v_b_rel_bias, v_b_w_o, v_ffn_norm, v_ffn_w_gate_up, v_ffn_w_down, v_final_norm):
    S, D = x.shape[1], x.shape[2]
    n_a = a_w_in.shape[0]
    n_b = b_w_q.shape[0]
    depth = ffn_w_gate_up.shape[0]
    xi, yi, ci = lax.axis_index("x"), lax.axis_index("y"), lax.axis_index("c")
    chip = 2 * xi + yi

    place = jnp.stack([chip, ci, 2 * chip + ci]).astype(jnp.int32)
    stacked = {"a_w_in": a_w_in, "a_w_out": a_w_out, "w_kv": w_kv[None], "b_w_q": b_w_q, "b_w_o": b_w_o,
               "ffn_w_gate_up": ffn_w_gate_up, "ffn_w_down": ffn_w_down}
    groups = []
    for layer in range(depth):
        if layer < n_a:
            groups.append([("a_w_in", layer), ("a_w_out", layer)])
        elif layer == n_a:
            groups.append([("w_kv", 0), ("b_w_q", 0), ("b_w_o", 0)])
        else:
            groups.append([("b_w_q", layer - n_a), ("b_w_o", layer - n_a)])
        groups.append([("ffn_w_gate_up", layer), ("ffn_w_down", layer)])
    units = [u for group in groups for u in group]
    n_early = len(groups[0])
    slabs = [_cast_slab(stacked[k], l, place[:1], f"cast_{k}_{l}") for k, l in units[:n_early]]
    early = _allgather_start(slabs, "allgather_start_first")
    slabs = [_cast_slab(stacked[k], l, place[:1], f"cast_{k}_{l}", deps=(early[3],)) for k, l in units[n_early:]]
    late = _allgather_start(slabs, "allgather_start_rest")
    na_w, ns_w = a_norm.shape[1], a_sgu_norm.shape[1]
    small_g = _allgather_small(jnp.concatenate([a_norm, a_sgu_norm], axis=1), "allgather_small")
    a_norm_f = small_g[:, :, :na_w].transpose(1, 0, 2).reshape(n_a, N_CHIPS * na_w)
    a_sgu_f = small_g[:, :, na_w:].transpose(1, 0, 2).reshape(n_a, N_CHIPS * ns_w)
    W, relayed = {}, {}

    def relay(group_index, after):
        if group_index == len(groups):
            return ()
        group = groups[group_index]
        (send, recv, flying, _), first = (early, 0) if group_index == 0 else (late, units.index(group[0]) - n_early)
        relayed[group_index] = _allgather_relay(flying[first:first + len(group)], send, recv, first, after,
                                                f"allgather_relay_{group_index}")
        return (relayed[group_index][3],)

    def gathered(group_index, after):
        send2, recv2, arrays, _ = relayed.pop(group_index)
        W.update(zip(groups[group_index], _allgather_wait(arrays, send2, recv2, after, f"allgather_wait_{group_index}")))

    xc = x.reshape(S, D)
    saved = []
    kvp = x_kv = h_kv = None
    relay(0, late[3])
    order = ()
    for layer in range(depth):
        rec = {"x_in": xc}
        gathered(2 * layer, xc)
        if layer < n_a:
            i = layer
            rec["zpre"], rec["h"] = _norm_matmul(xc, a_norm_f[i][None], W["a_w_in", i], BF16, f"a{i}_in", deps=order)
            order = relay(2 * layer + 1, rec["h"])
            rec["uv"] = _sgu_fwd(rec["zpre"], a_sgu_f[i][None], a_w_spatial[i], a_b_spatial[i].T, f"a{i}_sgu",
                                 deps=order)
            xm = _matmul_res(rec["uv"], W["a_w_out", i], xc, f"a{i}_out")
        else:
            i = layer - n_a
            if i == 0:
                kv, h_kv = _norm_matmul(xc, kv_norm[None], W["w_kv", 0], BF16, "kv_proj")
                kvp = jnp.pad(kv, ((KV_PAD, 0), (0, 0)))
                x_kv = xc
            rec["q"], rec["h"] = _norm_matmul(xc, b_norm[i][None], W["b_w_q", i], BF16, f"b{i}_q", row_sharded=True)
            order = relay(2 * layer + 1, rec["h"])
            table = jnp.pad(b_rel_bias[i], ((0, 0), (0, TABLE_PAD - b_rel_bias.shape[2])))
            rec["bias"] = _rel_bias_fwd(table, f"b{i}_bias").transpose(1, 0, 2)
            rec["o"] = _attn_fwd(rec["q"], kvp, rec["bias"], f"b{i}_attn", deps=order)
            xm = _matmul_res(rec["o"], W["b_w_o", i], xc, f"b{i}_o")
        rec["x_mid"] = xm
        gathered(2 * layer + 1, xm)
        rec["gu"], rec["h_f"] = _norm_matmul(xm, ffn_norm[layer][None], W["ffn_w_gate_up", layer], BF16, f"f{layer}_in")
        order = relay(2 * layer + 2, rec["h_f"])
        xc = _matmul_res(rec["gu"], W["ffn_w_down", layer], xm, f"f{layer}_out", swiglu=True, deps=order)
        order = ()
        saved.append(rec)

    loss_tile, dx, d_final = _loss_head(xc, final_norm[None], loss_target.reshape(S, D), "loss_head")
    loss = lax.psum(loss_tile[0, 0], ("x", "y", "c"))

    started = []
    small_started = []
    pending = []

    held = []

    def weight_grad(unit, hold=False, **kw):
        full = (N_CHIPS,) + tuple(stacked[unit[0]].shape[1:])
        g = _matmul_tn(out_shape=full, name=f"d_{unit[0]}_{unit[1]}", deps=tuple(pending), **kw)
        pending.clear()
        held.append((unit, g))
        if hold:
            return ()
        send, recv, flying_g, flying_land, token = _reduce_start([g for _, g in held],
                                                                 f"reduce_start_{unit[0]}_{unit[1]}")
        started.append(([u for u, _ in held], send, recv, flying_g, flying_land))
        held.clear()
        return (token,)

    tt = 512
    tw = min(2048, S)
    tb = min(NORMBWD_ROWS, S)
    row_a = lambda w, rows=tw: pl.BlockSpec((rows, w), lambda o, t: (t, 0))
    d_ffn_norm, d_b_norm, d_a_norm, d_a_sgu = [None] * depth, [None] * n_b, [None] * n_a, [None] * n_a
    d_ws, d_bs, d_rel = [None] * n_a, [None] * n_a, [None] * n_b
    dkv = None
    first = lambda ref: ref[...]
    for layer in reversed(range(depth)):
        rec = saved[layer]
        r_d = ffn_w_down.shape[1]
        half_f = 2 * r_d
        token = weight_grad(
            ("ffn_w_down", layer), a_ops=[rec["gu"], rec["gu"]],
            a_specs=[pl.BlockSpec((tt, half_f), lambda o, t: (t, o)), pl.BlockSpec((tt, half_f), lambda o, t: (t, 2 + o))],
            a_fn=lambda g_ref, u_ref: _swiglu(g_ref[...].astype(F32), u_ref[...].astype(F32)).astype(BF16),
            b_op=dx, b_spec=row_a(D, tt), out_spec=pl.BlockSpec((2, r_d, D), lambda o, t: (o, 0, 0)),
            acc_shape=(half_f, D), n_outer=2, tt=tt, hold=True)
        dgu = _nt_swiglu_bwd(dx, W["ffn_w_down", layer], rec["gu"], f"f{layer}_dgu", deps=token)
        nsh = ffn_w_gate_up.shape[2]
        token = weight_grad(
            ("ffn_w_gate_up", layer), a_ops=[rec["h_f"]], a_specs=[row_a(D)], a_fn=first,
            b_op=dgu, b_spec=pl.BlockSpec((None, tw, nsh), lambda o, t: (o // 2, t, o % 2)),
            out_spec=pl.BlockSpec((None, D, nsh), lambda o, t: (o, 0, 0)), acc_shape=(D, nsh), n_outer=N_CHIPS, tt=tw)
        dx, d_ffn_norm[layer] = _nt_normbwd(
            dgu, pl.BlockSpec((None, tb, nsh), lambda i, k: (k // 2, i, k % 2)),
            W["ffn_w_gate_up", layer], pl.BlockSpec((None, D, nsh), lambda i, k: (k, 0, 0)),
            (D, nsh), N_CHIPS, rec["x_mid"], ffn_norm[layer][None], dx, f"f{layer}_dx", deps=token)
        if layer >= n_a:
            i = layer - n_a
            r_o = b_w_o.shape[1]
            token = weight_grad(
                ("b_w_o", i), a_ops=[rec["o"]], a_specs=[row_a(D)], a_fn=first, b_op=dx, b_spec=row_a(D),
                out_spec=pl.BlockSpec((N_CHIPS, r_o, D), lambda o, t: (0, 0, 0)), acc_shape=(D, D), n_outer=1, tt=tw,
                hold=True)
            do = _nt_rows(dx, W["b_w_o", i], N_CHIPS, BF16, f"b{i}_do", deps=token)
            dq, dkv, dbias = _attn_bwd(rec["q"], kvp, rec["bias"], do, dkv, f"b{i}_attn_bwd")
            d_rel[i] = _rel_bias_bwd(dbias.transpose(1, 0, 2), f"b{i}_dbias")[:, :b_rel_bias.shape[2]]
            token = weight_grad(
                ("b_w_q", i), a_ops=[rec["h"]], a_specs=[row_a(D)], a_fn=first, b_op=dq, b_spec=row_a(D),
                out_spec=pl.BlockSpec((N_CHIPS, r_o, D), lambda o, t: (0, 0, 0)), acc_shape=(D, D), n_outer=1, tt=tw)
            dx, d_b_norm[i] = _nt_normbwd(
                dq, pl.BlockSpec((tb, D), lambda i_, k: (i_, 0)),
                W["b_w_q", i], pl.BlockSpec((N_CHIPS, r_o, D), lambda i_, k: (0, 0, 0)),
                (D, D), 1, rec["x_in"], b_norm[i][None], dx, f"b{i}_dx", deps=token)
            if i == 0:
                dkv_b = dkv[:, KV_PAD:, :].astype(BF16)
                n_kv = w_kv.shape[1]
                token = weight_grad(
                    ("w_kv", 0), a_ops=[h_kv], a_specs=[row_a(D)], a_fn=first,
                    b_op=dkv_b, b_spec=pl.BlockSpec((None, tw, n_kv), lambda o, t: (o // 2, t, o % 2)),
                    out_spec=pl.BlockSpec((None, D, n_kv), lambda o, t: (o, 0, 0)), acc_shape=(D, n_kv),
                    n_outer=N_CHIPS, tt=tw)
                dx, d_kv_norm = _nt_normbwd(
                    dkv_b, pl.BlockSpec((None, tb, n_kv), lambda i_, k: (k // 2, i_, k % 2)),
                    W["w_kv", 0], pl.BlockSpec((None, D, n_kv), lambda i_, k: (k, 0, 0)),
                    (D, n_kv), N_CHIPS, x_kv, kv_norm[None], dx, "kv_dx", deps=token)
        else:
            i = layer
            r_w = a_w_out.shape[1]
            token = weight_grad(
                ("a_w_out", i), a_ops=[rec["uv"]], a_specs=[row_a(N_CHIPS * r_w, tw // 2)], a_fn=first,
                b_op=dx, b_spec=row_a(D, tw // 2), out_spec=pl.BlockSpec((N_CHIPS, r_w, D), lambda o, t: (0, 0, 0)),
                acc_shape=(N_CHIPS * r_w, D), n_outer=1, tt=tw // 2, hold=i > 0)
            duv = _nt_rows(dx, W["a_w_out", i], 2, BF16, f"a{i}_duv", deps=token)
            dz, d_a_sgu[i], d_ws[i], dbs = _sgu_bwd(rec["zpre"], duv, a_sgu_f[i][None], a_w_spatial[i],
                                                  a_w_spatial[i].transpose(0, 2, 1), a_b_spatial[i].T, f"a{i}_sgu_bwd")
            d_bs[i] = dbs[:, :, 0]
            if i == 0:
                batch = [d_a_sgu[0], d_ws[0][None], d_bs[0][None], d_ffn_norm[0]]
                small_started.append((batch, _gather_start(_pack_small(batch), "gather_start_late")))
                pending.append(small_started[-1][1][4])
            n_in = a_w_in.shape[2]
            token = weight_grad(
                ("a_w_in", i), a_ops=[rec["h"]], a_specs=[row_a(D)], a_fn=first,
                b_op=dz, b_spec=pl.BlockSpec((tw, n_in), lambda o, t: (t, o)),
                out_spec=pl.BlockSpec((None, D, n_in), lambda o, t: (o, 0, 0)), acc_shape=(D, n_in), n_outer=N_CHIPS,
                tt=tw)
            dx, d_a_norm[i] = _nt_normbwd(
                dz, pl.BlockSpec((tb, n_in), lambda i_, k: (i_, k)),
                W["a_w_in", i], pl.BlockSpec((None, D, n_in), lambda i_, k: (k, 0, 0)),
                (D, n_in), N_CHIPS, rec["x_in"], a_norm_f[i][None], dx, f"a{i}_dx", deps=token)
        if layer == 1:
            batch = [jnp.concatenate(d_a_norm[1:], axis=0), jnp.concatenate(d_a_sgu[1:], axis=0), jnp.stack(d_ws[1:]),
                     jnp.stack(d_bs[1:]), d_kv_norm, jnp.concatenate(d_b_norm, axis=0), jnp.stack(d_rel),
                     jnp.concatenate(d_ffn_norm[1:], axis=0), d_final]
            small_started.append((batch, _gather_start(_pack_small(batch), "gather_start_early")))
            pending.append(small_started[-1][1][4])
    grad_x = dx.reshape(x.shape)

    landed = _reduce_wait([(send, recv, g, land) for _, send, recv, g, land in started], dx, "reduce_wait")
    reduced_units = [unit for units_, *_ in started for unit in units_]
    halves = [_reduce_sum(g, land, place, f"reduce_sum_{unit[0]}_{unit[1]}")
              for unit, (g, land) in zip(reduced_units, [pair for gs, lands in landed for pair in zip(gs, lands)])]
    joined = dict(zip(reduced_units, _sibling_join(halves, "reduce_join")))
    reduced = {}

    (packed_e, land_e), (packed_l, land_l) = _gather_wait([s[:4] for _, s in small_started], dx, "gather_wait")
    total_e = _sum_gathered(packed_e, land_e, place[2:], "sum_small_grads_early")
    total_l = _sum_gathered(packed_l, land_l, place[2:], "sum_small_grads_late")
    total_t = _sum_devices(_gather_small(_pack_small([d_a_norm[0]]), "gather_small_grads_last"), "sum_small_grads_last")
    (e_a_norm, e_a_sgu, e_ws, e_bs, g_kv_norm, g_b_norm, g_rel, e_ffn_norm, g_final) = _unpack_small(
        total_e, [a.shape for a in small_started[0][0]])
    l_a_sgu, l_ws, l_bs, l_ffn_norm = _unpack_small(total_l, [a.shape for a in small_started[1][0]])
    (t_a_norm,) = _unpack_small(total_t, [d_a_norm[0].shape])
    g_a_norm = jnp.concatenate([t_a_norm, e_a_norm], axis=0)
    g_a_sgu = jnp.concatenate([l_a_sgu, e_a_sgu], axis=0)
    g_ws = jnp.concatenate([l_ws, e_ws], axis=0)
    g_bs = jnp.concatenate([l_bs, e_bs], axis=0)
    g_ffn_norm = jnp.concatenate([l_ffn_norm, e_ffn_norm], axis=0)
    reduced["a_norm"] = lax.dynamic_slice_in_dim(g_a_norm, chip * na_w, na_w, axis=1)
    reduced["a_sgu_norm"] = lax.dynamic_slice_in_dim(g_a_sgu, chip * ns_w, ns_w, axis=1)
    reduced.update(a_w_spatial=g_ws, a_b_spatial=g_bs, kv_norm=g_kv_norm.reshape(kv_norm.shape), b_norm=g_b_norm,
                   b_rel_bias=g_rel, ffn_norm=g_ffn_norm, final_norm=g_final.reshape(final_norm.shape))

    weights = dict(a_norm=a_norm, a_w_in=a_w_in, a_sgu_norm=a_sgu_norm, a_w_spatial=a_w_spatial,
                   a_b_spatial=a_b_spatial, a_w_out=a_w_out, kv_norm=kv_norm, w_kv=w_kv, b_norm=b_norm, b_w_q=b_w_q,
                   b_rel_bias=b_rel_bias, b_w_o=b_w_o, ffn_norm=ffn_norm, ffn_w_gate_up=ffn_w_gate_up,
                   ffn_w_down=ffn_w_down, final_norm=final_norm)
    m_in = dict(a_norm=m_a_norm, a_w_in=m_a_w_in, a_sgu_norm=m_a_sgu_norm, a_w_spatial=m_a_w_spatial,
                a_b_spatial=m_a_b_spatial, a_w_out=m_a_w_out, kv_norm=m_kv_norm, w_kv=m_w_kv, b_norm=m_b_norm,
                b_w_q=m_b_w_q, b_rel_bias=m_b_rel_bias, b_w_o=m_b_w_o, ffn_norm=m_ffn_norm,
                ffn_w_gate_up=m_ffn_w_gate_up, ffn_w_down=m_ffn_w_down, final_norm=m_final_norm)
    v_in = dict(a_norm=v_a_norm, a_w_in=v_a_w_in, a_sgu_norm=v_a_sgu_norm, a_w_spatial=v_a_w_spatial,
                a_b_spatial=v_a_b_spatial, a_w_out=v_a_w_out, kv_norm=v_kv_norm, w_kv=v_w_kv, b_norm=v_b_norm,
                b_w_q=v_b_w_q, b_rel_bias=v_b_rel_bias, b_w_o=v_b_w_o, ffn_norm=v_ffn_norm,
                ffn_w_gate_up=v_ffn_w_gate_up, ffn_w_down=v_ffn_w_down, final_norm=v_final_norm)
    grad_out, delta_out, m_out, v_out = [], [], [], []
    for key, w in weights.items():
        if key in stacked:
            as_layers = lambda a: a.reshape(stacked[key].shape)
            bufs = None
            for layer in range(stacked[key].shape[0]):
                bufs = _adamw_layer(as_layers(w), joined[key, layer], as_layers(m_in[key]), as_layers(v_in[key]),
                                    layer, bufs, f"adamw_{key}_{layer}")
            g, d, nm, nv = bufs
        else:
            g = reduced[key].reshape(w.shape)
            view = (1, w.shape[0]) if w.ndim == 1 else (-1, w.shape[-1])
            d, nm, nv = _adamw(w.reshape(view), g.reshape(view), m_in[key].reshape(view), v_in[key].reshape(view),
                               "adamw_" + key)
        grad_out.append(g.reshape(w.shape))
        delta_out.append(d.reshape(w.shape))
        m_out.append(nm.reshape(w.shape))
        v_out.append(nv.reshape(w.shape))
    return (loss, grad_x, *grad_out, *delta_out, *m_out, *v_out)
```

```python
import math

import jax
import jax.numpy as jnp
from jax import lax
from jax.experimental import pallas as pl
from jax.experimental.pallas import tpu as pltpu

F32, BF16 = jnp.float32, jnp.bfloat16
MESH = pl.DeviceIdType.MESH
HIGHEST = lax.Precision.HIGHEST
NT_DIMS = (((1,), (1,)), ((), ()))
TN_DIMS = (((0,), (0,)), ((), ()))

EPS = 1e-6
CHUNK = 64
A_CHUNK = 128
A_GROUPS = 8
N_HEADS = 16
HEAD_DIM = 64
N_LEFT = 8
MAX_REL = 256
ATTN_SCALE = HEAD_DIM ** -0.5
NEG_INF = -1e30
Q_BLOCK = 2 * CHUNK
KV_PAD = N_LEFT * CHUNK
BAND = KV_PAD + Q_BLOCK
DIAGS = BAND + Q_BLOCK
TABLE_PAD = 640
HEADS_PER_BLOCK = 2
BLOCKS_PER_STEP = 4
NORMBWD_ROWS = 1024

ADAM_LR, ADAM_B1, ADAM_B2, ADAM_EPS, ADAM_WD, ADAM_STEP = 0.001, 0.9, 0.999, 1e-08, 0.01, 10

VMEM_LIMIT_BYTES = 56 * 1024 * 1024
N_CHIPS = 4
N_DEV = 8
SMALL_COLS = 1024


def _cparams(n_grid):
    return pltpu.CompilerParams(dimension_semantics=("arbitrary",) * n_grid, vmem_limit_bytes=VMEM_LIMIT_BYTES)


def _sds(shape, dtype):
    return jax.ShapeDtypeStruct(tuple(shape), dtype)


def _gelu(x):
    return x * (0.5 * (1.0 + lax.erf(x * math.sqrt(0.5))))


def _gelu_and_grad(x):
    cdf = 0.5 * (1.0 + lax.erf(x * math.sqrt(0.5)))
    return x * cdf, cdf + x * (jnp.exp(-0.5 * x * x) * (1.0 / math.sqrt(2.0 * math.pi)))


def _rms_hat(xv):
    r = lax.rsqrt(jnp.mean(xv * xv, axis=-1, keepdims=True) + EPS)
    return xv * r, r


def _rms_bwd(xhat, r, g, dy):
    dxhat = dy * g
    dx = r * (dxhat - xhat * jnp.mean(dxhat * xhat, axis=-1, keepdims=True))
    return dx, dy * xhat


def _swiglu(gate, up):
    return (gate * jax.nn.sigmoid(gate)) * up


def _row_tile(rows, cols, itemsize, cap_bytes, align):
    t = rows
    while t * cols * itemsize > cap_bytes and t % (2 * align) == 0:
        t //= 2
    return t


def _cast_slab(w, layer, chip, name, deps=()):
    _, r, C = w.shape
    tr = _row_tile(r, C, 4, 4 * 1024 * 1024, 16)

    def body(chip_ref, w_ref, *rest):
        del chip_ref
        rest[-1][...] = w_ref[...].astype(BF16)

    grid_spec = pltpu.PrefetchScalarGridSpec(
        num_scalar_prefetch=1, grid=(r // tr,),
        in_specs=[pl.BlockSpec((None, tr, C), lambda i, chip_ref: (layer, i, 0))] + [HBM_SPEC] * len(deps),
        out_specs=pl.BlockSpec((None, tr, C), lambda i, chip_ref: (chip_ref[0], i, 0)))
    return pl.pallas_call(body, name=name, grid_spec=grid_spec, out_shape=_sds((N_CHIPS, r, C), BF16),
                          compiler_params=_cparams(1))(chip, w, *deps)


def _adamw_stacked(w, gs, m, v, name):
    L, r, C = w.shape
    tr = _row_tile(r, C, 4, 2 * 1024 * 1024, 8)
    nb = r // tr

    def body(w_ref, m_ref, v_ref, *rest):
        go_ref, d_ref, nm_ref, nv_ref = rest[-4:]
        layer = pl.program_id(0)
        gv = rest[0][...]
        for k in range(1, L):
            gv = jnp.where(layer == k, rest[k][...], gv)
        mn = ADAM_B1 * m_ref[...] + (1.0 - ADAM_B1) * gv
        vn = ADAM_B2 * v_ref[...] + (1.0 - ADAM_B2) * jnp.square(gv)
        m_hat = mn / (1.0 - ADAM_B1 ** ADAM_STEP)
        v_hat = vn / (1.0 - ADAM_B2 ** ADAM_STEP)
        d_ref[...] = -ADAM_LR * (m_hat / (jnp.sqrt(v_hat) + ADAM_EPS) + ADAM_WD * w_ref[...])
        nm_ref[...] = mn
        nv_ref[...] = vn
        go_ref[...] = gv

    def grad_spec(k):
        return pl.BlockSpec((tr, C), lambda l, i: (jnp.where(l == k, i, jnp.where(l > k, nb - 1, 0)), 0))

    stacked = pl.BlockSpec((None, tr, C), lambda l, i: (l, i, 0))
    return pl.pallas_call(body, name=name, grid=(L, nb), in_specs=[stacked] * 3 + [grad_spec(k) for k in range(L)],
                          out_specs=[stacked] * 4, out_shape=[_sds((L, r, C), F32)] * 4,
                          compiler_params=_cparams(2))(w, m, v, *gs)


def _adamw(w, g, m, v, name):
    R, C = w.shape
    tr = _row_tile(R, C, 4, 1024 * 1024, 8)

    def body(w_ref, g_ref, m_ref, v_ref, d_ref, nm_ref, nv_ref):
        gv = g_ref[...]
        mn = ADAM_B1 * m_ref[...] + (1.0 - ADAM_B1) * gv
        vn = ADAM_B2 * v_ref[...] + (1.0 - ADAM_B2) * jnp.square(gv)
        m_hat = mn / (1.0 - ADAM_B1 ** ADAM_STEP)
        v_hat = vn / (1.0 - ADAM_B2 ** ADAM_STEP)
        d_ref[...] = -ADAM_LR * (m_hat / (jnp.sqrt(v_hat) + ADAM_EPS) + ADAM_WD * w_ref[...])
        nm_ref[...] = mn
        nv_ref[...] = vn

    spec = pl.BlockSpec((tr, C), lambda i: (i, 0))
    return pl.pallas_call(body, name=name, grid=(R // tr,), in_specs=[spec] * 4, out_specs=[spec] * 3,
                          out_shape=[_sds((R, C), F32)] * 3, compiler_params=_cparams(1))(w, g, m, v)


def _norm_matmul(x, g, w_g, out_dtype, name, row_sharded=False, deps=(), tm=1024):
    S, D = x.shape
    tm = min(tm, S)
    if row_sharded:
        r, N = w_g.shape[1], w_g.shape[2]
        tn = 512
        w_spec = pl.BlockSpec((N_CHIPS, r, tn), lambda i, j: (0, 0, j))
    else:
        nsh = w_g.shape[2]
        N = N_CHIPS * nsh
        tn = 512 if nsh % 512 == 0 else nsh
        bps = nsh // tn
        w_spec = pl.BlockSpec((None, D, tn), lambda i, j: (j // bps, 0, j % bps))

    def body(x_ref, g_ref, w_ref, *rest):
        y_ref, h_ref = rest[-2:]

        @pl.when(pl.program_id(1) == 0)
        def _():
            xhat, _ = _rms_hat(x_ref[...])
            h_ref[...] = (xhat * g_ref[...]).astype(BF16)

        w = w_ref[...].reshape(D, tn)
        y_ref[...] = jnp.dot(h_ref[...], w, preferred_element_type=F32).astype(y_ref.dtype)

    return pl.pallas_call(
        body, name=name, grid=(S // tm, N // tn),
        in_specs=[pl.BlockSpec((tm, D), lambda i, j: (i, 0)), pl.BlockSpec((1, D), lambda i, j: (0, 0)), w_spec]
        + [HBM_SPEC] * len(deps),
        out_specs=[pl.BlockSpec((tm, tn), lambda i, j: (i, j)), pl.BlockSpec((tm, D), lambda i, j: (i, 0))],
        out_shape=[_sds((S, N), out_dtype), _sds((S, D), BF16)],
        compiler_params=_cparams(2))(x, g, w_g, *deps)


def _matmul_res(a, w_g, res, name, swiglu=False, deps=(), tm=256):
    S, N = res.shape
    r = w_g.shape[1]
    K = N_CHIPS * r

    def body(*refs):
        o_ref = refs[-1]
        if swiglu:
            gate_ref, up_ref, w_ref, res_ref = refs[:4]
            a_blk = _swiglu(gate_ref[...].astype(F32), up_ref[...].astype(F32)).astype(BF16)
        else:
            a_ref, w_ref, res_ref = refs[:3]
            a_blk = a_ref[...]
        o_ref[...] = res_ref[...] + jnp.dot(a_blk, w_ref[...].reshape(K, N), preferred_element_type=F32)

    a_specs, a_ops = [pl.BlockSpec((tm, K), lambda i: (i, 0))], [a]
    if swiglu:
        a_specs.append(pl.BlockSpec((tm, K), lambda i: (i, 1)))
        a_ops.append(a)
    row = pl.BlockSpec((tm, N), lambda i: (i, 0))
    return pl.pallas_call(
        body, name=name, grid=(S // tm,),
        in_specs=a_specs + [pl.BlockSpec((N_CHIPS, r, N), lambda i: (0, 0, 0)), row] + [HBM_SPEC] * len(deps),
        out_specs=row, out_shape=_sds((S, N), F32), compiler_params=_cparams(1))(*a_ops, w_g, res, *deps)


def _matmul_tn(a_ops, a_specs, a_fn, b_op, b_spec, out_spec, out_shape, acc_shape, n_outer, name, deps=(), tt=512):
    S = b_op.shape[-2]
    na = len(a_ops)
    nt = S // tt

    def body(*refs):
        a_refs, b_ref, o_ref, acc_ref = refs[:na], refs[na], refs[-2], refs[-1]
        t = pl.program_id(1)
        part = lax.dot_general(a_fn(*a_refs), b_ref[...].astype(BF16), TN_DIMS, preferred_element_type=F32)

        @pl.when(t == 0)
        def _():
            acc_ref[...] = part

        @pl.when(t > 0)
        def _():
            acc_ref[...] += part

        @pl.when(t == nt - 1)
        def _():
            o_ref[...] = acc_ref[...].reshape(o_ref.shape).astype(BF16)

    return pl.pallas_call(
        body, name=name, grid=(n_outer, nt), in_specs=list(a_specs) + [b_spec] + [HBM_SPEC] * len(deps),
        out_specs=out_spec, out_shape=_sds(out_shape, BF16), scratch_shapes=[pltpu.VMEM(acc_shape, F32)],
        compiler_params=_cparams(2))(*a_ops, b_op, *deps)


def _nt_accumulate(a_ref, w_ref, acc_ref, w2d, nk):
    k = pl.program_id(1)
    part = lax.dot_general(a_ref[...].astype(BF16), w_ref[...].reshape(w2d), NT_DIMS, preferred_element_type=F32)

    @pl.when(k == 0)
    def _():
        acc_ref[...] = part

    @pl.when(k > 0)
    def _():
        acc_ref[...] += part

    return k == nk - 1


def _nt_normbwd(dy, dy_spec, w_g, w_spec, w2d, nk, x, g, dres, name, deps=(), tm=NORMBWD_ROWS):
    S, D = x.shape
    tm = min(tm, S)

    def body(dy_ref, w_ref, x_ref, g_ref, dres_ref, *rest):
        dx_ref, dxb_ref, dg_ref, acc_ref = rest[-4:]

        @pl.when((pl.program_id(0) == 0) & (pl.program_id(1) == 0))
        def _():
            dg_ref[...] = jnp.zeros_like(dg_ref)

        last = _nt_accumulate(dy_ref, w_ref, acc_ref, w2d, nk)

        @pl.when(last)
        def _():
            xhat, r = _rms_hat(x_ref[...])
            dx, dgp = _rms_bwd(xhat, r, g_ref[...], acc_ref[...])
            total = dres_ref[...] + dx
            dx_ref[...] = total
            dxb_ref[...] = total.astype(BF16)
            dg_ref[...] += jnp.sum(dgp, axis=0, keepdims=True)

    row = pl.BlockSpec((tm, D), lambda i, k: (i, 0))
    vec = pl.BlockSpec((1, D), lambda i, k: (0, 0))
    return pl.pallas_call(
        body, name=name, grid=(S // tm, nk),
        in_specs=[dy_spec, w_spec, row, vec, row] + [HBM_SPEC] * len(deps), out_specs=[row, row, vec],
        out_shape=[_sds((S, D), F32), _sds((S, D), BF16), _sds((1, D), F32)],
        scratch_shapes=[pltpu.VMEM((tm, D), F32)], compiler_params=_cparams(2))(dy, w_g, x, g, dres, *deps)


def _nt_rows(dy, w_g, shards_per_block, out_dtype, name, deps=(), tm=1024):
    S, N = dy.shape
    tm = min(tm, S)
    r = w_g.shape[1]
    tn = shards_per_block * r

    def body(dy_ref, w_ref, *rest):
        o_ref = rest[-1]
        o_ref[...] = lax.dot_general(dy_ref[...].astype(BF16), w_ref[...].reshape(tn, N), NT_DIMS,
                                     preferred_element_type=F32).astype(o_ref.dtype)

    return pl.pallas_call(
        body, name=name, grid=(S // tm, N_CHIPS // shards_per_block),
        in_specs=[pl.BlockSpec((tm, N), lambda i, j: (i, 0)),
                  pl.BlockSpec((shards_per_block, r, N), lambda i, j: (j, 0, 0))] + [HBM_SPEC] * len(deps),
        out_specs=pl.BlockSpec((tm, tn), lambda i, j: (i, j)),
        out_shape=_sds((S, N_CHIPS * r), out_dtype), compiler_params=_cparams(2))(dy, w_g, *deps)


def _nt_swiglu_bwd(dy, w_g, gu, name, deps=(), tm=512):
    S, N = dy.shape
    r = w_g.shape[1]
    tn = 2 * r
    F = N_CHIPS * r

    def body(dy_ref, w_ref, gate_ref, up_ref, *rest):
        o_ref = rest[-1]
        dact = lax.dot_general(dy_ref[...].astype(BF16), w_ref[...].reshape(tn, N), NT_DIMS,
                               preferred_element_type=F32)
        gate, up = gate_ref[...].astype(F32), up_ref[...].astype(F32)
        sg = jax.nn.sigmoid(gate)
        silu = gate * sg
        o_ref[0] = ((dact * up) * (sg + silu * (1.0 - sg))).astype(BF16)
        o_ref[1] = (dact * silu).astype(BF16)

    return pl.pallas_call(
        body, name=name, grid=(2, S // tm),
        in_specs=[pl.BlockSpec((tm, N), lambda j, i: (i, 0)),
                  pl.BlockSpec((2, r, N), lambda j, i: (j, 0, 0)),
                  pl.BlockSpec((tm, tn), lambda j, i: (i, j)),
                  pl.BlockSpec((tm, tn), lambda j, i: (i, 2 + j))] + [HBM_SPEC] * len(deps),
        out_specs=pl.BlockSpec((2, tm, tn), lambda j, i: (0, i, j)),
        out_shape=_sds((2, S, F), BF16), compiler_params=_cparams(2))(dy, w_g, gu, gu, *deps)


def _chunk_causal_mask(transposed):
    i = lax.broadcasted_iota(jnp.int32, (A_CHUNK, A_CHUNK), 0) // CHUNK
    j = lax.broadcasted_iota(jnp.int32, (A_CHUNK, A_CHUNK), 1) // CHUNK
    return ((i <= j) if transposed else (i >= j)).astype(F32)


def _sgu_fwd(zpre, g_sgu, ws, bs_t, name, deps=()):
    S, F2 = zpre.shape
    F = F2 // 2
    gd = F // A_GROUPS

    windows = 2
    rows = windows * A_CHUNK

    def body(zu_ref, zv_ref, g_ref, ws_ref, b_ref, *rest):
        o_ref = rest[-1]
        vhat, _ = _rms_hat(_gelu(zv_ref[...].astype(F32)))
        vn = (vhat * g_ref[...]).astype(BF16)
        u = _gelu(zu_ref[...].astype(F32))
        mask = _chunk_causal_mask(False)
        for gi in range(A_GROUPS):
            sl = slice(gi * gd, (gi + 1) * gd)
            wm = (ws_ref[gi] * mask).astype(BF16)
            for w in range(windows):
                win = slice(w * A_CHUNK, (w + 1) * A_CHUNK)
                vs = jnp.dot(wm, vn[win, sl], preferred_element_type=F32) + b_ref[:, gi:gi + 1]
                o_ref[win, sl] = (u[win, sl] * vs).astype(BF16)

    return pl.pallas_call(
        body, name=name, grid=(S // rows,),
        in_specs=[pl.BlockSpec((rows, F), lambda i: (i, 0)),
                  pl.BlockSpec((rows, F), lambda i: (i, 1)),
                  pl.BlockSpec((1, F), lambda i: (0, 0)),
                  pl.BlockSpec((A_GROUPS, A_CHUNK, A_CHUNK), lambda i: (0, 0, 0)),
                  pl.BlockSpec((A_CHUNK, A_GROUPS), lambda i: (0, 0))] + [HBM_SPEC] * len(deps),
        out_specs=pl.BlockSpec((rows, F), lambda i: (i, 0)),
        out_shape=_sds((S, F), BF16), compiler_params=_cparams(1))(zpre, zpre, g_sgu, ws, bs_t, *deps)


def _sgu_bwd(zpre, duv, g_sgu, ws, ws_t, bs_t, name):
    S, F2 = zpre.shape
    F = F2 // 2
    gd = F // A_GROUPS

    def body(zu_ref, zv_ref, duv_ref, g_ref, ws_ref, wst_ref, b_ref, dz_ref, dg_ref, dws_ref, dbs_ref, dvn_ref):
        @pl.when(pl.program_id(0) == 0)
        def _():
            dg_ref[...] = jnp.zeros_like(dg_ref)
            dws_ref[...] = jnp.zeros_like(dws_ref)
            dbs_ref[...] = jnp.zeros_like(dbs_ref)

        gv = g_ref[...]
        u, u_grad = _gelu_and_grad(zu_ref[...].astype(F32))
        v, v_grad = _gelu_and_grad(zv_ref[...].astype(F32))
        vhat, r = _rms_hat(v)
        vn = (vhat * gv).astype(BF16)
        duv_v = duv_ref[...].astype(F32)
        dvs = duv_v * u
        dvs_b = dvs.astype(BF16)
        mask = _chunk_causal_mask(False)
        mask_t = _chunk_causal_mask(True)
        for gi in range(A_GROUPS):
            sl = slice(gi * gd, (gi + 1) * gd)
            wm = (ws_ref[gi] * mask).astype(BF16)
            vs = jnp.dot(wm, vn[:, sl], preferred_element_type=F32) + b_ref[:, gi:gi + 1]
            dz_ref[:, sl] = ((duv_v[:, sl] * vs) * u_grad[:, sl]).astype(BF16)
            dws_ref[gi] += lax.dot_general(dvs_b[:, sl], vn[:, sl], NT_DIMS, preferred_element_type=F32) * mask
            dbs_ref[gi] += jnp.broadcast_to(jnp.sum(dvs[:, sl], axis=1, keepdims=True), (A_CHUNK, A_CHUNK))
            wm_t = (wst_ref[gi] * mask_t).astype(BF16)
            dvn_ref[:, sl] = jnp.dot(wm_t, dvs_b[:, sl], preferred_element_type=F32)
        dv, dg_part = _rms_bwd(vhat, r, gv, dvn_ref[...])
        dg_ref[...] += jnp.sum(dg_part, axis=0, keepdims=True)
        dz_ref[:, F:] = (dv * v_grad).astype(BF16)

    blk = pl.BlockSpec((A_CHUNK, F), lambda i: (i, 0))
    const3 = pl.BlockSpec((A_GROUPS, A_CHUNK, A_CHUNK), lambda i: (0, 0, 0))
    return pl.pallas_call(
        body, name=name, grid=(S // A_CHUNK,),
        in_specs=[blk, pl.BlockSpec((A_CHUNK, F), lambda i: (i, 1)), blk,
                  pl.BlockSpec((1, F), lambda i: (0, 0)), const3, const3,
                  pl.BlockSpec((A_CHUNK, A_GROUPS), lambda i: (0, 0))],
        out_specs=[pl.BlockSpec((A_CHUNK, F2), lambda i: (i, 0)), pl.BlockSpec((1, F), lambda i: (0, 0)),
                   const3, const3],
        out_shape=[_sds((S, F2), BF16), _sds((1, F), F32), _sds((A_GROUPS, A_CHUNK, A_CHUNK), F32),
                   _sds((A_GROUPS, A_CHUNK, A_CHUNK), F32)],
        scratch_shapes=[pltpu.VMEM((A_CHUNK, F), F32)],
        compiler_params=_cparams(1))(zpre, zpre, duv, g_sgu, ws, ws_t, bs_t)


def _toeplitz_one_hot():
    row = lax.broadcasted_iota(jnp.int32, (TABLE_PAD, DIAGS), 0)
    j = lax.broadcasted_iota(jnp.int32, (TABLE_PAD, DIAGS), 1)
    idx = jnp.clip(KV_PAD + Q_BLOCK - j, -MAX_REL, MAX_REL) + MAX_REL
    return (row == idx).astype(F32)


def _rel_bias_fwd(table, name):
    H = table.shape[0]

    def body(t_ref, o_ref):
        diag = jnp.dot(t_ref[...], _toeplitz_one_hot(), precision=HIGHEST, preferred_element_type=F32)
        k_chunk = lax.broadcasted_iota(jnp.int32, (1, BAND), 1) // CHUNK

        def step(r, carry):
            q_chunk = r // CHUNK
            seen = (k_chunk >= q_chunk) & (k_chunk <= q_chunk + N_LEFT)
            o_ref[r] = pltpu.roll(diag, DIAGS - Q_BLOCK + r, 1)[:, :BAND] + jnp.where(seen, 0.0, NEG_INF)
            return carry

        lax.fori_loop(0, Q_BLOCK, step, 0)

    return pl.pallas_call(body, name=name, out_shape=_sds((Q_BLOCK, H, BAND), F32),
                          compiler_params=pltpu.CompilerParams(vmem_limit_bytes=VMEM_LIMIT_BYTES))(table)


def _rel_bias_bwd(dbias, name):
    H = dbias.shape[1]

    def body(d_ref, o_ref):
        def step(r, acc):
            row = jnp.concatenate([d_ref[r], jnp.zeros((H, DIAGS - BAND), F32)], axis=1)
            return acc + pltpu.roll(row, Q_BLOCK - r, 1)

        diag = lax.fori_loop(0, Q_BLOCK, step, jnp.zeros((H, DIAGS), F32))
        o_ref[...] = lax.dot_general(diag, _toeplitz_one_hot(), NT_DIMS, precision=HIGHEST,
                                     preferred_element_type=F32)

    return pl.pallas_call(body, name=name, out_shape=_sds((H, TABLE_PAD), F32),
                          compiler_params=pltpu.CompilerParams(vmem_limit_bytes=VMEM_LIMIT_BYTES))(dbias)


def _head_rows(t):
    lane = lax.broadcasted_iota(jnp.int32, t.shape, 1)
    zero = jnp.zeros_like(t)
    return jnp.concatenate([jnp.where(lane < HEAD_DIM, t, zero), jnp.where(lane >= HEAD_DIM, t, zero)], axis=0)


def _head_lanes(t2):
    lane = lax.broadcasted_iota(jnp.int32, (Q_BLOCK, t2.shape[1]), 1)
    return jnp.where(lane < HEAD_DIM, t2[:Q_BLOCK], t2[Q_BLOCK:])


def _attn_probs(q2, kb, bias2, block):
    kj = lax.broadcasted_iota(jnp.int32, (1, BAND), 1)
    before_start = jnp.where(block * Q_BLOCK + kj - KV_PAD >= 0, 0.0, NEG_INF)
    s = lax.dot_general(q2 * ATTN_SCALE, kb, NT_DIMS, preferred_element_type=F32) + bias2 + before_start
    e = jnp.exp(s - jnp.max(s, axis=-1, keepdims=True))
    return e / jnp.sum(e, axis=-1, keepdims=True)


def _attn_specs(S):
    lanes = HEADS_PER_BLOCK * HEAD_DIM
    rows = S + KV_PAD
    q_spec = pl.BlockSpec((BLOCKS_PER_STEP * Q_BLOCK, lanes), lambda h, i: (i, h))
    k_spec = pl.BlockSpec((rows, lanes), lambda h, i: (0, h))
    v_spec = pl.BlockSpec((rows, lanes), lambda h, i: (0, N_HEADS // HEADS_PER_BLOCK + h))
    b_spec = pl.BlockSpec((HEADS_PER_BLOCK, Q_BLOCK, BAND), lambda h, i: (h, 0, 0))
    return q_spec, k_spec, v_spec, b_spec


def _attn_fwd(q, kvp, bias, name, deps=()):
    S, HD = q.shape
    q_spec, k_spec, v_spec, b_spec = _attn_specs(S)

    def body(q_ref, k_ref, v_ref, b_ref, *rest):
        o_ref = rest[-1]
        for b in range(BLOCKS_PER_STEP):
            block = pl.program_id(1) * BLOCKS_PER_STEP + b
            rows = slice(b * Q_BLOCK, (b + 1) * Q_BLOCK)
            band = pl.ds(pl.multiple_of(block * Q_BLOCK, Q_BLOCK), BAND)
            p = _attn_probs(_head_rows(q_ref[rows, :]), k_ref[band, :], b_ref[...].reshape(2 * Q_BLOCK, BAND), block)
            o2 = jnp.dot(p.astype(BF16), v_ref[band, :], preferred_element_type=F32)
            o_ref[rows, :] = _head_lanes(o2).astype(BF16)

    return pl.pallas_call(
        body, name=name, grid=(N_HEADS // HEADS_PER_BLOCK, S // (BLOCKS_PER_STEP * Q_BLOCK)),
        in_specs=[q_spec, k_spec, v_spec, b_spec] + [HBM_SPEC] * len(deps), out_specs=q_spec,
        out_shape=_sds((S, HD), BF16), compiler_params=_cparams(2))(q, kvp, kvp, bias, *deps)


def _attn_bwd(q, kvp, bias, do, dkv_prev, name):
    S, HD = q.shape
    lanes = HEADS_PER_BLOCK * HEAD_DIM
    q_spec, k_spec, v_spec, b_spec = _attn_specs(S)
    dkv_spec = pl.BlockSpec((2, S + KV_PAD, lanes), lambda h, i: (0, 0, h))
    prev = [] if dkv_prev is None else [dkv_prev]

    def body(q_ref, k_ref, v_ref, b_ref, do_ref, *rest):
        dq_ref, dkv_ref, db_ref = rest[-3:]

        @pl.when(pl.program_id(1) == 0)
        def _():
            dkv_ref[...] = rest[0][...] if prev else jnp.zeros_like(dkv_ref)
            db_ref[...] = jnp.zeros_like(db_ref)

        db = jnp.zeros((2 * Q_BLOCK, BAND), F32)
        for b in range(BLOCKS_PER_STEP):
            block = pl.program_id(1) * BLOCKS_PER_STEP + b
            rows = slice(b * Q_BLOCK, (b + 1) * Q_BLOCK)
            band = pl.ds(pl.multiple_of(block * Q_BLOCK, Q_BLOCK), BAND)
            kb, vb = k_ref[band, :], v_ref[band, :]
            q2, do2 = _head_rows(q_ref[rows, :]), _head_rows(do_ref[rows, :])
            p = _attn_probs(q2, kb, b_ref[...].reshape(2 * Q_BLOCK, BAND), block)
            dp = lax.dot_general(do2, vb, NT_DIMS, preferred_element_type=F32)
            ds = p * (dp - jnp.sum(dp * p, axis=-1, keepdims=True))
            db = db + ds
            ds_b = (ds * ATTN_SCALE).astype(BF16)
            dq_ref[rows, :] = _head_lanes(jnp.dot(ds_b, kb, preferred_element_type=F32)).astype(BF16)
            dkv_ref[0, band, :] += lax.dot_general(ds_b, q2, TN_DIMS, preferred_element_type=F32)
            dkv_ref[1, band, :] += lax.dot_general(p.astype(BF16), do2, TN_DIMS, preferred_element_type=F32)
        db_ref[...] += db.reshape(HEADS_PER_BLOCK, Q_BLOCK, BAND)

    return pl.pallas_call(
        body, name=name, grid=(N_HEADS // HEADS_PER_BLOCK, S // (BLOCKS_PER_STEP * Q_BLOCK)),
        in_specs=[q_spec, k_spec, v_spec, b_spec, q_spec] + [dkv_spec] * len(prev),
        out_specs=[q_spec, dkv_spec, b_spec],
        out_shape=[_sds((S, HD), BF16), _sds((2, S + KV_PAD, HD), F32), _sds((N_HEADS, Q_BLOCK, BAND), F32)],
        compiler_params=_cparams(2))(q, kvp, kvp, bias, do, *prev)


def _loss_head(x, g, target, name, tm=512):
    S, D = x.shape

    def body(x_ref, g_ref, t_ref, loss_ref, dx_ref, dxb_ref, dg_ref):
        @pl.when(pl.program_id(0) == 0)
        def _():
            loss_ref[...] = jnp.zeros_like(loss_ref)
            dg_ref[...] = jnp.zeros_like(dg_ref)

        xhat, r = _rms_hat(x_ref[...])
        gv = g_ref[...]
        err = xhat * gv - t_ref[...]
        loss_ref[...] += 0.5 * jnp.sum(jnp.mean(err * err, axis=-1, keepdims=True))
        dx, dgp = _rms_bwd(xhat, r, gv, err * (1.0 / D))
        dx_ref[...] = dx
        dxb_ref[...] = dx.astype(BF16)
        dg_ref[...] += jnp.sum(dgp, axis=0, keepdims=True)

    row = pl.BlockSpec((tm, D), lambda i: (i, 0))
    vec = pl.BlockSpec((1, D), lambda i: (0, 0))
    return pl.pallas_call(
        body, name=name, grid=(S // tm,), in_specs=[row, vec, row],
        out_specs=[pl.BlockSpec((8, 128), lambda i: (0, 0)), row, row, vec],
        out_shape=[_sds((8, 128), F32), _sds((S, D), F32), _sds((S, D), BF16), _sds((1, D), F32)],
        compiler_params=_cparams(1))(x, g, target)


def _place():
    x, y, c = lax.axis_index("x"), lax.axis_index("y"), lax.axis_index("c")
    chips = [(1 - x, y), (x, 1 - y), (1 - x, 1 - y)]
    return x, y, c, chips


def _half_rows(c, r):
    return pl.ds(pl.multiple_of(c * (r // 2), 8), r // 2)


HBM_SPEC = pl.BlockSpec(memory_space=pl.ANY)


STRICT_HBM_SPEC = pl.BlockSpec(memory_space=pltpu.HBM)
SEM_SPEC = pl.BlockSpec(memory_space=pltpu.SEMAPHORE)
EFFECT = pltpu.SideEffectType.DATAFLOW_SIDE_EFFECTING


def _peers(x, y, c):
    out = []
    for k in range(1, N_DEV):
        px, py, pc = (x + ((k >> 2) & 1)) % 2, (y + ((k >> 1) & 1)) % 2, (c + (k & 1)) % 2
        out.append(((px, py, pc), 2 * px + py, pc, 4 * px + 2 * py + pc))
    return out


def _token_spec():
    return pl.BlockSpec(memory_space=pltpu.VMEM)


def _hbm(a):
    return pltpu.with_memory_space_constraint(a, pltpu.HBM)


def _slab_half(ref, chip, core):
    return ref.at[2 * chip[0] + chip[1], _half_rows(core, ref.shape[1]), :]


def _allgather_start(slabs, name):
    n = len(slabs)

    def body(*refs):
        src, send, recv, token = refs[:n], refs[n], refs[n + 1], refs[-1]
        x, y, c, chips = _place()
        for a in range(n):
            own = _slab_half(src[a], (x, y), c)
            for j, chip in enumerate(chips):
                pltpu.make_async_remote_copy(src_ref=own, dst_ref=own, send_sem=send.at[3 * a + j], recv_sem=recv.at[3 * a + j],
                                             device_id=(*chip, c), device_id_type=MESH).start()
        token[...] = jnp.zeros_like(token)

    sems = pltpu.SemaphoreType.DMA((3 * n,))
    send, recv, *flying, token = pl.pallas_call(
        body, name=name, in_specs=[STRICT_HBM_SPEC] * n,
        out_shape=(sems, sems, *[pltpu.HBM(s.shape, s.dtype) for s in slabs], _sds((8, 128), F32)),
        out_specs=(SEM_SPEC, SEM_SPEC, *[STRICT_HBM_SPEC] * n, _token_spec()),
        input_output_aliases={a: a + 2 for a in range(n)},
        compiler_params=pltpu.CompilerParams(has_side_effects=EFFECT))(*[_hbm(s) for s in slabs])
    return send, recv, flying, token


def _allgather_relay(flying, send, recv, first, after, name):
    n = len(flying)

    def body(*refs):
        src, send_ref, recv_ref = refs[:n], refs[n], refs[n + 1]
        send2, recv2, token = refs[n + 3], refs[n + 4], refs[-1]
        token[...] = jnp.zeros_like(token)
        x, y, c, chips = _place()
        for a in range(n):
            for j, chip in enumerate(chips):
                cp = pltpu.make_async_remote_copy(
                    src_ref=_slab_half(src[a], (x, y), c), dst_ref=_slab_half(src[a], chip, c),
                    send_sem=send_ref.at[3 * (first + a) + j], recv_sem=recv_ref.at[3 * (first + a) + j],
                    device_id=(*chip, c), device_id_type=MESH)
                cp.wait_send()
                cp.wait_recv()
        for a in range(n):
            for j, chip in enumerate(chips):
                landed = _slab_half(src[a], chip, c)
                pltpu.make_async_remote_copy(src_ref=landed, dst_ref=landed, send_sem=send2.at[3 * a + j],
                                             recv_sem=recv2.at[3 * a + j], device_id=(x, y, 1 - c),
                                             device_id_type=MESH).start()

    sems = pltpu.SemaphoreType.DMA((3 * n,))
    send2, recv2, *relayed, token = pl.pallas_call(
        body, name=name, in_specs=[STRICT_HBM_SPEC] * n + [SEM_SPEC, SEM_SPEC, HBM_SPEC],
        out_shape=(sems, sems, *[pltpu.HBM(s.shape, s.dtype) for s in flying], _sds((8, 128), F32)),
        out_specs=(SEM_SPEC, SEM_SPEC, *[STRICT_HBM_SPEC] * n, _token_spec()),
        input_output_aliases={a: a + 2 for a in range(n)},
        compiler_params=pltpu.CompilerParams(has_side_effects=EFFECT))(*flying, send, recv, after)
    return send2, recv2, relayed, token


def _allgather_wait(relayed, send2, recv2, after, name):
    n = len(relayed)

    def body(*refs):
        src, send_ref, recv_ref = refs[:n], refs[n], refs[n + 1]
        x, y, c, chips = _place()
        for a in range(n):
            for j, chip in enumerate(chips):
                cp = pltpu.make_async_remote_copy(
                    src_ref=_slab_half(src[a], chip, c), dst_ref=_slab_half(src[a], chip, 1 - c),
                    send_sem=send_ref.at[3 * a + j], recv_sem=recv_ref.at[3 * a + j],
                    device_id=(x, y, 1 - c), device_id_type=MESH)
                cp.wait_send()
                cp.wait_recv()

    return pl.pallas_call(
        body, name=name, in_specs=[STRICT_HBM_SPEC] * n + [SEM_SPEC, SEM_SPEC, HBM_SPEC],
        out_shape=tuple(pltpu.HBM(s.shape, s.dtype) for s in relayed), out_specs=tuple([STRICT_HBM_SPEC] * n),
        input_output_aliases={a: a for a in range(n)},
        compiler_params=pltpu.CompilerParams(has_side_effects=EFFECT))(*relayed, send2, recv2, after)


def _allgather_small(small, name):
    def body(sm, osm, send, recv, local):
        x, y, c, chips = _place()
        own = pltpu.make_async_copy(sm, osm.at[2 * x + y], local)
        own.start()
        cps = [pltpu.make_async_remote_copy(src_ref=sm, dst_ref=osm.at[2 * x + y], send_sem=send.at[j],
                                            recv_sem=recv.at[j], device_id=(*chip, c), device_id_type=MESH)
               for j, chip in enumerate(chips)]
        for cp in cps:
            cp.start()
        for j, chip in enumerate(chips):
            got = osm.at[2 * chip[0] + chip[1]]
            pltpu.make_async_remote_copy(src_ref=got, dst_ref=got, send_sem=send.at[j], recv_sem=recv.at[j],
                                         device_id=(x, y, c), device_id_type=MESH).wait_recv()
        for cp in cps:
            cp.wait_send()
        own.wait()

    return pl.pallas_call(
        body, name=name, in_specs=[pl.BlockSpec(memory_space=pltpu.VMEM)], out_specs=HBM_SPEC,
        out_shape=_sds((N_CHIPS, *small.shape), small.dtype),
        scratch_shapes=[pltpu.SemaphoreType.DMA((3,)), pltpu.SemaphoreType.DMA((3,)), pltpu.SemaphoreType.DMA])(small)


def _reduce_start(grads, name):
    n = len(grads)

    def body(*refs):
        src, land, send, recv, token = refs[:n], refs[n:2 * n], refs[2 * n], refs[2 * n + 1], refs[-1]
        x, y, c, _ = _place()
        me = 4 * x + 2 * y + c
        for a in range(n):
            for k, (peer, p_chip, p_core, _) in enumerate(_peers(x, y, c)):
                pltpu.make_async_remote_copy(
                    src_ref=src[a].at[p_chip, _half_rows(p_core, src[a].shape[1]), :], dst_ref=land[a].at[me],
                    send_sem=send.at[(N_DEV - 1) * a + k], recv_sem=recv.at[(N_DEV - 1) * a + k],
                    device_id=peer, device_id_type=MESH).start()
        token[...] = jnp.zeros_like(token)

    lands = [lax.empty((N_DEV, g.shape[1] // 2, g.shape[2]), BF16) for g in grads]
    sems = pltpu.SemaphoreType.DMA(((N_DEV - 1) * n,))
    shapes = [pltpu.HBM(a.shape, a.dtype) for a in grads + lands]
    send, recv, *flying, token = pl.pallas_call(
        body, name=name, in_specs=[STRICT_HBM_SPEC] * (2 * n),
        out_shape=(sems, sems, *shapes, _sds((8, 128), F32)),
        out_specs=(SEM_SPEC, SEM_SPEC, *[STRICT_HBM_SPEC] * (2 * n), _token_spec()),
        input_output_aliases={a: a + 2 for a in range(2 * n)},
        compiler_params=pltpu.CompilerParams(has_side_effects=EFFECT))(*[_hbm(a) for a in grads + lands])
    return send, recv, flying[:n], flying[n:], token


def _reduce_wait(started, after, name):
    sizes = [len(grads) for _, _, grads, _ in started]
    n_arr = 2 * sum(sizes)

    def body(*refs):
        x, y, c, _ = _place()
        at = 0
        for s, n in enumerate(sizes):
            src, land = refs[at:at + n], refs[at + n:at + 2 * n]
            send_ref, recv_ref = refs[n_arr + 2 * s], refs[n_arr + 2 * s + 1]
            at += 2 * n
            for a in range(n):
                for k, (peer, p_chip, p_core, p_dev) in enumerate(_peers(x, y, c)):
                    cp = pltpu.make_async_remote_copy(
                        src_ref=src[a].at[p_chip, _half_rows(p_core, src[a].shape[1]), :], dst_ref=land[a].at[p_dev],
                        send_sem=send_ref.at[(N_DEV - 1) * a + k], recv_sem=recv_ref.at[(N_DEV - 1) * a + k],
                        device_id=peer, device_id_type=MESH)
                    cp.wait_send()
                    cp.wait_recv()

    arrays, sems = [], []
    for send, recv, grads, lands in started:
        arrays += list(grads) + list(lands)
        sems += [send, recv]
    out = pl.pallas_call(
        body, name=name, in_specs=[STRICT_HBM_SPEC] * n_arr + [SEM_SPEC] * len(sems) + [HBM_SPEC],
        out_shape=tuple(pltpu.HBM(a.shape, a.dtype) for a in arrays), out_specs=tuple([STRICT_HBM_SPEC] * n_arr),
        input_output_aliases={a: a for a in range(n_arr)},
        compiler_params=pltpu.CompilerParams(has_side_effects=EFFECT))(*arrays, *sems, after)
    result, at = [], 0
    for n in sizes:
        result.append((out[at:at + n], out[at + n:at + 2 * n]))
        at += 2 * n
    return result


def _reduce_sum(grad, land, place, name):
    _, r2, C = land.shape
    tr = _row_tile(r2, C, 4, 1024 * 1024, 16)
    nb = r2 // tr

    def body(place_ref, own_ref, *rest):
        del place_ref
        acc = own_ref[...].astype(F32)
        for ref in rest[:N_DEV - 1]:
            acc = acc + ref[...].astype(F32)
        rest[-1][...] = acc

    def from_dev(k):
        return pl.BlockSpec((None, tr, C), lambda i, place_ref: ((place_ref[2] + k) % N_DEV, i, 0))

    grid_spec = pltpu.PrefetchScalarGridSpec(
        num_scalar_prefetch=1, grid=(nb,),
        in_specs=[pl.BlockSpec((None, tr, C), lambda i, place_ref: (place_ref[0], place_ref[1] * nb + i, 0))]
        + [from_dev(k) for k in range(1, N_DEV)],
        out_specs=pl.BlockSpec((tr, C), lambda i, place_ref: (place_ref[1] * nb + i, 0)))
    return pl.pallas_call(body, name=name, grid_spec=grid_spec, out_shape=_sds((2 * r2, C), F32),
                          compiler_params=_cparams(1))(place, grad, *[land] * (N_DEV - 1))


def _sibling_join(halves, name):
    n = len(halves)

    def body(*refs):
        out, send, recv = refs[n:2 * n], refs[2 * n], refs[2 * n + 1]
        x, y, c, _ = _place()
        cps = []
        for w in range(n):
            mine = out[w].at[_half_rows(c, out[w].shape[0]), :]
            cps.append(pltpu.make_async_remote_copy(src_ref=mine, dst_ref=mine, send_sem=send.at[w],
                                                    recv_sem=recv.at[w], device_id=(x, y, 1 - c), device_id_type=MESH))
        for cp in cps:
            cp.start()
        for w in range(n):
            theirs = out[w].at[_half_rows(1 - c, out[w].shape[0]), :]
            pltpu.make_async_remote_copy(src_ref=theirs, dst_ref=theirs, send_sem=send.at[w], recv_sem=recv.at[w],
                                         device_id=(x, y, c), device_id_type=MESH).wait_recv()
        for cp in cps:
            cp.wait_send()

    return pl.pallas_call(
        body, name=name, in_specs=[HBM_SPEC] * n, out_specs=[HBM_SPEC] * n,
        out_shape=[_sds(a.shape, F32) for a in halves], input_output_aliases={w: w for w in range(n)},
        scratch_shapes=[pltpu.SemaphoreType.DMA((n,)), pltpu.SemaphoreType.DMA((n,))])(*halves)


def _gather_small(packed, name):
    def body(p_ref, out, send, recv, local):
        x, y, c, _ = _place()
        me = 4 * x + 2 * y + c
        own = pltpu.make_async_copy(p_ref, out.at[me], local)
        own.start()
        cps = []
        for k in range(1, N_DEV):
            fx, fy, fc = (k >> 2) & 1, (k >> 1) & 1, k & 1
            peer = ((x + fx) % 2, (y + fy) % 2, (c + fc) % 2)
            cps.append(pltpu.make_async_remote_copy(src_ref=p_ref, dst_ref=out.at[me], send_sem=send.at[k - 1],
                                                    recv_sem=recv.at[k - 1], device_id=peer, device_id_type=MESH))
        for cp in cps:
            cp.start()
        for k in range(1, N_DEV):
            fx, fy, fc = (k >> 2) & 1, (k >> 1) & 1, k & 1
            src = out.at[4 * ((x + fx) % 2) + 2 * ((y + fy) % 2) + (c + fc) % 2]
            pltpu.make_async_remote_copy(src_ref=src, dst_ref=src, send_sem=send.at[k - 1], recv_sem=recv.at[k - 1],
                                         device_id=(x, y, c), device_id_type=MESH).wait_recv()
        for cp in cps:
            cp.wait_send()
        own.wait()

    return pl.pallas_call(
        body, name=name, in_specs=[pl.BlockSpec(memory_space=pltpu.VMEM)], out_specs=HBM_SPEC,
        out_shape=_sds((N_DEV, *packed.shape), F32),
        scratch_shapes=[pltpu.SemaphoreType.DMA((N_DEV - 1,)), pltpu.SemaphoreType.DMA((N_DEV - 1,)),
                        pltpu.SemaphoreType.DMA])(packed)


def _sum_devices(gathered, name):
    _, R, C = gathered.shape

    def body(g_ref, o_ref):
        acc = g_ref[0]
        for d in range(1, N_DEV):
            acc = acc + g_ref[d]
        o_ref[...] = acc

    tr = 8
    return pl.pallas_call(
        body, name=name, grid=(R // tr,), in_specs=[pl.BlockSpec((N_DEV, tr, C), lambda i: (0, i, 0))],
        out_specs=pl.BlockSpec((tr, C), lambda i: (i, 0)), out_shape=_sds((R, C), F32),
        compiler_params=_cparams(1))(gathered)


def _gather_start(packed, name):
    def body(src, land, send, recv, *rest):
        x, y, c, _ = _place()
        for k, (peer, _, _, _) in enumerate(_peers(x, y, c)):
            pltpu.make_async_remote_copy(src_ref=src, dst_ref=land.at[4 * x + 2 * y + c], send_sem=send.at[k],
                                         recv_sem=recv.at[k], device_id=peer, device_id_type=MESH).start()
        rest[-1][...] = jnp.zeros_like(rest[-1])

    land = lax.empty((N_DEV, *packed.shape), F32)
    sems = pltpu.SemaphoreType.DMA((N_DEV - 1,))
    return pl.pallas_call(
        body, name=name, in_specs=[STRICT_HBM_SPEC] * 2,
        out_shape=(sems, sems, pltpu.HBM(packed.shape, F32), pltpu.HBM(land.shape, F32), _sds((8, 128), F32)),
        out_specs=(SEM_SPEC, SEM_SPEC, STRICT_HBM_SPEC, STRICT_HBM_SPEC, _token_spec()),
        input_output_aliases={0: 2, 1: 3},
        compiler_params=pltpu.CompilerParams(has_side_effects=EFFECT))(_hbm(packed), _hbm(land))


def _gather_wait(started, after, name):
    n = len(started)

    def body(*refs):
        x, y, c, _ = _place()
        for s in range(n):
            src, land, send, recv = refs[2 * s], refs[2 * s + 1], refs[2 * n + 2 * s], refs[2 * n + 2 * s + 1]
            for k, (peer, _, _, p_dev) in enumerate(_peers(x, y, c)):
                cp = pltpu.make_async_remote_copy(src_ref=src, dst_ref=land.at[p_dev], send_sem=send.at[k],
                                                  recv_sem=recv.at[k], device_id=peer,
                                                  device_id_type=MESH)
                cp.wait_send()
                cp.wait_recv()

    arrays = [a for _, _, packed, land in started for a in (packed, land)]
    sems = [s for send, recv, _, _ in started for s in (send, recv)]
    out = pl.pallas_call(
        body, name=name, in_specs=[STRICT_HBM_SPEC] * (2 * n) + [SEM_SPEC] * (2 * n) + [HBM_SPEC],
        out_shape=tuple(pltpu.HBM(a.shape, a.dtype) for a in arrays), out_specs=tuple([STRICT_HBM_SPEC] * (2 * n)),
        input_output_aliases={a: a for a in range(2 * n)},
        compiler_params=pltpu.CompilerParams(has_side_effects=EFFECT))(*arrays, *sems, after)
    return [(out[2 * s], out[2 * s + 1]) for s in range(n)]


def _sum_gathered(packed, land, device, name):
    R, C = packed.shape
    tr = 8

    def body(dev_ref, own_ref, *rest):
        me = dev_ref[0]
        acc = None
        for d in range(N_DEV):
            term = jnp.where(me == d, own_ref[...], rest[d][...])
            acc = term if acc is None else acc + term
        rest[-1][...] = acc

    def slab(d):
        return pl.BlockSpec((None, tr, C), lambda i, dev_ref: (jnp.where(dev_ref[0] == d, (d + 1) % N_DEV, d), i, 0))

    grid_spec = pltpu.PrefetchScalarGridSpec(
        num_scalar_prefetch=1, grid=(R // tr,),
        in_specs=[pl.BlockSpec((tr, C), lambda i, dev_ref: (i, 0))] + [slab(d) for d in range(N_DEV)],
        out_specs=pl.BlockSpec((tr, C), lambda i, dev_ref: (i, 0)))
    return pl.pallas_call(body, name=name, grid_spec=grid_spec, out_shape=_sds((R, C), F32),
                          compiler_params=_cparams(1))(device, packed, *[land] * N_DEV)


def _pack_small(arrays):
    rows = []
    for a in arrays:
        flat = a.reshape(-1)
        pad = (-flat.shape[0]) % SMALL_COLS
        rows.append(jnp.pad(flat, (0, pad)).reshape(-1, SMALL_COLS))
    packed = jnp.concatenate(rows, axis=0)
    return jnp.pad(packed, ((0, (-packed.shape[0]) % 8), (0, 0)))


def _unpack_small(packed, shapes):
    out, row = [], 0
    for shape in shapes:
        size = math.prod(shape)
        n_rows = -(-size // SMALL_COLS)
        out.append(packed[row:row + n_rows].reshape(-1)[:size].reshape(shape))
        row += n_rows
    return out


def kernel(x, a_norm, a_w_in, a_sgu_norm, a_w_spatial, a_b_spatial, a_w_out, kv_norm, w_kv, b_norm, b_w_q, b_rel_bias, b_w_o, ffn_norm, ffn_w_gate_up, ffn_w_down, final_norm, loss_target, m_a_norm, m_a_w_in, m_a_sgu_norm, m_a_w_spatial, m_a_b_spatial, m_a_w_out, m_kv_norm, m_w_kv, m_b_norm, m_b_w_q, m_b_rel_bias, m_b_w_o, m_ffn_norm, m_ffn_w_gate_up, m_ffn_w_down, m_final_norm, v_a_norm, v_a_w_in, v_a_sgu_norm, v_a_w_spatial, v_a_b_spatial, v_a_w_out, v_kv_norm, v_w_kv, v_b_norm, v_b_w_q, v_b_rel_bias, v_b_w_o, v_ffn_norm, v_ffn_w_gate_up, v_ffn_w_down, v_final_norm):
    S, D = x.shape[1], x.shape[2]
    n_a = a_w_in.shape[0]
    n_b = b_w_q.shape[0]
    depth = ffn_w_gate_up.shape[0]
    xi, yi, ci = lax.axis_index("x"), lax.axis_index("y"), lax.axis_index("c")
    chip = 2 * xi + yi

    place = jnp.stack([chip, ci, 2 * chip + ci]).astype(jnp.int32)
    stacked = {"a_w_in": a_w_in, "a_w_out": a_w_out, "w_kv": w_kv[None], "b_w_q": b_w_q, "b_w_o": b_w_o,
               "ffn_w_gate_up": ffn_w_gate_up, "ffn_w_down": ffn_w_down}
    groups = []
    for layer in range(depth):
        if layer < n_a:
            groups.append([("a_w_in", layer), ("a_w_out", layer)])
        elif layer == n_a:
            groups.append([("w_kv", 0), ("b_w_q", 0), ("b_w_o", 0)])
        else:
            groups.append([("b_w_q", layer - n_a), ("b_w_o", layer - n_a)])
        groups.append([("ffn_w_gate_up", layer), ("ffn_w_down", layer)])
    units = [u for group in groups for u in group]
    n_early = len(groups[0])
    slabs = [_cast_slab(stacked[k], l, place[:1], f"cast_{k}_{l}") for k, l in units[:n_early]]
    early = _allgather_start(slabs, "allgather_start_first")
    slabs = [_cast_slab(stacked[k], l, place[:1], f"cast_{k}_{l}", deps=(early[3],)) for k, l in units[n_early:]]
    late = _allgather_start(slabs, "allgather_start_rest")
    na_w, ns_w = a_norm.shape[1], a_sgu_norm.shape[1]
    small_g = _allgather_small(jnp.concatenate([a_norm, a_sgu_norm], axis=1), "allgather_small")
    a_norm_f = small_g[:, :, :na_w].transpose(1, 0, 2).reshape(n_a, N_CHIPS * na_w)
    a_sgu_f = small_g[:, :, na_w:].transpose(1, 0, 2).reshape(n_a, N_CHIPS * ns_w)
    W, relayed = {}, {}

    def relay(group_index, after):
        if group_index == len(groups):
            return ()
        group = groups[group_index]
        (send, recv, flying, _), first = (early, 0) if group_index == 0 else (late, units.index(group[0]) - n_early)
        relayed[group_index] = _allgather_relay(flying[first:first + len(group)], send, recv, first, after,
                                                f"allgather_relay_{group_index}")
        return (relayed[group_index][3],)

    def gathered(group_index, after):
        send2, recv2, arrays, _ = relayed.pop(group_index)
        W.update(zip(groups[group_index], _allgather_wait(arrays, send2, recv2, after, f"allgather_wait_{group_index}")))

    xc = x.reshape(S, D)
    saved = []
    kvp = x_kv = h_kv = None
    relay(0, late[3])
    order = ()
    for layer in range(depth):
        rec = {"x_in": xc}
        gathered(2 * layer, xc)
        if layer < n_a:
            i = layer
            rec["zpre"], rec["h"] = _norm_matmul(xc, a_norm_f[i][None], W["a_w_in", i], BF16, f"a{i}_in", deps=order)
            order = relay(2 * layer + 1, rec["h"])
            rec["uv"] = _sgu_fwd(rec["zpre"], a_sgu_f[i][None], a_w_spatial[i], a_b_spatial[i].T, f"a{i}_sgu",
                                 deps=order)
            xm = _matmul_res(rec["uv"], W["a_w_out", i], xc, f"a{i}_out", tm=512)
        else:
            i = layer - n_a
            if i == 0:
                kv, h_kv = _norm_matmul(xc, kv_norm[None], W["w_kv", 0], BF16, "kv_proj")
                kvp = jnp.pad(kv, ((KV_PAD, 0), (0, 0)))
                x_kv = xc
            rec["q"], rec["h"] = _norm_matmul(xc, b_norm[i][None], W["b_w_q", i], BF16, f"b{i}_q", row_sharded=True)
            order = relay(2 * layer + 1, rec["h"])
            table = jnp.pad(b_rel_bias[i], ((0, 0), (0, TABLE_PAD - b_rel_bias.shape[2])))
            rec["bias"] = _rel_bias_fwd(table, f"b{i}_bias").transpose(1, 0, 2)
            rec["o"] = _attn_fwd(rec["q"], kvp, rec["bias"], f"b{i}_attn", deps=order)
            xm = _matmul_res(rec["o"], W["b_w_o", i], xc, f"b{i}_o", tm=512)
        rec["x_mid"] = xm
        gathered(2 * layer + 1, xm)
        rec["gu"], rec["h_f"] = _norm_matmul(xm, ffn_norm[layer][None], W["ffn_w_gate_up", layer], BF16, f"f{layer}_in")
        order = relay(2 * layer + 2, rec["h_f"])
        xc = _matmul_res(rec["gu"], W["ffn_w_down", layer], xm, f"f{layer}_out", swiglu=True, deps=order)
        order = ()
        saved.append(rec)

    loss_tile, dx, dxb, d_final = _loss_head(xc, final_norm[None], loss_target.reshape(S, D), "loss_head")
    loss = lax.psum(loss_tile[0, 0], ("x", "y", "c"))

    started = []
    small_started = []
    pending = []

    held = []

    def weight_grad(unit, hold=False, **kw):
        full = (N_CHIPS,) + tuple(stacked[unit[0]].shape[1:])
        g = _matmul_tn(out_shape=full, name=f"d_{unit[0]}_{unit[1]}", deps=tuple(pending), **kw)
        pending.clear()
        held.append((unit, g))
        if hold:
            return ()
        send, recv, flying_g, flying_land, token = _reduce_start([g for _, g in held],
                                                                 f"reduce_start_{unit[0]}_{unit[1]}")
        started.append(([u for u, _ in held], send, recv, flying_g, flying_land))
        held.clear()
        return (token,)

    tt = 512
    tw = min(2048, S)
    tb = min(NORMBWD_ROWS, S)
    row_a = lambda w, rows=tw: pl.BlockSpec((rows, w), lambda o, t: (t, 0))
    d_ffn_norm, d_b_norm, d_a_norm, d_a_sgu = [None] * depth, [None] * n_b, [None] * n_a, [None] * n_a
    d_ws, d_bs, d_rel = [None] * n_a, [None] * n_a, [None] * n_b
    dkv = None
    first = lambda ref: ref[...]
    for layer in reversed(range(depth)):
        rec = saved[layer]
        r_d = ffn_w_down.shape[1]
        half_f = 2 * r_d
        token = weight_grad(
            ("ffn_w_down", layer), a_ops=[rec["gu"], rec["gu"]],
            a_specs=[pl.BlockSpec((tt, half_f), lambda o, t: (t, o)), pl.BlockSpec((tt, half_f), lambda o, t: (t, 2 + o))],
            a_fn=lambda g_ref, u_ref: _swiglu(g_ref[...].astype(F32), u_ref[...].astype(F32)).astype(BF16),
            b_op=dxb, b_spec=row_a(D, tt), out_spec=pl.BlockSpec((2, r_d, D), lambda o, t: (o, 0, 0)),
            acc_shape=(half_f, D), n_outer=2, tt=tt, hold=True)
        dgu = _nt_swiglu_bwd(dxb, W["ffn_w_down", layer], rec["gu"], f"f{layer}_dgu", deps=token)
        nsh = ffn_w_gate_up.shape[2]
        token = weight_grad(
            ("ffn_w_gate_up", layer), a_ops=[rec["h_f"]], a_specs=[row_a(D)], a_fn=first,
            b_op=dgu, b_spec=pl.BlockSpec((None, tw, nsh), lambda o, t: (o // 2, t, o % 2)),
            out_spec=pl.BlockSpec((None, D, nsh), lambda o, t: (o, 0, 0)), acc_shape=(D, nsh), n_outer=N_CHIPS, tt=tw)
        dx, dxb, d_ffn_norm[layer] = _nt_normbwd(
            dgu, pl.BlockSpec((None, tb, nsh), lambda i, k: (k // 2, i, k % 2)),
            W["ffn_w_gate_up", layer], pl.BlockSpec((None, D, nsh), lambda i, k: (k, 0, 0)),
            (D, nsh), N_CHIPS, rec["x_mid"], ffn_norm[layer][None], dx, f"f{layer}_dx", deps=token)
        if layer >= n_a:
            i = layer - n_a
            r_o = b_w_o.shape[1]
            token = weight_grad(
                ("b_w_o", i), a_ops=[rec["o"]], a_specs=[row_a(D)], a_fn=first, b_op=dxb, b_spec=row_a(D),
                out_spec=pl.BlockSpec((N_CHIPS, r_o, D), lambda o, t: (0, 0, 0)), acc_shape=(D, D), n_outer=1, tt=tw,
                hold=True)
            do = _nt_rows(dxb, W["b_w_o", i], N_CHIPS, BF16, f"b{i}_do", deps=token)
            dq, dkv, dbias = _attn_bwd(rec["q"], kvp, rec["bias"], do, dkv, f"b{i}_attn_bwd")
            d_rel[i] = _rel_bias_bwd(dbias.transpose(1, 0, 2), f"b{i}_dbias")[:, :b_rel_bias.shape[2]]
            token = weight_grad(
                ("b_w_q", i), a_ops=[rec["h"]], a_specs=[row_a(D)], a_fn=first, b_op=dq, b_spec=row_a(D),
                out_spec=pl.BlockSpec((N_CHIPS, r_o, D), lambda o, t: (0, 0, 0)), acc_shape=(D, D), n_outer=1, tt=tw)
            dx, dxb, d_b_norm[i] = _nt_normbwd(
                dq, pl.BlockSpec((tb, D), lambda i_, k: (i_, 0)),
                W["b_w_q", i], pl.BlockSpec((N_CHIPS, r_o, D), lambda i_, k: (0, 0, 0)),
                (D, D), 1, rec["x_in"], b_norm[i][None], dx, f"b{i}_dx", deps=token)
            if i == 0:
                dkv_b = dkv[:, KV_PAD:, :].astype(BF16)
                n_kv = w_kv.shape[1]
                token = weight_grad(
                    ("w_kv", 0), a_ops=[h_kv], a_specs=[row_a(D)], a_fn=first,
                    b_op=dkv_b, b_spec=pl.BlockSpec((None, tw, n_kv), lambda o, t: (o // 2, t, o % 2)),
                    out_spec=pl.BlockSpec((None, D, n_kv), lambda o, t: (o, 0, 0)), acc_shape=(D, n_kv),
                    n_outer=N_CHIPS, tt=tw)
                dx, dxb, d_kv_norm = _nt_normbwd(
                    dkv_b, pl.BlockSpec((None, tb, n_kv), lambda i_, k: (k // 2, i_, k % 2)),
                    W["w_kv", 0], pl.BlockSpec((None, D, n_kv), lambda i_, k: (k, 0, 0)),
                    (D, n_kv), N_CHIPS, x_kv, kv_norm[None], dx, "kv_dx", deps=token)
        else:
            i = layer
            r_w = a_w_out.shape[1]
            token = weight_grad(
                ("a_w_out", i), a_ops=[rec["uv"]], a_specs=[row_a(N_CHIPS * r_w, tw // 2)], a_fn=first,
                b_op=dxb, b_spec=row_a(D, tw // 2), out_spec=pl.BlockSpec((N_CHIPS, r_w, D), lambda o, t: (0, 0, 0)),
                acc_shape=(N_CHIPS * r_w, D), n_outer=1, tt=tw // 2, hold=i > 0)
            duv = _nt_rows(dxb, W["a_w_out", i], 2, BF16, f"a{i}_duv", deps=token)
            dz, d_a_sgu[i], d_ws[i], dbs = _sgu_bwd(rec["zpre"], duv, a_sgu_f[i][None], a_w_spatial[i],
                                                  a_w_spatial[i].transpose(0, 2, 1), a_b_spatial[i].T, f"a{i}_sgu_bwd")
            d_bs[i] = dbs[:, :, 0]
            if i == 0:
                batch = [d_a_sgu[0], d_ws[0][None], d_bs[0][None], d_ffn_norm[0]]
                small_started.append((batch, _gather_start(_pack_small(batch), "gather_start_late")))
                pending.append(small_started[-1][1][4])
            n_in = a_w_in.shape[2]
            token = weight_grad(
                ("a_w_in", i), a_ops=[rec["h"]], a_specs=[row_a(D)], a_fn=first,
                b_op=dz, b_spec=pl.BlockSpec((tw, n_in), lambda o, t: (t, o)),
                out_spec=pl.BlockSpec((None, D, n_in), lambda o, t: (o, 0, 0)), acc_shape=(D, n_in), n_outer=N_CHIPS,
                tt=tw)
            dx, dxb, d_a_norm[i] = _nt_normbwd(
                dz, pl.BlockSpec((tb, n_in), lambda i_, k: (i_, k)),
                W["a_w_in", i], pl.BlockSpec((None, D, n_in), lambda i_, k: (k, 0, 0)),
                (D, n_in), N_CHIPS, rec["x_in"], a_norm_f[i][None], dx, f"a{i}_dx", deps=token)
        if layer == 1:
            batch = [jnp.concatenate(d_a_norm[1:], axis=0), jnp.concatenate(d_a_sgu[1:], axis=0), jnp.stack(d_ws[1:]),
                     jnp.stack(d_bs[1:]), d_kv_norm, jnp.concatenate(d_b_norm, axis=0), jnp.stack(d_rel),
                     jnp.concatenate(d_ffn_norm[1:], axis=0), d_final]
            small_started.append((batch, _gather_start(_pack_small(batch), "gather_start_early")))
            pending.append(small_started[-1][1][4])
    grad_x = dx.reshape(x.shape)

    landed = _reduce_wait([(send, recv, g, land) for _, send, recv, g, land in started], dx, "reduce_wait")
    reduced_units = [unit for units_, *_ in started for unit in units_]
    halves = [_reduce_sum(g, land, place, f"reduce_sum_{unit[0]}_{unit[1]}")
              for unit, (g, land) in zip(reduced_units, [pair for gs, lands in landed for pair in zip(gs, lands)])]
    joined = dict(zip(reduced_units, _sibling_join(halves, "reduce_join")))
    reduced = {}

    (packed_e, land_e), (packed_l, land_l) = _gather_wait([s[:4] for _, s in small_started], dx, "gather_wait")
    total_e = _sum_gathered(packed_e, land_e, place[2:], "sum_small_grads_early")
    total_l = _sum_gathered(packed_l, land_l, place[2:], "sum_small_grads_late")
    total_t = _sum_devices(_gather_small(_pack_small([d_a_norm[0]]), "gather_small_grads_last"), "sum_small_grads_last")
    (e_a_norm, e_a_sgu, e_ws, e_bs, g_kv_norm, g_b_norm, g_rel, e_ffn_norm, g_final) = _unpack_small(
        total_e, [a.shape for a in small_started[0][0]])
    l_a_sgu, l_ws, l_bs, l_ffn_norm = _unpack_small(total_l, [a.shape for a in small_started[1][0]])
    (t_a_norm,) = _unpack_small(total_t, [d_a_norm[0].shape])
    g_a_norm = jnp.concatenate([t_a_norm, e_a_norm], axis=0)
    g_a_sgu = jnp.concatenate([l_a_sgu, e_a_sgu], axis=0)
    g_ws = jnp.concatenate([l_ws, e_ws], axis=0)
    g_bs = jnp.concatenate([l_bs, e_bs], axis=0)
    g_ffn_norm = jnp.concatenate([l_ffn_norm, e_ffn_norm], axis=0)
    reduced["a_norm"] = lax.dynamic_slice_in_dim(g_a_norm, chip * na_w, na_w, axis=1)
    reduced["a_sgu_norm"] = lax.dynamic_slice_in_dim(g_a_sgu, chip * ns_w, ns_w, axis=1)
    reduced.update(a_w_spatial=g_ws, a_b_spatial=g_bs, kv_norm=g_kv_norm.reshape(kv_norm.shape), b_norm=g_b_norm,
                   b_rel_bias=g_rel, ffn_norm=g_ffn_norm, final_norm=g_final.reshape(final_norm.shape))

    weights = dict(a_norm=a_norm, a_w_in=a_w_in, a_sgu_norm=a_sgu_norm, a_w_spatial=a_w_spatial,
                   a_b_spatial=a_b_spatial, a_w_out=a_w_out, kv_norm=kv_norm, w_kv=w_kv, b_norm=b_norm, b_w_q=b_w_q,
                   b_rel_bias=b_rel_bias, b_w_o=b_w_o, ffn_norm=ffn_norm, ffn_w_gate_up=ffn_w_gate_up,
                   ffn_w_down=ffn_w_down, final_norm=final_norm)
    m_in = dict(a_norm=m_a_norm, a_w_in=m_a_w_in, a_sgu_norm=m_a_sgu_norm, a_w_spatial=m_a_w_spatial,
                a_b_spatial=m_a_b_spatial, a_w_out=m_a_w_out, kv_norm=m_kv_norm, w_kv=m_w_kv, b_norm=m_b_norm,
                b_w_q=m_b_w_q, b_rel_bias=m_b_rel_bias, b_w_o=m_b_w_o, ffn_norm=m_ffn_norm,
                ffn_w_gate_up=m_ffn_w_gate_up, ffn_w_down=m_ffn_w_down, final_norm=m_final_norm)
    v_in = dict(a_norm=v_a_norm, a_w_in=v_a_w_in, a_sgu_norm=v_a_sgu_norm, a_w_spatial=v_a_w_spatial,
                a_b_spatial=v_a_b_spatial, a_w_out=v_a_w_out, kv_norm=v_kv_norm, w_kv=v_w_kv, b_norm=v_b_norm,
                b_w_q=v_b_w_q, b_rel_bias=v_b_rel_bias, b_w_o=v_b_w_o, ffn_norm=v_ffn_norm,
                ffn_w_gate_up=v_ffn_w_gate_up, ffn_w_down=v_ffn_w_down, final_norm=v_final_norm)
    grad_out, delta_out, m_out, v_out = [], [], [], []
    for key, w in weights.items():
        if key in stacked:
            as_layers = lambda a: a.reshape(stacked[key].shape)
            g, d, nm, nv = _adamw_stacked(
                as_layers(w), [joined[key, layer] for layer in range(stacked[key].shape[0])], as_layers(m_in[key]),
                as_layers(v_in[key]), "adamw_" + key)
        else:
            g = reduced[key].reshape(w.shape)
            view = (1, w.shape[0]) if w.ndim == 1 else (-1, w.shape[-1])
            d, nm, nv = _adamw(w.reshape(view), g.reshape(view), m_in[key].reshape(view), v_in[key].reshape(view),
                               "adamw_" + key)
        grad_out.append(g.reshape(w.shape))
        delta_out.append(d.reshape(w.shape))
        m_out.append(nm.reshape(w.shape))
        v_out.append(nv.reshape(w.shape))
    return (loss, grad_x, *grad_out, *delta_out, *m_out, *v_out)
```

```python
import math

import jax
import jax.numpy as jnp
from jax import lax
from jax.experimental import pallas as pl
from jax.experimental.pallas import tpu as pltpu

F32, BF16 = jnp.float32, jnp.bfloat16
MESH = pl.DeviceIdType.MESH
HIGHEST = lax.Precision.HIGHEST
NT_DIMS = (((1,), (1,)), ((), ()))
TN_DIMS = (((0,), (0,)), ((), ()))

EPS = 1e-6
CHUNK = 64
A_CHUNK = 128
A_GROUPS = 8
N_HEADS = 16
HEAD_DIM = 64
N_LEFT = 8
MAX_REL = 256
ATTN_SCALE = HEAD_DIM ** -0.5
NEG_INF = -1e30
Q_BLOCK = 2 * CHUNK
KV_PAD = N_LEFT * CHUNK
BAND = KV_PAD + Q_BLOCK
DIAGS = BAND + Q_BLOCK
TABLE_PAD = 640
HEADS_PER_BLOCK = 2
BLOCKS_PER_STEP = 4
NORMBWD_ROWS = 1024

ADAM_LR, ADAM_B1, ADAM_B2, ADAM_EPS, ADAM_WD, ADAM_STEP = 0.001, 0.9, 0.999, 1e-08, 0.01, 10

VMEM_LIMIT_BYTES = 56 * 1024 * 1024
N_CHIPS = 4
N_DEV = 8
SMALL_COLS = 1024


def _cparams(n_grid):
    return pltpu.CompilerParams(dimension_semantics=("arbitrary",) * n_grid, vmem_limit_bytes=VMEM_LIMIT_BYTES)


def _sds(shape, dtype):
    return jax.ShapeDtypeStruct(tuple(shape), dtype)


def _gelu(x):
    return x * (0.5 * (1.0 + lax.erf(x * math.sqrt(0.5))))


def _gelu_and_grad(x):
    cdf = 0.5 * (1.0 + lax.erf(x * math.sqrt(0.5)))
    return x * cdf, cdf + x * (jnp.exp(-0.5 * x * x) * (1.0 / math.sqrt(2.0 * math.pi)))


def _rms_hat(xv):
    r = lax.rsqrt(jnp.mean(xv * xv, axis=-1, keepdims=True) + EPS)
    return xv * r, r


def _rms_bwd(xhat, r, g, dy):
    dxhat = dy * g
    dx = r * (dxhat - xhat * jnp.mean(dxhat * xhat, axis=-1, keepdims=True))
    return dx, dy * xhat


def _swiglu(gate, up):
    return (gate * jax.nn.sigmoid(gate)) * up


def _row_tile(rows, cols, itemsize, cap_bytes, align):
    t = rows
    while t * cols * itemsize > cap_bytes and t % (2 * align) == 0:
        t //= 2
    return t


def _cast_slab(w, layer, chip, name, deps=()):
    _, r, C = w.shape
    tr = _row_tile(r, C, 4, 4 * 1024 * 1024, 16)

    def body(chip_ref, w_ref, *rest):
        del chip_ref
        rest[-1][...] = w_ref[...].astype(BF16)

    grid_spec = pltpu.PrefetchScalarGridSpec(
        num_scalar_prefetch=1, grid=(r // tr,),
        in_specs=[pl.BlockSpec((None, tr, C), lambda i, chip_ref: (layer, i, 0))] + [HBM_SPEC] * len(deps),
        out_specs=pl.BlockSpec((None, tr, C), lambda i, chip_ref: (chip_ref[0], i, 0)))
    return pl.pallas_call(body, name=name, grid_spec=grid_spec, out_shape=_sds((N_CHIPS, r, C), BF16),
                          compiler_params=_cparams(1))(chip, w, *deps)


def _adamw_stacked(w, gs, m, v, name):
    L, r, C = w.shape
    tr = _row_tile(r, C, 4, 2 * 1024 * 1024, 8)
    nb = r // tr

    def body(w_ref, m_ref, v_ref, *rest):
        go_ref, d_ref, nm_ref, nv_ref = rest[-4:]
        layer = pl.program_id(0)
        gv = rest[0][...]
        for k in range(1, L):
            gv = jnp.where(layer == k, rest[k][...], gv)
        mn = ADAM_B1 * m_ref[...] + (1.0 - ADAM_B1) * gv
        vn = ADAM_B2 * v_ref[...] + (1.0 - ADAM_B2) * jnp.square(gv)
        m_hat = mn / (1.0 - ADAM_B1 ** ADAM_STEP)
        v_hat = vn / (1.0 - ADAM_B2 ** ADAM_STEP)
        d_ref[...] = -ADAM_LR * (m_hat / (jnp.sqrt(v_hat) + ADAM_EPS) + ADAM_WD * w_ref[...])
        nm_ref[...] = mn
        nv_ref[...] = vn
        go_ref[...] = gv

    def grad_spec(k):
        return pl.BlockSpec((tr, C), lambda l, i: (jnp.where(l == k, i, jnp.where(l > k, nb - 1, 0)), 0))

    stacked = pl.BlockSpec((None, tr, C), lambda l, i: (l, i, 0))
    return pl.pallas_call(body, name=name, grid=(L, nb), in_specs=[stacked] * 3 + [grad_spec(k) for k in range(L)],
                          out_specs=[stacked] * 4, out_shape=[_sds((L, r, C), F32)] * 4,
                          compiler_params=_cparams(2))(w, m, v, *gs)


def _adamw(w, g, m, v, name):
    R, C = w.shape
    tr = _row_tile(R, C, 4, 1024 * 1024, 8)

    def body(w_ref, g_ref, m_ref, v_ref, d_ref, nm_ref, nv_ref):
        gv = g_ref[...]
        mn = ADAM_B1 * m_ref[...] + (1.0 - ADAM_B1) * gv
        vn = ADAM_B2 * v_ref[...] + (1.0 - ADAM_B2) * jnp.square(gv)
        m_hat = mn / (1.0 - ADAM_B1 ** ADAM_STEP)
        v_hat = vn / (1.0 - ADAM_B2 ** ADAM_STEP)
        d_ref[...] = -ADAM_LR * (m_hat / (jnp.sqrt(v_hat) + ADAM_EPS) + ADAM_WD * w_ref[...])
        nm_ref[...] = mn
        nv_ref[...] = vn

    spec = pl.BlockSpec((tr, C), lambda i: (i, 0))
    return pl.pallas_call(body, name=name, grid=(R // tr,), in_specs=[spec] * 4, out_specs=[spec] * 3,
                          out_shape=[_sds((R, C), F32)] * 3, compiler_params=_cparams(1))(w, g, m, v)


def _norm_matmul(x, g, w_g, out_dtype, name, row_sharded=False, deps=(), tm=1024, zero_rows=0):
    S, D = x.shape
    tm = min(tm, S)
    lead = zero_rows // tm
    if row_sharded:
        r, N = w_g.shape[1], w_g.shape[2]
        tn = 512
        w_spec = pl.BlockSpec((N_CHIPS, r, tn), lambda i, j: (0, 0, j))
    else:
        nsh = w_g.shape[2]
        N = N_CHIPS * nsh
        tn = next((t for t in (1024, 512) if nsh % t == 0), nsh)
        bps = nsh // tn
        w_spec = pl.BlockSpec((None, D, tn), lambda i, j: (j // bps, 0, j % bps))

    def body(x_ref, g_ref, w_ref, *rest):
        y_ref, h_ref = rest[-2:]
        i = pl.program_id(0)

        @pl.when((i >= lead) & (pl.program_id(1) == 0))
        def _():
            xhat, _ = _rms_hat(x_ref[...])
            h_ref[...] = (xhat * g_ref[...]).astype(BF16)

        @pl.when(i >= lead)
        def _():
            w = w_ref[...].reshape(D, tn)
            y_ref[...] = jnp.dot(h_ref[...], w, preferred_element_type=F32).astype(y_ref.dtype)

        if lead:
            @pl.when(i < lead)
            def _():
                y_ref[...] = jnp.zeros_like(y_ref)

    rows = lambda i, j: (jnp.maximum(i - lead, 0), 0)
    return pl.pallas_call(
        body, name=name, grid=(lead + S // tm, N // tn),
        in_specs=[pl.BlockSpec((tm, D), rows), pl.BlockSpec((1, D), lambda i, j: (0, 0)), w_spec]
        + [HBM_SPEC] * len(deps),
        out_specs=[pl.BlockSpec((tm, tn), lambda i, j: (i, j)), pl.BlockSpec((tm, D), rows)],
        out_shape=[_sds((zero_rows + S, N), out_dtype), _sds((S, D), BF16)],
        compiler_params=_cparams(2))(x, g, w_g, *deps)


def _matmul_res(a, w_g, res, name, swiglu=False, deps=(), tm=256):
    S, N = res.shape
    r = w_g.shape[1]
    K = N_CHIPS * r

    def body(*refs):
        o_ref = refs[-1]
        if swiglu:
            gate_ref, up_ref, w_ref, res_ref = refs[:4]
            a_blk = _swiglu(gate_ref[...].astype(F32), up_ref[...].astype(F32)).astype(BF16)
        else:
            a_ref, w_ref, res_ref = refs[:3]
            a_blk = a_ref[...]
        o_ref[...] = res_ref[...] + jnp.dot(a_blk, w_ref[...].reshape(K, N), preferred_element_type=F32)

    a_specs, a_ops = [pl.BlockSpec((tm, K), lambda i: (i, 0))], [a]
    if swiglu:
        a_specs.append(pl.BlockSpec((tm, K), lambda i: (i, 1)))
        a_ops.append(a)
    row = pl.BlockSpec((tm, N), lambda i: (i, 0))
    return pl.pallas_call(
        body, name=name, grid=(S // tm,),
        in_specs=a_specs + [pl.BlockSpec((N_CHIPS, r, N), lambda i: (0, 0, 0)), row] + [HBM_SPEC] * len(deps),
        out_specs=row, out_shape=_sds((S, N), F32), compiler_params=_cparams(1))(*a_ops, w_g, res, *deps)


def _matmul_tn(a_ops, a_specs, a_fn, b_op, b_spec, out_spec, out_shape, acc_shape, n_outer, name, deps=(), tt=512):
    S = b_op.shape[-2]
    na = len(a_ops)
    nt = S // tt

    def body(*refs):
        a_refs, b_ref, o_ref, acc_ref = refs[:na], refs[na], refs[-2], refs[-1]
        t = pl.program_id(1)
        part = lax.dot_general(a_fn(*a_refs), b_ref[...].astype(BF16), TN_DIMS, preferred_element_type=F32)

        @pl.when(t == 0)
        def _():
            acc_ref[...] = part

        @pl.when(t > 0)
        def _():
            acc_ref[...] += part

        @pl.when(t == nt - 1)
        def _():
            o_ref[...] = acc_ref[...].reshape(o_ref.shape).astype(BF16)

    return pl.pallas_call(
        body, name=name, grid=(n_outer, nt), in_specs=list(a_specs) + [b_spec] + [HBM_SPEC] * len(deps),
        out_specs=out_spec, out_shape=_sds(out_shape, BF16), scratch_shapes=[pltpu.VMEM(acc_shape, F32)],
        compiler_params=_cparams(2))(*a_ops, b_op, *deps)


def _nt_accumulate(a_ref, w_ref, acc_ref, w2d, nk):
    k = pl.program_id(1)
    part = lax.dot_general(a_ref[...].astype(BF16), w_ref[...].reshape(w2d), NT_DIMS, preferred_element_type=F32)

    @pl.when(k == 0)
    def _():
        acc_ref[...] = part

    @pl.when(k > 0)
    def _():
        acc_ref[...] += part

    return k == nk - 1


def _nt_normbwd(dy, dy_spec, w_g, w_spec, w2d, nk, x, g, dres, name, deps=(), tm=NORMBWD_ROWS):
    S, D = x.shape
    tm = min(tm, S)

    def body(dy_ref, w_ref, x_ref, g_ref, dres_ref, *rest):
        dx_ref, dxb_ref, dg_ref, acc_ref = rest[-4:]

        @pl.when((pl.program_id(0) == 0) & (pl.program_id(1) == 0))
        def _():
            dg_ref[...] = jnp.zeros_like(dg_ref)

        last = _nt_accumulate(dy_ref, w_ref, acc_ref, w2d, nk)

        @pl.when(last)
        def _():
            xhat, r = _rms_hat(x_ref[...])
            dx, dgp = _rms_bwd(xhat, r, g_ref[...], acc_ref[...])
            total = dres_ref[...] + dx
            dx_ref[...] = total
            dxb_ref[...] = total.astype(BF16)
            dg_ref[...] += jnp.sum(dgp, axis=0, keepdims=True)

    row = pl.BlockSpec((tm, D), lambda i, k: (i, 0))
    vec = pl.BlockSpec((1, D), lambda i, k: (0, 0))
    return pl.pallas_call(
        body, name=name, grid=(S // tm, nk),
        in_specs=[dy_spec, w_spec, row, vec, row] + [HBM_SPEC] * len(deps), out_specs=[row, row, vec],
        out_shape=[_sds((S, D), F32), _sds((S, D), BF16), _sds((1, D), F32)],
        scratch_shapes=[pltpu.VMEM((tm, D), F32)], compiler_params=_cparams(2))(dy, w_g, x, g, dres, *deps)


def _nt_rows(dy, w_g, shards_per_block, out_dtype, name, deps=(), tm=1024):
    S, N = dy.shape
    tm = min(tm, S)
    r = w_g.shape[1]
    tn = shards_per_block * r

    def body(dy_ref, w_ref, *rest):
        o_ref = rest[-1]
        o_ref[...] = lax.dot_general(dy_ref[...].astype(BF16), w_ref[...].reshape(tn, N), NT_DIMS,
                                     preferred_element_type=F32).astype(o_ref.dtype)

    return pl.pallas_call(
        body, name=name, grid=(S // tm, N_CHIPS // shards_per_block),
        in_specs=[pl.BlockSpec((tm, N), lambda i, j: (i, 0)),
                  pl.BlockSpec((shards_per_block, r, N), lambda i, j: (j, 0, 0))] + [HBM_SPEC] * len(deps),
        out_specs=pl.BlockSpec((tm, tn), lambda i, j: (i, j)),
        out_shape=_sds((S, N_CHIPS * r), out_dtype), compiler_params=_cparams(2))(dy, w_g, *deps)


def _nt_swiglu_bwd(dy, w_g, gu, name, deps=(), tm=1024):
    S, N = dy.shape
    tm = min(tm, S)
    r = w_g.shape[1]
    tn = 2 * r
    F = N_CHIPS * r

    def body(dy_ref, w_ref, gate_ref, up_ref, *rest):
        o_ref = rest[-1]
        dact = lax.dot_general(dy_ref[...].astype(BF16), w_ref[...].reshape(tn, N), NT_DIMS,
                               preferred_element_type=F32)
        gate, up = gate_ref[...].astype(F32), up_ref[...].astype(F32)
        sg = jax.nn.sigmoid(gate)
        silu = gate * sg
        o_ref[0] = ((dact * up) * (sg + silu * (1.0 - sg))).astype(BF16)
        o_ref[1] = (dact * silu).astype(BF16)

    return pl.pallas_call(
        body, name=name, grid=(2, S // tm),
        in_specs=[pl.BlockSpec((tm, N), lambda j, i: (i, 0)),
                  pl.BlockSpec((2, r, N), lambda j, i: (j, 0, 0)),
                  pl.BlockSpec((tm, tn), lambda j, i: (i, j)),
                  pl.BlockSpec((tm, tn), lambda j, i: (i, 2 + j))] + [HBM_SPEC] * len(deps),
        out_specs=pl.BlockSpec((2, tm, tn), lambda j, i: (0, i, j)),
        out_shape=_sds((2, S, F), BF16), compiler_params=_cparams(2))(dy, w_g, gu, gu, *deps)


def _chunk_causal_mask(transposed):
    i = lax.broadcasted_iota(jnp.int32, (A_CHUNK, A_CHUNK), 0) // CHUNK
    j = lax.broadcasted_iota(jnp.int32, (A_CHUNK, A_CHUNK), 1) // CHUNK
    return ((i <= j) if transposed else (i >= j)).astype(F32)


def _sgu_fwd(zpre, g_sgu, ws, bs_t, name, deps=()):
    S, F2 = zpre.shape
    F = F2 // 2
    gd = F // A_GROUPS

    windows = 2
    rows = windows * A_CHUNK

    def body(zu_ref, zv_ref, g_ref, ws_ref, b_ref, *rest):
        o_ref = rest[-1]
        vhat, _ = _rms_hat(_gelu(zv_ref[...].astype(F32)))
        vn = (vhat * g_ref[...]).astype(BF16)
        u = _gelu(zu_ref[...].astype(F32))
        mask = _chunk_causal_mask(False)
        for gi in range(A_GROUPS):
            sl = slice(gi * gd, (gi + 1) * gd)
            wm = (ws_ref[gi] * mask).astype(BF16)
            for w in range(windows):
                win = slice(w * A_CHUNK, (w + 1) * A_CHUNK)
                vs = jnp.dot(wm, vn[win, sl], preferred_element_type=F32) + b_ref[:, gi:gi + 1]
                o_ref[win, sl] = (u[win, sl] * vs).astype(BF16)

    return pl.pallas_call(
        body, name=name, grid=(S // rows,),
        in_specs=[pl.BlockSpec((rows, F), lambda i: (i, 0)),
                  pl.BlockSpec((rows, F), lambda i: (i, 1)),
                  pl.BlockSpec((1, F), lambda i: (0, 0)),
                  pl.BlockSpec((A_GROUPS, A_CHUNK, A_CHUNK), lambda i: (0, 0, 0)),
                  pl.BlockSpec((A_CHUNK, A_GROUPS), lambda i: (0, 0))] + [HBM_SPEC] * len(deps),
        out_specs=pl.BlockSpec((rows, F), lambda i: (i, 0)),
        out_shape=_sds((S, F), BF16), compiler_params=_cparams(1))(zpre, zpre, g_sgu, ws, bs_t, *deps)


def _sgu_bwd(zpre, duv, g_sgu, ws, ws_t, bs_t, name):
    S, F2 = zpre.shape
    F = F2 // 2
    gd = F // A_GROUPS

    def body(zu_ref, zv_ref, duv_ref, g_ref, ws_ref, wst_ref, b_ref, dz_ref, dg_ref, dws_ref, dbs_ref, dvn_ref):
        @pl.when(pl.program_id(0) == 0)
        def _():
            dg_ref[...] = jnp.zeros_like(dg_ref)
            dws_ref[...] = jnp.zeros_like(dws_ref)
            dbs_ref[...] = jnp.zeros_like(dbs_ref)

        gv = g_ref[...]
        u, u_grad = _gelu_and_grad(zu_ref[...].astype(F32))
        v, v_grad = _gelu_and_grad(zv_ref[...].astype(F32))
        vhat, r = _rms_hat(v)
        vn = (vhat * gv).astype(BF16)
        duv_v = duv_ref[...].astype(F32)
        dvs = duv_v * u
        dvs_b = dvs.astype(BF16)
        mask = _chunk_causal_mask(False)
        mask_t = _chunk_causal_mask(True)
        for gi in range(A_GROUPS):
            sl = slice(gi * gd, (gi + 1) * gd)
            wm = (ws_ref[gi] * mask).astype(BF16)
            vs = jnp.dot(wm, vn[:, sl], preferred_element_type=F32) + b_ref[:, gi:gi + 1]
            dz_ref[:, sl] = ((duv_v[:, sl] * vs) * u_grad[:, sl]).astype(BF16)
            dws_ref[gi] += lax.dot_general(dvs_b[:, sl], vn[:, sl], NT_DIMS, preferred_element_type=F32) * mask
            dbs_ref[gi] += jnp.broadcast_to(jnp.sum(dvs[:, sl], axis=1, keepdims=True), (A_CHUNK, A_CHUNK))
            wm_t = (wst_ref[gi] * mask_t).astype(BF16)
            dvn_ref[:, sl] = jnp.dot(wm_t, dvs_b[:, sl], preferred_element_type=F32)
        dv, dg_part = _rms_bwd(vhat, r, gv, dvn_ref[...])
        dg_ref[...] += jnp.sum(dg_part, axis=0, keepdims=True)
        dz_ref[:, F:] = (dv * v_grad).astype(BF16)

    blk = pl.BlockSpec((A_CHUNK, F), lambda i: (i, 0))
    const3 = pl.BlockSpec((A_GROUPS, A_CHUNK, A_CHUNK), lambda i: (0, 0, 0))
    return pl.pallas_call(
        body, name=name, grid=(S // A_CHUNK,),
        in_specs=[blk, pl.BlockSpec((A_CHUNK, F), lambda i: (i, 1)), blk,
                  pl.BlockSpec((1, F), lambda i: (0, 0)), const3, const3,
                  pl.BlockSpec((A_CHUNK, A_GROUPS), lambda i: (0, 0))],
        out_specs=[pl.BlockSpec((A_CHUNK, F2), lambda i: (i, 0)), pl.BlockSpec((1, F), lambda i: (0, 0)),
                   const3, const3],
        out_shape=[_sds((S, F2), BF16), _sds((1, F), F32), _sds((A_GROUPS, A_CHUNK, A_CHUNK), F32),
                   _sds((A_GROUPS, A_CHUNK, A_CHUNK), F32)],
        scratch_shapes=[pltpu.VMEM((A_CHUNK, F), F32)],
        compiler_params=_cparams(1))(zpre, zpre, duv, g_sgu, ws, ws_t, bs_t)


def _toeplitz_one_hot():
    row = lax.broadcasted_iota(jnp.int32, (TABLE_PAD, DIAGS), 0)
    j = lax.broadcasted_iota(jnp.int32, (TABLE_PAD, DIAGS), 1)
    idx = jnp.clip(KV_PAD + Q_BLOCK - j, -MAX_REL, MAX_REL) + MAX_REL
    return (row == idx).astype(F32)


def _rel_bias_fwd(table, name):
    H = table.shape[0]

    def body(t_ref, o_ref):
        diag = jnp.dot(t_ref[...], _toeplitz_one_hot(), precision=HIGHEST, preferred_element_type=F32)
        k_chunk = lax.broadcasted_iota(jnp.int32, (1, BAND), 1) // CHUNK

        def step(r, carry):
            q_chunk = r // CHUNK
            seen = (k_chunk >= q_chunk) & (k_chunk <= q_chunk + N_LEFT)
            o_ref[r] = pltpu.roll(diag, DIAGS - Q_BLOCK + r, 1)[:, :BAND] + jnp.where(seen, 0.0, NEG_INF)
            return carry

        lax.fori_loop(0, Q_BLOCK, step, 0)

    return pl.pallas_call(body, name=name, out_shape=_sds((Q_BLOCK, H, BAND), F32),
                          compiler_params=pltpu.CompilerParams(vmem_limit_bytes=VMEM_LIMIT_BYTES))(table)


def _rel_bias_bwd(dbias, name):
    H = dbias.shape[1]

    def body(d_ref, o_ref):
        def step(r, acc):
            row = jnp.concatenate([d_ref[r], jnp.zeros((H, DIAGS - BAND), F32)], axis=1)
            return acc + pltpu.roll(row, Q_BLOCK - r, 1)

        diag = lax.fori_loop(0, Q_BLOCK, step, jnp.zeros((H, DIAGS), F32))
        o_ref[...] = lax.dot_general(diag, _toeplitz_one_hot(), NT_DIMS, precision=HIGHEST,
                                     preferred_element_type=F32)

    return pl.pallas_call(body, name=name, out_shape=_sds((H, TABLE_PAD), F32),
                          compiler_params=pltpu.CompilerParams(vmem_limit_bytes=VMEM_LIMIT_BYTES))(dbias)


def _head_rows(t):
    lane = lax.broadcasted_iota(jnp.int32, t.shape, 1)
    zero = jnp.zeros_like(t)
    return jnp.concatenate([jnp.where(lane < HEAD_DIM, t, zero), jnp.where(lane >= HEAD_DIM, t, zero)], axis=0)


def _head_lanes(t2):
    lane = lax.broadcasted_iota(jnp.int32, (Q_BLOCK, t2.shape[1]), 1)
    return jnp.where(lane < HEAD_DIM, t2[:Q_BLOCK], t2[Q_BLOCK:])


def _attn_probs(q2, kb, bias2, block):
    kj = lax.broadcasted_iota(jnp.int32, (1, BAND), 1)
    before_start = jnp.where(block * Q_BLOCK + kj - KV_PAD >= 0, 0.0, NEG_INF)
    s = lax.dot_general(q2 * ATTN_SCALE, kb, NT_DIMS, preferred_element_type=F32) + bias2 + before_start
    e = jnp.exp(s - jnp.max(s, axis=-1, keepdims=True))
    return e / jnp.sum(e, axis=-1, keepdims=True)


def _attn_specs(S):
    lanes = HEADS_PER_BLOCK * HEAD_DIM
    rows = S + KV_PAD
    q_spec = pl.BlockSpec((BLOCKS_PER_STEP * Q_BLOCK, lanes), lambda h, i: (i, h))
    k_spec = pl.BlockSpec((rows, lanes), lambda h, i: (0, h))
    v_spec = pl.BlockSpec((rows, lanes), lambda h, i: (0, N_HEADS // HEADS_PER_BLOCK + h))
    b_spec = pl.BlockSpec((HEADS_PER_BLOCK, Q_BLOCK, BAND), lambda h, i: (h, 0, 0))
    return q_spec, k_spec, v_spec, b_spec


def _attn_fwd(q, kvp, bias, name, deps=()):
    S, HD = q.shape
    q_spec, k_spec, v_spec, b_spec = _attn_specs(S)

    def body(q_ref, k_ref, v_ref, b_ref, *rest):
        o_ref = rest[-1]
        for b in range(BLOCKS_PER_STEP):
            block = pl.program_id(1) * BLOCKS_PER_STEP + b
            rows = slice(b * Q_BLOCK, (b + 1) * Q_BLOCK)
            band = pl.ds(pl.multiple_of(block * Q_BLOCK, Q_BLOCK), BAND)
            p = _attn_probs(_head_rows(q_ref[rows, :]), k_ref[band, :], b_ref[...].reshape(2 * Q_BLOCK, BAND), block)
            o2 = jnp.dot(p.astype(BF16), v_ref[band, :], preferred_element_type=F32)
            o_ref[rows, :] = _head_lanes(o2).astype(BF16)

    return pl.pallas_call(
        body, name=name, grid=(N_HEADS // HEADS_PER_BLOCK, S // (BLOCKS_PER_STEP * Q_BLOCK)),
        in_specs=[q_spec, k_spec, v_spec, b_spec] + [HBM_SPEC] * len(deps), out_specs=q_spec,
        out_shape=_sds((S, HD), BF16), compiler_params=_cparams(2))(q, kvp, kvp, bias, *deps)


def _attn_bwd(q, kvp, bias, do, dkv_prev, name):
    S, HD = q.shape
    lanes = HEADS_PER_BLOCK * HEAD_DIM
    q_spec, k_spec, v_spec, b_spec = _attn_specs(S)
    dkv_spec = pl.BlockSpec((2, S + KV_PAD, lanes), lambda h, i: (0, 0, h))
    prev = [] if dkv_prev is None else [dkv_prev]
    n_steps = S // (BLOCKS_PER_STEP * Q_BLOCK)

    def body(q_ref, k_ref, v_ref, b_ref, do_ref, *rest):
        dq_ref, dkv_ref, db_ref = rest[len(prev):len(prev) + 3]

        @pl.when(pl.program_id(1) == 0)
        def _():
            dkv_ref[...] = rest[0][...] if prev else jnp.zeros_like(dkv_ref)
            db_ref[...] = jnp.zeros_like(db_ref)

        db = jnp.zeros((2 * Q_BLOCK, BAND), F32)
        for b in range(BLOCKS_PER_STEP):
            block = pl.program_id(1) * BLOCKS_PER_STEP + b
            rows = slice(b * Q_BLOCK, (b + 1) * Q_BLOCK)
            band = pl.ds(pl.multiple_of(block * Q_BLOCK, Q_BLOCK), BAND)
            kb, vb = k_ref[band, :], v_ref[band, :]
            q2, do2 = _head_rows(q_ref[rows, :]), _head_rows(do_ref[rows, :])
            p = _attn_probs(q2, kb, b_ref[...].reshape(2 * Q_BLOCK, BAND), block)
            dp = lax.dot_general(do2, vb, NT_DIMS, preferred_element_type=F32)
            ds = p * (dp - jnp.sum(dp * p, axis=-1, keepdims=True))
            db = db + ds
            ds_b = (ds * ATTN_SCALE).astype(BF16)
            dq_ref[rows, :] = _head_lanes(jnp.dot(ds_b, kb, preferred_element_type=F32)).astype(BF16)
            dkv_ref[0, band, :] += lax.dot_general(ds_b, q2, TN_DIMS, preferred_element_type=F32)
            dkv_ref[1, band, :] += lax.dot_general(p.astype(BF16), do2, TN_DIMS, preferred_element_type=F32)
        db_ref[...] += db.reshape(HEADS_PER_BLOCK, Q_BLOCK, BAND)

        if prev:
            @pl.when(pl.program_id(1) == n_steps - 1)
            def _():
                rest[-1][...] = dkv_ref[:, KV_PAD:, :].astype(BF16)

    return pl.pallas_call(
        body, name=name, grid=(N_HEADS // HEADS_PER_BLOCK, n_steps),
        in_specs=[q_spec, k_spec, v_spec, b_spec, q_spec] + [dkv_spec] * len(prev),
        out_specs=[q_spec, dkv_spec, b_spec] + [pl.BlockSpec((2, S, lanes), lambda h, i: (0, 0, h))] * len(prev),
        out_shape=[_sds((S, HD), BF16), _sds((2, S + KV_PAD, HD), F32), _sds((N_HEADS, Q_BLOCK, BAND), F32)]
        + [_sds((2, S, HD), BF16)] * len(prev),
        compiler_params=_cparams(2))(q, kvp, kvp, bias, do, *prev)


def _loss_head(x, g, target, name, tm=512):
    S, D = x.shape

    def body(x_ref, g_ref, t_ref, loss_ref, dx_ref, dxb_ref, dg_ref):
        @pl.when(pl.program_id(0) == 0)
        def _():
            loss_ref[...] = jnp.zeros_like(loss_ref)
            dg_ref[...] = jnp.zeros_like(dg_ref)

        xhat, r = _rms_hat(x_ref[...])
        gv = g_ref[...]
        err = xhat * gv - t_ref[...]
        loss_ref[...] += 0.5 * jnp.sum(jnp.mean(err * err, axis=-1, keepdims=True))
        dx, dgp = _rms_bwd(xhat, r, gv, err * (1.0 / D))
        dx_ref[...] = dx
        dxb_ref[...] = dx.astype(BF16)
        dg_ref[...] += jnp.sum(dgp, axis=0, keepdims=True)

    row = pl.BlockSpec((tm, D), lambda i: (i, 0))
    vec = pl.BlockSpec((1, D), lambda i: (0, 0))
    return pl.pallas_call(
        body, name=name, grid=(S // tm,), in_specs=[row, vec, row],
        out_specs=[pl.BlockSpec((8, 128), lambda i: (0, 0)), row, row, vec],
        out_shape=[_sds((8, 128), F32), _sds((S, D), F32), _sds((S, D), BF16), _sds((1, D), F32)],
        compiler_params=_cparams(1))(x, g, target)


def _place():
    x, y, c = lax.axis_index("x"), lax.axis_index("y"), lax.axis_index("c")
    chips = [(1 - x, y), (x, 1 - y), (1 - x, 1 - y)]
    return x, y, c, chips


def _half_rows(c, r):
    return pl.ds(pl.multiple_of(c * (r // 2), 8), r // 2)


HBM_SPEC = pl.BlockSpec(memory_space=pl.ANY)


STRICT_HBM_SPEC = pl.BlockSpec(memory_space=pltpu.HBM)
SEM_SPEC = pl.BlockSpec(memory_space=pltpu.SEMAPHORE)
EFFECT = pltpu.SideEffectType.DATAFLOW_SIDE_EFFECTING


def _peers(x, y, c):
    out = []
    for k in range(1, N_DEV):
        px, py, pc = (x + ((k >> 2) & 1)) % 2, (y + ((k >> 1) & 1)) % 2, (c + (k & 1)) % 2
        out.append(((px, py, pc), 2 * px + py, pc, 4 * px + 2 * py + pc))
    return out


def _token_spec():
    return pl.BlockSpec(memory_space=pltpu.VMEM)


def _hbm(a):
    return pltpu.with_memory_space_constraint(a, pltpu.HBM)


def _slab_half(ref, chip, core):
    return ref.at[2 * chip[0] + chip[1], _half_rows(core, ref.shape[1]), :]


def _allgather_start(slabs, name):
    n = len(slabs)

    def body(*refs):
        src, send, recv, token = refs[:n], refs[n], refs[n + 1], refs[-1]
        x, y, c, chips = _place()
        for a in range(n):
            own = _slab_half(src[a], (x, y), c)
            for j, chip in enumerate(chips):
                pltpu.make_async_remote_copy(src_ref=own, dst_ref=own, send_sem=send.at[3 * a + j], recv_sem=recv.at[3 * a + j],
                                             device_id=(*chip, c), device_id_type=MESH).start()
        token[...] = jnp.zeros_like(token)

    sems = pltpu.SemaphoreType.DMA((3 * n,))
    send, recv, *flying, token = pl.pallas_call(
        body, name=name, in_specs=[STRICT_HBM_SPEC] * n,
        out_shape=(sems, sems, *[pltpu.HBM(s.shape, s.dtype) for s in slabs], _sds((8, 128), F32)),
        out_specs=(SEM_SPEC, SEM_SPEC, *[STRICT_HBM_SPEC] * n, _token_spec()),
        input_output_aliases={a: a + 2 for a in range(n)},
        compiler_params=pltpu.CompilerParams(has_side_effects=EFFECT))(*[_hbm(s) for s in slabs])
    return send, recv, flying, token


def _allgather_relay(flying, send, recv, first, after, name):
    n = len(flying)

    def body(*refs):
        src, send_ref, recv_ref = refs[:n], refs[n], refs[n + 1]
        send2, recv2, token = refs[n + 3], refs[n + 4], refs[-1]
        token[...] = jnp.zeros_like(token)
        x, y, c, chips = _place()
        for a in range(n):
            for j, chip in enumerate(chips):
                cp = pltpu.make_async_remote_copy(
                    src_ref=_slab_half(src[a], (x, y), c), dst_ref=_slab_half(src[a], chip, c),
                    send_sem=send_ref.at[3 * (first + a) + j], recv_sem=recv_ref.at[3 * (first + a) + j],
                    device_id=(*chip, c), device_id_type=MESH)
                cp.wait_send()
                cp.wait_recv()
        for a in range(n):
            for j, chip in enumerate(chips):
                landed = _slab_half(src[a], chip, c)
                pltpu.make_async_remote_copy(src_ref=landed, dst_ref=landed, send_sem=send2.at[3 * a + j],
                                             recv_sem=recv2.at[3 * a + j], device_id=(x, y, 1 - c),
                                             device_id_type=MESH).start()

    sems = pltpu.SemaphoreType.DMA((3 * n,))
    send2, recv2, *relayed, token = pl.pallas_call(
        body, name=name, in_specs=[STRICT_HBM_SPEC] * n + [SEM_SPEC, SEM_SPEC, HBM_SPEC],
        out_shape=(sems, sems, *[pltpu.HBM(s.shape, s.dtype) for s in flying], _sds((8, 128), F32)),
        out_specs=(SEM_SPEC, SEM_SPEC, *[STRICT_HBM_SPEC] * n, _token_spec()),
        input_output_aliases={a: a + 2 for a in range(n)},
        compiler_params=pltpu.CompilerParams(has_side_effects=EFFECT))(*flying, send, recv, after)
    return send2, recv2, relayed, token


def _allgather_wait(relayed, send2, recv2, after, name):
    n = len(relayed)

    def body(*refs):
        src, send_ref, recv_ref = refs[:n], refs[n], refs[n + 1]
        x, y, c, chips = _place()
        for a in range(n):
            for j, chip in enumerate(chips):
                cp = pltpu.make_async_remote_copy(
                    src_ref=_slab_half(src[a], chip, c), dst_ref=_slab_half(src[a], chip, 1 - c),
                    send_sem=send_ref.at[3 * a + j], recv_sem=recv_ref.at[3 * a + j],
                    device_id=(x, y, 1 - c), device_id_type=MESH)
                cp.wait_send()
                cp.wait_recv()

    return pl.pallas_call(
        body, name=name, in_specs=[STRICT_HBM_SPEC] * n + [SEM_SPEC, SEM_SPEC, HBM_SPEC],
        out_shape=tuple(pltpu.HBM(s.shape, s.dtype) for s in relayed), out_specs=tuple([STRICT_HBM_SPEC] * n),
        input_output_aliases={a: a for a in range(n)},
        compiler_params=pltpu.CompilerParams(has_side_effects=EFFECT))(*relayed, send2, recv2, after)


def _allgather_small(small, name):
    def body(sm, osm, send, recv, local):
        x, y, c, chips = _place()
        own = pltpu.make_async_copy(sm, osm.at[2 * x + y], local)
        own.start()
        cps = [pltpu.make_async_remote_copy(src_ref=sm, dst_ref=osm.at[2 * x + y], send_sem=send.at[j],
                                            recv_sem=recv.at[j], device_id=(*chip, c), device_id_type=MESH)
               for j, chip in enumerate(chips)]
        for cp in cps:
            cp.start()
        for j, chip in enumerate(chips):
            got = osm.at[2 * chip[0] + chip[1]]
            pltpu.make_async_remote_copy(src_ref=got, dst_ref=got, send_sem=send.at[j], recv_sem=recv.at[j],
                                         device_id=(x, y, c), device_id_type=MESH).wait_recv()
        for cp in cps:
            cp.wait_send()
        own.wait()

    return pl.pallas_call(
        body, name=name, in_specs=[pl.BlockSpec(memory_space=pltpu.VMEM)], out_specs=HBM_SPEC,
        out_shape=_sds((N_CHIPS, *small.shape), small.dtype),
        scratch_shapes=[pltpu.SemaphoreType.DMA((3,)), pltpu.SemaphoreType.DMA((3,)), pltpu.SemaphoreType.DMA])(small)


def _reduce_start(grads, name):
    n = len(grads)

    def body(*refs):
        src, land, send, recv, token = refs[:n], refs[n:2 * n], refs[2 * n], refs[2 * n + 1], refs[-1]
        x, y, c, _ = _place()
        me = 4 * x + 2 * y + c
        for a in range(n):
            for k, (peer, p_chip, p_core, _) in enumerate(_peers(x, y, c)):
                pltpu.make_async_remote_copy(
                    src_ref=src[a].at[p_chip, _half_rows(p_core, src[a].shape[1]), :], dst_ref=land[a].at[me],
                    send_sem=send.at[(N_DEV - 1) * a + k], recv_sem=recv.at[(N_DEV - 1) * a + k],
                    device_id=peer, device_id_type=MESH).start()
        token[...] = jnp.zeros_like(token)

    lands = [lax.empty((N_DEV, g.shape[1] // 2, g.shape[2]), BF16) for g in grads]
    sems = pltpu.SemaphoreType.DMA(((N_DEV - 1) * n,))
    shapes = [pltpu.HBM(a.shape, a.dtype) for a in grads + lands]
    send, recv, *flying, token = pl.pallas_call(
        body, name=name, in_specs=[STRICT_HBM_SPEC] * (2 * n),
        out_shape=(sems, sems, *shapes, _sds((8, 128), F32)),
        out_specs=(SEM_SPEC, SEM_SPEC, *[STRICT_HBM_SPEC] * (2 * n), _token_spec()),
        input_output_aliases={a: a + 2 for a in range(2 * n)},
        compiler_params=pltpu.CompilerParams(has_side_effects=EFFECT))(*[_hbm(a) for a in grads + lands])
    return send, recv, flying[:n], flying[n:], token


def _reduce_wait(started, after, name):
    sizes = [len(grads) for _, _, grads, _ in started]
    n_arr = 2 * sum(sizes)

    def body(*refs):
        x, y, c, _ = _place()
        at = 0
        for s, n in enumerate(sizes):
            src, land = refs[at:at + n], refs[at + n:at + 2 * n]
            send_ref, recv_ref = refs[n_arr + 2 * s], refs[n_arr + 2 * s + 1]
            at += 2 * n
            for a in range(n):
                for k, (peer, p_chip, p_core, p_dev) in enumerate(_peers(x, y, c)):
                    cp = pltpu.make_async_remote_copy(
                        src_ref=src[a].at[p_chip, _half_rows(p_core, src[a].shape[1]), :], dst_ref=land[a].at[p_dev],
                        send_sem=send_ref.at[(N_DEV - 1) * a + k], recv_sem=recv_ref.at[(N_DEV - 1) * a + k],
                        device_id=peer, device_id_type=MESH)
                    cp.wait_send()
                    cp.wait_recv()

    arrays, sems = [], []
    for send, recv, grads, lands in started:
        arrays += list(grads) + list(lands)
        sems += [send, recv]
    out = pl.pallas_call(
        body, name=name, in_specs=[STRICT_HBM_SPEC] * n_arr + [SEM_SPEC] * len(sems) + [HBM_SPEC],
        out_shape=tuple(pltpu.HBM(a.shape, a.dtype) for a in arrays), out_specs=tuple([STRICT_HBM_SPEC] * n_arr),
        input_output_aliases={a: a for a in range(n_arr)},
        compiler_params=pltpu.CompilerParams(has_side_effects=EFFECT))(*arrays, *sems, after)
    result, at = [], 0
    for n in sizes:
        result.append((out[at:at + n], out[at + n:at + 2 * n]))
        at += 2 * n
    return result


def _reduce_sum(grad, land, place, name):
    _, r2, C = land.shape
    tr = _row_tile(r2, C, 4, 1024 * 1024, 16)
    nb = r2 // tr

    def body(place_ref, own_ref, *rest):
        del place_ref
        acc = own_ref[...].astype(F32)
        for ref in rest[:N_DEV - 1]:
            acc = acc + ref[...].astype(F32)
        rest[-1][...] = acc

    def from_dev(k):
        return pl.BlockSpec((None, tr, C), lambda i, place_ref: ((place_ref[2] + k) % N_DEV, i, 0))

    grid_spec = pltpu.PrefetchScalarGridSpec(
        num_scalar_prefetch=1, grid=(nb,),
        in_specs=[pl.BlockSpec((None, tr, C), lambda i, place_ref: (place_ref[0], place_ref[1] * nb + i, 0))]
        + [from_dev(k) for k in range(1, N_DEV)],
        out_specs=pl.BlockSpec((tr, C), lambda i, place_ref: (place_ref[1] * nb + i, 0)))
    return pl.pallas_call(body, name=name, grid_spec=grid_spec, out_shape=_sds((2 * r2, C), F32),
                          compiler_params=_cparams(1))(place, grad, *[land] * (N_DEV - 1))


def _sibling_join(halves, name):
    n = len(halves)

    def body(*refs):
        out, send, recv = refs[n:2 * n], refs[2 * n], refs[2 * n + 1]
        x, y, c, _ = _place()
        cps = []
        for w in range(n):
            mine = out[w].at[_half_rows(c, out[w].shape[0]), :]
            cps.append(pltpu.make_async_remote_copy(src_ref=mine, dst_ref=mine, send_sem=send.at[w],
                                                    recv_sem=recv.at[w], device_id=(x, y, 1 - c), device_id_type=MESH))
        for cp in cps:
            cp.start()
        for w in range(n):
            theirs = out[w].at[_half_rows(1 - c, out[w].shape[0]), :]
            pltpu.make_async_remote_copy(src_ref=theirs, dst_ref=theirs, send_sem=send.at[w], recv_sem=recv.at[w],
                                         device_id=(x, y, c), device_id_type=MESH).wait_recv()
        for cp in cps:
            cp.wait_send()

    return pl.pallas_call(
        body, name=name, in_specs=[HBM_SPEC] * n, out_specs=[HBM_SPEC] * n,
        out_shape=[_sds(a.shape, F32) for a in halves], input_output_aliases={w: w for w in range(n)},
        scratch_shapes=[pltpu.SemaphoreType.DMA((n,)), pltpu.SemaphoreType.DMA((n,))])(*halves)


def _gather_small(packed, name):
    def body(p_ref, out, send, recv, local):
        x, y, c, _ = _place()
        me = 4 * x + 2 * y + c
        own = pltpu.make_async_copy(p_ref, out.at[me], local)
        own.start()
        cps = []
        for k in range(1, N_DEV):
            fx, fy, fc = (k >> 2) & 1, (k >> 1) & 1, k & 1
            peer = ((x + fx) % 2, (y + fy) % 2, (c + fc) % 2)
            cps.append(pltpu.make_async_remote_copy(src_ref=p_ref, dst_ref=out.at[me], send_sem=send.at[k - 1],
                                                    recv_sem=recv.at[k - 1], device_id=peer, device_id_type=MESH))
        for cp in cps:
            cp.start()
        for k in range(1, N_DEV):
            fx, fy, fc = (k >> 2) & 1, (k >> 1) & 1, k & 1
            src = out.at[4 * ((x + fx) % 2) + 2 * ((y + fy) % 2) + (c + fc) % 2]
            pltpu.make_async_remote_copy(src_ref=src, dst_ref=src, send_sem=send.at[k - 1], recv_sem=recv.at[k - 1],
                                         device_id=(x, y, c), device_id_type=MESH).wait_recv()
        for cp in cps:
            cp.wait_send()
        own.wait()

    return pl.pallas_call(
        body, name=name, in_specs=[pl.BlockSpec(memory_space=pltpu.VMEM)], out_specs=HBM_SPEC,
        out_shape=_sds((N_DEV, *packed.shape), F32),
        scratch_shapes=[pltpu.SemaphoreType.DMA((N_DEV - 1,)), pltpu.SemaphoreType.DMA((N_DEV - 1,)),
                        pltpu.SemaphoreType.DMA])(packed)


def _sum_devices(gathered, name):
    _, R, C = gathered.shape

    def body(g_ref, o_ref):
        acc = g_ref[0]
        for d in range(1, N_DEV):
            acc = acc + g_ref[d]
        o_ref[...] = acc

    tr = 8
    return pl.pallas_call(
        body, name=name, grid=(R // tr,), in_specs=[pl.BlockSpec((N_DEV, tr, C), lambda i: (0, i, 0))],
        out_specs=pl.BlockSpec((tr, C), lambda i: (i, 0)), out_shape=_sds((R, C), F32),
        compiler_params=_cparams(1))(gathered)


def _gather_start(packed, name):
    def body(src, land, send, recv, *rest):
        x, y, c, _ = _place()
        for k, (peer, _, _, _) in enumerate(_peers(x, y, c)):
            pltpu.make_async_remote_copy(src_ref=src, dst_ref=land.at[4 * x + 2 * y + c], send_sem=send.at[k],
                                         recv_sem=recv.at[k], device_id=peer, device_id_type=MESH).start()
        rest[-1][...] = jnp.zeros_like(rest[-1])

    land = lax.empty((N_DEV, *packed.shape), F32)
    sems = pltpu.SemaphoreType.DMA((N_DEV - 1,))
    return pl.pallas_call(
        body, name=name, in_specs=[STRICT_HBM_SPEC] * 2,
        out_shape=(sems, sems, pltpu.HBM(packed.shape, F32), pltpu.HBM(land.shape, F32), _sds((8, 128), F32)),
        out_specs=(SEM_SPEC, SEM_SPEC, STRICT_HBM_SPEC, STRICT_HBM_SPEC, _token_spec()),
        input_output_aliases={0: 2, 1: 3},
        compiler_params=pltpu.CompilerParams(has_side_effects=EFFECT))(_hbm(packed), _hbm(land))


def _gather_wait(started, after, name):
    n = len(started)

    def body(*refs):
        x, y, c, _ = _place()
        for s in range(n):
            src, land, send, recv = refs[2 * s], refs[2 * s + 1], refs[2 * n + 2 * s], refs[2 * n + 2 * s + 1]
            for k, (peer, _, _, p_dev) in enumerate(_peers(x, y, c)):
                cp = pltpu.make_async_remote_copy(src_ref=src, dst_ref=land.at[p_dev], send_sem=send.at[k],
                                                  recv_sem=recv.at[k], device_id=peer,
                                                  device_id_type=MESH)
                cp.wait_send()
                cp.wait_recv()

    arrays = [a for _, _, packed, land in started for a in (packed, land)]
    sems = [s for send, recv, _, _ in started for s in (send, recv)]
    out = pl.pallas_call(
        body, name=name, in_specs=[STRICT_HBM_SPEC] * (2 * n) + [SEM_SPEC] * (2 * n) + [HBM_SPEC],
        out_shape=tuple(pltpu.HBM(a.shape, a.dtype) for a in arrays), out_specs=tuple([STRICT_HBM_SPEC] * (2 * n)),
        input_output_aliases={a: a for a in range(2 * n)},
        compiler_params=pltpu.CompilerParams(has_side_effects=EFFECT))(*arrays, *sems, after)
    return [(out[2 * s], out[2 * s + 1]) for s in range(n)]


def _sum_gathered(packed, land, device, name):
    R, C = packed.shape
    tr = 8

    def body(dev_ref, own_ref, *rest):
        me = dev_ref[0]
        acc = None
        for d in range(N_DEV):
            term = jnp.where(me == d, own_ref[...], rest[d][...])
            acc = term if acc is None else acc + term
        rest[-1][...] = acc

    def slab(d):
        return pl.BlockSpec((None, tr, C), lambda i, dev_ref: (jnp.where(dev_ref[0] == d, (d + 1) % N_DEV, d), i, 0))

    grid_spec = pltpu.PrefetchScalarGridSpec(
        num_scalar_prefetch=1, grid=(R // tr,),
        in_specs=[pl.BlockSpec((tr, C), lambda i, dev_ref: (i, 0))] + [slab(d) for d in range(N_DEV)],
        out_specs=pl.BlockSpec((tr, C), lambda i, dev_ref: (i, 0)))
    return pl.pallas_call(body, name=name, grid_spec=grid_spec, out_shape=_sds((R, C), F32),
                          compiler_params=_cparams(1))(device, packed, *[land] * N_DEV)


def _pack_small(arrays):
    rows = []
    for a in arrays:
        flat = a.reshape(-1)
        pad = (-flat.shape[0]) % SMALL_COLS
        rows.append(jnp.pad(flat, (0, pad)).reshape(-1, SMALL_COLS))
    packed = jnp.concatenate(rows, axis=0)
    return jnp.pad(packed, ((0, (-packed.shape[0]) % 8), (0, 0)))


def _unpack_small(packed, shapes):
    out, row = [], 0
    for shape in shapes:
        size = math.prod(shape)
        n_rows = -(-size // SMALL_COLS)
        out.append(packed[row:row + n_rows].reshape(-1)[:size].reshape(shape))
        row += n_rows
    return out


def kernel(x, a_norm, a_w_in, a_sgu_norm, a_w_spatial, a_b_spatial, a_w_out, kv_norm, w_kv, b_norm, b_w_q, b_rel_bias, b_w_o, ffn_norm, ffn_w_gate_up, ffn_w_down, final_norm, loss_target, m_a_norm, m_a_w_in, m_a_sgu_norm, m_a_w_spatial, m_a_b_spatial, m_a_w_out, m_kv_norm, m_w_kv, m_b_norm, m_b_w_q, m_b_rel_bias, m_b_w_o, m_ffn_norm, m_ffn_w_gate_up, m_ffn_w_down, m_final_norm, v_a_norm, v_a_w_in, v_a_sgu_norm, v_a_w_spatial, v_a_b_spatial, v_a_w_out, v_kv_norm, v_w_kv, v_b_norm, v_b_w_q, v_b_rel_bias, v_b_w_o, v_ffn_norm, v_ffn_w_gate_up, v_ffn_w_down, v_final_norm):
    S, D = x.shape[1], x.shape[2]
    n_a = a_w_in.shape[0]
    n_b = b_w_q.shape[0]
    depth = ffn_w_gate_up.shape[0]
    xi, yi, ci = lax.axis_index("x"), lax.axis_index("y"), lax.axis_index("c")
    chip = 2 * xi + yi

    place = jnp.stack([chip, ci, 2 * chip + ci]).astype(jnp.int32)
    stacked = {"a_w_in": a_w_in, "a_w_out": a_w_out, "w_kv": w_kv[None], "b_w_q": b_w_q, "b_w_o": b_w_o,
               "ffn_w_gate_up": ffn_w_gate_up, "ffn_w_down": ffn_w_down}
    groups = []
    for layer in range(depth):
        if layer < n_a:
            groups.append([("a_w_in", layer), ("a_w_out", layer)])
        elif layer == n_a:
            groups.append([("w_kv", 0), ("b_w_q", 0), ("b_w_o", 0)])
        else:
            groups.append([("b_w_q", layer - n_a), ("b_w_o", layer - n_a)])
        groups.append([("ffn_w_gate_up", layer), ("ffn_w_down", layer)])
    units = [u for group in groups for u in group]
    n_early = len(groups[0])
    slabs = [_cast_slab(stacked[k], l, place[:1], f"cast_{k}_{l}") for k, l in units[:n_early]]
    early = _allgather_start(slabs, "allgather_start_first")
    slabs = [_cast_slab(stacked[k], l, place[:1], f"cast_{k}_{l}", deps=(early[3],)) for k, l in units[n_early:]]
    late = _allgather_start(slabs, "allgather_start_rest")
    na_w, ns_w = a_norm.shape[1], a_sgu_norm.shape[1]
    small_g = _allgather_small(jnp.concatenate([a_norm, a_sgu_norm], axis=1), "allgather_small")
    a_norm_f = small_g[:, :, :na_w].transpose(1, 0, 2).reshape(n_a, N_CHIPS * na_w)
    a_sgu_f = small_g[:, :, na_w:].transpose(1, 0, 2).reshape(n_a, N_CHIPS * ns_w)
    W, relayed = {}, {}

    def relay(group_index, after):
        if group_index == len(groups):
            return ()
        group = groups[group_index]
        (send, recv, flying, _), first = (early, 0) if group_index == 0 else (late, units.index(group[0]) - n_early)
        relayed[group_index] = _allgather_relay(flying[first:first + len(group)], send, recv, first, after,
                                                f"allgather_relay_{group_index}")
        return (relayed[group_index][3],)

    def gathered(group_index, after):
        send2, recv2, arrays, _ = relayed.pop(group_index)
        W.update(zip(groups[group_index], _allgather_wait(arrays, send2, recv2, after, f"allgather_wait_{group_index}")))

    xc = x.reshape(S, D)
    saved = []
    kvp = x_kv = h_kv = None
    relay(0, late[3])
    order = ()
    for layer in range(depth):
        rec = {"x_in": xc}
        gathered(2 * layer, xc)
        if layer < n_a:
            i = layer
            rec["zpre"], rec["h"] = _norm_matmul(xc, a_norm_f[i][None], W["a_w_in", i], BF16, f"a{i}_in", deps=order)
            order = relay(2 * layer + 1, rec["h"])
            rec["uv"] = _sgu_fwd(rec["zpre"], a_sgu_f[i][None], a_w_spatial[i], a_b_spatial[i].T, f"a{i}_sgu",
                                 deps=order)
            xm = _matmul_res(rec["uv"], W["a_w_out", i], xc, f"a{i}_out", tm=512)
        else:
            i = layer - n_a
            if i == 0:
                kvp, h_kv = _norm_matmul(xc, kv_norm[None], W["w_kv", 0], BF16, "kv_proj", tm=KV_PAD, zero_rows=KV_PAD)
                x_kv = xc
            rec["q"], rec["h"] = _norm_matmul(xc, b_norm[i][None], W["b_w_q", i], BF16, f"b{i}_q", row_sharded=True)
            order = relay(2 * layer + 1, rec["h"])
            table = jnp.pad(b_rel_bias[i], ((0, 0), (0, TABLE_PAD - b_rel_bias.shape[2])))
            rec["bias"] = _rel_bias_fwd(table, f"b{i}_bias").transpose(1, 0, 2)
            rec["o"] = _attn_fwd(rec["q"], kvp, rec["bias"], f"b{i}_attn", deps=order)
            xm = _matmul_res(rec["o"], W["b_w_o", i], xc, f"b{i}_o", tm=512)
        rec["x_mid"] = xm
        gathered(2 * layer + 1, xm)
        rec["gu"], rec["h_f"] = _norm_matmul(xm, ffn_norm[layer][None], W["ffn_w_gate_up", layer], BF16, f"f{layer}_in")
        order = relay(2 * layer + 2, rec["h_f"])
        xc = _matmul_res(rec["gu"], W["ffn_w_down", layer], xm, f"f{layer}_out", swiglu=True, deps=order, tm=512)
        order = ()
        saved.append(rec)

    loss_tile, dx, dxb, d_final = _loss_head(xc, final_norm[None], loss_target.reshape(S, D), "loss_head")

    started = []
    small_started = []
    pending = []

    held = []

    def weight_grad(unit, hold=False, **kw):
        full = (N_CHIPS,) + tuple(stacked[unit[0]].shape[1:])
        g = _matmul_tn(out_shape=full, name=f"d_{unit[0]}_{unit[1]}", deps=tuple(pending), **kw)
        pending.clear()
        held.append((unit, g))
        if hold:
            return ()
        send, recv, flying_g, flying_land, token = _reduce_start([g for _, g in held],
                                                                 f"reduce_start_{unit[0]}_{unit[1]}")
        started.append(([u for u, _ in held], send, recv, flying_g, flying_land))
        held.clear()
        return (token,)

    tt = 512
    tw = min(2048, S)
    tb = min(NORMBWD_ROWS, S)
    row_a = lambda w, rows=tw: pl.BlockSpec((rows, w), lambda o, t: (t, 0))
    d_ffn_norm, d_b_norm, d_a_norm, d_a_sgu = [None] * depth, [None] * n_b, [None] * n_a, [None] * n_a
    d_ws, d_bs, d_rel = [None] * n_a, [None] * n_a, [None] * n_b
    dkv = None
    first = lambda ref: ref[...]
    for layer in reversed(range(depth)):
        rec = saved[layer]
        r_d = ffn_w_down.shape[1]
        half_f = 2 * r_d
        token = weight_grad(
            ("ffn_w_down", layer), a_ops=[rec["gu"], rec["gu"]],
            a_specs=[pl.BlockSpec((tt, half_f), lambda o, t: (t, o)), pl.BlockSpec((tt, half_f), lambda o, t: (t, 2 + o))],
            a_fn=lambda g_ref, u_ref: _swiglu(g_ref[...].astype(F32), u_ref[...].astype(F32)).astype(BF16),
            b_op=dxb, b_spec=row_a(D, tt), out_spec=pl.BlockSpec((2, r_d, D), lambda o, t: (o, 0, 0)),
            acc_shape=(half_f, D), n_outer=2, tt=tt, hold=True)
        dgu = _nt_swiglu_bwd(dxb, W["ffn_w_down", layer], rec["gu"], f"f{layer}_dgu", deps=token)
        nsh = ffn_w_gate_up.shape[2]
        token = weight_grad(
            ("ffn_w_gate_up", layer), a_ops=[rec["h_f"]], a_specs=[row_a(D)], a_fn=first,
            b_op=dgu, b_spec=pl.BlockSpec((None, tw, nsh), lambda o, t: (o // 2, t, o % 2)),
            out_spec=pl.BlockSpec((None, D, nsh), lambda o, t: (o, 0, 0)), acc_shape=(D, nsh), n_outer=N_CHIPS, tt=tw)
        dx, dxb, d_ffn_norm[layer] = _nt_normbwd(
            dgu, pl.BlockSpec((None, tb, nsh), lambda i, k: (k // 2, i, k % 2)),
            W["ffn_w_gate_up", layer], pl.BlockSpec((None, D, nsh), lambda i, k: (k, 0, 0)),
            (D, nsh), N_CHIPS, rec["x_mid"], ffn_norm[layer][None], dx, f"f{layer}_dx", deps=token)
        if layer >= n_a:
            i = layer - n_a
            r_o = b_w_o.shape[1]
            token = weight_grad(
                ("b_w_o", i), a_ops=[rec["o"]], a_specs=[row_a(D)], a_fn=first, b_op=dxb, b_spec=row_a(D),
                out_spec=pl.BlockSpec((N_CHIPS, r_o, D), lambda o, t: (0, 0, 0)), acc_shape=(D, D), n_outer=1, tt=tw,
                hold=True)
            do = _nt_rows(dxb, W["b_w_o", i], N_CHIPS, BF16, f"b{i}_do", deps=token)
            dq, dkv, dbias, *dkv_bf16 = _attn_bwd(rec["q"], kvp, rec["bias"], do, dkv, f"b{i}_attn_bwd")
            d_rel[i] = _rel_bias_bwd(dbias.transpose(1, 0, 2), f"b{i}_dbias")[:, :b_rel_bias.shape[2]]
            token = weight_grad(
                ("b_w_q", i), a_ops=[rec["h"]], a_specs=[row_a(D)], a_fn=first, b_op=dq, b_spec=row_a(D),
                out_spec=pl.BlockSpec((N_CHIPS, r_o, D), lambda o, t: (0, 0, 0)), acc_shape=(D, D), n_outer=1, tt=tw)
            dx, dxb, d_b_norm[i] = _nt_normbwd(
                dq, pl.BlockSpec((tb, D), lambda i_, k: (i_, 0)),
                W["b_w_q", i], pl.BlockSpec((N_CHIPS, r_o, D), lambda i_, k: (0, 0, 0)),
                (D, D), 1, rec["x_in"], b_norm[i][None], dx, f"b{i}_dx", deps=token)
            if i == 0:
                dkv_b = dkv_bf16[0] if dkv_bf16 else dkv[:, KV_PAD:, :].astype(BF16)
                n_kv = w_kv.shape[1]
                token = weight_grad(
                    ("w_kv", 0), a_ops=[h_kv], a_specs=[row_a(D)], a_fn=first,
                    b_op=dkv_b, b_spec=pl.BlockSpec((None, tw, n_kv), lambda o, t: (o // 2, t, o % 2)),
                    out_spec=pl.BlockSpec((None, D, n_kv), lambda o, t: (o, 0, 0)), acc_shape=(D, n_kv),
                    n_outer=N_CHIPS, tt=tw)
                dx, dxb, d_kv_norm = _nt_normbwd(
                    dkv_b, pl.BlockSpec((None, tb, n_kv), lambda i_, k: (k // 2, i_, k % 2)),
                    W["w_kv", 0], pl.BlockSpec((None, D, n_kv), lambda i_, k: (k, 0, 0)),
                    (D, n_kv), N_CHIPS, x_kv, kv_norm[None], dx, "kv_dx", deps=token)
        else:
            i = layer
            r_w = a_w_out.shape[1]
            token = weight_grad(
                ("a_w_out", i), a_ops=[rec["uv"]], a_specs=[row_a(N_CHIPS * r_w, tw // 2)], a_fn=first,
                b_op=dxb, b_spec=row_a(D, tw // 2), out_spec=pl.BlockSpec((N_CHIPS, r_w, D), lambda o, t: (0, 0, 0)),
                acc_shape=(N_CHIPS * r_w, D), n_outer=1, tt=tw // 2, hold=i > 0)
            duv = _nt_rows(dxb, W["a_w_out", i], 2, BF16, f"a{i}_duv", deps=token)
            dz, d_a_sgu[i], d_ws[i], dbs = _sgu_bwd(rec["zpre"], duv, a_sgu_f[i][None], a_w_spatial[i],
                                                  a_w_spatial[i].transpose(0, 2, 1), a_b_spatial[i].T, f"a{i}_sgu_bwd")
            d_bs[i] = dbs[:, :, 0]
            if i == 0:
                batch = [d_a_sgu[0], d_ws[0][None], d_bs[0][None], d_ffn_norm[0]]
                small_started.append((batch, _gather_start(_pack_small(batch), "gather_start_late")))
                pending.append(small_started[-1][1][4])
            n_in = a_w_in.shape[2]
            token = weight_grad(
                ("a_w_in", i), a_ops=[rec["h"]], a_specs=[row_a(D)], a_fn=first,
                b_op=dz, b_spec=pl.BlockSpec((tw, n_in), lambda o, t: (t, o)),
                out_spec=pl.BlockSpec((None, D, n_in), lambda o, t: (o, 0, 0)), acc_shape=(D, n_in), n_outer=N_CHIPS,
                tt=tw)
            dx, dxb, d_a_norm[i] = _nt_normbwd(
                dz, pl.BlockSpec((tb, n_in), lambda i_, k: (i_, k)),
                W["a_w_in", i], pl.BlockSpec((None, D, n_in), lambda i_, k: (k, 0, 0)),
                (D, n_in), N_CHIPS, rec["x_in"], a_norm_f[i][None], dx, f"a{i}_dx", deps=token)
        if layer == 1:
            batch = [jnp.concatenate(d_a_norm[1:], axis=0), jnp.concatenate(d_a_sgu[1:], axis=0), jnp.stack(d_ws[1:]),
                     jnp.stack(d_bs[1:]), d_kv_norm, jnp.concatenate(d_b_norm, axis=0), jnp.stack(d_rel),
                     jnp.concatenate(d_ffn_norm[1:], axis=0), d_final, loss_tile[:1, :1]]
            small_started.append((batch, _gather_start(_pack_small(batch), "gather_start_early")))
            pending.append(small_started[-1][1][4])
    grad_x = dx.reshape(x.shape)

    landed = _reduce_wait([(send, recv, g, land) for _, send, recv, g, land in started], dx, "reduce_wait")
    reduced_units = [unit for units_, *_ in started for unit in units_]
    halves = [_reduce_sum(g, land, place, f"reduce_sum_{unit[0]}_{unit[1]}")
              for unit, (g, land) in zip(reduced_units, [pair for gs, lands in landed for pair in zip(gs, lands)])]
    joined = dict(zip(reduced_units, _sibling_join(halves, "reduce_join")))
    reduced = {}

    (packed_e, land_e), (packed_l, land_l) = _gather_wait([s[:4] for _, s in small_started], dx, "gather_wait")
    total_e = _sum_gathered(packed_e, land_e, place[2:], "sum_small_grads_early")
    total_l = _sum_gathered(packed_l, land_l, place[2:], "sum_small_grads_late")
    total_t = _sum_devices(_gather_small(_pack_small([d_a_norm[0]]), "gather_small_grads_last"), "sum_small_grads_last")
    (e_a_norm, e_a_sgu, e_ws, e_bs, g_kv_norm, g_b_norm, g_rel, e_ffn_norm, g_final, loss) = _unpack_small(
        total_e, [a.shape for a in small_started[0][0]])
    l_a_sgu, l_ws, l_bs, l_ffn_norm = _unpack_small(total_l, [a.shape for a in small_started[1][0]])
    (t_a_norm,) = _unpack_small(total_t, [d_a_norm[0].shape])
    g_a_norm = jnp.concatenate([t_a_norm, e_a_norm], axis=0)
    g_a_sgu = jnp.concatenate([l_a_sgu, e_a_sgu], axis=0)
    g_ws = jnp.concatenate([l_ws, e_ws], axis=0)
    g_bs = jnp.concatenate([l_bs, e_bs], axis=0)
    g_ffn_norm = jnp.concatenate([l_ffn_norm, e_ffn_norm], axis=0)
    reduced["a_norm"] = lax.dynamic_slice_in_dim(g_a_norm, chip * na_w, na_w, axis=1)
    reduced["a_sgu_norm"] = lax.dynamic_slice_in_dim(g_a_sgu, chip * ns_w, ns_w, axis=1)
    reduced.update(a_w_spatial=g_ws, a_b_spatial=g_bs, kv_norm=g_kv_norm.reshape(kv_norm.shape), b_norm=g_b_norm,
                   b_rel_bias=g_rel, ffn_norm=g_ffn_norm, final_norm=g_final.reshape(final_norm.shape))

    weights = dict(a_norm=a_norm, a_w_in=a_w_in, a_sgu_norm=a_sgu_norm, a_w_spatial=a_w_spatial,
                   a_b_spatial=a_b_spatial, a_w_out=a_w_out, kv_norm=kv_norm, w_kv=w_kv, b_norm=b_norm, b_w_q=b_w_q,
                   b_rel_bias=b_rel_bias, b_w_o=b_w_o, ffn_norm=ffn_norm, ffn_w_gate_up=ffn_w_gate_up,
                   ffn_w_down=ffn_w_down, final_norm=final_norm)
    m_in = dict(a_norm=m_a_norm, a_w_in=m_a_w_in, a_sgu_norm=m_a_sgu_norm, a_w_spatial=m_a_w_spatial,
                a_b_spatial=m_a_b_spatial, a_w_out=m_a_w_out, kv_norm=m_kv_norm, w_kv=m_w_kv, b_norm=m_b_norm,
                b_w_q=m_b_w_q, b_rel_bias=m_b_rel_bias, b_w_o=m_b_w_o, ffn_norm=m_ffn_norm,
                ffn_w_gate_up=m_ffn_w_gate_up, ffn_w_down=m_ffn_w_down, final_norm=m_final_norm)
    v_in = dict(a_norm=v_a_norm, a_w_in=v_a_w_in, a_sgu_norm=v_a_sgu_norm, a_w_spatial=v_a_w_spatial,
                a_b_spatial=v_a_b_spatial, a_w_out=v_a_w_out, kv_norm=v_kv_norm, w_kv=v_w_kv, b_norm=v_b_norm,
                b_w_q=v_b_w_q, b_rel_bias=v_b_rel_bias, b_w_o=v_b_w_o, ffn_norm=v_ffn_norm,
                ffn_w_gate_up=v_ffn_w_gate_up, ffn_w_down=v_ffn_w_down, final_norm=v_final_norm)
    grad_out, delta_out, m_out, v_out = [], [], [], []
    for key, w in weights.items():
        if key in stacked:
            as_layers = lambda a: a.reshape(stacked[key].shape)
            g, d, nm, nv = _adamw_stacked(
                as_layers(w), [joined[key, layer] for layer in range(stacked[key].shape[0])], as_layers(m_in[key]),
                as_layers(v_in[key]), "adamw_" + key)
        else:
            g = reduced[key].reshape(w.shape)
            view = (1, w.shape[0]) if w.ndim == 1 else (-1, w.shape[-1])
            d, nm, nv = _adamw(w.reshape(view), g.reshape(view), m_in[key].reshape(view), v_in[key].reshape(view),
                               "adamw_" + key)
        grad_out.append(g.reshape(w.shape))
        delta_out.append(d.reshape(w.shape))
        m_out.append(nm.reshape(w.shape))
        v_out.append(nv.reshape(w.shape))
    return (loss.reshape(()), grad_x, *grad_out, *delta_out, *m_out, *v_out)
```

```python
import math

import jax
import jax.numpy as jnp
from jax import lax
from jax.experimental import pallas as pl
from jax.experimental.pallas import tpu as pltpu

F32, BF16 = jnp.float32, jnp.bfloat16
MESH = pl.DeviceIdType.MESH
HIGHEST = lax.Precision.HIGHEST
NT_DIMS = (((1,), (1,)), ((), ()))
TN_DIMS = (((0,), (0,)), ((), ()))

EPS = 1e-6
CHUNK = 64
A_CHUNK = 128
A_GROUPS = 8
N_HEADS = 16
HEAD_DIM = 64
N_LEFT = 8
MAX_REL = 256
ATTN_SCALE = HEAD_DIM ** -0.5
NEG_INF = -1e30
Q_BLOCK = 2 * CHUNK
KV_PAD = N_LEFT * CHUNK
BAND = KV_PAD + Q_BLOCK
DIAGS = BAND + Q_BLOCK
TABLE_PAD = 640
HEADS_PER_BLOCK = 2
BLOCKS_PER_STEP = 4
NORMBWD_ROWS = 1024

ADAM_LR, ADAM_B1, ADAM_B2, ADAM_EPS, ADAM_WD, ADAM_STEP = 0.001, 0.9, 0.999, 1e-08, 0.01, 10

VMEM_LIMIT_BYTES = 56 * 1024 * 1024
N_CHIPS = 4
N_DEV = 8
SMALL_COLS = 1024


def _cparams(n_grid):
    return pltpu.CompilerParams(dimension_semantics=("arbitrary",) * n_grid, vmem_limit_bytes=VMEM_LIMIT_BYTES)


def _sds(shape, dtype):
    return jax.ShapeDtypeStruct(tuple(shape), dtype)


def _gelu(x):
    return x * (0.5 * (1.0 + lax.erf(x * math.sqrt(0.5))))


def _gelu_and_grad(x):
    cdf = 0.5 * (1.0 + lax.erf(x * math.sqrt(0.5)))
    return x * cdf, cdf + x * (jnp.exp(-0.5 * x * x) * (1.0 / math.sqrt(2.0 * math.pi)))


def _rms_hat(xv):
    r = lax.rsqrt(jnp.mean(xv * xv, axis=-1, keepdims=True) + EPS)
    return xv * r, r


def _rms_bwd(xhat, r, g, dy):
    dxhat = dy * g
    dx = r * (dxhat - xhat * jnp.mean(dxhat * xhat, axis=-1, keepdims=True))
    return dx, dy * xhat


def _swiglu(gate, up):
    return (gate * jax.nn.sigmoid(gate)) * up


def _row_tile(rows, cols, itemsize, cap_bytes, align):
    t = rows
    while t * cols * itemsize > cap_bytes and t % (2 * align) == 0:
        t //= 2
    return t


def _cast_slab(w, layer, chip, name, deps=()):
    _, r, C = w.shape
    tr = _row_tile(r, C, 4, 4 * 1024 * 1024, 16)

    def body(chip_ref, w_ref, *rest):
        del chip_ref
        rest[-1][...] = w_ref[...].astype(BF16)

    grid_spec = pltpu.PrefetchScalarGridSpec(
        num_scalar_prefetch=1, grid=(r // tr,),
        in_specs=[pl.BlockSpec((None, tr, C), lambda i, chip_ref: (layer, i, 0))] + [HBM_SPEC] * len(deps),
        out_specs=pl.BlockSpec((None, tr, C), lambda i, chip_ref: (chip_ref[0], i, 0)))
    return pl.pallas_call(body, name=name, grid_spec=grid_spec, out_shape=_sds((N_CHIPS, r, C), BF16),
                          compiler_params=_cparams(1))(chip, w, *deps)


def _adamw_stacked(w, gs, m, v, name):
    L, r, C = w.shape
    tr = _row_tile(r, C, 4, 2 * 1024 * 1024, 8)
    nb = r // tr

    def body(w_ref, m_ref, v_ref, *rest):
        go_ref, d_ref, nm_ref, nv_ref = rest[-4:]
        layer = pl.program_id(0)
        gv = rest[0][...]
        for k in range(1, L):
            gv = jnp.where(layer == k, rest[k][...], gv)
        mn = ADAM_B1 * m_ref[...] + (1.0 - ADAM_B1) * gv
        vn = ADAM_B2 * v_ref[...] + (1.0 - ADAM_B2) * jnp.square(gv)
        m_hat = mn / (1.0 - ADAM_B1 ** ADAM_STEP)
        v_hat = vn / (1.0 - ADAM_B2 ** ADAM_STEP)
        d_ref[...] = -ADAM_LR * (m_hat / (jnp.sqrt(v_hat) + ADAM_EPS) + ADAM_WD * w_ref[...])
        nm_ref[...] = mn
        nv_ref[...] = vn
        go_ref[...] = gv

    def grad_spec(k):
        return pl.BlockSpec((tr, C), lambda l, i: (jnp.where(l == k, i, jnp.where(l > k, nb - 1, 0)), 0))

    stacked = pl.BlockSpec((None, tr, C), lambda l, i: (l, i, 0))
    return pl.pallas_call(body, name=name, grid=(L, nb), in_specs=[stacked] * 3 + [grad_spec(k) for k in range(L)],
                          out_specs=[stacked] * 4, out_shape=[_sds((L, r, C), F32)] * 4,
                          compiler_params=_cparams(2))(w, m, v, *gs)


def _adamw(w, g, m, v, name):
    R, C = w.shape
    tr = _row_tile(R, C, 4, 1024 * 1024, 8)

    def body(w_ref, g_ref, m_ref, v_ref, d_ref, nm_ref, nv_ref):
        gv = g_ref[...]
        mn = ADAM_B1 * m_ref[...] + (1.0 - ADAM_B1) * gv
        vn = ADAM_B2 * v_ref[...] + (1.0 - ADAM_B2) * jnp.square(gv)
        m_hat = mn / (1.0 - ADAM_B1 ** ADAM_STEP)
        v_hat = vn / (1.0 - ADAM_B2 ** ADAM_STEP)
        d_ref[...] = -ADAM_LR * (m_hat / (jnp.sqrt(v_hat) + ADAM_EPS) + ADAM_WD * w_ref[...])
        nm_ref[...] = mn
        nv_ref[...] = vn

    spec = pl.BlockSpec((tr, C), lambda i: (i, 0))
    return pl.pallas_call(body, name=name, grid=(R // tr,), in_specs=[spec] * 4, out_specs=[spec] * 3,
                          out_shape=[_sds((R, C), F32)] * 3, compiler_params=_cparams(1))(w, g, m, v)


def _norm_matmul(x, g, w_g, out_dtype, name, row_sharded=False, deps=(), tm=1024, zero_rows=0):
    S, D = x.shape
    tm = min(tm, S)
    lead = zero_rows // tm
    if row_sharded:
        r, N = w_g.shape[1], w_g.shape[2]
        tn = 512
        w_spec = pl.BlockSpec((N_CHIPS, r, tn), lambda i, j: (0, 0, j))
    else:
        nsh = w_g.shape[2]
        N = N_CHIPS * nsh
        tn = next((t for t in (1024, 512) if nsh % t == 0), nsh)
        bps = nsh // tn
        w_spec = pl.BlockSpec((None, D, tn), lambda i, j: (j // bps, 0, j % bps))

    def body(x_ref, g_ref, w_ref, *rest):
        y_ref, h_ref = rest[-2:]
        i = pl.program_id(0)

        @pl.when((i >= lead) & (pl.program_id(1) == 0))
        def _():
            xhat, _ = _rms_hat(x_ref[...])
            h_ref[...] = (xhat * g_ref[...]).astype(BF16)

        @pl.when(i >= lead)
        def _():
            w = w_ref[...].reshape(D, tn)
            y_ref[...] = jnp.dot(h_ref[...], w, preferred_element_type=F32).astype(y_ref.dtype)

        if lead:
            @pl.when(i < lead)
            def _():
                y_ref[...] = jnp.zeros_like(y_ref)

    rows = lambda i, j: (jnp.maximum(i - lead, 0), 0)
    return pl.pallas_call(
        body, name=name, grid=(lead + S // tm, N // tn),
        in_specs=[pl.BlockSpec((tm, D), rows), pl.BlockSpec((1, D), lambda i, j: (0, 0)), w_spec]
        + [HBM_SPEC] * len(deps),
        out_specs=[pl.BlockSpec((tm, tn), lambda i, j: (i, j)), pl.BlockSpec((tm, D), rows)],
        out_shape=[_sds((zero_rows + S, N), out_dtype), _sds((S, D), BF16)],
        compiler_params=_cparams(2))(x, g, w_g, *deps)


def _matmul_res(a, w_g, res, name, swiglu=False, deps=(), tm=256):
    S, N = res.shape
    r = w_g.shape[1]
    K = N_CHIPS * r

    def body(*refs):
        o_ref = refs[-1]
        if swiglu:
            gate_ref, up_ref, w_ref, res_ref = refs[:4]
            a_blk = _swiglu(gate_ref[...].astype(F32), up_ref[...].astype(F32)).astype(BF16)
        else:
            a_ref, w_ref, res_ref = refs[:3]
            a_blk = a_ref[...]
        o_ref[...] = res_ref[...] + jnp.dot(a_blk, w_ref[...].reshape(K, N), preferred_element_type=F32)

    a_specs, a_ops = [pl.BlockSpec((tm, K), lambda i: (i, 0))], [a]
    if swiglu:
        a_specs.append(pl.BlockSpec((tm, K), lambda i: (i, 1)))
        a_ops.append(a)
    row = pl.BlockSpec((tm, N), lambda i: (i, 0))
    return pl.pallas_call(
        body, name=name, grid=(S // tm,),
        in_specs=a_specs + [pl.BlockSpec((N_CHIPS, r, N), lambda i: (0, 0, 0)), row] + [HBM_SPEC] * len(deps),
        out_specs=row, out_shape=_sds((S, N), F32), compiler_params=_cparams(1))(*a_ops, w_g, res, *deps)


def _matmul_tn(a_ops, a_specs, a_fn, b_op, b_spec, out_spec, out_shape, acc_shape, n_outer, name, deps=(), tt=512):
    S = b_op.shape[-2]
    na = len(a_ops)
    nt = S // tt

    def body(*refs):
        a_refs, b_ref, o_ref, acc_ref = refs[:na], refs[na], refs[-2], refs[-1]
        t = pl.program_id(1)
        part = lax.dot_general(a_fn(*a_refs), b_ref[...].astype(BF16), TN_DIMS, preferred_element_type=F32)

        @pl.when(t == 0)
        def _():
            acc_ref[...] = part

        @pl.when(t > 0)
        def _():
            acc_ref[...] += part

        @pl.when(t == nt - 1)
        def _():
            o_ref[...] = acc_ref[...].reshape(o_ref.shape).astype(BF16)

    return pl.pallas_call(
        body, name=name, grid=(n_outer, nt), in_specs=list(a_specs) + [b_spec] + [HBM_SPEC] * len(deps),
        out_specs=out_spec, out_shape=_sds(out_shape, BF16), scratch_shapes=[pltpu.VMEM(acc_shape, F32)],
        compiler_params=_cparams(2))(*a_ops, b_op, *deps)


def _nt_accumulate(a_ref, w_ref, acc_ref, w2d, nk):
    k = pl.program_id(1)
    part = lax.dot_general(a_ref[...].astype(BF16), w_ref[...].reshape(w2d), NT_DIMS, preferred_element_type=F32)

    @pl.when(k == 0)
    def _():
        acc_ref[...] = part

    @pl.when(k > 0)
    def _():
        acc_ref[...] += part

    return k == nk - 1


def _nt_normbwd(dy, dy_spec, w_g, w_spec, w2d, nk, x, g, dres, name, deps=(), tm=NORMBWD_ROWS):
    S, D = x.shape
    tm = min(tm, S)

    def body(dy_ref, w_ref, x_ref, g_ref, dres_ref, *rest):
        dx_ref, dxb_ref, dg_ref, acc_ref = rest[-4:]

        @pl.when((pl.program_id(0) == 0) & (pl.program_id(1) == 0))
        def _():
            dg_ref[...] = jnp.zeros_like(dg_ref)

        last = _nt_accumulate(dy_ref, w_ref, acc_ref, w2d, nk)

        @pl.when(last)
        def _():
            xhat, r = _rms_hat(x_ref[...])
            dx, dgp = _rms_bwd(xhat, r, g_ref[...], acc_ref[...])
            total = dres_ref[...] + dx
            dx_ref[...] = total
            dxb_ref[...] = total.astype(BF16)
            dg_ref[...] += jnp.sum(dgp, axis=0, keepdims=True)

    row = pl.BlockSpec((tm, D), lambda i, k: (i, 0))
    vec = pl.BlockSpec((1, D), lambda i, k: (0, 0))
    return pl.pallas_call(
        body, name=name, grid=(S // tm, nk),
        in_specs=[dy_spec, w_spec, row, vec, row] + [HBM_SPEC] * len(deps), out_specs=[row, row, vec],
        out_shape=[_sds((S, D), F32), _sds((S, D), BF16), _sds((1, D), F32)],
        scratch_shapes=[pltpu.VMEM((tm, D), F32)], compiler_params=_cparams(2))(dy, w_g, x, g, dres, *deps)


def _nt_rows(dy, w_g, shards_per_block, out_dtype, name, deps=(), tm=1024):
    S, N = dy.shape
    tm = min(tm, S)
    r = w_g.shape[1]
    tn = shards_per_block * r

    def body(dy_ref, w_ref, *rest):
        o_ref = rest[-1]
        o_ref[...] = lax.dot_general(dy_ref[...].astype(BF16), w_ref[...].reshape(tn, N), NT_DIMS,
                                     preferred_element_type=F32).astype(o_ref.dtype)

    return pl.pallas_call(
        body, name=name, grid=(S // tm, N_CHIPS // shards_per_block),
        in_specs=[pl.BlockSpec((tm, N), lambda i, j: (i, 0)),
                  pl.BlockSpec((shards_per_block, r, N), lambda i, j: (j, 0, 0))] + [HBM_SPEC] * len(deps),
        out_specs=pl.BlockSpec((tm, tn), lambda i, j: (i, j)),
        out_shape=_sds((S, N_CHIPS * r), out_dtype), compiler_params=_cparams(2))(dy, w_g, *deps)


def _nt_swiglu_bwd(dy, w_g, gu, name, deps=(), tm=1024):
    S, N = dy.shape
    tm = min(tm, S)
    r = w_g.shape[1]
    tn = 2 * r
    F = N_CHIPS * r

    def body(dy_ref, w_ref, gate_ref, up_ref, *rest):
        o_ref = rest[-1]
        dact = lax.dot_general(dy_ref[...].astype(BF16), w_ref[...].reshape(tn, N), NT_DIMS,
                               preferred_element_type=F32)
        gate, up = gate_ref[...].astype(F32), up_ref[...].astype(F32)
        sg = jax.nn.sigmoid(gate)
        silu = gate * sg
        o_ref[0] = ((dact * up) * (sg + silu * (1.0 - sg))).astype(BF16)
        o_ref[1] = (dact * silu).astype(BF16)

    return pl.pallas_call(
        body, name=name, grid=(2, S // tm),
        in_specs=[pl.BlockSpec((tm, N), lambda j, i: (i, 0)),
                  pl.BlockSpec((2, r, N), lambda j, i: (j, 0, 0)),
                  pl.BlockSpec((tm, tn), lambda j, i: (i, j)),
                  pl.BlockSpec((tm, tn), lambda j, i: (i, 2 + j))] + [HBM_SPEC] * len(deps),
        out_specs=pl.BlockSpec((2, tm, tn), lambda j, i: (0, i, j)),
        out_shape=_sds((2, S, F), BF16), compiler_params=_cparams(2))(dy, w_g, gu, gu, *deps)


def _chunk_causal_mask(transposed):
    i = lax.broadcasted_iota(jnp.int32, (A_CHUNK, A_CHUNK), 0) // CHUNK
    j = lax.broadcasted_iota(jnp.int32, (A_CHUNK, A_CHUNK), 1) // CHUNK
    return ((i <= j) if transposed else (i >= j)).astype(F32)


def _sgu_fwd(zpre, g_sgu, ws, bs_t, name, deps=()):
    S, F2 = zpre.shape
    F = F2 // 2
    gd = F // A_GROUPS

    windows = 2
    rows = windows * A_CHUNK

    def body(zu_ref, zv_ref, g_ref, ws_ref, b_ref, *rest):
        o_ref = rest[-1]
        vhat, _ = _rms_hat(_gelu(zv_ref[...].astype(F32)))
        vn = (vhat * g_ref[...]).astype(BF16)
        u = _gelu(zu_ref[...].astype(F32))
        mask = _chunk_causal_mask(False)
        for gi in range(A_GROUPS):
            sl = slice(gi * gd, (gi + 1) * gd)
            wm = (ws_ref[gi] * mask).astype(BF16)
            for w in range(windows):
                win = slice(w * A_CHUNK, (w + 1) * A_CHUNK)
                vs = jnp.dot(wm, vn[win, sl], preferred_element_type=F32) + b_ref[:, gi:gi + 1]
                o_ref[win, sl] = (u[win, sl] * vs).astype(BF16)

    return pl.pallas_call(
        body, name=name, grid=(S // rows,),
        in_specs=[pl.BlockSpec((rows, F), lambda i: (i, 0)),
                  pl.BlockSpec((rows, F), lambda i: (i, 1)),
                  pl.BlockSpec((1, F), lambda i: (0, 0)),
                  pl.BlockSpec((A_GROUPS, A_CHUNK, A_CHUNK), lambda i: (0, 0, 0)),
                  pl.BlockSpec((A_CHUNK, A_GROUPS), lambda i: (0, 0))] + [HBM_SPEC] * len(deps),
        out_specs=pl.BlockSpec((rows, F), lambda i: (i, 0)),
        out_shape=_sds((S, F), BF16), compiler_params=_cparams(1))(zpre, zpre, g_sgu, ws, bs_t, *deps)


def _sgu_bwd(zpre, duv, g_sgu, ws, ws_t, bs_t, name):
    S, F2 = zpre.shape
    F = F2 // 2
    gd = F // A_GROUPS

    def body(zu_ref, zv_ref, duv_ref, g_ref, ws_ref, wst_ref, b_ref, dz_ref, dg_ref, dws_ref, dbs_ref, dvn_ref):
        @pl.when(pl.program_id(0) == 0)
        def _():
            dg_ref[...] = jnp.zeros_like(dg_ref)
            dws_ref[...] = jnp.zeros_like(dws_ref)
            dbs_ref[...] = jnp.zeros_like(dbs_ref)

        gv = g_ref[...]
        u, u_grad = _gelu_and_grad(zu_ref[...].astype(F32))
        v, v_grad = _gelu_and_grad(zv_ref[...].astype(F32))
        vhat, r = _rms_hat(v)
        vn = (vhat * gv).astype(BF16)
        duv_v = duv_ref[...].astype(F32)
        dvs = duv_v * u
        dvs_b = dvs.astype(BF16)
        mask = _chunk_causal_mask(False)
        mask_t = _chunk_causal_mask(True)
        for gi in range(A_GROUPS):
            sl = slice(gi * gd, (gi + 1) * gd)
            wm = (ws_ref[gi] * mask).astype(BF16)
            vs = jnp.dot(wm, vn[:, sl], preferred_element_type=F32) + b_ref[:, gi:gi + 1]
            dz_ref[:, sl] = ((duv_v[:, sl] * vs) * u_grad[:, sl]).astype(BF16)
            dws_ref[gi] += lax.dot_general(dvs_b[:, sl], vn[:, sl], NT_DIMS, preferred_element_type=F32) * mask
            dbs_ref[gi] += jnp.broadcast_to(jnp.sum(dvs[:, sl], axis=1, keepdims=True), (A_CHUNK, A_CHUNK))
            wm_t = (wst_ref[gi] * mask_t).astype(BF16)
            dvn_ref[:, sl] = jnp.dot(wm_t, dvs_b[:, sl], preferred_element_type=F32)
        dv, dg_part = _rms_bwd(vhat, r, gv, dvn_ref[...])
        dg_ref[...] += jnp.sum(dg_part, axis=0, keepdims=True)
        dz_ref[:, F:] = (dv * v_grad).astype(BF16)

    blk = pl.BlockSpec((A_CHUNK, F), lambda i: (i, 0))
    const3 = pl.BlockSpec((A_GROUPS, A_CHUNK, A_CHUNK), lambda i: (0, 0, 0))
    return pl.pallas_call(
        body, name=name, grid=(S // A_CHUNK,),
        in_specs=[blk, pl.BlockSpec((A_CHUNK, F), lambda i: (i, 1)), blk,
                  pl.BlockSpec((1, F), lambda i: (0, 0)), const3, const3,
                  pl.BlockSpec((A_CHUNK, A_GROUPS), lambda i: (0, 0))],
        out_specs=[pl.BlockSpec((A_CHUNK, F2), lambda i: (i, 0)), pl.BlockSpec((1, F), lambda i: (0, 0)),
                   const3, const3],
        out_shape=[_sds((S, F2), BF16), _sds((1, F), F32), _sds((A_GROUPS, A_CHUNK, A_CHUNK), F32),
                   _sds((A_GROUPS, A_CHUNK, A_CHUNK), F32)],
        scratch_shapes=[pltpu.VMEM((A_CHUNK, F), F32)],
        compiler_params=_cparams(1))(zpre, zpre, duv, g_sgu, ws, ws_t, bs_t)


def _toeplitz_one_hot():
    row = lax.broadcasted_iota(jnp.int32, (TABLE_PAD, DIAGS), 0)
    j = lax.broadcasted_iota(jnp.int32, (TABLE_PAD, DIAGS), 1)
    idx = jnp.clip(KV_PAD + Q_BLOCK - j, -MAX_REL, MAX_REL) + MAX_REL
    return (row == idx).astype(F32)


def _rel_bias_fwd(table, name):
    H = table.shape[0]

    def body(t_ref, o_ref):
        diag = jnp.dot(t_ref[...], _toeplitz_one_hot(), precision=HIGHEST, preferred_element_type=F32)
        k_chunk = lax.broadcasted_iota(jnp.int32, (1, BAND), 1) // CHUNK

        def step(r, carry):
            q_chunk = r // CHUNK
            seen = (k_chunk >= q_chunk) & (k_chunk <= q_chunk + N_LEFT)
            o_ref[r] = pltpu.roll(diag, DIAGS - Q_BLOCK + r, 1)[:, :BAND] + jnp.where(seen, 0.0, NEG_INF)
            return carry

        lax.fori_loop(0, Q_BLOCK, step, 0)

    return pl.pallas_call(body, name=name, out_shape=_sds((Q_BLOCK, H, BAND), F32),
                          compiler_params=pltpu.CompilerParams(vmem_limit_bytes=VMEM_LIMIT_BYTES))(table)


def _rel_bias_bwd(dbias, name):
    H = dbias.shape[1]

    def body(d_ref, o_ref):
        def step(r, acc):
            row = jnp.concatenate([d_ref[r], jnp.zeros((H, DIAGS - BAND), F32)], axis=1)
            return acc + pltpu.roll(row, Q_BLOCK - r, 1)

        diag = lax.fori_loop(0, Q_BLOCK, step, jnp.zeros((H, DIAGS), F32))
        o_ref[...] = lax.dot_general(diag, _toeplitz_one_hot(), NT_DIMS, precision=HIGHEST,
                                     preferred_element_type=F32)

    return pl.pallas_call(body, name=name, out_shape=_sds((H, TABLE_PAD), F32),
                          compiler_params=pltpu.CompilerParams(vmem_limit_bytes=VMEM_LIMIT_BYTES))(dbias)


def _head_rows(t):
    lane = lax.broadcasted_iota(jnp.int32, t.shape, 1)
    zero = jnp.zeros_like(t)
    return jnp.concatenate([jnp.where(lane < HEAD_DIM, t, zero), jnp.where(lane >= HEAD_DIM, t, zero)], axis=0)


def _head_lanes(t2):
    lane = lax.broadcasted_iota(jnp.int32, (Q_BLOCK, t2.shape[1]), 1)
    return jnp.where(lane < HEAD_DIM, t2[:Q_BLOCK], t2[Q_BLOCK:])


def _attn_probs(q2, kb, bias2, block):
    kj = lax.broadcasted_iota(jnp.int32, (1, BAND), 1)
    before_start = jnp.where(block * Q_BLOCK + kj - KV_PAD >= 0, 0.0, NEG_INF)
    s = lax.dot_general(q2 * ATTN_SCALE, kb, NT_DIMS, preferred_element_type=F32) + bias2 + before_start
    e = jnp.exp(s - jnp.max(s, axis=-1, keepdims=True))
    return e / jnp.sum(e, axis=-1, keepdims=True)


def _attn_specs(S):
    lanes = HEADS_PER_BLOCK * HEAD_DIM
    rows = S + KV_PAD
    q_spec = pl.BlockSpec((BLOCKS_PER_STEP * Q_BLOCK, lanes), lambda h, i: (i, h))
    k_spec = pl.BlockSpec((rows, lanes), lambda h, i: (0, h))
    v_spec = pl.BlockSpec((rows, lanes), lambda h, i: (0, N_HEADS // HEADS_PER_BLOCK + h))
    b_spec = pl.BlockSpec((HEADS_PER_BLOCK, Q_BLOCK, BAND), lambda h, i: (h, 0, 0))
    return q_spec, k_spec, v_spec, b_spec


def _attn_fwd(q, kvp, bias, name, deps=()):
    S, HD = q.shape
    q_spec, k_spec, v_spec, b_spec = _attn_specs(S)

    def body(q_ref, k_ref, v_ref, b_ref, *rest):
        o_ref = rest[-1]
        for b in range(BLOCKS_PER_STEP):
            block = pl.program_id(1) * BLOCKS_PER_STEP + b
            rows = slice(b * Q_BLOCK, (b + 1) * Q_BLOCK)
            band = pl.ds(pl.multiple_of(block * Q_BLOCK, Q_BLOCK), BAND)
            p = _attn_probs(_head_rows(q_ref[rows, :]), k_ref[band, :], b_ref[...].reshape(2 * Q_BLOCK, BAND), block)
            o2 = jnp.dot(p.astype(BF16), v_ref[band, :], preferred_element_type=F32)
            o_ref[rows, :] = _head_lanes(o2).astype(BF16)

    return pl.pallas_call(
        body, name=name, grid=(N_HEADS // HEADS_PER_BLOCK, S // (BLOCKS_PER_STEP * Q_BLOCK)),
        in_specs=[q_spec, k_spec, v_spec, b_spec] + [HBM_SPEC] * len(deps), out_specs=q_spec,
        out_shape=_sds((S, HD), BF16), compiler_params=_cparams(2))(q, kvp, kvp, bias, *deps)


def _attn_bwd(q, kvp, bias, do, dkv_prev, name):
    S, HD = q.shape
    lanes = HEADS_PER_BLOCK * HEAD_DIM
    q_spec, k_spec, v_spec, b_spec = _attn_specs(S)
    dkv_spec = pl.BlockSpec((2, S + KV_PAD, lanes), lambda h, i: (0, 0, h))
    prev = [] if dkv_prev is None else [dkv_prev]
    n_steps = S // (BLOCKS_PER_STEP * Q_BLOCK)

    def body(q_ref, k_ref, v_ref, b_ref, do_ref, *rest):
        dq_ref, dkv_ref, db_ref = rest[len(prev):len(prev) + 3]

        @pl.when(pl.program_id(1) == 0)
        def _():
            dkv_ref[...] = rest[0][...] if prev else jnp.zeros_like(dkv_ref)
            db_ref[...] = jnp.zeros_like(db_ref)

        db = jnp.zeros((2 * Q_BLOCK, BAND), F32)
        for b in range(BLOCKS_PER_STEP):
            block = pl.program_id(1) * BLOCKS_PER_STEP + b
            rows = slice(b * Q_BLOCK, (b + 1) * Q_BLOCK)
            band = pl.ds(pl.multiple_of(block * Q_BLOCK, Q_BLOCK), BAND)
            kb, vb = k_ref[band, :], v_ref[band, :]
            q2, do2 = _head_rows(q_ref[rows, :]), _head_rows(do_ref[rows, :])
            p = _attn_probs(q2, kb, b_ref[...].reshape(2 * Q_BLOCK, BAND), block)
            dp = lax.dot_general(do2, vb, NT_DIMS, preferred_element_type=F32)
            ds = p * (dp - jnp.sum(dp * p, axis=-1, keepdims=True))
            db = db + ds
            ds_b = (ds * ATTN_SCALE).astype(BF16)
            dq_ref[rows, :] = _head_lanes(jnp.dot(ds_b, kb, preferred_element_type=F32)).astype(BF16)
            dkv_ref[0, band, :] += lax.dot_general(ds_b, q2, TN_DIMS, preferred_element_type=F32)
            dkv_ref[1, band, :] += lax.dot_general(p.astype(BF16), do2, TN_DIMS, preferred_element_type=F32)
        db_ref[...] += db.reshape(HEADS_PER_BLOCK, Q_BLOCK, BAND)

        if prev:
            @pl.when(pl.program_id(1) == n_steps - 1)
            def _():
                rest[-1][...] = dkv_ref[:, KV_PAD:, :].astype(BF16)

    return pl.pallas_call(
        body, name=name, grid=(N_HEADS // HEADS_PER_BLOCK, n_steps),
        in_specs=[q_spec, k_spec, v_spec, b_spec, q_spec] + [dkv_spec] * len(prev),
        out_specs=[q_spec, dkv_spec, b_spec] + [pl.BlockSpec((2, S, lanes), lambda h, i: (0, 0, h))] * len(prev),
        out_shape=[_sds((S, HD), BF16), _sds((2, S + KV_PAD, HD), F32), _sds((N_HEADS, Q_BLOCK, BAND), F32)]
        + [_sds((2, S, HD), BF16)] * len(prev),
        compiler_params=_cparams(2))(q, kvp, kvp, bias, do, *prev)


def _loss_head(x, g, target, name, tm=512):
    S, D = x.shape

    def body(x_ref, g_ref, t_ref, loss_ref, dx_ref, dxb_ref, dg_ref):
        @pl.when(pl.program_id(0) == 0)
        def _():
            loss_ref[...] = jnp.zeros_like(loss_ref)
            dg_ref[...] = jnp.zeros_like(dg_ref)

        xhat, r = _rms_hat(x_ref[...])
        gv = g_ref[...]
        err = xhat * gv - t_ref[...]
        loss_ref[...] += 0.5 * jnp.sum(jnp.mean(err * err, axis=-1, keepdims=True))
        dx, dgp = _rms_bwd(xhat, r, gv, err * (1.0 / D))
        dx_ref[...] = dx
        dxb_ref[...] = dx.astype(BF16)
        dg_ref[...] += jnp.sum(dgp, axis=0, keepdims=True)

    row = pl.BlockSpec((tm, D), lambda i: (i, 0))
    vec = pl.BlockSpec((1, D), lambda i: (0, 0))
    return pl.pallas_call(
        body, name=name, grid=(S // tm,), in_specs=[row, vec, row],
        out_specs=[pl.BlockSpec((8, 128), lambda i: (0, 0)), row, row, vec],
        out_shape=[_sds((8, 128), F32), _sds((S, D), F32), _sds((S, D), BF16), _sds((1, D), F32)],
        compiler_params=_cparams(1))(x, g, target)


def _place():
    x, y, c = lax.axis_index("x"), lax.axis_index("y"), lax.axis_index("c")
    chips = [(1 - x, y), (x, 1 - y), (1 - x, 1 - y)]
    return x, y, c, chips


def _half_rows(c, r):
    return pl.ds(pl.multiple_of(c * (r // 2), 8), r // 2)


HBM_SPEC = pl.BlockSpec(memory_space=pl.ANY)


STRICT_HBM_SPEC = pl.BlockSpec(memory_space=pltpu.HBM)
SEM_SPEC = pl.BlockSpec(memory_space=pltpu.SEMAPHORE)
EFFECT = pltpu.SideEffectType.DATAFLOW_SIDE_EFFECTING


def _peers(x, y, c):
    out = []
    for k in range(1, N_DEV):
        px, py, pc = (x + ((k >> 2) & 1)) % 2, (y + ((k >> 1) & 1)) % 2, (c + (k & 1)) % 2
        out.append(((px, py, pc), 2 * px + py, pc, 4 * px + 2 * py + pc))
    return out


def _token_spec():
    return pl.BlockSpec(memory_space=pltpu.VMEM)


def _hbm(a):
    return pltpu.with_memory_space_constraint(a, pltpu.HBM)


def _slab_half(ref, chip, core):
    return ref.at[2 * chip[0] + chip[1], _half_rows(core, ref.shape[1]), :]


def _allgather_start(slabs, name):
    n = len(slabs)

    def body(*refs):
        src, send, recv, token = refs[:n], refs[n], refs[n + 1], refs[-1]
        x, y, c, chips = _place()
        for a in range(n):
            own = _slab_half(src[a], (x, y), c)
            for j, chip in enumerate(chips):
                pltpu.make_async_remote_copy(src_ref=own, dst_ref=own, send_sem=send.at[3 * a + j], recv_sem=recv.at[3 * a + j],
                                             device_id=(*chip, c), device_id_type=MESH).start()
        token[...] = jnp.zeros_like(token)

    sems = pltpu.SemaphoreType.DMA((3 * n,))
    send, recv, *flying, token = pl.pallas_call(
        body, name=name, in_specs=[STRICT_HBM_SPEC] * n,
        out_shape=(sems, sems, *[pltpu.HBM(s.shape, s.dtype) for s in slabs], _sds((8, 128), F32)),
        out_specs=(SEM_SPEC, SEM_SPEC, *[STRICT_HBM_SPEC] * n, _token_spec()),
        input_output_aliases={a: a + 2 for a in range(n)},
        compiler_params=pltpu.CompilerParams(has_side_effects=EFFECT))(*[_hbm(s) for s in slabs])
    return send, recv, flying, token


def _allgather_relay(flying, send, recv, first, after, name):
    n = len(flying)

    def body(*refs):
        src, send_ref, recv_ref = refs[:n], refs[n], refs[n + 1]
        send2, recv2, token = refs[n + 3], refs[n + 4], refs[-1]
        token[...] = jnp.zeros_like(token)
        x, y, c, chips = _place()
        for a in range(n):
            for j, chip in enumerate(chips):
                cp = pltpu.make_async_remote_copy(
                    src_ref=_slab_half(src[a], (x, y), c), dst_ref=_slab_half(src[a], chip, c),
                    send_sem=send_ref.at[3 * (first + a) + j], recv_sem=recv_ref.at[3 * (first + a) + j],
                    device_id=(*chip, c), device_id_type=MESH)
                cp.wait_send()
                cp.wait_recv()
        for a in range(n):
            for j, chip in enumerate(chips):
                landed = _slab_half(src[a], chip, c)
                pltpu.make_async_remote_copy(src_ref=landed, dst_ref=landed, send_sem=send2.at[3 * a + j],
                                             recv_sem=recv2.at[3 * a + j], device_id=(x, y, 1 - c),
                                             device_id_type=MESH).start()

    sems = pltpu.SemaphoreType.DMA((3 * n,))
    send2, recv2, *relayed, token = pl.pallas_call(
        body, name=name, in_specs=[STRICT_HBM_SPEC] * n + [SEM_SPEC, SEM_SPEC, HBM_SPEC],
        out_shape=(sems, sems, *[pltpu.HBM(s.shape, s.dtype) for s in flying], _sds((8, 128), F32)),
        out_specs=(SEM_SPEC, SEM_SPEC, *[STRICT_HBM_SPEC] * n, _token_spec()),
        input_output_aliases={a: a + 2 for a in range(n)},
        compiler_params=pltpu.CompilerParams(has_side_effects=EFFECT))(*flying, send, recv, after)
    return send2, recv2, relayed, token


def _allgather_wait(relayed, send2, recv2, after, name):
    n = len(relayed)

    def body(*refs):
        src, send_ref, recv_ref = refs[:n], refs[n], refs[n + 1]
        x, y, c, chips = _place()
        for a in range(n):
            for j, chip in enumerate(chips):
                cp = pltpu.make_async_remote_copy(
                    src_ref=_slab_half(src[a], chip, c), dst_ref=_slab_half(src[a], chip, 1 - c),
                    send_sem=send_ref.at[3 * a + j], recv_sem=recv_ref.at[3 * a + j],
                    device_id=(x, y, 1 - c), device_id_type=MESH)
                cp.wait_send()
                cp.wait_recv()

    return pl.pallas_call(
        body, name=name, in_specs=[STRICT_HBM_SPEC] * n + [SEM_SPEC, SEM_SPEC, HBM_SPEC],
        out_shape=tuple(pltpu.HBM(s.shape, s.dtype) for s in relayed), out_specs=tuple([STRICT_HBM_SPEC] * n),
        input_output_aliases={a: a for a in range(n)},
        compiler_params=pltpu.CompilerParams(has_side_effects=EFFECT))(*relayed, send2, recv2, after)


def _allgather_small(small, name):
    def body(sm, osm, send, recv, local):
        x, y, c, chips = _place()
        own = pltpu.make_async_copy(sm, osm.at[2 * x + y], local)
        own.start()
        cps = [pltpu.make_async_remote_copy(src_ref=sm, dst_ref=osm.at[2 * x + y], send_sem=send.at[j],
                                            recv_sem=recv.at[j], device_id=(*chip, c), device_id_type=MESH)
               for j, chip in enumerate(chips)]
        for cp in cps:
            cp.start()
        for j, chip in enumerate(chips):
            got = osm.at[2 * chip[0] + chip[1]]
            pltpu.make_async_remote_copy(src_ref=got, dst_ref=got, send_sem=send.at[j], recv_sem=recv.at[j],
                                         device_id=(x, y, c), device_id_type=MESH).wait_recv()
        for cp in cps:
            cp.wait_send()
        own.wait()

    return pl.pallas_call(
        body, name=name, in_specs=[pl.BlockSpec(memory_space=pltpu.VMEM)], out_specs=HBM_SPEC,
        out_shape=_sds((N_CHIPS, *small.shape), small.dtype),
        scratch_shapes=[pltpu.SemaphoreType.DMA((3,)), pltpu.SemaphoreType.DMA((3,)), pltpu.SemaphoreType.DMA])(small)


def _reduce_start(grads, name):
    n = len(grads)

    def body(*refs):
        src, land, send, recv, token = refs[:n], refs[n:2 * n], refs[2 * n], refs[2 * n + 1], refs[-1]
        x, y, c, _ = _place()
        me = 4 * x + 2 * y + c
        for a in range(n):
            for k, (peer, p_chip, p_core, _) in enumerate(_peers(x, y, c)):
                pltpu.make_async_remote_copy(
                    src_ref=src[a].at[p_chip, _half_rows(p_core, src[a].shape[1]), :], dst_ref=land[a].at[me],
                    send_sem=send.at[(N_DEV - 1) * a + k], recv_sem=recv.at[(N_DEV - 1) * a + k],
                    device_id=peer, device_id_type=MESH).start()
        token[...] = jnp.zeros_like(token)

    lands = [lax.empty((N_DEV, g.shape[1] // 2, g.shape[2]), BF16) for g in grads]
    sems = pltpu.SemaphoreType.DMA(((N_DEV - 1) * n,))
    shapes = [pltpu.HBM(a.shape, a.dtype) for a in grads + lands]
    send, recv, *flying, token = pl.pallas_call(
        body, name=name, in_specs=[STRICT_HBM_SPEC] * (2 * n),
        out_shape=(sems, sems, *shapes, _sds((8, 128), F32)),
        out_specs=(SEM_SPEC, SEM_SPEC, *[STRICT_HBM_SPEC] * (2 * n), _token_spec()),
        input_output_aliases={a: a + 2 for a in range(2 * n)},
        compiler_params=pltpu.CompilerParams(has_side_effects=EFFECT))(*[_hbm(a) for a in grads + lands])
    return send, recv, flying[:n], flying[n:], token


def _reduce_wait(started, after, name):
    sizes = [len(grads) for _, _, grads, _ in started]
    n_arr = 2 * sum(sizes)

    def body(*refs):
        x, y, c, _ = _place()
        at = 0
        for s, n in enumerate(sizes):
            src, land = refs[at:at + n], refs[at + n:at + 2 * n]
            send_ref, recv_ref = refs[n_arr + 2 * s], refs[n_arr + 2 * s + 1]
            at += 2 * n
            for a in range(n):
                for k, (peer, p_chip, p_core, p_dev) in enumerate(_peers(x, y, c)):
                    cp = pltpu.make_async_remote_copy(
                        src_ref=src[a].at[p_chip, _half_rows(p_core, src[a].shape[1]), :], dst_ref=land[a].at[p_dev],
                        send_sem=send_ref.at[(N_DEV - 1) * a + k], recv_sem=recv_ref.at[(N_DEV - 1) * a + k],
                        device_id=peer, device_id_type=MESH)
                    cp.wait_send()
                    cp.wait_recv()

    arrays, sems = [], []
    for send, recv, grads, lands in started:
        arrays += list(grads) + list(lands)
        sems += [send, recv]
    out = pl.pallas_call(
        body, name=name, in_specs=[STRICT_HBM_SPEC] * n_arr + [SEM_SPEC] * len(sems) + [HBM_SPEC],
        out_shape=tuple(pltpu.HBM(a.shape, a.dtype) for a in arrays), out_specs=tuple([STRICT_HBM_SPEC] * n_arr),
        input_output_aliases={a: a for a in range(n_arr)},
        compiler_params=pltpu.CompilerParams(has_side_effects=EFFECT))(*arrays, *sems, after)
    result, at = [], 0
    for n in sizes:
        result.append((out[at:at + n], out[at + n:at + 2 * n]))
        at += 2 * n
    return result


def _reduce_sum(grad, land, place, name, deps=()):
    _, r2, C = land.shape
    tr = _row_tile(r2, C, 4, 1024 * 1024, 16)
    nb = r2 // tr

    def body(place_ref, own_ref, *rest):
        del place_ref
        acc = own_ref[...].astype(F32)
        for ref in rest[:N_DEV - 1]:
            acc = acc + ref[...].astype(F32)
        rest[-1][...] = acc

    def from_dev(k):
        return pl.BlockSpec((None, tr, C), lambda i, place_ref: ((place_ref[2] + k) % N_DEV, i, 0))

    grid_spec = pltpu.PrefetchScalarGridSpec(
        num_scalar_prefetch=1, grid=(nb,),
        in_specs=[pl.BlockSpec((None, tr, C), lambda i, place_ref: (place_ref[0], place_ref[1] * nb + i, 0))]
        + [from_dev(k) for k in range(1, N_DEV)] + [HBM_SPEC] * len(deps),
        out_specs=pl.BlockSpec((tr, C), lambda i, place_ref: (place_ref[1] * nb + i, 0)))
    return pl.pallas_call(body, name=name, grid_spec=grid_spec, out_shape=_sds((2 * r2, C), F32),
                          compiler_params=_cparams(1))(place, grad, *[land] * (N_DEV - 1), *deps)


def _join_start(halves, name):
    n = len(halves)

    def body(*refs):
        src, send, recv, token = refs[:n], refs[n], refs[n + 1], refs[-1]
        x, y, c, _ = _place()
        for w in range(n):
            mine = src[w].at[_half_rows(c, src[w].shape[0]), :]
            pltpu.make_async_remote_copy(src_ref=mine, dst_ref=mine, send_sem=send.at[w], recv_sem=recv.at[w],
                                         device_id=(x, y, 1 - c), device_id_type=MESH).start()
        token[...] = jnp.zeros_like(token)

    sems = pltpu.SemaphoreType.DMA((n,))
    send, recv, *flying, token = pl.pallas_call(
        body, name=name, in_specs=[STRICT_HBM_SPEC] * n,
        out_shape=(sems, sems, *[pltpu.HBM(a.shape, a.dtype) for a in halves], _sds((8, 128), F32)),
        out_specs=(SEM_SPEC, SEM_SPEC, *[STRICT_HBM_SPEC] * n, _token_spec()),
        input_output_aliases={w: w + 2 for w in range(n)},
        compiler_params=pltpu.CompilerParams(has_side_effects=EFFECT))(*[_hbm(a) for a in halves])
    return send, recv, flying, token


def _join_wait(flying, send, recv, after, name):
    n = len(flying)

    def body(*refs):
        src, send_ref, recv_ref = refs[:n], refs[n], refs[n + 1]
        x, y, c, _ = _place()
        for w in range(n):
            cp = pltpu.make_async_remote_copy(
                src_ref=src[w].at[_half_rows(c, src[w].shape[0]), :],
                dst_ref=src[w].at[_half_rows(1 - c, src[w].shape[0]), :], send_sem=send_ref.at[w],
                recv_sem=recv_ref.at[w], device_id=(x, y, 1 - c),
                device_id_type=MESH)
            cp.wait_send()
            cp.wait_recv()

    return pl.pallas_call(
        body, name=name, in_specs=[STRICT_HBM_SPEC] * n + [SEM_SPEC, SEM_SPEC, HBM_SPEC],
        out_shape=tuple(pltpu.HBM(a.shape, a.dtype) for a in flying), out_specs=tuple([STRICT_HBM_SPEC] * n),
        input_output_aliases={w: w for w in range(n)},
        compiler_params=pltpu.CompilerParams(has_side_effects=EFFECT))(*flying, send, recv, after)


def _gather_small(packed, name):
    def body(p_ref, out, send, recv, local):
        x, y, c, _ = _place()
        me = 4 * x + 2 * y + c
        own = pltpu.make_async_copy(p_ref, out.at[me], local)
        own.start()
        cps = []
        for k in range(1, N_DEV):
            fx, fy, fc = (k >> 2) & 1, (k >> 1) & 1, k & 1
            peer = ((x + fx) % 2, (y + fy) % 2, (c + fc) % 2)
            cps.append(pltpu.make_async_remote_copy(src_ref=p_ref, dst_ref=out.at[me], send_sem=send.at[k - 1],
                                                    recv_sem=recv.at[k - 1], device_id=peer, device_id_type=MESH))
        for cp in cps:
            cp.start()
        for k in range(1, N_DEV):
            fx, fy, fc = (k >> 2) & 1, (k >> 1) & 1, k & 1
            src = out.at[4 * ((x + fx) % 2) + 2 * ((y + fy) % 2) + (c + fc) % 2]
            pltpu.make_async_remote_copy(src_ref=src, dst_ref=src, send_sem=send.at[k - 1], recv_sem=recv.at[k - 1],
                                         device_id=(x, y, c), device_id_type=MESH).wait_recv()
        for cp in cps:
            cp.wait_send()
        own.wait()

    return pl.pallas_call(
        body, name=name, in_specs=[pl.BlockSpec(memory_space=pltpu.VMEM)], out_specs=HBM_SPEC,
        out_shape=_sds((N_DEV, *packed.shape), F32),
        scratch_shapes=[pltpu.SemaphoreType.DMA((N_DEV - 1,)), pltpu.SemaphoreType.DMA((N_DEV - 1,)),
                        pltpu.SemaphoreType.DMA])(packed)


def _sum_devices(gathered, name):
    _, R, C = gathered.shape

    def body(g_ref, o_ref):
        acc = g_ref[0]
        for d in range(1, N_DEV):
            acc = acc + g_ref[d]
        o_ref[...] = acc

    tr = 8
    return pl.pallas_call(
        body, name=name, grid=(R // tr,), in_specs=[pl.BlockSpec((N_DEV, tr, C), lambda i: (0, i, 0))],
        out_specs=pl.BlockSpec((tr, C), lambda i: (i, 0)), out_shape=_sds((R, C), F32),
        compiler_params=_cparams(1))(gathered)


def _gather_start(packed, name):
    def body(src, land, send, recv, *rest):
        x, y, c, _ = _place()
        for k, (peer, _, _, _) in enumerate(_peers(x, y, c)):
            pltpu.make_async_remote_copy(src_ref=src, dst_ref=land.at[4 * x + 2 * y + c], send_sem=send.at[k],
                                         recv_sem=recv.at[k], device_id=peer, device_id_type=MESH).start()
        rest[-1][...] = jnp.zeros_like(rest[-1])

    land = lax.empty((N_DEV, *packed.shape), F32)
    sems = pltpu.SemaphoreType.DMA((N_DEV - 1,))
    return pl.pallas_call(
        body, name=name, in_specs=[STRICT_HBM_SPEC] * 2,
        out_shape=(sems, sems, pltpu.HBM(packed.shape, F32), pltpu.HBM(land.shape, F32), _sds((8, 128), F32)),
        out_specs=(SEM_SPEC, SEM_SPEC, STRICT_HBM_SPEC, STRICT_HBM_SPEC, _token_spec()),
        input_output_aliases={0: 2, 1: 3},
        compiler_params=pltpu.CompilerParams(has_side_effects=EFFECT))(_hbm(packed), _hbm(land))


def _gather_wait(started, after, name):
    n = len(started)

    def body(*refs):
        x, y, c, _ = _place()
        for s in range(n):
            src, land, send, recv = refs[2 * s], refs[2 * s + 1], refs[2 * n + 2 * s], refs[2 * n + 2 * s + 1]
            for k, (peer, _, _, p_dev) in enumerate(_peers(x, y, c)):
                cp = pltpu.make_async_remote_copy(src_ref=src, dst_ref=land.at[p_dev], send_sem=send.at[k],
                                                  recv_sem=recv.at[k], device_id=peer,
                                                  device_id_type=MESH)
                cp.wait_send()
                cp.wait_recv()

    arrays = [a for _, _, packed, land in started for a in (packed, land)]
    sems = [s for send, recv, _, _ in started for s in (send, recv)]
    out = pl.pallas_call(
        body, name=name, in_specs=[STRICT_HBM_SPEC] * (2 * n) + [SEM_SPEC] * (2 * n) + [HBM_SPEC],
        out_shape=tuple(pltpu.HBM(a.shape, a.dtype) for a in arrays), out_specs=tuple([STRICT_HBM_SPEC] * (2 * n)),
        input_output_aliases={a: a for a in range(2 * n)},
        compiler_params=pltpu.CompilerParams(has_side_effects=EFFECT))(*arrays, *sems, after)
    return [(out[2 * s], out[2 * s + 1]) for s in range(n)]


def _sum_gathered(packed, land, device, name):
    R, C = packed.shape
    tr = 8

    def body(dev_ref, own_ref, *rest):
        me = dev_ref[0]
        acc = None
        for d in range(N_DEV):
            term = jnp.where(me == d, own_ref[...], rest[d][...])
            acc = term if acc is None else acc + term
        rest[-1][...] = acc

    def slab(d):
        return pl.BlockSpec((None, tr, C), lambda i, dev_ref: (jnp.where(dev_ref[0] == d, (d + 1) % N_DEV, d), i, 0))

    grid_spec = pltpu.PrefetchScalarGridSpec(
        num_scalar_prefetch=1, grid=(R // tr,),
        in_specs=[pl.BlockSpec((tr, C), lambda i, dev_ref: (i, 0))] + [slab(d) for d in range(N_DEV)],
        out_specs=pl.BlockSpec((tr, C), lambda i, dev_ref: (i, 0)))
    return pl.pallas_call(body, name=name, grid_spec=grid_spec, out_shape=_sds((R, C), F32),
                          compiler_params=_cparams(1))(device, packed, *[land] * N_DEV)


def _pack_small(arrays):
    rows = []
    for a in arrays:
        flat = a.reshape(-1)
        pad = (-flat.shape[0]) % SMALL_COLS
        rows.append(jnp.pad(flat, (0, pad)).reshape(-1, SMALL_COLS))
    packed = jnp.concatenate(rows, axis=0)
    return jnp.pad(packed, ((0, (-packed.shape[0]) % 8), (0, 0)))


def _unpack_small(packed, shapes):
    out, row = [], 0
    for shape in shapes:
        size = math.prod(shape)
        n_rows = -(-size // SMALL_COLS)
        out.append(packed[row:row + n_rows].reshape(-1)[:size].reshape(shape))
        row += n_rows
    return out


def kernel(x, a_norm, a_w_in, a_sgu_norm, a_w_spatial, a_b_spatial, a_w_out, kv_norm, w_kv, b_norm, b_w_q, b_rel_bias, b_w_o, ffn_norm, ffn_w_gate_up, ffn_w_down, final_norm, loss_target, m_a_norm, m_a_w_in, m_a_sgu_norm, m_a_w_spatial, m_a_b_spatial, m_a_w_out, m_kv_norm, m_w_kv, m_b_norm, m_b_w_q, m_b_rel_bias, m_b_w_o, m_ffn_norm, m_ffn_w_gate_up, m_ffn_w_down, m_final_norm, v_a_norm, v_a_w_in, v_a_sgu_norm, v_a_w_spatial, v_a_b_spatial, v_a_w_out, v_kv_norm, v_w_kv, v_b_norm, v_b_w_q, v_b_rel_bias, v_b_w_o, v_ffn_norm, v_ffn_w_gate_up, v_ffn_w_down, v_final_norm):
    S, D = x.shape[1], x.shape[2]
    n_a = a_w_in.shape[0]
    n_b = b_w_q.shape[0]
    depth = ffn_w_gate_up.shape[0]
    xi, yi, ci = lax.axis_index("x"), lax.axis_index("y"), lax.axis_index("c")
    chip = 2 * xi + yi

    place = jnp.stack([chip, ci, 2 * chip + ci]).astype(jnp.int32)
    stacked = {"a_w_in": a_w_in, "a_w_out": a_w_out, "w_kv": w_kv[None], "b_w_q": b_w_q, "b_w_o": b_w_o,
               "ffn_w_gate_up": ffn_w_gate_up, "ffn_w_down": ffn_w_down}
    groups = []
    for layer in range(depth):
        if layer == 0 and n_a > 0:
            groups += [[("a_w_in", 0)], [("a_w_out", 0)]]
        elif layer < n_a:
            groups.append([("a_w_in", layer), ("a_w_out", layer)])
        elif layer == n_a:
            groups.append([("w_kv", 0), ("b_w_q", 0), ("b_w_o", 0)])
        else:
            groups.append([("b_w_q", layer - n_a), ("b_w_o", layer - n_a)])
        groups.append([("ffn_w_gate_up", layer), ("ffn_w_down", layer)])
    units = [u for group in groups for u in group]
    n_early = len(groups[0])
    slabs = [_cast_slab(stacked[k], l, place[:1], f"cast_{k}_{l}") for k, l in units[:n_early]]
    early = _allgather_start(slabs, "allgather_start_first")
    slabs = [_cast_slab(stacked[k], l, place[:1], f"cast_{k}_{l}", deps=(early[3],)) for k, l in units[n_early:]]
    late = _allgather_start(slabs, "allgather_start_rest")
    na_w, ns_w = a_norm.shape[1], a_sgu_norm.shape[1]
    small_g = _allgather_small(jnp.concatenate([a_norm, a_sgu_norm], axis=1), "allgather_small")
    a_norm_f = small_g[:, :, :na_w].transpose(1, 0, 2).reshape(n_a, N_CHIPS * na_w)
    a_sgu_f = small_g[:, :, na_w:].transpose(1, 0, 2).reshape(n_a, N_CHIPS * ns_w)
    W, relayed = {}, []

    def relay(after):
        if len(relayed) == len(groups):
            return ()
        index = sum(len(g) for g in groups[:len(relayed)])
        group = groups[len(relayed)]
        (send, recv, flying, _), first = (early, index) if index < n_early else (late, index - n_early)
        relayed.append(_allgather_relay(flying[first:first + len(group)], send, recv, first, after,
                                        f"allgather_relay_{len(relayed)}"))
        return (relayed[-1][3],)

    n_gathered = [0]

    def gathered(after):
        index = n_gathered[0]
        send2, recv2, arrays, _ = relayed[index]
        W.update(zip(groups[index], _allgather_wait(arrays, send2, recv2, after, f"allgather_wait_{index}")))
        n_gathered[0] += 1

    xc = x.reshape(S, D)
    saved = []
    kvp = x_kv = h_kv = None
    relay(late[3])
    order = ()
    for layer in range(depth):
        rec = {"x_in": xc}
        gathered(xc)
        if layer < n_a:
            i = layer
            rec["zpre"], rec["h"] = _norm_matmul(xc, a_norm_f[i][None], W["a_w_in", i], BF16, f"a{i}_in", deps=order)
            order = relay(rec["h"])
            rec["uv"] = _sgu_fwd(rec["zpre"], a_sgu_f[i][None], a_w_spatial[i], a_b_spatial[i].T, f"a{i}_sgu",
                                 deps=order)
            order = ()
            if ("a_w_out", i) not in W:
                gathered(rec["uv"])
                order = relay(rec["uv"])
            xm = _matmul_res(rec["uv"], W["a_w_out", i], xc, f"a{i}_out", tm=512, deps=order)
        else:
            i = layer - n_a
            if i == 0:
                kvp, h_kv = _norm_matmul(xc, kv_norm[None], W["w_kv", 0], BF16, "kv_proj", tm=KV_PAD, zero_rows=KV_PAD)
                x_kv = xc
            rec["q"], rec["h"] = _norm_matmul(xc, b_norm[i][None], W["b_w_q", i], BF16, f"b{i}_q", row_sharded=True)
            order = relay(rec["h"])
            table = jnp.pad(b_rel_bias[i], ((0, 0), (0, TABLE_PAD - b_rel_bias.shape[2])))
            rec["bias"] = _rel_bias_fwd(table, f"b{i}_bias").transpose(1, 0, 2)
            rec["o"] = _attn_fwd(rec["q"], kvp, rec["bias"], f"b{i}_attn", deps=order)
            xm = _matmul_res(rec["o"], W["b_w_o", i], xc, f"b{i}_o", tm=512)
        rec["x_mid"] = xm
        gathered(xm)
        rec["gu"], rec["h_f"] = _norm_matmul(xm, ffn_norm[layer][None], W["ffn_w_gate_up", layer], BF16, f"f{layer}_in")
        order = relay(rec["h_f"])
        xc = _matmul_res(rec["gu"], W["ffn_w_down", layer], xm, f"f{layer}_out", swiglu=True, deps=order, tm=512)
        order = ()
        saved.append(rec)

    loss_tile, dx, dxb, d_final = _loss_head(xc, final_norm[None], loss_target.reshape(S, D), "loss_head")

    started = []
    small_started = []
    pending = []

    held = []

    def weight_grad(unit, hold=False, **kw):
        full = (N_CHIPS,) + tuple(stacked[unit[0]].shape[1:])
        g = _matmul_tn(out_shape=full, name=f"d_{unit[0]}_{unit[1]}", deps=tuple(pending), **kw)
        pending.clear()
        held.append((unit, g))
        if hold:
            return ()
        send, recv, flying_g, flying_land, token = _reduce_start([g for _, g in held],
                                                                 f"reduce_start_{unit[0]}_{unit[1]}")
        started.append(([u for u, _ in held], send, recv, flying_g, flying_land))
        held.clear()
        return (token,)

    tt = 512
    tw = min(2048, S)
    tb = min(NORMBWD_ROWS, S)
    row_a = lambda w, rows=tw: pl.BlockSpec((rows, w), lambda o, t: (t, 0))
    d_ffn_norm, d_b_norm, d_a_norm, d_a_sgu = [None] * depth, [None] * n_b, [None] * n_a, [None] * n_a
    d_ws, d_bs, d_rel = [None] * n_a, [None] * n_a, [None] * n_b
    dkv = None
    first = lambda ref: ref[...]
    for layer in reversed(range(depth)):
        rec = saved[layer]
        r_d = ffn_w_down.shape[1]
        half_f = 2 * r_d
        token = weight_grad(
            ("ffn_w_down", layer), a_ops=[rec["gu"], rec["gu"]],
            a_specs=[pl.BlockSpec((tt, half_f), lambda o, t: (t, o)), pl.BlockSpec((tt, half_f), lambda o, t: (t, 2 + o))],
            a_fn=lambda g_ref, u_ref: _swiglu(g_ref[...].astype(F32), u_ref[...].astype(F32)).astype(BF16),
            b_op=dxb, b_spec=row_a(D, tt), out_spec=pl.BlockSpec((2, r_d, D), lambda o, t: (o, 0, 0)),
            acc_shape=(half_f, D), n_outer=2, tt=tt, hold=True)
        dgu = _nt_swiglu_bwd(dxb, W["ffn_w_down", layer], rec["gu"], f"f{layer}_dgu", deps=token)
        nsh = ffn_w_gate_up.shape[2]
        token = weight_grad(
            ("ffn_w_gate_up", layer), a_ops=[rec["h_f"]], a_specs=[row_a(D)], a_fn=first,
            b_op=dgu, b_spec=pl.BlockSpec((None, tw, nsh), lambda o, t: (o // 2, t, o % 2)),
            out_spec=pl.BlockSpec((None, D, nsh), lambda o, t: (o, 0, 0)), acc_shape=(D, nsh), n_outer=N_CHIPS, tt=tw)
        dx, dxb, d_ffn_norm[layer] = _nt_normbwd(
            dgu, pl.BlockSpec((None, tb, nsh), lambda i, k: (k // 2, i, k % 2)),
            W["ffn_w_gate_up", layer], pl.BlockSpec((None, D, nsh), lambda i, k: (k, 0, 0)),
            (D, nsh), N_CHIPS, rec["x_mid"], ffn_norm[layer][None], dx, f"f{layer}_dx", deps=token)
        if layer >= n_a:
            i = layer - n_a
            r_o = b_w_o.shape[1]
            token = weight_grad(
                ("b_w_o", i), a_ops=[rec["o"]], a_specs=[row_a(D)], a_fn=first, b_op=dxb, b_spec=row_a(D),
                out_spec=pl.BlockSpec((N_CHIPS, r_o, D), lambda o, t: (0, 0, 0)), acc_shape=(D, D), n_outer=1, tt=tw,
                hold=True)
            do = _nt_rows(dxb, W["b_w_o", i], N_CHIPS, BF16, f"b{i}_do", deps=token)
            dq, dkv, dbias, *dkv_bf16 = _attn_bwd(rec["q"], kvp, rec["bias"], do, dkv, f"b{i}_attn_bwd")
            d_rel[i] = _rel_bias_bwd(dbias.transpose(1, 0, 2), f"b{i}_dbias")[:, :b_rel_bias.shape[2]]
            token = weight_grad(
                ("b_w_q", i), a_ops=[rec["h"]], a_specs=[row_a(D)], a_fn=first, b_op=dq, b_spec=row_a(D),
                out_spec=pl.BlockSpec((N_CHIPS, r_o, D), lambda o, t: (0, 0, 0)), acc_shape=(D, D), n_outer=1, tt=tw)
            dx, dxb, d_b_norm[i] = _nt_normbwd(
                dq, pl.BlockSpec((tb, D), lambda i_, k: (i_, 0)),
                W["b_w_q", i], pl.BlockSpec((N_CHIPS, r_o, D), lambda i_, k: (0, 0, 0)),
                (D, D), 1, rec["x_in"], b_norm[i][None], dx, f"b{i}_dx", deps=token)
            if i == 0:
                dkv_b = dkv_bf16[0] if dkv_bf16 else dkv[:, KV_PAD:, :].astype(BF16)
                n_kv = w_kv.shape[1]
                token = weight_grad(
                    ("w_kv", 0), a_ops=[h_kv], a_specs=[row_a(D)], a_fn=first,
                    b_op=dkv_b, b_spec=pl.BlockSpec((None, tw, n_kv), lambda o, t: (o // 2, t, o % 2)),
                    out_spec=pl.BlockSpec((None, D, n_kv), lambda o, t: (o, 0, 0)), acc_shape=(D, n_kv),
                    n_outer=N_CHIPS, tt=tw)
                dx, dxb, d_kv_norm = _nt_normbwd(
                    dkv_b, pl.BlockSpec((None, tb, n_kv), lambda i_, k: (k // 2, i_, k % 2)),
                    W["w_kv", 0], pl.BlockSpec((None, D, n_kv), lambda i_, k: (k, 0, 0)),
                    (D, n_kv), N_CHIPS, x_kv, kv_norm[None], dx, "kv_dx", deps=token)
        else:
            i = layer
            r_w = a_w_out.shape[1]
            token = weight_grad(
                ("a_w_out", i), a_ops=[rec["uv"]], a_specs=[row_a(N_CHIPS * r_w, tw // 2)], a_fn=first,
                b_op=dxb, b_spec=row_a(D, tw // 2), out_spec=pl.BlockSpec((N_CHIPS, r_w, D), lambda o, t: (0, 0, 0)),
                acc_shape=(N_CHIPS * r_w, D), n_outer=1, tt=tw // 2, hold=i > 0)
            duv = _nt_rows(dxb, W["a_w_out", i], 2, BF16, f"a{i}_duv", deps=token)
            dz, d_a_sgu[i], d_ws[i], dbs = _sgu_bwd(rec["zpre"], duv, a_sgu_f[i][None], a_w_spatial[i],
                                                  a_w_spatial[i].transpose(0, 2, 1), a_b_spatial[i].T, f"a{i}_sgu_bwd")
            d_bs[i] = dbs[:, :, 0]
            if i == 0:
                batch = [d_a_sgu[0], d_ws[0][None], d_bs[0][None], d_ffn_norm[0]]
                small_started.append((batch, _gather_start(_pack_small(batch), "gather_start_late")))
                pending.append(small_started[-1][1][4])
            n_in = a_w_in.shape[2]
            token = weight_grad(
                ("a_w_in", i), a_ops=[rec["h"]], a_specs=[row_a(D)], a_fn=first,
                b_op=dz, b_spec=pl.BlockSpec((tw, n_in), lambda o, t: (t, o)),
                out_spec=pl.BlockSpec((None, D, n_in), lambda o, t: (o, 0, 0)), acc_shape=(D, n_in), n_outer=N_CHIPS,
                tt=tw)
            dx, dxb, d_a_norm[i] = _nt_normbwd(
                dz, pl.BlockSpec((tb, n_in), lambda i_, k: (i_, k)),
                W["a_w_in", i], pl.BlockSpec((None, D, n_in), lambda i_, k: (k, 0, 0)),
                (D, n_in), N_CHIPS, rec["x_in"], a_norm_f[i][None], dx, f"a{i}_dx", deps=token)
        if layer == 1:
            batch = [jnp.concatenate(d_a_norm[1:], axis=0), jnp.concatenate(d_a_sgu[1:], axis=0), jnp.stack(d_ws[1:]),
                     jnp.stack(d_bs[1:]), d_kv_norm, jnp.concatenate(d_b_norm, axis=0), jnp.stack(d_rel),
                     jnp.concatenate(d_ffn_norm[1:], axis=0), d_final, loss_tile[:1, :1]]
            small_started.append((batch, _gather_start(_pack_small(batch), "gather_start_early")))
            pending.append(small_started[-1][1][4])
    grad_x = dx.reshape(x.shape)

    landed = _reduce_wait([(send, recv, g, land) for _, send, recv, g, land in started], dx, "reduce_wait")
    reduced_units = [unit for units_, *_ in started for unit in units_]
    arrived = dict(zip(reduced_units, [pair for gs, lands in landed for pair in zip(gs, lands)]))
    ffn_units = [u for u in reduced_units if u[0].startswith("ffn_")]
    other_units = [u for u in reduced_units if not u[0].startswith("ffn_")]
    halves = [_reduce_sum(*arrived[u], place, f"reduce_sum_{u[0]}_{u[1]}") for u in ffn_units]
    ffn_send, ffn_recv, ffn_flying, ffn_token = _join_start(halves, "join_start_ffn")
    halves = [_reduce_sum(*arrived[u], place, f"reduce_sum_{u[0]}_{u[1]}", deps=(ffn_token,)) for u in other_units]
    other_send, other_recv, other_flying, other_token = _join_start(halves, "join_start_rest")
    joined = dict(zip(ffn_units, _join_wait(ffn_flying, ffn_send, ffn_recv, other_token, "join_wait_ffn")))
    reduced = {}

    (packed_e, land_e), (packed_l, land_l) = _gather_wait([s[:4] for _, s in small_started], dx, "gather_wait")
    total_e = _sum_gathered(packed_e, land_e, place[2:], "sum_small_grads_early")
    total_l = _sum_gathered(packed_l, land_l, place[2:], "sum_small_grads_late")
    total_t = _sum_devices(_gather_small(_pack_small([d_a_norm[0]]), "gather_small_grads_last"), "sum_small_grads_last")
    (e_a_norm, e_a_sgu, e_ws, e_bs, g_kv_norm, g_b_norm, g_rel, e_ffn_norm, g_final, loss) = _unpack_small(
        total_e, [a.shape for a in small_started[0][0]])
    l_a_sgu, l_ws, l_bs, l_ffn_norm = _unpack_small(total_l, [a.shape for a in small_started[1][0]])
    (t_a_norm,) = _unpack_small(total_t, [d_a_norm[0].shape])
    g_a_norm = jnp.concatenate([t_a_norm, e_a_norm], axis=0)
    g_a_sgu = jnp.concatenate([l_a_sgu, e_a_sgu], axis=0)
    g_ws = jnp.concatenate([l_ws, e_ws], axis=0)
    g_bs = jnp.concatenate([l_bs, e_bs], axis=0)
    g_ffn_norm = jnp.concatenate([l_ffn_norm, e_ffn_norm], axis=0)
    reduced["a_norm"] = lax.dynamic_slice_in_dim(g_a_norm, chip * na_w, na_w, axis=1)
    reduced["a_sgu_norm"] = lax.dynamic_slice_in_dim(g_a_sgu, chip * ns_w, ns_w, axis=1)
    reduced.update(a_w_spatial=g_ws, a_b_spatial=g_bs, kv_norm=g_kv_norm.reshape(kv_norm.shape), b_norm=g_b_norm,
                   b_rel_bias=g_rel, ffn_norm=g_ffn_norm, final_norm=g_final.reshape(final_norm.shape))

    weights = dict(a_norm=a_norm, a_w_in=a_w_in, a_sgu_norm=a_sgu_norm, a_w_spatial=a_w_spatial,
                   a_b_spatial=a_b_spatial, a_w_out=a_w_out, kv_norm=kv_norm, w_kv=w_kv, b_norm=b_norm, b_w_q=b_w_q,
                   b_rel_bias=b_rel_bias, b_w_o=b_w_o, ffn_norm=ffn_norm, ffn_w_gate_up=ffn_w_gate_up,
                   ffn_w_down=ffn_w_down, final_norm=final_norm)
    m_in = dict(a_norm=m_a_norm, a_w_in=m_a_w_in, a_sgu_norm=m_a_sgu_norm, a_w_spatial=m_a_w_spatial,
                a_b_spatial=m_a_b_spatial, a_w_out=m_a_w_out, kv_norm=m_kv_norm, w_kv=m_w_kv, b_norm=m_b_norm,
                b_w_q=m_b_w_q, b_rel_bias=m_b_rel_bias, b_w_o=m_b_w_o, ffn_norm=m_ffn_norm,
                ffn_w_gate_up=m_ffn_w_gate_up, ffn_w_down=m_ffn_w_down, final_norm=m_final_norm)
    v_in = dict(a_norm=v_a_norm, a_w_in=v_a_w_in, a_sgu_norm=v_a_sgu_norm, a_w_spatial=v_a_w_spatial,
                a_b_spatial=v_a_b_spatial, a_w_out=v_a_w_out, kv_norm=v_kv_norm, w_kv=v_w_kv, b_norm=v_b_norm,
                b_w_q=v_b_w_q, b_rel_bias=v_b_rel_bias, b_w_o=v_b_w_o, ffn_norm=v_ffn_norm,
                ffn_w_gate_up=v_ffn_w_gate_up, ffn_w_down=v_ffn_w_down, final_norm=v_final_norm)
    results = {}

    def adamw_large(key):
        as_layers = lambda a: a.reshape(stacked[key].shape)
        results[key] = _adamw_stacked(
            as_layers(weights[key]), [joined[key, layer] for layer in range(stacked[key].shape[0])],
            as_layers(m_in[key]), as_layers(v_in[key]), "adamw_" + key)

    for key in ("ffn_w_gate_up", "ffn_w_down"):
        adamw_large(key)
    joined.update(zip(other_units, _join_wait(other_flying, other_send, other_recv, results["ffn_w_down"][1],
                                              "join_wait_rest")))
    for key, w in weights.items():
        if key in stacked:
            if key not in results:
                adamw_large(key)
        else:
            g = reduced[key].reshape(w.shape)
            view = (1, w.shape[0]) if w.ndim == 1 else (-1, w.shape[-1])
            d, nm, nv = _adamw(w.reshape(view), g.reshape(view), m_in[key].reshape(view), v_in[key].reshape(view),
                               "adamw_" + key)
            results[key] = (g, d, nm, nv)
    outs = [[results[key][k].reshape(w.shape) for key, w in weights.items()] for k in range(4)]
    return (loss.reshape(()), grad_x, *outs[0], *outs[1], *outs[2], *outs[3])
```

```python
import math

import jax
import jax.numpy as jnp
from jax import lax
from jax.experimental import pallas as pl
from jax.experimental.pallas import tpu as pltpu

F32, BF16 = jnp.float32, jnp.bfloat16
MESH = pl.DeviceIdType.MESH
HIGHEST = lax.Precision.HIGHEST
NT_DIMS = (((1,), (1,)), ((), ()))
TN_DIMS = (((0,), (0,)), ((), ()))

EPS = 1e-6
CHUNK = 64
A_CHUNK = 128
A_GROUPS = 8
N_HEADS = 16
HEAD_DIM = 64
N_LEFT = 8
MAX_REL = 256
ATTN_SCALE = HEAD_DIM ** -0.5
NEG_INF = -1e30
Q_BLOCK = 2 * CHUNK
KV_PAD = N_LEFT * CHUNK
BAND = KV_PAD + Q_BLOCK
DIAGS = BAND + Q_BLOCK
TABLE_PAD = 640
HEADS_PER_BLOCK = 2
BLOCKS_PER_STEP = 8
NORMBWD_ROWS = 1024

ADAM_LR, ADAM_B1, ADAM_B2, ADAM_EPS, ADAM_WD, ADAM_STEP = 0.001, 0.9, 0.999, 1e-08, 0.01, 10

VMEM_LIMIT_BYTES = 56 * 1024 * 1024
N_CHIPS = 4
N_DEV = 8
SMALL_COLS = 1024


def _cparams(n_grid):
    return pltpu.CompilerParams(dimension_semantics=("arbitrary",) * n_grid, vmem_limit_bytes=VMEM_LIMIT_BYTES)


def _sds(shape, dtype):
    return jax.ShapeDtypeStruct(tuple(shape), dtype)


def _gelu(x):
    return x * (0.5 * (1.0 + lax.erf(x * math.sqrt(0.5))))


def _gelu_and_grad(x):
    cdf = 0.5 * (1.0 + lax.erf(x * math.sqrt(0.5)))
    return x * cdf, cdf + x * (jnp.exp(-0.5 * x * x) * (1.0 / math.sqrt(2.0 * math.pi)))


def _rms_hat(xv):
    r = lax.rsqrt(jnp.mean(xv * xv, axis=-1, keepdims=True) + EPS)
    return xv * r, r


def _rms_bwd(xhat, r, g, dy):
    dxhat = dy * g
    dx = r * (dxhat - xhat * jnp.mean(dxhat * xhat, axis=-1, keepdims=True))
    return dx, dy * xhat


def _swiglu(gate, up):
    return (gate * jax.nn.sigmoid(gate)) * up


def _row_tile(rows, cols, itemsize, cap_bytes, align):
    t = rows
    while t * cols * itemsize > cap_bytes and t % (2 * align) == 0:
        t //= 2
    return t


def _cast_slab(w, layer, chip, name, deps=()):
    _, r, C = w.shape
    tr = _row_tile(r, C, 4, 4 * 1024 * 1024, 16)

    def body(chip_ref, w_ref, *rest):
        del chip_ref
        rest[-1][...] = w_ref[...].astype(BF16)

    grid_spec = pltpu.PrefetchScalarGridSpec(
        num_scalar_prefetch=1, grid=(r // tr,),
        in_specs=[pl.BlockSpec((None, tr, C), lambda i, chip_ref: (layer, i, 0))] + [HBM_SPEC] * len(deps),
        out_specs=pl.BlockSpec((None, tr, C), lambda i, chip_ref: (chip_ref[0], i, 0)))
    return pl.pallas_call(body, name=name, grid_spec=grid_spec, out_shape=_sds((N_CHIPS, r, C), BF16),
                          compiler_params=_cparams(1))(chip, w, *deps)


def _adamw_stacked(w, gs, m, v, name):
    L, r, C = w.shape
    tr = _row_tile(r, C, 4, 2 * 1024 * 1024, 8)
    nb = r // tr

    def body(w_ref, m_ref, v_ref, *rest):
        go_ref, d_ref, nm_ref, nv_ref = rest[-4:]
        layer = pl.program_id(0)
        gv = rest[0][...]
        for k in range(1, L):
            gv = jnp.where(layer == k, rest[k][...], gv)
        mn = ADAM_B1 * m_ref[...] + (1.0 - ADAM_B1) * gv
        vn = ADAM_B2 * v_ref[...] + (1.0 - ADAM_B2) * jnp.square(gv)
        m_hat = mn / (1.0 - ADAM_B1 ** ADAM_STEP)
        v_hat = vn / (1.0 - ADAM_B2 ** ADAM_STEP)
        d_ref[...] = -ADAM_LR * (m_hat / (jnp.sqrt(v_hat) + ADAM_EPS) + ADAM_WD * w_ref[...])
        nm_ref[...] = mn
        nv_ref[...] = vn
        go_ref[...] = gv

    def grad_spec(k):
        return pl.BlockSpec((tr, C), lambda l, i: (jnp.where(l == k, i, jnp.where(l > k, nb - 1, 0)), 0))

    stacked = pl.BlockSpec((None, tr, C), lambda l, i: (l, i, 0))
    return pl.pallas_call(body, name=name, grid=(L, nb), in_specs=[stacked] * 3 + [grad_spec(k) for k in range(L)],
                          out_specs=[stacked] * 4, out_shape=[_sds((L, r, C), F32)] * 4,
                          compiler_params=_cparams(2))(w, m, v, *gs)


def _adamw(w, g, m, v, name):
    R, C = w.shape
    tr = _row_tile(R, C, 4, 1024 * 1024, 8)

    def body(w_ref, g_ref, m_ref, v_ref, d_ref, nm_ref, nv_ref):
        gv = g_ref[...]
        mn = ADAM_B1 * m_ref[...] + (1.0 - ADAM_B1) * gv
        vn = ADAM_B2 * v_ref[...] + (1.0 - ADAM_B2) * jnp.square(gv)
        m_hat = mn / (1.0 - ADAM_B1 ** ADAM_STEP)
        v_hat = vn / (1.0 - ADAM_B2 ** ADAM_STEP)
        d_ref[...] = -ADAM_LR * (m_hat / (jnp.sqrt(v_hat) + ADAM_EPS) + ADAM_WD * w_ref[...])
        nm_ref[...] = mn
        nv_ref[...] = vn

    spec = pl.BlockSpec((tr, C), lambda i: (i, 0))
    return pl.pallas_call(body, name=name, grid=(R // tr,), in_specs=[spec] * 4, out_specs=[spec] * 3,
                          out_shape=[_sds((R, C), F32)] * 3, compiler_params=_cparams(1))(w, g, m, v)


def _norm_matmul(x, g, w_g, out_dtype, name, row_sharded=False, deps=(), tm=1024, zero_rows=0):
    S, D = x.shape
    tm = min(tm, S)
    lead = zero_rows // tm
    if row_sharded:
        r, N = w_g.shape[1], w_g.shape[2]
        tn = 512
        w_spec = pl.BlockSpec((N_CHIPS, r, tn), lambda i, j: (0, 0, j))
    else:
        nsh = w_g.shape[2]
        N = N_CHIPS * nsh
        tn = next((t for t in (1024, 512) if nsh % t == 0), nsh)
        bps = nsh // tn
        w_spec = pl.BlockSpec((None, D, tn), lambda i, j: (j // bps, 0, j % bps))
    whole = not row_sharded and N <= 2048
    if whole:
        tn = N
        w_spec = pl.BlockSpec((N_CHIPS, D, nsh), lambda i, j: (0, 0, 0))

    def body(x_ref, g_ref, w_ref, *rest):
        y_ref, h_ref = rest[-2:]
        i = pl.program_id(0)

        @pl.when((i >= lead) & (pl.program_id(1) == 0))
        def _():
            xhat, _ = _rms_hat(x_ref[...])
            h_ref[...] = (xhat * g_ref[...]).astype(BF16)

        @pl.when(i >= lead)
        def _():
            if whole:
                for s in range(N_CHIPS):
                    y_ref[:, s * nsh:(s + 1) * nsh] = jnp.dot(h_ref[...], w_ref[s],
                                                             preferred_element_type=F32).astype(y_ref.dtype)
            else:
                w = w_ref[...].reshape(D, tn)
                y_ref[...] = jnp.dot(h_ref[...], w, preferred_element_type=F32).astype(y_ref.dtype)

        if lead:
            @pl.when(i < lead)
            def _():
                y_ref[...] = jnp.zeros_like(y_ref)

    rows = lambda i, j: (jnp.maximum(i - lead, 0), 0)
    return pl.pallas_call(
        body, name=name, grid=(lead + S // tm, N // tn),
        in_specs=[pl.BlockSpec((tm, D), rows), pl.BlockSpec((1, D), lambda i, j: (0, 0)), w_spec]
        + [HBM_SPEC] * len(deps),
        out_specs=[pl.BlockSpec((tm, tn), lambda i, j: (i, j)), pl.BlockSpec((tm, D), rows)],
        out_shape=[_sds((zero_rows + S, N), out_dtype), _sds((S, D), BF16)],
        compiler_params=_cparams(2))(x, g, w_g, *deps)


def _matmul_res(a, w_g, res, name, swiglu=False, deps=(), tm=256):
    S, N = res.shape
    r = w_g.shape[1]
    K = N_CHIPS * r

    def body(*refs):
        o_ref = refs[-1]
        if swiglu:
            gate_ref, up_ref, w_ref, res_ref = refs[:4]
            a_blk = _swiglu(gate_ref[...].astype(F32), up_ref[...].astype(F32)).astype(BF16)
        else:
            a_ref, w_ref, res_ref = refs[:3]
            a_blk = a_ref[...]
        o_ref[...] = res_ref[...] + jnp.dot(a_blk, w_ref[...].reshape(K, N), preferred_element_type=F32)

    a_specs, a_ops = [pl.BlockSpec((tm, K), lambda i: (i, 0))], [a]
    if swiglu:
        a_specs.append(pl.BlockSpec((tm, K), lambda i: (i, 1)))
        a_ops.append(a)
    row = pl.BlockSpec((tm, N), lambda i: (i, 0))
    return pl.pallas_call(
        body, name=name, grid=(S // tm,),
        in_specs=a_specs + [pl.BlockSpec((N_CHIPS, r, N), lambda i: (0, 0, 0)), row] + [HBM_SPEC] * len(deps),
        out_specs=row, out_shape=_sds((S, N), F32), compiler_params=_cparams(1))(*a_ops, w_g, res, *deps)


def _matmul_tn(a_ops, a_specs, a_fn, b_op, b_spec, out_spec, out_shape, acc_shape, n_outer, name, deps=(), tt=512):
    S = b_op.shape[-2]
    na = len(a_ops)
    nt = S // tt

    def body(*refs):
        a_refs, b_ref, o_ref, acc_ref = refs[:na], refs[na], refs[-2], refs[-1]
        t = pl.program_id(1)
        part = lax.dot_general(a_fn(*a_refs), b_ref[...].astype(BF16), TN_DIMS, preferred_element_type=F32)

        @pl.when(t == 0)
        def _():
            acc_ref[...] = part

        @pl.when(t > 0)
        def _():
            acc_ref[...] += part

        @pl.when(t == nt - 1)
        def _():
            o_ref[...] = acc_ref[...].reshape(o_ref.shape).astype(BF16)

    return pl.pallas_call(
        body, name=name, grid=(n_outer, nt), in_specs=list(a_specs) + [b_spec] + [HBM_SPEC] * len(deps),
        out_specs=out_spec, out_shape=_sds(out_shape, BF16), scratch_shapes=[pltpu.VMEM(acc_shape, F32)],
        compiler_params=_cparams(2))(*a_ops, b_op, *deps)


def _nt_accumulate(a_ref, w_ref, acc_ref, w2d, nk):
    k = pl.program_id(1)
    part = lax.dot_general(a_ref[...].astype(BF16), w_ref[...].reshape(w2d), NT_DIMS, preferred_element_type=F32)

    @pl.when(k == 0)
    def _():
        acc_ref[...] = part

    @pl.when(k > 0)
    def _():
        acc_ref[...] += part

    return k == nk - 1


def _nt_normbwd(dy, dy_spec, w_g, w_spec, w2d, nk, x, g, dres, name, deps=(), tm=NORMBWD_ROWS):
    S, D = x.shape
    tm = min(tm, S)

    def body(dy_ref, w_ref, x_ref, g_ref, dres_ref, *rest):
        dx_ref, dxb_ref, dg_ref, acc_ref = rest[-4:]

        @pl.when((pl.program_id(0) == 0) & (pl.program_id(1) == 0))
        def _():
            dg_ref[...] = jnp.zeros_like(dg_ref)

        last = _nt_accumulate(dy_ref, w_ref, acc_ref, w2d, nk)

        @pl.when(last)
        def _():
            xhat, r = _rms_hat(x_ref[...])
            dx, dgp = _rms_bwd(xhat, r, g_ref[...], acc_ref[...])
            total = dres_ref[...] + dx
            dx_ref[...] = total
            dxb_ref[...] = total.astype(BF16)
            dg_ref[...] += jnp.sum(dgp, axis=0, keepdims=True)

    row = pl.BlockSpec((tm, D), lambda i, k: (i, 0))
    vec = pl.BlockSpec((1, D), lambda i, k: (0, 0))
    return pl.pallas_call(
        body, name=name, grid=(S // tm, nk),
        in_specs=[dy_spec, w_spec, row, vec, row] + [HBM_SPEC] * len(deps), out_specs=[row, row, vec],
        out_shape=[_sds((S, D), F32), _sds((S, D), BF16), _sds((1, D), F32)],
        scratch_shapes=[pltpu.VMEM((tm, D), F32)], compiler_params=_cparams(2))(dy, w_g, x, g, dres, *deps)


def _nt_rows(dy, w_g, shards_per_block, out_dtype, name, deps=(), tm=1024):
    S, N = dy.shape
    tm = min(tm, S)
    r = w_g.shape[1]
    tn = shards_per_block * r

    def body(dy_ref, w_ref, *rest):
        o_ref = rest[-1]
        o_ref[...] = lax.dot_general(dy_ref[...].astype(BF16), w_ref[...].reshape(tn, N), NT_DIMS,
                                     preferred_element_type=F32).astype(o_ref.dtype)

    return pl.pallas_call(
        body, name=name, grid=(S // tm, N_CHIPS // shards_per_block),
        in_specs=[pl.BlockSpec((tm, N), lambda i, j: (i, 0)),
                  pl.BlockSpec((shards_per_block, r, N), lambda i, j: (j, 0, 0))] + [HBM_SPEC] * len(deps),
        out_specs=pl.BlockSpec((tm, tn), lambda i, j: (i, j)),
        out_shape=_sds((S, N_CHIPS * r), out_dtype), compiler_params=_cparams(2))(dy, w_g, *deps)


def _nt_swiglu_bwd(dy, w_g, gu, name, deps=(), tm=1024):
    S, N = dy.shape
    tm = min(tm, S)
    r = w_g.shape[1]
    tn = 2 * r
    F = N_CHIPS * r

    def body(dy_ref, w_ref, gate_ref, up_ref, *rest):
        o_ref = rest[-1]
        dact = lax.dot_general(dy_ref[...].astype(BF16), w_ref[...].reshape(tn, N), NT_DIMS,
                               preferred_element_type=F32)
        gate, up = gate_ref[...].astype(F32), up_ref[...].astype(F32)
        sg = jax.nn.sigmoid(gate)
        silu = gate * sg
        o_ref[0] = ((dact * up) * (sg + silu * (1.0 - sg))).astype(BF16)
        o_ref[1] = (dact * silu).astype(BF16)

    return pl.pallas_call(
        body, name=name, grid=(2, S // tm),
        in_specs=[pl.BlockSpec((tm, N), lambda j, i: (i, 0)),
                  pl.BlockSpec((2, r, N), lambda j, i: (j, 0, 0)),
                  pl.BlockSpec((tm, tn), lambda j, i: (i, j)),
                  pl.BlockSpec((tm, tn), lambda j, i: (i, 2 + j))] + [HBM_SPEC] * len(deps),
        out_specs=pl.BlockSpec((2, tm, tn), lambda j, i: (0, i, j)),
        out_shape=_sds((2, S, F), BF16), compiler_params=_cparams(2))(dy, w_g, gu, gu, *deps)


def _chunk_causal_mask(transposed):
    i = lax.broadcasted_iota(jnp.int32, (A_CHUNK, A_CHUNK), 0) // CHUNK
    j = lax.broadcasted_iota(jnp.int32, (A_CHUNK, A_CHUNK), 1) // CHUNK
    return ((i <= j) if transposed else (i >= j)).astype(F32)


def _sgu_fwd(zpre, g_sgu, ws, bs_t, name, deps=()):
    S, F2 = zpre.shape
    F = F2 // 2
    gd = F // A_GROUPS

    windows = 2
    rows = windows * A_CHUNK

    def body(zu_ref, zv_ref, g_ref, ws_ref, b_ref, *rest):
        o_ref = rest[-1]
        vhat, _ = _rms_hat(_gelu(zv_ref[...].astype(F32)))
        vn = (vhat * g_ref[...]).astype(BF16)
        u = _gelu(zu_ref[...].astype(F32))
        mask = _chunk_causal_mask(False)
        for gi in range(A_GROUPS):
            sl = slice(gi * gd, (gi + 1) * gd)
            wm = (ws_ref[gi] * mask).astype(BF16)
            for w in range(windows):
                win = slice(w * A_CHUNK, (w + 1) * A_CHUNK)
                vs = jnp.dot(wm, vn[win, sl], preferred_element_type=F32) + b_ref[:, gi:gi + 1]
                o_ref[win, sl] = (u[win, sl] * vs).astype(BF16)

    return pl.pallas_call(
        body, name=name, grid=(S // rows,),
        in_specs=[pl.BlockSpec((rows, F), lambda i: (i, 0)),
                  pl.BlockSpec((rows, F), lambda i: (i, 1)),
                  pl.BlockSpec((1, F), lambda i: (0, 0)),
                  pl.BlockSpec((A_GROUPS, A_CHUNK, A_CHUNK), lambda i: (0, 0, 0)),
                  pl.BlockSpec((A_CHUNK, A_GROUPS), lambda i: (0, 0))] + [HBM_SPEC] * len(deps),
        out_specs=pl.BlockSpec((rows, F), lambda i: (i, 0)),
        out_shape=_sds((S, F), BF16), compiler_params=_cparams(1))(zpre, zpre, g_sgu, ws, bs_t, *deps)


def _sgu_bwd(zpre, duv, g_sgu, ws, ws_t, bs_t, name):
    S, F2 = zpre.shape
    F = F2 // 2
    gd = F // A_GROUPS

    def body(zu_ref, zv_ref, duv_ref, g_ref, ws_ref, wst_ref, b_ref, dz_ref, dg_ref, dws_ref, dbs_ref, dvn_ref):
        @pl.when(pl.program_id(0) == 0)
        def _():
            dg_ref[...] = jnp.zeros_like(dg_ref)
            dws_ref[...] = jnp.zeros_like(dws_ref)
            dbs_ref[...] = jnp.zeros_like(dbs_ref)

        gv = g_ref[...]
        u, u_grad = _gelu_and_grad(zu_ref[...].astype(F32))
        v, v_grad = _gelu_and_grad(zv_ref[...].astype(F32))
        vhat, r = _rms_hat(v)
        vn = (vhat * gv).astype(BF16)
        duv_v = duv_ref[...].astype(F32)
        dvs = duv_v * u
        dvs_b = dvs.astype(BF16)
        mask = _chunk_causal_mask(False)
        mask_t = _chunk_causal_mask(True)
        for gi in range(A_GROUPS):
            sl = slice(gi * gd, (gi + 1) * gd)
            wm = (ws_ref[gi] * mask).astype(BF16)
            vs = jnp.dot(wm, vn[:, sl], preferred_element_type=F32) + b_ref[:, gi:gi + 1]
            dz_ref[:, sl] = ((duv_v[:, sl] * vs) * u_grad[:, sl]).astype(BF16)
            dws_ref[gi] += lax.dot_general(dvs_b[:, sl], vn[:, sl], NT_DIMS, preferred_element_type=F32) * mask
            dbs_ref[gi] += jnp.broadcast_to(jnp.sum(dvs[:, sl], axis=1, keepdims=True), (A_CHUNK, A_CHUNK))
            wm_t = (wst_ref[gi] * mask_t).astype(BF16)
            dvn_ref[:, sl] = jnp.dot(wm_t, dvs_b[:, sl], preferred_element_type=F32)
        dv, dg_part = _rms_bwd(vhat, r, gv, dvn_ref[...])
        dg_ref[...] += jnp.sum(dg_part, axis=0, keepdims=True)
        dz_ref[:, F:] = (dv * v_grad).astype(BF16)

    blk = pl.BlockSpec((A_CHUNK, F), lambda i: (i, 0))
    const3 = pl.BlockSpec((A_GROUPS, A_CHUNK, A_CHUNK), lambda i: (0, 0, 0))
    return pl.pallas_call(
        body, name=name, grid=(S // A_CHUNK,),
        in_specs=[blk, pl.BlockSpec((A_CHUNK, F), lambda i: (i, 1)), blk,
                  pl.BlockSpec((1, F), lambda i: (0, 0)), const3, const3,
                  pl.BlockSpec((A_CHUNK, A_GROUPS), lambda i: (0, 0))],
        out_specs=[pl.BlockSpec((A_CHUNK, F2), lambda i: (i, 0)), pl.BlockSpec((1, F), lambda i: (0, 0)),
                   const3, const3],
        out_shape=[_sds((S, F2), BF16), _sds((1, F), F32), _sds((A_GROUPS, A_CHUNK, A_CHUNK), F32),
                   _sds((A_GROUPS, A_CHUNK, A_CHUNK), F32)],
        scratch_shapes=[pltpu.VMEM((A_CHUNK, F), F32)],
        compiler_params=_cparams(1))(zpre, zpre, duv, g_sgu, ws, ws_t, bs_t)


def _toeplitz_one_hot():
    row = lax.broadcasted_iota(jnp.int32, (TABLE_PAD, DIAGS), 0)
    j = lax.broadcasted_iota(jnp.int32, (TABLE_PAD, DIAGS), 1)
    idx = jnp.clip(KV_PAD + Q_BLOCK - j, -MAX_REL, MAX_REL) + MAX_REL
    return (row == idx).astype(F32)


def _rel_bias_fwd(table, name):
    H = table.shape[0]

    def body(t_ref, o_ref):
        diag = jnp.dot(t_ref[...], _toeplitz_one_hot(), precision=HIGHEST, preferred_element_type=F32)
        k_chunk = lax.broadcasted_iota(jnp.int32, (1, BAND), 1) // CHUNK

        def step(r, carry):
            q_chunk = r // CHUNK
            seen = (k_chunk >= q_chunk) & (k_chunk <= q_chunk + N_LEFT)
            o_ref[r] = pltpu.roll(diag, DIAGS - Q_BLOCK + r, 1)[:, :BAND] + jnp.where(seen, 0.0, NEG_INF)
            return carry

        lax.fori_loop(0, Q_BLOCK, step, 0)

    return pl.pallas_call(body, name=name, out_shape=_sds((Q_BLOCK, H, BAND), F32),
                          compiler_params=pltpu.CompilerParams(vmem_limit_bytes=VMEM_LIMIT_BYTES))(table)


def _rel_bias_bwd(dbias, name):
    H = dbias.shape[1]

    def body(d_ref, o_ref):
        def step(r, acc):
            row = jnp.concatenate([d_ref[r], jnp.zeros((H, DIAGS - BAND), F32)], axis=1)
            return acc + pltpu.roll(row, Q_BLOCK - r, 1)

        diag = lax.fori_loop(0, Q_BLOCK, step, jnp.zeros((H, DIAGS), F32))
        o_ref[...] = lax.dot_general(diag, _toeplitz_one_hot(), NT_DIMS, precision=HIGHEST,
                                     preferred_element_type=F32)

    return pl.pallas_call(body, name=name, out_shape=_sds((H, TABLE_PAD), F32),
                          compiler_params=pltpu.CompilerParams(vmem_limit_bytes=VMEM_LIMIT_BYTES))(dbias)


def _head_rows(t):
    lane = lax.broadcasted_iota(jnp.int32, t.shape, 1)
    zero = jnp.zeros_like(t)
    return jnp.concatenate([jnp.where(lane < HEAD_DIM, t, zero), jnp.where(lane >= HEAD_DIM, t, zero)], axis=0)


def _head_lanes(t2):
    lane = lax.broadcasted_iota(jnp.int32, (Q_BLOCK, t2.shape[1]), 1)
    return jnp.where(lane < HEAD_DIM, t2[:Q_BLOCK], t2[Q_BLOCK:])


def _attn_probs(q2, kb, bias2, block):
    kj = lax.broadcasted_iota(jnp.int32, (1, BAND), 1)
    before_start = jnp.where(block * Q_BLOCK + kj - KV_PAD >= 0, 0.0, NEG_INF)
    s = lax.dot_general(q2 * ATTN_SCALE, kb, NT_DIMS, preferred_element_type=F32) + bias2 + before_start
    e = jnp.exp(s - jnp.max(s, axis=-1, keepdims=True))
    return e / jnp.sum(e, axis=-1, keepdims=True)


def _attn_specs(S):
    lanes = HEADS_PER_BLOCK * HEAD_DIM
    rows = S + KV_PAD
    per_step = min(BLOCKS_PER_STEP, S // Q_BLOCK)
    q_spec = pl.BlockSpec((per_step * Q_BLOCK, lanes), lambda h, i: (i, h))
    k_spec = pl.BlockSpec((rows, lanes), lambda h, i: (0, h))
    v_spec = pl.BlockSpec((rows, lanes), lambda h, i: (0, N_HEADS // HEADS_PER_BLOCK + h))
    b_spec = pl.BlockSpec((HEADS_PER_BLOCK, Q_BLOCK, BAND), lambda h, i: (h, 0, 0))
    return q_spec, k_spec, v_spec, b_spec, per_step


def _attn_fwd(q, kvp, bias, name, deps=()):
    S, HD = q.shape
    q_spec, k_spec, v_spec, b_spec, per_step = _attn_specs(S)

    def body(q_ref, k_ref, v_ref, b_ref, *rest):
        o_ref = rest[-1]
        for b in range(per_step):
            block = pl.program_id(1) * per_step + b
            rows = slice(b * Q_BLOCK, (b + 1) * Q_BLOCK)
            band = pl.ds(pl.multiple_of(block * Q_BLOCK, Q_BLOCK), BAND)
            p = _attn_probs(_head_rows(q_ref[rows, :]), k_ref[band, :], b_ref[...].reshape(2 * Q_BLOCK, BAND), block)
            o2 = jnp.dot(p.astype(BF16), v_ref[band, :], preferred_element_type=F32)
            o_ref[rows, :] = _head_lanes(o2).astype(BF16)

    return pl.pallas_call(
        body, name=name, grid=(N_HEADS // HEADS_PER_BLOCK, S // (per_step * Q_BLOCK)),
        in_specs=[q_spec, k_spec, v_spec, b_spec] + [HBM_SPEC] * len(deps), out_specs=q_spec,
        out_shape=_sds((S, HD), BF16), compiler_params=_cparams(2))(q, kvp, kvp, bias, *deps)


def _attn_bwd(q, kvp, bias, do, dkv_prev, name):
    S, HD = q.shape
    lanes = HEADS_PER_BLOCK * HEAD_DIM
    q_spec, k_spec, v_spec, b_spec, per_step = _attn_specs(S)
    dkv_spec = pl.BlockSpec((2, S + KV_PAD, lanes), lambda h, i: (0, 0, h))
    prev = [] if dkv_prev is None else [dkv_prev]
    n_steps = S // (per_step * Q_BLOCK)

    def body(q_ref, k_ref, v_ref, b_ref, do_ref, *rest):
        dq_ref, dkv_ref, db_ref = rest[len(prev):len(prev) + 3]

        @pl.when(pl.program_id(1) == 0)
        def _():
            dkv_ref[...] = rest[0][...] if prev else jnp.zeros_like(dkv_ref)
            db_ref[...] = jnp.zeros_like(db_ref)

        db = jnp.zeros((2 * Q_BLOCK, BAND), F32)
        for b in range(per_step):
            block = pl.program_id(1) * per_step + b
            rows = slice(b * Q_BLOCK, (b + 1) * Q_BLOCK)
            band = pl.ds(pl.multiple_of(block * Q_BLOCK, Q_BLOCK), BAND)
            kb, vb = k_ref[band, :], v_ref[band, :]
            q2, do2 = _head_rows(q_ref[rows, :]), _head_rows(do_ref[rows, :])
            p = _attn_probs(q2, kb, b_ref[...].reshape(2 * Q_BLOCK, BAND), block)
            dp = lax.dot_general(do2, vb, NT_DIMS, preferred_element_type=F32)
            ds = p * (dp - jnp.sum(dp * p, axis=-1, keepdims=True))
            db = db + ds
            ds_b = (ds * ATTN_SCALE).astype(BF16)
            dq_ref[rows, :] = _head_lanes(jnp.dot(ds_b, kb, preferred_element_type=F32)).astype(BF16)
            dkv_ref[0, band, :] += lax.dot_general(ds_b, q2, TN_DIMS, preferred_element_type=F32)
            dkv_ref[1, band, :] += lax.dot_general(p.astype(BF16), do2, TN_DIMS, preferred_element_type=F32)
        db_ref[...] += db.reshape(HEADS_PER_BLOCK, Q_BLOCK, BAND)

        if prev:
            @pl.when(pl.program_id(1) == n_steps - 1)
            def _():
                rest[-1][...] = dkv_ref[:, KV_PAD:, :].astype(BF16)

    return pl.pallas_call(
        body, name=name, grid=(N_HEADS // HEADS_PER_BLOCK, n_steps),
        in_specs=[q_spec, k_spec, v_spec, b_spec, q_spec] + [dkv_spec] * len(prev),
        out_specs=[q_spec, dkv_spec, b_spec] + [pl.BlockSpec((2, S, lanes), lambda h, i: (0, 0, h))] * len(prev),
        out_shape=[_sds((S, HD), BF16), _sds((2, S + KV_PAD, HD), F32), _sds((N_HEADS, Q_BLOCK, BAND), F32)]
        + [_sds((2, S, HD), BF16)] * len(prev),
        compiler_params=_cparams(2))(q, kvp, kvp, bias, do, *prev)


def _loss_head(x, g, target, name, tm=512):
    S, D = x.shape

    def body(x_ref, g_ref, t_ref, loss_ref, dx_ref, dxb_ref, dg_ref):
        @pl.when(pl.program_id(0) == 0)
        def _():
            loss_ref[...] = jnp.zeros_like(loss_ref)
            dg_ref[...] = jnp.zeros_like(dg_ref)

        xhat, r = _rms_hat(x_ref[...])
        gv = g_ref[...]
        err = xhat * gv - t_ref[...]
        loss_ref[...] += 0.5 * jnp.sum(jnp.mean(err * err, axis=-1, keepdims=True))
        dx, dgp = _rms_bwd(xhat, r, gv, err * (1.0 / D))
        dx_ref[...] = dx
        dxb_ref[...] = dx.astype(BF16)
        dg_ref[...] += jnp.sum(dgp, axis=0, keepdims=True)

    row = pl.BlockSpec((tm, D), lambda i: (i, 0))
    vec = pl.BlockSpec((1, D), lambda i: (0, 0))
    return pl.pallas_call(
        body, name=name, grid=(S // tm,), in_specs=[row, vec, row],
        out_specs=[pl.BlockSpec((8, 128), lambda i: (0, 0)), row, row, vec],
        out_shape=[_sds((8, 128), F32), _sds((S, D), F32), _sds((S, D), BF16), _sds((1, D), F32)],
        compiler_params=_cparams(1))(x, g, target)


def _place():
    x, y, c = lax.axis_index("x"), lax.axis_index("y"), lax.axis_index("c")
    chips = [(1 - x, y), (x, 1 - y), (1 - x, 1 - y)]
    return x, y, c, chips


def _half_rows(c, r):
    return pl.ds(pl.multiple_of(c * (r // 2), 8), r // 2)


HBM_SPEC = pl.BlockSpec(memory_space=pl.ANY)


STRICT_HBM_SPEC = pl.BlockSpec(memory_space=pltpu.HBM)
SEM_SPEC = pl.BlockSpec(memory_space=pltpu.SEMAPHORE)
EFFECT = pltpu.SideEffectType.DATAFLOW_SIDE_EFFECTING


def _peers(x, y, c):
    out = []
    for k in range(1, N_DEV):
        px, py, pc = (x + ((k >> 2) & 1)) % 2, (y + ((k >> 1) & 1)) % 2, (c + (k & 1)) % 2
        out.append(((px, py, pc), 2 * px + py, pc, 4 * px + 2 * py + pc))
    return out


def _token_spec():
    return pl.BlockSpec(memory_space=pltpu.VMEM)


def _hbm(a):
    return pltpu.with_memory_space_constraint(a, pltpu.HBM)


def _slab_half(ref, chip, core):
    return ref.at[2 * chip[0] + chip[1], _half_rows(core, ref.shape[1]), :]


def _allgather_start(slabs, name):
    n = len(slabs)

    def body(*refs):
        src, send, recv, token = refs[:n], refs[n], refs[n + 1], refs[-1]
        x, y, c, chips = _place()
        for a in range(n):
            own = _slab_half(src[a], (x, y), c)
            for j, chip in enumerate(chips):
                pltpu.make_async_remote_copy(src_ref=own, dst_ref=own, send_sem=send.at[3 * a + j], recv_sem=recv.at[3 * a + j],
                                             device_id=(*chip, c), device_id_type=MESH).start()
        token[...] = jnp.zeros_like(token)

    sems = pltpu.SemaphoreType.DMA((3 * n,))
    send, recv, *flying, token = pl.pallas_call(
        body, name=name, in_specs=[STRICT_HBM_SPEC] * n,
        out_shape=(sems, sems, *[pltpu.HBM(s.shape, s.dtype) for s in slabs], _sds((8, 128), F32)),
        out_specs=(SEM_SPEC, SEM_SPEC, *[STRICT_HBM_SPEC] * n, _token_spec()),
        input_output_aliases={a: a + 2 for a in range(n)},
        compiler_params=pltpu.CompilerParams(has_side_effects=EFFECT))(*[_hbm(s) for s in slabs])
    return send, recv, flying, token


def _allgather_relay(flying, send, recv, first, after, name):
    n = len(flying)

    def body(*refs):
        src, send_ref, recv_ref = refs[:n], refs[n], refs[n + 1]
        send2, recv2, token = refs[n + 3], refs[n + 4], refs[-1]
        token[...] = jnp.zeros_like(token)
        x, y, c, chips = _place()
        for a in range(n):
            for j, chip in enumerate(chips):
                cp = pltpu.make_async_remote_copy(
                    src_ref=_slab_half(src[a], (x, y), c), dst_ref=_slab_half(src[a], chip, c),
                    send_sem=send_ref.at[3 * (first + a) + j], recv_sem=recv_ref.at[3 * (first + a) + j],
                    device_id=(*chip, c), device_id_type=MESH)
                cp.wait_send()
                cp.wait_recv()
        for a in range(n):
            for j, chip in enumerate(chips):
                landed = _slab_half(src[a], chip, c)
                pltpu.make_async_remote_copy(src_ref=landed, dst_ref=landed, send_sem=send2.at[3 * a + j],
                                             recv_sem=recv2.at[3 * a + j], device_id=(x, y, 1 - c),
                                             device_id_type=MESH).start()

    sems = pltpu.SemaphoreType.DMA((3 * n,))
    send2, recv2, *relayed, token = pl.pallas_call(
        body, name=name, in_specs=[STRICT_HBM_SPEC] * n + [SEM_SPEC, SEM_SPEC, HBM_SPEC],
        out_shape=(sems, sems, *[pltpu.HBM(s.shape, s.dtype) for s in flying], _sds((8, 128), F32)),
        out_specs=(SEM_SPEC, SEM_SPEC, *[STRICT_HBM_SPEC] * n, _token_spec()),
        input_output_aliases={a: a + 2 for a in range(n)},
        compiler_params=pltpu.CompilerParams(has_side_effects=EFFECT))(*flying, send, recv, after)
    return send2, recv2, relayed, token


def _allgather_wait(relayed, send2, recv2, after, name):
    n = len(relayed)

    def body(*refs):
        src, send_ref, recv_ref = refs[:n], refs[n], refs[n + 1]
        x, y, c, chips = _place()
        for a in range(n):
            for j, chip in enumerate(chips):
                cp = pltpu.make_async_remote_copy(
                    src_ref=_slab_half(src[a], chip, c), dst_ref=_slab_half(src[a], chip, 1 - c),
                    send_sem=send_ref.at[3 * a + j], recv_sem=recv_ref.at[3 * a + j],
                    device_id=(x, y, 1 - c), device_id_type=MESH)
                cp.wait_send()
                cp.wait_recv()

    return pl.pallas_call(
        body, name=name, in_specs=[STRICT_HBM_SPEC] * n + [SEM_SPEC, SEM_SPEC, HBM_SPEC],
        out_shape=tuple(pltpu.HBM(s.shape, s.dtype) for s in relayed), out_specs=tuple([STRICT_HBM_SPEC] * n),
        input_output_aliases={a: a for a in range(n)},
        compiler_params=pltpu.CompilerParams(has_side_effects=EFFECT))(*relayed, send2, recv2, after)


def _allgather_small(small, name):
    def body(sm, osm, send, recv, local):
        x, y, c, chips = _place()
        own = pltpu.make_async_copy(sm, osm.at[2 * x + y], local)
        own.start()
        cps = [pltpu.make_async_remote_copy(src_ref=sm, dst_ref=osm.at[2 * x + y], send_sem=send.at[j],
                                            recv_sem=recv.at[j], device_id=(*chip, c), device_id_type=MESH)
               for j, chip in enumerate(chips)]
        for cp in cps:
            cp.start()
        for j, chip in enumerate(chips):
            got = osm.at[2 * chip[0] + chip[1]]
            pltpu.make_async_remote_copy(src_ref=got, dst_ref=got, send_sem=send.at[j], recv_sem=recv.at[j],
                                         device_id=(x, y, c), device_id_type=MESH).wait_recv()
        for cp in cps:
            cp.wait_send()
        own.wait()

    return pl.pallas_call(
        body, name=name, in_specs=[pl.BlockSpec(memory_space=pltpu.VMEM)], out_specs=HBM_SPEC,
        out_shape=_sds((N_CHIPS, *small.shape), small.dtype),
        scratch_shapes=[pltpu.SemaphoreType.DMA((3,)), pltpu.SemaphoreType.DMA((3,)), pltpu.SemaphoreType.DMA])(small)


def _reduce_start(grads, name):
    n = len(grads)

    def body(*refs):
        src, land, send, recv, token = refs[:n], refs[n:2 * n], refs[2 * n], refs[2 * n + 1], refs[-1]
        x, y, c, _ = _place()
        me = 4 * x + 2 * y + c
        for a in range(n):
            for k, (peer, p_chip, p_core, _) in enumerate(_peers(x, y, c)):
                pltpu.make_async_remote_copy(
                    src_ref=src[a].at[p_chip, _half_rows(p_core, src[a].shape[1]), :], dst_ref=land[a].at[me],
                    send_sem=send.at[(N_DEV - 1) * a + k], recv_sem=recv.at[(N_DEV - 1) * a + k],
                    device_id=peer, device_id_type=MESH).start()
        token[...] = jnp.zeros_like(token)

    lands = [lax.empty((N_DEV, g.shape[1] // 2, g.shape[2]), BF16) for g in grads]
    sems = pltpu.SemaphoreType.DMA(((N_DEV - 1) * n,))
    shapes = [pltpu.HBM(a.shape, a.dtype) for a in grads + lands]
    send, recv, *flying, token = pl.pallas_call(
        body, name=name, in_specs=[STRICT_HBM_SPEC] * (2 * n),
        out_shape=(sems, sems, *shapes, _sds((8, 128), F32)),
        out_specs=(SEM_SPEC, SEM_SPEC, *[STRICT_HBM_SPEC] * (2 * n), _token_spec()),
        input_output_aliases={a: a + 2 for a in range(2 * n)},
        compiler_params=pltpu.CompilerParams(has_side_effects=EFFECT))(*[_hbm(a) for a in grads + lands])
    return send, recv, flying[:n], flying[n:], token


def _reduce_wait(started, after, name):
    sizes = [len(grads) for _, _, grads, _ in started]
    n_arr = 2 * sum(sizes)

    def body(*refs):
        x, y, c, _ = _place()
        at = 0
        for s, n in enumerate(sizes):
            src, land = refs[at:at + n], refs[at + n:at + 2 * n]
            send_ref, recv_ref = refs[n_arr + 2 * s], refs[n_arr + 2 * s + 1]
            at += 2 * n
            for a in range(n):
                for k, (peer, p_chip, p_core, p_dev) in enumerate(_peers(x, y, c)):
                    cp = pltpu.make_async_remote_copy(
                        src_ref=src[a].at[p_chip, _half_rows(p_core, src[a].shape[1]), :], dst_ref=land[a].at[p_dev],
                        send_sem=send_ref.at[(N_DEV - 1) * a + k], recv_sem=recv_ref.at[(N_DEV - 1) * a + k],
                        device_id=peer, device_id_type=MESH)
                    cp.wait_send()
                    cp.wait_recv()

    arrays, sems = [], []
    for send, recv, grads, lands in started:
        arrays += list(grads) + list(lands)
        sems += [send, recv]
    out = pl.pallas_call(
        body, name=name, in_specs=[STRICT_HBM_SPEC] * n_arr + [SEM_SPEC] * len(sems) + [HBM_SPEC],
        out_shape=tuple(pltpu.HBM(a.shape, a.dtype) for a in arrays), out_specs=tuple([STRICT_HBM_SPEC] * n_arr),
        input_output_aliases={a: a for a in range(n_arr)},
        compiler_params=pltpu.CompilerParams(has_side_effects=EFFECT))(*arrays, *sems, after)
    result, at = [], 0
    for n in sizes:
        result.append((out[at:at + n], out[at + n:at + 2 * n]))
        at += 2 * n
    return result


def _reduce_sum(grad, land, place, name, deps=()):
    _, r2, C = land.shape
    tr = _row_tile(r2, C, 4, 1024 * 1024, 16)
    nb = r2 // tr

    def body(place_ref, own_ref, *rest):
        del place_ref
        acc = own_ref[...].astype(F32)
        for ref in rest[:N_DEV - 1]:
            acc = acc + ref[...].astype(F32)
        rest[-1][...] = acc

    def from_dev(k):
        return pl.BlockSpec((None, tr, C), lambda i, place_ref: ((place_ref[2] + k) % N_DEV, i, 0))

    grid_spec = pltpu.PrefetchScalarGridSpec(
        num_scalar_prefetch=1, grid=(nb,),
        in_specs=[pl.BlockSpec((None, tr, C), lambda i, place_ref: (place_ref[0], place_ref[1] * nb + i, 0))]
        + [from_dev(k) for k in range(1, N_DEV)] + [HBM_SPEC] * len(deps),
        out_specs=pl.BlockSpec((tr, C), lambda i, place_ref: (place_ref[1] * nb + i, 0)))
    return pl.pallas_call(body, name=name, grid_spec=grid_spec, out_shape=_sds((2 * r2, C), F32),
                          compiler_params=_cparams(1))(place, grad, *[land] * (N_DEV - 1), *deps)


def _join_start(halves, name):
    n = len(halves)

    def body(*refs):
        src, send, recv, token = refs[:n], refs[n], refs[n + 1], refs[-1]
        x, y, c, _ = _place()
        for w in range(n):
            mine = src[w].at[_half_rows(c, src[w].shape[0]), :]
            pltpu.make_async_remote_copy(src_ref=mine, dst_ref=mine, send_sem=send.at[w], recv_sem=recv.at[w],
                                         device_id=(x, y, 1 - c), device_id_type=MESH).start()
        token[...] = jnp.zeros_like(token)

    sems = pltpu.SemaphoreType.DMA((n,))
    send, recv, *flying, token = pl.pallas_call(
        body, name=name, in_specs=[STRICT_HBM_SPEC] * n,
        out_shape=(sems, sems, *[pltpu.HBM(a.shape, a.dtype) for a in halves], _sds((8, 128), F32)),
        out_specs=(SEM_SPEC, SEM_SPEC, *[STRICT_HBM_SPEC] * n, _token_spec()),
        input_output_aliases={w: w + 2 for w in range(n)},
        compiler_params=pltpu.CompilerParams(has_side_effects=EFFECT))(*[_hbm(a) for a in halves])
    return send, recv, flying, token


def _join_wait(flying, send, recv, after, name):
    n = len(flying)

    def body(*refs):
        src, send_ref, recv_ref = refs[:n], refs[n], refs[n + 1]
        x, y, c, _ = _place()
        for w in range(n):
            cp = pltpu.make_async_remote_copy(
                src_ref=src[w].at[_half_rows(c, src[w].shape[0]), :],
                dst_ref=src[w].at[_half_rows(1 - c, src[w].shape[0]), :], send_sem=send_ref.at[w],
                recv_sem=recv_ref.at[w], device_id=(x, y, 1 - c),
                device_id_type=MESH)
            cp.wait_send()
            cp.wait_recv()

    return pl.pallas_call(
        body, name=name, in_specs=[STRICT_HBM_SPEC] * n + [SEM_SPEC, SEM_SPEC, HBM_SPEC],
        out_shape=tuple(pltpu.HBM(a.shape, a.dtype) for a in flying), out_specs=tuple([STRICT_HBM_SPEC] * n),
        input_output_aliases={w: w for w in range(n)},
        compiler_params=pltpu.CompilerParams(has_side_effects=EFFECT))(*flying, send, recv, after)


def _gather_small(packed, name):
    def body(p_ref, out, send, recv, local):
        x, y, c, _ = _place()
        me = 4 * x + 2 * y + c
        own = pltpu.make_async_copy(p_ref, out.at[me], local)
        own.start()
        cps = []
        for k in range(1, N_DEV):
            fx, fy, fc = (k >> 2) & 1, (k >> 1) & 1, k & 1
            peer = ((x + fx) % 2, (y + fy) % 2, (c + fc) % 2)
            cps.append(pltpu.make_async_remote_copy(src_ref=p_ref, dst_ref=out.at[me], send_sem=send.at[k - 1],
                                                    recv_sem=recv.at[k - 1], device_id=peer, device_id_type=MESH))
        for cp in cps:
            cp.start()
        for k in range(1, N_DEV):
            fx, fy, fc = (k >> 2) & 1, (k >> 1) & 1, k & 1
            src = out.at[4 * ((x + fx) % 2) + 2 * ((y + fy) % 2) + (c + fc) % 2]
            pltpu.make_async_remote_copy(src_ref=src, dst_ref=src, send_sem=send.at[k - 1], recv_sem=recv.at[k - 1],
                                         device_id=(x, y, c), device_id_type=MESH).wait_recv()
        for cp in cps:
            cp.wait_send()
        own.wait()

    return pl.pallas_call(
        body, name=name, in_specs=[pl.BlockSpec(memory_space=pltpu.VMEM)], out_specs=HBM_SPEC,
        out_shape=_sds((N_DEV, *packed.shape), F32),
        scratch_shapes=[pltpu.SemaphoreType.DMA((N_DEV - 1,)), pltpu.SemaphoreType.DMA((N_DEV - 1,)),
                        pltpu.SemaphoreType.DMA])(packed)


def _sum_devices(gathered, name):
    _, R, C = gathered.shape

    def body(g_ref, o_ref):
        acc = g_ref[0]
        for d in range(1, N_DEV):
            acc = acc + g_ref[d]
        o_ref[...] = acc

    tr = 8
    return pl.pallas_call(
        body, name=name, grid=(R // tr,), in_specs=[pl.BlockSpec((N_DEV, tr, C), lambda i: (0, i, 0))],
        out_specs=pl.BlockSpec((tr, C), lambda i: (i, 0)), out_shape=_sds((R, C), F32),
        compiler_params=_cparams(1))(gathered)


def _gather_start(packed, name):
    def body(src, land, send, recv, *rest):
        x, y, c, _ = _place()
        for k, (peer, _, _, _) in enumerate(_peers(x, y, c)):
            pltpu.make_async_remote_copy(src_ref=src, dst_ref=land.at[4 * x + 2 * y + c], send_sem=send.at[k],
                                         recv_sem=recv.at[k], device_id=peer, device_id_type=MESH).start()
        rest[-1][...] = jnp.zeros_like(rest[-1])

    land = lax.empty((N_DEV, *packed.shape), F32)
    sems = pltpu.SemaphoreType.DMA((N_DEV - 1,))
    return pl.pallas_call(
        body, name=name, in_specs=[STRICT_HBM_SPEC] * 2,
        out_shape=(sems, sems, pltpu.HBM(packed.shape, F32), pltpu.HBM(land.shape, F32), _sds((8, 128), F32)),
        out_specs=(SEM_SPEC, SEM_SPEC, STRICT_HBM_SPEC, STRICT_HBM_SPEC, _token_spec()),
        input_output_aliases={0: 2, 1: 3},
        compiler_params=pltpu.CompilerParams(has_side_effects=EFFECT))(_hbm(packed), _hbm(land))


def _gather_wait(started, after, name):
    n = len(started)

    def body(*refs):
        x, y, c, _ = _place()
        for s in range(n):
            src, land, send, recv = refs[2 * s], refs[2 * s + 1], refs[2 * n + 2 * s], refs[2 * n + 2 * s + 1]
            for k, (peer, _, _, p_dev) in enumerate(_peers(x, y, c)):
                cp = pltpu.make_async_remote_copy(src_ref=src, dst_ref=land.at[p_dev], send_sem=send.at[k],
                                                  recv_sem=recv.at[k], device_id=peer,
                                                  device_id_type=MESH)
                cp.wait_send()
                cp.wait_recv()

    arrays = [a for _, _, packed, land in started for a in (packed, land)]
    sems = [s for send, recv, _, _ in started for s in (send, recv)]
    out = pl.pallas_call(
        body, name=name, in_specs=[STRICT_HBM_SPEC] * (2 * n) + [SEM_SPEC] * (2 * n) + [HBM_SPEC],
        out_shape=tuple(pltpu.HBM(a.shape, a.dtype) for a in arrays), out_specs=tuple([STRICT_HBM_SPEC] * (2 * n)),
        input_output_aliases={a: a for a in range(2 * n)},
        compiler_params=pltpu.CompilerParams(has_side_effects=EFFECT))(*arrays, *sems, after)
    return [(out[2 * s], out[2 * s + 1]) for s in range(n)]


def _sum_gathered(packed, land, device, name):
    R, C = packed.shape
    tr = 8

    def body(dev_ref, own_ref, *rest):
        me = dev_ref[0]
        acc = None
        for d in range(N_DEV):
            term = jnp.where(me == d, own_ref[...], rest[d][...])
            acc = term if acc is None else acc + term
        rest[-1][...] = acc

    def slab(d):
        return pl.BlockSpec((None, tr, C), lambda i, dev_ref: (jnp.where(dev_ref[0] == d, (d + 1) % N_DEV, d), i, 0))

    grid_spec = pltpu.PrefetchScalarGridSpec(
        num_scalar_prefetch=1, grid=(R // tr,),
        in_specs=[pl.BlockSpec((tr, C), lambda i, dev_ref: (i, 0))] + [slab(d) for d in range(N_DEV)],
        out_specs=pl.BlockSpec((tr, C), lambda i, dev_ref: (i, 0)))
    return pl.pallas_call(body, name=name, grid_spec=grid_spec, out_shape=_sds((R, C), F32),
                          compiler_params=_cparams(1))(device, packed, *[land] * N_DEV)


def _pack_small(arrays):
    rows = []
    for a in arrays:
        flat = a.reshape(-1)
        pad = (-flat.shape[0]) % SMALL_COLS
        rows.append(jnp.pad(flat, (0, pad)).reshape(-1, SMALL_COLS))
    packed = jnp.concatenate(rows, axis=0)
    return jnp.pad(packed, ((0, (-packed.shape[0]) % 8), (0, 0)))


def _unpack_small(packed, shapes):
    out, row = [], 0
    for shape in shapes:
        size = math.prod(shape)
        n_rows = -(-size // SMALL_COLS)
        out.append(packed[row:row + n_rows].reshape(-1)[:size].reshape(shape))
        row += n_rows
    return out


def kernel(x, a_norm, a_w_in, a_sgu_norm, a_w_spatial, a_b_spatial, a_w_out, kv_norm, w_kv, b_norm, b_w_q, b_rel_bias, b_w_o, ffn_norm, ffn_w_gate_up, ffn_w_down, final_norm, loss_target, m_a_norm, m_a_w_in, m_a_sgu_norm, m_a_w_spatial, m_a_b_spatial, m_a_w_out, m_kv_norm, m_w_kv, m_b_norm, m_b_w_q, m_b_rel_bias, m_b_w_o, m_ffn_norm, m_ffn_w_gate_up, m_ffn_w_down, m_final_norm, v_a_norm, v_a_w_in, v_a_sgu_norm, v_a_w_spatial, v_a_b_spatial, v_a_w_out, v_kv_norm, v_w_kv, v_b_norm, v_b_w_q, v_b_rel_bias, v_b_w_o, v_ffn_norm, v_ffn_w_gate_up, v_ffn_w_down, v_final_norm):
    S, D = x.shape[1], x.shape[2]
    n_a = a_w_in.shape[0]
    n_b = b_w_q.shape[0]
    depth = ffn_w_gate_up.shape[0]
    xi, yi, ci = lax.axis_index("x"), lax.axis_index("y"), lax.axis_index("c")
    chip = 2 * xi + yi

    place = jnp.stack([chip, ci, 2 * chip + ci]).astype(jnp.int32)
    stacked = {"a_w_in": a_w_in, "a_w_out": a_w_out, "w_kv": w_kv[None], "b_w_q": b_w_q, "b_w_o": b_w_o,
               "ffn_w_gate_up": ffn_w_gate_up, "ffn_w_down": ffn_w_down}
    groups = []
    for layer in range(depth):
        if layer == 0 and n_a > 0:
            groups += [[("a_w_in", 0)], [("a_w_out", 0)]]
        elif layer < n_a:
            groups.append([("a_w_in", layer), ("a_w_out", layer)])
        elif layer == n_a:
            groups.append([("w_kv", 0), ("b_w_q", 0), ("b_w_o", 0)])
        else:
            groups.append([("b_w_q", layer - n_a), ("b_w_o", layer - n_a)])
        groups.append([("ffn_w_gate_up", layer), ("ffn_w_down", layer)])
    units = [u for group in groups for u in group]
    n_early = len(groups[0])
    slabs = [_cast_slab(stacked[k], l, place[:1], f"cast_{k}_{l}") for k, l in units[:n_early]]
    early = _allgather_start(slabs, "allgather_start_first")
    slabs = [_cast_slab(stacked[k], l, place[:1], f"cast_{k}_{l}", deps=(early[3],)) for k, l in units[n_early:]]
    late = _allgather_start(slabs, "allgather_start_rest")
    na_w, ns_w = a_norm.shape[1], a_sgu_norm.shape[1]
    small_g = _allgather_small(jnp.concatenate([a_norm, a_sgu_norm], axis=1), "allgather_small")
    a_norm_f = small_g[:, :, :na_w].transpose(1, 0, 2).reshape(n_a, N_CHIPS * na_w)
    a_sgu_f = small_g[:, :, na_w:].transpose(1, 0, 2).reshape(n_a, N_CHIPS * ns_w)
    W, relayed = {}, []

    def relay(after):
        if len(relayed) == len(groups):
            return ()
        index = sum(len(g) for g in groups[:len(relayed)])
        group = groups[len(relayed)]
        (send, recv, flying, _), first = (early, index) if index < n_early else (late, index - n_early)
        relayed.append(_allgather_relay(flying[first:first + len(group)], send, recv, first, after,
                                        f"allgather_relay_{len(relayed)}"))
        return (relayed[-1][3],)

    n_gathered = [0]

    def gathered(after):
        index = n_gathered[0]
        send2, recv2, arrays, _ = relayed[index]
        W.update(zip(groups[index], _allgather_wait(arrays, send2, recv2, after, f"allgather_wait_{index}")))
        n_gathered[0] += 1

    xc = x.reshape(S, D)
    saved = []
    kvp = x_kv = h_kv = None
    relay(late[3])
    order = ()
    for layer in range(depth):
        rec = {"x_in": xc}
        gathered(xc)
        if layer < n_a:
            i = layer
            rec["zpre"], rec["h"] = _norm_matmul(xc, a_norm_f[i][None], W["a_w_in", i], BF16, f"a{i}_in", deps=order)
            order = relay(rec["h"])
            rec["uv"] = _sgu_fwd(rec["zpre"], a_sgu_f[i][None], a_w_spatial[i], a_b_spatial[i].T, f"a{i}_sgu",
                                 deps=order)
            order = ()
            if ("a_w_out", i) not in W:
                gathered(rec["uv"])
                order = relay(rec["uv"])
            xm = _matmul_res(rec["uv"], W["a_w_out", i], xc, f"a{i}_out", tm=512, deps=order)
        else:
            i = layer - n_a
            if i == 0:
                kvp, h_kv = _norm_matmul(xc, kv_norm[None], W["w_kv", 0], BF16, "kv_proj", tm=KV_PAD, zero_rows=KV_PAD)
                x_kv = xc
            rec["q"], rec["h"] = _norm_matmul(xc, b_norm[i][None], W["b_w_q", i], BF16, f"b{i}_q", row_sharded=True)
            order = relay(rec["h"])
            table = jnp.pad(b_rel_bias[i], ((0, 0), (0, TABLE_PAD - b_rel_bias.shape[2])))
            rec["bias"] = _rel_bias_fwd(table, f"b{i}_bias").transpose(1, 0, 2)
            rec["o"] = _attn_fwd(rec["q"], kvp, rec["bias"], f"b{i}_attn", deps=order)
            xm = _matmul_res(rec["o"], W["b_w_o", i], xc, f"b{i}_o", tm=512)
        rec["x_mid"] = xm
        gathered(xm)
        rec["gu"], rec["h_f"] = _norm_matmul(xm, ffn_norm[layer][None], W["ffn_w_gate_up", layer], BF16, f"f{layer}_in")
        order = relay(rec["h_f"])
        xc = _matmul_res(rec["gu"], W["ffn_w_down", layer], xm, f"f{layer}_out", swiglu=True, deps=order, tm=512)
        order = ()
        saved.append(rec)

    loss_tile, dx, dxb, d_final = _loss_head(xc, final_norm[None], loss_target.reshape(S, D), "loss_head")

    started = []
    small_started = []
    pending = []

    held = []

    def weight_grad(unit, hold=False, **kw):
        full = (N_CHIPS,) + tuple(stacked[unit[0]].shape[1:])
        g = _matmul_tn(out_shape=full, name=f"d_{unit[0]}_{unit[1]}", deps=tuple(pending), **kw)
        pending.clear()
        held.append((unit, g))
        if hold:
            return ()
        send, recv, flying_g, flying_land, token = _reduce_start([g for _, g in held],
                                                                 f"reduce_start_{unit[0]}_{unit[1]}")
        started.append(([u for u, _ in held], send, recv, flying_g, flying_land))
        held.clear()
        return (token,)

    tt = min(1024, S)
    tw = min(2048, S)
    tb = min(NORMBWD_ROWS, S)
    row_a = lambda w, rows=tw: pl.BlockSpec((rows, w), lambda o, t: (t, 0))
    d_ffn_norm, d_b_norm, d_a_norm, d_a_sgu = [None] * depth, [None] * n_b, [None] * n_a, [None] * n_a
    d_ws, d_bs, d_rel = [None] * n_a, [None] * n_a, [None] * n_b
    dkv = None
    first = lambda ref: ref[...]
    for layer in reversed(range(depth)):
        rec = saved[layer]
        r_d = ffn_w_down.shape[1]
        half_f = 2 * r_d
        token = weight_grad(
            ("ffn_w_down", layer), a_ops=[rec["gu"], rec["gu"]],
            a_specs=[pl.BlockSpec((tt, half_f), lambda o, t: (t, o)), pl.BlockSpec((tt, half_f), lambda o, t: (t, 2 + o))],
            a_fn=lambda g_ref, u_ref: _swiglu(g_ref[...].astype(F32), u_ref[...].astype(F32)).astype(BF16),
            b_op=dxb, b_spec=row_a(D, tt), out_spec=pl.BlockSpec((2, r_d, D), lambda o, t: (o, 0, 0)),
            acc_shape=(half_f, D), n_outer=2, tt=tt, hold=True)
        dgu = _nt_swiglu_bwd(dxb, W["ffn_w_down", layer], rec["gu"], f"f{layer}_dgu", deps=token)
        nsh = ffn_w_gate_up.shape[2]
        token = weight_grad(
            ("ffn_w_gate_up", layer), a_ops=[rec["h_f"]], a_specs=[row_a(D)], a_fn=first,
            b_op=dgu, b_spec=pl.BlockSpec((None, tw, nsh), lambda o, t: (o // 2, t, o % 2)),
            out_spec=pl.BlockSpec((None, D, nsh), lambda o, t: (o, 0, 0)), acc_shape=(D, nsh), n_outer=N_CHIPS, tt=tw)
        dx, dxb, d_ffn_norm[layer] = _nt_normbwd(
            dgu, pl.BlockSpec((None, tb, nsh), lambda i, k: (k // 2, i, k % 2)),
            W["ffn_w_gate_up", layer], pl.BlockSpec((None, D, nsh), lambda i, k: (k, 0, 0)),
            (D, nsh), N_CHIPS, rec["x_mid"], ffn_norm[layer][None], dx, f"f{layer}_dx", deps=token)
        if layer >= n_a:
            i = layer - n_a
            r_o = b_w_o.shape[1]
            token = weight_grad(
                ("b_w_o", i), a_ops=[rec["o"]], a_specs=[row_a(D)], a_fn=first, b_op=dxb, b_spec=row_a(D),
                out_spec=pl.BlockSpec((N_CHIPS, r_o, D), lambda o, t: (0, 0, 0)), acc_shape=(D, D), n_outer=1, tt=tw,
                hold=True)
            do = _nt_rows(dxb, W["b_w_o", i], N_CHIPS, BF16, f"b{i}_do", deps=token)
            dq, dkv, dbias, *dkv_bf16 = _attn_bwd(rec["q"], kvp, rec["bias"], do, dkv, f"b{i}_attn_bwd")
            d_rel[i] = _rel_bias_bwd(dbias.transpose(1, 0, 2), f"b{i}_dbias")[:, :b_rel_bias.shape[2]]
            token = weight_grad(
                ("b_w_q", i), a_ops=[rec["h"]], a_specs=[row_a(D)], a_fn=first, b_op=dq, b_spec=row_a(D),
                out_spec=pl.BlockSpec((N_CHIPS, r_o, D), lambda o, t: (0, 0, 0)), acc_shape=(D, D), n_outer=1, tt=tw)
            dx, dxb, d_b_norm[i] = _nt_normbwd(
                dq, pl.BlockSpec((tb, D), lambda i_, k: (i_, 0)),
                W["b_w_q", i], pl.BlockSpec((N_CHIPS, r_o, D), lambda i_, k: (0, 0, 0)),
                (D, D), 1, rec["x_in"], b_norm[i][None], dx, f"b{i}_dx", deps=token)
            if i == 0:
                dkv_b = dkv_bf16[0] if dkv_bf16 else dkv[:, KV_PAD:, :].astype(BF16)
                n_kv = w_kv.shape[1]
                token = weight_grad(
                    ("w_kv", 0), a_ops=[h_kv], a_specs=[row_a(D)], a_fn=first,
                    b_op=dkv_b, b_spec=pl.BlockSpec((None, tw, n_kv), lambda o, t: (o // 2, t, o % 2)),
                    out_spec=pl.BlockSpec((None, D, n_kv), lambda o, t: (o, 0, 0)), acc_shape=(D, n_kv),
                    n_outer=N_CHIPS, tt=tw)
                dx, dxb, d_kv_norm = _nt_normbwd(
                    dkv_b, pl.BlockSpec((None, tb, n_kv), lambda i_, k: (k // 2, i_, k % 2)),
                    W["w_kv", 0], pl.BlockSpec((None, D, n_kv), lambda i_, k: (k, 0, 0)),
                    (D, n_kv), N_CHIPS, x_kv, kv_norm[None], dx, "kv_dx", deps=token)
        else:
            i = layer
            r_w = a_w_out.shape[1]
            token = weight_grad(
                ("a_w_out", i), a_ops=[rec["uv"]], a_specs=[row_a(N_CHIPS * r_w, tw // 2)], a_fn=first,
                b_op=dxb, b_spec=row_a(D, tw // 2), out_spec=pl.BlockSpec((N_CHIPS, r_w, D), lambda o, t: (0, 0, 0)),
                acc_shape=(N_CHIPS * r_w, D), n_outer=1, tt=tw // 2, hold=i > 0)
            duv = _nt_rows(dxb, W["a_w_out", i], 2, BF16, f"a{i}_duv", deps=token)
            dz, d_a_sgu[i], d_ws[i], dbs = _sgu_bwd(rec["zpre"], duv, a_sgu_f[i][None], a_w_spatial[i],
                                                  a_w_spatial[i].transpose(0, 2, 1), a_b_spatial[i].T, f"a{i}_sgu_bwd")
            d_bs[i] = dbs[:, :, 0]
            if i == 0:
                batch = [d_a_sgu[0], d_ws[0][None], d_bs[0][None], d_ffn_norm[0]]
                small_started.append((batch, _gather_start(_pack_small(batch), "gather_start_late")))
                pending.append(small_started[-1][1][4])
            n_in = a_w_in.shape[2]
            token = weight_grad(
                ("a_w_in", i), a_ops=[rec["h"]], a_specs=[row_a(D)], a_fn=first,
                b_op=dz, b_spec=pl.BlockSpec((tw, n_in), lambda o, t: (t, o)),
                out_spec=pl.BlockSpec((None, D, n_in), lambda o, t: (o, 0, 0)), acc_shape=(D, n_in), n_outer=N_CHIPS,
                tt=tw)
            dx, dxb, d_a_norm[i] = _nt_normbwd(
                dz, pl.BlockSpec((tb, n_in), lambda i_, k: (i_, k)),
                W["a_w_in", i], pl.BlockSpec((None, D, n_in), lambda i_, k: (k, 0, 0)),
                (D, n_in), N_CHIPS, rec["x_in"], a_norm_f[i][None], dx, f"a{i}_dx", deps=token)
        if layer == 1:
            batch = [jnp.concatenate(d_a_norm[1:], axis=0), jnp.concatenate(d_a_sgu[1:], axis=0), jnp.stack(d_ws[1:]),
                     jnp.stack(d_bs[1:]), d_kv_norm, jnp.concatenate(d_b_norm, axis=0), jnp.stack(d_rel),
                     jnp.concatenate(d_ffn_norm[1:], axis=0), d_final, loss_tile[:1, :1]]
            small_started.append((batch, _gather_start(_pack_small(batch), "gather_start_early")))
            pending.append(small_started[-1][1][4])
    grad_x = dx.reshape(x.shape)

    landed = _reduce_wait([(send, recv, g, land) for _, send, recv, g, land in started], dx, "reduce_wait")
    reduced_units = [unit for units_, *_ in started for unit in units_]
    arrived = dict(zip(reduced_units, [pair for gs, lands in landed for pair in zip(gs, lands)]))
    ffn_units = [u for u in reduced_units if u[0].startswith("ffn_")]
    other_units = [u for u in reduced_units if not u[0].startswith("ffn_")]
    halves = [_reduce_sum(*arrived[u], place, f"reduce_sum_{u[0]}_{u[1]}") for u in ffn_units]
    ffn_send, ffn_recv, ffn_flying, ffn_token = _join_start(halves, "join_start_ffn")
    halves = [_reduce_sum(*arrived[u], place, f"reduce_sum_{u[0]}_{u[1]}", deps=(ffn_token,)) for u in other_units]
    other_send, other_recv, other_flying, other_token = _join_start(halves, "join_start_rest")
    joined = dict(zip(ffn_units, _join_wait(ffn_flying, ffn_send, ffn_recv, other_token, "join_wait_ffn")))
    reduced = {}

    (packed_e, land_e), (packed_l, land_l) = _gather_wait([s[:4] for _, s in small_started], dx, "gather_wait")
    total_e = _sum_gathered(packed_e, land_e, place[2:], "sum_small_grads_early")
    total_l = _sum_gathered(packed_l, land_l, place[2:], "sum_small_grads_late")
    total_t = _sum_devices(_gather_small(_pack_small([d_a_norm[0]]), "gather_small_grads_last"), "sum_small_grads_last")
    (e_a_norm, e_a_sgu, e_ws, e_bs, g_kv_norm, g_b_norm, g_rel, e_ffn_norm, g_final, loss) = _unpack_small(
        total_e, [a.shape for a in small_started[0][0]])
    l_a_sgu, l_ws, l_bs, l_ffn_norm = _unpack_small(total_l, [a.shape for a in small_started[1][0]])
    (t_a_norm,) = _unpack_small(total_t, [d_a_norm[0].shape])
    g_a_norm = jnp.concatenate([t_a_norm, e_a_norm], axis=0)
    g_a_sgu = jnp.concatenate([l_a_sgu, e_a_sgu], axis=0)
    g_ws = jnp.concatenate([l_ws, e_ws], axis=0)
    g_bs = jnp.concatenate([l_bs, e_bs], axis=0)
    g_ffn_norm = jnp.concatenate([l_ffn_norm, e_ffn_norm], axis=0)
    reduced["a_norm"] = lax.dynamic_slice_in_dim(g_a_norm, chip * na_w, na_w, axis=1)
    reduced["a_sgu_norm"] = lax.dynamic_slice_in_dim(g_a_sgu, chip * ns_w, ns_w, axis=1)
    reduced.update(a_w_spatial=g_ws, a_b_spatial=g_bs, kv_norm=g_kv_norm.reshape(kv_norm.shape), b_norm=g_b_norm,
                   b_rel_bias=g_rel, ffn_norm=g_ffn_norm, final_norm=g_final.reshape(final_norm.shape))

    weights = dict(a_norm=a_norm, a_w_in=a_w_in, a_sgu_norm=a_sgu_norm, a_w_spatial=a_w_spatial,
                   a_b_spatial=a_b_spatial, a_w_out=a_w_out, kv_norm=kv_norm, w_kv=w_kv, b_norm=b_norm, b_w_q=b_w_q,
                   b_rel_bias=b_rel_bias, b_w_o=b_w_o, ffn_norm=ffn_norm, ffn_w_gate_up=ffn_w_gate_up,
                   ffn_w_down=ffn_w_down, final_norm=final_norm)
    m_in = dict(a_norm=m_a_norm, a_w_in=m_a_w_in, a_sgu_norm=m_a_sgu_norm, a_w_spatial=m_a_w_spatial,
                a_b_spatial=m_a_b_spatial, a_w_out=m_a_w_out, kv_norm=m_kv_norm, w_kv=m_w_kv, b_norm=m_b_norm,
                b_w_q=m_b_w_q, b_rel_bias=m_b_rel_bias, b_w_o=m_b_w_o, ffn_norm=m_ffn_norm,
                ffn_w_gate_up=m_ffn_w_gate_up, ffn_w_down=m_ffn_w_down, final_norm=m_final_norm)
    v_in = dict(a_norm=v_a_norm, a_w_in=v_a_w_in, a_sgu_norm=v_a_sgu_norm, a_w_spatial=v_a_w_spatial,
                a_b_spatial=v_a_b_spatial, a_w_out=v_a_w_out, kv_norm=v_kv_norm, w_kv=v_w_kv, b_norm=v_b_norm,
                b_w_q=v_b_w_q, b_rel_bias=v_b_rel_bias, b_w_o=v_b_w_o, ffn_norm=v_ffn_norm,
                ffn_w_gate_up=v_ffn_w_gate_up, ffn_w_down=v_ffn_w_down, final_norm=v_final_norm)
    results = {}

    def adamw_large(key):
        as_layers = lambda a: a.reshape(stacked[key].shape)
        results[key] = _adamw_stacked(
            as_layers(weights[key]), [joined[key, layer] for layer in range(stacked[key].shape[0])],
            as_layers(m_in[key]), as_layers(v_in[key]), "adamw_" + key)

    for key in ("ffn_w_gate_up", "ffn_w_down"):
        adamw_large(key)
    joined.update(zip(other_units, _join_wait(other_flying, other_send, other_recv, results["ffn_w_down"][1],
                                              "join_wait_rest")))
    for key, w in weights.items():
        if key in stacked:
            if key not in results:
                adamw_large(key)
        else:
            g = reduced[key].reshape(w.shape)
            view = (1, w.shape[0]) if w.ndim == 1 else (-1, w.shape[-1])
            d, nm, nv = _adamw(w.reshape(view), g.reshape(view), m_in[key].reshape(view), v_in[key].reshape(view),
                               "adamw_" + key)
            results[key] = (g, d, nm, nv)
    outs = [[results[key][k].reshape(w.shape) for key, w in weights.items()] for k in range(4)]
    return (loss.reshape(()), grad_x, *outs[0], *outs[1], *outs[2], *outs[3])
```

```python
import math

import jax
import jax.numpy as jnp
from jax import lax
from jax.experimental import pallas as pl
from jax.experimental.pallas import tpu as pltpu

F32, BF16 = jnp.float32, jnp.bfloat16
MESH = pl.DeviceIdType.MESH
HIGHEST = lax.Precision.HIGHEST
NT_DIMS = (((1,), (1,)), ((), ()))
TN_DIMS = (((0,), (0,)), ((), ()))

EPS = 1e-6
CHUNK = 64
A_CHUNK = 128
A_GROUPS = 8
N_HEADS = 16
HEAD_DIM = 64
N_LEFT = 8
MAX_REL = 256
ATTN_SCALE = HEAD_DIM ** -0.5
NEG_INF = -1e30
Q_BLOCK = 4 * CHUNK
KV_PAD = N_LEFT * CHUNK
BAND = KV_PAD + Q_BLOCK
DIAGS = BAND + Q_BLOCK
TABLE_PAD = 640
HEADS_PER_BLOCK = 2
BLOCKS_PER_STEP = 8
NORMBWD_ROWS = 1024

ADAM_LR, ADAM_B1, ADAM_B2, ADAM_EPS, ADAM_WD, ADAM_STEP = 0.001, 0.9, 0.999, 1e-08, 0.01, 10

VMEM_LIMIT_BYTES = 56 * 1024 * 1024
N_CHIPS = 4
N_DEV = 8
SMALL_COLS = 1024


def _cparams(n_grid):
    return pltpu.CompilerParams(dimension_semantics=("arbitrary",) * n_grid, vmem_limit_bytes=VMEM_LIMIT_BYTES)


def _sds(shape, dtype):
    return jax.ShapeDtypeStruct(tuple(shape), dtype)


def _gelu(x):
    return x * (0.5 * (1.0 + lax.erf(x * math.sqrt(0.5))))


def _gelu_and_grad(x):
    cdf = 0.5 * (1.0 + lax.erf(x * math.sqrt(0.5)))
    return x * cdf, cdf + x * (jnp.exp(-0.5 * x * x) * (1.0 / math.sqrt(2.0 * math.pi)))


def _rms_hat(xv):
    r = lax.rsqrt(jnp.mean(xv * xv, axis=-1, keepdims=True) + EPS)
    return xv * r, r


def _rms_bwd(xhat, r, g, dy):
    dxhat = dy * g
    dx = r * (dxhat - xhat * jnp.mean(dxhat * xhat, axis=-1, keepdims=True))
    return dx, dy * xhat


def _swiglu(gate, up):
    return (gate * jax.nn.sigmoid(gate)) * up


def _row_tile(rows, cols, itemsize, cap_bytes, align):
    t = rows
    while t * cols * itemsize > cap_bytes and t % (2 * align) == 0:
        t //= 2
    return t


def _cast_slab(w, layer, chip, name, deps=()):
    _, r, C = w.shape
    tr = _row_tile(r, C, 4, 4 * 1024 * 1024, 16)

    def body(chip_ref, w_ref, *rest):
        del chip_ref
        rest[-1][...] = w_ref[...].astype(BF16)

    grid_spec = pltpu.PrefetchScalarGridSpec(
        num_scalar_prefetch=1, grid=(r // tr,),
        in_specs=[pl.BlockSpec((None, tr, C), lambda i, chip_ref: (layer, i, 0))] + [HBM_SPEC] * len(deps),
        out_specs=pl.BlockSpec((None, tr, C), lambda i, chip_ref: (chip_ref[0], i, 0)))
    return pl.pallas_call(body, name=name, grid_spec=grid_spec, out_shape=_sds((N_CHIPS, r, C), BF16),
                          compiler_params=_cparams(1))(chip, w, *deps)


def _adamw_stacked(w, gs, m, v, name):
    L, r, C = w.shape
    tr = _row_tile(r, C, 4, 2 * 1024 * 1024, 8)
    nb = r // tr

    def body(w_ref, m_ref, v_ref, *rest):
        go_ref, d_ref, nm_ref, nv_ref = rest[-4:]
        layer = pl.program_id(0)
        gv = rest[0][...]
        for k in range(1, L):
            gv = jnp.where(layer == k, rest[k][...], gv)
        mn = ADAM_B1 * m_ref[...] + (1.0 - ADAM_B1) * gv
        vn = ADAM_B2 * v_ref[...] + (1.0 - ADAM_B2) * jnp.square(gv)
        m_hat = mn / (1.0 - ADAM_B1 ** ADAM_STEP)
        v_hat = vn / (1.0 - ADAM_B2 ** ADAM_STEP)
        d_ref[...] = -ADAM_LR * (m_hat / (jnp.sqrt(v_hat) + ADAM_EPS) + ADAM_WD * w_ref[...])
        nm_ref[...] = mn
        nv_ref[...] = vn
        go_ref[...] = gv

    def grad_spec(k):
        return pl.BlockSpec((tr, C), lambda l, i: (jnp.where(l == k, i, jnp.where(l > k, nb - 1, 0)), 0))

    stacked = pl.BlockSpec((None, tr, C), lambda l, i: (l, i, 0))
    return pl.pallas_call(body, name=name, grid=(L, nb), in_specs=[stacked] * 3 + [grad_spec(k) for k in range(L)],
                          out_specs=[stacked] * 4, out_shape=[_sds((L, r, C), F32)] * 4,
                          compiler_params=_cparams(2))(w, m, v, *gs)


def _adamw(w, g, m, v, name):
    R, C = w.shape
    tr = _row_tile(R, C, 4, 1024 * 1024, 8)

    def body(w_ref, g_ref, m_ref, v_ref, d_ref, nm_ref, nv_ref):
        gv = g_ref[...]
        mn = ADAM_B1 * m_ref[...] + (1.0 - ADAM_B1) * gv
        vn = ADAM_B2 * v_ref[...] + (1.0 - ADAM_B2) * jnp.square(gv)
        m_hat = mn / (1.0 - ADAM_B1 ** ADAM_STEP)
        v_hat = vn / (1.0 - ADAM_B2 ** ADAM_STEP)
        d_ref[...] = -ADAM_LR * (m_hat / (jnp.sqrt(v_hat) + ADAM_EPS) + ADAM_WD * w_ref[...])
        nm_ref[...] = mn
        nv_ref[...] = vn

    spec = pl.BlockSpec((tr, C), lambda i: (i, 0))
    return pl.pallas_call(body, name=name, grid=(R // tr,), in_specs=[spec] * 4, out_specs=[spec] * 3,
                          out_shape=[_sds((R, C), F32)] * 3, compiler_params=_cparams(1))(w, g, m, v)


def _norm_matmul(x, g, w_g, out_dtype, name, row_sharded=False, deps=(), tm=1024, zero_rows=0):
    S, D = x.shape
    tm = min(tm, S)
    lead = zero_rows // tm
    if row_sharded:
        r, N = w_g.shape[1], w_g.shape[2]
        tn = 512
        w_spec = pl.BlockSpec((N_CHIPS, r, tn), lambda i, j: (0, 0, j))
    else:
        nsh = w_g.shape[2]
        N = N_CHIPS * nsh
        tn = next((t for t in (1024, 512) if nsh % t == 0), nsh)
        bps = nsh // tn
        w_spec = pl.BlockSpec((None, D, tn), lambda i, j: (j // bps, 0, j % bps))
    whole = not row_sharded and N <= 2048
    if whole:
        tn = N
        w_spec = pl.BlockSpec((N_CHIPS, D, nsh), lambda i, j: (0, 0, 0))

    def body(x_ref, g_ref, w_ref, *rest):
        y_ref, h_ref = rest[-2:]
        i = pl.program_id(0)

        @pl.when((i >= lead) & (pl.program_id(1) == 0))
        def _():
            xhat, _ = _rms_hat(x_ref[...])
            h_ref[...] = (xhat * g_ref[...]).astype(BF16)

        @pl.when(i >= lead)
        def _():
            if whole:
                for s in range(N_CHIPS):
                    y_ref[:, s * nsh:(s + 1) * nsh] = jnp.dot(h_ref[...], w_ref[s],
                                                             preferred_element_type=F32).astype(y_ref.dtype)
            else:
                w = w_ref[...].reshape(D, tn)
                y_ref[...] = jnp.dot(h_ref[...], w, preferred_element_type=F32).astype(y_ref.dtype)

        if lead:
            @pl.when(i < lead)
            def _():
                y_ref[...] = jnp.zeros_like(y_ref)

    rows = lambda i, j: (jnp.maximum(i - lead, 0), 0)
    return pl.pallas_call(
        body, name=name, grid=(lead + S // tm, N // tn),
        in_specs=[pl.BlockSpec((tm, D), rows), pl.BlockSpec((1, D), lambda i, j: (0, 0)), w_spec]
        + [HBM_SPEC] * len(deps),
        out_specs=[pl.BlockSpec((tm, tn), lambda i, j: (i, j)), pl.BlockSpec((tm, D), rows)],
        out_shape=[_sds((zero_rows + S, N), out_dtype), _sds((S, D), BF16)],
        compiler_params=_cparams(2))(x, g, w_g, *deps)


def _matmul_res(a, w_g, res, name, swiglu=False, deps=(), tm=256):
    S, N = res.shape
    r = w_g.shape[1]
    K = N_CHIPS * r

    def body(*refs):
        o_ref = refs[-1]
        if swiglu:
            gate_ref, up_ref, w_ref, res_ref = refs[:4]
            a_blk = _swiglu(gate_ref[...].astype(F32), up_ref[...].astype(F32)).astype(BF16)
        else:
            a_ref, w_ref, res_ref = refs[:3]
            a_blk = a_ref[...]
        o_ref[...] = res_ref[...] + jnp.dot(a_blk, w_ref[...].reshape(K, N), preferred_element_type=F32)

    a_specs, a_ops = [pl.BlockSpec((tm, K), lambda i: (i, 0))], [a]
    if swiglu:
        a_specs.append(pl.BlockSpec((tm, K), lambda i: (i, 1)))
        a_ops.append(a)
    row = pl.BlockSpec((tm, N), lambda i: (i, 0))
    return pl.pallas_call(
        body, name=name, grid=(S // tm,),
        in_specs=a_specs + [pl.BlockSpec((N_CHIPS, r, N), lambda i: (0, 0, 0)), row] + [HBM_SPEC] * len(deps),
        out_specs=row, out_shape=_sds((S, N), F32), compiler_params=_cparams(1))(*a_ops, w_g, res, *deps)


def _matmul_tn(a_ops, a_specs, a_fn, b_op, b_spec, out_spec, out_shape, acc_shape, n_outer, name, deps=(), tt=512):
    S = b_op.shape[-2]
    na = len(a_ops)
    nt = S // tt

    def body(*refs):
        a_refs, b_ref, o_ref, acc_ref = refs[:na], refs[na], refs[-2], refs[-1]
        t = pl.program_id(1)
        part = lax.dot_general(a_fn(*a_refs), b_ref[...].astype(BF16), TN_DIMS, preferred_element_type=F32)

        @pl.when(t == 0)
        def _():
            acc_ref[...] = part

        @pl.when(t > 0)
        def _():
            acc_ref[...] += part

        @pl.when(t == nt - 1)
        def _():
            o_ref[...] = acc_ref[...].reshape(o_ref.shape).astype(BF16)

    return pl.pallas_call(
        body, name=name, grid=(n_outer, nt), in_specs=list(a_specs) + [b_spec] + [HBM_SPEC] * len(deps),
        out_specs=out_spec, out_shape=_sds(out_shape, BF16), scratch_shapes=[pltpu.VMEM(acc_shape, F32)],
        compiler_params=_cparams(2))(*a_ops, b_op, *deps)


def _nt_accumulate(a_ref, w_ref, acc_ref, w2d, nk):
    k = pl.program_id(1)
    part = lax.dot_general(a_ref[...].astype(BF16), w_ref[...].reshape(w2d), NT_DIMS, preferred_element_type=F32)

    @pl.when(k == 0)
    def _():
        acc_ref[...] = part

    @pl.when(k > 0)
    def _():
        acc_ref[...] += part

    return k == nk - 1


def _nt_normbwd(dy, dy_spec, w_g, w_spec, w2d, nk, x, g, dres, name, deps=(), tm=NORMBWD_ROWS):
    S, D = x.shape
    tm = min(tm, S)

    def body(dy_ref, w_ref, x_ref, g_ref, dres_ref, *rest):
        dx_ref, dxb_ref, dg_ref, acc_ref = rest[-4:]

        @pl.when((pl.program_id(0) == 0) & (pl.program_id(1) == 0))
        def _():
            dg_ref[...] = jnp.zeros_like(dg_ref)

        last = _nt_accumulate(dy_ref, w_ref, acc_ref, w2d, nk)

        @pl.when(last)
        def _():
            xhat, r = _rms_hat(x_ref[...])
            dx, dgp = _rms_bwd(xhat, r, g_ref[...], acc_ref[...])
            total = dres_ref[...] + dx
            dx_ref[...] = total
            dxb_ref[...] = total.astype(BF16)
            dg_ref[...] += jnp.sum(dgp, axis=0, keepdims=True)

    row = pl.BlockSpec((tm, D), lambda i, k: (i, 0))
    vec = pl.BlockSpec((1, D), lambda i, k: (0, 0))
    return pl.pallas_call(
        body, name=name, grid=(S // tm, nk),
        in_specs=[dy_spec, w_spec, row, vec, row] + [HBM_SPEC] * len(deps), out_specs=[row, row, vec],
        out_shape=[_sds((S, D), F32), _sds((S, D), BF16), _sds((1, D), F32)],
        scratch_shapes=[pltpu.VMEM((tm, D), F32)], compiler_params=_cparams(2))(dy, w_g, x, g, dres, *deps)


def _nt_rows(dy, w_g, shards_per_block, out_dtype, name, deps=(), tm=1024):
    S, N = dy.shape
    tm = min(tm, S)
    r = w_g.shape[1]
    tn = shards_per_block * r

    def body(dy_ref, w_ref, *rest):
        o_ref = rest[-1]
        o_ref[...] = lax.dot_general(dy_ref[...].astype(BF16), w_ref[...].reshape(tn, N), NT_DIMS,
                                     preferred_element_type=F32).astype(o_ref.dtype)

    return pl.pallas_call(
        body, name=name, grid=(S // tm, N_CHIPS // shards_per_block),
        in_specs=[pl.BlockSpec((tm, N), lambda i, j: (i, 0)),
                  pl.BlockSpec((shards_per_block, r, N), lambda i, j: (j, 0, 0))] + [HBM_SPEC] * len(deps),
        out_specs=pl.BlockSpec((tm, tn), lambda i, j: (i, j)),
        out_shape=_sds((S, N_CHIPS * r), out_dtype), compiler_params=_cparams(2))(dy, w_g, *deps)


def _nt_swiglu_bwd(dy, w_g, gu, name, deps=(), tm=1024):
    S, N = dy.shape
    tm = min(tm, S)
    r = w_g.shape[1]
    tn = 2 * r
    F = N_CHIPS * r

    def body(dy_ref, w_ref, gate_ref, up_ref, *rest):
        o_ref = rest[-1]
        dact = lax.dot_general(dy_ref[...].astype(BF16), w_ref[...].reshape(tn, N), NT_DIMS,
                               preferred_element_type=F32)
        gate, up = gate_ref[...].astype(F32), up_ref[...].astype(F32)
        sg = jax.nn.sigmoid(gate)
        silu = gate * sg
        o_ref[0] = ((dact * up) * (sg + silu * (1.0 - sg))).astype(BF16)
        o_ref[1] = (dact * silu).astype(BF16)

    return pl.pallas_call(
        body, name=name, grid=(2, S // tm),
        in_specs=[pl.BlockSpec((tm, N), lambda j, i: (i, 0)),
                  pl.BlockSpec((2, r, N), lambda j, i: (j, 0, 0)),
                  pl.BlockSpec((tm, tn), lambda j, i: (i, j)),
                  pl.BlockSpec((tm, tn), lambda j, i: (i, 2 + j))] + [HBM_SPEC] * len(deps),
        out_specs=pl.BlockSpec((2, tm, tn), lambda j, i: (0, i, j)),
        out_shape=_sds((2, S, F), BF16), compiler_params=_cparams(2))(dy, w_g, gu, gu, *deps)


def _chunk_causal_mask(transposed):
    i = lax.broadcasted_iota(jnp.int32, (A_CHUNK, A_CHUNK), 0) // CHUNK
    j = lax.broadcasted_iota(jnp.int32, (A_CHUNK, A_CHUNK), 1) // CHUNK
    return ((i <= j) if transposed else (i >= j)).astype(F32)


def _sgu_fwd(zpre, g_sgu, ws, bs_t, name, deps=()):
    S, F2 = zpre.shape
    F = F2 // 2
    gd = F // A_GROUPS

    windows = 2
    rows = windows * A_CHUNK

    def body(zu_ref, zv_ref, g_ref, ws_ref, b_ref, *rest):
        o_ref = rest[-1]
        vhat, _ = _rms_hat(_gelu(zv_ref[...].astype(F32)))
        vn = (vhat * g_ref[...]).astype(BF16)
        u = _gelu(zu_ref[...].astype(F32))
        mask = _chunk_causal_mask(False)
        for gi in range(A_GROUPS):
            sl = slice(gi * gd, (gi + 1) * gd)
            wm = (ws_ref[gi] * mask).astype(BF16)
            for w in range(windows):
                win = slice(w * A_CHUNK, (w + 1) * A_CHUNK)
                vs = jnp.dot(wm, vn[win, sl], preferred_element_type=F32) + b_ref[:, gi:gi + 1]
                o_ref[win, sl] = (u[win, sl] * vs).astype(BF16)

    return pl.pallas_call(
        body, name=name, grid=(S // rows,),
        in_specs=[pl.BlockSpec((rows, F), lambda i: (i, 0)),
                  pl.BlockSpec((rows, F), lambda i: (i, 1)),
                  pl.BlockSpec((1, F), lambda i: (0, 0)),
                  pl.BlockSpec((A_GROUPS, A_CHUNK, A_CHUNK), lambda i: (0, 0, 0)),
                  pl.BlockSpec((A_CHUNK, A_GROUPS), lambda i: (0, 0))] + [HBM_SPEC] * len(deps),
        out_specs=pl.BlockSpec((rows, F), lambda i: (i, 0)),
        out_shape=_sds((S, F), BF16), compiler_params=_cparams(1))(zpre, zpre, g_sgu, ws, bs_t, *deps)


def _sgu_bwd(zpre, duv, g_sgu, ws, ws_t, bs_t, name):
    S, F2 = zpre.shape
    F = F2 // 2
    gd = F // A_GROUPS

    def body(zu_ref, zv_ref, duv_ref, g_ref, ws_ref, wst_ref, b_ref, dz_ref, dg_ref, dws_ref, dbs_ref, dvn_ref):
        @pl.when(pl.program_id(0) == 0)
        def _():
            dg_ref[...] = jnp.zeros_like(dg_ref)
            dws_ref[...] = jnp.zeros_like(dws_ref)
            dbs_ref[...] = jnp.zeros_like(dbs_ref)

        gv = g_ref[...]
        u, u_grad = _gelu_and_grad(zu_ref[...].astype(F32))
        v, v_grad = _gelu_and_grad(zv_ref[...].astype(F32))
        vhat, r = _rms_hat(v)
        vn = (vhat * gv).astype(BF16)
        duv_v = duv_ref[...].astype(F32)
        dvs = duv_v * u
        dvs_b = dvs.astype(BF16)
        mask = _chunk_causal_mask(False)
        mask_t = _chunk_causal_mask(True)
        for gi in range(A_GROUPS):
            sl = slice(gi * gd, (gi + 1) * gd)
            wm = (ws_ref[gi] * mask).astype(BF16)
            vs = jnp.dot(wm, vn[:, sl], preferred_element_type=F32) + b_ref[:, gi:gi + 1]
            dz_ref[:, sl] = ((duv_v[:, sl] * vs) * u_grad[:, sl]).astype(BF16)
            dws_ref[gi] += lax.dot_general(dvs_b[:, sl], vn[:, sl], NT_DIMS, preferred_element_type=F32) * mask
            dbs_ref[gi] += jnp.broadcast_to(jnp.sum(dvs[:, sl], axis=1, keepdims=True), (A_CHUNK, A_CHUNK))
            wm_t = (wst_ref[gi] * mask_t).astype(BF16)
            dvn_ref[:, sl] = jnp.dot(wm_t, dvs_b[:, sl], preferred_element_type=F32)
        dv, dg_part = _rms_bwd(vhat, r, gv, dvn_ref[...])
        dg_ref[...] += jnp.sum(dg_part, axis=0, keepdims=True)
        dz_ref[:, F:] = (dv * v_grad).astype(BF16)

    blk = pl.BlockSpec((A_CHUNK, F), lambda i: (i, 0))
    const3 = pl.BlockSpec((A_GROUPS, A_CHUNK, A_CHUNK), lambda i: (0, 0, 0))
    return pl.pallas_call(
        body, name=name, grid=(S // A_CHUNK,),
        in_specs=[blk, pl.BlockSpec((A_CHUNK, F), lambda i: (i, 1)), blk,
                  pl.BlockSpec((1, F), lambda i: (0, 0)), const3, const3,
                  pl.BlockSpec((A_CHUNK, A_GROUPS), lambda i: (0, 0))],
        out_specs=[pl.BlockSpec((A_CHUNK, F2), lambda i: (i, 0)), pl.BlockSpec((1, F), lambda i: (0, 0)),
                   const3, const3],
        out_shape=[_sds((S, F2), BF16), _sds((1, F), F32), _sds((A_GROUPS, A_CHUNK, A_CHUNK), F32),
                   _sds((A_GROUPS, A_CHUNK, A_CHUNK), F32)],
        scratch_shapes=[pltpu.VMEM((A_CHUNK, F), F32)],
        compiler_params=_cparams(1))(zpre, zpre, duv, g_sgu, ws, ws_t, bs_t)


def _toeplitz_one_hot():
    row = lax.broadcasted_iota(jnp.int32, (TABLE_PAD, DIAGS), 0)
    j = lax.broadcasted_iota(jnp.int32, (TABLE_PAD, DIAGS), 1)
    idx = jnp.clip(KV_PAD + Q_BLOCK - j, -MAX_REL, MAX_REL) + MAX_REL
    return (row == idx).astype(F32)


def _rel_bias_fwd(table, name):
    H = table.shape[0]

    def body(t_ref, o_ref):
        diag = jnp.dot(t_ref[...], _toeplitz_one_hot(), precision=HIGHEST, preferred_element_type=F32)
        q_chunk = lax.broadcasted_iota(jnp.int32, (Q_BLOCK, BAND), 0) // CHUNK
        k_chunk = lax.broadcasted_iota(jnp.int32, (Q_BLOCK, BAND), 1) // CHUNK
        unseen = jnp.where((k_chunk >= q_chunk) & (k_chunk <= q_chunk + N_LEFT), 0.0, NEG_INF)
        for h in range(H):
            rows = jnp.broadcast_to(diag[h:h + 1, :], (Q_BLOCK, DIAGS))
            o_ref[h] = pltpu.roll(rows, DIAGS - Q_BLOCK, 1, stride=1, stride_axis=0)[:, :BAND] + unseen

    return pl.pallas_call(body, name=name, out_shape=_sds((H, Q_BLOCK, BAND), F32),
                          compiler_params=pltpu.CompilerParams(vmem_limit_bytes=VMEM_LIMIT_BYTES))(table)


def _rel_bias_bwd(dbias, name):
    H = dbias.shape[0]

    def body(d_ref, o_ref):
        def step(r, acc):
            row = d_ref[:, pl.ds(r, 1), :].reshape(H, BAND)
            row = jnp.concatenate([row, jnp.zeros((H, DIAGS - BAND), F32)], axis=1)
            return acc + pltpu.roll(row, Q_BLOCK - r, 1)

        diag = lax.fori_loop(0, Q_BLOCK, step, jnp.zeros((H, DIAGS), F32))
        o_ref[...] = lax.dot_general(diag, _toeplitz_one_hot(), NT_DIMS, precision=HIGHEST,
                                     preferred_element_type=F32)

    return pl.pallas_call(body, name=name, out_shape=_sds((H, TABLE_PAD), F32),
                          compiler_params=pltpu.CompilerParams(vmem_limit_bytes=VMEM_LIMIT_BYTES))(dbias)


def _head_rows(t):
    lane = lax.broadcasted_iota(jnp.int32, t.shape, 1)
    zero = jnp.zeros_like(t)
    return jnp.concatenate([jnp.where(lane < HEAD_DIM, t, zero), jnp.where(lane >= HEAD_DIM, t, zero)], axis=0)


def _head_lanes(t2):
    lane = lax.broadcasted_iota(jnp.int32, (Q_BLOCK, t2.shape[1]), 1)
    return jnp.where(lane < HEAD_DIM, t2[:Q_BLOCK], t2[Q_BLOCK:])


def _attn_probs(q2, kb, bias2, block):
    kj = lax.broadcasted_iota(jnp.int32, (1, BAND), 1)
    before_start = jnp.where(block * Q_BLOCK + kj - KV_PAD >= 0, 0.0, NEG_INF)
    s = lax.dot_general(q2 * ATTN_SCALE, kb, NT_DIMS, preferred_element_type=F32) + bias2 + before_start
    e = jnp.exp(s - jnp.max(s, axis=-1, keepdims=True))
    return e / jnp.sum(e, axis=-1, keepdims=True)


def _attn_specs(S):
    lanes = HEADS_PER_BLOCK * HEAD_DIM
    rows = S + KV_PAD
    per_step = min(BLOCKS_PER_STEP, S // Q_BLOCK)
    q_spec = pl.BlockSpec((per_step * Q_BLOCK, lanes), lambda h, i: (i, h))
    k_spec = pl.BlockSpec((rows, lanes), lambda h, i: (0, h))
    v_spec = pl.BlockSpec((rows, lanes), lambda h, i: (0, N_HEADS // HEADS_PER_BLOCK + h))
    b_spec = pl.BlockSpec((HEADS_PER_BLOCK, Q_BLOCK, BAND), lambda h, i: (h, 0, 0))
    return q_spec, k_spec, v_spec, b_spec, per_step


def _attn_fwd(q, kvp, bias, name, deps=()):
    S, HD = q.shape
    q_spec, k_spec, v_spec, b_spec, per_step = _attn_specs(S)

    def body(q_ref, k_ref, v_ref, b_ref, *rest):
        o_ref = rest[-1]
        for b in range(per_step):
            block = pl.program_id(1) * per_step + b
            rows = slice(b * Q_BLOCK, (b + 1) * Q_BLOCK)
            band = pl.ds(pl.multiple_of(block * Q_BLOCK, Q_BLOCK), BAND)
            p = _attn_probs(_head_rows(q_ref[rows, :]), k_ref[band, :], b_ref[...].reshape(2 * Q_BLOCK, BAND), block)
            o2 = jnp.dot(p.astype(BF16), v_ref[band, :], preferred_element_type=F32)
            o_ref[rows, :] = _head_lanes(o2).astype(BF16)

    return pl.pallas_call(
        body, name=name, grid=(N_HEADS // HEADS_PER_BLOCK, S // (per_step * Q_BLOCK)),
        in_specs=[q_spec, k_spec, v_spec, b_spec] + [HBM_SPEC] * len(deps), out_specs=q_spec,
        out_shape=_sds((S, HD), BF16), compiler_params=_cparams(2))(q, kvp, kvp, bias, *deps)


def _attn_bwd(q, kvp, bias, do, dkv_prev, name):
    S, HD = q.shape
    lanes = HEADS_PER_BLOCK * HEAD_DIM
    q_spec, k_spec, v_spec, b_spec, per_step = _attn_specs(S)
    dkv_spec = pl.BlockSpec((2, S + KV_PAD, lanes), lambda h, i: (0, 0, h))
    prev = [] if dkv_prev is None else [dkv_prev]
    n_steps = S // (per_step * Q_BLOCK)

    def body(q_ref, k_ref, v_ref, b_ref, do_ref, *rest):
        dq_ref, dkv_ref, db_ref = rest[len(prev):len(prev) + 3]

        @pl.when(pl.program_id(1) == 0)
        def _():
            dkv_ref[...] = rest[0][...] if prev else jnp.zeros_like(dkv_ref)
            db_ref[...] = jnp.zeros_like(db_ref)

        db = jnp.zeros((2 * Q_BLOCK, BAND), F32)
        for b in range(per_step):
            block = pl.program_id(1) * per_step + b
            rows = slice(b * Q_BLOCK, (b + 1) * Q_BLOCK)
            band = pl.ds(pl.multiple_of(block * Q_BLOCK, Q_BLOCK), BAND)
            kb, vb = k_ref[band, :], v_ref[band, :]
            q2, do2 = _head_rows(q_ref[rows, :]), _head_rows(do_ref[rows, :])
            p = _attn_probs(q2, kb, b_ref[...].reshape(2 * Q_BLOCK, BAND), block)
            dp = lax.dot_general(do2, vb, NT_DIMS, preferred_element_type=F32)
            ds = p * (dp - jnp.sum(dp * p, axis=-1, keepdims=True))
            db = db + ds
            ds_b = (ds * ATTN_SCALE).astype(BF16)
            dq_ref[rows, :] = _head_lanes(jnp.dot(ds_b, kb, preferred_element_type=F32)).astype(BF16)
            dkv_ref[0, band, :] += lax.dot_general(ds_b, q2, TN_DIMS, preferred_element_type=F32)
            dkv_ref[1, band, :] += lax.dot_general(p.astype(BF16), do2, TN_DIMS, preferred_element_type=F32)
        db_ref[...] += db.reshape(HEADS_PER_BLOCK, Q_BLOCK, BAND)

        if prev:
            @pl.when(pl.program_id(1) == n_steps - 1)
            def _():
                rest[-1][...] = dkv_ref[:, KV_PAD:, :].astype(BF16)

    return pl.pallas_call(
        body, name=name, grid=(N_HEADS // HEADS_PER_BLOCK, n_steps),
        in_specs=[q_spec, k_spec, v_spec, b_spec, q_spec] + [dkv_spec] * len(prev),
        out_specs=[q_spec, dkv_spec, b_spec] + [pl.BlockSpec((2, S, lanes), lambda h, i: (0, 0, h))] * len(prev),
        out_shape=[_sds((S, HD), BF16), _sds((2, S + KV_PAD, HD), F32), _sds((N_HEADS, Q_BLOCK, BAND), F32)]
        + [_sds((2, S, HD), BF16)] * len(prev),
        compiler_params=_cparams(2))(q, kvp, kvp, bias, do, *prev)


def _loss_head(x, g, target, name, tm=512):
    S, D = x.shape

    def body(x_ref, g_ref, t_ref, loss_ref, dx_ref, dxb_ref, dg_ref):
        @pl.when(pl.program_id(0) == 0)
        def _():
            loss_ref[...] = jnp.zeros_like(loss_ref)
            dg_ref[...] = jnp.zeros_like(dg_ref)

        xhat, r = _rms_hat(x_ref[...])
        gv = g_ref[...]
        err = xhat * gv - t_ref[...]
        loss_ref[...] += 0.5 * jnp.sum(jnp.mean(err * err, axis=-1, keepdims=True))
        dx, dgp = _rms_bwd(xhat, r, gv, err * (1.0 / D))
        dx_ref[...] = dx
        dxb_ref[...] = dx.astype(BF16)
        dg_ref[...] += jnp.sum(dgp, axis=0, keepdims=True)

    row = pl.BlockSpec((tm, D), lambda i: (i, 0))
    vec = pl.BlockSpec((1, D), lambda i: (0, 0))
    return pl.pallas_call(
        body, name=name, grid=(S // tm,), in_specs=[row, vec, row],
        out_specs=[pl.BlockSpec((8, 128), lambda i: (0, 0)), row, row, vec],
        out_shape=[_sds((8, 128), F32), _sds((S, D), F32), _sds((S, D), BF16), _sds((1, D), F32)],
        compiler_params=_cparams(1))(x, g, target)


def _place():
    x, y, c = lax.axis_index("x"), lax.axis_index("y"), lax.axis_index("c")
    chips = [(1 - x, y), (x, 1 - y), (1 - x, 1 - y)]
    return x, y, c, chips


def _half_rows(c, r):
    return pl.ds(pl.multiple_of(c * (r // 2), 8), r // 2)


HBM_SPEC = pl.BlockSpec(memory_space=pl.ANY)


STRICT_HBM_SPEC = pl.BlockSpec(memory_space=pltpu.HBM)
SEM_SPEC = pl.BlockSpec(memory_space=pltpu.SEMAPHORE)
EFFECT = pltpu.SideEffectType.DATAFLOW_SIDE_EFFECTING


def _peers(x, y, c):
    out = []
    for k in range(1, N_DEV):
        px, py, pc = (x + ((k >> 2) & 1)) % 2, (y + ((k >> 1) & 1)) % 2, (c + (k & 1)) % 2
        out.append(((px, py, pc), 2 * px + py, pc, 4 * px + 2 * py + pc))
    return out


def _token_spec():
    return pl.BlockSpec(memory_space=pltpu.VMEM)


def _hbm(a):
    return pltpu.with_memory_space_constraint(a, pltpu.HBM)


def _slab_half(ref, chip, core):
    return ref.at[2 * chip[0] + chip[1], _half_rows(core, ref.shape[1]), :]


def _allgather_start(slabs, name):
    n = len(slabs)

    def body(*refs):
        src, send, recv, token = refs[:n], refs[n], refs[n + 1], refs[-1]
        x, y, c, chips = _place()
        for a in range(n):
            own = _slab_half(src[a], (x, y), c)
            for j, chip in enumerate(chips):
                pltpu.make_async_remote_copy(src_ref=own, dst_ref=own, send_sem=send.at[3 * a + j], recv_sem=recv.at[3 * a + j],
                                             device_id=(*chip, c), device_id_type=MESH).start()
        token[...] = jnp.zeros_like(token)

    sems = pltpu.SemaphoreType.DMA((3 * n,))
    send, recv, *flying, token = pl.pallas_call(
        body, name=name, in_specs=[STRICT_HBM_SPEC] * n,
        out_shape=(sems, sems, *[pltpu.HBM(s.shape, s.dtype) for s in slabs], _sds((8, 128), F32)),
        out_specs=(SEM_SPEC, SEM_SPEC, *[STRICT_HBM_SPEC] * n, _token_spec()),
        input_output_aliases={a: a + 2 for a in range(n)},
        compiler_params=pltpu.CompilerParams(has_side_effects=EFFECT))(*[_hbm(s) for s in slabs])
    return send, recv, flying, token


def _allgather_relay(flying, send, recv, first, after, name):
    n = len(flying)

    def body(*refs):
        src, send_ref, recv_ref = refs[:n], refs[n], refs[n + 1]
        send2, recv2, token = refs[n + 3], refs[n + 4], refs[-1]
        token[...] = jnp.zeros_like(token)
        x, y, c, chips = _place()
        for a in range(n):
            for j, chip in enumerate(chips):
                cp = pltpu.make_async_remote_copy(
                    src_ref=_slab_half(src[a], (x, y), c), dst_ref=_slab_half(src[a], chip, c),
                    send_sem=send_ref.at[3 * (first + a) + j], recv_sem=recv_ref.at[3 * (first + a) + j],
                    device_id=(*chip, c), device_id_type=MESH)
                cp.wait_send()
                cp.wait_recv()
        for a in range(n):
            for j, chip in enumerate(chips):
                landed = _slab_half(src[a], chip, c)
                pltpu.make_async_remote_copy(src_ref=landed, dst_ref=landed, send_sem=send2.at[3 * a + j],
                                             recv_sem=recv2.at[3 * a + j], device_id=(x, y, 1 - c),
                                             device_id_type=MESH).start()

    sems = pltpu.SemaphoreType.DMA((3 * n,))
    send2, recv2, *relayed, token = pl.pallas_call(
        body, name=name, in_specs=[STRICT_HBM_SPEC] * n + [SEM_SPEC, SEM_SPEC, HBM_SPEC],
        out_shape=(sems, sems, *[pltpu.HBM(s.shape, s.dtype) for s in flying], _sds((8, 128), F32)),
        out_specs=(SEM_SPEC, SEM_SPEC, *[STRICT_HBM_SPEC] * n, _token_spec()),
        input_output_aliases={a: a + 2 for a in range(n)},
        compiler_params=pltpu.CompilerParams(has_side_effects=EFFECT))(*flying, send, recv, after)
    return send2, recv2, relayed, token


def _allgather_wait(relayed, send2, recv2, after, name):
    n = len(relayed)

    def body(*refs):
        src, send_ref, recv_ref = refs[:n], refs[n], refs[n + 1]
        x, y, c, chips = _place()
        for a in range(n):
            for j, chip in enumerate(chips):
                cp = pltpu.make_async_remote_copy(
                    src_ref=_slab_half(src[a], chip, c), dst_ref=_slab_half(src[a], chip, 1 - c),
                    send_sem=send_ref.at[3 * a + j], recv_sem=recv_ref.at[3 * a + j],
                    device_id=(x, y, 1 - c), device_id_type=MESH)
                cp.wait_send()
                cp.wait_recv()

    return pl.pallas_call(
        body, name=name, in_specs=[STRICT_HBM_SPEC] * n + [SEM_SPEC, SEM_SPEC, HBM_SPEC],
        out_shape=tuple(pltpu.HBM(s.shape, s.dtype) for s in relayed), out_specs=tuple([STRICT_HBM_SPEC] * n),
        input_output_aliases={a: a for a in range(n)},
        compiler_params=pltpu.CompilerParams(has_side_effects=EFFECT))(*relayed, send2, recv2, after)


def _allgather_small(small, name):
    def body(sm, osm, send, recv, local):
        x, y, c, chips = _place()
        own = pltpu.make_async_copy(sm, osm.at[2 * x + y], local)
        own.start()
        cps = [pltpu.make_async_remote_copy(src_ref=sm, dst_ref=osm.at[2 * x + y], send_sem=send.at[j],
                                            recv_sem=recv.at[j], device_id=(*chip, c), device_id_type=MESH)
               for j, chip in enumerate(chips)]
        for cp in cps:
            cp.start()
        for j, chip in enumerate(chips):
            got = osm.at[2 * chip[0] + chip[1]]
            pltpu.make_async_remote_copy(src_ref=got, dst_ref=got, send_sem=send.at[j], recv_sem=recv.at[j],
                                         device_id=(x, y, c), device_id_type=MESH).wait_recv()
        for cp in cps:
            cp.wait_send()
        own.wait()

    return pl.pallas_call(
        body, name=name, in_specs=[pl.BlockSpec(memory_space=pltpu.VMEM)], out_specs=HBM_SPEC,
        out_shape=_sds((N_CHIPS, *small.shape), small.dtype),
        scratch_shapes=[pltpu.SemaphoreType.DMA((3,)), pltpu.SemaphoreType.DMA((3,)), pltpu.SemaphoreType.DMA])(small)


def _reduce_start(grads, name):
    n = len(grads)

    def body(*refs):
        src, land, send, recv, token = refs[:n], refs[n:2 * n], refs[2 * n], refs[2 * n + 1], refs[-1]
        x, y, c, _ = _place()
        me = 4 * x + 2 * y + c
        for a in range(n):
            for k, (peer, p_chip, p_core, _) in enumerate(_peers(x, y, c)):
                pltpu.make_async_remote_copy(
                    src_ref=src[a].at[p_chip, _half_rows(p_core, src[a].shape[1]), :], dst_ref=land[a].at[me],
                    send_sem=send.at[(N_DEV - 1) * a + k], recv_sem=recv.at[(N_DEV - 1) * a + k],
                    device_id=peer, device_id_type=MESH).start()
        token[...] = jnp.zeros_like(token)

    lands = [lax.empty((N_DEV, g.shape[1] // 2, g.shape[2]), BF16) for g in grads]
    sems = pltpu.SemaphoreType.DMA(((N_DEV - 1) * n,))
    shapes = [pltpu.HBM(a.shape, a.dtype) for a in grads + lands]
    send, recv, *flying, token = pl.pallas_call(
        body, name=name, in_specs=[STRICT_HBM_SPEC] * (2 * n),
        out_shape=(sems, sems, *shapes, _sds((8, 128), F32)),
        out_specs=(SEM_SPEC, SEM_SPEC, *[STRICT_HBM_SPEC] * (2 * n), _token_spec()),
        input_output_aliases={a: a + 2 for a in range(2 * n)},
        compiler_params=pltpu.CompilerParams(has_side_effects=EFFECT))(*[_hbm(a) for a in grads + lands])
    return send, recv, flying[:n], flying[n:], token


def _reduce_wait(started, after, name):
    sizes = [len(grads) for _, _, grads, _ in started]
    n_arr = 2 * sum(sizes)

    def body(*refs):
        x, y, c, _ = _place()
        at = 0
        for s, n in enumerate(sizes):
            src, land = refs[at:at + n], refs[at + n:at + 2 * n]
            send_ref, recv_ref = refs[n_arr + 2 * s], refs[n_arr + 2 * s + 1]
            at += 2 * n
            for a in range(n):
                for k, (peer, p_chip, p_core, p_dev) in enumerate(_peers(x, y, c)):
                    cp = pltpu.make_async_remote_copy(
                        src_ref=src[a].at[p_chip, _half_rows(p_core, src[a].shape[1]), :], dst_ref=land[a].at[p_dev],
                        send_sem=send_ref.at[(N_DEV - 1) * a + k], recv_sem=recv_ref.at[(N_DEV - 1) * a + k],
                        device_id=peer, device_id_type=MESH)
                    cp.wait_send()
                    cp.wait_recv()

    arrays, sems = [], []
    for send, recv, grads, lands in started:
        arrays += list(grads) + list(lands)
        sems += [send, recv]
    out = pl.pallas_call(
        body, name=name, in_specs=[STRICT_HBM_SPEC] * n_arr + [SEM_SPEC] * len(sems) + [HBM_SPEC],
        out_shape=tuple(pltpu.HBM(a.shape, a.dtype) for a in arrays), out_specs=tuple([STRICT_HBM_SPEC] * n_arr),
        input_output_aliases={a: a for a in range(n_arr)},
        compiler_params=pltpu.CompilerParams(has_side_effects=EFFECT))(*arrays, *sems, after)
    result, at = [], 0
    for n in sizes:
        result.append((out[at:at + n], out[at + n:at + 2 * n]))
        at += 2 * n
    return result


def _reduce_sum(grad, land, place, name, deps=()):
    _, r2, C = land.shape
    tr = _row_tile(r2, C, 4, 1024 * 1024, 16)
    nb = r2 // tr

    def body(place_ref, own_ref, *rest):
        del place_ref
        acc = own_ref[...].astype(F32)
        for ref in rest[:N_DEV - 1]:
            acc = acc + ref[...].astype(F32)
        rest[-1][...] = acc

    def from_dev(k):
        return pl.BlockSpec((None, tr, C), lambda i, place_ref: ((place_ref[2] + k) % N_DEV, i, 0))

    grid_spec = pltpu.PrefetchScalarGridSpec(
        num_scalar_prefetch=1, grid=(nb,),
        in_specs=[pl.BlockSpec((None, tr, C), lambda i, place_ref: (place_ref[0], place_ref[1] * nb + i, 0))]
        + [from_dev(k) for k in range(1, N_DEV)] + [HBM_SPEC] * len(deps),
        out_specs=pl.BlockSpec((tr, C), lambda i, place_ref: (place_ref[1] * nb + i, 0)))
    return pl.pallas_call(body, name=name, grid_spec=grid_spec, out_shape=_sds((2 * r2, C), F32),
                          compiler_params=_cparams(1))(place, grad, *[land] * (N_DEV - 1), *deps)


def _join_start(halves, name):
    n = len(halves)

    def body(*refs):
        src, send, recv, token = refs[:n], refs[n], refs[n + 1], refs[-1]
        x, y, c, _ = _place()
        for w in range(n):
            mine = src[w].at[_half_rows(c, src[w].shape[0]), :]
            pltpu.make_async_remote_copy(src_ref=mine, dst_ref=mine, send_sem=send.at[w], recv_sem=recv.at[w],
                                         device_id=(x, y, 1 - c), device_id_type=MESH).start()
        token[...] = jnp.zeros_like(token)

    sems = pltpu.SemaphoreType.DMA((n,))
    send, recv, *flying, token = pl.pallas_call(
        body, name=name, in_specs=[STRICT_HBM_SPEC] * n,
        out_shape=(sems, sems, *[pltpu.HBM(a.shape, a.dtype) for a in halves], _sds((8, 128), F32)),
        out_specs=(SEM_SPEC, SEM_SPEC, *[STRICT_HBM_SPEC] * n, _token_spec()),
        input_output_aliases={w: w + 2 for w in range(n)},
        compiler_params=pltpu.CompilerParams(has_side_effects=EFFECT))(*[_hbm(a) for a in halves])
    return send, recv, flying, token


def _join_wait(flying, send, recv, after, name):
    n = len(flying)

    def body(*refs):
        src, send_ref, recv_ref = refs[:n], refs[n], refs[n + 1]
        x, y, c, _ = _place()
        for w in range(n):
            cp = pltpu.make_async_remote_copy(
                src_ref=src[w].at[_half_rows(c, src[w].shape[0]), :],
                dst_ref=src[w].at[_half_rows(1 - c, src[w].shape[0]), :], send_sem=send_ref.at[w],
                recv_sem=recv_ref.at[w], device_id=(x, y, 1 - c),
                device_id_type=MESH)
            cp.wait_send()
            cp.wait_recv()

    return pl.pallas_call(
        body, name=name, in_specs=[STRICT_HBM_SPEC] * n + [SEM_SPEC, SEM_SPEC, HBM_SPEC],
        out_shape=tuple(pltpu.HBM(a.shape, a.dtype) for a in flying), out_specs=tuple([STRICT_HBM_SPEC] * n),
        input_output_aliases={w: w for w in range(n)},
        compiler_params=pltpu.CompilerParams(has_side_effects=EFFECT))(*flying, send, recv, after)


def _gather_small(packed, name):
    def body(p_ref, out, send, recv, local):
        x, y, c, _ = _place()
        me = 4 * x + 2 * y + c
        own = pltpu.make_async_copy(p_ref, out.at[me], local)
        own.start()
        cps = []
        for k in range(1, N_DEV):
            fx, fy, fc = (k >> 2) & 1, (k >> 1) & 1, k & 1
            peer = ((x + fx) % 2, (y + fy) % 2, (c + fc) % 2)
            cps.append(pltpu.make_async_remote_copy(src_ref=p_ref, dst_ref=out.at[me], send_sem=send.at[k - 1],
                                                    recv_sem=recv.at[k - 1], device_id=peer, device_id_type=MESH))
        for cp in cps:
            cp.start()
        for k in range(1, N_DEV):
            fx, fy, fc = (k >> 2) & 1, (k >> 1) & 1, k & 1
            src = out.at[4 * ((x + fx) % 2) + 2 * ((y + fy) % 2) + (c + fc) % 2]
            pltpu.make_async_remote_copy(src_ref=src, dst_ref=src, send_sem=send.at[k - 1], recv_sem=recv.at[k - 1],
                                         device_id=(x, y, c), device_id_type=MESH).wait_recv()
        for cp in cps:
            cp.wait_send()
        own.wait()

    return pl.pallas_call(
        body, name=name, in_specs=[pl.BlockSpec(memory_space=pltpu.VMEM)], out_specs=HBM_SPEC,
        out_shape=_sds((N_DEV, *packed.shape), F32),
        scratch_shapes=[pltpu.SemaphoreType.DMA((N_DEV - 1,)), pltpu.SemaphoreType.DMA((N_DEV - 1,)),
                        pltpu.SemaphoreType.DMA])(packed)


def _sum_devices(gathered, name):
    _, R, C = gathered.shape

    def body(g_ref, o_ref):
        acc = g_ref[0]
        for d in range(1, N_DEV):
            acc = acc + g_ref[d]
        o_ref[...] = acc

    tr = 8
    return pl.pallas_call(
        body, name=name, grid=(R // tr,), in_specs=[pl.BlockSpec((N_DEV, tr, C), lambda i: (0, i, 0))],
        out_specs=pl.BlockSpec((tr, C), lambda i: (i, 0)), out_shape=_sds((R, C), F32),
        compiler_params=_cparams(1))(gathered)


def _gather_start(packed, name):
    def body(src, land, send, recv, *rest):
        x, y, c, _ = _place()
        for k, (peer, _, _, _) in enumerate(_peers(x, y, c)):
            pltpu.make_async_remote_copy(src_ref=src, dst_ref=land.at[4 * x + 2 * y + c], send_sem=send.at[k],
                                         recv_sem=recv.at[k], device_id=peer, device_id_type=MESH).start()
        rest[-1][...] = jnp.zeros_like(rest[-1])

    land = lax.empty((N_DEV, *packed.shape), F32)
    sems = pltpu.SemaphoreType.DMA((N_DEV - 1,))
    return pl.pallas_call(
        body, name=name, in_specs=[STRICT_HBM_SPEC] * 2,
        out_shape=(sems, sems, pltpu.HBM(packed.shape, F32), pltpu.HBM(land.shape, F32), _sds((8, 128), F32)),
        out_specs=(SEM_SPEC, SEM_SPEC, STRICT_HBM_SPEC, STRICT_HBM_SPEC, _token_spec()),
        input_output_aliases={0: 2, 1: 3},
        compiler_params=pltpu.CompilerParams(has_side_effects=EFFECT))(_hbm(packed), _hbm(land))


def _gather_wait(started, after, name):
    n = len(started)

    def body(*refs):
        x, y, c, _ = _place()
        for s in range(n):
            src, land, send, recv = refs[2 * s], refs[2 * s + 1], refs[2 * n + 2 * s], refs[2 * n + 2 * s + 1]
            for k, (peer, _, _, p_dev) in enumerate(_peers(x, y, c)):
                cp = pltpu.make_async_remote_copy(src_ref=src, dst_ref=land.at[p_dev], send_sem=send.at[k],
                                                  recv_sem=recv.at[k], device_id=peer,
                                                  device_id_type=MESH)
                cp.wait_send()
                cp.wait_recv()

    arrays = [a for _, _, packed, land in started for a in (packed, land)]
    sems = [s for send, recv, _, _ in started for s in (send, recv)]
    out = pl.pallas_call(
        body, name=name, in_specs=[STRICT_HBM_SPEC] * (2 * n) + [SEM_SPEC] * (2 * n) + [HBM_SPEC],
        out_shape=tuple(pltpu.HBM(a.shape, a.dtype) for a in arrays), out_specs=tuple([STRICT_HBM_SPEC] * (2 * n)),
        input_output_aliases={a: a for a in range(2 * n)},
        compiler_params=pltpu.CompilerParams(has_side_effects=EFFECT))(*arrays, *sems, after)
    return [(out[2 * s], out[2 * s + 1]) for s in range(n)]


def _sum_gathered(packed, land, device, name):
    R, C = packed.shape
    tr = 8

    def body(dev_ref, own_ref, *rest):
        me = dev_ref[0]
        acc = None
        for d in range(N_DEV):
            term = jnp.where(me == d, own_ref[...], rest[d][...])
            acc = term if acc is None else acc + term
        rest[-1][...] = acc

    def slab(d):
        return pl.BlockSpec((None, tr, C), lambda i, dev_ref: (jnp.where(dev_ref[0] == d, (d + 1) % N_DEV, d), i, 0))

    grid_spec = pltpu.PrefetchScalarGridSpec(
        num_scalar_prefetch=1, grid=(R // tr,),
        in_specs=[pl.BlockSpec((tr, C), lambda i, dev_ref: (i, 0))] + [slab(d) for d in range(N_DEV)],
        out_specs=pl.BlockSpec((tr, C), lambda i, dev_ref: (i, 0)))
    return pl.pallas_call(body, name=name, grid_spec=grid_spec, out_shape=_sds((R, C), F32),
                          compiler_params=_cparams(1))(device, packed, *[land] * N_DEV)


def _pack_small(arrays):
    rows = []
    for a in arrays:
        flat = a.reshape(-1)
        pad = (-flat.shape[0]) % SMALL_COLS
        rows.append(jnp.pad(flat, (0, pad)).reshape(-1, SMALL_COLS))
    packed = jnp.concatenate(rows, axis=0)
    return jnp.pad(packed, ((0, (-packed.shape[0]) % 8), (0, 0)))


def _unpack_small(packed, shapes):
    out, row = [], 0
    for shape in shapes:
        size = math.prod(shape)
        n_rows = -(-size // SMALL_COLS)
        out.append(packed[row:row + n_rows].reshape(-1)[:size].reshape(shape))
        row += n_rows
    return out


def kernel(x, a_norm, a_w_in, a_sgu_norm, a_w_spatial, a_b_spatial, a_w_out, kv_norm, w_kv, b_norm, b_w_q, b_rel_bias, b_w_o, ffn_norm, ffn_w_gate_up, ffn_w_down, final_norm, loss_target, m_a_norm, m_a_w_in, m_a_sgu_norm, m_a_w_spatial, m_a_b_spatial, m_a_w_out, m_kv_norm, m_w_kv, m_b_norm, m_b_w_q, m_b_rel_bias, m_b_w_o, m_ffn_norm, m_ffn_w_gate_up, m_ffn_w_down, m_final_norm, v_a_norm, v_a_w_in, v_a_sgu_norm, v_a_w_spatial, v_a_b_spatial, v_a_w_out, v_kv_norm, v_w_kv, v_b_norm, v_b_w_q, v_b_rel_bias, v_b_w_o, v_ffn_norm, v_ffn_w_gate_up, v_ffn_w_down, v_final_norm):
    S, D = x.shape[1], x.shape[2]
    n_a = a_w_in.shape[0]
    n_b = b_w_q.shape[0]
    depth = ffn_w_gate_up.shape[0]
    xi, yi, ci = lax.axis_index("x"), lax.axis_index("y"), lax.axis_index("c")
    chip = 2 * xi + yi

    place = jnp.stack([chip, ci, 2 * chip + ci]).astype(jnp.int32)
    stacked = {"a_w_in": a_w_in, "a_w_out": a_w_out, "w_kv": w_kv[None], "b_w_q": b_w_q, "b_w_o": b_w_o,
               "ffn_w_gate_up": ffn_w_gate_up, "ffn_w_down": ffn_w_down}
    groups = []
    for layer in range(depth):
        if layer == 0 and n_a > 0:
            groups += [[("a_w_in", 0)], [("a_w_out", 0)]]
        elif layer < n_a:
            groups.append([("a_w_in", layer), ("a_w_out", layer)])
        elif layer == n_a:
            groups.append([("w_kv", 0), ("b_w_q", 0), ("b_w_o", 0)])
        else:
            groups.append([("b_w_q", layer - n_a), ("b_w_o", layer - n_a)])
        groups.append([("ffn_w_gate_up", layer), ("ffn_w_down", layer)])
    units = [u for group in groups for u in group]
    n_early = len(groups[0])
    slabs = [_cast_slab(stacked[k], l, place[:1], f"cast_{k}_{l}") for k, l in units[:n_early]]
    early = _allgather_start(slabs, "allgather_start_first")
    slabs = [_cast_slab(stacked[k], l, place[:1], f"cast_{k}_{l}", deps=(early[3],)) for k, l in units[n_early:]]
    late = _allgather_start(slabs, "allgather_start_rest")
    na_w, ns_w = a_norm.shape[1], a_sgu_norm.shape[1]
    small_g = _allgather_small(jnp.concatenate([a_norm, a_sgu_norm], axis=1), "allgather_small")
    a_norm_f = small_g[:, :, :na_w].transpose(1, 0, 2).reshape(n_a, N_CHIPS * na_w)
    a_sgu_f = small_g[:, :, na_w:].transpose(1, 0, 2).reshape(n_a, N_CHIPS * ns_w)
    W, relayed = {}, []

    def relay(after):
        if len(relayed) == len(groups):
            return ()
        index = sum(len(g) for g in groups[:len(relayed)])
        group = groups[len(relayed)]
        (send, recv, flying, _), first = (early, index) if index < n_early else (late, index - n_early)
        relayed.append(_allgather_relay(flying[first:first + len(group)], send, recv, first, after,
                                        f"allgather_relay_{len(relayed)}"))
        return (relayed[-1][3],)

    n_gathered = [0]

    def gathered(after):
        index = n_gathered[0]
        send2, recv2, arrays, _ = relayed[index]
        W.update(zip(groups[index], _allgather_wait(arrays, send2, recv2, after, f"allgather_wait_{index}")))
        n_gathered[0] += 1

    xc = x.reshape(S, D)
    saved = []
    kvp = x_kv = h_kv = None
    relay(late[3])
    order = ()
    for layer in range(depth):
        rec = {"x_in": xc}
        gathered(xc)
        if layer < n_a:
            i = layer
            rec["zpre"], rec["h"] = _norm_matmul(xc, a_norm_f[i][None], W["a_w_in", i], BF16, f"a{i}_in", deps=order)
            order = relay(rec["h"])
            rec["uv"] = _sgu_fwd(rec["zpre"], a_sgu_f[i][None], a_w_spatial[i], a_b_spatial[i].T, f"a{i}_sgu",
                                 deps=order)
            order = ()
            if ("a_w_out", i) not in W:
                gathered(rec["uv"])
                order = relay(rec["uv"])
            xm = _matmul_res(rec["uv"], W["a_w_out", i], xc, f"a{i}_out", tm=512, deps=order)
        else:
            i = layer - n_a
            if i == 0:
                kvp, h_kv = _norm_matmul(xc, kv_norm[None], W["w_kv", 0], BF16, "kv_proj", tm=KV_PAD, zero_rows=KV_PAD)
                x_kv = xc
            rec["q"], rec["h"] = _norm_matmul(xc, b_norm[i][None], W["b_w_q", i], BF16, f"b{i}_q", row_sharded=True)
            order = relay(rec["h"])
            table = jnp.pad(b_rel_bias[i], ((0, 0), (0, TABLE_PAD - b_rel_bias.shape[2])))
            rec["bias"] = _rel_bias_fwd(table, f"b{i}_bias")
            rec["o"] = _attn_fwd(rec["q"], kvp, rec["bias"], f"b{i}_attn", deps=order)
            xm = _matmul_res(rec["o"], W["b_w_o", i], xc, f"b{i}_o", tm=512)
        rec["x_mid"] = xm
        gathered(xm)
        rec["gu"], rec["h_f"] = _norm_matmul(xm, ffn_norm[layer][None], W["ffn_w_gate_up", layer], BF16, f"f{layer}_in")
        order = relay(rec["h_f"])
        xc = _matmul_res(rec["gu"], W["ffn_w_down", layer], xm, f"f{layer}_out", swiglu=True, deps=order, tm=512)
        order = ()
        saved.append(rec)

    loss_tile, dx, dxb, d_final = _loss_head(xc, final_norm[None], loss_target.reshape(S, D), "loss_head")

    started = []
    small_started = []
    pending = []

    held = []

    def weight_grad(unit, hold=False, **kw):
        full = (N_CHIPS,) + tuple(stacked[unit[0]].shape[1:])
        g = _matmul_tn(out_shape=full, name=f"d_{unit[0]}_{unit[1]}", deps=tuple(pending), **kw)
        pending.clear()
        held.append((unit, g))
        if hold:
            return ()
        send, recv, flying_g, flying_land, token = _reduce_start([g for _, g in held],
                                                                 f"reduce_start_{unit[0]}_{unit[1]}")
        started.append(([u for u, _ in held], send, recv, flying_g, flying_land))
        held.clear()
        return (token,)

    tt = min(1024, S)
    tw = min(2048, S)
    tb = min(NORMBWD_ROWS, S)
    row_a = lambda w, rows=tw: pl.BlockSpec((rows, w), lambda o, t: (t, 0))
    d_ffn_norm, d_b_norm, d_a_norm, d_a_sgu = [None] * depth, [None] * n_b, [None] * n_a, [None] * n_a
    d_ws, d_bs, d_rel = [None] * n_a, [None] * n_a, [None] * n_b
    dkv = None
    first = lambda ref: ref[...]
    for layer in reversed(range(depth)):
        rec = saved[layer]
        r_d = ffn_w_down.shape[1]
        half_f = 2 * r_d
        token = weight_grad(
            ("ffn_w_down", layer), a_ops=[rec["gu"], rec["gu"]],
            a_specs=[pl.BlockSpec((tt, half_f), lambda o, t: (t, o)), pl.BlockSpec((tt, half_f), lambda o, t: (t, 2 + o))],
            a_fn=lambda g_ref, u_ref: _swiglu(g_ref[...].astype(F32), u_ref[...].astype(F32)).astype(BF16),
            b_op=dxb, b_spec=row_a(D, tt), out_spec=pl.BlockSpec((2, r_d, D), lambda o, t: (o, 0, 0)),
            acc_shape=(half_f, D), n_outer=2, tt=tt, hold=True)
        dgu = _nt_swiglu_bwd(dxb, W["ffn_w_down", layer], rec["gu"], f"f{layer}_dgu", deps=token)
        nsh = ffn_w_gate_up.shape[2]
        token = weight_grad(
            ("ffn_w_gate_up", layer), a_ops=[rec["h_f"]], a_specs=[row_a(D)], a_fn=first,
            b_op=dgu, b_spec=pl.BlockSpec((None, tw, nsh), lambda o, t: (o // 2, t, o % 2)),
            out_spec=pl.BlockSpec((None, D, nsh), lambda o, t: (o, 0, 0)), acc_shape=(D, nsh), n_outer=N_CHIPS, tt=tw)
        dx, dxb, d_ffn_norm[layer] = _nt_normbwd(
            dgu, pl.BlockSpec((None, tb, nsh), lambda i, k: (k // 2, i, k % 2)),
            W["ffn_w_gate_up", layer], pl.BlockSpec((None, D, nsh), lambda i, k: (k, 0, 0)),
            (D, nsh), N_CHIPS, rec["x_mid"], ffn_norm[layer][None], dx, f"f{layer}_dx", deps=token)
        if layer >= n_a:
            i = layer - n_a
            r_o = b_w_o.shape[1]
            token = weight_grad(
                ("b_w_o", i), a_ops=[rec["o"]], a_specs=[row_a(D)], a_fn=first, b_op=dxb, b_spec=row_a(D),
                out_spec=pl.BlockSpec((N_CHIPS, r_o, D), lambda o, t: (0, 0, 0)), acc_shape=(D, D), n_outer=1, tt=tw,
                hold=True)
            do = _nt_rows(dxb, W["b_w_o", i], N_CHIPS, BF16, f"b{i}_do", deps=token)
            dq, dkv, dbias, *dkv_bf16 = _attn_bwd(rec["q"], kvp, rec["bias"], do, dkv, f"b{i}_attn_bwd")
            d_rel[i] = _rel_bias_bwd(dbias, f"b{i}_dbias")[:, :b_rel_bias.shape[2]]
            token = weight_grad(
                ("b_w_q", i), a_ops=[rec["h"]], a_specs=[row_a(D)], a_fn=first, b_op=dq, b_spec=row_a(D),
                out_spec=pl.BlockSpec((N_CHIPS, r_o, D), lambda o, t: (0, 0, 0)), acc_shape=(D, D), n_outer=1, tt=tw)
            dx, dxb, d_b_norm[i] = _nt_normbwd(
                dq, pl.BlockSpec((tb, D), lambda i_, k: (i_, 0)),
                W["b_w_q", i], pl.BlockSpec((N_CHIPS, r_o, D), lambda i_, k: (0, 0, 0)),
                (D, D), 1, rec["x_in"], b_norm[i][None], dx, f"b{i}_dx", deps=token)
            if i == 0:
                dkv_b = dkv_bf16[0] if dkv_bf16 else dkv[:, KV_PAD:, :].astype(BF16)
                n_kv = w_kv.shape[1]
                token = weight_grad(
                    ("w_kv", 0), a_ops=[h_kv], a_specs=[row_a(D)], a_fn=first,
                    b_op=dkv_b, b_spec=pl.BlockSpec((None, tw, n_kv), lambda o, t: (o // 2, t, o % 2)),
                    out_spec=pl.BlockSpec((None, D, n_kv), lambda o, t: (o, 0, 0)), acc_shape=(D, n_kv),
                    n_outer=N_CHIPS, tt=tw)
                dx, dxb, d_kv_norm = _nt_normbwd(
                    dkv_b, pl.BlockSpec((None, tb, n_kv), lambda i_, k: (k // 2, i_, k % 2)),
                    W["w_kv", 0], pl.BlockSpec((None, D, n_kv), lambda i_, k: (k, 0, 0)),
                    (D, n_kv), N_CHIPS, x_kv, kv_norm[None], dx, "kv_dx", deps=token)
        else:
            i = layer
            r_w = a_w_out.shape[1]
            token = weight_grad(
                ("a_w_out", i), a_ops=[rec["uv"]], a_specs=[row_a(N_CHIPS * r_w, tw // 2)], a_fn=first,
                b_op=dxb, b_spec=row_a(D, tw // 2), out_spec=pl.BlockSpec((N_CHIPS, r_w, D), lambda o, t: (0, 0, 0)),
                acc_shape=(N_CHIPS * r_w, D), n_outer=1, tt=tw // 2, hold=i > 0)
            duv = _nt_rows(dxb, W["a_w_out", i], 2, BF16, f"a{i}_duv", deps=token)
            dz, d_a_sgu[i], d_ws[i], dbs = _sgu_bwd(rec["zpre"], duv, a_sgu_f[i][None], a_w_spatial[i],
                                                  a_w_spatial[i].transpose(0, 2, 1), a_b_spatial[i].T, f"a{i}_sgu_bwd")
            d_bs[i] = dbs[:, :, 0]
            if i == 0:
                batch = [d_a_sgu[0], d_ws[0][None], d_bs[0][None], d_ffn_norm[0]]
                small_started.append((batch, _gather_start(_pack_small(batch), "gather_start_late")))
                pending.append(small_started[-1][1][4])
            n_in = a_w_in.shape[2]
            token = weight_grad(
                ("a_w_in", i), a_ops=[rec["h"]], a_specs=[row_a(D)], a_fn=first,
                b_op=dz, b_spec=pl.BlockSpec((tw, n_in), lambda o, t: (t, o)),
                out_spec=pl.BlockSpec((None, D, n_in), lambda o, t: (o, 0, 0)), acc_shape=(D, n_in), n_outer=N_CHIPS,
                tt=tw)
            dx, dxb, d_a_norm[i] = _nt_normbwd(
                dz, pl.BlockSpec((tb, n_in), lambda i_, k: (i_, k)),
                W["a_w_in", i], pl.BlockSpec((None, D, n_in), lambda i_, k: (k, 0, 0)),
                (D, n_in), N_CHIPS, rec["x_in"], a_norm_f[i][None], dx, f"a{i}_dx", deps=token)
        if layer == 1:
            batch = [jnp.concatenate(d_a_norm[1:], axis=0), jnp.concatenate(d_a_sgu[1:], axis=0), jnp.stack(d_ws[1:]),
                     jnp.stack(d_bs[1:]), d_kv_norm, jnp.concatenate(d_b_norm, axis=0), jnp.stack(d_rel),
                     jnp.concatenate(d_ffn_norm[1:], axis=0), d_final, loss_tile[:1, :1]]
            small_started.append((batch, _gather_start(_pack_small(batch), "gather_start_early")))
            pending.append(small_started[-1][1][4])
    grad_x = dx.reshape(x.shape)

    landed = _reduce_wait([(send, recv, g, land) for _, send, recv, g, land in started], dx, "reduce_wait")
    reduced_units = [unit for units_, *_ in started for unit in units_]
    arrived = dict(zip(reduced_units, [pair for gs, lands in landed for pair in zip(gs, lands)]))
    ffn_units = [u for u in reduced_units if u[0].startswith("ffn_")]
    other_units = [u for u in reduced_units if not u[0].startswith("ffn_")]
    halves = [_reduce_sum(*arrived[u], place, f"reduce_sum_{u[0]}_{u[1]}") for u in ffn_units]
    ffn_send, ffn_recv, ffn_flying, ffn_token = _join_start(halves, "join_start_ffn")
    halves = [_reduce_sum(*arrived[u], place, f"reduce_sum_{u[0]}_{u[1]}", deps=(ffn_token,)) for u in other_units]
    other_send, other_recv, other_flying, other_token = _join_start(halves, "join_start_rest")
    joined = dict(zip(ffn_units, _join_wait(ffn_flying, ffn_send, ffn_recv, other_token, "join_wait_ffn")))
    reduced = {}

    (packed_e, land_e), (packed_l, land_l) = _gather_wait([s[:4] for _, s in small_started], dx, "gather_wait")
    total_e = _sum_gathered(packed_e, land_e, place[2:], "sum_small_grads_early")
    total_l = _sum_gathered(packed_l, land_l, place[2:], "sum_small_grads_late")
    total_t = _sum_devices(_gather_small(_pack_small([d_a_norm[0]]), "gather_small_grads_last"), "sum_small_grads_last")
    (e_a_norm, e_a_sgu, e_ws, e_bs, g_kv_norm, g_b_norm, g_rel, e_ffn_norm, g_final, loss) = _unpack_small(
        total_e, [a.shape for a in small_started[0][0]])
    l_a_sgu, l_ws, l_bs, l_ffn_norm = _unpack_small(total_l, [a.shape for a in small_started[1][0]])
    (t_a_norm,) = _unpack_small(total_t, [d_a_norm[0].shape])
    g_a_norm = jnp.concatenate([t_a_norm, e_a_norm], axis=0)
    g_a_sgu = jnp.concatenate([l_a_sgu, e_a_sgu], axis=0)
    g_ws = jnp.concatenate([l_ws, e_ws], axis=0)
    g_bs = jnp.concatenate([l_bs, e_bs], axis=0)
    g_ffn_norm = jnp.concatenate([l_ffn_norm, e_ffn_norm], axis=0)
    reduced["a_norm"] = lax.dynamic_slice_in_dim(g_a_norm, chip * na_w, na_w, axis=1)
    reduced["a_sgu_norm"] = lax.dynamic_slice_in_dim(g_a_sgu, chip * ns_w, ns_w, axis=1)
    reduced.update(a_w_spatial=g_ws, a_b_spatial=g_bs, kv_norm=g_kv_norm.reshape(kv_norm.shape), b_norm=g_b_norm,
                   b_rel_bias=g_rel, ffn_norm=g_ffn_norm, final_norm=g_final.reshape(final_norm.shape))

    weights = dict(a_norm=a_norm, a_w_in=a_w_in, a_sgu_norm=a_sgu_norm, a_w_spatial=a_w_spatial,
                   a_b_spatial=a_b_spatial, a_w_out=a_w_out, kv_norm=kv_norm, w_kv=w_kv, b_norm=b_norm, b_w_q=b_w_q,
                   b_rel_bias=b_rel_bias, b_w_o=b_w_o, ffn_norm=ffn_norm, ffn_w_gate_up=ffn_w_gate_up,
                   ffn_w_down=ffn_w_down, final_norm=final_norm)
    m_in = dict(a_norm=m_a_norm, a_w_in=m_a_w_in, a_sgu_norm=m_a_sgu_norm, a_w_spatial=m_a_w_spatial,
                a_b_spatial=m_a_b_spatial, a_w_out=m_a_w_out, kv_norm=m_kv_norm, w_kv=m_w_kv, b_norm=m_b_norm,
                b_w_q=m_b_w_q, b_rel_bias=m_b_rel_bias, b_w_o=m_b_w_o, ffn_norm=m_ffn_norm,
                ffn_w_gate_up=m_ffn_w_gate_up, ffn_w_down=m_ffn_w_down, final_norm=m_final_norm)
    v_in = dict(a_norm=v_a_norm, a_w_in=v_a_w_in, a_sgu_norm=v_a_sgu_norm, a_w_spatial=v_a_w_spatial,
                a_b_spatial=v_a_b_spatial, a_w_out=v_a_w_out, kv_norm=v_kv_norm, w_kv=v_w_kv, b_norm=v_b_norm,
                b_w_q=v_b_w_q, b_rel_bias=v_b_rel_bias, b_w_o=v_b_w_o, ffn_norm=v_ffn_norm,
                ffn_w_gate_up=v_ffn_w_gate_up, ffn_w_down=v_ffn_w_down, final_norm=v_final_norm)
    results = {}

    def adamw_large(key):
        as_layers = lambda a: a.reshape(stacked[key].shape)
        results[key] = _adamw_stacked(
            as_layers(weights[key]), [joined[key, layer] for layer in range(stacked[key].shape[0])],
            as_layers(m_in[key]), as_layers(v_in[key]), "adamw_" + key)

    for key in ("ffn_w_gate_up", "ffn_w_down"):
        adamw_large(key)
    joined.update(zip(other_units, _join_wait(other_flying, other_send, other_recv, results["ffn_w_down"][1],
                                              "join_wait_rest")))
    for key, w in weights.items():
        if key in stacked:
            if key not in results:
                adamw_large(key)
        else:
            g = reduced[key].reshape(w.shape)
            view = (1, w.shape[0]) if w.ndim == 1 else (-1, w.shape[-1])
            d, nm, nv = _adamw(w.reshape(view), g.reshape(view), m_in[key].reshape(view), v_in[key].reshape(view),
                               "adamw_" + key)
            results[key] = (g, d, nm, nv)
    outs = [[results[key][k].reshape(w.shape) for key, w in weights.items()] for k in range(4)]
    return (loss.reshape(()), grad_x, *outs[0], *outs[1], *outs[2], *outs[3])
```

```python
import math

import jax
import jax.numpy as jnp
from jax import lax
from jax.experimental import pallas as pl
from jax.experimental.pallas import tpu as pltpu

F32, BF16 = jnp.float32, jnp.bfloat16
MESH = pl.DeviceIdType.MESH
HIGHEST = lax.Precision.HIGHEST
NT_DIMS = (((1,), (1,)), ((), ()))
TN_DIMS = (((0,), (0,)), ((), ()))

EPS = 1e-6
CHUNK = 64
A_CHUNK = 128
A_GROUPS = 8
N_HEADS = 16
HEAD_DIM = 64
N_LEFT = 8
MAX_REL = 256
ATTN_SCALE = HEAD_DIM ** -0.5
NEG_INF = -1e30
Q_BLOCK = 4 * CHUNK
KV_PAD = N_LEFT * CHUNK
BAND = KV_PAD + Q_BLOCK
DIAGS = BAND + Q_BLOCK
TABLE_PAD = 640
HEADS_PER_BLOCK = 2
BLOCKS_PER_STEP = 8
NORMBWD_ROWS = 1024

ADAM_LR, ADAM_B1, ADAM_B2, ADAM_EPS, ADAM_WD, ADAM_STEP = 0.001, 0.9, 0.999, 1e-08, 0.01, 10

VMEM_LIMIT_BYTES = 56 * 1024 * 1024
N_CHIPS = 4
N_DEV = 8
SMALL_COLS = 1024


def _cparams(n_grid):
    return pltpu.CompilerParams(dimension_semantics=("arbitrary",) * n_grid, vmem_limit_bytes=VMEM_LIMIT_BYTES)


def _sds(shape, dtype):
    return jax.ShapeDtypeStruct(tuple(shape), dtype)


def _gelu(x):
    return x * (0.5 * (1.0 + lax.erf(x * math.sqrt(0.5))))


def _gelu_and_grad(x):
    cdf = 0.5 * (1.0 + lax.erf(x * math.sqrt(0.5)))
    return x * cdf, cdf + x * (jnp.exp(-0.5 * x * x) * (1.0 / math.sqrt(2.0 * math.pi)))


def _rms_hat(xv):
    r = lax.rsqrt(jnp.mean(xv * xv, axis=-1, keepdims=True) + EPS)
    return xv * r, r


def _rms_bwd(xhat, r, g, dy):
    dxhat = dy * g
    dx = r * (dxhat - xhat * jnp.mean(dxhat * xhat, axis=-1, keepdims=True))
    return dx, dy * xhat


def _swiglu(gate, up):
    return (gate * jax.nn.sigmoid(gate)) * up


def _row_tile(rows, cols, itemsize, cap_bytes, align):
    t = rows
    while t * cols * itemsize > cap_bytes and t % (2 * align) == 0:
        t //= 2
    return t


def _cast_slab(w, layer, chip, name, deps=()):
    _, r, C = w.shape
    tr = _row_tile(r, C, 4, 4 * 1024 * 1024, 16)

    def body(chip_ref, w_ref, *rest):
        del chip_ref
        rest[-1][...] = w_ref[...].astype(BF16)

    grid_spec = pltpu.PrefetchScalarGridSpec(
        num_scalar_prefetch=1, grid=(r // tr,),
        in_specs=[pl.BlockSpec((None, tr, C), lambda i, chip_ref: (layer, i, 0))] + [HBM_SPEC] * len(deps),
        out_specs=pl.BlockSpec((None, tr, C), lambda i, chip_ref: (chip_ref[0], i, 0)))
    return pl.pallas_call(body, name=name, grid_spec=grid_spec, out_shape=_sds((N_CHIPS, r, C), BF16),
                          compiler_params=_cparams(1))(chip, w, *deps)


def _adamw_stacked(w, gs, m, v, name):
    L, r, C = w.shape
    tr = _row_tile(r, C, 4, 2 * 1024 * 1024, 8)
    nb = r // tr

    def body(w_ref, m_ref, v_ref, *rest):
        go_ref, d_ref, nm_ref, nv_ref = rest[-4:]
        layer = pl.program_id(0)
        gv = rest[0][...]
        for k in range(1, L):
            gv = jnp.where(layer == k, rest[k][...], gv)
        mn = ADAM_B1 * m_ref[...] + (1.0 - ADAM_B1) * gv
        vn = ADAM_B2 * v_ref[...] + (1.0 - ADAM_B2) * jnp.square(gv)
        m_hat = mn / (1.0 - ADAM_B1 ** ADAM_STEP)
        v_hat = vn / (1.0 - ADAM_B2 ** ADAM_STEP)
        d_ref[...] = -ADAM_LR * (m_hat / (jnp.sqrt(v_hat) + ADAM_EPS) + ADAM_WD * w_ref[...])
        nm_ref[...] = mn
        nv_ref[...] = vn
        go_ref[...] = gv

    def grad_spec(k):
        return pl.BlockSpec((tr, C), lambda l, i: (jnp.where(l == k, i, jnp.where(l > k, nb - 1, 0)), 0))

    stacked = pl.BlockSpec((None, tr, C), lambda l, i: (l, i, 0))
    return pl.pallas_call(body, name=name, grid=(L, nb), in_specs=[stacked] * 3 + [grad_spec(k) for k in range(L)],
                          out_specs=[stacked] * 4, out_shape=[_sds((L, r, C), F32)] * 4,
                          compiler_params=_cparams(2))(w, m, v, *gs)


def _adamw(w, g, m, v, name):
    R, C = w.shape
    tr = _row_tile(R, C, 4, 1024 * 1024, 8)

    def body(w_ref, g_ref, m_ref, v_ref, d_ref, nm_ref, nv_ref):
        gv = g_ref[...]
        mn = ADAM_B1 * m_ref[...] + (1.0 - ADAM_B1) * gv
        vn = ADAM_B2 * v_ref[...] + (1.0 - ADAM_B2) * jnp.square(gv)
        m_hat = mn / (1.0 - ADAM_B1 ** ADAM_STEP)
        v_hat = vn / (1.0 - ADAM_B2 ** ADAM_STEP)
        d_ref[...] = -ADAM_LR * (m_hat / (jnp.sqrt(v_hat) + ADAM_EPS) + ADAM_WD * w_ref[...])
        nm_ref[...] = mn
        nv_ref[...] = vn

    spec = pl.BlockSpec((tr, C), lambda i: (i, 0))
    return pl.pallas_call(body, name=name, grid=(R // tr,), in_specs=[spec] * 4, out_specs=[spec] * 3,
                          out_shape=[_sds((R, C), F32)] * 3, compiler_params=_cparams(1))(w, g, m, v)


def _norm_matmul(x, g, w_g, out_dtype, name, row_sharded=False, deps=(), tm=1024, zero_rows=0):
    S, D = x.shape
    tm = min(tm, S)
    lead = zero_rows // tm
    if row_sharded:
        r, N = w_g.shape[1], w_g.shape[2]
        tn = 512
        w_spec = pl.BlockSpec((N_CHIPS, r, tn), lambda i, j: (0, 0, j))
    else:
        nsh = w_g.shape[2]
        N = N_CHIPS * nsh
        tn = next((t for t in (1024, 512) if nsh % t == 0), nsh)
        bps = nsh // tn
        w_spec = pl.BlockSpec((None, D, tn), lambda i, j: (j // bps, 0, j % bps))
    whole = not row_sharded and N <= 2048
    if whole:
        tn = N
        w_spec = pl.BlockSpec((N_CHIPS, D, nsh), lambda i, j: (0, 0, 0))

    def body(x_ref, g_ref, w_ref, *rest):
        y_ref, h_ref = rest[-2:]
        i = pl.program_id(0)

        @pl.when((i >= lead) & (pl.program_id(1) == 0))
        def _():
            xhat, _ = _rms_hat(x_ref[...])
            h_ref[...] = (xhat * g_ref[...]).astype(BF16)

        @pl.when(i >= lead)
        def _():
            if whole:
                for s in range(N_CHIPS):
                    y_ref[:, s * nsh:(s + 1) * nsh] = jnp.dot(h_ref[...], w_ref[s],
                                                             preferred_element_type=F32).astype(y_ref.dtype)
            else:
                w = w_ref[...].reshape(D, tn)
                y_ref[...] = jnp.dot(h_ref[...], w, preferred_element_type=F32).astype(y_ref.dtype)

        if lead:
            @pl.when(i < lead)
            def _():
                y_ref[...] = jnp.zeros_like(y_ref)

    rows = lambda i, j: (jnp.maximum(i - lead, 0), 0)
    return pl.pallas_call(
        body, name=name, grid=(lead + S // tm, N // tn),
        in_specs=[pl.BlockSpec((tm, D), rows), pl.BlockSpec((1, D), lambda i, j: (0, 0)), w_spec]
        + [HBM_SPEC] * len(deps),
        out_specs=[pl.BlockSpec((tm, tn), lambda i, j: (i, j)), pl.BlockSpec((tm, D), rows)],
        out_shape=[_sds((zero_rows + S, N), out_dtype), _sds((S, D), BF16)],
        compiler_params=_cparams(2))(x, g, w_g, *deps)


def _matmul_res(a, w_g, res, name, swiglu=False, deps=(), tm=256):
    S, N = res.shape
    r = w_g.shape[1]
    K = N_CHIPS * r

    def body(*refs):
        o_ref = refs[-1]
        if swiglu:
            gate_ref, up_ref, w_ref, res_ref = refs[:4]
            a_blk = _swiglu(gate_ref[...].astype(F32), up_ref[...].astype(F32)).astype(BF16)
        else:
            a_ref, w_ref, res_ref = refs[:3]
            a_blk = a_ref[...]
        o_ref[...] = res_ref[...] + jnp.dot(a_blk, w_ref[...].reshape(K, N), preferred_element_type=F32)

    a_specs, a_ops = [pl.BlockSpec((tm, K), lambda i: (i, 0))], [a]
    if swiglu:
        a_specs.append(pl.BlockSpec((tm, K), lambda i: (i, 1)))
        a_ops.append(a)
    row = pl.BlockSpec((tm, N), lambda i: (i, 0))
    return pl.pallas_call(
        body, name=name, grid=(S // tm,),
        in_specs=a_specs + [pl.BlockSpec((N_CHIPS, r, N), lambda i: (0, 0, 0)), row] + [HBM_SPEC] * len(deps),
        out_specs=row, out_shape=_sds((S, N), F32), compiler_params=_cparams(1))(*a_ops, w_g, res, *deps)


def _matmul_tn(a_ops, a_specs, a_fn, b_op, b_spec, out_spec, out_shape, acc_shape, n_outer, name, deps=(), tt=512):
    S = b_op.shape[-2]
    na = len(a_ops)
    nt = S // tt

    def body(*refs):
        a_refs, b_ref, o_ref, acc_ref = refs[:na], refs[na], refs[-2], refs[-1]
        t = pl.program_id(1)
        part = lax.dot_general(a_fn(*a_refs), b_ref[...].astype(BF16), TN_DIMS, preferred_element_type=F32)

        @pl.when(t == 0)
        def _():
            acc_ref[...] = part

        @pl.when(t > 0)
        def _():
            acc_ref[...] += part

        @pl.when(t == nt - 1)
        def _():
            o_ref[...] = acc_ref[...].reshape(o_ref.shape).astype(BF16)

    return pl.pallas_call(
        body, name=name, grid=(n_outer, nt), in_specs=list(a_specs) + [b_spec] + [HBM_SPEC] * len(deps),
        out_specs=out_spec, out_shape=_sds(out_shape, BF16), scratch_shapes=[pltpu.VMEM(acc_shape, F32)],
        compiler_params=_cparams(2))(*a_ops, b_op, *deps)


def _nt_accumulate(a_ref, w_ref, acc_ref, w2d, nk):
    k = pl.program_id(1)
    part = lax.dot_general(a_ref[...].astype(BF16), w_ref[...].reshape(w2d), NT_DIMS, preferred_element_type=F32)

    @pl.when(k == 0)
    def _():
        acc_ref[...] = part

    @pl.when(k > 0)
    def _():
        acc_ref[...] += part

    return k == nk - 1


def _nt_normbwd(dy, dy_spec, w_g, w_spec, w2d, nk, x, g, dres, name, deps=(), tm=NORMBWD_ROWS):
    S, D = x.shape
    tm = min(tm, S)

    def body(dy_ref, w_ref, x_ref, g_ref, dres_ref, *rest):
        dx_ref, dxb_ref, dg_ref, acc_ref = rest[-4:]

        @pl.when((pl.program_id(0) == 0) & (pl.program_id(1) == 0))
        def _():
            dg_ref[...] = jnp.zeros_like(dg_ref)

        last = _nt_accumulate(dy_ref, w_ref, acc_ref, w2d, nk)

        @pl.when(last)
        def _():
            xhat, r = _rms_hat(x_ref[...])
            dx, dgp = _rms_bwd(xhat, r, g_ref[...], acc_ref[...])
            total = dres_ref[...] + dx
            dx_ref[...] = total
            dxb_ref[...] = total.astype(BF16)
            dg_ref[...] += jnp.sum(dgp, axis=0, keepdims=True)

    row = pl.BlockSpec((tm, D), lambda i, k: (i, 0))
    vec = pl.BlockSpec((1, D), lambda i, k: (0, 0))
    return pl.pallas_call(
        body, name=name, grid=(S // tm, nk),
        in_specs=[dy_spec, w_spec, row, vec, row] + [HBM_SPEC] * len(deps), out_specs=[row, row, vec],
        out_shape=[_sds((S, D), F32), _sds((S, D), BF16), _sds((1, D), F32)],
        scratch_shapes=[pltpu.VMEM((tm, D), F32)], compiler_params=_cparams(2))(dy, w_g, x, g, dres, *deps)


def _nt_rows(dy, w_g, shards_per_block, out_dtype, name, deps=(), tm=1024):
    S, N = dy.shape
    tm = min(tm, S)
    r = w_g.shape[1]
    tn = shards_per_block * r

    def body(dy_ref, w_ref, *rest):
        o_ref = rest[-1]
        o_ref[...] = lax.dot_general(dy_ref[...].astype(BF16), w_ref[...].reshape(tn, N), NT_DIMS,
                                     preferred_element_type=F32).astype(o_ref.dtype)

    return pl.pallas_call(
        body, name=name, grid=(S // tm, N_CHIPS // shards_per_block),
        in_specs=[pl.BlockSpec((tm, N), lambda i, j: (i, 0)),
                  pl.BlockSpec((shards_per_block, r, N), lambda i, j: (j, 0, 0))] + [HBM_SPEC] * len(deps),
        out_specs=pl.BlockSpec((tm, tn), lambda i, j: (i, j)),
        out_shape=_sds((S, N_CHIPS * r), out_dtype), compiler_params=_cparams(2))(dy, w_g, *deps)


def _nt_swiglu_bwd(dy, w_g, gu, name, deps=(), tm=1024):
    S, N = dy.shape
    tm = min(tm, S)
    r = w_g.shape[1]
    tn = 2 * r
    F = N_CHIPS * r

    def body(dy_ref, w_ref, gate_ref, up_ref, *rest):
        o_ref = rest[-1]
        dact = lax.dot_general(dy_ref[...].astype(BF16), w_ref[...].reshape(tn, N), NT_DIMS,
                               preferred_element_type=F32)
        gate, up = gate_ref[...].astype(F32), up_ref[...].astype(F32)
        sg = jax.nn.sigmoid(gate)
        silu = gate * sg
        o_ref[0] = ((dact * up) * (sg + silu * (1.0 - sg))).astype(BF16)
        o_ref[1] = (dact * silu).astype(BF16)

    return pl.pallas_call(
        body, name=name, grid=(2, S // tm),
        in_specs=[pl.BlockSpec((tm, N), lambda j, i: (i, 0)),
                  pl.BlockSpec((2, r, N), lambda j, i: (j, 0, 0)),
                  pl.BlockSpec((tm, tn), lambda j, i: (i, j)),
                  pl.BlockSpec((tm, tn), lambda j, i: (i, 2 + j))] + [HBM_SPEC] * len(deps),
        out_specs=pl.BlockSpec((2, tm, tn), lambda j, i: (0, i, j)),
        out_shape=_sds((2, S, F), BF16), compiler_params=_cparams(2))(dy, w_g, gu, gu, *deps)


def _chunk_causal_mask(transposed):
    i = lax.broadcasted_iota(jnp.int32, (A_CHUNK, A_CHUNK), 0) // CHUNK
    j = lax.broadcasted_iota(jnp.int32, (A_CHUNK, A_CHUNK), 1) // CHUNK
    return ((i <= j) if transposed else (i >= j)).astype(F32)


def _sgu_fwd(zpre, g_sgu, ws, bs_t, name, deps=()):
    S, F2 = zpre.shape
    F = F2 // 2
    gd = F // A_GROUPS

    windows = 2
    rows = windows * A_CHUNK

    def body(zu_ref, zv_ref, g_ref, ws_ref, b_ref, *rest):
        o_ref = rest[-1]
        vhat, _ = _rms_hat(_gelu(zv_ref[...].astype(F32)))
        vn = (vhat * g_ref[...]).astype(BF16)
        u = _gelu(zu_ref[...].astype(F32))
        mask = _chunk_causal_mask(False)
        for gi in range(A_GROUPS):
            sl = slice(gi * gd, (gi + 1) * gd)
            wm = (ws_ref[gi] * mask).astype(BF16)
            for w in range(windows):
                win = slice(w * A_CHUNK, (w + 1) * A_CHUNK)
                vs = jnp.dot(wm, vn[win, sl], preferred_element_type=F32) + b_ref[:, gi:gi + 1]
                o_ref[win, sl] = (u[win, sl] * vs).astype(BF16)

    return pl.pallas_call(
        body, name=name, grid=(S // rows,),
        in_specs=[pl.BlockSpec((rows, F), lambda i: (i, 0)),
                  pl.BlockSpec((rows, F), lambda i: (i, 1)),
                  pl.BlockSpec((1, F), lambda i: (0, 0)),
                  pl.BlockSpec((A_GROUPS, A_CHUNK, A_CHUNK), lambda i: (0, 0, 0)),
                  pl.BlockSpec((A_CHUNK, A_GROUPS), lambda i: (0, 0))] + [HBM_SPEC] * len(deps),
        out_specs=pl.BlockSpec((rows, F), lambda i: (i, 0)),
        out_shape=_sds((S, F), BF16), compiler_params=_cparams(1))(zpre, zpre, g_sgu, ws, bs_t, *deps)


def _sgu_bwd(zpre, duv, g_sgu, ws, ws_t, bs_t, name):
    S, F2 = zpre.shape
    F = F2 // 2
    gd = F // A_GROUPS

    def body(zu_ref, zv_ref, duv_ref, g_ref, ws_ref, wst_ref, b_ref, dz_ref, dg_ref, dws_ref, dbs_ref, dvn_ref):
        @pl.when(pl.program_id(0) == 0)
        def _():
            dg_ref[...] = jnp.zeros_like(dg_ref)
            dws_ref[...] = jnp.zeros_like(dws_ref)
            dbs_ref[...] = jnp.zeros_like(dbs_ref)

        gv = g_ref[...]
        u, u_grad = _gelu_and_grad(zu_ref[...].astype(F32))
        v, v_grad = _gelu_and_grad(zv_ref[...].astype(F32))
        vhat, r = _rms_hat(v)
        vn = (vhat * gv).astype(BF16)
        duv_v = duv_ref[...].astype(F32)
        dvs = duv_v * u
        dvs_b = dvs.astype(BF16)
        mask = _chunk_causal_mask(False)
        mask_t = _chunk_causal_mask(True)
        for gi in range(A_GROUPS):
            sl = slice(gi * gd, (gi + 1) * gd)
            wm = (ws_ref[gi] * mask).astype(BF16)
            vs = jnp.dot(wm, vn[:, sl], preferred_element_type=F32) + b_ref[:, gi:gi + 1]
            dz_ref[:, sl] = ((duv_v[:, sl] * vs) * u_grad[:, sl]).astype(BF16)
            dws_ref[gi] += lax.dot_general(dvs_b[:, sl], vn[:, sl], NT_DIMS, preferred_element_type=F32) * mask
            dbs_ref[gi] += jnp.broadcast_to(jnp.sum(dvs[:, sl], axis=1, keepdims=True), (A_CHUNK, A_CHUNK))
            wm_t = (wst_ref[gi] * mask_t).astype(BF16)
            dvn_ref[:, sl] = jnp.dot(wm_t, dvs_b[:, sl], preferred_element_type=F32)
        dv, dg_part = _rms_bwd(vhat, r, gv, dvn_ref[...])
        dg_ref[...] += jnp.sum(dg_part, axis=0, keepdims=True)
        dz_ref[:, F:] = (dv * v_grad).astype(BF16)

    blk = pl.BlockSpec((A_CHUNK, F), lambda i: (i, 0))
    const3 = pl.BlockSpec((A_GROUPS, A_CHUNK, A_CHUNK), lambda i: (0, 0, 0))
    return pl.pallas_call(
        body, name=name, grid=(S // A_CHUNK,),
        in_specs=[blk, pl.BlockSpec((A_CHUNK, F), lambda i: (i, 1)), blk,
                  pl.BlockSpec((1, F), lambda i: (0, 0)), const3, const3,
                  pl.BlockSpec((A_CHUNK, A_GROUPS), lambda i: (0, 0))],
        out_specs=[pl.BlockSpec((A_CHUNK, F2), lambda i: (i, 0)), pl.BlockSpec((1, F), lambda i: (0, 0)),
                   const3, const3],
        out_shape=[_sds((S, F2), BF16), _sds((1, F), F32), _sds((A_GROUPS, A_CHUNK, A_CHUNK), F32),
                   _sds((A_GROUPS, A_CHUNK, A_CHUNK), F32)],
        scratch_shapes=[pltpu.VMEM((A_CHUNK, F), F32)],
        compiler_params=_cparams(1))(zpre, zpre, duv, g_sgu, ws, ws_t, bs_t)


def _toeplitz_one_hot():
    row = lax.broadcasted_iota(jnp.int32, (TABLE_PAD, DIAGS), 0)
    j = lax.broadcasted_iota(jnp.int32, (TABLE_PAD, DIAGS), 1)
    idx = jnp.clip(KV_PAD + Q_BLOCK - j, -MAX_REL, MAX_REL) + MAX_REL
    return (row == idx).astype(F32)


def _rel_bias_fwd(table, name):
    H = table.shape[0]

    def body(t_ref, o_ref):
        diag = jnp.dot(t_ref[...], _toeplitz_one_hot(), precision=HIGHEST, preferred_element_type=F32)
        q_chunk = lax.broadcasted_iota(jnp.int32, (Q_BLOCK, BAND), 0) // CHUNK
        k_chunk = lax.broadcasted_iota(jnp.int32, (Q_BLOCK, BAND), 1) // CHUNK
        unseen = jnp.where((k_chunk >= q_chunk) & (k_chunk <= q_chunk + N_LEFT), 0.0, NEG_INF)
        for h in range(H):
            rows = jnp.broadcast_to(diag[h:h + 1, :], (Q_BLOCK, DIAGS))
            o_ref[h] = pltpu.roll(rows, DIAGS - Q_BLOCK, 1, stride=1, stride_axis=0)[:, :BAND] + unseen

    return pl.pallas_call(body, name=name, out_shape=_sds((H, Q_BLOCK, BAND), F32),
                          compiler_params=pltpu.CompilerParams(vmem_limit_bytes=VMEM_LIMIT_BYTES))(table)


def _rel_bias_bwd(dbias, name):
    H = dbias.shape[0]

    def body(d_ref, o_ref):
        def step(r, acc):
            row = d_ref[:, pl.ds(r, 1), :].reshape(H, BAND)
            row = jnp.concatenate([row, jnp.zeros((H, DIAGS - BAND), F32)], axis=1)
            return acc + pltpu.roll(row, Q_BLOCK - r, 1)

        diag = lax.fori_loop(0, Q_BLOCK, step, jnp.zeros((H, DIAGS), F32))
        o_ref[...] = lax.dot_general(diag, _toeplitz_one_hot(), NT_DIMS, precision=HIGHEST,
                                     preferred_element_type=F32)

    return pl.pallas_call(body, name=name, out_shape=_sds((H, TABLE_PAD), F32),
                          compiler_params=pltpu.CompilerParams(vmem_limit_bytes=VMEM_LIMIT_BYTES))(dbias)


def _head_rows(t):
    lane = lax.broadcasted_iota(jnp.int32, t.shape, 1)
    zero = jnp.zeros_like(t)
    return jnp.concatenate([jnp.where(lane < HEAD_DIM, t, zero), jnp.where(lane >= HEAD_DIM, t, zero)], axis=0)


def _head_lanes(t2):
    lane = lax.broadcasted_iota(jnp.int32, (Q_BLOCK, t2.shape[1]), 1)
    return jnp.where(lane < HEAD_DIM, t2[:Q_BLOCK], t2[Q_BLOCK:])


def _attn_probs(q2, kb, bias2, block):
    kj = lax.broadcasted_iota(jnp.int32, (1, BAND), 1)
    before_start = jnp.where(block * Q_BLOCK + kj - KV_PAD >= 0, 0.0, NEG_INF)
    s = lax.dot_general(q2 * ATTN_SCALE, kb, NT_DIMS, preferred_element_type=F32) + bias2 + before_start
    e = jnp.exp(s - jnp.max(s, axis=-1, keepdims=True))
    return e / jnp.sum(e, axis=-1, keepdims=True)


def _attn_specs(S):
    lanes = HEADS_PER_BLOCK * HEAD_DIM
    rows = S + KV_PAD
    per_step = min(BLOCKS_PER_STEP, S // Q_BLOCK)
    q_spec = pl.BlockSpec((per_step * Q_BLOCK, lanes), lambda h, i: (i, h))
    k_spec = pl.BlockSpec((rows, lanes), lambda h, i: (0, h))
    v_spec = pl.BlockSpec((rows, lanes), lambda h, i: (0, N_HEADS // HEADS_PER_BLOCK + h))
    b_spec = pl.BlockSpec((HEADS_PER_BLOCK, Q_BLOCK, BAND), lambda h, i: (h, 0, 0))
    return q_spec, k_spec, v_spec, b_spec, per_step


def _attn_fwd(q, kvp, bias, name, deps=()):
    S, HD = q.shape
    q_spec, k_spec, v_spec, b_spec, per_step = _attn_specs(S)

    def body(q_ref, k_ref, v_ref, b_ref, *rest):
        o_ref = rest[-1]
        for b in range(per_step):
            block = pl.program_id(1) * per_step + b
            rows = slice(b * Q_BLOCK, (b + 1) * Q_BLOCK)
            band = pl.ds(pl.multiple_of(block * Q_BLOCK, Q_BLOCK), BAND)
            p = _attn_probs(_head_rows(q_ref[rows, :]), k_ref[band, :], b_ref[...].reshape(2 * Q_BLOCK, BAND), block)
            o2 = jnp.dot(p.astype(BF16), v_ref[band, :], preferred_element_type=F32)
            o_ref[rows, :] = _head_lanes(o2).astype(BF16)

    return pl.pallas_call(
        body, name=name, grid=(N_HEADS // HEADS_PER_BLOCK, S // (per_step * Q_BLOCK)),
        in_specs=[q_spec, k_spec, v_spec, b_spec] + [HBM_SPEC] * len(deps), out_specs=q_spec,
        out_shape=_sds((S, HD), BF16), compiler_params=_cparams(2))(q, kvp, kvp, bias, *deps)


def _attn_bwd(q, kvp, bias, do, dkv_prev, name):
    S, HD = q.shape
    lanes = HEADS_PER_BLOCK * HEAD_DIM
    q_spec, k_spec, v_spec, b_spec, per_step = _attn_specs(S)
    dkv_spec = pl.BlockSpec((2, S + KV_PAD, lanes), lambda h, i: (0, 0, h))
    prev = [] if dkv_prev is None else [dkv_prev]
    n_steps = S // (per_step * Q_BLOCK)

    def body(q_ref, k_ref, v_ref, b_ref, do_ref, *rest):
        dq_ref, dkv_ref, db_ref = rest[len(prev):len(prev) + 3]

        @pl.when(pl.program_id(1) == 0)
        def _():
            dkv_ref[...] = rest[0][...] if prev else jnp.zeros_like(dkv_ref)
            db_ref[...] = jnp.zeros_like(db_ref)

        db = jnp.zeros((2 * Q_BLOCK, BAND), F32)
        for b in range(per_step):
            block = pl.program_id(1) * per_step + b
            rows = slice(b * Q_BLOCK, (b + 1) * Q_BLOCK)
            band = pl.ds(pl.multiple_of(block * Q_BLOCK, Q_BLOCK), BAND)
            kb, vb = k_ref[band, :], v_ref[band, :]
            q2, do2 = _head_rows(q_ref[rows, :]), _head_rows(do_ref[rows, :])
            p = _attn_probs(q2, kb, b_ref[...].reshape(2 * Q_BLOCK, BAND), block)
            dp = lax.dot_general(do2, vb, NT_DIMS, preferred_element_type=F32)
            ds = p * (dp - jnp.sum(dp * p, axis=-1, keepdims=True))
            db = db + ds
            ds_b = (ds * ATTN_SCALE).astype(BF16)
            dq_ref[rows, :] = _head_lanes(jnp.dot(ds_b, kb, preferred_element_type=F32)).astype(BF16)
            dkv_ref[0, band, :] += lax.dot_general(ds_b, q2, TN_DIMS, preferred_element_type=F32)
            dkv_ref[1, band, :] += lax.dot_general(p.astype(BF16), do2, TN_DIMS, preferred_element_type=F32)
        db_ref[...] += db.reshape(HEADS_PER_BLOCK, Q_BLOCK, BAND)

        if prev:
            @pl.when(pl.program_id(1) == n_steps - 1)
            def _():
                rest[-1][...] = dkv_ref[:, KV_PAD:, :].astype(BF16)

    return pl.pallas_call(
        body, name=name, grid=(N_HEADS // HEADS_PER_BLOCK, n_steps),
        in_specs=[q_spec, k_spec, v_spec, b_spec, q_spec] + [dkv_spec] * len(prev),
        out_specs=[q_spec, dkv_spec, b_spec] + [pl.BlockSpec((2, S, lanes), lambda h, i: (0, 0, h))] * len(prev),
        out_shape=[_sds((S, HD), BF16), _sds((2, S + KV_PAD, HD), F32), _sds((N_HEADS, Q_BLOCK, BAND), F32)]
        + [_sds((2, S, HD), BF16)] * len(prev),
        compiler_params=_cparams(2))(q, kvp, kvp, bias, do, *prev)


def _loss_head(x, g, target, name, tm=512):
    S, D = x.shape

    def body(x_ref, g_ref, t_ref, loss_ref, dx_ref, dxb_ref, dg_ref):
        @pl.when(pl.program_id(0) == 0)
        def _():
            loss_ref[...] = jnp.zeros_like(loss_ref)
            dg_ref[...] = jnp.zeros_like(dg_ref)

        xhat, r = _rms_hat(x_ref[...])
        gv = g_ref[...]
        err = xhat * gv - t_ref[...]
        loss_ref[...] += 0.5 * jnp.sum(jnp.mean(err * err, axis=-1, keepdims=True))
        dx, dgp = _rms_bwd(xhat, r, gv, err * (1.0 / D))
        dx_ref[...] = dx
        dxb_ref[...] = dx.astype(BF16)
        dg_ref[...] += jnp.sum(dgp, axis=0, keepdims=True)

    row = pl.BlockSpec((tm, D), lambda i: (i, 0))
    vec = pl.BlockSpec((1, D), lambda i: (0, 0))
    return pl.pallas_call(
        body, name=name, grid=(S // tm,), in_specs=[row, vec, row],
        out_specs=[pl.BlockSpec((8, 128), lambda i: (0, 0)), row, row, vec],
        out_shape=[_sds((8, 128), F32), _sds((S, D), F32), _sds((S, D), BF16), _sds((1, D), F32)],
        compiler_params=_cparams(1))(x, g, target)


def _place():
    x, y, c = lax.axis_index("x"), lax.axis_index("y"), lax.axis_index("c")
    chips = [(1 - x, y), (x, 1 - y), (1 - x, 1 - y)]
    return x, y, c, chips


def _half_rows(c, r):
    return pl.ds(pl.multiple_of(c * (r // 2), 8), r // 2)


HBM_SPEC = pl.BlockSpec(memory_space=pl.ANY)


STRICT_HBM_SPEC = pl.BlockSpec(memory_space=pltpu.HBM)
SEM_SPEC = pl.BlockSpec(memory_space=pltpu.SEMAPHORE)
EFFECT = pltpu.SideEffectType.DATAFLOW_SIDE_EFFECTING


def _peers(x, y, c):
    out = []
    for k in range(1, N_DEV):
        px, py, pc = (x + ((k >> 2) & 1)) % 2, (y + ((k >> 1) & 1)) % 2, (c + (k & 1)) % 2
        out.append(((px, py, pc), 2 * px + py, pc, 4 * px + 2 * py + pc))
    return out


def _token_spec():
    return pl.BlockSpec(memory_space=pltpu.VMEM)


def _hbm(a):
    return pltpu.with_memory_space_constraint(a, pltpu.HBM)


def _slab_half(ref, chip, core):
    return ref.at[2 * chip[0] + chip[1], _half_rows(core, ref.shape[1]), :]


def _allgather_start(slabs, name):
    n = len(slabs)

    def body(*refs):
        src, send, recv, token = refs[:n], refs[n], refs[n + 1], refs[-1]
        x, y, c, chips = _place()
        for a in range(n):
            own = _slab_half(src[a], (x, y), c)
            for j, chip in enumerate(chips):
                pltpu.make_async_remote_copy(src_ref=own, dst_ref=own, send_sem=send.at[3 * a + j], recv_sem=recv.at[3 * a + j],
                                             device_id=(*chip, c), device_id_type=MESH).start()
        token[...] = jnp.zeros_like(token)

    sems = pltpu.SemaphoreType.DMA((3 * n,))
    send, recv, *flying, token = pl.pallas_call(
        body, name=name, in_specs=[STRICT_HBM_SPEC] * n,
        out_shape=(sems, sems, *[pltpu.HBM(s.shape, s.dtype) for s in slabs], _sds((8, 128), F32)),
        out_specs=(SEM_SPEC, SEM_SPEC, *[STRICT_HBM_SPEC] * n, _token_spec()),
        input_output_aliases={a: a + 2 for a in range(n)},
        compiler_params=pltpu.CompilerParams(has_side_effects=EFFECT))(*[_hbm(s) for s in slabs])
    return send, recv, flying, token


def _allgather_relay(flying, send, recv, first, after, name):
    n = len(flying)

    def body(*refs):
        src, send_ref, recv_ref = refs[:n], refs[n], refs[n + 1]
        send2, recv2, token = refs[n + 3], refs[n + 4], refs[-1]
        token[...] = jnp.zeros_like(token)
        x, y, c, chips = _place()
        for a in range(n):
            for j, chip in enumerate(chips):
                cp = pltpu.make_async_remote_copy(
                    src_ref=_slab_half(src[a], (x, y), c), dst_ref=_slab_half(src[a], chip, c),
                    send_sem=send_ref.at[3 * (first + a) + j], recv_sem=recv_ref.at[3 * (first + a) + j],
                    device_id=(*chip, c), device_id_type=MESH)
                cp.wait_send()
                cp.wait_recv()
        for a in range(n):
            for j, chip in enumerate(chips):
                landed = _slab_half(src[a], chip, c)
                pltpu.make_async_remote_copy(src_ref=landed, dst_ref=landed, send_sem=send2.at[3 * a + j],
                                             recv_sem=recv2.at[3 * a + j], device_id=(x, y, 1 - c),
                                             device_id_type=MESH).start()

    sems = pltpu.SemaphoreType.DMA((3 * n,))
    send2, recv2, *relayed, token = pl.pallas_call(
        body, name=name, in_specs=[STRICT_HBM_SPEC] * n + [SEM_SPEC, SEM_SPEC, HBM_SPEC],
        out_shape=(sems, sems, *[pltpu.HBM(s.shape, s.dtype) for s in flying], _sds((8, 128), F32)),
        out_specs=(SEM_SPEC, SEM_SPEC, *[STRICT_HBM_SPEC] * n, _token_spec()),
        input_output_aliases={a: a + 2 for a in range(n)},
        compiler_params=pltpu.CompilerParams(has_side_effects=EFFECT))(*flying, send, recv, after)
    return send2, recv2, relayed, token


def _allgather_wait(relayed, send2, recv2, after, name):
    n = len(relayed)

    def body(*refs):
        src, send_ref, recv_ref = refs[:n], refs[n], refs[n + 1]
        x, y, c, chips = _place()
        for a in range(n):
            for j, chip in enumerate(chips):
                cp = pltpu.make_async_remote_copy(
                    src_ref=_slab_half(src[a], chip, c), dst_ref=_slab_half(src[a], chip, 1 - c),
                    send_sem=send_ref.at[3 * a + j], recv_sem=recv_ref.at[3 * a + j],
                    device_id=(x, y, 1 - c), device_id_type=MESH)
                cp.wait_send()
                cp.wait_recv()

    return pl.pallas_call(
        body, name=name, in_specs=[STRICT_HBM_SPEC] * n + [SEM_SPEC, SEM_SPEC, HBM_SPEC],
        out_shape=tuple(pltpu.HBM(s.shape, s.dtype) for s in relayed), out_specs=tuple([STRICT_HBM_SPEC] * n),
        input_output_aliases={a: a for a in range(n)},
        compiler_params=pltpu.CompilerParams(has_side_effects=EFFECT))(*relayed, send2, recv2, after)


def _allgather_small(small, name):
    def body(sm, osm, send, recv, local):
        x, y, c, chips = _place()
        own = pltpu.make_async_copy(sm, osm.at[2 * x + y], local)
        own.start()
        cps = [pltpu.make_async_remote_copy(src_ref=sm, dst_ref=osm.at[2 * x + y], send_sem=send.at[j],
                                            recv_sem=recv.at[j], device_id=(*chip, c), device_id_type=MESH)
               for j, chip in enumerate(chips)]
        for cp in cps:
            cp.start()
        for j, chip in enumerate(chips):
            got = osm.at[2 * chip[0] + chip[1]]
            pltpu.make_async_remote_copy(src_ref=got, dst_ref=got, send_sem=send.at[j], recv_sem=recv.at[j],
                                         device_id=(x, y, c), device_id_type=MESH).wait_recv()
        for cp in cps:
            cp.wait_send()
        own.wait()

    return pl.pallas_call(
        body, name=name, in_specs=[pl.BlockSpec(memory_space=pltpu.VMEM)], out_specs=HBM_SPEC,
        out_shape=_sds((N_CHIPS, *small.shape), small.dtype),
        scratch_shapes=[pltpu.SemaphoreType.DMA((3,)), pltpu.SemaphoreType.DMA((3,)), pltpu.SemaphoreType.DMA])(small)


def _reduce_start(grads, name):
    n = len(grads)

    def body(*refs):
        src, land, send, recv, token = refs[:n], refs[n:2 * n], refs[2 * n], refs[2 * n + 1], refs[-1]
        x, y, c, _ = _place()
        me = 4 * x + 2 * y + c
        for a in range(n):
            for k, (peer, p_chip, p_core, _) in enumerate(_peers(x, y, c)):
                pltpu.make_async_remote_copy(
                    src_ref=src[a].at[p_chip, _half_rows(p_core, src[a].shape[1]), :], dst_ref=land[a].at[me],
                    send_sem=send.at[(N_DEV - 1) * a + k], recv_sem=recv.at[(N_DEV - 1) * a + k],
                    device_id=peer, device_id_type=MESH).start()
        token[...] = jnp.zeros_like(token)

    lands = [lax.empty((N_DEV, g.shape[1] // 2, g.shape[2]), BF16) for g in grads]
    sems = pltpu.SemaphoreType.DMA(((N_DEV - 1) * n,))
    shapes = [pltpu.HBM(a.shape, a.dtype) for a in grads + lands]
    send, recv, *flying, token = pl.pallas_call(
        body, name=name, in_specs=[STRICT_HBM_SPEC] * (2 * n),
        out_shape=(sems, sems, *shapes, _sds((8, 128), F32)),
        out_specs=(SEM_SPEC, SEM_SPEC, *[STRICT_HBM_SPEC] * (2 * n), _token_spec()),
        input_output_aliases={a: a + 2 for a in range(2 * n)},
        compiler_params=pltpu.CompilerParams(has_side_effects=EFFECT))(*[_hbm(a) for a in grads + lands])
    return send, recv, flying[:n], flying[n:], token


def _reduce_wait(started, after, name):
    sizes = [len(grads) for _, _, grads, _ in started]
    n_arr = 2 * sum(sizes)

    def body(*refs):
        x, y, c, _ = _place()
        at = 0
        for s, n in enumerate(sizes):
            src, land = refs[at:at + n], refs[at + n:at + 2 * n]
            send_ref, recv_ref = refs[n_arr + 2 * s], refs[n_arr + 2 * s + 1]
            at += 2 * n
            for a in range(n):
                for k, (peer, p_chip, p_core, p_dev) in enumerate(_peers(x, y, c)):
                    cp = pltpu.make_async_remote_copy(
                        src_ref=src[a].at[p_chip, _half_rows(p_core, src[a].shape[1]), :], dst_ref=land[a].at[p_dev],
                        send_sem=send_ref.at[(N_DEV - 1) * a + k], recv_sem=recv_ref.at[(N_DEV - 1) * a + k],
                        device_id=peer, device_id_type=MESH)
                    cp.wait_send()
                    cp.wait_recv()

    arrays, sems = [], []
    for send, recv, grads, lands in started:
        arrays += list(grads) + list(lands)
        sems += [send, recv]
    out = pl.pallas_call(
        body, name=name, in_specs=[STRICT_HBM_SPEC] * n_arr + [SEM_SPEC] * len(sems) + [HBM_SPEC],
        out_shape=tuple(pltpu.HBM(a.shape, a.dtype) for a in arrays), out_specs=tuple([STRICT_HBM_SPEC] * n_arr),
        input_output_aliases={a: a for a in range(n_arr)},
        compiler_params=pltpu.CompilerParams(has_side_effects=EFFECT))(*arrays, *sems, after)
    result, at = [], 0
    for n in sizes:
        result.append((out[at:at + n], out[at + n:at + 2 * n]))
        at += 2 * n
    return result


def _reduce_sum(grad, land, place, name, deps=()):
    _, r2, C = land.shape
    tr = _row_tile(r2, C, 4, 2 * 1024 * 1024, 16)
    nb = r2 // tr

    def body(place_ref, own_ref, *rest):
        del place_ref
        acc = own_ref[...].astype(F32)
        for ref in rest[:N_DEV - 1]:
            acc = acc + ref[...].astype(F32)
        rest[-1][...] = acc

    def from_dev(k):
        return pl.BlockSpec((None, tr, C), lambda i, place_ref: ((place_ref[2] + k) % N_DEV, i, 0))

    grid_spec = pltpu.PrefetchScalarGridSpec(
        num_scalar_prefetch=1, grid=(nb,),
        in_specs=[pl.BlockSpec((None, tr, C), lambda i, place_ref: (place_ref[0], place_ref[1] * nb + i, 0))]
        + [from_dev(k) for k in range(1, N_DEV)] + [HBM_SPEC] * len(deps),
        out_specs=pl.BlockSpec((tr, C), lambda i, place_ref: (place_ref[1] * nb + i, 0)))
    return pl.pallas_call(body, name=name, grid_spec=grid_spec, out_shape=_sds((2 * r2, C), F32),
                          compiler_params=_cparams(1))(place, grad, *[land] * (N_DEV - 1), *deps)


def _join_start(halves, name):
    n = len(halves)

    def body(*refs):
        src, send, recv, token = refs[:n], refs[n], refs[n + 1], refs[-1]
        x, y, c, _ = _place()
        for w in range(n):
            mine = src[w].at[_half_rows(c, src[w].shape[0]), :]
            pltpu.make_async_remote_copy(src_ref=mine, dst_ref=mine, send_sem=send.at[w], recv_sem=recv.at[w],
                                         device_id=(x, y, 1 - c), device_id_type=MESH).start()
        token[...] = jnp.zeros_like(token)

    sems = pltpu.SemaphoreType.DMA((n,))
    send, recv, *flying, token = pl.pallas_call(
        body, name=name, in_specs=[STRICT_HBM_SPEC] * n,
        out_shape=(sems, sems, *[pltpu.HBM(a.shape, a.dtype) for a in halves], _sds((8, 128), F32)),
        out_specs=(SEM_SPEC, SEM_SPEC, *[STRICT_HBM_SPEC] * n, _token_spec()),
        input_output_aliases={w: w + 2 for w in range(n)},
        compiler_params=pltpu.CompilerParams(has_side_effects=EFFECT))(*[_hbm(a) for a in halves])
    return send, recv, flying, token


def _join_wait(flying, send, recv, after, name):
    n = len(flying)

    def body(*refs):
        src, send_ref, recv_ref = refs[:n], refs[n], refs[n + 1]
        x, y, c, _ = _place()
        for w in range(n):
            cp = pltpu.make_async_remote_copy(
                src_ref=src[w].at[_half_rows(c, src[w].shape[0]), :],
                dst_ref=src[w].at[_half_rows(1 - c, src[w].shape[0]), :], send_sem=send_ref.at[w],
                recv_sem=recv_ref.at[w], device_id=(x, y, 1 - c),
                device_id_type=MESH)
            cp.wait_send()
            cp.wait_recv()

    return pl.pallas_call(
        body, name=name, in_specs=[STRICT_HBM_SPEC] * n + [SEM_SPEC, SEM_SPEC, HBM_SPEC],
        out_shape=tuple(pltpu.HBM(a.shape, a.dtype) for a in flying), out_specs=tuple([STRICT_HBM_SPEC] * n),
        input_output_aliases={w: w for w in range(n)},
        compiler_params=pltpu.CompilerParams(has_side_effects=EFFECT))(*flying, send, recv, after)


def _gather_small(packed, name):
    def body(p_ref, out, send, recv, local):
        x, y, c, _ = _place()
        me = 4 * x + 2 * y + c
        own = pltpu.make_async_copy(p_ref, out.at[me], local)
        own.start()
        cps = []
        for k in range(1, N_DEV):
            fx, fy, fc = (k >> 2) & 1, (k >> 1) & 1, k & 1
            peer = ((x + fx) % 2, (y + fy) % 2, (c + fc) % 2)
            cps.append(pltpu.make_async_remote_copy(src_ref=p_ref, dst_ref=out.at[me], send_sem=send.at[k - 1],
                                                    recv_sem=recv.at[k - 1], device_id=peer, device_id_type=MESH))
        for cp in cps:
            cp.start()
        for k in range(1, N_DEV):
            fx, fy, fc = (k >> 2) & 1, (k >> 1) & 1, k & 1
            src = out.at[4 * ((x + fx) % 2) + 2 * ((y + fy) % 2) + (c + fc) % 2]
            pltpu.make_async_remote_copy(src_ref=src, dst_ref=src, send_sem=send.at[k - 1], recv_sem=recv.at[k - 1],
                                         device_id=(x, y, c), device_id_type=MESH).wait_recv()
        for cp in cps:
            cp.wait_send()
        own.wait()

    return pl.pallas_call(
        body, name=name, in_specs=[pl.BlockSpec(memory_space=pltpu.VMEM)], out_specs=HBM_SPEC,
        out_shape=_sds((N_DEV, *packed.shape), F32),
        scratch_shapes=[pltpu.SemaphoreType.DMA((N_DEV - 1,)), pltpu.SemaphoreType.DMA((N_DEV - 1,)),
                        pltpu.SemaphoreType.DMA])(packed)


def _sum_devices(gathered, name):
    _, R, C = gathered.shape

    def body(g_ref, o_ref):
        acc = g_ref[0]
        for d in range(1, N_DEV):
            acc = acc + g_ref[d]
        o_ref[...] = acc

    tr = 8
    return pl.pallas_call(
        body, name=name, grid=(R // tr,), in_specs=[pl.BlockSpec((N_DEV, tr, C), lambda i: (0, i, 0))],
        out_specs=pl.BlockSpec((tr, C), lambda i: (i, 0)), out_shape=_sds((R, C), F32),
        compiler_params=_cparams(1))(gathered)


def _gather_start(packed, name):
    def body(src, land, send, recv, *rest):
        x, y, c, _ = _place()
        for k, (peer, _, _, _) in enumerate(_peers(x, y, c)):
            pltpu.make_async_remote_copy(src_ref=src, dst_ref=land.at[4 * x + 2 * y + c], send_sem=send.at[k],
                                         recv_sem=recv.at[k], device_id=peer, device_id_type=MESH).start()
        rest[-1][...] = jnp.zeros_like(rest[-1])

    land = lax.empty((N_DEV, *packed.shape), F32)
    sems = pltpu.SemaphoreType.DMA((N_DEV - 1,))
    return pl.pallas_call(
        body, name=name, in_specs=[STRICT_HBM_SPEC] * 2,
        out_shape=(sems, sems, pltpu.HBM(packed.shape, F32), pltpu.HBM(land.shape, F32), _sds((8, 128), F32)),
        out_specs=(SEM_SPEC, SEM_SPEC, STRICT_HBM_SPEC, STRICT_HBM_SPEC, _token_spec()),
        input_output_aliases={0: 2, 1: 3},
        compiler_params=pltpu.CompilerParams(has_side_effects=EFFECT))(_hbm(packed), _hbm(land))


def _gather_wait(started, after, name):
    n = len(started)

    def body(*refs):
        x, y, c, _ = _place()
        for s in range(n):
            src, land, send, recv = refs[2 * s], refs[2 * s + 1], refs[2 * n + 2 * s], refs[2 * n + 2 * s + 1]
            for k, (peer, _, _, p_dev) in enumerate(_peers(x, y, c)):
                cp = pltpu.make_async_remote_copy(src_ref=src, dst_ref=land.at[p_dev], send_sem=send.at[k],
                                                  recv_sem=recv.at[k], device_id=peer,
                                                  device_id_type=MESH)
                cp.wait_send()
                cp.wait_recv()

    arrays = [a for _, _, packed, land in started for a in (packed, land)]
    sems = [s for send, recv, _, _ in started for s in (send, recv)]
    out = pl.pallas_call(
        body, name=name, in_specs=[STRICT_HBM_SPEC] * (2 * n) + [SEM_SPEC] * (2 * n) + [HBM_SPEC],
        out_shape=tuple(pltpu.HBM(a.shape, a.dtype) for a in arrays), out_specs=tuple([STRICT_HBM_SPEC] * (2 * n)),
        input_output_aliases={a: a for a in range(2 * n)},
        compiler_params=pltpu.CompilerParams(has_side_effects=EFFECT))(*arrays, *sems, after)
    return [(out[2 * s], out[2 * s + 1]) for s in range(n)]


def _sum_gathered(packed, land, device, name):
    R, C = packed.shape
    tr = 8

    def body(dev_ref, own_ref, *rest):
        me = dev_ref[0]
        acc = None
        for d in range(N_DEV):
            term = jnp.where(me == d, own_ref[...], rest[d][...])
            acc = term if acc is None else acc + term
        rest[-1][...] = acc

    def slab(d):
        return pl.BlockSpec((None, tr, C), lambda i, dev_ref: (jnp.where(dev_ref[0] == d, (d + 1) % N_DEV, d), i, 0))

    grid_spec = pltpu.PrefetchScalarGridSpec(
        num_scalar_prefetch=1, grid=(R // tr,),
        in_specs=[pl.BlockSpec((tr, C), lambda i, dev_ref: (i, 0))] + [slab(d) for d in range(N_DEV)],
        out_specs=pl.BlockSpec((tr, C), lambda i, dev_ref: (i, 0)))
    return pl.pallas_call(body, name=name, grid_spec=grid_spec, out_shape=_sds((R, C), F32),
                          compiler_params=_cparams(1))(device, packed, *[land] * N_DEV)


def _pack_small(arrays):
    rows = []
    for a in arrays:
        flat = a.reshape(-1)
        pad = (-flat.shape[0]) % SMALL_COLS
        rows.append(jnp.pad(flat, (0, pad)).reshape(-1, SMALL_COLS))
    packed = jnp.concatenate(rows, axis=0)
    return jnp.pad(packed, ((0, (-packed.shape[0]) % 8), (0, 0)))


def _unpack_small(packed, shapes):
    out, row = [], 0
    for shape in shapes:
        size = math.prod(shape)
        n_rows = -(-size // SMALL_COLS)
        out.append(packed[row:row + n_rows].reshape(-1)[:size].reshape(shape))
        row += n_rows
    return out


def kernel(x, a_norm, a_w_in, a_sgu_norm, a_w_spatial, a_b_spatial, a_w_out, kv_norm, w_kv, b_norm, b_w_q, b_rel_bias, b_w_o, ffn_norm, ffn_w_gate_up, ffn_w_down, final_norm, loss_target, m_a_norm, m_a_w_in, m_a_sgu_norm, m_a_w_spatial, m_a_b_spatial, m_a_w_out, m_kv_norm, m_w_kv, m_b_norm, m_b_w_q, m_b_rel_bias, m_b_w_o, m_ffn_norm, m_ffn_w_gate_up, m_ffn_w_down, m_final_norm, v_a_norm, v_a_w_in, v_a_sgu_norm, v_a_w_spatial, v_a_b_spatial, v_a_w_out, v_kv_norm, v_w_kv, v_b_norm, v_b_w_q, v_b_rel_bias, v_b_w_o, v_ffn_norm, v_ffn_w_gate_up, v_ffn_w_down, v_final_norm):
    S, D = x.shape[1], x.shape[2]
    n_a = a_w_in.shape[0]
    n_b = b_w_q.shape[0]
    depth = ffn_w_gate_up.shape[0]
    xi, yi, ci = lax.axis_index("x"), lax.axis_index("y"), lax.axis_index("c")
    chip = 2 * xi + yi

    place = jnp.stack([chip, ci, 2 * chip + ci]).astype(jnp.int32)
    stacked = {"a_w_in": a_w_in, "a_w_out": a_w_out, "w_kv": w_kv[None], "b_w_q": b_w_q, "b_w_o": b_w_o,
               "ffn_w_gate_up": ffn_w_gate_up, "ffn_w_down": ffn_w_down}
    groups = []
    for layer in range(depth):
        if layer == 0 and n_a > 0:
            groups += [[("a_w_in", 0)], [("a_w_out", 0)]]
        elif layer < n_a:
            groups.append([("a_w_in", layer), ("a_w_out", layer)])
        elif layer == n_a:
            groups.append([("w_kv", 0), ("b_w_q", 0), ("b_w_o", 0)])
        else:
            groups.append([("b_w_q", layer - n_a), ("b_w_o", layer - n_a)])
        groups.append([("ffn_w_gate_up", layer), ("ffn_w_down", layer)])
    units = [u for group in groups for u in group]
    n_early = len(groups[0])
    slabs = [_cast_slab(stacked[k], l, place[:1], f"cast_{k}_{l}") for k, l in units[:n_early]]
    early = _allgather_start(slabs, "allgather_start_first")
    slabs = [_cast_slab(stacked[k], l, place[:1], f"cast_{k}_{l}", deps=(early[3],)) for k, l in units[n_early:]]
    late = _allgather_start(slabs, "allgather_start_rest")
    na_w, ns_w = a_norm.shape[1], a_sgu_norm.shape[1]
    small_g = _allgather_small(jnp.concatenate([a_norm, a_sgu_norm], axis=1), "allgather_small")
    a_norm_f = small_g[:, :, :na_w].transpose(1, 0, 2).reshape(n_a, N_CHIPS * na_w)
    a_sgu_f = small_g[:, :, na_w:].transpose(1, 0, 2).reshape(n_a, N_CHIPS * ns_w)
    W, relayed = {}, []

    def relay(after):
        if len(relayed) == len(groups):
            return ()
        index = sum(len(g) for g in groups[:len(relayed)])
        group = groups[len(relayed)]
        (send, recv, flying, _), first = (early, index) if index < n_early else (late, index - n_early)
        relayed.append(_allgather_relay(flying[first:first + len(group)], send, recv, first, after,
                                        f"allgather_relay_{len(relayed)}"))
        return (relayed[-1][3],)

    n_gathered = [0]

    def gathered(after):
        index = n_gathered[0]
        send2, recv2, arrays, _ = relayed[index]
        W.update(zip(groups[index], _allgather_wait(arrays, send2, recv2, after, f"allgather_wait_{index}")))
        n_gathered[0] += 1

    xc = x.reshape(S, D)
    saved = []
    kvp = x_kv = h_kv = None
    relay(late[3])
    order = ()
    for layer in range(depth):
        rec = {"x_in": xc}
        gathered(xc)
        if layer < n_a:
            i = layer
            rec["zpre"], rec["h"] = _norm_matmul(xc, a_norm_f[i][None], W["a_w_in", i], BF16, f"a{i}_in", deps=order)
            order = relay(rec["h"])
            rec["uv"] = _sgu_fwd(rec["zpre"], a_sgu_f[i][None], a_w_spatial[i], a_b_spatial[i].T, f"a{i}_sgu",
                                 deps=order)
            order = ()
            if ("a_w_out", i) not in W:
                gathered(rec["uv"])
                order = relay(rec["uv"])
            xm = _matmul_res(rec["uv"], W["a_w_out", i], xc, f"a{i}_out", tm=512, deps=order)
        else:
            i = layer - n_a
            if i == 0:
                kvp, h_kv = _norm_matmul(xc, kv_norm[None], W["w_kv", 0], BF16, "kv_proj", tm=KV_PAD, zero_rows=KV_PAD)
                x_kv = xc
            rec["q"], rec["h"] = _norm_matmul(xc, b_norm[i][None], W["b_w_q", i], BF16, f"b{i}_q", row_sharded=True)
            order = relay(rec["h"])
            table = jnp.pad(b_rel_bias[i], ((0, 0), (0, TABLE_PAD - b_rel_bias.shape[2])))
            rec["bias"] = _rel_bias_fwd(table, f"b{i}_bias")
            rec["o"] = _attn_fwd(rec["q"], kvp, rec["bias"], f"b{i}_attn", deps=order)
            xm = _matmul_res(rec["o"], W["b_w_o", i], xc, f"b{i}_o", tm=512)
        rec["x_mid"] = xm
        gathered(xm)
        rec["gu"], rec["h_f"] = _norm_matmul(xm, ffn_norm[layer][None], W["ffn_w_gate_up", layer], BF16, f"f{layer}_in")
        order = relay(rec["h_f"])
        xc = _matmul_res(rec["gu"], W["ffn_w_down", layer], xm, f"f{layer}_out", swiglu=True, deps=order, tm=512)
        order = ()
        saved.append(rec)

    loss_tile, dx, dxb, d_final = _loss_head(xc, final_norm[None], loss_target.reshape(S, D), "loss_head")

    started = []
    small_started = []
    pending = []

    held = []

    def weight_grad(unit, hold=False, **kw):
        full = (N_CHIPS,) + tuple(stacked[unit[0]].shape[1:])
        g = _matmul_tn(out_shape=full, name=f"d_{unit[0]}_{unit[1]}", deps=tuple(pending), **kw)
        pending.clear()
        held.append((unit, g))
        if hold:
            return ()
        send, recv, flying_g, flying_land, token = _reduce_start([g for _, g in held],
                                                                 f"reduce_start_{unit[0]}_{unit[1]}")
        started.append(([u for u, _ in held], send, recv, flying_g, flying_land))
        held.clear()
        return (token,)

    tt = min(1024, S)
    tw = min(2048, S)
    ts = S
    tb = min(NORMBWD_ROWS, S)
    row_a = lambda w, rows=tw: pl.BlockSpec((rows, w), lambda o, t: (t, 0))
    d_ffn_norm, d_b_norm, d_a_norm, d_a_sgu = [None] * depth, [None] * n_b, [None] * n_a, [None] * n_a
    d_ws, d_bs, d_rel = [None] * n_a, [None] * n_a, [None] * n_b
    dkv = None
    first = lambda ref: ref[...]
    for layer in reversed(range(depth)):
        rec = saved[layer]
        r_d = ffn_w_down.shape[1]
        half_f = 2 * r_d
        token = weight_grad(
            ("ffn_w_down", layer), a_ops=[rec["gu"], rec["gu"]],
            a_specs=[pl.BlockSpec((tt, half_f), lambda o, t: (t, o)), pl.BlockSpec((tt, half_f), lambda o, t: (t, 2 + o))],
            a_fn=lambda g_ref, u_ref: _swiglu(g_ref[...].astype(F32), u_ref[...].astype(F32)).astype(BF16),
            b_op=dxb, b_spec=row_a(D, tt), out_spec=pl.BlockSpec((2, r_d, D), lambda o, t: (o, 0, 0)),
            acc_shape=(half_f, D), n_outer=2, tt=tt, hold=True)
        dgu = _nt_swiglu_bwd(dxb, W["ffn_w_down", layer], rec["gu"], f"f{layer}_dgu", deps=token)
        nsh = ffn_w_gate_up.shape[2]
        token = weight_grad(
            ("ffn_w_gate_up", layer), a_ops=[rec["h_f"]], a_specs=[row_a(D)], a_fn=first,
            b_op=dgu, b_spec=pl.BlockSpec((None, tw, nsh), lambda o, t: (o // 2, t, o % 2)),
            out_spec=pl.BlockSpec((None, D, nsh), lambda o, t: (o, 0, 0)), acc_shape=(D, nsh), n_outer=N_CHIPS, tt=tw)
        dx, dxb, d_ffn_norm[layer] = _nt_normbwd(
            dgu, pl.BlockSpec((None, tb, nsh), lambda i, k: (k // 2, i, k % 2)),
            W["ffn_w_gate_up", layer], pl.BlockSpec((None, D, nsh), lambda i, k: (k, 0, 0)),
            (D, nsh), N_CHIPS, rec["x_mid"], ffn_norm[layer][None], dx, f"f{layer}_dx", deps=token)
        if layer >= n_a:
            i = layer - n_a
            r_o = b_w_o.shape[1]
            token = weight_grad(
                ("b_w_o", i), a_ops=[rec["o"]], a_specs=[row_a(D, ts)], a_fn=first, b_op=dxb, b_spec=row_a(D, ts),
                out_spec=pl.BlockSpec((N_CHIPS, r_o, D), lambda o, t: (0, 0, 0)), acc_shape=(D, D), n_outer=1, tt=ts,
                hold=True)
            do = _nt_rows(dxb, W["b_w_o", i], N_CHIPS, BF16, f"b{i}_do", deps=token)
            dq, dkv, dbias, *dkv_bf16 = _attn_bwd(rec["q"], kvp, rec["bias"], do, dkv, f"b{i}_attn_bwd")
            d_rel[i] = _rel_bias_bwd(dbias, f"b{i}_dbias")[:, :b_rel_bias.shape[2]]
            token = weight_grad(
                ("b_w_q", i), a_ops=[rec["h"]], a_specs=[row_a(D, ts)], a_fn=first, b_op=dq, b_spec=row_a(D, ts),
                out_spec=pl.BlockSpec((N_CHIPS, r_o, D), lambda o, t: (0, 0, 0)), acc_shape=(D, D), n_outer=1, tt=ts)
            dx, dxb, d_b_norm[i] = _nt_normbwd(
                dq, pl.BlockSpec((tb, D), lambda i_, k: (i_, 0)),
                W["b_w_q", i], pl.BlockSpec((N_CHIPS, r_o, D), lambda i_, k: (0, 0, 0)),
                (D, D), 1, rec["x_in"], b_norm[i][None], dx, f"b{i}_dx", deps=token)
            if i == 0:
                dkv_b = dkv_bf16[0] if dkv_bf16 else dkv[:, KV_PAD:, :].astype(BF16)
                n_kv = w_kv.shape[1]
                token = weight_grad(
                    ("w_kv", 0), a_ops=[h_kv], a_specs=[row_a(D, ts)], a_fn=first,
                    b_op=dkv_b, b_spec=pl.BlockSpec((None, ts, n_kv), lambda o, t: (o // 2, t, o % 2)),
                    out_spec=pl.BlockSpec((None, D, n_kv), lambda o, t: (o, 0, 0)), acc_shape=(D, n_kv),
                    n_outer=N_CHIPS, tt=ts)
                dx, dxb, d_kv_norm = _nt_normbwd(
                    dkv_b, pl.BlockSpec((None, tb, n_kv), lambda i_, k: (k // 2, i_, k % 2)),
                    W["w_kv", 0], pl.BlockSpec((None, D, n_kv), lambda i_, k: (k, 0, 0)),
                    (D, n_kv), N_CHIPS, x_kv, kv_norm[None], dx, "kv_dx", deps=token)
        else:
            i = layer
            r_w = a_w_out.shape[1]
            token = weight_grad(
                ("a_w_out", i), a_ops=[rec["uv"]], a_specs=[row_a(N_CHIPS * r_w, tw // 2)], a_fn=first,
                b_op=dxb, b_spec=row_a(D, tw // 2), out_spec=pl.BlockSpec((N_CHIPS, r_w, D), lambda o, t: (0, 0, 0)),
                acc_shape=(N_CHIPS * r_w, D), n_outer=1, tt=tw // 2, hold=i > 0)
            duv = _nt_rows(dxb, W["a_w_out", i], 2, BF16, f"a{i}_duv", deps=token)
            dz, d_a_sgu[i], d_ws[i], dbs = _sgu_bwd(rec["zpre"], duv, a_sgu_f[i][None], a_w_spatial[i],
                                                  a_w_spatial[i].transpose(0, 2, 1), a_b_spatial[i].T, f"a{i}_sgu_bwd")
            d_bs[i] = dbs[:, :, 0]
            if i == 0:
                batch = [d_a_sgu[0], d_ws[0][None], d_bs[0][None], d_ffn_norm[0]]
                small_started.append((batch, _gather_start(_pack_small(batch), "gather_start_late")))
                pending.append(small_started[-1][1][4])
            n_in = a_w_in.shape[2]
            token = weight_grad(
                ("a_w_in", i), a_ops=[rec["h"]], a_specs=[row_a(D, ts)], a_fn=first,
                b_op=dz, b_spec=pl.BlockSpec((ts, n_in), lambda o, t: (t, o)),
                out_spec=pl.BlockSpec((None, D, n_in), lambda o, t: (o, 0, 0)), acc_shape=(D, n_in), n_outer=N_CHIPS,
                tt=ts)
            dx, dxb, d_a_norm[i] = _nt_normbwd(
                dz, pl.BlockSpec((tb, n_in), lambda i_, k: (i_, k)),
                W["a_w_in", i], pl.BlockSpec((None, D, n_in), lambda i_, k: (k, 0, 0)),
                (D, n_in), N_CHIPS, rec["x_in"], a_norm_f[i][None], dx, f"a{i}_dx", deps=token)
        if layer == 1:
            batch = [jnp.concatenate(d_a_norm[1:], axis=0), jnp.concatenate(d_a_sgu[1:], axis=0), jnp.stack(d_ws[1:]),
                     jnp.stack(d_bs[1:]), d_kv_norm, jnp.concatenate(d_b_norm, axis=0), jnp.stack(d_rel),
                     jnp.concatenate(d_ffn_norm[1:], axis=0), d_final, loss_tile[:1, :1]]
            small_started.append((batch, _gather_start(_pack_small(batch), "gather_start_early")))
            pending.append(small_started[-1][1][4])
    grad_x = dx.reshape(x.shape)

    landed = _reduce_wait([(send, recv, g, land) for _, send, recv, g, land in started], dx, "reduce_wait")
    reduced_units = [unit for units_, *_ in started for unit in units_]
    arrived = dict(zip(reduced_units, [pair for gs, lands in landed for pair in zip(gs, lands)]))
    ffn_units = [u for u in reduced_units if u[0].startswith("ffn_")]
    other_units = [u for u in reduced_units if not u[0].startswith("ffn_")]
    halves = [_reduce_sum(*arrived[u], place, f"reduce_sum_{u[0]}_{u[1]}") for u in ffn_units]
    ffn_send, ffn_recv, ffn_flying, ffn_token = _join_start(halves, "join_start_ffn")
    halves = [_reduce_sum(*arrived[u], place, f"reduce_sum_{u[0]}_{u[1]}", deps=(ffn_token,)) for u in other_units]
    other_send, other_recv, other_flying, other_token = _join_start(halves, "join_start_rest")
    joined = dict(zip(ffn_units, _join_wait(ffn_flying, ffn_send, ffn_recv, other_token, "join_wait_ffn")))
    reduced = {}

    (packed_e, land_e), (packed_l, land_l) = _gather_wait([s[:4] for _, s in small_started], dx, "gather_wait")
    total_e = _sum_gathered(packed_e, land_e, place[2:], "sum_small_grads_early")
    total_l = _sum_gathered(packed_l, land_l, place[2:], "sum_small_grads_late")
    total_t = _sum_devices(_gather_small(_pack_small([d_a_norm[0]]), "gather_small_grads_last"), "sum_small_grads_last")
    (e_a_norm, e_a_sgu, e_ws, e_bs, g_kv_norm, g_b_norm, g_rel, e_ffn_norm, g_final, loss) = _unpack_small(
        total_e, [a.shape for a in small_started[0][0]])
    l_a_sgu, l_ws, l_bs, l_ffn_norm = _unpack_small(total_l, [a.shape for a in small_started[1][0]])
    (t_a_norm,) = _unpack_small(total_t, [d_a_norm[0].shape])
    g_a_norm = jnp.concatenate([t_a_norm, e_a_norm], axis=0)
    g_a_sgu = jnp.concatenate([l_a_sgu, e_a_sgu], axis=0)
    g_ws = jnp.concatenate([l_ws, e_ws], axis=0)
    g_bs = jnp.concatenate([l_bs, e_bs], axis=0)
    g_ffn_norm = jnp.concatenate([l_ffn_norm, e_ffn_norm], axis=0)
    reduced["a_norm"] = lax.dynamic_slice_in_dim(g_a_norm, chip * na_w, na_w, axis=1)
    reduced["a_sgu_norm"] = lax.dynamic_slice_in_dim(g_a_sgu, chip * ns_w, ns_w, axis=1)
    reduced.update(a_w_spatial=g_ws, a_b_spatial=g_bs, kv_norm=g_kv_norm.reshape(kv_norm.shape), b_norm=g_b_norm,
                   b_rel_bias=g_rel, ffn_norm=g_ffn_norm, final_norm=g_final.reshape(final_norm.shape))

    weights = dict(a_norm=a_norm, a_w_in=a_w_in, a_sgu_norm=a_sgu_norm, a_w_spatial=a_w_spatial,
                   a_b_spatial=a_b_spatial, a_w_out=a_w_out, kv_norm=kv_norm, w_kv=w_kv, b_norm=b_norm, b_w_q=b_w_q,
                   b_rel_bias=b_rel_bias, b_w_o=b_w_o, ffn_norm=ffn_norm, ffn_w_gate_up=ffn_w_gate_up,
                   ffn_w_down=ffn_w_down, final_norm=final_norm)
    m_in = dict(a_norm=m_a_norm, a_w_in=m_a_w_in, a_sgu_norm=m_a_sgu_norm, a_w_spatial=m_a_w_spatial,
                a_b_spatial=m_a_b_spatial, a_w_out=m_a_w_out, kv_norm=m_kv_norm, w_kv=m_w_kv, b_norm=m_b_norm,
                b_w_q=m_b_w_q, b_rel_bias=m_b_rel_bias, b_w_o=m_b_w_o, ffn_norm=m_ffn_norm,
                ffn_w_gate_up=m_ffn_w_gate_up, ffn_w_down=m_ffn_w_down, final_norm=m_final_norm)
    v_in = dict(a_norm=v_a_norm, a_w_in=v_a_w_in, a_sgu_norm=v_a_sgu_norm, a_w_spatial=v_a_w_spatial,
                a_b_spatial=v_a_b_spatial, a_w_out=v_a_w_out, kv_norm=v_kv_norm, w_kv=v_w_kv, b_norm=v_b_norm,
                b_w_q=v_b_w_q, b_rel_bias=v_b_rel_bias, b_w_o=v_b_w_o, ffn_norm=v_ffn_norm,
                ffn_w_gate_up=v_ffn_w_gate_up, ffn_w_down=v_ffn_w_down, final_norm=v_final_norm)
    results = {}

    def adamw_large(key):
        as_layers = lambda a: a.reshape(stacked[key].shape)
        results[key] = _adamw_stacked(
            as_layers(weights[key]), [joined[key, layer] for layer in range(stacked[key].shape[0])],
            as_layers(m_in[key]), as_layers(v_in[key]), "adamw_" + key)

    for key in ("ffn_w_gate_up", "ffn_w_down"):
        adamw_large(key)
    joined.update(zip(other_units, _join_wait(other_flying, other_send, other_recv, results["ffn_w_down"][1],
                                              "join_wait_rest")))
    for key, w in weights.items():
        if key in stacked:
            if key not in results:
                adamw_large(key)
        else:
            g = reduced[key].reshape(w.shape)
            view = (1, w.shape[0]) if w.ndim == 1 else (-1, w.shape[-1])
            d, nm, nv = _adamw(w.reshape(view), g.reshape(view), m_in[key].reshape(view), v_in[key].reshape(view),
                               "adamw_" + key)
            results[key] = (g, d, nm, nv)
    outs = [[results[key][k].reshape(w.shape) for key, w in weights.items()] for k in range(4)]
    return (loss.reshape(()), grad_x, *outs[0], *outs[1], *outs[2], *outs[3])
```

```python
import math

import jax
import jax.numpy as jnp
from jax import lax
from jax.experimental import pallas as pl
from jax.experimental.pallas import tpu as pltpu

F32, BF16 = jnp.float32, jnp.bfloat16
MESH = pl.DeviceIdType.MESH
HIGHEST = lax.Precision.HIGHEST
NT_DIMS = (((1,), (1,)), ((), ()))
TN_DIMS = (((0,), (0,)), ((), ()))

EPS = 1e-6
CHUNK = 64
A_CHUNK = 128
A_GROUPS = 8
N_HEADS = 16
HEAD_DIM = 64
N_LEFT = 8
MAX_REL = 256
ATTN_SCALE = HEAD_DIM ** -0.5
NEG_INF = -1e30
Q_BLOCK = 4 * CHUNK
KV_PAD = N_LEFT * CHUNK
BAND = KV_PAD + Q_BLOCK
DIAGS = BAND + Q_BLOCK
TABLE_PAD = 640
HEADS_PER_BLOCK = 2
BLOCKS_PER_STEP = 8
NORMBWD_ROWS = 1024

ADAM_LR, ADAM_B1, ADAM_B2, ADAM_EPS, ADAM_WD, ADAM_STEP = 0.001, 0.9, 0.999, 1e-08, 0.01, 10

VMEM_LIMIT_BYTES = 56 * 1024 * 1024
N_CHIPS = 4
N_DEV = 8
SMALL_COLS = 1024


def _cparams(n_grid):
    return pltpu.CompilerParams(dimension_semantics=("arbitrary",) * n_grid, vmem_limit_bytes=VMEM_LIMIT_BYTES)


def _sds(shape, dtype):
    return jax.ShapeDtypeStruct(tuple(shape), dtype)


def _gelu(x):
    return x * (0.5 * (1.0 + lax.erf(x * math.sqrt(0.5))))


def _gelu_and_grad(x):
    cdf = 0.5 * (1.0 + lax.erf(x * math.sqrt(0.5)))
    return x * cdf, cdf + x * (jnp.exp(-0.5 * x * x) * (1.0 / math.sqrt(2.0 * math.pi)))


def _rms_hat(xv):
    r = lax.rsqrt(jnp.mean(xv * xv, axis=-1, keepdims=True) + EPS)
    return xv * r, r


def _rms_bwd(xhat, r, g, dy):
    dxhat = dy * g
    dx = r * (dxhat - xhat * jnp.mean(dxhat * xhat, axis=-1, keepdims=True))
    return dx, dy * xhat


def _swiglu(gate, up):
    return (gate * jax.nn.sigmoid(gate)) * up


def _row_tile(rows, cols, itemsize, cap_bytes, align):
    t = rows
    while t * cols * itemsize > cap_bytes and t % (2 * align) == 0:
        t //= 2
    return t


def _cast_slab(w, layer, chip, name, deps=()):
    _, r, C = w.shape
    tr = _row_tile(r, C, 4, 4 * 1024 * 1024, 16)

    def body(chip_ref, w_ref, *rest):
        del chip_ref
        rest[-1][...] = w_ref[...].astype(BF16)

    grid_spec = pltpu.PrefetchScalarGridSpec(
        num_scalar_prefetch=1, grid=(r // tr,),
        in_specs=[pl.BlockSpec((None, tr, C), lambda i, chip_ref: (layer, i, 0))] + [HBM_SPEC] * len(deps),
        out_specs=pl.BlockSpec((None, tr, C), lambda i, chip_ref: (chip_ref[0], i, 0)))
    return pl.pallas_call(body, name=name, grid_spec=grid_spec, out_shape=_sds((N_CHIPS, r, C), BF16),
                          compiler_params=_cparams(1))(chip, w, *deps)


def _adamw_stacked(w, gs, m, v, name):
    L, r, C = w.shape
    tr = _row_tile(r, C, 4, 2 * 1024 * 1024, 8)
    nb = r // tr

    def body(w_ref, m_ref, v_ref, *rest):
        go_ref, d_ref, nm_ref, nv_ref = rest[-4:]
        layer = pl.program_id(0)
        gv = rest[0][...]
        for k in range(1, L):
            gv = jnp.where(layer == k, rest[k][...], gv)
        mn = ADAM_B1 * m_ref[...] + (1.0 - ADAM_B1) * gv
        vn = ADAM_B2 * v_ref[...] + (1.0 - ADAM_B2) * jnp.square(gv)
        m_hat = mn / (1.0 - ADAM_B1 ** ADAM_STEP)
        v_hat = vn / (1.0 - ADAM_B2 ** ADAM_STEP)
        d_ref[...] = -ADAM_LR * (m_hat / (jnp.sqrt(v_hat) + ADAM_EPS) + ADAM_WD * w_ref[...])
        nm_ref[...] = mn
        nv_ref[...] = vn
        go_ref[...] = gv

    def grad_spec(k):
        return pl.BlockSpec((tr, C), lambda l, i: (jnp.where(l == k, i, jnp.where(l > k, nb - 1, 0)), 0))

    stacked = pl.BlockSpec((None, tr, C), lambda l, i: (l, i, 0))
    return pl.pallas_call(body, name=name, grid=(L, nb), in_specs=[stacked] * 3 + [grad_spec(k) for k in range(L)],
                          out_specs=[stacked] * 4, out_shape=[_sds((L, r, C), F32)] * 4,
                          compiler_params=_cparams(2))(w, m, v, *gs)


def _adamw(w, g, m, v, name):
    R, C = w.shape
    tr = _row_tile(R, C, 4, 1024 * 1024, 8)

    def body(w_ref, g_ref, m_ref, v_ref, d_ref, nm_ref, nv_ref):
        gv = g_ref[...]
        mn = ADAM_B1 * m_ref[...] + (1.0 - ADAM_B1) * gv
        vn = ADAM_B2 * v_ref[...] + (1.0 - ADAM_B2) * jnp.square(gv)
        m_hat = mn / (1.0 - ADAM_B1 ** ADAM_STEP)
        v_hat = vn / (1.0 - ADAM_B2 ** ADAM_STEP)
        d_ref[...] = -ADAM_LR * (m_hat / (jnp.sqrt(v_hat) + ADAM_EPS) + ADAM_WD * w_ref[...])
        nm_ref[...] = mn
        nv_ref[...] = vn

    spec = pl.BlockSpec((tr, C), lambda i: (i, 0))
    return pl.pallas_call(body, name=name, grid=(R // tr,), in_specs=[spec] * 4, out_specs=[spec] * 3,
                          out_shape=[_sds((R, C), F32)] * 3, compiler_params=_cparams(1))(w, g, m, v)


def _norm_matmul(x, g, w_g, out_dtype, name, row_sharded=False, deps=(), tm=1024, zero_rows=0):
    S, D = x.shape
    tm = min(tm, S)
    lead = zero_rows // tm
    if row_sharded:
        r, N = w_g.shape[1], w_g.shape[2]
        tn = 512
        w_spec = pl.BlockSpec((N_CHIPS, r, tn), lambda i, j: (0, 0, j))
    else:
        nsh = w_g.shape[2]
        N = N_CHIPS * nsh
        tn = next((t for t in (1024, 512) if nsh % t == 0), nsh)
        bps = nsh // tn
        w_spec = pl.BlockSpec((None, D, tn), lambda i, j: (j // bps, 0, j % bps))
    whole = not row_sharded and N <= 2048
    if whole:
        tn = N
        w_spec = pl.BlockSpec((N_CHIPS, D, nsh), lambda i, j: (0, 0, 0))

    def body(x_ref, g_ref, w_ref, *rest):
        y_ref, h_ref = rest[-2:]
        i = pl.program_id(0)

        @pl.when((i >= lead) & (pl.program_id(1) == 0))
        def _():
            xhat, _ = _rms_hat(x_ref[...])
            h_ref[...] = (xhat * g_ref[...]).astype(BF16)

        @pl.when(i >= lead)
        def _():
            if whole:
                for s in range(N_CHIPS):
                    y_ref[:, s * nsh:(s + 1) * nsh] = jnp.dot(h_ref[...], w_ref[s],
                                                             preferred_element_type=F32).astype(y_ref.dtype)
            else:
                w = w_ref[...].reshape(D, tn)
                y_ref[...] = jnp.dot(h_ref[...], w, preferred_element_type=F32).astype(y_ref.dtype)

        if lead:
            @pl.when(i < lead)
            def _():
                y_ref[...] = jnp.zeros_like(y_ref)

    rows = lambda i, j: (jnp.maximum(i - lead, 0), 0)
    return pl.pallas_call(
        body, name=name, grid=(lead + S // tm, N // tn),
        in_specs=[pl.BlockSpec((tm, D), rows), pl.BlockSpec((1, D), lambda i, j: (0, 0)), w_spec]
        + [HBM_SPEC] * len(deps),
        out_specs=[pl.BlockSpec((tm, tn), lambda i, j: (i, j)), pl.BlockSpec((tm, D), rows)],
        out_shape=[_sds((zero_rows + S, N), out_dtype), _sds((S, D), BF16)],
        compiler_params=_cparams(2))(x, g, w_g, *deps)


def _matmul_res(a, w_g, res, name, swiglu=False, deps=(), tm=256):
    S, N = res.shape
    r = w_g.shape[1]
    K = N_CHIPS * r

    def body(*refs):
        o_ref = refs[-1]
        if swiglu:
            gate_ref, up_ref, w_ref, res_ref = refs[:4]
            a_blk = _swiglu(gate_ref[...].astype(F32), up_ref[...].astype(F32)).astype(BF16)
        else:
            a_ref, w_ref, res_ref = refs[:3]
            a_blk = a_ref[...]
        o_ref[...] = res_ref[...] + jnp.dot(a_blk, w_ref[...].reshape(K, N), preferred_element_type=F32)

    a_specs, a_ops = [pl.BlockSpec((tm, K), lambda i: (i, 0))], [a]
    if swiglu:
        a_specs.append(pl.BlockSpec((tm, K), lambda i: (i, 1)))
        a_ops.append(a)
    row = pl.BlockSpec((tm, N), lambda i: (i, 0))
    return pl.pallas_call(
        body, name=name, grid=(S // tm,),
        in_specs=a_specs + [pl.BlockSpec((N_CHIPS, r, N), lambda i: (0, 0, 0)), row] + [HBM_SPEC] * len(deps),
        out_specs=row, out_shape=_sds((S, N), F32), compiler_params=_cparams(1))(*a_ops, w_g, res, *deps)


def _matmul_tn(a_ops, a_specs, a_fn, b_op, b_spec, out_spec, out_shape, acc_shape, n_outer, name, deps=(), tt=512):
    S = b_op.shape[-2]
    na = len(a_ops)
    nt = S // tt

    def body(*refs):
        a_refs, b_ref, o_ref, acc_ref = refs[:na], refs[na], refs[-2], refs[-1]
        t = pl.program_id(1)
        part = lax.dot_general(a_fn(*a_refs), b_ref[...].astype(BF16), TN_DIMS, preferred_element_type=F32)

        @pl.when(t == 0)
        def _():
            acc_ref[...] = part

        @pl.when(t > 0)
        def _():
            acc_ref[...] += part

        @pl.when(t == nt - 1)
        def _():
            o_ref[...] = acc_ref[...].reshape(o_ref.shape).astype(BF16)

    return pl.pallas_call(
        body, name=name, grid=(n_outer, nt), in_specs=list(a_specs) + [b_spec] + [HBM_SPEC] * len(deps),
        out_specs=out_spec, out_shape=_sds(out_shape, BF16), scratch_shapes=[pltpu.VMEM(acc_shape, F32)],
        compiler_params=_cparams(2))(*a_ops, b_op, *deps)


def _nt_accumulate(a_ref, w_ref, acc_ref, w2d, nk):
    k = pl.program_id(1)
    part = lax.dot_general(a_ref[...].astype(BF16), w_ref[...].reshape(w2d), NT_DIMS, preferred_element_type=F32)

    @pl.when(k == 0)
    def _():
        acc_ref[...] = part

    @pl.when(k > 0)
    def _():
        acc_ref[...] += part

    return k == nk - 1


def _nt_normbwd(dy, dy_spec, w_g, w_spec, w2d, nk, x, g, dres, name, deps=(), tm=NORMBWD_ROWS):
    S, D = x.shape
    tm = min(tm, S)

    def body(dy_ref, w_ref, x_ref, g_ref, dres_ref, *rest):
        dx_ref, dxb_ref, dg_ref, acc_ref = rest[-4:]

        @pl.when((pl.program_id(0) == 0) & (pl.program_id(1) == 0))
        def _():
            dg_ref[...] = jnp.zeros_like(dg_ref)

        last = _nt_accumulate(dy_ref, w_ref, acc_ref, w2d, nk)

        @pl.when(last)
        def _():
            xhat, r = _rms_hat(x_ref[...])
            dx, dgp = _rms_bwd(xhat, r, g_ref[...], acc_ref[...])
            total = dres_ref[...] + dx
            dx_ref[...] = total
            dxb_ref[...] = total.astype(BF16)
            dg_ref[...] += jnp.sum(dgp, axis=0, keepdims=True)

    row = pl.BlockSpec((tm, D), lambda i, k: (i, 0))
    vec = pl.BlockSpec((1, D), lambda i, k: (0, 0))
    return pl.pallas_call(
        body, name=name, grid=(S // tm, nk),
        in_specs=[dy_spec, w_spec, row, vec, row] + [HBM_SPEC] * len(deps), out_specs=[row, row, vec],
        out_shape=[_sds((S, D), F32), _sds((S, D), BF16), _sds((1, D), F32)],
        scratch_shapes=[pltpu.VMEM((tm, D), F32)], compiler_params=_cparams(2))(dy, w_g, x, g, dres, *deps)


def _nt_rows(dy, w_g, shards_per_block, out_dtype, name, deps=(), tm=1024):
    S, N = dy.shape
    tm = min(tm, S)
    r = w_g.shape[1]
    tn = shards_per_block * r

    def body(dy_ref, w_ref, *rest):
        o_ref = rest[-1]
        o_ref[...] = lax.dot_general(dy_ref[...].astype(BF16), w_ref[...].reshape(tn, N), NT_DIMS,
                                     preferred_element_type=F32).astype(o_ref.dtype)

    return pl.pallas_call(
        body, name=name, grid=(S // tm, N_CHIPS // shards_per_block),
        in_specs=[pl.BlockSpec((tm, N), lambda i, j: (i, 0)),
                  pl.BlockSpec((shards_per_block, r, N), lambda i, j: (j, 0, 0))] + [HBM_SPEC] * len(deps),
        out_specs=pl.BlockSpec((tm, tn), lambda i, j: (i, j)),
        out_shape=_sds((S, N_CHIPS * r), out_dtype), compiler_params=_cparams(2))(dy, w_g, *deps)


def _nt_swiglu_bwd(dy, w_g, gu, name, deps=(), tm=1024):
    S, N = dy.shape
    tm = min(tm, S)
    r = w_g.shape[1]
    tn = 2 * r
    F = N_CHIPS * r

    def body(dy_ref, w_ref, gate_ref, up_ref, *rest):
        o_ref = rest[-1]
        dact = lax.dot_general(dy_ref[...].astype(BF16), w_ref[...].reshape(tn, N), NT_DIMS,
                               preferred_element_type=F32)
        gate, up = gate_ref[...].astype(F32), up_ref[...].astype(F32)
        sg = jax.nn.sigmoid(gate)
        silu = gate * sg
        o_ref[0] = ((dact * up) * (sg + silu * (1.0 - sg))).astype(BF16)
        o_ref[1] = (dact * silu).astype(BF16)

    return pl.pallas_call(
        body, name=name, grid=(2, S // tm),
        in_specs=[pl.BlockSpec((tm, N), lambda j, i: (i, 0)),
                  pl.BlockSpec((2, r, N), lambda j, i: (j, 0, 0)),
                  pl.BlockSpec((tm, tn), lambda j, i: (i, j)),
                  pl.BlockSpec((tm, tn), lambda j, i: (i, 2 + j))] + [HBM_SPEC] * len(deps),
        out_specs=pl.BlockSpec((2, tm, tn), lambda j, i: (0, i, j)),
        out_shape=_sds((2, S, F), BF16), compiler_params=_cparams(2))(dy, w_g, gu, gu, *deps)


def _chunk_causal_mask(transposed):
    i = lax.broadcasted_iota(jnp.int32, (A_CHUNK, A_CHUNK), 0) // CHUNK
    j = lax.broadcasted_iota(jnp.int32, (A_CHUNK, A_CHUNK), 1) // CHUNK
    return ((i <= j) if transposed else (i >= j)).astype(F32)


def _sgu_fwd(zpre, g_sgu, ws, bs_t, name, deps=()):
    S, F2 = zpre.shape
    F = F2 // 2
    gd = F // A_GROUPS

    windows = 2
    rows = windows * A_CHUNK

    def body(zu_ref, zv_ref, g_ref, ws_ref, b_ref, *rest):
        o_ref = rest[-1]
        vhat, _ = _rms_hat(_gelu(zv_ref[...].astype(F32)))
        vn = (vhat * g_ref[...]).astype(BF16)
        u = _gelu(zu_ref[...].astype(F32))
        mask = _chunk_causal_mask(False)
        for gi in range(A_GROUPS):
            sl = slice(gi * gd, (gi + 1) * gd)
            wm = (ws_ref[gi] * mask).astype(BF16)
            for w in range(windows):
                win = slice(w * A_CHUNK, (w + 1) * A_CHUNK)
                vs = jnp.dot(wm, vn[win, sl], preferred_element_type=F32) + b_ref[:, gi:gi + 1]
                o_ref[win, sl] = (u[win, sl] * vs).astype(BF16)

    return pl.pallas_call(
        body, name=name, grid=(S // rows,),
        in_specs=[pl.BlockSpec((rows, F), lambda i: (i, 0)),
                  pl.BlockSpec((rows, F), lambda i: (i, 1)),
                  pl.BlockSpec((1, F), lambda i: (0, 0)),
                  pl.BlockSpec((A_GROUPS, A_CHUNK, A_CHUNK), lambda i: (0, 0, 0)),
                  pl.BlockSpec((A_CHUNK, A_GROUPS), lambda i: (0, 0))] + [HBM_SPEC] * len(deps),
        out_specs=pl.BlockSpec((rows, F), lambda i: (i, 0)),
        out_shape=_sds((S, F), BF16), compiler_params=_cparams(1))(zpre, zpre, g_sgu, ws, bs_t, *deps)


def _sgu_bwd(zpre, duv, g_sgu, ws, ws_t, bs_t, name):
    S, F2 = zpre.shape
    F = F2 // 2
    gd = F // A_GROUPS

    def body(zu_ref, zv_ref, duv_ref, g_ref, ws_ref, wst_ref, b_ref, dz_ref, dg_ref, dws_ref, dbs_ref, dvn_ref):
        @pl.when(pl.program_id(0) == 0)
        def _():
            dg_ref[...] = jnp.zeros_like(dg_ref)
            dws_ref[...] = jnp.zeros_like(dws_ref)
            dbs_ref[...] = jnp.zeros_like(dbs_ref)

        gv = g_ref[...]
        u, u_grad = _gelu_and_grad(zu_ref[...].astype(F32))
        v, v_grad = _gelu_and_grad(zv_ref[...].astype(F32))
        vhat, r = _rms_hat(v)
        vn = (vhat * gv).astype(BF16)
        duv_v = duv_ref[...].astype(F32)
        dvs = duv_v * u
        dvs_b = dvs.astype(BF16)
        mask = _chunk_causal_mask(False)
        mask_t = _chunk_causal_mask(True)
        for gi in range(A_GROUPS):
            sl = slice(gi * gd, (gi + 1) * gd)
            wm = (ws_ref[gi] * mask).astype(BF16)
            vs = jnp.dot(wm, vn[:, sl], preferred_element_type=F32) + b_ref[:, gi:gi + 1]
            dz_ref[:, sl] = ((duv_v[:, sl] * vs) * u_grad[:, sl]).astype(BF16)
            dws_ref[gi] += lax.dot_general(dvs_b[:, sl], vn[:, sl], NT_DIMS, preferred_element_type=F32) * mask
            dbs_ref[gi] += jnp.broadcast_to(jnp.sum(dvs[:, sl], axis=1, keepdims=True), (A_CHUNK, A_CHUNK))
            wm_t = (wst_ref[gi] * mask_t).astype(BF16)
            dvn_ref[:, sl] = jnp.dot(wm_t, dvs_b[:, sl], preferred_element_type=F32)
        dv, dg_part = _rms_bwd(vhat, r, gv, dvn_ref[...])
        dg_ref[...] += jnp.sum(dg_part, axis=0, keepdims=True)
        dz_ref[:, F:] = (dv * v_grad).astype(BF16)

    blk = pl.BlockSpec((A_CHUNK, F), lambda i: (i, 0))
    const3 = pl.BlockSpec((A_GROUPS, A_CHUNK, A_CHUNK), lambda i: (0, 0, 0))
    return pl.pallas_call(
        body, name=name, grid=(S // A_CHUNK,),
        in_specs=[blk, pl.BlockSpec((A_CHUNK, F), lambda i: (i, 1)), blk,
                  pl.BlockSpec((1, F), lambda i: (0, 0)), const3, const3,
                  pl.BlockSpec((A_CHUNK, A_GROUPS), lambda i: (0, 0))],
        out_specs=[pl.BlockSpec((A_CHUNK, F2), lambda i: (i, 0)), pl.BlockSpec((1, F), lambda i: (0, 0)),
                   const3, const3],
        out_shape=[_sds((S, F2), BF16), _sds((1, F), F32), _sds((A_GROUPS, A_CHUNK, A_CHUNK), F32),
                   _sds((A_GROUPS, A_CHUNK, A_CHUNK), F32)],
        scratch_shapes=[pltpu.VMEM((A_CHUNK, F), F32)],
        compiler_params=_cparams(1))(zpre, zpre, duv, g_sgu, ws, ws_t, bs_t)


def _toeplitz_one_hot():
    row = lax.broadcasted_iota(jnp.int32, (TABLE_PAD, DIAGS), 0)
    j = lax.broadcasted_iota(jnp.int32, (TABLE_PAD, DIAGS), 1)
    idx = jnp.clip(KV_PAD + Q_BLOCK - j, -MAX_REL, MAX_REL) + MAX_REL
    return (row == idx).astype(F32)


def _rel_bias_fwd(table, name):
    H = table.shape[0]

    def body(t_ref, o_ref):
        diag = jnp.dot(t_ref[...], _toeplitz_one_hot(), precision=HIGHEST, preferred_element_type=F32)
        q_chunk = lax.broadcasted_iota(jnp.int32, (Q_BLOCK, BAND), 0) // CHUNK
        k_chunk = lax.broadcasted_iota(jnp.int32, (Q_BLOCK, BAND), 1) // CHUNK
        unseen = jnp.where((k_chunk >= q_chunk) & (k_chunk <= q_chunk + N_LEFT), 0.0, NEG_INF)
        for h in range(H):
            rows = jnp.broadcast_to(diag[h:h + 1, :], (Q_BLOCK, DIAGS))
            o_ref[h] = pltpu.roll(rows, DIAGS - Q_BLOCK, 1, stride=1, stride_axis=0)[:, :BAND] + unseen

    return pl.pallas_call(body, name=name, out_shape=_sds((H, Q_BLOCK, BAND), F32),
                          compiler_params=pltpu.CompilerParams(vmem_limit_bytes=VMEM_LIMIT_BYTES))(table)


def _rel_bias_bwd(dbias, name):
    H = dbias.shape[0]

    def body(d_ref, o_ref):
        def step(r, acc):
            row = d_ref[:, pl.ds(r, 1), :].reshape(H, BAND)
            row = jnp.concatenate([row, jnp.zeros((H, DIAGS - BAND), F32)], axis=1)
            return acc + pltpu.roll(row, Q_BLOCK - r, 1)

        diag = lax.fori_loop(0, Q_BLOCK, step, jnp.zeros((H, DIAGS), F32))
        o_ref[...] = lax.dot_general(diag, _toeplitz_one_hot(), NT_DIMS, precision=HIGHEST,
                                     preferred_element_type=F32)

    return pl.pallas_call(body, name=name, out_shape=_sds((H, TABLE_PAD), F32),
                          compiler_params=pltpu.CompilerParams(vmem_limit_bytes=VMEM_LIMIT_BYTES))(dbias)


def _head_rows(t):
    lane = lax.broadcasted_iota(jnp.int32, t.shape, 1)
    zero = jnp.zeros_like(t)
    return jnp.concatenate([jnp.where(lane < HEAD_DIM, t, zero), jnp.where(lane >= HEAD_DIM, t, zero)], axis=0)


def _head_lanes(t2):
    lane = lax.broadcasted_iota(jnp.int32, (Q_BLOCK, t2.shape[1]), 1)
    return jnp.where(lane < HEAD_DIM, t2[:Q_BLOCK], t2[Q_BLOCK:])


def _attn_probs(q2, kb, bias2, block):
    kj = lax.broadcasted_iota(jnp.int32, (1, BAND), 1)
    before_start = jnp.where(block * Q_BLOCK + kj - KV_PAD >= 0, 0.0, NEG_INF)
    s = lax.dot_general(q2 * ATTN_SCALE, kb, NT_DIMS, preferred_element_type=F32) + bias2 + before_start
    e = jnp.exp(s - jnp.max(s, axis=-1, keepdims=True))
    return e / jnp.sum(e, axis=-1, keepdims=True)


def _attn_specs(S):
    lanes = HEADS_PER_BLOCK * HEAD_DIM
    rows = S + KV_PAD
    per_step = min(BLOCKS_PER_STEP, S // Q_BLOCK)
    q_spec = pl.BlockSpec((per_step * Q_BLOCK, lanes), lambda h, i: (i, h))
    k_spec = pl.BlockSpec((rows, lanes), lambda h, i: (0, h))
    v_spec = pl.BlockSpec((rows, lanes), lambda h, i: (0, N_HEADS // HEADS_PER_BLOCK + h))
    b_spec = pl.BlockSpec((HEADS_PER_BLOCK, Q_BLOCK, BAND), lambda h, i: (h, 0, 0))
    return q_spec, k_spec, v_spec, b_spec, per_step


def _attn_fwd(q, kvp, bias, name, deps=()):
    S, HD = q.shape
    q_spec, k_spec, v_spec, b_spec, per_step = _attn_specs(S)

    def body(q_ref, k_ref, v_ref, b_ref, *rest):
        o_ref = rest[-1]
        for b in range(per_step):
            block = pl.program_id(1) * per_step + b
            rows = slice(b * Q_BLOCK, (b + 1) * Q_BLOCK)
            band = pl.ds(pl.multiple_of(block * Q_BLOCK, Q_BLOCK), BAND)
            p = _attn_probs(_head_rows(q_ref[rows, :]), k_ref[band, :], b_ref[...].reshape(2 * Q_BLOCK, BAND), block)
            o2 = jnp.dot(p.astype(BF16), v_ref[band, :], preferred_element_type=F32)
            o_ref[rows, :] = _head_lanes(o2).astype(BF16)

    return pl.pallas_call(
        body, name=name, grid=(N_HEADS // HEADS_PER_BLOCK, S // (per_step * Q_BLOCK)),
        in_specs=[q_spec, k_spec, v_spec, b_spec] + [HBM_SPEC] * len(deps), out_specs=q_spec,
        out_shape=_sds((S, HD), BF16), compiler_params=_cparams(2))(q, kvp, kvp, bias, *deps)


def _attn_bwd(q, kvp, bias, do, dkv_prev, name):
    S, HD = q.shape
    lanes = HEADS_PER_BLOCK * HEAD_DIM
    q_spec, k_spec, v_spec, b_spec, per_step = _attn_specs(S)
    dkv_spec = pl.BlockSpec((2, S + KV_PAD, lanes), lambda h, i: (0, 0, h))
    prev = [] if dkv_prev is None else [dkv_prev]
    n_steps = S // (per_step * Q_BLOCK)

    def body(q_ref, k_ref, v_ref, b_ref, do_ref, *rest):
        dq_ref, dkv_ref, db_ref = rest[len(prev):len(prev) + 3]

        @pl.when(pl.program_id(1) == 0)
        def _():
            dkv_ref[...] = rest[0][...] if prev else jnp.zeros_like(dkv_ref)
            db_ref[...] = jnp.zeros_like(db_ref)

        db = jnp.zeros((2 * Q_BLOCK, BAND), F32)
        for b in range(per_step):
            block = pl.program_id(1) * per_step + b
            rows = slice(b * Q_BLOCK, (b + 1) * Q_BLOCK)
            band = pl.ds(pl.multiple_of(block * Q_BLOCK, Q_BLOCK), BAND)
            kb, vb = k_ref[band, :], v_ref[band, :]
            q2, do2 = _head_rows(q_ref[rows, :]), _head_rows(do_ref[rows, :])
            p = _attn_probs(q2, kb, b_ref[...].reshape(2 * Q_BLOCK, BAND), block)
            dp = lax.dot_general(do2, vb, NT_DIMS, preferred_element_type=F32)
            ds = p * (dp - jnp.sum(dp * p, axis=-1, keepdims=True))
            db = db + ds
            ds_b = (ds * ATTN_SCALE).astype(BF16)
            dq_ref[rows, :] = _head_lanes(jnp.dot(ds_b, kb, preferred_element_type=F32)).astype(BF16)
            dkv_ref[0, band, :] += lax.dot_general(ds_b, q2, TN_DIMS, preferred_element_type=F32)
            dkv_ref[1, band, :] += lax.dot_general(p.astype(BF16), do2, TN_DIMS, preferred_element_type=F32)
        db_ref[...] += db.reshape(HEADS_PER_BLOCK, Q_BLOCK, BAND)

        if prev:
            @pl.when(pl.program_id(1) == n_steps - 1)
            def _():
                rest[-1][...] = dkv_ref[:, KV_PAD:, :].astype(BF16)

    return pl.pallas_call(
        body, name=name, grid=(N_HEADS // HEADS_PER_BLOCK, n_steps),
        in_specs=[q_spec, k_spec, v_spec, b_spec, q_spec] + [dkv_spec] * len(prev),
        out_specs=[q_spec, dkv_spec, b_spec] + [pl.BlockSpec((2, S, lanes), lambda h, i: (0, 0, h))] * len(prev),
        out_shape=[_sds((S, HD), BF16), _sds((2, S + KV_PAD, HD), F32), _sds((N_HEADS, Q_BLOCK, BAND), F32)]
        + [_sds((2, S, HD), BF16)] * len(prev),
        compiler_params=_cparams(2))(q, kvp, kvp, bias, do, *prev)


def _loss_head(x, g, target, name, tm=512):
    S, D = x.shape

    def body(x_ref, g_ref, t_ref, loss_ref, dx_ref, dxb_ref, dg_ref):
        @pl.when(pl.program_id(0) == 0)
        def _():
            loss_ref[...] = jnp.zeros_like(loss_ref)
            dg_ref[...] = jnp.zeros_like(dg_ref)

        xhat, r = _rms_hat(x_ref[...])
        gv = g_ref[...]
        err = xhat * gv - t_ref[...]
        loss_ref[...] += 0.5 * jnp.sum(jnp.mean(err * err, axis=-1, keepdims=True))
        dx, dgp = _rms_bwd(xhat, r, gv, err * (1.0 / D))
        dx_ref[...] = dx
        dxb_ref[...] = dx.astype(BF16)
        dg_ref[...] += jnp.sum(dgp, axis=0, keepdims=True)

    row = pl.BlockSpec((tm, D), lambda i: (i, 0))
    vec = pl.BlockSpec((1, D), lambda i: (0, 0))
    return pl.pallas_call(
        body, name=name, grid=(S // tm,), in_specs=[row, vec, row],
        out_specs=[pl.BlockSpec((8, 128), lambda i: (0, 0)), row, row, vec],
        out_shape=[_sds((8, 128), F32), _sds((S, D), F32), _sds((S, D), BF16), _sds((1, D), F32)],
        compiler_params=_cparams(1))(x, g, target)


def _place():
    x, y, c = lax.axis_index("x"), lax.axis_index("y"), lax.axis_index("c")
    chips = [(1 - x, y), (x, 1 - y), (1 - x, 1 - y)]
    return x, y, c, chips


def _half_rows(c, r):
    return pl.ds(pl.multiple_of(c * (r // 2), 8), r // 2)


HBM_SPEC = pl.BlockSpec(memory_space=pl.ANY)


STRICT_HBM_SPEC = pl.BlockSpec(memory_space=pltpu.HBM)
SEM_SPEC = pl.BlockSpec(memory_space=pltpu.SEMAPHORE)
EFFECT = pltpu.SideEffectType.DATAFLOW_SIDE_EFFECTING


def _peers(x, y, c):
    out = []
    for k in range(1, N_DEV):
        px, py, pc = (x + ((k >> 2) & 1)) % 2, (y + ((k >> 1) & 1)) % 2, (c + (k & 1)) % 2
        out.append(((px, py, pc), 2 * px + py, pc, 4 * px + 2 * py + pc))
    return out


def _token_spec():
    return pl.BlockSpec(memory_space=pltpu.VMEM)


def _hbm(a):
    return pltpu.with_memory_space_constraint(a, pltpu.HBM)


def _slab_half(ref, chip, core):
    return ref.at[2 * chip[0] + chip[1], _half_rows(core, ref.shape[1]), :]


def _allgather_start(slabs, name):
    n = len(slabs)

    def body(*refs):
        src, send, recv, token = refs[:n], refs[n], refs[n + 1], refs[-1]
        x, y, c, chips = _place()
        for a in range(n):
            own = _slab_half(src[a], (x, y), c)
            for j, chip in enumerate(chips):
                pltpu.make_async_remote_copy(src_ref=own, dst_ref=own, send_sem=send.at[3 * a + j], recv_sem=recv.at[3 * a + j],
                                             device_id=(*chip, c), device_id_type=MESH).start()
        token[...] = jnp.zeros_like(token)

    sems = pltpu.SemaphoreType.DMA((3 * n,))
    send, recv, *flying, token = pl.pallas_call(
        body, name=name, in_specs=[STRICT_HBM_SPEC] * n,
        out_shape=(sems, sems, *[pltpu.HBM(s.shape, s.dtype) for s in slabs], _sds((8, 128), F32)),
        out_specs=(SEM_SPEC, SEM_SPEC, *[STRICT_HBM_SPEC] * n, _token_spec()),
        input_output_aliases={a: a + 2 for a in range(n)},
        compiler_params=pltpu.CompilerParams(has_side_effects=EFFECT))(*[_hbm(s) for s in slabs])
    return send, recv, flying, token


def _allgather_relay(flying, send, recv, first, after, name):
    n = len(flying)

    def body(*refs):
        src, send_ref, recv_ref = refs[:n], refs[n], refs[n + 1]
        send2, recv2, token = refs[n + 3], refs[n + 4], refs[-1]
        token[...] = jnp.zeros_like(token)
        x, y, c, chips = _place()
        for a in range(n):
            for j, chip in enumerate(chips):
                cp = pltpu.make_async_remote_copy(
                    src_ref=_slab_half(src[a], (x, y), c), dst_ref=_slab_half(src[a], chip, c),
                    send_sem=send_ref.at[3 * (first + a) + j], recv_sem=recv_ref.at[3 * (first + a) + j],
                    device_id=(*chip, c), device_id_type=MESH)
                cp.wait_send()
                cp.wait_recv()
        for a in range(n):
            for j, chip in enumerate(chips):
                landed = _slab_half(src[a], chip, c)
                pltpu.make_async_remote_copy(src_ref=landed, dst_ref=landed, send_sem=send2.at[3 * a + j],
                                             recv_sem=recv2.at[3 * a + j], device_id=(x, y, 1 - c),
                                             device_id_type=MESH).start()

    sems = pltpu.SemaphoreType.DMA((3 * n,))
    send2, recv2, *relayed, token = pl.pallas_call(
        body, name=name, in_specs=[STRICT_HBM_SPEC] * n + [SEM_SPEC, SEM_SPEC, HBM_SPEC],
        out_shape=(sems, sems, *[pltpu.HBM(s.shape, s.dtype) for s in flying], _sds((8, 128), F32)),
        out_specs=(SEM_SPEC, SEM_SPEC, *[STRICT_HBM_SPEC] * n, _token_spec()),
        input_output_aliases={a: a + 2 for a in range(n)},
        compiler_params=pltpu.CompilerParams(has_side_effects=EFFECT))(*flying, send, recv, after)
    return send2, recv2, relayed, token


def _allgather_wait(relayed, send2, recv2, after, name):
    n = len(relayed)

    def body(*refs):
        src, send_ref, recv_ref = refs[:n], refs[n], refs[n + 1]
        x, y, c, chips = _place()
        for a in range(n):
            for j, chip in enumerate(chips):
                cp = pltpu.make_async_remote_copy(
                    src_ref=_slab_half(src[a], chip, c), dst_ref=_slab_half(src[a], chip, 1 - c),
                    send_sem=send_ref.at[3 * a + j], recv_sem=recv_ref.at[3 * a + j],
                    device_id=(x, y, 1 - c), device_id_type=MESH)
                cp.wait_send()
                cp.wait_recv()

    return pl.pallas_call(
        body, name=name, in_specs=[STRICT_HBM_SPEC] * n + [SEM_SPEC, SEM_SPEC, HBM_SPEC],
        out_shape=tuple(pltpu.HBM(s.shape, s.dtype) for s in relayed), out_specs=tuple([STRICT_HBM_SPEC] * n),
        input_output_aliases={a: a for a in range(n)},
        compiler_params=pltpu.CompilerParams(has_side_effects=EFFECT))(*relayed, send2, recv2, after)


def _allgather_small(small, name):
    def body(sm, osm, send, recv, local):
        x, y, c, chips = _place()
        own = pltpu.make_async_copy(sm, osm.at[2 * x + y], local)
        own.start()
        cps = [pltpu.make_async_remote_copy(src_ref=sm, dst_ref=osm.at[2 * x + y], send_sem=send.at[j],
                                            recv_sem=recv.at[j], device_id=(*chip, c), device_id_type=MESH)
               for j, chip in enumerate(chips)]
        for cp in cps:
            cp.start()
        for j, chip in enumerate(chips):
            got = osm.at[2 * chip[0] + chip[1]]
            pltpu.make_async_remote_copy(src_ref=got, dst_ref=got, send_sem=send.at[j], recv_sem=recv.at[j],
                                         device_id=(x, y, c), device_id_type=MESH).wait_recv()
        for cp in cps:
            cp.wait_send()
        own.wait()

    return pl.pallas_call(
        body, name=name, in_specs=[pl.BlockSpec(memory_space=pltpu.VMEM)], out_specs=HBM_SPEC,
        out_shape=_sds((N_CHIPS, *small.shape), small.dtype),
        scratch_shapes=[pltpu.SemaphoreType.DMA((3,)), pltpu.SemaphoreType.DMA((3,)), pltpu.SemaphoreType.DMA])(small)


def _reduce_start(grads, name):
    n = len(grads)

    def body(*refs):
        src, land, send, recv, token = refs[:n], refs[n:2 * n], refs[2 * n], refs[2 * n + 1], refs[-1]
        x, y, c, _ = _place()
        me = 4 * x + 2 * y + c
        for a in range(n):
            for k, (peer, p_chip, p_core, _) in enumerate(_peers(x, y, c)):
                pltpu.make_async_remote_copy(
                    src_ref=src[a].at[p_chip, _half_rows(p_core, src[a].shape[1]), :], dst_ref=land[a].at[me],
                    send_sem=send.at[(N_DEV - 1) * a + k], recv_sem=recv.at[(N_DEV - 1) * a + k],
                    device_id=peer, device_id_type=MESH).start()
        token[...] = jnp.zeros_like(token)

    lands = [lax.empty((N_DEV, g.shape[1] // 2, g.shape[2]), BF16) for g in grads]
    sems = pltpu.SemaphoreType.DMA(((N_DEV - 1) * n,))
    shapes = [pltpu.HBM(a.shape, a.dtype) for a in grads + lands]
    send, recv, *flying, token = pl.pallas_call(
        body, name=name, in_specs=[STRICT_HBM_SPEC] * (2 * n),
        out_shape=(sems, sems, *shapes, _sds((8, 128), F32)),
        out_specs=(SEM_SPEC, SEM_SPEC, *[STRICT_HBM_SPEC] * (2 * n), _token_spec()),
        input_output_aliases={a: a + 2 for a in range(2 * n)},
        compiler_params=pltpu.CompilerParams(has_side_effects=EFFECT))(*[_hbm(a) for a in grads + lands])
    return send, recv, flying[:n], flying[n:], token


def _reduce_wait(started, after, name):
    sizes = [len(grads) for _, _, grads, _ in started]
    n_arr = 2 * sum(sizes)

    def body(*refs):
        x, y, c, _ = _place()
        at = 0
        for s, n in enumerate(sizes):
            src, land = refs[at:at + n], refs[at + n:at + 2 * n]
            send_ref, recv_ref = refs[n_arr + 2 * s], refs[n_arr + 2 * s + 1]
            at += 2 * n
            for a in range(n):
                for k, (peer, p_chip, p_core, p_dev) in enumerate(_peers(x, y, c)):
                    cp = pltpu.make_async_remote_copy(
                        src_ref=src[a].at[p_chip, _half_rows(p_core, src[a].shape[1]), :], dst_ref=land[a].at[p_dev],
                        send_sem=send_ref.at[(N_DEV - 1) * a + k], recv_sem=recv_ref.at[(N_DEV - 1) * a + k],
                        device_id=peer, device_id_type=MESH)
                    cp.wait_send()
                    cp.wait_recv()

    arrays, sems = [], []
    for send, recv, grads, lands in started:
        arrays += list(grads) + list(lands)
        sems += [send, recv]
    out = pl.pallas_call(
        body, name=name, in_specs=[STRICT_HBM_SPEC] * n_arr + [SEM_SPEC] * len(sems) + [HBM_SPEC],
        out_shape=tuple(pltpu.HBM(a.shape, a.dtype) for a in arrays), out_specs=tuple([STRICT_HBM_SPEC] * n_arr),
        input_output_aliases={a: a for a in range(n_arr)},
        compiler_params=pltpu.CompilerParams(has_side_effects=EFFECT))(*arrays, *sems, after)
    result, at = [], 0
    for n in sizes:
        result.append((out[at:at + n], out[at + n:at + 2 * n]))
        at += 2 * n
    return result


def _reduce_sum(grad, land, place, name, deps=()):
    _, r2, C = land.shape
    tr = _row_tile(r2, C, 4, 2 * 1024 * 1024, 16)
    nb = r2 // tr

    def body(place_ref, own_ref, *rest):
        del place_ref
        acc = own_ref[...].astype(F32)
        for ref in rest[:N_DEV - 1]:
            acc = acc + ref[...].astype(F32)
        rest[-1][...] = acc

    def from_dev(k):
        return pl.BlockSpec((None, tr, C), lambda i, place_ref: ((place_ref[2] + k) % N_DEV, i, 0))

    grid_spec = pltpu.PrefetchScalarGridSpec(
        num_scalar_prefetch=1, grid=(nb,),
        in_specs=[pl.BlockSpec((None, tr, C), lambda i, place_ref: (place_ref[0], place_ref[1] * nb + i, 0))]
        + [from_dev(k) for k in range(1, N_DEV)] + [HBM_SPEC] * len(deps),
        out_specs=pl.BlockSpec((tr, C), lambda i, place_ref: (place_ref[1] * nb + i, 0)))
    return pl.pallas_call(body, name=name, grid_spec=grid_spec, out_shape=_sds((2 * r2, C), F32),
                          compiler_params=_cparams(1))(place, grad, *[land] * (N_DEV - 1), *deps)


def _join_start(halves, name):
    n = len(halves)

    def body(*refs):
        src, send, recv, token = refs[:n], refs[n], refs[n + 1], refs[-1]
        x, y, c, _ = _place()
        for w in range(n):
            mine = src[w].at[_half_rows(c, src[w].shape[0]), :]
            pltpu.make_async_remote_copy(src_ref=mine, dst_ref=mine, send_sem=send.at[w], recv_sem=recv.at[w],
                                         device_id=(x, y, 1 - c), device_id_type=MESH).start()
        token[...] = jnp.zeros_like(token)

    sems = pltpu.SemaphoreType.DMA((n,))
    send, recv, *flying, token = pl.pallas_call(
        body, name=name, in_specs=[STRICT_HBM_SPEC] * n,
        out_shape=(sems, sems, *[pltpu.HBM(a.shape, a.dtype) for a in halves], _sds((8, 128), F32)),
        out_specs=(SEM_SPEC, SEM_SPEC, *[STRICT_HBM_SPEC] * n, _token_spec()),
        input_output_aliases={w: w + 2 for w in range(n)},
        compiler_params=pltpu.CompilerParams(has_side_effects=EFFECT))(*[_hbm(a) for a in halves])
    return send, recv, flying, token


def _join_wait(flying, send, recv, after, name):
    n = len(flying)

    def body(*refs):
        src, send_ref, recv_ref = refs[:n], refs[n], refs[n + 1]
        x, y, c, _ = _place()
        for w in range(n):
            cp = pltpu.make_async_remote_copy(
                src_ref=src[w].at[_half_rows(c, src[w].shape[0]), :],
                dst_ref=src[w].at[_half_rows(1 - c, src[w].shape[0]), :], send_sem=send_ref.at[w],
                recv_sem=recv_ref.at[w], device_id=(x, y, 1 - c),
                device_id_type=MESH)
            cp.wait_send()
            cp.wait_recv()

    return pl.pallas_call(
        body, name=name, in_specs=[STRICT_HBM_SPEC] * n + [SEM_SPEC, SEM_SPEC, HBM_SPEC],
        out_shape=tuple(pltpu.HBM(a.shape, a.dtype) for a in flying), out_specs=tuple([STRICT_HBM_SPEC] * n),
        input_output_aliases={w: w for w in range(n)},
        compiler_params=pltpu.CompilerParams(has_side_effects=EFFECT))(*flying, send, recv, after)


def _gather_small(packed, name):
    def body(p_ref, out, send, recv, local):
        x, y, c, _ = _place()
        me = 4 * x + 2 * y + c
        own = pltpu.make_async_copy(p_ref, out.at[me], local)
        own.start()
        cps = []
        for k in range(1, N_DEV):
            fx, fy, fc = (k >> 2) & 1, (k >> 1) & 1, k & 1
            peer = ((x + fx) % 2, (y + fy) % 2, (c + fc) % 2)
            cps.append(pltpu.make_async_remote_copy(src_ref=p_ref, dst_ref=out.at[me], send_sem=send.at[k - 1],
                                                    recv_sem=recv.at[k - 1], device_id=peer, device_id_type=MESH))
        for cp in cps:
            cp.start()
        for k in range(1, N_DEV):
            fx, fy, fc = (k >> 2) & 1, (k >> 1) & 1, k & 1
            src = out.at[4 * ((x + fx) % 2) + 2 * ((y + fy) % 2) + (c + fc) % 2]
            pltpu.make_async_remote_copy(src_ref=src, dst_ref=src, send_sem=send.at[k - 1], recv_sem=recv.at[k - 1],
                                         device_id=(x, y, c), device_id_type=MESH).wait_recv()
        for cp in cps:
            cp.wait_send()
        own.wait()

    return pl.pallas_call(
        body, name=name, in_specs=[pl.BlockSpec(memory_space=pltpu.VMEM)], out_specs=HBM_SPEC,
        out_shape=_sds((N_DEV, *packed.shape), F32),
        scratch_shapes=[pltpu.SemaphoreType.DMA((N_DEV - 1,)), pltpu.SemaphoreType.DMA((N_DEV - 1,)),
                        pltpu.SemaphoreType.DMA])(packed)


def _sum_devices(gathered, name):
    _, R, C = gathered.shape

    def body(g_ref, o_ref):
        acc = g_ref[0]
        for d in range(1, N_DEV):
            acc = acc + g_ref[d]
        o_ref[...] = acc

    tr = 8
    return pl.pallas_call(
        body, name=name, grid=(R // tr,), in_specs=[pl.BlockSpec((N_DEV, tr, C), lambda i: (0, i, 0))],
        out_specs=pl.BlockSpec((tr, C), lambda i: (i, 0)), out_shape=_sds((R, C), F32),
        compiler_params=_cparams(1))(gathered)


def _gather_start(packed, name):
    def body(src, land, send, recv, *rest):
        x, y, c, _ = _place()
        for k, (peer, _, _, _) in enumerate(_peers(x, y, c)):
            pltpu.make_async_remote_copy(src_ref=src, dst_ref=land.at[4 * x + 2 * y + c], send_sem=send.at[k],
                                         recv_sem=recv.at[k], device_id=peer, device_id_type=MESH).start()
        rest[-1][...] = jnp.zeros_like(rest[-1])

    land = lax.empty((N_DEV, *packed.shape), F32)
    sems = pltpu.SemaphoreType.DMA((N_DEV - 1,))
    return pl.pallas_call(
        body, name=name, in_specs=[STRICT_HBM_SPEC] * 2,
        out_shape=(sems, sems, pltpu.HBM(packed.shape, F32), pltpu.HBM(land.shape, F32), _sds((8, 128), F32)),
        out_specs=(SEM_SPEC, SEM_SPEC, STRICT_HBM_SPEC, STRICT_HBM_SPEC, _token_spec()),
        input_output_aliases={0: 2, 1: 3},
        compiler_params=pltpu.CompilerParams(has_side_effects=EFFECT))(_hbm(packed), _hbm(land))


def _gather_wait(started, after, name):
    n = len(started)

    def body(*refs):
        x, y, c, _ = _place()
        for s in range(n):
            src, land, send, recv = refs[2 * s], refs[2 * s + 1], refs[2 * n + 2 * s], refs[2 * n + 2 * s + 1]
            for k, (peer, _, _, p_dev) in enumerate(_peers(x, y, c)):
                cp = pltpu.make_async_remote_copy(src_ref=src, dst_ref=land.at[p_dev], send_sem=send.at[k],
                                                  recv_sem=recv.at[k], device_id=peer,
                                                  device_id_type=MESH)
                cp.wait_send()
                cp.wait_recv()

    arrays = [a for _, _, packed, land in started for a in (packed, land)]
    sems = [s for send, recv, _, _ in started for s in (send, recv)]
    out = pl.pallas_call(
        body, name=name, in_specs=[STRICT_HBM_SPEC] * (2 * n) + [SEM_SPEC] * (2 * n) + [HBM_SPEC],
        out_shape=tuple(pltpu.HBM(a.shape, a.dtype) for a in arrays), out_specs=tuple([STRICT_HBM_SPEC] * (2 * n)),
        input_output_aliases={a: a for a in range(2 * n)},
        compiler_params=pltpu.CompilerParams(has_side_effects=EFFECT))(*arrays, *sems, after)
    return [(out[2 * s], out[2 * s + 1]) for s in range(n)]


def _sum_gathered(packed, land, device, name):
    R, C = packed.shape
    tr = 8

    def body(dev_ref, own_ref, *rest):
        me = dev_ref[0]
        acc = None
        for d in range(N_DEV):
            term = jnp.where(me == d, own_ref[...], rest[d][...])
            acc = term if acc is None else acc + term
        rest[-1][...] = acc

    def slab(d):
        return pl.BlockSpec((None, tr, C), lambda i, dev_ref: (jnp.where(dev_ref[0] == d, (d + 1) % N_DEV, d), i, 0))

    grid_spec = pltpu.PrefetchScalarGridSpec(
        num_scalar_prefetch=1, grid=(R // tr,),
        in_specs=[pl.BlockSpec((tr, C), lambda i, dev_ref: (i, 0))] + [slab(d) for d in range(N_DEV)],
        out_specs=pl.BlockSpec((tr, C), lambda i, dev_ref: (i, 0)))
    return pl.pallas_call(body, name=name, grid_spec=grid_spec, out_shape=_sds((R, C), F32),
                          compiler_params=_cparams(1))(device, packed, *[land] * N_DEV)


def _pack_small(arrays):
    rows = []
    for a in arrays:
        flat = a.reshape(-1)
        pad = (-flat.shape[0]) % SMALL_COLS
        rows.append(jnp.pad(flat, (0, pad)).reshape(-1, SMALL_COLS))
    packed = jnp.concatenate(rows, axis=0)
    return jnp.pad(packed, ((0, (-packed.shape[0]) % 8), (0, 0)))


def _unpack_small(packed, shapes):
    out, row = [], 0
    for shape in shapes:
        size = math.prod(shape)
        n_rows = -(-size // SMALL_COLS)
        out.append(packed[row:row + n_rows].reshape(-1)[:size].reshape(shape))
        row += n_rows
    return out


def kernel(x, a_norm, a_w_in, a_sgu_norm, a_w_spatial, a_b_spatial, a_w_out, kv_norm, w_kv, b_norm, b_w_q, b_rel_bias, b_w_o, ffn_norm, ffn_w_gate_up, ffn_w_down, final_norm, loss_target, m_a_norm, m_a_w_in, m_a_sgu_norm, m_a_w_spatial, m_a_b_spatial, m_a_w_out, m_kv_norm, m_w_kv, m_b_norm, m_b_w_q, m_b_rel_bias, m_b_w_o, m_ffn_norm, m_ffn_w_gate_up, m_ffn_w_down, m_final_norm, v_a_norm, v_a_w_in, v_a_sgu_norm, v_a_w_spatial, v_a_b_spatial, v_a_w_out, v_kv_norm, v_w_kv, v_b_norm, v_b_w_q, v_b_rel_bias, v_b_w_o, v_ffn_norm, v_ffn_w_gate_up, v_ffn_w_down, v_final_norm):
    S, D = x.shape[1], x.shape[2]
    n_a = a_w_in.shape[0]
    n_b = b_w_q.shape[0]
    depth = ffn_w_gate_up.shape[0]
    xi, yi, ci = lax.axis_index("x"), lax.axis_index("y"), lax.axis_index("c")
    chip = 2 * xi + yi

    place = jnp.stack([chip, ci, 2 * chip + ci]).astype(jnp.int32)
    stacked = {"a_w_in": a_w_in, "a_w_out": a_w_out, "w_kv": w_kv[None], "b_w_q": b_w_q, "b_w_o": b_w_o,
               "ffn_w_gate_up": ffn_w_gate_up, "ffn_w_down": ffn_w_down}
    groups = []
    for layer in range(depth):
        if layer == 0 and n_a > 0:
            groups += [[("a_w_in", 0)], [("a_w_out", 0)]]
        elif layer < n_a:
            groups.append([("a_w_in", layer), ("a_w_out", layer)])
        elif layer == n_a:
            groups.append([("w_kv", 0), ("b_w_q", 0), ("b_w_o", 0)])
        else:
            groups.append([("b_w_q", layer - n_a), ("b_w_o", layer - n_a)])
        groups.append([("ffn_w_gate_up", layer), ("ffn_w_down", layer)])
    units = [u for group in groups for u in group]
    n_early = len(groups[0])
    slabs = [_cast_slab(stacked[k], l, place[:1], f"cast_{k}_{l}") for k, l in units[:n_early]]
    early = _allgather_start(slabs, "allgather_start_first")
    slabs = [_cast_slab(stacked[k], l, place[:1], f"cast_{k}_{l}", deps=(early[3],)) for k, l in units[n_early:]]
    late = _allgather_start(slabs, "allgather_start_rest")
    na_w, ns_w = a_norm.shape[1], a_sgu_norm.shape[1]
    small_g = _allgather_small(jnp.concatenate([a_norm, a_sgu_norm], axis=1), "allgather_small")
    a_norm_f = small_g[:, :, :na_w].transpose(1, 0, 2).reshape(n_a, N_CHIPS * na_w)
    a_sgu_f = small_g[:, :, na_w:].transpose(1, 0, 2).reshape(n_a, N_CHIPS * ns_w)
    W, relayed = {}, []

    def relay(after):
        if len(relayed) == len(groups):
            return ()
        index = sum(len(g) for g in groups[:len(relayed)])
        group = groups[len(relayed)]
        (send, recv, flying, _), first = (early, index) if index < n_early else (late, index - n_early)
        relayed.append(_allgather_relay(flying[first:first + len(group)], send, recv, first, after,
                                        f"allgather_relay_{len(relayed)}"))
        return (relayed[-1][3],)

    n_gathered = [0]

    def gathered(after):
        index = n_gathered[0]
        send2, recv2, arrays, _ = relayed[index]
        W.update(zip(groups[index], _allgather_wait(arrays, send2, recv2, after, f"allgather_wait_{index}")))
        n_gathered[0] += 1

    xc = x.reshape(S, D)
    saved = []
    kvp = x_kv = h_kv = None
    relay(late[3])
    order = ()
    for layer in range(depth):
        rec = {"x_in": xc}
        gathered(xc)
        if layer < n_a:
            i = layer
            rec["zpre"], rec["h"] = _norm_matmul(xc, a_norm_f[i][None], W["a_w_in", i], BF16, f"a{i}_in", deps=order)
            order = relay(rec["h"])
            rec["uv"] = _sgu_fwd(rec["zpre"], a_sgu_f[i][None], a_w_spatial[i], a_b_spatial[i].T, f"a{i}_sgu",
                                 deps=order)
            order = ()
            if ("a_w_out", i) not in W:
                gathered(rec["uv"])
                order = relay(rec["uv"])
            xm = _matmul_res(rec["uv"], W["a_w_out", i], xc, f"a{i}_out", tm=512, deps=order)
        else:
            i = layer - n_a
            if i == 0:
                kvp, h_kv = _norm_matmul(xc, kv_norm[None], W["w_kv", 0], BF16, "kv_proj", tm=KV_PAD, zero_rows=KV_PAD)
                x_kv = xc
            rec["q"], rec["h"] = _norm_matmul(xc, b_norm[i][None], W["b_w_q", i], BF16, f"b{i}_q", row_sharded=True)
            order = relay(rec["h"])
            table = jnp.pad(b_rel_bias[i], ((0, 0), (0, TABLE_PAD - b_rel_bias.shape[2])))
            rec["bias"] = _rel_bias_fwd(table, f"b{i}_bias")
            rec["o"] = _attn_fwd(rec["q"], kvp, rec["bias"], f"b{i}_attn", deps=order)
            xm = _matmul_res(rec["o"], W["b_w_o", i], xc, f"b{i}_o", tm=512)
        rec["x_mid"] = xm
        gathered(xm)
        rec["gu"], rec["h_f"] = _norm_matmul(xm, ffn_norm[layer][None], W["ffn_w_gate_up", layer], BF16, f"f{layer}_in")
        order = relay(rec["h_f"])
        xc = _matmul_res(rec["gu"], W["ffn_w_down", layer], xm, f"f{layer}_out", swiglu=True, deps=order, tm=512)
        order = ()
        saved.append(rec)

    loss_tile, dx, dxb, d_final = _loss_head(xc, final_norm[None], loss_target.reshape(S, D), "loss_head")

    started = []
    small_started = []
    pending = []

    held = []

    def weight_grad(unit, hold=False, **kw):
        full = (N_CHIPS,) + tuple(stacked[unit[0]].shape[1:])
        g = _matmul_tn(out_shape=full, name=f"d_{unit[0]}_{unit[1]}", deps=tuple(pending), **kw)
        pending.clear()
        held.append((unit, g))
        if hold:
            return ()
        send, recv, flying_g, flying_land, token = _reduce_start([g for _, g in held],
                                                                 f"reduce_start_{unit[0]}_{unit[1]}")
        started.append(([u for u, _ in held], send, recv, flying_g, flying_land))
        held.clear()
        return (token,)

    tt = min(1024, S)
    tw = min(2048, S)
    ts = S
    tb = min(NORMBWD_ROWS, S)
    row_a = lambda w, rows=tw: pl.BlockSpec((rows, w), lambda o, t: (t, 0))
    d_ffn_norm, d_b_norm, d_a_norm, d_a_sgu = [None] * depth, [None] * n_b, [None] * n_a, [None] * n_a
    d_ws, d_bs, d_rel = [None] * n_a, [None] * n_a, [None] * n_b
    dkv = None
    first = lambda ref: ref[...]
    for layer in reversed(range(depth)):
        rec = saved[layer]
        r_d = ffn_w_down.shape[1]
        half_f = 2 * r_d
        token = weight_grad(
            ("ffn_w_down", layer), a_ops=[rec["gu"], rec["gu"]],
            a_specs=[pl.BlockSpec((tt, half_f), lambda o, t: (t, o)), pl.BlockSpec((tt, half_f), lambda o, t: (t, 2 + o))],
            a_fn=lambda g_ref, u_ref: _swiglu(g_ref[...].astype(F32), u_ref[...].astype(F32)).astype(BF16),
            b_op=dxb, b_spec=row_a(D, tt), out_spec=pl.BlockSpec((2, r_d, D), lambda o, t: (o, 0, 0)),
            acc_shape=(half_f, D), n_outer=2, tt=tt, hold=True)
        dgu = _nt_swiglu_bwd(dxb, W["ffn_w_down", layer], rec["gu"], f"f{layer}_dgu", deps=token)
        nsh = ffn_w_gate_up.shape[2]
        token = weight_grad(
            ("ffn_w_gate_up", layer), a_ops=[rec["h_f"]], a_specs=[row_a(D, ts)], a_fn=first,
            b_op=dgu, b_spec=pl.BlockSpec((None, ts, nsh), lambda o, t: (o // 2, t, o % 2)),
            out_spec=pl.BlockSpec((None, D, nsh), lambda o, t: (o, 0, 0)), acc_shape=(D, nsh), n_outer=N_CHIPS, tt=ts)
        dx, dxb, d_ffn_norm[layer] = _nt_normbwd(
            dgu, pl.BlockSpec((None, tb, nsh), lambda i, k: (k // 2, i, k % 2)),
            W["ffn_w_gate_up", layer], pl.BlockSpec((None, D, nsh), lambda i, k: (k, 0, 0)),
            (D, nsh), N_CHIPS, rec["x_mid"], ffn_norm[layer][None], dx, f"f{layer}_dx", deps=token)
        if layer >= n_a:
            i = layer - n_a
            r_o = b_w_o.shape[1]
            token = weight_grad(
                ("b_w_o", i), a_ops=[rec["o"]], a_specs=[row_a(D, ts)], a_fn=first, b_op=dxb, b_spec=row_a(D, ts),
                out_spec=pl.BlockSpec((N_CHIPS, r_o, D), lambda o, t: (0, 0, 0)), acc_shape=(D, D), n_outer=1, tt=ts,
                hold=True)
            do = _nt_rows(dxb, W["b_w_o", i], N_CHIPS, BF16, f"b{i}_do", deps=token)
            dq, dkv, dbias, *dkv_bf16 = _attn_bwd(rec["q"], kvp, rec["bias"], do, dkv, f"b{i}_attn_bwd")
            d_rel[i] = _rel_bias_bwd(dbias, f"b{i}_dbias")[:, :b_rel_bias.shape[2]]
            token = weight_grad(
                ("b_w_q", i), a_ops=[rec["h"]], a_specs=[row_a(D, ts)], a_fn=first, b_op=dq, b_spec=row_a(D, ts),
                out_spec=pl.BlockSpec((N_CHIPS, r_o, D), lambda o, t: (0, 0, 0)), acc_shape=(D, D), n_outer=1, tt=ts)
            dx, dxb, d_b_norm[i] = _nt_normbwd(
                dq, pl.BlockSpec((tb, D), lambda i_, k: (i_, 0)),
                W["b_w_q", i], pl.BlockSpec((N_CHIPS, r_o, D), lambda i_, k: (0, 0, 0)),
                (D, D), 1, rec["x_in"], b_norm[i][None], dx, f"b{i}_dx", deps=token)
            if i == 0:
                dkv_b = dkv_bf16[0] if dkv_bf16 else dkv[:, KV_PAD:, :].astype(BF16)
                n_kv = w_kv.shape[1]
                token = weight_grad(
                    ("w_kv", 0), a_ops=[h_kv], a_specs=[row_a(D, ts)], a_fn=first,
                    b_op=dkv_b, b_spec=pl.BlockSpec((None, ts, n_kv), lambda o, t: (o // 2, t, o % 2)),
                    out_spec=pl.BlockSpec((None, D, n_kv), lambda o, t: (o, 0, 0)), acc_shape=(D, n_kv),
                    n_outer=N_CHIPS, tt=ts)
                dx, dxb, d_kv_norm = _nt_normbwd(
                    dkv_b, pl.BlockSpec((None, tb, n_kv), lambda i_, k: (k // 2, i_, k % 2)),
                    W["w_kv", 0], pl.BlockSpec((None, D, n_kv), lambda i_, k: (k, 0, 0)),
                    (D, n_kv), N_CHIPS, x_kv, kv_norm[None], dx, "kv_dx", deps=token)
        else:
            i = layer
            r_w = a_w_out.shape[1]
            token = weight_grad(
                ("a_w_out", i), a_ops=[rec["uv"]], a_specs=[row_a(N_CHIPS * r_w, tw)], a_fn=first,
                b_op=dxb, b_spec=row_a(D, tw), out_spec=pl.BlockSpec((N_CHIPS, r_w, D), lambda o, t: (0, 0, 0)),
                acc_shape=(N_CHIPS * r_w, D), n_outer=1, tt=tw, hold=i > 0)
            duv = _nt_rows(dxb, W["a_w_out", i], 2, BF16, f"a{i}_duv", deps=token)
            dz, d_a_sgu[i], d_ws[i], dbs = _sgu_bwd(rec["zpre"], duv, a_sgu_f[i][None], a_w_spatial[i],
                                                  a_w_spatial[i].transpose(0, 2, 1), a_b_spatial[i].T, f"a{i}_sgu_bwd")
            d_bs[i] = dbs[:, :, 0]
            if i == 0:
                batch = [d_a_sgu[0], d_ws[0][None], d_bs[0][None], d_ffn_norm[0]]
                small_started.append((batch, _gather_start(_pack_small(batch), "gather_start_late")))
                pending.append(small_started[-1][1][4])
            n_in = a_w_in.shape[2]
            token = weight_grad(
                ("a_w_in", i), a_ops=[rec["h"]], a_specs=[row_a(D, ts)], a_fn=first,
                b_op=dz, b_spec=pl.BlockSpec((ts, n_in), lambda o, t: (t, o)),
                out_spec=pl.BlockSpec((None, D, n_in), lambda o, t: (o, 0, 0)), acc_shape=(D, n_in), n_outer=N_CHIPS,
                tt=ts)
            dx, dxb, d_a_norm[i] = _nt_normbwd(
                dz, pl.BlockSpec((tb, n_in), lambda i_, k: (i_, k)),
                W["a_w_in", i], pl.BlockSpec((None, D, n_in), lambda i_, k: (k, 0, 0)),
                (D, n_in), N_CHIPS, rec["x_in"], a_norm_f[i][None], dx, f"a{i}_dx", deps=token)
        if layer == 1:
            batch = [jnp.concatenate(d_a_norm[1:], axis=0), jnp.concatenate(d_a_sgu[1:], axis=0), jnp.stack(d_ws[1:]),
                     jnp.stack(d_bs[1:]), d_kv_norm, jnp.concatenate(d_b_norm, axis=0), jnp.stack(d_rel),
                     jnp.concatenate(d_ffn_norm[1:], axis=0), d_final, loss_tile[:1, :1]]
            small_started.append((batch, _gather_start(_pack_small(batch), "gather_start_early")))
            pending.append(small_started[-1][1][4])
    grad_x = dx.reshape(x.shape)

    landed = _reduce_wait([(send, recv, g, land) for _, send, recv, g, land in started], dx, "reduce_wait")
    reduced_units = [unit for units_, *_ in started for unit in units_]
    arrived = dict(zip(reduced_units, [pair for gs, lands in landed for pair in zip(gs, lands)]))
    ffn_units = [u for u in reduced_units if u[0].startswith("ffn_")]
    other_units = [u for u in reduced_units if not u[0].startswith("ffn_")]
    halves = [_reduce_sum(*arrived[u], place, f"reduce_sum_{u[0]}_{u[1]}") for u in ffn_units]
    ffn_send, ffn_recv, ffn_flying, ffn_token = _join_start(halves, "join_start_ffn")
    halves = [_reduce_sum(*arrived[u], place, f"reduce_sum_{u[0]}_{u[1]}", deps=(ffn_token,)) for u in other_units]
    other_send, other_recv, other_flying, other_token = _join_start(halves, "join_start_rest")
    joined = dict(zip(ffn_units, _join_wait(ffn_flying, ffn_send, ffn_recv, other_token, "join_wait_ffn")))
    reduced = {}

    (packed_e, land_e), (packed_l, land_l) = _gather_wait([s[:4] for _, s in small_started], dx, "gather_wait")
    total_e = _sum_gathered(packed_e, land_e, place[2:], "sum_small_grads_early")
    total_l = _sum_gathered(packed_l, land_l, place[2:], "sum_small_grads_late")
    total_t = _sum_devices(_gather_small(_pack_small([d_a_norm[0]]), "gather_small_grads_last"), "sum_small_grads_last")
    (e_a_norm, e_a_sgu, e_ws, e_bs, g_kv_norm, g_b_norm, g_rel, e_ffn_norm, g_final, loss) = _unpack_small(
        total_e, [a.shape for a in small_started[0][0]])
    l_a_sgu, l_ws, l_bs, l_ffn_norm = _unpack_small(total_l, [a.shape for a in small_started[1][0]])
    (t_a_norm,) = _unpack_small(total_t, [d_a_norm[0].shape])
    g_a_norm = jnp.concatenate([t_a_norm, e_a_norm], axis=0)
    g_a_sgu = jnp.concatenate([l_a_sgu, e_a_sgu], axis=0)
    g_ws = jnp.concatenate([l_ws, e_ws], axis=0)
    g_bs = jnp.concatenate([l_bs, e_bs], axis=0)
    g_ffn_norm = jnp.concatenate([l_ffn_norm, e_ffn_norm], axis=0)
    reduced["a_norm"] = lax.dynamic_slice_in_dim(g_a_norm, chip * na_w, na_w, axis=1)
    reduced["a_sgu_norm"] = lax.dynamic_slice_in_dim(g_a_sgu, chip * ns_w, ns_w, axis=1)
    reduced.update(a_w_spatial=g_ws, a_b_spatial=g_bs, kv_norm=g_kv_norm.reshape(kv_norm.shape), b_norm=g_b_norm,
                   b_rel_bias=g_rel, ffn_norm=g_ffn_norm, final_norm=g_final.reshape(final_norm.shape))

    weights = dict(a_norm=a_norm, a_w_in=a_w_in, a_sgu_norm=a_sgu_norm, a_w_spatial=a_w_spatial,
                   a_b_spatial=a_b_spatial, a_w_out=a_w_out, kv_norm=kv_norm, w_kv=w_kv, b_norm=b_norm, b_w_q=b_w_q,
                   b_rel_bias=b_rel_bias, b_w_o=b_w_o, ffn_norm=ffn_norm, ffn_w_gate_up=ffn_w_gate_up,
                   ffn_w_down=ffn_w_down, final_norm=final_norm)
    m_in = dict(a_norm=m_a_norm, a_w_in=m_a_w_in, a_sgu_norm=m_a_sgu_norm, a_w_spatial=m_a_w_spatial,
                a_b_spatial=m_a_b_spatial, a_w_out=m_a_w_out, kv_norm=m_kv_norm, w_kv=m_w_kv, b_norm=m_b_norm,
                b_w_q=m_b_w_q, b_rel_bias=m_b_rel_bias, b_w_o=m_b_w_o, ffn_norm=m_ffn_norm,
                ffn_w_gate_up=m_ffn_w_gate_up, ffn_w_down=m_ffn_w_down, final_norm=m_final_norm)
    v_in = dict(a_norm=v_a_norm, a_w_in=v_a_w_in, a_sgu_norm=v_a_sgu_norm, a_w_spatial=v_a_w_spatial,
                a_b_spatial=v_a_b_spatial, a_w_out=v_a_w_out, kv_norm=v_kv_norm, w_kv=v_w_kv, b_norm=v_b_norm,
                b_w_q=v_b_w_q, b_rel_bias=v_b_rel_bias, b_w_o=v_b_w_o, ffn_norm=v_ffn_norm,
                ffn_w_gate_up=v_ffn_w_gate_up, ffn_w_down=v_ffn_w_down, final_norm=v_final_norm)
    results = {}

    def adamw_large(key):
        as_layers = lambda a: a.reshape(stacked[key].shape)
        results[key] = _adamw_stacked(
            as_layers(weights[key]), [joined[key, layer] for layer in range(stacked[key].shape[0])],
            as_layers(m_in[key]), as_layers(v_in[key]), "adamw_" + key)

    for key in ("ffn_w_gate_up", "ffn_w_down"):
        adamw_large(key)
    joined.update(zip(other_units, _join_wait(other_flying, other_send, other_recv, results["ffn_w_down"][1],
                                              "join_wait_rest")))
    for key, w in weights.items():
        if key in stacked:
            if key not in results:
                adamw_large(key)
        else:
            g = reduced[key].reshape(w.shape)
            view = (1, w.shape[0]) if w.ndim == 1 else (-1, w.shape[-1])
            d, nm, nv = _adamw(w.reshape(view), g.reshape(view), m_in[key].reshape(view), v_in[key].reshape(view),
                               "adamw_" + key)
            results[key] = (g, d, nm, nv)
    outs = [[results[key][k].reshape(w.shape) for key, w in weights.items()] for k in range(4)]
    return (loss.reshape(()), grad_x, *outs[0], *outs[1], *outs[2], *outs[3])
```

```python
import math

import jax
import jax.numpy as jnp
from jax import lax
from jax.experimental import pallas as pl
from jax.experimental.pallas import tpu as pltpu

F32, BF16 = jnp.float32, jnp.bfloat16
MESH = pl.DeviceIdType.MESH
HIGHEST = lax.Precision.HIGHEST
NT_DIMS = (((1,), (1,)), ((), ()))
TN_DIMS = (((0,), (0,)), ((), ()))

EPS = 1e-6
CHUNK = 64
A_CHUNK = 128
A_GROUPS = 8
N_HEADS = 16
HEAD_DIM = 64
N_LEFT = 8
MAX_REL = 256
ATTN_SCALE = HEAD_DIM ** -0.5
NEG_INF = -1e30
Q_BLOCK = 4 * CHUNK
KV_PAD = N_LEFT * CHUNK
BAND = KV_PAD + Q_BLOCK
DIAGS = BAND + Q_BLOCK
TABLE_PAD = 640
HEADS_PER_BLOCK = 2
BLOCKS_PER_STEP = 8
NORMBWD_ROWS = 1024

ADAM_LR, ADAM_B1, ADAM_B2, ADAM_EPS, ADAM_WD, ADAM_STEP = 0.001, 0.9, 0.999, 1e-08, 0.01, 10

VMEM_LIMIT_BYTES = 56 * 1024 * 1024
N_CHIPS = 4
N_DEV = 8
SMALL_COLS = 1024


def _cparams(n_grid):
    return pltpu.CompilerParams(dimension_semantics=("arbitrary",) * n_grid, vmem_limit_bytes=VMEM_LIMIT_BYTES)


def _sds(shape, dtype):
    return jax.ShapeDtypeStruct(tuple(shape), dtype)


def _gelu(x):
    return x * (0.5 * (1.0 + lax.erf(x * math.sqrt(0.5))))


def _gelu_and_grad(x):
    cdf = 0.5 * (1.0 + lax.erf(x * math.sqrt(0.5)))
    return x * cdf, cdf + x * (jnp.exp(-0.5 * x * x) * (1.0 / math.sqrt(2.0 * math.pi)))


def _rms_hat(xv):
    r = lax.rsqrt(jnp.mean(xv * xv, axis=-1, keepdims=True) + EPS)
    return xv * r, r


def _rms_bwd(xhat, r, g, dy):
    dxhat = dy * g
    dx = r * (dxhat - xhat * jnp.mean(dxhat * xhat, axis=-1, keepdims=True))
    return dx, dy * xhat


def _swiglu(gate, up):
    return (gate * jax.nn.sigmoid(gate)) * up


def _row_tile(rows, cols, itemsize, cap_bytes, align):
    t = rows
    while t * cols * itemsize > cap_bytes and t % (2 * align) == 0:
        t //= 2
    return t


def _cast_slab(w, layer, chip, name, deps=()):
    _, r, C = w.shape
    tr = _row_tile(r, C, 4, 4 * 1024 * 1024, 16)

    def body(chip_ref, w_ref, *rest):
        del chip_ref
        rest[-1][...] = w_ref[...].astype(BF16)

    grid_spec = pltpu.PrefetchScalarGridSpec(
        num_scalar_prefetch=1, grid=(r // tr,),
        in_specs=[pl.BlockSpec((None, tr, C), lambda i, chip_ref: (layer, i, 0))] + [HBM_SPEC] * len(deps),
        out_specs=pl.BlockSpec((None, tr, C), lambda i, chip_ref: (chip_ref[0], i, 0)))
    return pl.pallas_call(body, name=name, grid_spec=grid_spec, out_shape=_sds((N_CHIPS, r, C), BF16),
                          compiler_params=_cparams(1))(chip, w, *deps)


def _adamw_stacked(w, gs, m, v, name):
    L, r, C = w.shape
    tr = _row_tile(r, C, 4, 2 * 1024 * 1024, 8)
    nb = r // tr

    def body(w_ref, m_ref, v_ref, *rest):
        go_ref, d_ref, nm_ref, nv_ref = rest[-4:]
        layer = pl.program_id(0)
        gv = rest[0][...]
        for k in range(1, L):
            gv = jnp.where(layer == k, rest[k][...], gv)
        mn = ADAM_B1 * m_ref[...] + (1.0 - ADAM_B1) * gv
        vn = ADAM_B2 * v_ref[...] + (1.0 - ADAM_B2) * jnp.square(gv)
        m_hat = mn / (1.0 - ADAM_B1 ** ADAM_STEP)
        v_hat = vn / (1.0 - ADAM_B2 ** ADAM_STEP)
        d_ref[...] = -ADAM_LR * (m_hat / (jnp.sqrt(v_hat) + ADAM_EPS) + ADAM_WD * w_ref[...])
        nm_ref[...] = mn
        nv_ref[...] = vn
        go_ref[...] = gv

    def grad_spec(k):
        return pl.BlockSpec((tr, C), lambda l, i: (jnp.where(l == k, i, jnp.where(l > k, nb - 1, 0)), 0))

    stacked = pl.BlockSpec((None, tr, C), lambda l, i: (l, i, 0))
    return pl.pallas_call(body, name=name, grid=(L, nb), in_specs=[stacked] * 3 + [grad_spec(k) for k in range(L)],
                          out_specs=[stacked] * 4, out_shape=[_sds((L, r, C), F32)] * 4,
                          compiler_params=_cparams(2))(w, m, v, *gs)


def _adamw(w, g, m, v, name):
    R, C = w.shape
    tr = _row_tile(R, C, 4, 1024 * 1024, 8)

    def body(w_ref, g_ref, m_ref, v_ref, d_ref, nm_ref, nv_ref):
        gv = g_ref[...]
        mn = ADAM_B1 * m_ref[...] + (1.0 - ADAM_B1) * gv
        vn = ADAM_B2 * v_ref[...] + (1.0 - ADAM_B2) * jnp.square(gv)
        m_hat = mn / (1.0 - ADAM_B1 ** ADAM_STEP)
        v_hat = vn / (1.0 - ADAM_B2 ** ADAM_STEP)
        d_ref[...] = -ADAM_LR * (m_hat / (jnp.sqrt(v_hat) + ADAM_EPS) + ADAM_WD * w_ref[...])
        nm_ref[...] = mn
        nv_ref[...] = vn

    spec = pl.BlockSpec((tr, C), lambda i: (i, 0))
    return pl.pallas_call(body, name=name, grid=(R // tr,), in_specs=[spec] * 4, out_specs=[spec] * 3,
                          out_shape=[_sds((R, C), F32)] * 3, compiler_params=_cparams(1))(w, g, m, v)


def _norm_matmul(x, g, w_g, out_dtype, name, row_sharded=False, deps=(), tm=1024, zero_rows=0):
    S, D = x.shape
    tm = min(tm, S)
    lead = zero_rows // tm
    if row_sharded:
        r, N = w_g.shape[1], w_g.shape[2]
        tn = 512
        w_spec = pl.BlockSpec((N_CHIPS, r, tn), lambda i, j: (0, 0, j))
    else:
        nsh = w_g.shape[2]
        N = N_CHIPS * nsh
        tn = next((t for t in (1024, 512) if nsh % t == 0), nsh)
        bps = nsh // tn
        w_spec = pl.BlockSpec((None, D, tn), lambda i, j: (j // bps, 0, j % bps))
    whole = not row_sharded and N <= 2048
    if whole:
        tn = N
        w_spec = pl.BlockSpec((N_CHIPS, D, nsh), lambda i, j: (0, 0, 0))

    def body(x_ref, g_ref, w_ref, *rest):
        y_ref, h_ref = rest[-2:]
        i = pl.program_id(0)

        @pl.when((i >= lead) & (pl.program_id(1) == 0))
        def _():
            xhat, _ = _rms_hat(x_ref[...])
            h_ref[...] = (xhat * g_ref[...]).astype(BF16)

        @pl.when(i >= lead)
        def _():
            if whole:
                for s in range(N_CHIPS):
                    y_ref[:, s * nsh:(s + 1) * nsh] = jnp.dot(h_ref[...], w_ref[s],
                                                             preferred_element_type=F32).astype(y_ref.dtype)
            else:
                w = w_ref[...].reshape(D, tn)
                y_ref[...] = jnp.dot(h_ref[...], w, preferred_element_type=F32).astype(y_ref.dtype)

        if lead:
            @pl.when(i < lead)
            def _():
                y_ref[...] = jnp.zeros_like(y_ref)

    rows = lambda i, j: (jnp.maximum(i - lead, 0), 0)
    return pl.pallas_call(
        body, name=name, grid=(lead + S // tm, N // tn),
        in_specs=[pl.BlockSpec((tm, D), rows), pl.BlockSpec((1, D), lambda i, j: (0, 0)), w_spec]
        + [HBM_SPEC] * len(deps),
        out_specs=[pl.BlockSpec((tm, tn), lambda i, j: (i, j)), pl.BlockSpec((tm, D), rows)],
        out_shape=[_sds((zero_rows + S, N), out_dtype), _sds((S, D), BF16)],
        compiler_params=_cparams(2))(x, g, w_g, *deps)


def _matmul_res(a, w_g, res, name, swiglu=False, deps=(), tm=256):
    S, N = res.shape
    r = w_g.shape[1]
    K = N_CHIPS * r

    def body(*refs):
        o_ref = refs[-1]
        if swiglu:
            gate_ref, up_ref, w_ref, res_ref = refs[:4]
            a_blk = _swiglu(gate_ref[...].astype(F32), up_ref[...].astype(F32)).astype(BF16)
        else:
            a_ref, w_ref, res_ref = refs[:3]
            a_blk = a_ref[...]
        o_ref[...] = res_ref[...] + jnp.dot(a_blk, w_ref[...].reshape(K, N), preferred_element_type=F32)

    a_specs, a_ops = [pl.BlockSpec((tm, K), lambda i: (i, 0))], [a]
    if swiglu:
        a_specs.append(pl.BlockSpec((tm, K), lambda i: (i, 1)))
        a_ops.append(a)
    row = pl.BlockSpec((tm, N), lambda i: (i, 0))
    return pl.pallas_call(
        body, name=name, grid=(S // tm,),
        in_specs=a_specs + [pl.BlockSpec((N_CHIPS, r, N), lambda i: (0, 0, 0)), row] + [HBM_SPEC] * len(deps),
        out_specs=row, out_shape=_sds((S, N), F32), compiler_params=_cparams(1))(*a_ops, w_g, res, *deps)


def _matmul_tn(a_ops, a_specs, a_fn, b_op, b_spec, out_spec, out_shape, acc_shape, n_outer, name, deps=(), tt=512):
    S = b_op.shape[-2]
    na = len(a_ops)
    nt = S // tt

    def body(*refs):
        a_refs, b_ref, o_ref, acc_ref = refs[:na], refs[na], refs[-2], refs[-1]
        t = pl.program_id(1)
        part = lax.dot_general(a_fn(*a_refs), b_ref[...].astype(BF16), TN_DIMS, preferred_element_type=F32)

        @pl.when(t == 0)
        def _():
            acc_ref[...] = part

        @pl.when(t > 0)
        def _():
            acc_ref[...] += part

        @pl.when(t == nt - 1)
        def _():
            o_ref[...] = acc_ref[...].reshape(o_ref.shape).astype(BF16)

    return pl.pallas_call(
        body, name=name, grid=(n_outer, nt), in_specs=list(a_specs) + [b_spec] + [HBM_SPEC] * len(deps),
        out_specs=out_spec, out_shape=_sds(out_shape, BF16), scratch_shapes=[pltpu.VMEM(acc_shape, F32)],
        compiler_params=_cparams(2))(*a_ops, b_op, *deps)


def _nt_accumulate(a_ref, w_ref, acc_ref, w2d, nk):
    k = pl.program_id(1)
    part = lax.dot_general(a_ref[...].astype(BF16), w_ref[...].reshape(w2d), NT_DIMS, preferred_element_type=F32)

    @pl.when(k == 0)
    def _():
        acc_ref[...] = part

    @pl.when(k > 0)
    def _():
        acc_ref[...] += part

    return k == nk - 1


def _nt_normbwd(dy, dy_spec, w_g, w_spec, w2d, nk, x, g, dres, name, deps=(), tm=NORMBWD_ROWS):
    S, D = x.shape
    tm = min(tm, S)

    def body(dy_ref, w_ref, x_ref, g_ref, dres_ref, *rest):
        dx_ref, dxb_ref, dg_ref, acc_ref = rest[-4:]

        @pl.when((pl.program_id(0) == 0) & (pl.program_id(1) == 0))
        def _():
            dg_ref[...] = jnp.zeros_like(dg_ref)

        last = _nt_accumulate(dy_ref, w_ref, acc_ref, w2d, nk)

        @pl.when(last)
        def _():
            xhat, r = _rms_hat(x_ref[...])
            dx, dgp = _rms_bwd(xhat, r, g_ref[...], acc_ref[...])
            total = dres_ref[...] + dx
            dx_ref[...] = total
            dxb_ref[...] = total.astype(BF16)
            dg_ref[...] += jnp.sum(dgp, axis=0, keepdims=True)

    row = pl.BlockSpec((tm, D), lambda i, k: (i, 0))
    vec = pl.BlockSpec((1, D), lambda i, k: (0, 0))
    return pl.pallas_call(
        body, name=name, grid=(S // tm, nk),
        in_specs=[dy_spec, w_spec, row, vec, row] + [HBM_SPEC] * len(deps), out_specs=[row, row, vec],
        out_shape=[_sds((S, D), F32), _sds((S, D), BF16), _sds((1, D), F32)],
        scratch_shapes=[pltpu.VMEM((tm, D), F32)], compiler_params=_cparams(2))(dy, w_g, x, g, dres, *deps)


def _nt_rows(dy, w_g, shards_per_block, out_dtype, name, deps=(), tm=1024):
    S, N = dy.shape
    tm = min(tm, S)
    r = w_g.shape[1]
    tn = shards_per_block * r

    def body(dy_ref, w_ref, *rest):
        o_ref = rest[-1]
        o_ref[...] = lax.dot_general(dy_ref[...].astype(BF16), w_ref[...].reshape(tn, N), NT_DIMS,
                                     preferred_element_type=F32).astype(o_ref.dtype)

    return pl.pallas_call(
        body, name=name, grid=(S // tm, N_CHIPS // shards_per_block),
        in_specs=[pl.BlockSpec((tm, N), lambda i, j: (i, 0)),
                  pl.BlockSpec((shards_per_block, r, N), lambda i, j: (j, 0, 0))] + [HBM_SPEC] * len(deps),
        out_specs=pl.BlockSpec((tm, tn), lambda i, j: (i, j)),
        out_shape=_sds((S, N_CHIPS * r), out_dtype), compiler_params=_cparams(2))(dy, w_g, *deps)


def _nt_swiglu_bwd(dy, w_g, gu, name, deps=(), tm=1024):
    S, N = dy.shape
    tm = min(tm, S)
    r = w_g.shape[1]
    tn = 2 * r
    F = N_CHIPS * r

    def body(dy_ref, w_ref, gate_ref, up_ref, *rest):
        o_ref = rest[-1]
        dact = lax.dot_general(dy_ref[...].astype(BF16), w_ref[...].reshape(tn, N), NT_DIMS,
                               preferred_element_type=F32)
        gate, up = gate_ref[...].astype(F32), up_ref[...].astype(F32)
        sg = jax.nn.sigmoid(gate)
        silu = gate * sg
        o_ref[0] = ((dact * up) * (sg + silu * (1.0 - sg))).astype(BF16)
        o_ref[1] = (dact * silu).astype(BF16)

    return pl.pallas_call(
        body, name=name, grid=(2, S // tm),
        in_specs=[pl.BlockSpec((tm, N), lambda j, i: (i, 0)),
                  pl.BlockSpec((2, r, N), lambda j, i: (j, 0, 0)),
                  pl.BlockSpec((tm, tn), lambda j, i: (i, j)),
                  pl.BlockSpec((tm, tn), lambda j, i: (i, 2 + j))] + [HBM_SPEC] * len(deps),
        out_specs=pl.BlockSpec((2, tm, tn), lambda j, i: (0, i, j)),
        out_shape=_sds((2, S, F), BF16), compiler_params=_cparams(2))(dy, w_g, gu, gu, *deps)


def _chunk_causal_mask(transposed):
    i = lax.broadcasted_iota(jnp.int32, (A_CHUNK, A_CHUNK), 0) // CHUNK
    j = lax.broadcasted_iota(jnp.int32, (A_CHUNK, A_CHUNK), 1) // CHUNK
    return ((i <= j) if transposed else (i >= j)).astype(F32)


def _sgu_fwd(zpre, g_sgu, ws, bs_t, name, deps=()):
    S, F2 = zpre.shape
    F = F2 // 2
    gd = F // A_GROUPS

    windows = 2
    rows = windows * A_CHUNK

    def body(zu_ref, zv_ref, g_ref, ws_ref, b_ref, *rest):
        o_ref = rest[-1]
        vhat, _ = _rms_hat(_gelu(zv_ref[...].astype(F32)))
        vn = (vhat * g_ref[...]).astype(BF16)
        u = _gelu(zu_ref[...].astype(F32))
        mask = _chunk_causal_mask(False)
        for gi in range(A_GROUPS):
            sl = slice(gi * gd, (gi + 1) * gd)
            wm = (ws_ref[gi] * mask).astype(BF16)
            for w in range(windows):
                win = slice(w * A_CHUNK, (w + 1) * A_CHUNK)
                vs = jnp.dot(wm, vn[win, sl], preferred_element_type=F32) + b_ref[:, gi:gi + 1]
                o_ref[win, sl] = (u[win, sl] * vs).astype(BF16)

    return pl.pallas_call(
        body, name=name, grid=(S // rows,),
        in_specs=[pl.BlockSpec((rows, F), lambda i: (i, 0)),
                  pl.BlockSpec((rows, F), lambda i: (i, 1)),
                  pl.BlockSpec((1, F), lambda i: (0, 0)),
                  pl.BlockSpec((A_GROUPS, A_CHUNK, A_CHUNK), lambda i: (0, 0, 0)),
                  pl.BlockSpec((A_CHUNK, A_GROUPS), lambda i: (0, 0))] + [HBM_SPEC] * len(deps),
        out_specs=pl.BlockSpec((rows, F), lambda i: (i, 0)),
        out_shape=_sds((S, F), BF16), compiler_params=_cparams(1))(zpre, zpre, g_sgu, ws, bs_t, *deps)


def _sgu_bwd(zpre, duv, g_sgu, ws, ws_t, bs_t, name):
    S, F2 = zpre.shape
    F = F2 // 2
    gd = F // A_GROUPS

    def body(zu_ref, zv_ref, duv_ref, g_ref, ws_ref, wst_ref, b_ref, dz_ref, dg_ref, dws_ref, dbs_ref, dvn_ref):
        @pl.when(pl.program_id(0) == 0)
        def _():
            dg_ref[...] = jnp.zeros_like(dg_ref)
            dws_ref[...] = jnp.zeros_like(dws_ref)
            dbs_ref[...] = jnp.zeros_like(dbs_ref)

        gv = g_ref[...]
        u, u_grad = _gelu_and_grad(zu_ref[...].astype(F32))
        v, v_grad = _gelu_and_grad(zv_ref[...].astype(F32))
        vhat, r = _rms_hat(v)
        vn = (vhat * gv).astype(BF16)
        duv_v = duv_ref[...].astype(F32)
        dvs = duv_v * u
        dvs_b = dvs.astype(BF16)
        mask = _chunk_causal_mask(False)
        mask_t = _chunk_causal_mask(True)
        for gi in range(A_GROUPS):
            sl = slice(gi * gd, (gi + 1) * gd)
            wm = (ws_ref[gi] * mask).astype(BF16)
            vs = jnp.dot(wm, vn[:, sl], preferred_element_type=F32) + b_ref[:, gi:gi + 1]
            dz_ref[:, sl] = ((duv_v[:, sl] * vs) * u_grad[:, sl]).astype(BF16)
            dws_ref[gi] += lax.dot_general(dvs_b[:, sl], vn[:, sl], NT_DIMS, preferred_element_type=F32) * mask
            dbs_ref[gi] += jnp.broadcast_to(jnp.sum(dvs[:, sl], axis=1, keepdims=True), (A_CHUNK, A_CHUNK))
            wm_t = (wst_ref[gi] * mask_t).astype(BF16)
            dvn_ref[:, sl] = jnp.dot(wm_t, dvs_b[:, sl], preferred_element_type=F32)
        dv, dg_part = _rms_bwd(vhat, r, gv, dvn_ref[...])
        dg_ref[...] += jnp.sum(dg_part, axis=0, keepdims=True)
        dz_ref[:, F:] = (dv * v_grad).astype(BF16)

    blk = pl.BlockSpec((A_CHUNK, F), lambda i: (i, 0))
    const3 = pl.BlockSpec((A_GROUPS, A_CHUNK, A_CHUNK), lambda i: (0, 0, 0))
    return pl.pallas_call(
        body, name=name, grid=(S // A_CHUNK,),
        in_specs=[blk, pl.BlockSpec((A_CHUNK, F), lambda i: (i, 1)), blk,
                  pl.BlockSpec((1, F), lambda i: (0, 0)), const3, const3,
                  pl.BlockSpec((A_CHUNK, A_GROUPS), lambda i: (0, 0))],
        out_specs=[pl.BlockSpec((A_CHUNK, F2), lambda i: (i, 0)), pl.BlockSpec((1, F), lambda i: (0, 0)),
                   const3, const3],
        out_shape=[_sds((S, F2), BF16), _sds((1, F), F32), _sds((A_GROUPS, A_CHUNK, A_CHUNK), F32),
                   _sds((A_GROUPS, A_CHUNK, A_CHUNK), F32)],
        scratch_shapes=[pltpu.VMEM((A_CHUNK, F), F32)],
        compiler_params=_cparams(1))(zpre, zpre, duv, g_sgu, ws, ws_t, bs_t)


def _toeplitz_one_hot():
    row = lax.broadcasted_iota(jnp.int32, (TABLE_PAD, DIAGS), 0)
    j = lax.broadcasted_iota(jnp.int32, (TABLE_PAD, DIAGS), 1)
    idx = jnp.clip(KV_PAD + Q_BLOCK - j, -MAX_REL, MAX_REL) + MAX_REL
    return (row == idx).astype(F32)


def _rel_bias_fwd(table, name):
    H = table.shape[0]

    def body(t_ref, o_ref):
        diag = jnp.dot(t_ref[...], _toeplitz_one_hot(), precision=HIGHEST, preferred_element_type=F32)
        q_chunk = lax.broadcasted_iota(jnp.int32, (Q_BLOCK, BAND), 0) // CHUNK
        k_chunk = lax.broadcasted_iota(jnp.int32, (Q_BLOCK, BAND), 1) // CHUNK
        unseen = jnp.where((k_chunk >= q_chunk) & (k_chunk <= q_chunk + N_LEFT), 0.0, NEG_INF)
        for h in range(H):
            rows = jnp.broadcast_to(diag[h:h + 1, :], (Q_BLOCK, DIAGS))
            o_ref[h] = pltpu.roll(rows, DIAGS - Q_BLOCK, 1, stride=1, stride_axis=0)[:, :BAND] + unseen

    return pl.pallas_call(body, name=name, out_shape=_sds((H, Q_BLOCK, BAND), F32),
                          compiler_params=pltpu.CompilerParams(vmem_limit_bytes=VMEM_LIMIT_BYTES))(table)


def _rel_bias_bwd(dbias, name):
    H = dbias.shape[0]

    def body(d_ref, o_ref):
        def step(r, acc):
            row = d_ref[:, pl.ds(r, 1), :].reshape(H, BAND)
            row = jnp.concatenate([row, jnp.zeros((H, DIAGS - BAND), F32)], axis=1)
            return acc + pltpu.roll(row, Q_BLOCK - r, 1)

        diag = lax.fori_loop(0, Q_BLOCK, step, jnp.zeros((H, DIAGS), F32))
        o_ref[...] = lax.dot_general(diag, _toeplitz_one_hot(), NT_DIMS, precision=HIGHEST,
                                     preferred_element_type=F32)

    return pl.pallas_call(body, name=name, out_shape=_sds((H, TABLE_PAD), F32),
                          compiler_params=pltpu.CompilerParams(vmem_limit_bytes=VMEM_LIMIT_BYTES))(dbias)


def _head_rows(t):
    lane = lax.broadcasted_iota(jnp.int32, t.shape, 1)
    zero = jnp.zeros_like(t)
    return jnp.concatenate([jnp.where(lane < HEAD_DIM, t, zero), jnp.where(lane >= HEAD_DIM, t, zero)], axis=0)


def _head_lanes(t2):
    lane = lax.broadcasted_iota(jnp.int32, (Q_BLOCK, t2.shape[1]), 1)
    return jnp.where(lane < HEAD_DIM, t2[:Q_BLOCK], t2[Q_BLOCK:])


def _attn_probs(q2, kb, bias2, block):
    kj = lax.broadcasted_iota(jnp.int32, (1, BAND), 1)
    before_start = jnp.where(block * Q_BLOCK + kj - KV_PAD >= 0, 0.0, NEG_INF)
    s = lax.dot_general(q2 * ATTN_SCALE, kb, NT_DIMS, preferred_element_type=F32) + bias2 + before_start
    e = jnp.exp(s - jnp.max(s, axis=-1, keepdims=True))
    return e / jnp.sum(e, axis=-1, keepdims=True)


def _attn_specs(S):
    lanes = HEADS_PER_BLOCK * HEAD_DIM
    rows = S + KV_PAD
    per_step = min(BLOCKS_PER_STEP, S // Q_BLOCK)
    q_spec = pl.BlockSpec((per_step * Q_BLOCK, lanes), lambda h, i: (i, h))
    k_spec = pl.BlockSpec((rows, lanes), lambda h, i: (0, h))
    v_spec = pl.BlockSpec((rows, lanes), lambda h, i: (0, N_HEADS // HEADS_PER_BLOCK + h))
    b_spec = pl.BlockSpec((HEADS_PER_BLOCK, Q_BLOCK, BAND), lambda h, i: (h, 0, 0))
    return q_spec, k_spec, v_spec, b_spec, per_step


def _attn_fwd(q, kvp, bias, name, deps=()):
    S, HD = q.shape
    q_spec, k_spec, v_spec, b_spec, per_step = _attn_specs(S)

    def body(q_ref, k_ref, v_ref, b_ref, *rest):
        o_ref = rest[-1]
        for b in range(per_step):
            block = pl.program_id(1) * per_step + b
            rows = slice(b * Q_BLOCK, (b + 1) * Q_BLOCK)
            band = pl.ds(pl.multiple_of(block * Q_BLOCK, Q_BLOCK), BAND)
            p = _attn_probs(_head_rows(q_ref[rows, :]), k_ref[band, :], b_ref[...].reshape(2 * Q_BLOCK, BAND), block)
            o2 = jnp.dot(p.astype(BF16), v_ref[band, :], preferred_element_type=F32)
            o_ref[rows, :] = _head_lanes(o2).astype(BF16)

    return pl.pallas_call(
        body, name=name, grid=(N_HEADS // HEADS_PER_BLOCK, S // (per_step * Q_BLOCK)),
        in_specs=[q_spec, k_spec, v_spec, b_spec] + [HBM_SPEC] * len(deps), out_specs=q_spec,
        out_shape=_sds((S, HD), BF16), compiler_params=_cparams(2))(q, kvp, kvp, bias, *deps)


def _attn_bwd(q, kvp, bias, do, dkv_prev, name):
    S, HD = q.shape
    lanes = HEADS_PER_BLOCK * HEAD_DIM
    q_spec, k_spec, v_spec, b_spec, per_step = _attn_specs(S)
    dkv_spec = pl.BlockSpec((2, S + KV_PAD, lanes), lambda h, i: (0, 0, h))
    prev = [] if dkv_prev is None else [dkv_prev]
    n_steps = S // (per_step * Q_BLOCK)

    def body(q_ref, k_ref, v_ref, b_ref, do_ref, *rest):
        dq_ref, dkv_ref, db_ref = rest[len(prev):len(prev) + 3]

        @pl.when(pl.program_id(1) == 0)
        def _():
            dkv_ref[...] = rest[0][...] if prev else jnp.zeros_like(dkv_ref)
            db_ref[...] = jnp.zeros_like(db_ref)

        db = jnp.zeros((2 * Q_BLOCK, BAND), F32)
        for b in range(per_step):
            block = pl.program_id(1) * per_step + b
            rows = slice(b * Q_BLOCK, (b + 1) * Q_BLOCK)
            band = pl.ds(pl.multiple_of(block * Q_BLOCK, Q_BLOCK), BAND)
            kb, vb = k_ref[band, :], v_ref[band, :]
            q2, do2 = _head_rows(q_ref[rows, :]), _head_rows(do_ref[rows, :])
            p = _attn_probs(q2, kb, b_ref[...].reshape(2 * Q_BLOCK, BAND), block)
            dp = lax.dot_general(do2, vb, NT_DIMS, preferred_element_type=F32)
            ds = p * (dp - jnp.sum(dp * p, axis=-1, keepdims=True))
            db = db + ds
            ds_b = (ds * ATTN_SCALE).astype(BF16)
            dq_ref[rows, :] = _head_lanes(jnp.dot(ds_b, kb, preferred_element_type=F32)).astype(BF16)
            dkv_ref[0, band, :] += lax.dot_general(ds_b, q2, TN_DIMS, preferred_element_type=F32)
            dkv_ref[1, band, :] += lax.dot_general(p.astype(BF16), do2, TN_DIMS, preferred_element_type=F32)
        db_ref[...] += db.reshape(HEADS_PER_BLOCK, Q_BLOCK, BAND)

        if prev:
            @pl.when(pl.program_id(1) == n_steps - 1)
            def _():
                rest[-1][...] = dkv_ref[:, KV_PAD:, :].astype(BF16)

    return pl.pallas_call(
        body, name=name, grid=(N_HEADS // HEADS_PER_BLOCK, n_steps),
        in_specs=[q_spec, k_spec, v_spec, b_spec, q_spec] + [dkv_spec] * len(prev),
        out_specs=[q_spec, dkv_spec, b_spec] + [pl.BlockSpec((2, S, lanes), lambda h, i: (0, 0, h))] * len(prev),
        out_shape=[_sds((S, HD), BF16), _sds((2, S + KV_PAD, HD), F32), _sds((N_HEADS, Q_BLOCK, BAND), F32)]
        + [_sds((2, S, HD), BF16)] * len(prev),
        compiler_params=_cparams(2))(q, kvp, kvp, bias, do, *prev)


def _loss_head(x, g, target, name, tm=512):
    S, D = x.shape

    def body(x_ref, g_ref, t_ref, loss_ref, dx_ref, dxb_ref, dg_ref):
        @pl.when(pl.program_id(0) == 0)
        def _():
            loss_ref[...] = jnp.zeros_like(loss_ref)
            dg_ref[...] = jnp.zeros_like(dg_ref)

        xhat, r = _rms_hat(x_ref[...])
        gv = g_ref[...]
        err = xhat * gv - t_ref[...]
        loss_ref[...] += 0.5 * jnp.sum(jnp.mean(err * err, axis=-1, keepdims=True))
        dx, dgp = _rms_bwd(xhat, r, gv, err * (1.0 / D))
        dx_ref[...] = dx
        dxb_ref[...] = dx.astype(BF16)
        dg_ref[...] += jnp.sum(dgp, axis=0, keepdims=True)

    row = pl.BlockSpec((tm, D), lambda i: (i, 0))
    vec = pl.BlockSpec((1, D), lambda i: (0, 0))
    return pl.pallas_call(
        body, name=name, grid=(S // tm,), in_specs=[row, vec, row],
        out_specs=[pl.BlockSpec((8, 128), lambda i: (0, 0)), row, row, vec],
        out_shape=[_sds((8, 128), F32), _sds((S, D), F32), _sds((S, D), BF16), _sds((1, D), F32)],
        compiler_params=_cparams(1))(x, g, target)


def _place():
    x, y, c = lax.axis_index("x"), lax.axis_index("y"), lax.axis_index("c")
    chips = [(1 - x, y), (x, 1 - y), (1 - x, 1 - y)]
    return x, y, c, chips


def _half_rows(c, r):
    return pl.ds(pl.multiple_of(c * (r // 2), 8), r // 2)


HBM_SPEC = pl.BlockSpec(memory_space=pl.ANY)


STRICT_HBM_SPEC = pl.BlockSpec(memory_space=pltpu.HBM)
SEM_SPEC = pl.BlockSpec(memory_space=pltpu.SEMAPHORE)
EFFECT = pltpu.SideEffectType.DATAFLOW_SIDE_EFFECTING


def _peers(x, y, c):
    out = []
    for k in range(1, N_DEV):
        px, py, pc = (x + ((k >> 2) & 1)) % 2, (y + ((k >> 1) & 1)) % 2, (c + (k & 1)) % 2
        out.append(((px, py, pc), 2 * px + py, pc, 4 * px + 2 * py + pc))
    return out


def _token_spec():
    return pl.BlockSpec(memory_space=pltpu.VMEM)


def _hbm(a):
    return pltpu.with_memory_space_constraint(a, pltpu.HBM)


def _slab_half(ref, chip, core):
    return ref.at[2 * chip[0] + chip[1], _half_rows(core, ref.shape[1]), :]


def _allgather_start(slabs, name):
    n = len(slabs)

    def body(*refs):
        src, send, recv, token = refs[:n], refs[n], refs[n + 1], refs[-1]
        x, y, c, chips = _place()
        for a in range(n):
            own = _slab_half(src[a], (x, y), c)
            for j, chip in enumerate(chips):
                pltpu.make_async_remote_copy(src_ref=own, dst_ref=own, send_sem=send.at[3 * a + j], recv_sem=recv.at[3 * a + j],
                                             device_id=(*chip, c), device_id_type=MESH).start()
        token[...] = jnp.zeros_like(token)

    sems = pltpu.SemaphoreType.DMA((3 * n,))
    send, recv, *flying, token = pl.pallas_call(
        body, name=name, in_specs=[STRICT_HBM_SPEC] * n,
        out_shape=(sems, sems, *[pltpu.HBM(s.shape, s.dtype) for s in slabs], _sds((8, 128), F32)),
        out_specs=(SEM_SPEC, SEM_SPEC, *[STRICT_HBM_SPEC] * n, _token_spec()),
        input_output_aliases={a: a + 2 for a in range(n)},
        compiler_params=pltpu.CompilerParams(has_side_effects=EFFECT))(*[_hbm(s) for s in slabs])
    return send, recv, flying, token


def _allgather_relay(flying, send, recv, first, after, name):
    n = len(flying)

    def body(*refs):
        src, send_ref, recv_ref = refs[:n], refs[n], refs[n + 1]
        send2, recv2, token = refs[n + 3], refs[n + 4], refs[-1]
        token[...] = jnp.zeros_like(token)
        x, y, c, chips = _place()
        for a in range(n):
            for j, chip in enumerate(chips):
                cp = pltpu.make_async_remote_copy(
                    src_ref=_slab_half(src[a], (x, y), c), dst_ref=_slab_half(src[a], chip, c),
                    send_sem=send_ref.at[3 * (first + a) + j], recv_sem=recv_ref.at[3 * (first + a) + j],
                    device_id=(*chip, c), device_id_type=MESH)
                cp.wait_send()
                cp.wait_recv()
        for a in range(n):
            for j, chip in enumerate(chips):
                landed = _slab_half(src[a], chip, c)
                pltpu.make_async_remote_copy(src_ref=landed, dst_ref=landed, send_sem=send2.at[3 * a + j],
                                             recv_sem=recv2.at[3 * a + j], device_id=(x, y, 1 - c),
                                             device_id_type=MESH).start()

    sems = pltpu.SemaphoreType.DMA((3 * n,))
    send2, recv2, *relayed, token = pl.pallas_call(
        body, name=name, in_specs=[STRICT_HBM_SPEC] * n + [SEM_SPEC, SEM_SPEC, HBM_SPEC],
        out_shape=(sems, sems, *[pltpu.HBM(s.shape, s.dtype) for s in flying], _sds((8, 128), F32)),
        out_specs=(SEM_SPEC, SEM_SPEC, *[STRICT_HBM_SPEC] * n, _token_spec()),
        input_output_aliases={a: a + 2 for a in range(n)},
        compiler_params=pltpu.CompilerParams(has_side_effects=EFFECT))(*flying, send, recv, after)
    return send2, recv2, relayed, token


def _allgather_wait(relayed, send2, recv2, after, name):
    n = len(relayed)

    def body(*refs):
        src, send_ref, recv_ref = refs[:n], refs[n], refs[n + 1]
        x, y, c, chips = _place()
        for a in range(n):
            for j, chip in enumerate(chips):
                cp = pltpu.make_async_remote_copy(
                    src_ref=_slab_half(src[a], chip, c), dst_ref=_slab_half(src[a], chip, 1 - c),
                    send_sem=send_ref.at[3 * a + j], recv_sem=recv_ref.at[3 * a + j],
                    device_id=(x, y, 1 - c), device_id_type=MESH)
                cp.wait_send()
                cp.wait_recv()

    return pl.pallas_call(
        body, name=name, in_specs=[STRICT_HBM_SPEC] * n + [SEM_SPEC, SEM_SPEC, HBM_SPEC],
        out_shape=tuple(pltpu.HBM(s.shape, s.dtype) for s in relayed), out_specs=tuple([STRICT_HBM_SPEC] * n),
        input_output_aliases={a: a for a in range(n)},
        compiler_params=pltpu.CompilerParams(has_side_effects=EFFECT))(*relayed, send2, recv2, after)


def _allgather_small(small, name):
    def body(sm, osm, send, recv, local):
        x, y, c, chips = _place()
        own = pltpu.make_async_copy(sm, osm.at[2 * x + y], local)
        own.start()
        cps = [pltpu.make_async_remote_copy(src_ref=sm, dst_ref=osm.at[2 * x + y], send_sem=send.at[j],
                                            recv_sem=recv.at[j], device_id=(*chip, c), device_id_type=MESH)
               for j, chip in enumerate(chips)]
        for cp in cps:
            cp.start()
        for j, chip in enumerate(chips):
            got = osm.at[2 * chip[0] + chip[1]]
            pltpu.make_async_remote_copy(src_ref=got, dst_ref=got, send_sem=send.at[j], recv_sem=recv.at[j],
                                         device_id=(x, y, c), device_id_type=MESH).wait_recv()
        for cp in cps:
            cp.wait_send()
        own.wait()

    return pl.pallas_call(
        body, name=name, in_specs=[pl.BlockSpec(memory_space=pltpu.VMEM)], out_specs=HBM_SPEC,
        out_shape=_sds((N_CHIPS, *small.shape), small.dtype),
        scratch_shapes=[pltpu.SemaphoreType.DMA((3,)), pltpu.SemaphoreType.DMA((3,)), pltpu.SemaphoreType.DMA])(small)


def _reduce_start(grads, name):
    n = len(grads)

    def body(*refs):
        src, land, send, recv, token = refs[:n], refs[n:2 * n], refs[2 * n], refs[2 * n + 1], refs[-1]
        x, y, c, _ = _place()
        me = 4 * x + 2 * y + c
        for a in range(n):
            for k, (peer, p_chip, p_core, _) in enumerate(_peers(x, y, c)):
                pltpu.make_async_remote_copy(
                    src_ref=src[a].at[p_chip, _half_rows(p_core, src[a].shape[1]), :], dst_ref=land[a].at[me],
                    send_sem=send.at[(N_DEV - 1) * a + k], recv_sem=recv.at[(N_DEV - 1) * a + k],
                    device_id=peer, device_id_type=MESH).start()
        token[...] = jnp.zeros_like(token)

    lands = [lax.empty((N_DEV, g.shape[1] // 2, g.shape[2]), BF16) for g in grads]
    sems = pltpu.SemaphoreType.DMA(((N_DEV - 1) * n,))
    shapes = [pltpu.HBM(a.shape, a.dtype) for a in grads + lands]
    send, recv, *flying, token = pl.pallas_call(
        body, name=name, in_specs=[STRICT_HBM_SPEC] * (2 * n),
        out_shape=(sems, sems, *shapes, _sds((8, 128), F32)),
        out_specs=(SEM_SPEC, SEM_SPEC, *[STRICT_HBM_SPEC] * (2 * n), _token_spec()),
        input_output_aliases={a: a + 2 for a in range(2 * n)},
        compiler_params=pltpu.CompilerParams(has_side_effects=EFFECT))(*[_hbm(a) for a in grads + lands])
    return send, recv, flying[:n], flying[n:], token


def _reduce_wait(started, after, name):
    sizes = [len(grads) for _, _, grads, _ in started]
    n_arr = 2 * sum(sizes)

    def body(*refs):
        x, y, c, _ = _place()
        at = 0
        for s, n in enumerate(sizes):
            src, land = refs[at:at + n], refs[at + n:at + 2 * n]
            send_ref, recv_ref = refs[n_arr + 2 * s], refs[n_arr + 2 * s + 1]
            at += 2 * n
            for a in range(n):
                for k, (peer, p_chip, p_core, p_dev) in enumerate(_peers(x, y, c)):
                    cp = pltpu.make_async_remote_copy(
                        src_ref=src[a].at[p_chip, _half_rows(p_core, src[a].shape[1]), :], dst_ref=land[a].at[p_dev],
                        send_sem=send_ref.at[(N_DEV - 1) * a + k], recv_sem=recv_ref.at[(N_DEV - 1) * a + k],
                        device_id=peer, device_id_type=MESH)
                    cp.wait_send()
                    cp.wait_recv()

    arrays, sems = [], []
    for send, recv, grads, lands in started:
        arrays += list(grads) + list(lands)
        sems += [send, recv]
    out = pl.pallas_call(
        body, name=name, in_specs=[STRICT_HBM_SPEC] * n_arr + [SEM_SPEC] * len(sems) + [HBM_SPEC],
        out_shape=tuple(pltpu.HBM(a.shape, a.dtype) for a in arrays), out_specs=tuple([STRICT_HBM_SPEC] * n_arr),
        input_output_aliases={a: a for a in range(n_arr)},
        compiler_params=pltpu.CompilerParams(has_side_effects=EFFECT))(*arrays, *sems, after)
    result, at = [], 0
    for n in sizes:
        result.append((out[at:at + n], out[at + n:at + 2 * n]))
        at += 2 * n
    return result


def _reduce_sum(grad, land, place, name, deps=()):
    _, r2, C = land.shape
    tr = _row_tile(r2, C, 4, 2 * 1024 * 1024, 16)
    nb = r2 // tr

    def body(place_ref, own_ref, *rest):
        del place_ref
        acc = own_ref[...].astype(F32)
        for ref in rest[:N_DEV - 1]:
            acc = acc + ref[...].astype(F32)
        rest[-1][...] = acc

    def from_dev(k):
        return pl.BlockSpec((None, tr, C), lambda i, place_ref: ((place_ref[2] + k) % N_DEV, i, 0))

    grid_spec = pltpu.PrefetchScalarGridSpec(
        num_scalar_prefetch=1, grid=(nb,),
        in_specs=[pl.BlockSpec((None, tr, C), lambda i, place_ref: (place_ref[0], place_ref[1] * nb + i, 0))]
        + [from_dev(k) for k in range(1, N_DEV)] + [HBM_SPEC] * len(deps),
        out_specs=pl.BlockSpec((tr, C), lambda i, place_ref: (place_ref[1] * nb + i, 0)))
    return pl.pallas_call(body, name=name, grid_spec=grid_spec, out_shape=_sds((2 * r2, C), F32),
                          compiler_params=_cparams(1))(place, grad, *[land] * (N_DEV - 1), *deps)


def _join_start(halves, name):
    n = len(halves)

    def body(*refs):
        src, send, recv, token = refs[:n], refs[n], refs[n + 1], refs[-1]
        x, y, c, _ = _place()
        for w in range(n):
            mine = src[w].at[_half_rows(c, src[w].shape[0]), :]
            pltpu.make_async_remote_copy(src_ref=mine, dst_ref=mine, send_sem=send.at[w], recv_sem=recv.at[w],
                                         device_id=(x, y, 1 - c), device_id_type=MESH).start()
        token[...] = jnp.zeros_like(token)

    sems = pltpu.SemaphoreType.DMA((n,))
    send, recv, *flying, token = pl.pallas_call(
        body, name=name, in_specs=[STRICT_HBM_SPEC] * n,
        out_shape=(sems, sems, *[pltpu.HBM(a.shape, a.dtype) for a in halves], _sds((8, 128), F32)),
        out_specs=(SEM_SPEC, SEM_SPEC, *[STRICT_HBM_SPEC] * n, _token_spec()),
        input_output_aliases={w: w + 2 for w in range(n)},
        compiler_params=pltpu.CompilerParams(has_side_effects=EFFECT))(*[_hbm(a) for a in halves])
    return send, recv, flying, token


def _join_wait(flying, send, recv, after, name):
    n = len(flying)

    def body(*refs):
        src, send_ref, recv_ref = refs[:n], refs[n], refs[n + 1]
        x, y, c, _ = _place()
        for w in range(n):
            cp = pltpu.make_async_remote_copy(
                src_ref=src[w].at[_half_rows(c, src[w].shape[0]), :],
                dst_ref=src[w].at[_half_rows(1 - c, src[w].shape[0]), :], send_sem=send_ref.at[w],
                recv_sem=recv_ref.at[w], device_id=(x, y, 1 - c),
                device_id_type=MESH)
            cp.wait_send()
            cp.wait_recv()

    return pl.pallas_call(
        body, name=name, in_specs=[STRICT_HBM_SPEC] * n + [SEM_SPEC, SEM_SPEC, HBM_SPEC],
        out_shape=tuple(pltpu.HBM(a.shape, a.dtype) for a in flying), out_specs=tuple([STRICT_HBM_SPEC] * n),
        input_output_aliases={w: w for w in range(n)},
        compiler_params=pltpu.CompilerParams(has_side_effects=EFFECT))(*flying, send, recv, after)


def _gather_small(packed, name):
    def body(p_ref, out, send, recv, local):
        x, y, c, _ = _place()
        me = 4 * x + 2 * y + c
        own = pltpu.make_async_copy(p_ref, out.at[me], local)
        own.start()
        cps = []
        for k in range(1, N_DEV):
            fx, fy, fc = (k >> 2) & 1, (k >> 1) & 1, k & 1
            peer = ((x + fx) % 2, (y + fy) % 2, (c + fc) % 2)
            cps.append(pltpu.make_async_remote_copy(src_ref=p_ref, dst_ref=out.at[me], send_sem=send.at[k - 1],
                                                    recv_sem=recv.at[k - 1], device_id=peer, device_id_type=MESH))
        for cp in cps:
            cp.start()
        for k in range(1, N_DEV):
            fx, fy, fc = (k >> 2) & 1, (k >> 1) & 1, k & 1
            src = out.at[4 * ((x + fx) % 2) + 2 * ((y + fy) % 2) + (c + fc) % 2]
            pltpu.make_async_remote_copy(src_ref=src, dst_ref=src, send_sem=send.at[k - 1], recv_sem=recv.at[k - 1],
                                         device_id=(x, y, c), device_id_type=MESH).wait_recv()
        for cp in cps:
            cp.wait_send()
        own.wait()

    return pl.pallas_call(
        body, name=name, in_specs=[pl.BlockSpec(memory_space=pltpu.VMEM)], out_specs=HBM_SPEC,
        out_shape=_sds((N_DEV, *packed.shape), F32),
        scratch_shapes=[pltpu.SemaphoreType.DMA((N_DEV - 1,)), pltpu.SemaphoreType.DMA((N_DEV - 1,)),
                        pltpu.SemaphoreType.DMA])(packed)


def _sum_devices(gathered, name):
    _, R, C = gathered.shape

    def body(g_ref, o_ref):
        acc = g_ref[0]
        for d in range(1, N_DEV):
            acc = acc + g_ref[d]
        o_ref[...] = acc

    tr = 8
    return pl.pallas_call(
        body, name=name, grid=(R // tr,), in_specs=[pl.BlockSpec((N_DEV, tr, C), lambda i: (0, i, 0))],
        out_specs=pl.BlockSpec((tr, C), lambda i: (i, 0)), out_shape=_sds((R, C), F32),
        compiler_params=_cparams(1))(gathered)


def _gather_start(packed, name):
    def body(src, land, send, recv, *rest):
        x, y, c, _ = _place()
        for k, (peer, _, _, _) in enumerate(_peers(x, y, c)):
            pltpu.make_async_remote_copy(src_ref=src, dst_ref=land.at[4 * x + 2 * y + c], send_sem=send.at[k],
                                         recv_sem=recv.at[k], device_id=peer, device_id_type=MESH).start()
        rest[-1][...] = jnp.zeros_like(rest[-1])

    land = lax.empty((N_DEV, *packed.shape), F32)
    sems = pltpu.SemaphoreType.DMA((N_DEV - 1,))
    return pl.pallas_call(
        body, name=name, in_specs=[STRICT_HBM_SPEC] * 2,
        out_shape=(sems, sems, pltpu.HBM(packed.shape, F32), pltpu.HBM(land.shape, F32), _sds((8, 128), F32)),
        out_specs=(SEM_SPEC, SEM_SPEC, STRICT_HBM_SPEC, STRICT_HBM_SPEC, _token_spec()),
        input_output_aliases={0: 2, 1: 3},
        compiler_params=pltpu.CompilerParams(has_side_effects=EFFECT))(_hbm(packed), _hbm(land))


def _gather_wait(started, after, name):
    n = len(started)

    def body(*refs):
        x, y, c, _ = _place()
        for s in range(n):
            src, land, send, recv = refs[2 * s], refs[2 * s + 1], refs[2 * n + 2 * s], refs[2 * n + 2 * s + 1]
            for k, (peer, _, _, p_dev) in enumerate(_peers(x, y, c)):
                cp = pltpu.make_async_remote_copy(src_ref=src, dst_ref=land.at[p_dev], send_sem=send.at[k],
                                                  recv_sem=recv.at[k], device_id=peer,
                                                  device_id_type=MESH)
                cp.wait_send()
                cp.wait_recv()

    arrays = [a for _, _, packed, land in started for a in (packed, land)]
    sems = [s for send, recv, _, _ in started for s in (send, recv)]
    out = pl.pallas_call(
        body, name=name, in_specs=[STRICT_HBM_SPEC] * (2 * n) + [SEM_SPEC] * (2 * n) + [HBM_SPEC],
        out_shape=tuple(pltpu.HBM(a.shape, a.dtype) for a in arrays), out_specs=tuple([STRICT_HBM_SPEC] * (2 * n)),
        input_output_aliases={a: a for a in range(2 * n)},
        compiler_params=pltpu.CompilerParams(has_side_effects=EFFECT))(*arrays, *sems, after)
    return [(out[2 * s], out[2 * s + 1]) for s in range(n)]


def _sum_gathered(packed, land, device, name):
    R, C = packed.shape
    tr = R

    def body(dev_ref, own_ref, *rest):
        me = dev_ref[0]
        acc = None
        for d in range(N_DEV):
            term = jnp.where(me == d, own_ref[...], rest[d][...])
            acc = term if acc is None else acc + term
        rest[-1][...] = acc

    def slab(d):
        return pl.BlockSpec((None, tr, C), lambda i, dev_ref: (jnp.where(dev_ref[0] == d, (d + 1) % N_DEV, d), i, 0))

    grid_spec = pltpu.PrefetchScalarGridSpec(
        num_scalar_prefetch=1, grid=(R // tr,),
        in_specs=[pl.BlockSpec((tr, C), lambda i, dev_ref: (i, 0))] + [slab(d) for d in range(N_DEV)],
        out_specs=pl.BlockSpec((tr, C), lambda i, dev_ref: (i, 0)))
    return pl.pallas_call(body, name=name, grid_spec=grid_spec, out_shape=_sds((R, C), F32),
                          compiler_params=_cparams(1))(device, packed, *[land] * N_DEV)


def _pack_small(arrays):
    rows = []
    for a in arrays:
        flat = a.reshape(-1)
        pad = (-flat.shape[0]) % SMALL_COLS
        rows.append(jnp.pad(flat, (0, pad)).reshape(-1, SMALL_COLS))
    packed = jnp.concatenate(rows, axis=0)
    return jnp.pad(packed, ((0, (-packed.shape[0]) % 8), (0, 0)))


def _unpack_small(packed, shapes):
    out, row = [], 0
    for shape in shapes:
        size = math.prod(shape)
        n_rows = -(-size // SMALL_COLS)
        out.append(packed[row:row + n_rows].reshape(-1)[:size].reshape(shape))
        row += n_rows
    return out


def kernel(x, a_norm, a_w_in, a_sgu_norm, a_w_spatial, a_b_spatial, a_w_out, kv_norm, w_kv, b_norm, b_w_q, b_rel_bias, b_w_o, ffn_norm, ffn_w_gate_up, ffn_w_down, final_norm, loss_target, m_a_norm, m_a_w_in, m_a_sgu_norm, m_a_w_spatial, m_a_b_spatial, m_a_w_out, m_kv_norm, m_w_kv, m_b_norm, m_b_w_q, m_b_rel_bias, m_b_w_o, m_ffn_norm, m_ffn_w_gate_up, m_ffn_w_down, m_final_norm, v_a_norm, v_a_w_in, v_a_sgu_norm, v_a_w_spatial, v_a_b_spatial, v_a_w_out, v_kv_norm, v_w_kv, v_b_norm, v_b_w_q, v_b_rel_bias, v_b_w_o, v_ffn_norm, v_ffn_w_gate_up, v_ffn_w_down, v_final_norm):
    S, D = x.shape[1], x.shape[2]
    n_a = a_w_in.shape[0]
    n_b = b_w_q.shape[0]
    depth = ffn_w_gate_up.shape[0]
    xi, yi, ci = lax.axis_index("x"), lax.axis_index("y"), lax.axis_index("c")
    chip = 2 * xi + yi

    place = jnp.stack([chip, ci, 2 * chip + ci]).astype(jnp.int32)
    stacked = {"a_w_in": a_w_in, "a_w_out": a_w_out, "w_kv": w_kv[None], "b_w_q": b_w_q, "b_w_o": b_w_o,
               "ffn_w_gate_up": ffn_w_gate_up, "ffn_w_down": ffn_w_down}
    groups = []
    for layer in range(depth):
        if layer == 0 and n_a > 0:
            groups += [[("a_w_in", 0)], [("a_w_out", 0)]]
        elif layer < n_a:
            groups.append([("a_w_in", layer), ("a_w_out", layer)])
        elif layer == n_a:
            groups.append([("w_kv", 0), ("b_w_q", 0), ("b_w_o", 0)])
        else:
            groups.append([("b_w_q", layer - n_a), ("b_w_o", layer - n_a)])
        groups.append([("ffn_w_gate_up", layer), ("ffn_w_down", layer)])
    units = [u for group in groups for u in group]
    n_early = len(groups[0])
    slabs = [_cast_slab(stacked[k], l, place[:1], f"cast_{k}_{l}") for k, l in units[:n_early]]
    early = _allgather_start(slabs, "allgather_start_first")
    slabs = [_cast_slab(stacked[k], l, place[:1], f"cast_{k}_{l}", deps=(early[3],)) for k, l in units[n_early:]]
    late = _allgather_start(slabs, "allgather_start_rest")
    na_w, ns_w = a_norm.shape[1], a_sgu_norm.shape[1]
    small_g = _allgather_small(jnp.concatenate([a_norm, a_sgu_norm], axis=1), "allgather_small")
    a_norm_f = small_g[:, :, :na_w].transpose(1, 0, 2).reshape(n_a, N_CHIPS * na_w)
    a_sgu_f = small_g[:, :, na_w:].transpose(1, 0, 2).reshape(n_a, N_CHIPS * ns_w)
    W, relayed = {}, []

    def relay(after):
        if len(relayed) == len(groups):
            return ()
        index = sum(len(g) for g in groups[:len(relayed)])
        group = groups[len(relayed)]
        (send, recv, flying, _), first = (early, index) if index < n_early else (late, index - n_early)
        relayed.append(_allgather_relay(flying[first:first + len(group)], send, recv, first, after,
                                        f"allgather_relay_{len(relayed)}"))
        return (relayed[-1][3],)

    n_gathered = [0]

    def gathered(after):
        index = n_gathered[0]
        send2, recv2, arrays, _ = relayed[index]
        W.update(zip(groups[index], _allgather_wait(arrays, send2, recv2, after, f"allgather_wait_{index}")))
        n_gathered[0] += 1

    xc = x.reshape(S, D)
    saved = []
    kvp = x_kv = h_kv = None
    relay(late[3])
    order = ()
    for layer in range(depth):
        rec = {"x_in": xc}
        gathered(xc)
        if layer < n_a:
            i = layer
            rec["zpre"], rec["h"] = _norm_matmul(xc, a_norm_f[i][None], W["a_w_in", i], BF16, f"a{i}_in", deps=order)
            order = relay(rec["h"])
            rec["uv"] = _sgu_fwd(rec["zpre"], a_sgu_f[i][None], a_w_spatial[i], a_b_spatial[i].T, f"a{i}_sgu",
                                 deps=order)
            order = ()
            if ("a_w_out", i) not in W:
                gathered(rec["uv"])
                order = relay(rec["uv"])
            xm = _matmul_res(rec["uv"], W["a_w_out", i], xc, f"a{i}_out", tm=512, deps=order)
        else:
            i = layer - n_a
            if i == 0:
                kvp, h_kv = _norm_matmul(xc, kv_norm[None], W["w_kv", 0], BF16, "kv_proj", tm=KV_PAD, zero_rows=KV_PAD)
                x_kv = xc
            rec["q"], rec["h"] = _norm_matmul(xc, b_norm[i][None], W["b_w_q", i], BF16, f"b{i}_q", row_sharded=True)
            order = relay(rec["h"])
            table = jnp.pad(b_rel_bias[i], ((0, 0), (0, TABLE_PAD - b_rel_bias.shape[2])))
            rec["bias"] = _rel_bias_fwd(table, f"b{i}_bias")
            rec["o"] = _attn_fwd(rec["q"], kvp, rec["bias"], f"b{i}_attn", deps=order)
            xm = _matmul_res(rec["o"], W["b_w_o", i], xc, f"b{i}_o", tm=512)
        rec["x_mid"] = xm
        gathered(xm)
        rec["gu"], rec["h_f"] = _norm_matmul(xm, ffn_norm[layer][None], W["ffn_w_gate_up", layer], BF16, f"f{layer}_in")
        order = relay(rec["h_f"])
        xc = _matmul_res(rec["gu"], W["ffn_w_down", layer], xm, f"f{layer}_out", swiglu=True, deps=order, tm=512)
        order = ()
        saved.append(rec)

    loss_tile, dx, dxb, d_final = _loss_head(xc, final_norm[None], loss_target.reshape(S, D), "loss_head")

    started = []
    small_started = []
    pending = []

    held = []

    def weight_grad(unit, hold=False, **kw):
        full = (N_CHIPS,) + tuple(stacked[unit[0]].shape[1:])
        g = _matmul_tn(out_shape=full, name=f"d_{unit[0]}_{unit[1]}", deps=tuple(pending), **kw)
        pending.clear()
        held.append((unit, g))
        if hold:
            return ()
        send, recv, flying_g, flying_land, token = _reduce_start([g for _, g in held],
                                                                 f"reduce_start_{unit[0]}_{unit[1]}")
        started.append(([u for u, _ in held], send, recv, flying_g, flying_land))
        held.clear()
        return (token,)

    tt = min(1024, S)
    tw = min(2048, S)
    ts = S
    tb = min(NORMBWD_ROWS, S)
    row_a = lambda w, rows=tw: pl.BlockSpec((rows, w), lambda o, t: (t, 0))
    d_ffn_norm, d_b_norm, d_a_norm, d_a_sgu = [None] * depth, [None] * n_b, [None] * n_a, [None] * n_a
    d_ws, d_bs, d_rel = [None] * n_a, [None] * n_a, [None] * n_b
    dkv = None
    first = lambda ref: ref[...]
    for layer in reversed(range(depth)):
        rec = saved[layer]
        r_d = ffn_w_down.shape[1]
        half_f = 2 * r_d
        token = weight_grad(
            ("ffn_w_down", layer), a_ops=[rec["gu"], rec["gu"]],
            a_specs=[pl.BlockSpec((tt, half_f), lambda o, t: (t, o)), pl.BlockSpec((tt, half_f), lambda o, t: (t, 2 + o))],
            a_fn=lambda g_ref, u_ref: _swiglu(g_ref[...].astype(F32), u_ref[...].astype(F32)).astype(BF16),
            b_op=dxb, b_spec=row_a(D, tt), out_spec=pl.BlockSpec((2, r_d, D), lambda o, t: (o, 0, 0)),
            acc_shape=(half_f, D), n_outer=2, tt=tt, hold=True)
        dgu = _nt_swiglu_bwd(dxb, W["ffn_w_down", layer], rec["gu"], f"f{layer}_dgu", deps=token)
        nsh = ffn_w_gate_up.shape[2]
        token = weight_grad(
            ("ffn_w_gate_up", layer), a_ops=[rec["h_f"]], a_specs=[row_a(D)], a_fn=first,
            b_op=dgu, b_spec=pl.BlockSpec((None, tw, nsh), lambda o, t: (o // 2, t, o % 2)),
            out_spec=pl.BlockSpec((None, D, nsh), lambda o, t: (o, 0, 0)), acc_shape=(D, nsh), n_outer=N_CHIPS, tt=tw)
        dx, dxb, d_ffn_norm[layer] = _nt_normbwd(
            dgu, pl.BlockSpec((None, tb, nsh), lambda i, k: (k // 2, i, k % 2)),
            W["ffn_w_gate_up", layer], pl.BlockSpec((None, D, nsh), lambda i, k: (k, 0, 0)),
            (D, nsh), N_CHIPS, rec["x_mid"], ffn_norm[layer][None], dx, f"f{layer}_dx", deps=token)
        if layer >= n_a:
            i = layer - n_a
            r_o = b_w_o.shape[1]
            token = weight_grad(
                ("b_w_o", i), a_ops=[rec["o"]], a_specs=[row_a(D, ts)], a_fn=first, b_op=dxb, b_spec=row_a(D, ts),
                out_spec=pl.BlockSpec((N_CHIPS, r_o, D), lambda o, t: (0, 0, 0)), acc_shape=(D, D), n_outer=1, tt=ts,
                hold=True)
            do = _nt_rows(dxb, W["b_w_o", i], N_CHIPS, BF16, f"b{i}_do", deps=token)
            dq, dkv, dbias, *dkv_bf16 = _attn_bwd(rec["q"], kvp, rec["bias"], do, dkv, f"b{i}_attn_bwd")
            d_rel[i] = _rel_bias_bwd(dbias, f"b{i}_dbias")[:, :b_rel_bias.shape[2]]
            token = weight_grad(
                ("b_w_q", i), a_ops=[rec["h"]], a_specs=[row_a(D, ts)], a_fn=first, b_op=dq, b_spec=row_a(D, ts),
                out_spec=pl.BlockSpec((N_CHIPS, r_o, D), lambda o, t: (0, 0, 0)), acc_shape=(D, D), n_outer=1, tt=ts)
            dx, dxb, d_b_norm[i] = _nt_normbwd(
                dq, pl.BlockSpec((tb, D), lambda i_, k: (i_, 0)),
                W["b_w_q", i], pl.BlockSpec((N_CHIPS, r_o, D), lambda i_, k: (0, 0, 0)),
                (D, D), 1, rec["x_in"], b_norm[i][None], dx, f"b{i}_dx", deps=token)
            if i == 0:
                dkv_b = dkv_bf16[0] if dkv_bf16 else dkv[:, KV_PAD:, :].astype(BF16)
                n_kv = w_kv.shape[1]
                token = weight_grad(
                    ("w_kv", 0), a_ops=[h_kv], a_specs=[row_a(D, ts)], a_fn=first,
                    b_op=dkv_b, b_spec=pl.BlockSpec((None, ts, n_kv), lambda o, t: (o // 2, t, o % 2)),
                    out_spec=pl.BlockSpec((None, D, n_kv), lambda o, t: (o, 0, 0)), acc_shape=(D, n_kv),
                    n_outer=N_CHIPS, tt=ts)
                dx, dxb, d_kv_norm = _nt_normbwd(
                    dkv_b, pl.BlockSpec((None, tb, n_kv), lambda i_, k: (k // 2, i_, k % 2)),
                    W["w_kv", 0], pl.BlockSpec((None, D, n_kv), lambda i_, k: (k, 0, 0)),
                    (D, n_kv), N_CHIPS, x_kv, kv_norm[None], dx, "kv_dx", deps=token)
        else:
            i = layer
            r_w = a_w_out.shape[1]
            token = weight_grad(
                ("a_w_out", i), a_ops=[rec["uv"]], a_specs=[row_a(N_CHIPS * r_w, tw // 2)], a_fn=first,
                b_op=dxb, b_spec=row_a(D, tw // 2), out_spec=pl.BlockSpec((N_CHIPS, r_w, D), lambda o, t: (0, 0, 0)),
                acc_shape=(N_CHIPS * r_w, D), n_outer=1, tt=tw // 2, hold=i > 0)
            duv = _nt_rows(dxb, W["a_w_out", i], 2, BF16, f"a{i}_duv", deps=token)
            dz, d_a_sgu[i], d_ws[i], dbs = _sgu_bwd(rec["zpre"], duv, a_sgu_f[i][None], a_w_spatial[i],
                                                  a_w_spatial[i].transpose(0, 2, 1), a_b_spatial[i].T, f"a{i}_sgu_bwd")
            d_bs[i] = dbs[:, :, 0]
            if i == 0:
                batch = [d_a_sgu[0], d_ws[0][None], d_bs[0][None], d_ffn_norm[0]]
                small_started.append((batch, _gather_start(_pack_small(batch), "gather_start_late")))
                pending.append(small_started[-1][1][4])
            n_in = a_w_in.shape[2]
            token = weight_grad(
                ("a_w_in", i), a_ops=[rec["h"]], a_specs=[row_a(D, ts)], a_fn=first,
                b_op=dz, b_spec=pl.BlockSpec((ts, n_in), lambda o, t: (t, o)),
                out_spec=pl.BlockSpec((None, D, n_in), lambda o, t: (o, 0, 0)), acc_shape=(D, n_in), n_outer=N_CHIPS,
                tt=ts)
            dx, dxb, d_a_norm[i] = _nt_normbwd(
                dz, pl.BlockSpec((tb, n_in), lambda i_, k: (i_, k)),
                W["a_w_in", i], pl.BlockSpec((None, D, n_in), lambda i_, k: (k, 0, 0)),
                (D, n_in), N_CHIPS, rec["x_in"], a_norm_f[i][None], dx, f"a{i}_dx", deps=token)
        if layer == 1:
            batch = [jnp.concatenate(d_a_norm[1:], axis=0), jnp.concatenate(d_a_sgu[1:], axis=0), jnp.stack(d_ws[1:]),
                     jnp.stack(d_bs[1:]), d_kv_norm, jnp.concatenate(d_b_norm, axis=0), jnp.stack(d_rel),
                     jnp.concatenate(d_ffn_norm[1:], axis=0), d_final, loss_tile[:1, :1]]
            small_started.append((batch, _gather_start(_pack_small(batch), "gather_start_early")))
            pending.append(small_started[-1][1][4])
    grad_x = dx.reshape(x.shape)

    landed = _reduce_wait([(send, recv, g, land) for _, send, recv, g, land in started], dx, "reduce_wait")
    reduced_units = [unit for units_, *_ in started for unit in units_]
    arrived = dict(zip(reduced_units, [pair for gs, lands in landed for pair in zip(gs, lands)]))
    ffn_units = [u for u in reduced_units if u[0].startswith("ffn_")]
    other_units = [u for u in reduced_units if not u[0].startswith("ffn_")]
    halves = [_reduce_sum(*arrived[u], place, f"reduce_sum_{u[0]}_{u[1]}") for u in ffn_units]
    ffn_send, ffn_recv, ffn_flying, ffn_token = _join_start(halves, "join_start_ffn")
    halves = [_reduce_sum(*arrived[u], place, f"reduce_sum_{u[0]}_{u[1]}", deps=(ffn_token,)) for u in other_units]
    other_send, other_recv, other_flying, other_token = _join_start(halves, "join_start_rest")
    joined = dict(zip(ffn_units, _join_wait(ffn_flying, ffn_send, ffn_recv, other_token, "join_wait_ffn")))
    reduced = {}

    (packed_e, land_e), (packed_l, land_l) = _gather_wait([s[:4] for _, s in small_started], dx, "gather_wait")
    total_e = _sum_gathered(packed_e, land_e, place[2:], "sum_small_grads_early")
    total_l = _sum_gathered(packed_l, land_l, place[2:], "sum_small_grads_late")
    total_t = _sum_devices(_gather_small(_pack_small([d_a_norm[0]]), "gather_small_grads_last"), "sum_small_grads_last")
    (e_a_norm, e_a_sgu, e_ws, e_bs, g_kv_norm, g_b_norm, g_rel, e_ffn_norm, g_final, loss) = _unpack_small(
        total_e, [a.shape for a in small_started[0][0]])
    l_a_sgu, l_ws, l_bs, l_ffn_norm = _unpack_small(total_l, [a.shape for a in small_started[1][0]])
    (t_a_norm,) = _unpack_small(total_t, [d_a_norm[0].shape])
    g_a_norm = jnp.concatenate([t_a_norm, e_a_norm], axis=0)
    g_a_sgu = jnp.concatenate([l_a_sgu, e_a_sgu], axis=0)
    g_ws = jnp.concatenate([l_ws, e_ws], axis=0)
    g_bs = jnp.concatenate([l_bs, e_bs], axis=0)
    g_ffn_norm = jnp.concatenate([l_ffn_norm, e_ffn_norm], axis=0)
    reduced["a_norm"] = lax.dynamic_slice_in_dim(g_a_norm, chip * na_w, na_w, axis=1)
    reduced["a_sgu_norm"] = lax.dynamic_slice_in_dim(g_a_sgu, chip * ns_w, ns_w, axis=1)
    reduced.update(a_w_spatial=g_ws, a_b_spatial=g_bs, kv_norm=g_kv_norm.reshape(kv_norm.shape), b_norm=g_b_norm,
                   b_rel_bias=g_rel, ffn_norm=g_ffn_norm, final_norm=g_final.reshape(final_norm.shape))

    weights = dict(a_norm=a_norm, a_w_in=a_w_in, a_sgu_norm=a_sgu_norm, a_w_spatial=a_w_spatial,
                   a_b_spatial=a_b_spatial, a_w_out=a_w_out, kv_norm=kv_norm, w_kv=w_kv, b_norm=b_norm, b_w_q=b_w_q,
                   b_rel_bias=b_rel_bias, b_w_o=b_w_o, ffn_norm=ffn_norm, ffn_w_gate_up=ffn_w_gate_up,
                   ffn_w_down=ffn_w_down, final_norm=final_norm)
    m_in = dict(a_norm=m_a_norm, a_w_in=m_a_w_in, a_sgu_norm=m_a_sgu_norm, a_w_spatial=m_a_w_spatial,
                a_b_spatial=m_a_b_spatial, a_w_out=m_a_w_out, kv_norm=m_kv_norm, w_kv=m_w_kv, b_norm=m_b_norm,
                b_w_q=m_b_w_q, b_rel_bias=m_b_rel_bias, b_w_o=m_b_w_o, ffn_norm=m_ffn_norm,
                ffn_w_gate_up=m_ffn_w_gate_up, ffn_w_down=m_ffn_w_down, final_norm=m_final_norm)
    v_in = dict(a_norm=v_a_norm, a_w_in=v_a_w_in, a_sgu_norm=v_a_sgu_norm, a_w_spatial=v_a_w_spatial,
                a_b_spatial=v_a_b_spatial, a_w_out=v_a_w_out, kv_norm=v_kv_norm, w_kv=v_w_kv, b_norm=v_b_norm,
                b_w_q=v_b_w_q, b_rel_bias=v_b_rel_bias, b_w_o=v_b_w_o, ffn_norm=v_ffn_norm,
                ffn_w_gate_up=v_ffn_w_gate_up, ffn_w_down=v_ffn_w_down, final_norm=v_final_norm)
    results = {}

    def adamw_large(key):
        as_layers = lambda a: a.reshape(stacked[key].shape)
        results[key] = _adamw_stacked(
            as_layers(weights[key]), [joined[key, layer] for layer in range(stacked[key].shape[0])],
            as_layers(m_in[key]), as_layers(v_in[key]), "adamw_" + key)

    for key in ("ffn_w_gate_up", "ffn_w_down"):
        adamw_large(key)
    joined.update(zip(other_units, _join_wait(other_flying, other_send, other_recv, results["ffn_w_down"][1],
                                              "join_wait_rest")))
    for key, w in weights.items():
        if key in stacked:
            if key not in results:
                adamw_large(key)
        else:
            g = reduced[key].reshape(w.shape)
            view = (1, w.shape[0]) if w.ndim == 1 else (-1, w.shape[-1])
            d, nm, nv = _adamw(w.reshape(view), g.reshape(view), m_in[key].reshape(view), v_in[key].reshape(view),
                               "adamw_" + key)
            results[key] = (g, d, nm, nv)
    outs = [[results[key][k].reshape(w.shape) for key, w in weights.items()] for k in range(4)]
    return (loss.reshape(()), grad_x, *outs[0], *outs[1], *outs[2], *outs[3])
```

```python
import math

import jax
import jax.numpy as jnp
from jax import lax
from jax.experimental import pallas as pl
from jax.experimental.pallas import tpu as pltpu

F32, BF16 = jnp.float32, jnp.bfloat16
MESH = pl.DeviceIdType.MESH
HIGHEST = lax.Precision.HIGHEST
NT_DIMS = (((1,), (1,)), ((), ()))
TN_DIMS = (((0,), (0,)), ((), ()))

EPS = 1e-6
CHUNK = 64
A_CHUNK = 128
A_GROUPS = 8
N_HEADS = 16
HEAD_DIM = 64
N_LEFT = 8
MAX_REL = 256
ATTN_SCALE = HEAD_DIM ** -0.5
NEG_INF = -1e30
Q_BLOCK = 4 * CHUNK
KV_PAD = N_LEFT * CHUNK
BAND = KV_PAD + Q_BLOCK
DIAGS = BAND + Q_BLOCK
TABLE_PAD = 640
HEADS_PER_BLOCK = 2
BLOCKS_PER_STEP = 16
NORMBWD_ROWS = 1024

ADAM_LR, ADAM_B1, ADAM_B2, ADAM_EPS, ADAM_WD, ADAM_STEP = 0.001, 0.9, 0.999, 1e-08, 0.01, 10

VMEM_LIMIT_BYTES = 56 * 1024 * 1024
N_CHIPS = 4
N_DEV = 8
SMALL_COLS = 1024


def _cparams(n_grid):
    return pltpu.CompilerParams(dimension_semantics=("arbitrary",) * n_grid, vmem_limit_bytes=VMEM_LIMIT_BYTES)


def _sds(shape, dtype):
    return jax.ShapeDtypeStruct(tuple(shape), dtype)


def _gelu(x):
    return x * (0.5 * (1.0 + lax.erf(x * math.sqrt(0.5))))


def _gelu_and_grad(x):
    cdf = 0.5 * (1.0 + lax.erf(x * math.sqrt(0.5)))
    return x * cdf, cdf + x * (jnp.exp(-0.5 * x * x) * (1.0 / math.sqrt(2.0 * math.pi)))


def _rms_hat(xv):
    r = lax.rsqrt(jnp.mean(xv * xv, axis=-1, keepdims=True) + EPS)
    return xv * r, r


def _rms_bwd(xhat, r, g, dy):
    dxhat = dy * g
    dx = r * (dxhat - xhat * jnp.mean(dxhat * xhat, axis=-1, keepdims=True))
    return dx, dy * xhat


def _swiglu(gate, up):
    return (gate * jax.nn.sigmoid(gate)) * up


def _row_tile(rows, cols, itemsize, cap_bytes, align):
    t = rows
    while t * cols * itemsize > cap_bytes and t % (2 * align) == 0:
        t //= 2
    return t


def _cast_slab(w, layer, chip, name, deps=()):
    _, r, C = w.shape
    tr = _row_tile(r, C, 4, 4 * 1024 * 1024, 16)

    def body(chip_ref, w_ref, *rest):
        del chip_ref
        rest[-1][...] = w_ref[...].astype(BF16)

    grid_spec = pltpu.PrefetchScalarGridSpec(
        num_scalar_prefetch=1, grid=(r // tr,),
        in_specs=[pl.BlockSpec((None, tr, C), lambda i, chip_ref: (layer, i, 0))] + [HBM_SPEC] * len(deps),
        out_specs=pl.BlockSpec((None, tr, C), lambda i, chip_ref: (chip_ref[0], i, 0)))
    return pl.pallas_call(body, name=name, grid_spec=grid_spec, out_shape=_sds((N_CHIPS, r, C), BF16),
                          compiler_params=_cparams(1))(chip, w, *deps)


def _adamw_stacked(w, gs, m, v, name):
    L, r, C = w.shape
    tr = _row_tile(r, C, 4, 2 * 1024 * 1024, 8)
    nb = r // tr

    def body(w_ref, m_ref, v_ref, *rest):
        go_ref, d_ref, nm_ref, nv_ref = rest[-4:]
        layer = pl.program_id(0)
        gv = rest[0][...]
        for k in range(1, L):
            gv = jnp.where(layer == k, rest[k][...], gv)
        mn = ADAM_B1 * m_ref[...] + (1.0 - ADAM_B1) * gv
        vn = ADAM_B2 * v_ref[...] + (1.0 - ADAM_B2) * jnp.square(gv)
        m_hat = mn / (1.0 - ADAM_B1 ** ADAM_STEP)
        v_hat = vn / (1.0 - ADAM_B2 ** ADAM_STEP)
        d_ref[...] = -ADAM_LR * (m_hat / (jnp.sqrt(v_hat) + ADAM_EPS) + ADAM_WD * w_ref[...])
        nm_ref[...] = mn
        nv_ref[...] = vn
        go_ref[...] = gv

    def grad_spec(k):
        return pl.BlockSpec((tr, C), lambda l, i: (jnp.where(l == k, i, jnp.where(l > k, nb - 1, 0)), 0))

    stacked = pl.BlockSpec((None, tr, C), lambda l, i: (l, i, 0))
    return pl.pallas_call(body, name=name, grid=(L, nb), in_specs=[stacked] * 3 + [grad_spec(k) for k in range(L)],
                          out_specs=[stacked] * 4, out_shape=[_sds((L, r, C), F32)] * 4,
                          compiler_params=_cparams(2))(w, m, v, *gs)


def _adamw(w, g, m, v, name):
    R, C = w.shape
    tr = _row_tile(R, C, 4, 1024 * 1024, 8)

    def body(w_ref, g_ref, m_ref, v_ref, d_ref, nm_ref, nv_ref):
        gv = g_ref[...]
        mn = ADAM_B1 * m_ref[...] + (1.0 - ADAM_B1) * gv
        vn = ADAM_B2 * v_ref[...] + (1.0 - ADAM_B2) * jnp.square(gv)
        m_hat = mn / (1.0 - ADAM_B1 ** ADAM_STEP)
        v_hat = vn / (1.0 - ADAM_B2 ** ADAM_STEP)
        d_ref[...] = -ADAM_LR * (m_hat / (jnp.sqrt(v_hat) + ADAM_EPS) + ADAM_WD * w_ref[...])
        nm_ref[...] = mn
        nv_ref[...] = vn

    spec = pl.BlockSpec((tr, C), lambda i: (i, 0))
    return pl.pallas_call(body, name=name, grid=(R // tr,), in_specs=[spec] * 4, out_specs=[spec] * 3,
                          out_shape=[_sds((R, C), F32)] * 3, compiler_params=_cparams(1))(w, g, m, v)


def _norm_matmul(x, g, w_g, out_dtype, name, row_sharded=False, deps=(), tm=1024, zero_rows=0):
    S, D = x.shape
    tm = min(tm, S)
    lead = zero_rows // tm
    if row_sharded:
        r, N = w_g.shape[1], w_g.shape[2]
        tn = 512
        w_spec = pl.BlockSpec((N_CHIPS, r, tn), lambda i, j: (0, 0, j))
    else:
        nsh = w_g.shape[2]
        N = N_CHIPS * nsh
        tn = next((t for t in (1024, 512) if nsh % t == 0), nsh)
        bps = nsh // tn
        w_spec = pl.BlockSpec((None, D, tn), lambda i, j: (j // bps, 0, j % bps))
    whole = not row_sharded and N <= 2048
    if whole:
        tn = N
        w_spec = pl.BlockSpec((N_CHIPS, D, nsh), lambda i, j: (0, 0, 0))

    def body(x_ref, g_ref, w_ref, *rest):
        y_ref, h_ref = rest[-2:]
        i = pl.program_id(0)

        @pl.when((i >= lead) & (pl.program_id(1) == 0))
        def _():
            xhat, _ = _rms_hat(x_ref[...])
            h_ref[...] = (xhat * g_ref[...]).astype(BF16)

        @pl.when(i >= lead)
        def _():
            if whole:
                for s in range(N_CHIPS):
                    y_ref[:, s * nsh:(s + 1) * nsh] = jnp.dot(h_ref[...], w_ref[s],
                                                             preferred_element_type=F32).astype(y_ref.dtype)
            else:
                w = w_ref[...].reshape(D, tn)
                y_ref[...] = jnp.dot(h_ref[...], w, preferred_element_type=F32).astype(y_ref.dtype)

        if lead:
            @pl.when(i < lead)
            def _():
                y_ref[...] = jnp.zeros_like(y_ref)

    rows = lambda i, j: (jnp.maximum(i - lead, 0), 0)
    return pl.pallas_call(
        body, name=name, grid=(lead + S // tm, N // tn),
        in_specs=[pl.BlockSpec((tm, D), rows), pl.BlockSpec((1, D), lambda i, j: (0, 0)), w_spec]
        + [HBM_SPEC] * len(deps),
        out_specs=[pl.BlockSpec((tm, tn), lambda i, j: (i, j)), pl.BlockSpec((tm, D), rows)],
        out_shape=[_sds((zero_rows + S, N), out_dtype), _sds((S, D), BF16)],
        compiler_params=_cparams(2))(x, g, w_g, *deps)


def _matmul_res(a, w_g, res, name, swiglu=False, deps=(), tm=256):
    S, N = res.shape
    r = w_g.shape[1]
    K = N_CHIPS * r

    def body(*refs):
        o_ref = refs[-1]
        if swiglu:
            gate_ref, up_ref, w_ref, res_ref = refs[:4]
            a_blk = _swiglu(gate_ref[...].astype(F32), up_ref[...].astype(F32)).astype(BF16)
        else:
            a_ref, w_ref, res_ref = refs[:3]
            a_blk = a_ref[...]
        o_ref[...] = res_ref[...] + jnp.dot(a_blk, w_ref[...].reshape(K, N), preferred_element_type=F32)

    a_specs, a_ops = [pl.BlockSpec((tm, K), lambda i: (i, 0))], [a]
    if swiglu:
        a_specs.append(pl.BlockSpec((tm, K), lambda i: (i, 1)))
        a_ops.append(a)
    row = pl.BlockSpec((tm, N), lambda i: (i, 0))
    return pl.pallas_call(
        body, name=name, grid=(S // tm,),
        in_specs=a_specs + [pl.BlockSpec((N_CHIPS, r, N), lambda i: (0, 0, 0)), row] + [HBM_SPEC] * len(deps),
        out_specs=row, out_shape=_sds((S, N), F32), compiler_params=_cparams(1))(*a_ops, w_g, res, *deps)


def _matmul_tn(a_ops, a_specs, a_fn, b_op, b_spec, out_spec, out_shape, acc_shape, n_outer, name, deps=(), tt=512):
    S = b_op.shape[-2]
    na = len(a_ops)
    nt = S // tt

    def body(*refs):
        a_refs, b_ref, o_ref, acc_ref = refs[:na], refs[na], refs[-2], refs[-1]
        t = pl.program_id(1)
        part = lax.dot_general(a_fn(*a_refs), b_ref[...].astype(BF16), TN_DIMS, preferred_element_type=F32)

        @pl.when(t == 0)
        def _():
            acc_ref[...] = part

        @pl.when(t > 0)
        def _():
            acc_ref[...] += part

        @pl.when(t == nt - 1)
        def _():
            o_ref[...] = acc_ref[...].reshape(o_ref.shape).astype(BF16)

    return pl.pallas_call(
        body, name=name, grid=(n_outer, nt), in_specs=list(a_specs) + [b_spec] + [HBM_SPEC] * len(deps),
        out_specs=out_spec, out_shape=_sds(out_shape, BF16), scratch_shapes=[pltpu.VMEM(acc_shape, F32)],
        compiler_params=_cparams(2))(*a_ops, b_op, *deps)


def _nt_accumulate(a_ref, w_ref, acc_ref, w2d, nk):
    k = pl.program_id(1)
    part = lax.dot_general(a_ref[...].astype(BF16), w_ref[...].reshape(w2d), NT_DIMS, preferred_element_type=F32)

    @pl.when(k == 0)
    def _():
        acc_ref[...] = part

    @pl.when(k > 0)
    def _():
        acc_ref[...] += part

    return k == nk - 1


def _nt_normbwd(dy, dy_spec, w_g, w_spec, w2d, nk, x, g, dres, name, deps=(), tm=NORMBWD_ROWS):
    S, D = x.shape
    tm = min(tm, S)

    def body(dy_ref, w_ref, x_ref, g_ref, dres_ref, *rest):
        dx_ref, dxb_ref, dg_ref, acc_ref = rest[-4:]

        @pl.when((pl.program_id(0) == 0) & (pl.program_id(1) == 0))
        def _():
            dg_ref[...] = jnp.zeros_like(dg_ref)

        last = _nt_accumulate(dy_ref, w_ref, acc_ref, w2d, nk)

        @pl.when(last)
        def _():
            xhat, r = _rms_hat(x_ref[...])
            dx, dgp = _rms_bwd(xhat, r, g_ref[...], acc_ref[...])
            total = dres_ref[...] + dx
            dx_ref[...] = total
            dxb_ref[...] = total.astype(BF16)
            dg_ref[...] += jnp.sum(dgp, axis=0, keepdims=True)

    row = pl.BlockSpec((tm, D), lambda i, k: (i, 0))
    vec = pl.BlockSpec((1, D), lambda i, k: (0, 0))
    return pl.pallas_call(
        body, name=name, grid=(S // tm, nk),
        in_specs=[dy_spec, w_spec, row, vec, row] + [HBM_SPEC] * len(deps), out_specs=[row, row, vec],
        out_shape=[_sds((S, D), F32), _sds((S, D), BF16), _sds((1, D), F32)],
        scratch_shapes=[pltpu.VMEM((tm, D), F32)], compiler_params=_cparams(2))(dy, w_g, x, g, dres, *deps)


def _nt_rows(dy, w_g, shards_per_block, out_dtype, name, deps=(), tm=1024):
    S, N = dy.shape
    tm = min(tm, S)
    r = w_g.shape[1]
    tn = shards_per_block * r

    def body(dy_ref, w_ref, *rest):
        o_ref = rest[-1]
        o_ref[...] = lax.dot_general(dy_ref[...].astype(BF16), w_ref[...].reshape(tn, N), NT_DIMS,
                                     preferred_element_type=F32).astype(o_ref.dtype)

    return pl.pallas_call(
        body, name=name, grid=(S // tm, N_CHIPS // shards_per_block),
        in_specs=[pl.BlockSpec((tm, N), lambda i, j: (i, 0)),
                  pl.BlockSpec((shards_per_block, r, N), lambda i, j: (j, 0, 0))] + [HBM_SPEC] * len(deps),
        out_specs=pl.BlockSpec((tm, tn), lambda i, j: (i, j)),
        out_shape=_sds((S, N_CHIPS * r), out_dtype), compiler_params=_cparams(2))(dy, w_g, *deps)


def _nt_swiglu_bwd(dy, w_g, gu, name, deps=(), tm=1024):
    S, N = dy.shape
    tm = min(tm, S)
    r = w_g.shape[1]
    tn = 2 * r
    F = N_CHIPS * r

    def body(dy_ref, w_ref, gate_ref, up_ref, *rest):
        o_ref = rest[-1]
        dact = lax.dot_general(dy_ref[...].astype(BF16), w_ref[...].reshape(tn, N), NT_DIMS,
                               preferred_element_type=F32)
        gate, up = gate_ref[...].astype(F32), up_ref[...].astype(F32)
        sg = jax.nn.sigmoid(gate)
        silu = gate * sg
        o_ref[0] = ((dact * up) * (sg + silu * (1.0 - sg))).astype(BF16)
        o_ref[1] = (dact * silu).astype(BF16)

    return pl.pallas_call(
        body, name=name, grid=(2, S // tm),
        in_specs=[pl.BlockSpec((tm, N), lambda j, i: (i, 0)),
                  pl.BlockSpec((2, r, N), lambda j, i: (j, 0, 0)),
                  pl.BlockSpec((tm, tn), lambda j, i: (i, j)),
                  pl.BlockSpec((tm, tn), lambda j, i: (i, 2 + j))] + [HBM_SPEC] * len(deps),
        out_specs=pl.BlockSpec((2, tm, tn), lambda j, i: (0, i, j)),
        out_shape=_sds((2, S, F), BF16), compiler_params=_cparams(2))(dy, w_g, gu, gu, *deps)


def _chunk_causal_mask(transposed):
    i = lax.broadcasted_iota(jnp.int32, (A_CHUNK, A_CHUNK), 0) // CHUNK
    j = lax.broadcasted_iota(jnp.int32, (A_CHUNK, A_CHUNK), 1) // CHUNK
    return ((i <= j) if transposed else (i >= j)).astype(F32)


def _sgu_fwd(zpre, g_sgu, ws, bs_t, name, deps=()):
    S, F2 = zpre.shape
    F = F2 // 2
    gd = F // A_GROUPS

    windows = 2
    rows = windows * A_CHUNK

    def body(zu_ref, zv_ref, g_ref, ws_ref, b_ref, *rest):
        o_ref = rest[-1]
        vhat, _ = _rms_hat(_gelu(zv_ref[...].astype(F32)))
        vn = (vhat * g_ref[...]).astype(BF16)
        u = _gelu(zu_ref[...].astype(F32))
        mask = _chunk_causal_mask(False)
        for gi in range(A_GROUPS):
            sl = slice(gi * gd, (gi + 1) * gd)
            wm = (ws_ref[gi] * mask).astype(BF16)
            for w in range(windows):
                win = slice(w * A_CHUNK, (w + 1) * A_CHUNK)
                vs = jnp.dot(wm, vn[win, sl], preferred_element_type=F32) + b_ref[:, gi:gi + 1]
                o_ref[win, sl] = (u[win, sl] * vs).astype(BF16)

    return pl.pallas_call(
        body, name=name, grid=(S // rows,),
        in_specs=[pl.BlockSpec((rows, F), lambda i: (i, 0)),
                  pl.BlockSpec((rows, F), lambda i: (i, 1)),
                  pl.BlockSpec((1, F), lambda i: (0, 0)),
                  pl.BlockSpec((A_GROUPS, A_CHUNK, A_CHUNK), lambda i: (0, 0, 0)),
                  pl.BlockSpec((A_CHUNK, A_GROUPS), lambda i: (0, 0))] + [HBM_SPEC] * len(deps),
        out_specs=pl.BlockSpec((rows, F), lambda i: (i, 0)),
        out_shape=_sds((S, F), BF16), compiler_params=_cparams(1))(zpre, zpre, g_sgu, ws, bs_t, *deps)


def _sgu_bwd(zpre, duv, g_sgu, ws, ws_t, bs_t, name):
    S, F2 = zpre.shape
    F = F2 // 2
    gd = F // A_GROUPS

    def body(zu_ref, zv_ref, duv_ref, g_ref, ws_ref, wst_ref, b_ref, dz_ref, dg_ref, dws_ref, dbs_ref, dvn_ref):
        @pl.when(pl.program_id(0) == 0)
        def _():
            dg_ref[...] = jnp.zeros_like(dg_ref)
            dws_ref[...] = jnp.zeros_like(dws_ref)
            dbs_ref[...] = jnp.zeros_like(dbs_ref)

        gv = g_ref[...]
        u, u_grad = _gelu_and_grad(zu_ref[...].astype(F32))
        v, v_grad = _gelu_and_grad(zv_ref[...].astype(F32))
        vhat, r = _rms_hat(v)
        vn = (vhat * gv).astype(BF16)
        duv_v = duv_ref[...].astype(F32)
        dvs = duv_v * u
        dvs_b = dvs.astype(BF16)
        mask = _chunk_causal_mask(False)
        mask_t = _chunk_causal_mask(True)
        for gi in range(A_GROUPS):
            sl = slice(gi * gd, (gi + 1) * gd)
            wm = (ws_ref[gi] * mask).astype(BF16)
            vs = jnp.dot(wm, vn[:, sl], preferred_element_type=F32) + b_ref[:, gi:gi + 1]
            dz_ref[:, sl] = ((duv_v[:, sl] * vs) * u_grad[:, sl]).astype(BF16)
            dws_ref[gi] += lax.dot_general(dvs_b[:, sl], vn[:, sl], NT_DIMS, preferred_element_type=F32) * mask
            dbs_ref[gi] += jnp.broadcast_to(jnp.sum(dvs[:, sl], axis=1, keepdims=True), (A_CHUNK, A_CHUNK))
            wm_t = (wst_ref[gi] * mask_t).astype(BF16)
            dvn_ref[:, sl] = jnp.dot(wm_t, dvs_b[:, sl], preferred_element_type=F32)
        dv, dg_part = _rms_bwd(vhat, r, gv, dvn_ref[...])
        dg_ref[...] += jnp.sum(dg_part, axis=0, keepdims=True)
        dz_ref[:, F:] = (dv * v_grad).astype(BF16)

    blk = pl.BlockSpec((A_CHUNK, F), lambda i: (i, 0))
    const3 = pl.BlockSpec((A_GROUPS, A_CHUNK, A_CHUNK), lambda i: (0, 0, 0))
    return pl.pallas_call(
        body, name=name, grid=(S // A_CHUNK,),
        in_specs=[blk, pl.BlockSpec((A_CHUNK, F), lambda i: (i, 1)), blk,
                  pl.BlockSpec((1, F), lambda i: (0, 0)), const3, const3,
                  pl.BlockSpec((A_CHUNK, A_GROUPS), lambda i: (0, 0))],
        out_specs=[pl.BlockSpec((A_CHUNK, F2), lambda i: (i, 0)), pl.BlockSpec((1, F), lambda i: (0, 0)),
                   const3, const3],
        out_shape=[_sds((S, F2), BF16), _sds((1, F), F32), _sds((A_GROUPS, A_CHUNK, A_CHUNK), F32),
                   _sds((A_GROUPS, A_CHUNK, A_CHUNK), F32)],
        scratch_shapes=[pltpu.VMEM((A_CHUNK, F), F32)],
        compiler_params=_cparams(1))(zpre, zpre, duv, g_sgu, ws, ws_t, bs_t)


def _toeplitz_one_hot():
    row = lax.broadcasted_iota(jnp.int32, (TABLE_PAD, DIAGS), 0)
    j = lax.broadcasted_iota(jnp.int32, (TABLE_PAD, DIAGS), 1)
    idx = jnp.clip(KV_PAD + Q_BLOCK - j, -MAX_REL, MAX_REL) + MAX_REL
    return (row == idx).astype(F32)


def _rel_bias_fwd(table, name):
    H = table.shape[0]

    def body(t_ref, o_ref):
        diag = jnp.dot(t_ref[...], _toeplitz_one_hot(), precision=HIGHEST, preferred_element_type=F32)
        q_chunk = lax.broadcasted_iota(jnp.int32, (Q_BLOCK, BAND), 0) // CHUNK
        k_chunk = lax.broadcasted_iota(jnp.int32, (Q_BLOCK, BAND), 1) // CHUNK
        unseen = jnp.where((k_chunk >= q_chunk) & (k_chunk <= q_chunk + N_LEFT), 0.0, NEG_INF)
        for h in range(H):
            rows = jnp.broadcast_to(diag[h:h + 1, :], (Q_BLOCK, DIAGS))
            o_ref[h] = pltpu.roll(rows, DIAGS - Q_BLOCK, 1, stride=1, stride_axis=0)[:, :BAND] + unseen

    return pl.pallas_call(body, name=name, out_shape=_sds((H, Q_BLOCK, BAND), F32),
                          compiler_params=pltpu.CompilerParams(vmem_limit_bytes=VMEM_LIMIT_BYTES))(table)


def _rel_bias_bwd(dbias, name):
    H = dbias.shape[0]

    def body(d_ref, o_ref):
        def step(r, acc):
            row = d_ref[:, pl.ds(r, 1), :].reshape(H, BAND)
            row = jnp.concatenate([row, jnp.zeros((H, DIAGS - BAND), F32)], axis=1)
            return acc + pltpu.roll(row, Q_BLOCK - r, 1)

        diag = lax.fori_loop(0, Q_BLOCK, step, jnp.zeros((H, DIAGS), F32))
        o_ref[...] = lax.dot_general(diag, _toeplitz_one_hot(), NT_DIMS, precision=HIGHEST,
                                     preferred_element_type=F32)

    return pl.pallas_call(body, name=name, out_shape=_sds((H, TABLE_PAD), F32),
                          compiler_params=pltpu.CompilerParams(vmem_limit_bytes=VMEM_LIMIT_BYTES))(dbias)


def _head_rows(t):
    lane = lax.broadcasted_iota(jnp.int32, t.shape, 1)
    zero = jnp.zeros_like(t)
    return jnp.concatenate([jnp.where(lane < HEAD_DIM, t, zero), jnp.where(lane >= HEAD_DIM, t, zero)], axis=0)


def _head_lanes(t2):
    lane = lax.broadcasted_iota(jnp.int32, (Q_BLOCK, t2.shape[1]), 1)
    return jnp.where(lane < HEAD_DIM, t2[:Q_BLOCK], t2[Q_BLOCK:])


def _attn_probs(q2, kb, bias2, block):
    kj = lax.broadcasted_iota(jnp.int32, (1, BAND), 1)
    before_start = jnp.where(block * Q_BLOCK + kj - KV_PAD >= 0, 0.0, NEG_INF)
    s = lax.dot_general(q2 * ATTN_SCALE, kb, NT_DIMS, preferred_element_type=F32) + bias2 + before_start
    e = jnp.exp(s - jnp.max(s, axis=-1, keepdims=True))
    return e / jnp.sum(e, axis=-1, keepdims=True)


def _attn_specs(S):
    lanes = HEADS_PER_BLOCK * HEAD_DIM
    rows = S + KV_PAD
    per_step = min(BLOCKS_PER_STEP, S // Q_BLOCK)
    q_spec = pl.BlockSpec((per_step * Q_BLOCK, lanes), lambda h, i: (i, h))
    k_spec = pl.BlockSpec((rows, lanes), lambda h, i: (0, h))
    v_spec = pl.BlockSpec((rows, lanes), lambda h, i: (0, N_HEADS // HEADS_PER_BLOCK + h))
    b_spec = pl.BlockSpec((HEADS_PER_BLOCK, Q_BLOCK, BAND), lambda h, i: (h, 0, 0))
    return q_spec, k_spec, v_spec, b_spec, per_step


def _attn_fwd(q, kvp, bias, name, deps=()):
    S, HD = q.shape
    q_spec, k_spec, v_spec, b_spec, per_step = _attn_specs(S)

    def body(q_ref, k_ref, v_ref, b_ref, *rest):
        o_ref = rest[-1]
        for b in range(per_step):
            block = pl.program_id(1) * per_step + b
            rows = slice(b * Q_BLOCK, (b + 1) * Q_BLOCK)
            band = pl.ds(pl.multiple_of(block * Q_BLOCK, Q_BLOCK), BAND)
            p = _attn_probs(_head_rows(q_ref[rows, :]), k_ref[band, :], b_ref[...].reshape(2 * Q_BLOCK, BAND), block)
            o2 = jnp.dot(p.astype(BF16), v_ref[band, :], preferred_element_type=F32)
            o_ref[rows, :] = _head_lanes(o2).astype(BF16)

    return pl.pallas_call(
        body, name=name, grid=(N_HEADS // HEADS_PER_BLOCK, S // (per_step * Q_BLOCK)),
        in_specs=[q_spec, k_spec, v_spec, b_spec] + [HBM_SPEC] * len(deps), out_specs=q_spec,
        out_shape=_sds((S, HD), BF16), compiler_params=_cparams(2))(q, kvp, kvp, bias, *deps)


def _attn_bwd(q, kvp, bias, do, dkv_prev, name):
    S, HD = q.shape
    lanes = HEADS_PER_BLOCK * HEAD_DIM
    q_spec, k_spec, v_spec, b_spec, per_step = _attn_specs(S)
    dkv_spec = pl.BlockSpec((2, S + KV_PAD, lanes), lambda h, i: (0, 0, h))
    prev = [] if dkv_prev is None else [dkv_prev]
    n_steps = S // (per_step * Q_BLOCK)

    def body(q_ref, k_ref, v_ref, b_ref, do_ref, *rest):
        dq_ref, dkv_ref, db_ref = rest[len(prev):len(prev) + 3]

        @pl.when(pl.program_id(1) == 0)
        def _():
            dkv_ref[...] = rest[0][...] if prev else jnp.zeros_like(dkv_ref)
            db_ref[...] = jnp.zeros_like(db_ref)

        db = jnp.zeros((2 * Q_BLOCK, BAND), F32)
        for b in range(per_step):
            block = pl.program_id(1) * per_step + b
            rows = slice(b * Q_BLOCK, (b + 1) * Q_BLOCK)
            band = pl.ds(pl.multiple_of(block * Q_BLOCK, Q_BLOCK), BAND)
            kb, vb = k_ref[band, :], v_ref[band, :]
            q2, do2 = _head_rows(q_ref[rows, :]), _head_rows(do_ref[rows, :])
            p = _attn_probs(q2, kb, b_ref[...].reshape(2 * Q_BLOCK, BAND), block)
            dp = lax.dot_general(do2, vb, NT_DIMS, preferred_element_type=F32)
            ds = p * (dp - jnp.sum(dp * p, axis=-1, keepdims=True))
            db = db + ds
            ds_b = (ds * ATTN_SCALE).astype(BF16)
            dq_ref[rows, :] = _head_lanes(jnp.dot(ds_b, kb, preferred_element_type=F32)).astype(BF16)
            dkv_ref[0, band, :] += lax.dot_general(ds_b, q2, TN_DIMS, preferred_element_type=F32)
            dkv_ref[1, band, :] += lax.dot_general(p.astype(BF16), do2, TN_DIMS, preferred_element_type=F32)
        db_ref[...] += db.reshape(HEADS_PER_BLOCK, Q_BLOCK, BAND)

        if prev:
            @pl.when(pl.program_id(1) == n_steps - 1)
            def _():
                rest[-1][...] = dkv_ref[:, KV_PAD:, :].astype(BF16)

    return pl.pallas_call(
        body, name=name, grid=(N_HEADS // HEADS_PER_BLOCK, n_steps),
        in_specs=[q_spec, k_spec, v_spec, b_spec, q_spec] + [dkv_spec] * len(prev),
        out_specs=[q_spec, dkv_spec, b_spec] + [pl.BlockSpec((2, S, lanes), lambda h, i: (0, 0, h))] * len(prev),
        out_shape=[_sds((S, HD), BF16), _sds((2, S + KV_PAD, HD), F32), _sds((N_HEADS, Q_BLOCK, BAND), F32)]
        + [_sds((2, S, HD), BF16)] * len(prev),
        compiler_params=_cparams(2))(q, kvp, kvp, bias, do, *prev)


def _loss_head(x, g, target, name, tm=512):
    S, D = x.shape

    def body(x_ref, g_ref, t_ref, loss_ref, dx_ref, dxb_ref, dg_ref):
        @pl.when(pl.program_id(0) == 0)
        def _():
            loss_ref[...] = jnp.zeros_like(loss_ref)
            dg_ref[...] = jnp.zeros_like(dg_ref)

        xhat, r = _rms_hat(x_ref[...])
        gv = g_ref[...]
        err = xhat * gv - t_ref[...]
        loss_ref[...] += 0.5 * jnp.sum(jnp.mean(err * err, axis=-1, keepdims=True))
        dx, dgp = _rms_bwd(xhat, r, gv, err * (1.0 / D))
        dx_ref[...] = dx
        dxb_ref[...] = dx.astype(BF16)
        dg_ref[...] += jnp.sum(dgp, axis=0, keepdims=True)

    row = pl.BlockSpec((tm, D), lambda i: (i, 0))
    vec = pl.BlockSpec((1, D), lambda i: (0, 0))
    return pl.pallas_call(
        body, name=name, grid=(S // tm,), in_specs=[row, vec, row],
        out_specs=[pl.BlockSpec((8, 128), lambda i: (0, 0)), row, row, vec],
        out_shape=[_sds((8, 128), F32), _sds((S, D), F32), _sds((S, D), BF16), _sds((1, D), F32)],
        compiler_params=_cparams(1))(x, g, target)


def _place():
    x, y, c = lax.axis_index("x"), lax.axis_index("y"), lax.axis_index("c")
    chips = [(1 - x, y), (x, 1 - y), (1 - x, 1 - y)]
    return x, y, c, chips


def _half_rows(c, r):
    return pl.ds(pl.multiple_of(c * (r // 2), 8), r // 2)


HBM_SPEC = pl.BlockSpec(memory_space=pl.ANY)


STRICT_HBM_SPEC = pl.BlockSpec(memory_space=pltpu.HBM)
SEM_SPEC = pl.BlockSpec(memory_space=pltpu.SEMAPHORE)
EFFECT = pltpu.SideEffectType.DATAFLOW_SIDE_EFFECTING


def _peers(x, y, c):
    out = []
    for k in range(1, N_DEV):
        px, py, pc = (x + ((k >> 2) & 1)) % 2, (y + ((k >> 1) & 1)) % 2, (c + (k & 1)) % 2
        out.append(((px, py, pc), 2 * px + py, pc, 4 * px + 2 * py + pc))
    return out


def _token_spec():
    return pl.BlockSpec(memory_space=pltpu.VMEM)


def _hbm(a):
    return pltpu.with_memory_space_constraint(a, pltpu.HBM)


def _slab_half(ref, chip, core):
    return ref.at[2 * chip[0] + chip[1], _half_rows(core, ref.shape[1]), :]


def _allgather_start(slabs, name):
    n = len(slabs)

    def body(*refs):
        src, send, recv, token = refs[:n], refs[n], refs[n + 1], refs[-1]
        x, y, c, chips = _place()
        for a in range(n):
            own = _slab_half(src[a], (x, y), c)
            for j, chip in enumerate(chips):
                pltpu.make_async_remote_copy(src_ref=own, dst_ref=own, send_sem=send.at[3 * a + j], recv_sem=recv.at[3 * a + j],
                                             device_id=(*chip, c), device_id_type=MESH).start()
        token[...] = jnp.zeros_like(token)

    sems = pltpu.SemaphoreType.DMA((3 * n,))
    send, recv, *flying, token = pl.pallas_call(
        body, name=name, in_specs=[STRICT_HBM_SPEC] * n,
        out_shape=(sems, sems, *[pltpu.HBM(s.shape, s.dtype) for s in slabs], _sds((8, 128), F32)),
        out_specs=(SEM_SPEC, SEM_SPEC, *[STRICT_HBM_SPEC] * n, _token_spec()),
        input_output_aliases={a: a + 2 for a in range(n)},
        compiler_params=pltpu.CompilerParams(has_side_effects=EFFECT))(*[_hbm(s) for s in slabs])
    return send, recv, flying, token


def _allgather_relay(flying, send, recv, first, after, name):
    n = len(flying)

    def body(*refs):
        src, send_ref, recv_ref = refs[:n], refs[n], refs[n + 1]
        send2, recv2, token = refs[n + 3], refs[n + 4], refs[-1]
        token[...] = jnp.zeros_like(token)
        x, y, c, chips = _place()
        for a in range(n):
            for j, chip in enumerate(chips):
                cp = pltpu.make_async_remote_copy(
                    src_ref=_slab_half(src[a], (x, y), c), dst_ref=_slab_half(src[a], chip, c),
                    send_sem=send_ref.at[3 * (first + a) + j], recv_sem=recv_ref.at[3 * (first + a) + j],
                    device_id=(*chip, c), device_id_type=MESH)
                cp.wait_send()
                cp.wait_recv()
        for a in range(n):
            for j, chip in enumerate(chips):
                landed = _slab_half(src[a], chip, c)
                pltpu.make_async_remote_copy(src_ref=landed, dst_ref=landed, send_sem=send2.at[3 * a + j],
                                             recv_sem=recv2.at[3 * a + j], device_id=(x, y, 1 - c),
                                             device_id_type=MESH).start()

    sems = pltpu.SemaphoreType.DMA((3 * n,))
    send2, recv2, *relayed, token = pl.pallas_call(
        body, name=name, in_specs=[STRICT_HBM_SPEC] * n + [SEM_SPEC, SEM_SPEC, HBM_SPEC],
        out_shape=(sems, sems, *[pltpu.HBM(s.shape, s.dtype) for s in flying], _sds((8, 128), F32)),
        out_specs=(SEM_SPEC, SEM_SPEC, *[STRICT_HBM_SPEC] * n, _token_spec()),
        input_output_aliases={a: a + 2 for a in range(n)},
        compiler_params=pltpu.CompilerParams(has_side_effects=EFFECT))(*flying, send, recv, after)
    return send2, recv2, relayed, token


def _allgather_wait(relayed, send2, recv2, after, name):
    n = len(relayed)

    def body(*refs):
        src, send_ref, recv_ref = refs[:n], refs[n], refs[n + 1]
        x, y, c, chips = _place()
        for a in range(n):
            for j, chip in enumerate(chips):
                cp = pltpu.make_async_remote_copy(
                    src_ref=_slab_half(src[a], chip, c), dst_ref=_slab_half(src[a], chip, 1 - c),
                    send_sem=send_ref.at[3 * a + j], recv_sem=recv_ref.at[3 * a + j],
                    device_id=(x, y, 1 - c), device_id_type=MESH)
                cp.wait_send()
                cp.wait_recv()

    return pl.pallas_call(
        body, name=name, in_specs=[STRICT_HBM_SPEC] * n + [SEM_SPEC, SEM_SPEC, HBM_SPEC],
        out_shape=tuple(pltpu.HBM(s.shape, s.dtype) for s in relayed), out_specs=tuple([STRICT_HBM_SPEC] * n),
        input_output_aliases={a: a for a in range(n)},
        compiler_params=pltpu.CompilerParams(has_side_effects=EFFECT))(*relayed, send2, recv2, after)


def _allgather_small(small, name):
    def body(sm, osm, send, recv, local):
        x, y, c, chips = _place()
        own = pltpu.make_async_copy(sm, osm.at[2 * x + y], local)
        own.start()
        cps = [pltpu.make_async_remote_copy(src_ref=sm, dst_ref=osm.at[2 * x + y], send_sem=send.at[j],
                                            recv_sem=recv.at[j], device_id=(*chip, c), device_id_type=MESH)
               for j, chip in enumerate(chips)]
        for cp in cps:
            cp.start()
        for j, chip in enumerate(chips):
            got = osm.at[2 * chip[0] + chip[1]]
            pltpu.make_async_remote_copy(src_ref=got, dst_ref=got, send_sem=send.at[j], recv_sem=recv.at[j],
                                         device_id=(x, y, c), device_id_type=MESH).wait_recv()
        for cp in cps:
            cp.wait_send()
        own.wait()

    return pl.pallas_call(
        body, name=name, in_specs=[pl.BlockSpec(memory_space=pltpu.VMEM)], out_specs=HBM_SPEC,
        out_shape=_sds((N_CHIPS, *small.shape), small.dtype),
        scratch_shapes=[pltpu.SemaphoreType.DMA((3,)), pltpu.SemaphoreType.DMA((3,)), pltpu.SemaphoreType.DMA])(small)


def _reduce_start(grads, name):
    n = len(grads)

    def body(*refs):
        src, land, send, recv, token = refs[:n], refs[n:2 * n], refs[2 * n], refs[2 * n + 1], refs[-1]
        x, y, c, _ = _place()
        me = 4 * x + 2 * y + c
        for a in range(n):
            for k, (peer, p_chip, p_core, _) in enumerate(_peers(x, y, c)):
                pltpu.make_async_remote_copy(
                    src_ref=src[a].at[p_chip, _half_rows(p_core, src[a].shape[1]), :], dst_ref=land[a].at[me],
                    send_sem=send.at[(N_DEV - 1) * a + k], recv_sem=recv.at[(N_DEV - 1) * a + k],
                    device_id=peer, device_id_type=MESH).start()
        token[...] = jnp.zeros_like(token)

    lands = [lax.empty((N_DEV, g.shape[1] // 2, g.shape[2]), BF16) for g in grads]
    sems = pltpu.SemaphoreType.DMA(((N_DEV - 1) * n,))
    shapes = [pltpu.HBM(a.shape, a.dtype) for a in grads + lands]
    send, recv, *flying, token = pl.pallas_call(
        body, name=name, in_specs=[STRICT_HBM_SPEC] * (2 * n),
        out_shape=(sems, sems, *shapes, _sds((8, 128), F32)),
        out_specs=(SEM_SPEC, SEM_SPEC, *[STRICT_HBM_SPEC] * (2 * n), _token_spec()),
        input_output_aliases={a: a + 2 for a in range(2 * n)},
        compiler_params=pltpu.CompilerParams(has_side_effects=EFFECT))(*[_hbm(a) for a in grads + lands])
    return send, recv, flying[:n], flying[n:], token


def _reduce_wait(started, after, name):
    sizes = [len(grads) for _, _, grads, _ in started]
    n_arr = 2 * sum(sizes)

    def body(*refs):
        x, y, c, _ = _place()
        at = 0
        for s, n in enumerate(sizes):
            src, land = refs[at:at + n], refs[at + n:at + 2 * n]
            send_ref, recv_ref = refs[n_arr + 2 * s], refs[n_arr + 2 * s + 1]
            at += 2 * n
            for a in range(n):
                for k, (peer, p_chip, p_core, p_dev) in enumerate(_peers(x, y, c)):
                    cp = pltpu.make_async_remote_copy(
                        src_ref=src[a].at[p_chip, _half_rows(p_core, src[a].shape[1]), :], dst_ref=land[a].at[p_dev],
                        send_sem=send_ref.at[(N_DEV - 1) * a + k], recv_sem=recv_ref.at[(N_DEV - 1) * a + k],
                        device_id=peer, device_id_type=MESH)
                    cp.wait_send()
                    cp.wait_recv()

    arrays, sems = [], []
    for send, recv, grads, lands in started:
        arrays += list(grads) + list(lands)
        sems += [send, recv]
    out = pl.pallas_call(
        body, name=name, in_specs=[STRICT_HBM_SPEC] * n_arr + [SEM_SPEC] * len(sems) + [HBM_SPEC],
        out_shape=tuple(pltpu.HBM(a.shape, a.dtype) for a in arrays), out_specs=tuple([STRICT_HBM_SPEC] * n_arr),
        input_output_aliases={a: a for a in range(n_arr)},
        compiler_params=pltpu.CompilerParams(has_side_effects=EFFECT))(*arrays, *sems, after)
    result, at = [], 0
    for n in sizes:
        result.append((out[at:at + n], out[at + n:at + 2 * n]))
        at += 2 * n
    return result


def _reduce_sum(grad, land, place, name, deps=()):
    _, r2, C = land.shape
    tr = _row_tile(r2, C, 4, 2 * 1024 * 1024, 16)
    nb = r2 // tr

    def body(place_ref, own_ref, *rest):
        del place_ref
        acc = own_ref[...].astype(F32)
        for ref in rest[:N_DEV - 1]:
            acc = acc + ref[...].astype(F32)
        rest[-1][...] = acc

    def from_dev(k):
        return pl.BlockSpec((None, tr, C), lambda i, place_ref: ((place_ref[2] + k) % N_DEV, i, 0))

    grid_spec = pltpu.PrefetchScalarGridSpec(
        num_scalar_prefetch=1, grid=(nb,),
        in_specs=[pl.BlockSpec((None, tr, C), lambda i, place_ref: (place_ref[0], place_ref[1] * nb + i, 0))]
        + [from_dev(k) for k in range(1, N_DEV)] + [HBM_SPEC] * len(deps),
        out_specs=pl.BlockSpec((tr, C), lambda i, place_ref: (place_ref[1] * nb + i, 0)))
    return pl.pallas_call(body, name=name, grid_spec=grid_spec, out_shape=_sds((2 * r2, C), F32),
                          compiler_params=_cparams(1))(place, grad, *[land] * (N_DEV - 1), *deps)


def _join_start(halves, name):
    n = len(halves)

    def body(*refs):
        src, send, recv, token = refs[:n], refs[n], refs[n + 1], refs[-1]
        x, y, c, _ = _place()
        for w in range(n):
            mine = src[w].at[_half_rows(c, src[w].shape[0]), :]
            pltpu.make_async_remote_copy(src_ref=mine, dst_ref=mine, send_sem=send.at[w], recv_sem=recv.at[w],
                                         device_id=(x, y, 1 - c), device_id_type=MESH).start()
        token[...] = jnp.zeros_like(token)

    sems = pltpu.SemaphoreType.DMA((n,))
    send, recv, *flying, token = pl.pallas_call(
        body, name=name, in_specs=[STRICT_HBM_SPEC] * n,
        out_shape=(sems, sems, *[pltpu.HBM(a.shape, a.dtype) for a in halves], _sds((8, 128), F32)),
        out_specs=(SEM_SPEC, SEM_SPEC, *[STRICT_HBM_SPEC] * n, _token_spec()),
        input_output_aliases={w: w + 2 for w in range(n)},
        compiler_params=pltpu.CompilerParams(has_side_effects=EFFECT))(*[_hbm(a) for a in halves])
    return send, recv, flying, token


def _join_wait(flying, send, recv, after, name):
    n = len(flying)

    def body(*refs):
        src, send_ref, recv_ref = refs[:n], refs[n], refs[n + 1]
        x, y, c, _ = _place()
        for w in range(n):
            cp = pltpu.make_async_remote_copy(
                src_ref=src[w].at[_half_rows(c, src[w].shape[0]), :],
                dst_ref=src[w].at[_half_rows(1 - c, src[w].shape[0]), :], send_sem=send_ref.at[w],
                recv_sem=recv_ref.at[w], device_id=(x, y, 1 - c),
                device_id_type=MESH)
            cp.wait_send()
            cp.wait_recv()

    return pl.pallas_call(
        body, name=name, in_specs=[STRICT_HBM_SPEC] * n + [SEM_SPEC, SEM_SPEC, HBM_SPEC],
        out_shape=tuple(pltpu.HBM(a.shape, a.dtype) for a in flying), out_specs=tuple([STRICT_HBM_SPEC] * n),
        input_output_aliases={w: w for w in range(n)},
        compiler_params=pltpu.CompilerParams(has_side_effects=EFFECT))(*flying, send, recv, after)


def _gather_small(packed, name):
    def body(p_ref, out, send, recv, local):
        x, y, c, _ = _place()
        me = 4 * x + 2 * y + c
        own = pltpu.make_async_copy(p_ref, out.at[me], local)
        own.start()
        cps = []
        for k in range(1, N_DEV):
            fx, fy, fc = (k >> 2) & 1, (k >> 1) & 1, k & 1
            peer = ((x + fx) % 2, (y + fy) % 2, (c + fc) % 2)
            cps.append(pltpu.make_async_remote_copy(src_ref=p_ref, dst_ref=out.at[me], send_sem=send.at[k - 1],
                                                    recv_sem=recv.at[k - 1], device_id=peer, device_id_type=MESH))
        for cp in cps:
            cp.start()
        for k in range(1, N_DEV):
            fx, fy, fc = (k >> 2) & 1, (k >> 1) & 1, k & 1
            src = out.at[4 * ((x + fx) % 2) + 2 * ((y + fy) % 2) + (c + fc) % 2]
            pltpu.make_async_remote_copy(src_ref=src, dst_ref=src, send_sem=send.at[k - 1], recv_sem=recv.at[k - 1],
                                         device_id=(x, y, c), device_id_type=MESH).wait_recv()
        for cp in cps:
            cp.wait_send()
        own.wait()

    return pl.pallas_call(
        body, name=name, in_specs=[pl.BlockSpec(memory_space=pltpu.VMEM)], out_specs=HBM_SPEC,
        out_shape=_sds((N_DEV, *packed.shape), F32),
        scratch_shapes=[pltpu.SemaphoreType.DMA((N_DEV - 1,)), pltpu.SemaphoreType.DMA((N_DEV - 1,)),
                        pltpu.SemaphoreType.DMA])(packed)


def _sum_devices(gathered, name):
    _, R, C = gathered.shape

    def body(g_ref, o_ref):
        acc = g_ref[0]
        for d in range(1, N_DEV):
            acc = acc + g_ref[d]
        o_ref[...] = acc

    tr = 8
    return pl.pallas_call(
        body, name=name, grid=(R // tr,), in_specs=[pl.BlockSpec((N_DEV, tr, C), lambda i: (0, i, 0))],
        out_specs=pl.BlockSpec((tr, C), lambda i: (i, 0)), out_shape=_sds((R, C), F32),
        compiler_params=_cparams(1))(gathered)


def _gather_start(packed, name):
    def body(src, land, send, recv, *rest):
        x, y, c, _ = _place()
        for k, (peer, _, _, _) in enumerate(_peers(x, y, c)):
            pltpu.make_async_remote_copy(src_ref=src, dst_ref=land.at[4 * x + 2 * y + c], send_sem=send.at[k],
                                         recv_sem=recv.at[k], device_id=peer, device_id_type=MESH).start()
        rest[-1][...] = jnp.zeros_like(rest[-1])

    land = lax.empty((N_DEV, *packed.shape), F32)
    sems = pltpu.SemaphoreType.DMA((N_DEV - 1,))
    return pl.pallas_call(
        body, name=name, in_specs=[STRICT_HBM_SPEC] * 2,
        out_shape=(sems, sems, pltpu.HBM(packed.shape, F32), pltpu.HBM(land.shape, F32), _sds((8, 128), F32)),
        out_specs=(SEM_SPEC, SEM_SPEC, STRICT_HBM_SPEC, STRICT_HBM_SPEC, _token_spec()),
        input_output_aliases={0: 2, 1: 3},
        compiler_params=pltpu.CompilerParams(has_side_effects=EFFECT))(_hbm(packed), _hbm(land))


def _gather_wait(started, after, name):
    n = len(started)

    def body(*refs):
        x, y, c, _ = _place()
        for s in range(n):
            src, land, send, recv = refs[2 * s], refs[2 * s + 1], refs[2 * n + 2 * s], refs[2 * n + 2 * s + 1]
            for k, (peer, _, _, p_dev) in enumerate(_peers(x, y, c)):
                cp = pltpu.make_async_remote_copy(src_ref=src, dst_ref=land.at[p_dev], send_sem=send.at[k],
                                                  recv_sem=recv.at[k], device_id=peer,
                                                  device_id_type=MESH)
                cp.wait_send()
                cp.wait_recv()

    arrays = [a for _, _, packed, land in started for a in (packed, land)]
    sems = [s for send, recv, _, _ in started for s in (send, recv)]
    out = pl.pallas_call(
        body, name=name, in_specs=[STRICT_HBM_SPEC] * (2 * n) + [SEM_SPEC] * (2 * n) + [HBM_SPEC],
        out_shape=tuple(pltpu.HBM(a.shape, a.dtype) for a in arrays), out_specs=tuple([STRICT_HBM_SPEC] * (2 * n)),
        input_output_aliases={a: a for a in range(2 * n)},
        compiler_params=pltpu.CompilerParams(has_side_effects=EFFECT))(*arrays, *sems, after)
    return [(out[2 * s], out[2 * s + 1]) for s in range(n)]


def _sum_gathered(packed, land, device, name):
    R, C = packed.shape
    tr = 8

    def body(dev_ref, own_ref, *rest):
        me = dev_ref[0]
        acc = None
        for d in range(N_DEV):
            term = jnp.where(me == d, own_ref[...], rest[d][...])
            acc = term if acc is None else acc + term
        rest[-1][...] = acc

    def slab(d):
        return pl.BlockSpec((None, tr, C), lambda i, dev_ref: (jnp.where(dev_ref[0] == d, (d + 1) % N_DEV, d), i, 0))

    grid_spec = pltpu.PrefetchScalarGridSpec(
        num_scalar_prefetch=1, grid=(R // tr,),
        in_specs=[pl.BlockSpec((tr, C), lambda i, dev_ref: (i, 0))] + [slab(d) for d in range(N_DEV)],
        out_specs=pl.BlockSpec((tr, C), lambda i, dev_ref: (i, 0)))
    return pl.pallas_call(body, name=name, grid_spec=grid_spec, out_shape=_sds((R, C), F32),
                          compiler_params=_cparams(1))(device, packed, *[land] * N_DEV)


def _pack_small(arrays):
    rows = []
    for a in arrays:
        flat = a.reshape(-1)
        pad = (-flat.shape[0]) % SMALL_COLS
        rows.append(jnp.pad(flat, (0, pad)).reshape(-1, SMALL_COLS))
    packed = jnp.concatenate(rows, axis=0)
    return jnp.pad(packed, ((0, (-packed.shape[0]) % 8), (0, 0)))


def _unpack_small(packed, shapes):
    out, row = [], 0
    for shape in shapes:
        size = math.prod(shape)
        n_rows = -(-size // SMALL_COLS)
        out.append(packed[row:row + n_rows].reshape(-1)[:size].reshape(shape))
        row += n_rows
    return out


def kernel(x, a_norm, a_w_in, a_sgu_norm, a_w_spatial, a_b_spatial, a_w_out, kv_norm, w_kv, b_norm, b_w_q, b_rel_bias, b_w_o, ffn_norm, ffn_w_gate_up, ffn_w_down, final_norm, loss_target, m_a_norm, m_a_w_in, m_a_sgu_norm, m_a_w_spatial, m_a_b_spatial, m_a_w_out, m_kv_norm, m_w_kv, m_b_norm, m_b_w_q, m_b_rel_bias, m_b_w_o, m_ffn_norm, m_ffn_w_gate_up, m_ffn_w_down, m_final_norm, v_a_norm, v_a_w_in, v_a_sgu_norm, v_a_w_spatial, v_a_b_spatial, v_a_w_out, v_kv_norm, v_w_kv, v_b_norm, v_b_w_q, v_b_rel_bias, v_b_w_o, v_ffn_norm, v_ffn_w_gate_up, v_ffn_w_down, v_final_norm):
    S, D = x.shape[1], x.shape[2]
    n_a = a_w_in.shape[0]
    n_b = b_w_q.shape[0]
    depth = ffn_w_gate_up.shape[0]
    xi, yi, ci = lax.axis_index("x"), lax.axis_index("y"), lax.axis_index("c")
    chip = 2 * xi + yi

    place = jnp.stack([chip, ci, 2 * chip + ci]).astype(jnp.int32)
    stacked = {"a_w_in": a_w_in, "a_w_out": a_w_out, "w_kv": w_kv[None], "b_w_q": b_w_q, "b_w_o": b_w_o,
               "ffn_w_gate_up": ffn_w_gate_up, "ffn_w_down": ffn_w_down}
    groups = []
    for layer in range(depth):
        if layer == 0 and n_a > 0:
            groups += [[("a_w_in", 0)], [("a_w_out", 0)]]
        elif layer < n_a:
            groups.append([("a_w_in", layer), ("a_w_out", layer)])
        elif layer == n_a:
            groups.append([("w_kv", 0), ("b_w_q", 0), ("b_w_o", 0)])
        else:
            groups.append([("b_w_q", layer - n_a), ("b_w_o", layer - n_a)])
        groups.append([("ffn_w_gate_up", layer), ("ffn_w_down", layer)])
    units = [u for group in groups for u in group]
    n_early = len(groups[0])
    slabs = [_cast_slab(stacked[k], l, place[:1], f"cast_{k}_{l}") for k, l in units[:n_early]]
    early = _allgather_start(slabs, "allgather_start_first")
    slabs = [_cast_slab(stacked[k], l, place[:1], f"cast_{k}_{l}", deps=(early[3],)) for k, l in units[n_early:]]
    late = _allgather_start(slabs, "allgather_start_rest")
    na_w, ns_w = a_norm.shape[1], a_sgu_norm.shape[1]
    small_g = _allgather_small(jnp.concatenate([a_norm, a_sgu_norm], axis=1), "allgather_small")
    a_norm_f = small_g[:, :, :na_w].transpose(1, 0, 2).reshape(n_a, N_CHIPS * na_w)
    a_sgu_f = small_g[:, :, na_w:].transpose(1, 0, 2).reshape(n_a, N_CHIPS * ns_w)
    W, relayed = {}, []

    def relay(after):
        if len(relayed) == len(groups):
            return ()
        index = sum(len(g) for g in groups[:len(relayed)])
        group = groups[len(relayed)]
        (send, recv, flying, _), first = (early, index) if index < n_early else (late, index - n_early)
        relayed.append(_allgather_relay(flying[first:first + len(group)], send, recv, first, after,
                                        f"allgather_relay_{len(relayed)}"))
        return (relayed[-1][3],)

    n_gathered = [0]

    def gathered(after):
        index = n_gathered[0]
        send2, recv2, arrays, _ = relayed[index]
        W.update(zip(groups[index], _allgather_wait(arrays, send2, recv2, after, f"allgather_wait_{index}")))
        n_gathered[0] += 1

    xc = x.reshape(S, D)
    saved = []
    kvp = x_kv = h_kv = None
    relay(late[3])
    order = ()
    for layer in range(depth):
        rec = {"x_in": xc}
        gathered(xc)
        if layer < n_a:
            i = layer
            rec["zpre"], rec["h"] = _norm_matmul(xc, a_norm_f[i][None], W["a_w_in", i], BF16, f"a{i}_in", deps=order)
            order = relay(rec["h"])
            rec["uv"] = _sgu_fwd(rec["zpre"], a_sgu_f[i][None], a_w_spatial[i], a_b_spatial[i].T, f"a{i}_sgu",
                                 deps=order)
            order = ()
            if ("a_w_out", i) not in W:
                gathered(rec["uv"])
                order = relay(rec["uv"])
            xm = _matmul_res(rec["uv"], W["a_w_out", i], xc, f"a{i}_out", tm=512, deps=order)
        else:
            i = layer - n_a
            if i == 0:
                kvp, h_kv = _norm_matmul(xc, kv_norm[None], W["w_kv", 0], BF16, "kv_proj", tm=KV_PAD, zero_rows=KV_PAD)
                x_kv = xc
            rec["q"], rec["h"] = _norm_matmul(xc, b_norm[i][None], W["b_w_q", i], BF16, f"b{i}_q", row_sharded=True)
            order = relay(rec["h"])
            table = jnp.pad(b_rel_bias[i], ((0, 0), (0, TABLE_PAD - b_rel_bias.shape[2])))
            rec["bias"] = _rel_bias_fwd(table, f"b{i}_bias")
            rec["o"] = _attn_fwd(rec["q"], kvp, rec["bias"], f"b{i}_attn", deps=order)
            xm = _matmul_res(rec["o"], W["b_w_o", i], xc, f"b{i}_o", tm=512)
        rec["x_mid"] = xm
        gathered(xm)
        rec["gu"], rec["h_f"] = _norm_matmul(xm, ffn_norm[layer][None], W["ffn_w_gate_up", layer], BF16, f"f{layer}_in")
        order = relay(rec["h_f"])
        xc = _matmul_res(rec["gu"], W["ffn_w_down", layer], xm, f"f{layer}_out", swiglu=True, deps=order, tm=512)
        order = ()
        saved.append(rec)

    loss_tile, dx, dxb, d_final = _loss_head(xc, final_norm[None], loss_target.reshape(S, D), "loss_head")

    started = []
    small_started = []
    pending = []

    held = []

    def weight_grad(unit, hold=False, **kw):
        full = (N_CHIPS,) + tuple(stacked[unit[0]].shape[1:])
        g = _matmul_tn(out_shape=full, name=f"d_{unit[0]}_{unit[1]}", deps=tuple(pending), **kw)
        pending.clear()
        held.append((unit, g))
        if hold:
            return ()
        send, recv, flying_g, flying_land, token = _reduce_start([g for _, g in held],
                                                                 f"reduce_start_{unit[0]}_{unit[1]}")
        started.append(([u for u, _ in held], send, recv, flying_g, flying_land))
        held.clear()
        return (token,)

    tt = min(1024, S)
    tw = min(2048, S)
    ts = S
    tb = min(NORMBWD_ROWS, S)
    row_a = lambda w, rows=tw: pl.BlockSpec((rows, w), lambda o, t: (t, 0))
    d_ffn_norm, d_b_norm, d_a_norm, d_a_sgu = [None] * depth, [None] * n_b, [None] * n_a, [None] * n_a
    d_ws, d_bs, d_rel = [None] * n_a, [None] * n_a, [None] * n_b
    dkv = None
    first = lambda ref: ref[...]
    for layer in reversed(range(depth)):
        rec = saved[layer]
        r_d = ffn_w_down.shape[1]
        half_f = 2 * r_d
        token = weight_grad(
            ("ffn_w_down", layer), a_ops=[rec["gu"], rec["gu"]],
            a_specs=[pl.BlockSpec((tt, half_f), lambda o, t: (t, o)), pl.BlockSpec((tt, half_f), lambda o, t: (t, 2 + o))],
            a_fn=lambda g_ref, u_ref: _swiglu(g_ref[...].astype(F32), u_ref[...].astype(F32)).astype(BF16),
            b_op=dxb, b_spec=row_a(D, tt), out_spec=pl.BlockSpec((2, r_d, D), lambda o, t: (o, 0, 0)),
            acc_shape=(half_f, D), n_outer=2, tt=tt, hold=True)
        dgu = _nt_swiglu_bwd(dxb, W["ffn_w_down", layer], rec["gu"], f"f{layer}_dgu", deps=token)
        nsh = ffn_w_gate_up.shape[2]
        token = weight_grad(
            ("ffn_w_gate_up", layer), a_ops=[rec["h_f"]], a_specs=[row_a(D)], a_fn=first,
            b_op=dgu, b_spec=pl.BlockSpec((None, tw, nsh), lambda o, t: (o // 2, t, o % 2)),
            out_spec=pl.BlockSpec((None, D, nsh), lambda o, t: (o, 0, 0)), acc_shape=(D, nsh), n_outer=N_CHIPS, tt=tw)
        dx, dxb, d_ffn_norm[layer] = _nt_normbwd(
            dgu, pl.BlockSpec((None, tb, nsh), lambda i, k: (k // 2, i, k % 2)),
            W["ffn_w_gate_up", layer], pl.BlockSpec((None, D, nsh), lambda i, k: (k, 0, 0)),
            (D, nsh), N_CHIPS, rec["x_mid"], ffn_norm[layer][None], dx, f"f{layer}_dx", deps=token)
        if layer >= n_a:
            i = layer - n_a
            r_o = b_w_o.shape[1]
            token = weight_grad(
                ("b_w_o", i), a_ops=[rec["o"]], a_specs=[row_a(D, ts)], a_fn=first, b_op=dxb, b_spec=row_a(D, ts),
                out_spec=pl.BlockSpec((N_CHIPS, r_o, D), lambda o, t: (0, 0, 0)), acc_shape=(D, D), n_outer=1, tt=ts,
                hold=True)
            do = _nt_rows(dxb, W["b_w_o", i], N_CHIPS, BF16, f"b{i}_do", deps=token)
            dq, dkv, dbias, *dkv_bf16 = _attn_bwd(rec["q"], kvp, rec["bias"], do, dkv, f"b{i}_attn_bwd")
            d_rel[i] = _rel_bias_bwd(dbias, f"b{i}_dbias")[:, :b_rel_bias.shape[2]]
            token = weight_grad(
                ("b_w_q", i), a_ops=[rec["h"]], a_specs=[row_a(D, ts)], a_fn=first, b_op=dq, b_spec=row_a(D, ts),
                out_spec=pl.BlockSpec((N_CHIPS, r_o, D), lambda o, t: (0, 0, 0)), acc_shape=(D, D), n_outer=1, tt=ts)
            dx, dxb, d_b_norm[i] = _nt_normbwd(
                dq, pl.BlockSpec((tb, D), lambda i_, k: (i_, 0)),
                W["b_w_q", i], pl.BlockSpec((N_CHIPS, r_o, D), lambda i_, k: (0, 0, 0)),
                (D, D), 1, rec["x_in"], b_norm[i][None], dx, f"b{i}_dx", deps=token)
            if i == 0:
                dkv_b = dkv_bf16[0] if dkv_bf16 else dkv[:, KV_PAD:, :].astype(BF16)
                n_kv = w_kv.shape[1]
                token = weight_grad(
                    ("w_kv", 0), a_ops=[h_kv], a_specs=[row_a(D, ts)], a_fn=first,
                    b_op=dkv_b, b_spec=pl.BlockSpec((None, ts, n_kv), lambda o, t: (o // 2, t, o % 2)),
                    out_spec=pl.BlockSpec((None, D, n_kv), lambda o, t: (o, 0, 0)), acc_shape=(D, n_kv),
                    n_outer=N_CHIPS, tt=ts)
                dx, dxb, d_kv_norm = _nt_normbwd(
                    dkv_b, pl.BlockSpec((None, tb, n_kv), lambda i_, k: (k // 2, i_, k % 2)),
                    W["w_kv", 0], pl.BlockSpec((None, D, n_kv), lambda i_, k: (k, 0, 0)),
                    (D, n_kv), N_CHIPS, x_kv, kv_norm[None], dx, "kv_dx", deps=token)
        else:
            i = layer
            r_w = a_w_out.shape[1]
            token = weight_grad(
                ("a_w_out", i), a_ops=[rec["uv"]], a_specs=[row_a(N_CHIPS * r_w, tw // 2)], a_fn=first,
                b_op=dxb, b_spec=row_a(D, tw // 2), out_spec=pl.BlockSpec((N_CHIPS, r_w, D), lambda o, t: (0, 0, 0)),
                acc_shape=(N_CHIPS * r_w, D), n_outer=1, tt=tw // 2, hold=i > 0)
            duv = _nt_rows(dxb, W["a_w_out", i], 2, BF16, f"a{i}_duv", deps=token)
            dz, d_a_sgu[i], d_ws[i], dbs = _sgu_bwd(rec["zpre"], duv, a_sgu_f[i][None], a_w_spatial[i],
                                                  a_w_spatial[i].transpose(0, 2, 1), a_b_spatial[i].T, f"a{i}_sgu_bwd")
            d_bs[i] = dbs[:, :, 0]
            if i == 0:
                batch = [d_a_sgu[0], d_ws[0][None], d_bs[0][None], d_ffn_norm[0]]
                small_started.append((batch, _gather_start(_pack_small(batch), "gather_start_late")))
                pending.append(small_started[-1][1][4])
            n_in = a_w_in.shape[2]
            token = weight_grad(
                ("a_w_in", i), a_ops=[rec["h"]], a_specs=[row_a(D, ts)], a_fn=first,
                b_op=dz, b_spec=pl.BlockSpec((ts, n_in), lambda o, t: (t, o)),
                out_spec=pl.BlockSpec((None, D, n_in), lambda o, t: (o, 0, 0)), acc_shape=(D, n_in), n_outer=N_CHIPS,
                tt=ts)
            dx, dxb, d_a_norm[i] = _nt_normbwd(
                dz, pl.BlockSpec((tb, n_in), lambda i_, k: (i_, k)),
                W["a_w_in", i], pl.BlockSpec((None, D, n_in), lambda i_, k: (k, 0, 0)),
                (D, n_in), N_CHIPS, rec["x_in"], a_norm_f[i][None], dx, f"a{i}_dx", deps=token)
        if layer == 1:
            batch = [jnp.concatenate(d_a_norm[1:], axis=0), jnp.concatenate(d_a_sgu[1:], axis=0), jnp.stack(d_ws[1:]),
                     jnp.stack(d_bs[1:]), d_kv_norm, jnp.concatenate(d_b_norm, axis=0), jnp.stack(d_rel),
                     jnp.concatenate(d_ffn_norm[1:], axis=0), d_final, loss_tile[:1, :1]]
            small_started.append((batch, _gather_start(_pack_small(batch), "gather_start_early")))
            pending.append(small_started[-1][1][4])
    grad_x = dx.reshape(x.shape)

    landed = _reduce_wait([(send, recv, g, land) for _, send, recv, g, land in started], dx, "reduce_wait")
    reduced_units = [unit for units_, *_ in started for unit in units_]
    arrived = dict(zip(reduced_units, [pair for gs, lands in landed for pair in zip(gs, lands)]))
    ffn_units = [u for u in reduced_units if u[0].startswith("ffn_")]
    other_units = [u for u in reduced_units if not u[0].startswith("ffn_")]
    halves = [_reduce_sum(*arrived[u], place, f"reduce_sum_{u[0]}_{u[1]}") for u in ffn_units]
    ffn_send, ffn_recv, ffn_flying, ffn_token = _join_start(halves, "join_start_ffn")
    halves = [_reduce_sum(*arrived[u], place, f"reduce_sum_{u[0]}_{u[1]}", deps=(ffn_token,)) for u in other_units]
    other_send, other_recv, other_flying, other_token = _join_start(halves, "join_start_rest")
    joined = dict(zip(ffn_units, _join_wait(ffn_flying, ffn_send, ffn_recv, other_token, "join_wait_ffn")))
    reduced = {}

    (packed_e, land_e), (packed_l, land_l) = _gather_wait([s[:4] for _, s in small_started], dx, "gather_wait")
    total_e = _sum_gathered(packed_e, land_e, place[2:], "sum_small_grads_early")
    total_l = _sum_gathered(packed_l, land_l, place[2:], "sum_small_grads_late")
    total_t = _sum_devices(_gather_small(_pack_small([d_a_norm[0]]), "gather_small_grads_last"), "sum_small_grads_last")
    (e_a_norm, e_a_sgu, e_ws, e_bs, g_kv_norm, g_b_norm, g_rel, e_ffn_norm, g_final, loss) = _unpack_small(
        total_e, [a.shape for a in small_started[0][0]])
    l_a_sgu, l_ws, l_bs, l_ffn_norm = _unpack_small(total_l, [a.shape for a in small_started[1][0]])
    (t_a_norm,) = _unpack_small(total_t, [d_a_norm[0].shape])
    g_a_norm = jnp.concatenate([t_a_norm, e_a_norm], axis=0)
    g_a_sgu = jnp.concatenate([l_a_sgu, e_a_sgu], axis=0)
    g_ws = jnp.concatenate([l_ws, e_ws], axis=0)
    g_bs = jnp.concatenate([l_bs, e_bs], axis=0)
    g_ffn_norm = jnp.concatenate([l_ffn_norm, e_ffn_norm], axis=0)
    reduced["a_norm"] = lax.dynamic_slice_in_dim(g_a_norm, chip * na_w, na_w, axis=1)
    reduced["a_sgu_norm"] = lax.dynamic_slice_in_dim(g_a_sgu, chip * ns_w, ns_w, axis=1)
    reduced.update(a_w_spatial=g_ws, a_b_spatial=g_bs, kv_norm=g_kv_norm.reshape(kv_norm.shape), b_norm=g_b_norm,
                   b_rel_bias=g_rel, ffn_norm=g_ffn_norm, final_norm=g_final.reshape(final_norm.shape))

    weights = dict(a_norm=a_norm, a_w_in=a_w_in, a_sgu_norm=a_sgu_norm, a_w_spatial=a_w_spatial,
                   a_b_spatial=a_b_spatial, a_w_out=a_w_out, kv_norm=kv_norm, w_kv=w_kv, b_norm=b_norm, b_w_q=b_w_q,
                   b_rel_bias=b_rel_bias, b_w_o=b_w_o, ffn_norm=ffn_norm, ffn_w_gate_up=ffn_w_gate_up,
                   ffn_w_down=ffn_w_down, final_norm=final_norm)
    m_in = dict(a_norm=m_a_norm, a_w_in=m_a_w_in, a_sgu_norm=m_a_sgu_norm, a_w_spatial=m_a_w_spatial,
                a_b_spatial=m_a_b_spatial, a_w_out=m_a_w_out, kv_norm=m_kv_norm, w_kv=m_w_kv, b_norm=m_b_norm,
                b_w_q=m_b_w_q, b_rel_bias=m_b_rel_bias, b_w_o=m_b_w_o, ffn_norm=m_ffn_norm,
                ffn_w_gate_up=m_ffn_w_gate_up, ffn_w_down=m_ffn_w_down, final_norm=m_final_norm)
    v_in = dict(a_norm=v_a_norm, a_w_in=v_a_w_in, a_sgu_norm=v_a_sgu_norm, a_w_spatial=v_a_w_spatial,
                a_b_spatial=v_a_b_spatial, a_w_out=v_a_w_out, kv_norm=v_kv_norm, w_kv=v_w_kv, b_norm=v_b_norm,
                b_w_q=v_b_w_q, b_rel_bias=v_b_rel_bias, b_w_o=v_b_w_o, ffn_norm=v_ffn_norm,
                ffn_w_gate_up=v_ffn_w_gate_up, ffn_w_down=v_ffn_w_down, final_norm=v_final_norm)
    results = {}

    def adamw_large(key):
        as_layers = lambda a: a.reshape(stacked[key].shape)
        results[key] = _adamw_stacked(
            as_layers(weights[key]), [joined[key, layer] for layer in range(stacked[key].shape[0])],
            as_layers(m_in[key]), as_layers(v_in[key]), "adamw_" + key)

    for key in ("ffn_w_gate_up", "ffn_w_down"):
        adamw_large(key)
    joined.update(zip(other_units, _join_wait(other_flying, other_send, other_recv, results["ffn_w_down"][1],
                                              "join_wait_rest")))
    for key, w in weights.items():
        if key in stacked:
            if key not in results:
                adamw_large(key)
        else:
            g = reduced[key].reshape(w.shape)
            view = (1, w.shape[0]) if w.ndim == 1 else (-1, w.shape[-1])
            d, nm, nv = _adamw(w.reshape(view), g.reshape(view), m_in[key].reshape(view), v_in[key].reshape(view),
                               "adamw_" + key)
            results[key] = (g, d, nm, nv)
    outs = [[results[key][k].reshape(w.shape) for key, w in weights.items()] for k in range(4)]
    return (loss.reshape(()), grad_x, *outs[0], *outs[1], *outs[2], *outs[3])
```
